```python
import math
import jax, jax.numpy as jnp
from jax import lax
import numpy as np

D_MODEL = 1024
BATCH = 8
SEQ = 8192
DEPTH = 2

N_MIXERS = 2
N_POOL_LAYERS = (DEPTH + 1) // 2
N_GDN_LAYERS = DEPTH // 2
PLE_DIM = 256
D_FF = 4 * D_MODEL
POOL_WINDOWS = (2, 4, 8, 16)
N_POOL_GROUPS = len(POOL_WINDOWS)
POOL_GROUP = D_MODEL // N_POOL_GROUPS
GDN_HEADS = 8
GDN_HEAD_DIM = 128
GDN_KEY_DIM = GDN_HEADS * GDN_HEAD_DIM
GDN_VAL_DIM = GDN_HEADS * GDN_HEAD_DIM
GDN_CONV_DIM = 2 * GDN_KEY_DIM + GDN_VAL_DIM
GDN_IN_DIM = GDN_CONV_DIM + GDN_VAL_DIM + 2 * GDN_HEADS
CONV_WIDTH = 4
CHUNK = 64
DEEPNORM_ALPHA = (2.0 * DEPTH) ** 0.25
DEEPNORM_BETA = (8.0 * DEPTH) ** -0.25
LN_EPS = 1e-5
RMS_EPS = 1e-6
L2_EPS = 1e-6

kernel_name = "pool_gdn_deepnorm_hybrid"


def layer_norm(x, g, b):
    xf = x.astype(jnp.float32)
    mu = jnp.mean(xf, axis=-1, keepdims=True)
    var = jnp.mean(jnp.square(xf - mu), axis=-1, keepdims=True)
    return ((xf - mu) * lax.rsqrt(var + LN_EPS) * g + b).astype(x.dtype)


def pool_mixer(x, w_grp, b_grp, scale):
    B, S, _ = x.shape
    xg = x.astype(jnp.float32).reshape(B, S, N_POOL_GROUPS, POOL_GROUP)
    cs = jnp.cumsum(xg, axis=1)
    pos = jnp.arange(1, S + 1, dtype=jnp.float32)[None, :, None]
    outs = []
    for gi, w in enumerate(POOL_WINDOWS):
        c = cs[:, :, gi]
        c_prev = jnp.pad(c, ((0, 0), (w, 0), (0, 0)))[:, :S]
        mean = (c - c_prev) / jnp.minimum(pos, float(w))
        outs.append(mean - xg[:, :, gi])
    pooled = jnp.stack(outs, axis=2).astype(x.dtype)
    y = jnp.einsum('bsgc,gcd->bsgd', pooled, w_grp) + b_grp
    return y.reshape(B, S, D_MODEL) * scale


def causal_depthwise_conv(x, w):
    C = x.shape[-1]
    return lax.conv_general_dilated(
        x, w[:, None, :].astype(x.dtype), window_strides=(1,),
        padding=[(CONV_WIDTH - 1, 0)], dimension_numbers=('NWC', 'WIO', 'NWC'),
        feature_group_count=C)


def l2_normalize(t):
    return t * lax.rsqrt(jnp.sum(jnp.square(t), axis=-1, keepdims=True) + L2_EPS)


def chunk_gated_delta_rule(q, k, v, g, beta):
    B, S, H, Dk = q.shape
    Dv = v.shape[-1]
    N = S // CHUNK

    def to_chunks(t):
        return t.reshape(B, N, CHUNK, H, -1).transpose(0, 3, 1, 2, 4)

    q, k, v = to_chunks(q), to_chunks(k), to_chunks(v)
    g = g.reshape(B, N, CHUNK, H).transpose(0, 3, 1, 2)
    beta = beta.reshape(B, N, CHUNK, H).transpose(0, 3, 1, 2)
    g = jnp.cumsum(g, axis=-1)

    idx = jnp.arange(CHUNK)
    causal = idx[:, None] >= idx[None, :]
    strict = idx[:, None] > idx[None, :]
    decay = jnp.exp(jnp.where(causal, g[..., :, None] - g[..., None, :], -jnp.inf))

    k_beta = k * beta[..., None]
    a = jnp.einsum('bhncd,bhnmd->bhncm', k_beta, k) * decay
    m = jnp.where(strict, a, 0.0) + jnp.eye(CHUNK, dtype=jnp.float32)
    rhs = jnp.concatenate([v * beta[..., None], k_beta * jnp.exp(g)[..., None]], axis=-1)
    sol = lax.linalg.triangular_solve(m, rhs, left_side=True, lower=True, unit_diagonal=True)
    u, w = sol[..., :Dv], sol[..., Dv:]

    qk = jnp.einsum('bhncd,bhnmd->bhncm', q, k) * decay
    q_dec = q * jnp.exp(g)[..., None]
    k_dec = k * jnp.exp(g[..., -1:] - g)[..., None]
    g_last = jnp.exp(g[..., -1])

    def step(state, xs):
        u_c, w_c, qk_c, qd_c, kd_c, gl_c = xs
        v_new = u_c - jnp.einsum('bhcd,bhde->bhce', w_c, state)
        o_c = (jnp.einsum('bhcd,bhde->bhce', qd_c, state)
               + jnp.einsum('bhcm,bhme->bhce', qk_c, v_new))
        state = state * gl_c[..., None, None] + jnp.einsum('bhcd,bhce->bhde', kd_c, v_new)
        return state, o_c

    xs = tuple(jnp.moveaxis(t, 2, 0) for t in (u, w, qk, q_dec, k_dec, g_last))
    s0 = jnp.zeros((B, H, Dk, Dv), jnp.float32)
    _, o = lax.scan(step, s0, xs)
    return o.transpose(1, 0, 3, 2, 4).reshape(B, S, H, Dv)


def gated_deltanet(x, w_in, conv_w, a_log, dt_bias, norm_w, w_out):
    B, S, _ = x.shape
    H, Dh = GDN_HEADS, GDN_HEAD_DIM
    proj = x @ w_in
    qkv, z, b_logit, a_logit = jnp.split(
        proj, [GDN_CONV_DIM, GDN_CONV_DIM + GDN_VAL_DIM, GDN_CONV_DIM + GDN_VAL_DIM + H], axis=-1)
    qkv = jax.nn.silu(causal_depthwise_conv(qkv, conv_w))
    q, k, v = jnp.split(qkv, [GDN_KEY_DIM, 2 * GDN_KEY_DIM], axis=-1)
    q = l2_normalize(q.reshape(B, S, H, Dh).astype(jnp.float32)) * (Dh ** -0.5)
    k = l2_normalize(k.reshape(B, S, H, Dh).astype(jnp.float32))
    v = v.reshape(B, S, H, Dh).astype(jnp.float32)
    beta = jax.nn.sigmoid(b_logit.astype(jnp.float32))
    g = -jnp.exp(a_log.astype(jnp.float32)) * jax.nn.softplus(
        a_logit.astype(jnp.float32) + dt_bias.astype(jnp.float32))
    o = chunk_gated_delta_rule(q, k, v, g, beta)
    zf = z.reshape(B, S, H, Dh).astype(jnp.float32)
    o = o * lax.rsqrt(jnp.mean(jnp.square(o), axis=-1, keepdims=True) + RMS_EPS) * norm_w * jax.nn.silu(zf)
    return o.reshape(B, S, GDN_VAL_DIM).astype(x.dtype) @ w_out


def squared_relu_mlp(x, w1, w2):
    return jnp.square(jax.nn.relu(x @ w1)) @ w2


def _fwd_setup_inputs(seed: int = 0) -> dict:
    key = jax.random.key(seed)
    ks = jax.random.split(key, 20)
    f32 = jnp.float32
    nrm = lambda k, s: jax.random.normal(k, s, f32)
    dt = jnp.exp(jax.random.uniform(ks[9], (N_GDN_LAYERS, GDN_HEADS), f32,
                                    math.log(1e-3), math.log(1e-1)))
    return {
        "x": nrm(ks[0], (BATCH, SEQ, D_MODEL)),
        "p": nrm(ks[1], (DEPTH, BATCH, SEQ, PLE_DIM)),
        "ln_gain": 1.0 + 0.02 * nrm(ks[2], (DEPTH, 2, D_MODEL)),
        "ln_bias": 0.02 * nrm(ks[3], (DEPTH, 2, D_MODEL)),
        "pool_w": nrm(ks[4], (N_POOL_LAYERS, N_POOL_GROUPS, POOL_GROUP, POOL_GROUP)) * (POOL_GROUP ** -0.5) * DEEPNORM_BETA,
        "pool_b": 0.02 * nrm(ks[5], (N_POOL_LAYERS, N_POOL_GROUPS, POOL_GROUP)),
        "pool_scale": 1.0 + 0.1 * nrm(ks[6], (N_POOL_LAYERS, D_MODEL)),
        "gdn_w_in": nrm(ks[7], (N_GDN_LAYERS, D_MODEL, GDN_IN_DIM)) * (D_MODEL ** -0.5),
        "gdn_conv": nrm(ks[8], (N_GDN_LAYERS, CONV_WIDTH, GDN_CONV_DIM)) * (CONV_WIDTH ** -0.5),
        "gdn_a_log": jnp.log(jax.random.uniform(ks[10], (N_GDN_LAYERS, GDN_HEADS), f32, 1.0, 16.0)),
        "gdn_dt_bias": dt + jnp.log(-jnp.expm1(-dt)),
        "gdn_norm_w": 1.0 + 0.02 * nrm(ks[11], (N_GDN_LAYERS, GDN_HEAD_DIM)),
        "gdn_w_out": nrm(ks[12], (N_GDN_LAYERS, GDN_VAL_DIM, D_MODEL)) * (GDN_VAL_DIM ** -0.5) * DEEPNORM_BETA,
        "mlp_w1": nrm(ks[13], (DEPTH, D_MODEL, D_FF)) * (D_MODEL ** -0.5),
        "mlp_w2": nrm(ks[14], (DEPTH, D_FF, D_MODEL)) * (D_FF ** -0.5) * DEEPNORM_BETA,
        "ple_gate_w": nrm(ks[15], (DEPTH, D_MODEL, D_MODEL)) * (D_MODEL ** -0.5),
        "ple_gate_b": 0.02 * nrm(ks[16], (DEPTH, D_MODEL)),
        "ple_proj": nrm(ks[17], (DEPTH, PLE_DIM, D_MODEL)) * (PLE_DIM ** -0.5),
    }


def _fwd_reference(x, p, ln_gain, ln_bias, pool_w, pool_b, pool_scale, gdn_w_in, gdn_conv,
              gdn_a_log, gdn_dt_bias, gdn_norm_w, gdn_w_out, mlp_w1, mlp_w2,
              ple_gate_w, ple_gate_b, ple_proj):
    for i in range(DEPTH):
        j = i // N_MIXERS
        if i % N_MIXERS == 0:
            mix = pool_mixer(x, pool_w[j], pool_b[j], pool_scale[j])
        else:
            mix = gated_deltanet(x, gdn_w_in[j], gdn_conv[j], gdn_a_log[j], gdn_dt_bias[j],
                                 gdn_norm_w[j], gdn_w_out[j])
        x = layer_norm(DEEPNORM_ALPHA * x + mix, ln_gain[i, 0], ln_bias[i, 0])
        ff = squared_relu_mlp(x, mlp_w1[i], mlp_w2[i])
        pe = jax.nn.sigmoid(x @ ple_gate_w[i] + ple_gate_b[i]) * (p[i] @ ple_proj[i])
        x = layer_norm(DEEPNORM_ALPHA * x + ff + pe, ln_gain[i, 1], ln_bias[i, 1])
    return x


import jax as _jax
import jax.numpy as _jnp

TWIN_FORMAT = 'train_step'
FWD_PARAMS = ['x', 'p', 'ln_gain', 'ln_bias', 'pool_w', 'pool_b', 'pool_scale', 'gdn_w_in', 'gdn_conv', 'gdn_a_log', 'gdn_dt_bias', 'gdn_norm_w', 'gdn_w_out', 'mlp_w1', 'mlp_w2', 'ple_gate_w', 'ple_gate_b', 'ple_proj']
TWIN_WEIGHTS = ['ln_gain', 'ln_bias', 'pool_w', 'pool_b', 'pool_scale', 'gdn_w_in', 'gdn_conv', 'gdn_a_log', 'gdn_dt_bias', 'gdn_norm_w', 'gdn_w_out', 'mlp_w1', 'mlp_w2', 'ple_gate_w', 'ple_gate_b', 'ple_proj']
TWIN_DIFF_INPUT = 'x'
TWIN_INPUTS = ['x', 'p', 'ln_gain', 'ln_bias', 'pool_w', 'pool_b', 'pool_scale', 'gdn_w_in', 'gdn_conv', 'gdn_a_log', 'gdn_dt_bias', 'gdn_norm_w', 'gdn_w_out', 'mlp_w1', 'mlp_w2', 'ple_gate_w', 'ple_gate_b', 'ple_proj', 'loss_target', 'm_ln_gain', 'm_ln_bias', 'm_pool_w', 'm_pool_b', 'm_pool_scale', 'm_gdn_w_in', 'm_gdn_conv', 'm_gdn_a_log', 'm_gdn_dt_bias', 'm_gdn_norm_w', 'm_gdn_w_out', 'm_mlp_w1', 'm_mlp_w2', 'm_ple_gate_w', 'm_ple_gate_b', 'm_ple_proj', 'v_ln_gain', 'v_ln_bias', 'v_pool_w', 'v_pool_b', 'v_pool_scale', 'v_gdn_w_in', 'v_gdn_conv', 'v_gdn_a_log', 'v_gdn_dt_bias', 'v_gdn_norm_w', 'v_gdn_w_out', 'v_mlp_w1', 'v_mlp_w2', 'v_ple_gate_w', 'v_ple_gate_b', 'v_ple_proj']
TWIN_OUTPUTS = ['loss', 'grad_x', 'grad_ln_gain', 'grad_ln_bias', 'grad_pool_w', 'grad_pool_b', 'grad_pool_scale', 'grad_gdn_w_in', 'grad_gdn_conv', 'grad_gdn_a_log', 'grad_gdn_dt_bias', 'grad_gdn_norm_w', 'grad_gdn_w_out', 'grad_mlp_w1', 'grad_mlp_w2', 'grad_ple_gate_w', 'grad_ple_gate_b', 'grad_ple_proj', 'delta_ln_gain', 'delta_ln_bias', 'delta_pool_w', 'delta_pool_b', 'delta_pool_scale', 'delta_gdn_w_in', 'delta_gdn_conv', 'delta_gdn_a_log', 'delta_gdn_dt_bias', 'delta_gdn_norm_w', 'delta_gdn_w_out', 'delta_mlp_w1', 'delta_mlp_w2', 'delta_ple_gate_w', 'delta_ple_gate_b', 'delta_ple_proj', 'new_m_ln_gain', 'new_m_ln_bias', 'new_m_pool_w', 'new_m_pool_b', 'new_m_pool_scale', 'new_m_gdn_w_in', 'new_m_gdn_conv', 'new_m_gdn_a_log', 'new_m_gdn_dt_bias', 'new_m_gdn_norm_w', 'new_m_gdn_w_out', 'new_m_mlp_w1', 'new_m_mlp_w2', 'new_m_ple_gate_w', 'new_m_ple_gate_b', 'new_m_ple_proj', 'new_v_ln_gain', 'new_v_ln_bias', 'new_v_pool_w', 'new_v_pool_b', 'new_v_pool_scale', 'new_v_gdn_w_in', 'new_v_gdn_conv', 'new_v_gdn_a_log', 'new_v_gdn_dt_bias', 'new_v_gdn_norm_w', 'new_v_gdn_w_out', 'new_v_mlp_w1', 'new_v_mlp_w2', 'new_v_ple_gate_w', 'new_v_ple_gate_b', 'new_v_ple_proj']
TWIN_LEAF_KINDS = {'loss': 'loss', 'grad_x': 'grad_x', 'grad_ln_gain': 'grad_w', 'grad_ln_bias': 'grad_w', 'grad_pool_w': 'grad_w', 'grad_pool_b': 'grad_w', 'grad_pool_scale': 'grad_w', 'grad_gdn_w_in': 'grad_w', 'grad_gdn_conv': 'grad_w', 'grad_gdn_a_log': 'grad_w', 'grad_gdn_dt_bias': 'grad_w', 'grad_gdn_norm_w': 'grad_w', 'grad_gdn_w_out': 'grad_w', 'grad_mlp_w1': 'grad_w', 'grad_mlp_w2': 'grad_w', 'grad_ple_gate_w': 'grad_w', 'grad_ple_gate_b': 'grad_w', 'grad_ple_proj': 'grad_w', 'delta_ln_gain': 'delta_w', 'delta_ln_bias': 'delta_w', 'delta_pool_w': 'delta_w', 'delta_pool_b': 'delta_w', 'delta_pool_scale': 'delta_w', 'delta_gdn_w_in': 'delta_w', 'delta_gdn_conv': 'delta_w', 'delta_gdn_a_log': 'delta_w', 'delta_gdn_dt_bias': 'delta_w', 'delta_gdn_norm_w': 'delta_w', 'delta_gdn_w_out': 'delta_w', 'delta_mlp_w1': 'delta_w', 'delta_mlp_w2': 'delta_w', 'delta_ple_gate_w': 'delta_w', 'delta_ple_gate_b': 'delta_w', 'delta_ple_proj': 'delta_w', 'new_m_ln_gain': 'new_m', 'new_m_ln_bias': 'new_m', 'new_m_pool_w': 'new_m', 'new_m_pool_b': 'new_m', 'new_m_pool_scale': 'new_m', 'new_m_gdn_w_in': 'new_m', 'new_m_gdn_conv': 'new_m', 'new_m_gdn_a_log': 'new_m', 'new_m_gdn_dt_bias': 'new_m', 'new_m_gdn_norm_w': 'new_m', 'new_m_gdn_w_out': 'new_m', 'new_m_mlp_w1': 'new_m', 'new_m_mlp_w2': 'new_m', 'new_m_ple_gate_w': 'new_m', 'new_m_ple_gate_b': 'new_m', 'new_m_ple_proj': 'new_m', 'new_v_ln_gain': 'new_v', 'new_v_ln_bias': 'new_v', 'new_v_pool_w': 'new_v', 'new_v_pool_b': 'new_v', 'new_v_pool_scale': 'new_v', 'new_v_gdn_w_in': 'new_v', 'new_v_gdn_conv': 'new_v', 'new_v_gdn_a_log': 'new_v', 'new_v_gdn_dt_bias': 'new_v', 'new_v_gdn_norm_w': 'new_v', 'new_v_gdn_w_out': 'new_v', 'new_v_mlp_w1': 'new_v', 'new_v_mlp_w2': 'new_v', 'new_v_ple_gate_w': 'new_v', 'new_v_ple_gate_b': 'new_v', 'new_v_ple_proj': 'new_v'}


def _forward(args):
    return _fwd_reference(*[args[k] for k in FWD_PARAMS])


def _output_shape():
    def fwd():
        inp = _fwd_setup_inputs(0)
        return _fwd_reference(*[inp[k] for k in FWD_PARAMS])
    out = _jax.eval_shape(fwd)
    return out.shape, out.dtype

N_MICROBATCH = 1
ADAM_LR = 0.001
ADAM_B1 = 0.9
ADAM_B2 = 0.999
ADAM_EPS = 1e-08
ADAM_WD = 0.01
ADAM_STEP = 10
PER_EXAMPLE_BATCH_AXIS = {'x': 0, 'p': 1, 'loss_target': 0}
SHARED_INPUTS = []
_WEIGHT_DTYPES = {'ln_gain': _jnp.float32, 'ln_bias': _jnp.float32, 'pool_w': _jnp.float32, 'pool_b': _jnp.float32, 'pool_scale': _jnp.float32, 'gdn_w_in': _jnp.float32, 'gdn_conv': _jnp.float32, 'gdn_a_log': _jnp.float32, 'gdn_dt_bias': _jnp.float32, 'gdn_norm_w': _jnp.float32, 'gdn_w_out': _jnp.float32, 'mlp_w1': _jnp.float32, 'mlp_w2': _jnp.float32, 'ple_gate_w': _jnp.float32, 'ple_gate_b': _jnp.float32, 'ple_proj': _jnp.float32}
MOMENT_SCALE = {'ln_gain': 3.212695e+01, 'ln_bias': 6.641921e+00, 'pool_w': 1.494952e-01, 'pool_b': 6.718551e-01, 'pool_scale': 1.031981e-01, 'gdn_w_in': 4.250903e-02, 'gdn_conv': 4.238241e-02, 'gdn_a_log': 3.134459e-01, 'gdn_dt_bias': 3.045729e-01, 'gdn_norm_w': 2.778699e-01, 'gdn_w_out': 1.226849e-01, 'mlp_w1': 5.609394e-02, 'mlp_w2': 3.265213e-01, 'ple_gate_w': 3.324379e-02, 'ple_gate_b': 6.242496e-02, 'ple_proj': 8.397146e-02}


def _to_microbatches(a, axis):
    t = _jnp.moveaxis(a, axis, 0)
    t = t.reshape((N_MICROBATCH, t.shape[0] // N_MICROBATCH) + t.shape[1:])
    return _jnp.moveaxis(t, 1, axis + 1)


def setup_inputs(seed: int = 0) -> dict:
    inp = _fwd_setup_inputs(seed)
    key = _jax.random.fold_in(_jax.random.key(seed), 7919)
    shape, _ = _output_shape()
    out = dict(inp)
    out["loss_target"] = _jax.random.normal(_jax.random.fold_in(key, 0), shape, _jnp.float32)
    for i, name in enumerate(TWIN_WEIGHTS):
        w = inp[name].astype(_jnp.float32)
        if MOMENT_SCALE is None:
            s = _jnp.sqrt(_jnp.mean(_jnp.square(w)) + 1e-30)
        else:
            s = MOMENT_SCALE[name]
        km, kv = _jax.random.split(_jax.random.fold_in(key, i + 1))
        out[name] = w
        out["m_" + name] = s * _jax.random.normal(km, w.shape, _jnp.float32)
        out["v_" + name] = (s * s) * _jax.random.uniform(kv, w.shape, _jnp.float32, 0.5, 1.5)
    if N_MICROBATCH > 1:
        for name, axis in PER_EXAMPLE_BATCH_AXIS.items():
            out[name] = _to_microbatches(out[name], axis)
    return {'x': out['x'], 'p': out['p'], 'ln_gain': out['ln_gain'], 'ln_bias': out['ln_bias'], 'pool_w': out['pool_w'], 'pool_b': out['pool_b'], 'pool_scale': out['pool_scale'], 'gdn_w_in': out['gdn_w_in'], 'gdn_conv': out['gdn_conv'], 'gdn_a_log': out['gdn_a_log'], 'gdn_dt_bias': out['gdn_dt_bias'], 'gdn_norm_w': out['gdn_norm_w'], 'gdn_w_out': out['gdn_w_out'], 'mlp_w1': out['mlp_w1'], 'mlp_w2': out['mlp_w2'], 'ple_gate_w': out['ple_gate_w'], 'ple_gate_b': out['ple_gate_b'], 'ple_proj': out['ple_proj'], 'loss_target': out['loss_target'], 'm_ln_gain': out['m_ln_gain'], 'm_ln_bias': out['m_ln_bias'], 'm_pool_w': out['m_pool_w'], 'm_pool_b': out['m_pool_b'], 'm_pool_scale': out['m_pool_scale'], 'm_gdn_w_in': out['m_gdn_w_in'], 'm_gdn_conv': out['m_gdn_conv'], 'm_gdn_a_log': out['m_gdn_a_log'], 'm_gdn_dt_bias': out['m_gdn_dt_bias'], 'm_gdn_norm_w': out['m_gdn_norm_w'], 'm_gdn_w_out': out['m_gdn_w_out'], 'm_mlp_w1': out['m_mlp_w1'], 'm_mlp_w2': out['m_mlp_w2'], 'm_ple_gate_w': out['m_ple_gate_w'], 'm_ple_gate_b': out['m_ple_gate_b'], 'm_ple_proj': out['m_ple_proj'], 'v_ln_gain': out['v_ln_gain'], 'v_ln_bias': out['v_ln_bias'], 'v_pool_w': out['v_pool_w'], 'v_pool_b': out['v_pool_b'], 'v_pool_scale': out['v_pool_scale'], 'v_gdn_w_in': out['v_gdn_w_in'], 'v_gdn_conv': out['v_gdn_conv'], 'v_gdn_a_log': out['v_gdn_a_log'], 'v_gdn_dt_bias': out['v_gdn_dt_bias'], 'v_gdn_norm_w': out['v_gdn_norm_w'], 'v_gdn_w_out': out['v_gdn_w_out'], 'v_mlp_w1': out['v_mlp_w1'], 'v_mlp_w2': out['v_mlp_w2'], 'v_ple_gate_w': out['v_ple_gate_w'], 'v_ple_gate_b': out['v_ple_gate_b'], 'v_ple_proj': out['v_ple_proj']}


def _loss(weights, diff, rest, loss_target):
    with _jax.named_scope("forward"):
        args = {**rest, TWIN_DIFF_INPUT: diff, **{k: w.astype(_WEIGHT_DTYPES[k]) for k, w in weights.items()}}
        y = _forward(args)
    with _jax.named_scope("loss_head"):
        err = _jnp.square(y.astype(_jnp.float32) - loss_target)
        return 0.5 * _jnp.sum(_jnp.mean(err, axis=-1)) if err.ndim else 0.5 * err


def _adamw(w, g, m, v):
    m = ADAM_B1 * m + (1.0 - ADAM_B1) * g
    v = ADAM_B2 * v + (1.0 - ADAM_B2) * _jnp.square(g)
    m_hat = m / (1.0 - ADAM_B1 ** ADAM_STEP)
    v_hat = v / (1.0 - ADAM_B2 ** ADAM_STEP)
    delta = -ADAM_LR * (m_hat / (_jnp.sqrt(v_hat) + ADAM_EPS) + ADAM_WD * w)
    return delta, m, v


def reference(x, p, ln_gain, ln_bias, pool_w, pool_b, pool_scale, gdn_w_in, gdn_conv, gdn_a_log, gdn_dt_bias, gdn_norm_w, gdn_w_out, mlp_w1, mlp_w2, ple_gate_w, ple_gate_b, ple_proj, loss_target, m_ln_gain, m_ln_bias, m_pool_w, m_pool_b, m_pool_scale, m_gdn_w_in, m_gdn_conv, m_gdn_a_log, m_gdn_dt_bias, m_gdn_norm_w, m_gdn_w_out, m_mlp_w1, m_mlp_w2, m_ple_gate_w, m_ple_gate_b, m_ple_proj, v_ln_gain, v_ln_bias, v_pool_w, v_pool_b, v_pool_scale, v_gdn_w_in, v_gdn_conv, v_gdn_a_log, v_gdn_dt_bias, v_gdn_norm_w, v_gdn_w_out, v_mlp_w1, v_mlp_w2, v_ple_gate_w, v_ple_gate_b, v_ple_proj):
    given = dict(x=x, p=p, ln_gain=ln_gain, ln_bias=ln_bias, pool_w=pool_w, pool_b=pool_b, pool_scale=pool_scale, gdn_w_in=gdn_w_in, gdn_conv=gdn_conv, gdn_a_log=gdn_a_log, gdn_dt_bias=gdn_dt_bias, gdn_norm_w=gdn_norm_w, gdn_w_out=gdn_w_out, mlp_w1=mlp_w1, mlp_w2=mlp_w2, ple_gate_w=ple_gate_w, ple_gate_b=ple_gate_b, ple_proj=ple_proj, loss_target=loss_target, m_ln_gain=m_ln_gain, m_ln_bias=m_ln_bias, m_pool_w=m_pool_w, m_pool_b=m_pool_b, m_pool_scale=m_pool_scale, m_gdn_w_in=m_gdn_w_in, m_gdn_conv=m_gdn_conv, m_gdn_a_log=m_gdn_a_log, m_gdn_dt_bias=m_gdn_dt_bias, m_gdn_norm_w=m_gdn_norm_w, m_gdn_w_out=m_gdn_w_out, m_mlp_w1=m_mlp_w1, m_mlp_w2=m_mlp_w2, m_ple_gate_w=m_ple_gate_w, m_ple_gate_b=m_ple_gate_b, m_ple_proj=m_ple_proj, v_ln_gain=v_ln_gain, v_ln_bias=v_ln_bias, v_pool_w=v_pool_w, v_pool_b=v_pool_b, v_pool_scale=v_pool_scale, v_gdn_w_in=v_gdn_w_in, v_gdn_conv=v_gdn_conv, v_gdn_a_log=v_gdn_a_log, v_gdn_dt_bias=v_gdn_dt_bias, v_gdn_norm_w=v_gdn_norm_w, v_gdn_w_out=v_gdn_w_out, v_mlp_w1=v_mlp_w1, v_mlp_w2=v_mlp_w2, v_ple_gate_w=v_ple_gate_w, v_ple_gate_b=v_ple_gate_b, v_ple_proj=v_ple_proj)
    weights = {n: given[n] for n in TWIN_WEIGHTS}
    shared = {n: given[n] for n in SHARED_INPUTS}
    per_example = {n: given[n] for n in ['x', 'p']}
    grad_fn = _jax.value_and_grad(_loss, argnums=(0, 1))

    def one_microbatch(ex, loss_target):
        ex = dict(ex)
        diff = ex.pop(TWIN_DIFF_INPUT)
        return grad_fn(weights, diff, {**shared, **ex}, loss_target)

    if N_MICROBATCH == 1:
        loss, (grad_w, grad_x) = one_microbatch(per_example, given["loss_target"])
    else:
        def body(carry, xs):
            loss_sum, grad_sum = carry
            l_k, (gw_k, gx_k) = one_microbatch(xs[0], xs[1])
            with _jax.named_scope("update"):
                return (loss_sum + l_k, _jax.tree.map(_jnp.add, grad_sum, gw_k)), gx_k

        init = (_jnp.zeros((), _jnp.float32), _jax.tree.map(_jnp.zeros_like, weights))
        (loss, grad_w), grad_x = _jax.lax.scan(body, init, (per_example, given["loss_target"]))
    with _jax.named_scope("update"):
        delta_w, new_m, new_v = {}, {}, {}
        for n in TWIN_WEIGHTS:
            delta_w[n], new_m[n], new_v[n] = _adamw(weights[n], grad_w[n], given["m_" + n], given["v_" + n])
    return (loss, grad_x, *[grad_w[n] for n in TWIN_WEIGHTS], *[delta_w[n] for n in TWIN_WEIGHTS],
            *[new_m[n] for n in TWIN_WEIGHTS], *[new_v[n] for n in TWIN_WEIGHTS])
```

```python
import functools

import jax
import jax.numpy as jnp
from jax import lax
from jax.experimental import pallas as pl
from jax.experimental.pallas import tpu as pltpu

F32 = jnp.float32
BF16 = jnp.bfloat16
MESH_AXES = ("x", "y", "c")
N_DEV = 8
MESH = pl.DeviceIdType.MESH

DEPTH = 2
ALPHA = (2.0 * DEPTH) ** 0.25
LN_EPS = 1e-5
RMS_EPS = 1e-6
L2_EPS = 1e-6
HEAD_DIM = 128
CONV_WIDTH = 4
POOL_WINDOWS = (2, 4, 8, 16)
POOL_HALO = 16
CONV_HALO = 8
LANES = 128
ADAM_LR = 0.001
ADAM_B1 = 0.9
ADAM_B2 = 0.999
ADAM_EPS = 1e-08
ADAM_WD = 0.01
ADAM_STEP = 10

VMEM_LIMIT = 56 * 1024 * 1024
ROW_TILE = 256
CONV_TILE = 256
CHUNK = 128
MM_TM, MM_TN, MM_TK = 512, 1024, 1024

_DIMS = {
    "nn": (((1,), (0,)), ((), ())),
    "nt": (((1,), (1,)), ((), ())),
    "tn": (((0,), (0,)), ((), ())),
}


def _params(n_axes):
    return pltpu.CompilerParams(dimension_semantics=("arbitrary",) * n_axes, vmem_limit_bytes=VMEM_LIMIT)


def _mm(name, a, b, mode, out_dtypes, epi=None, extras=(), a_fn=None, tm=None, tn=None, tk=None):
    if mode == "tn":
        K, M = a.shape
    else:
        M, K = a.shape
    N = b.shape[0] if mode == "nt" else b.shape[1]
    tm, tn, tk = min(tm or MM_TM, M), min(tn or MM_TN, N), min(tk or MM_TK, K)
    assert M % tm == 0 and N % tn == 0 and K % tk == 0, (name, M, N, K)
    nk = K // tk
    a_spec = pl.BlockSpec((tk, tm), lambda i, j, k: (k, i)) if mode == "tn" else pl.BlockSpec((tm, tk), lambda i, j, k: (i, k))
    b_spec = pl.BlockSpec((tn, tk), lambda i, j, k: (j, k)) if mode == "nt" else pl.BlockSpec((tk, tn), lambda i, j, k: (k, j))
    ex_specs = [
        pl.BlockSpec((tm, tn), lambda i, j, k: (i, j)) if kind == "tile" else pl.BlockSpec((1, tn), lambda i, j, k: (0, j))
        for _, kind in extras
    ]
    n_ex, n_out = len(extras), len(out_dtypes)

    def body(*refs):
        a_ref, b_ref = refs[0], refs[1]
        ex_refs = refs[2 : 2 + n_ex]
        out_refs = refs[2 + n_ex : 2 + n_ex + n_out]
        av = a_ref[...]
        if a_fn is not None:
            av = a_fn(av)
        part = lax.dot_general(av.astype(BF16), b_ref[...].astype(BF16), _DIMS[mode], preferred_element_type=F32)

        def finish(res):
            vals = epi(res, *[e[...] for e in ex_refs]) if epi is not None else (res,)
            for o_ref, v in zip(out_refs, vals):
                o_ref[...] = v.astype(o_ref.dtype)

        if nk == 1:
            finish(part)
        else:
            acc = refs[-1]
            k = pl.program_id(2)

            @pl.when(k == 0)
            def _():
                acc[...] = part

            @pl.when(k > 0)
            def _():
                acc[...] += part

            @pl.when(k == nk - 1)
            def _():
                finish(acc[...])

    outs = pl.pallas_call(
        body,
        name=name,
        grid=(M // tm, N // tn, nk),
        in_specs=[a_spec, b_spec] + ex_specs,
        out_specs=[pl.BlockSpec((tm, tn), lambda i, j, k: (i, j)) for _ in out_dtypes],
        out_shape=[jax.ShapeDtypeStruct((M, N), dt) for dt in out_dtypes],
        scratch_shapes=[pltpu.VMEM((tm, tn), F32)] if nk > 1 else [],
        compiler_params=_params(3),
    )(a, b, *[e for e, _ in extras])
    return outs[0] if n_out == 1 else outs


def _rowwise(name, fn, S, ts, rows=(), halos=(), consts=(), outs=(), accs=()):
    ts = min(ts, S)
    assert S % ts == 0
    n = S // ts
    in_specs = [pl.BlockSpec((ts, a.shape[1]), lambda i: (i, 0)) for a in rows]
    for a, kind, hr in halos:
        r, nb = ts // hr, S // hr
        if kind == "prev":
            in_specs.append(pl.BlockSpec((hr, a.shape[1]), lambda i, r=r: (jnp.maximum(i * r - 1, 0), 0)))
        else:
            in_specs.append(pl.BlockSpec((hr, a.shape[1]), lambda i, r=r, nb=nb: (jnp.minimum((i + 1) * r, nb - 1), 0)))
    in_specs += [pl.BlockSpec(a.shape, lambda i, nd=a.ndim: (0,) * nd) for a in consts]
    out_specs = [pl.BlockSpec((ts, w), lambda i: (i, 0)) for w, _ in outs]
    out_specs += [pl.BlockSpec((r, w), lambda i: (0, 0)) for r, w in accs]
    out_shape = [jax.ShapeDtypeStruct((S, w), dt) for w, dt in outs]
    out_shape += [jax.ShapeDtypeStruct((r, w), F32) for r, w in accs]
    nr, nh, nc, no = len(rows), len(halos), len(consts), len(outs)

    def body(*refs):
        i = pl.program_id(0)
        rv = [r[...] for r in refs[:nr]]
        hv = [r[...] for r in refs[nr : nr + nh]]
        cv = [r[...] for r in refs[nr + nh : nr + nh + nc]]
        o_refs = refs[nr + nh + nc : nr + nh + nc + no]
        a_refs = refs[nr + nh + nc + no :]
        ovals, avals = fn(i, n, rv, hv, cv)
        for o_ref, v in zip(o_refs, ovals):
            o_ref[...] = v.astype(o_ref.dtype)
        for a_ref, v in zip(a_refs, avals):

            @pl.when(i == 0)
            def _(a_ref=a_ref, v=v):
                a_ref[...] = v

            @pl.when(i > 0)
            def _(a_ref=a_ref, v=v):
                a_ref[...] += v

    res = pl.pallas_call(
        body,
        name=name,
        grid=(n,),
        in_specs=in_specs,
        out_specs=out_specs,
        out_shape=out_shape,
        compiler_params=_params(1),
    )(*rows, *[h[0] for h in halos], *consts)
    return list(res)


def _ln(h, g, b):
    mu = jnp.mean(h, axis=-1, keepdims=True)
    d = h - mu
    var = jnp.mean(d * d, axis=-1, keepdims=True)
    rstd = lax.rsqrt(var + LN_EPS)
    xhat = d * rstd
    return xhat, rstd, xhat * g + b


def _ln_bwd(dy, xhat, rstd, g):
    dxh = dy * g
    m1 = jnp.mean(dxh, axis=-1, keepdims=True)
    m2 = jnp.mean(dxh * xhat, axis=-1, keepdims=True)
    dh = rstd * (dxh - m1 - xhat * m2)
    return dh, jnp.sum(dy * xhat, axis=0, keepdims=True), jnp.sum(dy, axis=0, keepdims=True)


def _wide(col, ts):
    return jnp.broadcast_to(col, (ts, LANES))


def _pool_fwd(x, wp, pb, ps, g, b):
    S, D = x.shape
    gw = D // len(POOL_WINDOWS)
    ts = min(ROW_TILE, S)

    def fn(i, n, rv, hv, cv):
        (xc,), (xp,) = rv, hv
        wpv, pbv, psv, gv, bv = cv
        xp = jnp.where(i > 0, xp, 0.0)
        xx = jnp.concatenate([xp, xc], axis=0)
        t = i * ts + lax.broadcasted_iota(jnp.int32, (ts, 1), 0)
        pooled, ys = [], []
        for gi, w in enumerate(POOL_WINDOWS):
            s = xx[:, gi * gw : (gi + 1) * gw]
            k = 1
            while k < w:
                s = s + pltpu.roll(s, k, axis=0)
                k *= 2
            cnt = jnp.minimum(t + 1, w).astype(F32)
            pg = (s[POOL_HALO:, :] / cnt - xc[:, gi * gw : (gi + 1) * gw]).astype(BF16)
            pooled.append(pg)
            ys.append(jnp.dot(pg, wpv[gi], preferred_element_type=F32))
        y = jnp.concatenate(ys, axis=1)
        h = ALPHA * xc + (y + pbv) * psv
        xhat, rstd, xa = _ln(h, gv, bv)
        return (jnp.concatenate(pooled, axis=1), xhat, _wide(rstd, ts), xa), ()

    return _rowwise(
        "pool_fwd", fn, S, ts, rows=[x], halos=[(x, "prev", POOL_HALO)], consts=[wp, pb, ps, g, b],
        outs=[(D, BF16), (D, F32), (LANES, F32), (D, BF16)],
    )


def _pool_bwd(dh, pooled, wp, pb, ps):
    S, D = dh.shape
    gw = D // len(POOL_WINDOWS)
    ts = min(ROW_TILE, S)
    te = ts + POOL_HALO

    def fn(i, n, rv, hv, cv):
        (dhc, pc), (dhn,) = rv, hv
        wpv, pbv, psv = cv
        dhn = jnp.where(i < n - 1, dhn, 0.0)
        dy_ext = jnp.concatenate([dhc, dhn], axis=0) * psv
        dyb = dy_ext.astype(BF16)
        t = i * ts + lax.broadcasted_iota(jnp.int32, (te, 1), 0)
        dxs, ys = [], []
        for gi, w in enumerate(POOL_WINDOWS):
            sl = slice(gi * gw, (gi + 1) * gw)
            dp = lax.dot_general(dyb[:, sl], wpv[gi], _DIMS["nt"], preferred_element_type=F32)
            s = dp / jnp.minimum(t + 1, w).astype(F32)
            k = 1
            while k < w:
                s = s + pltpu.roll(s, k, axis=0)
                k *= 2
            s = pltpu.roll(s, POOL_HALO - (w - 1), axis=0)
            dxs.append(s[POOL_HALO:, :] - dp[:ts, :])
            ys.append(jnp.dot(pc[:, sl], wpv[gi], preferred_element_type=F32))
        dx = ALPHA * dhc + jnp.concatenate(dxs, axis=1)
        y = jnp.concatenate(ys, axis=1) + pbv
        dscale = jnp.sum(dhc * y, axis=0, keepdims=True)
        dbias = jnp.sum(dy_ext[:ts, :], axis=0, keepdims=True)
        return (dx, dyb[:ts, :]), (dscale, dbias)

    return _rowwise(
        "pool_bwd", fn, S, ts, rows=[dh, pooled], halos=[(dh, "next", POOL_HALO)], consts=[wp, pb, ps],
        outs=[(D, F32), (D, BF16)], accs=[(1, D), (1, D)],
    )


def _pool_dw(pooled, dy):
    S, D = pooled.shape
    G = len(POOL_WINDOWS)
    gw = D // G
    tk = min(MM_TK, S)
    nk = S // tk

    def body(p_ref, d_ref, o_ref):
        k = pl.program_id(1)
        part = lax.dot_general(p_ref[...], d_ref[...], _DIMS["tn"], preferred_element_type=F32)

        @pl.when(k == 0)
        def _():
            o_ref[...] = part

        @pl.when(k > 0)
        def _():
            o_ref[...] += part

    return pl.pallas_call(
        body,
        name="pool_dw",
        grid=(G, nk),
        in_specs=[pl.BlockSpec((tk, gw), lambda g, k: (k, g)), pl.BlockSpec((tk, gw), lambda g, k: (k, g))],
        out_specs=pl.BlockSpec((None, gw, gw), lambda g, k: (g, 0, 0)),
        out_shape=jax.ShapeDtypeStruct((G, gw, gw), F32),
        compiler_params=_params(2),
    )(pooled, dy)


def _res_ln_mix(name, xhat_p, mix, gp_, bp_, g, b):
    S, D = xhat_p.shape
    ts = min(ROW_TILE, S)

    def fn(i, n, rv, hv, cv):
        xh, m = rv
        gpv, bpv, gv, bv = cv
        xhat, rstd, xo = _ln(ALPHA * (xh * gpv + bpv) + m, gv, bv)
        return (xhat, _wide(rstd, ts), xo), ()

    return _rowwise(name, fn, S, ts, rows=[xhat_p, mix], consts=[gp_, bp_, g, b], outs=[(D, F32), (LANES, F32), (D, BF16)])


def _res_ln_ffpe(name, xhat_p, ff, gate, pp, gp_, bp_, g, b):
    S, D = xhat_p.shape
    ts = min(ROW_TILE, S)

    def fn(i, n, rv, hv, cv):
        xh, f, gt, p_ = rv
        gpv, bpv, gv, bv = cv
        xhat, rstd, xo = _ln(ALPHA * (xh * gpv + bpv) + f + jax.nn.sigmoid(gt) * p_, gv, bv)
        return (xhat, _wide(rstd, ts), xo), ()

    return _rowwise(name, fn, S, ts, rows=[xhat_p, ff, gate, pp], consts=[gp_, bp_, g, b], outs=[(D, F32), (LANES, F32), (D, BF16)])


def _final_ln_loss(xhat_p, ff, gate, pp, tgt, gp_, bp_, g, b):
    S, D = xhat_p.shape
    ts = min(ROW_TILE, S)

    def fn(i, n, rv, hv, cv):
        xh, f, gt, p_, tg = rv
        gpv, bpv, gv, bv = cv
        xhat, rstd, y = _ln(ALPHA * (xh * gpv + bpv) + f + jax.nn.sigmoid(gt) * p_, gv, bv)
        e = y - tg
        dh, dg, db = _ln_bwd(e * (1.0 / D), xhat, rstd, gv)
        return (dh, dh), (jnp.sum(e * e, axis=0, keepdims=True), dg, db)

    return _rowwise(
        "final_ln_loss", fn, S, ts, rows=[xhat_p, ff, gate, pp, tgt], consts=[gp_, bp_, g, b],
        outs=[(D, F32), (D, BF16)], accs=[(1, D), (1, D), (1, D)],
    )


def _ln_bwd_call(name, dy, xhat, rstd, g):
    S, D = dy.shape
    ts = min(ROW_TILE, S)

    def fn(i, n, rv, hv, cv):
        dyv, xh, rs = rv
        dh, dg, db = _ln_bwd(dyv, xh, rs[:, :1], cv[0])
        return (dh, dh), (dg, db)

    return _rowwise(name, fn, S, ts, rows=[dy, xhat, rstd], consts=[g], outs=[(D, F32), (D, BF16)], accs=[(1, D), (1, D)])


def _ple_bwd(name, dh, gate, pp):
    S, D = dh.shape
    ts = min(ROW_TILE, S)

    def fn(i, n, rv, hv, cv):
        d, gt, p_ = rv
        sg = jax.nn.sigmoid(gt)
        dgt = d * p_ * sg * (1.0 - sg)
        return (dgt, d * sg), (jnp.sum(dgt, axis=0, keepdims=True),)

    return _rowwise(name, fn, S, ts, rows=[dh, gate, pp], outs=[(D, BF16), (D, BF16)], accs=[(1, D)])


def _silu(c):
    return c * jax.nn.sigmoid(c)


def _qkv_point(c, is_qk, scale):
    s = _silu(c)
    nrm = s * lax.rsqrt(jnp.sum(s * s, axis=-1, keepdims=True) + L2_EPS) * scale
    return jnp.where(is_qk, nrm, s)


def _conv_rows(xx, wv, lo, rows):
    acc = None
    for j in range(CONV_WIDTH):
        sh = CONV_WIDTH - 1 - j
        term = (pltpu.roll(xx, sh, axis=0) if sh else xx)[lo : lo + rows, :] * wv[j : j + 1, :]
        acc = term if acc is None else acc + term
    return acc


def _conv_fwd(qkv_pre, conv_w):
    S, W = qkv_pre.shape
    D = W // 3
    H = D // HEAD_DIM
    ts = min(CONV_TILE, S)
    r = ts // CONV_HALO

    def body(x_ref, xp_ref, w_ref, o_ref):
        j, i = pl.program_id(0), pl.program_id(1)
        xp = jnp.where(i > 0, xp_ref[...], 0.0)
        xx = jnp.concatenate([xp, x_ref[...]], axis=0)
        c = _conv_rows(xx, w_ref[...], CONV_HALO, ts)
        scale = jnp.where(j == 0, HEAD_DIM**-0.5, 1.0).astype(F32)
        for h in range(H):
            sl = slice(h * HEAD_DIM, (h + 1) * HEAD_DIM)
            o_ref[:, sl] = _qkv_point(c[:, sl], j < 2, scale)

    return pl.pallas_call(
        body,
        name="gdn_conv_fwd",
        grid=(3, S // ts),
        in_specs=[
            pl.BlockSpec((ts, D), lambda j, i: (i, j)),
            pl.BlockSpec((CONV_HALO, D), lambda j, i: (jnp.maximum(i * r - 1, 0), j)),
            pl.BlockSpec((CONV_WIDTH, D), lambda j, i: (0, j)),
        ],
        out_specs=pl.BlockSpec((ts, D), lambda j, i: (i, j)),
        out_shape=jax.ShapeDtypeStruct((S, W), F32),
        compiler_params=_params(2),
    )(qkv_pre, qkv_pre, conv_w)


def _conv_bwd(qkv_pre, conv_w, dqkvn):
    S, W = qkv_pre.shape
    D = W // 3
    H = D // HEAD_DIM
    ts = min(CONV_TILE, S)
    r, nb = ts // CONV_HALO, S // CONV_HALO
    te = ts + CONV_HALO

    def body(x_ref, xp_ref, xn_ref, w_ref, d_ref, dn_ref, dx_ref, dw_ref):
        j, i = pl.program_id(0), pl.program_id(1)
        n = pl.num_programs(1)
        wv = w_ref[...]
        xp = jnp.where(i > 0, xp_ref[...], 0.0)
        xx = jnp.concatenate([xp, x_ref[...], xn_ref[...]], axis=0)
        c = _conv_rows(xx, wv, CONV_HALO, te)
        dn = jnp.where(i < n - 1, dn_ref[...], 0.0)
        dout = jnp.concatenate([d_ref[...], dn], axis=0)
        scale = jnp.where(j == 0, HEAD_DIM**-0.5, 1.0).astype(F32)
        dcs = []
        for h in range(H):
            sl = slice(h * HEAD_DIM, (h + 1) * HEAD_DIM)
            _, vjp = jax.vjp(lambda cc: _qkv_point(cc, j < 2, scale), c[:, sl])
            dcs.append(vjp(dout[:, sl])[0])
        dc = jnp.concatenate(dcs, axis=1)
        dx = None
        dws = []
        for jj in range(CONV_WIDTH):
            sh = CONV_WIDTH - 1 - jj
            term = pltpu.roll(dc, CONV_HALO - sh, axis=0)[CONV_HALO:, :] * wv[jj : jj + 1, :]
            dx = term if dx is None else dx + term
            xs = (pltpu.roll(xx, sh, axis=0) if sh else xx)[CONV_HALO : CONV_HALO + ts, :]
            dws.append(jnp.sum(dc[:ts, :] * xs, axis=0, keepdims=True))
        dx_ref[...] = dx.astype(dx_ref.dtype)
        dw = jnp.concatenate(dws, axis=0)

        @pl.when(i == 0)
        def _():
            dw_ref[...] = dw

        @pl.when(i > 0)
        def _():
            dw_ref[...] += dw

    return pl.pallas_call(
        body,
        name="gdn_conv_bwd",
        grid=(3, S // ts),
        in_specs=[
            pl.BlockSpec((ts, D), lambda j, i: (i, j)),
            pl.BlockSpec((CONV_HALO, D), lambda j, i: (jnp.maximum(i * r - 1, 0), j)),
            pl.BlockSpec((CONV_HALO, D), lambda j, i: (jnp.minimum((i + 1) * r, nb - 1), j)),
            pl.BlockSpec((CONV_WIDTH, D), lambda j, i: (0, j)),
            pl.BlockSpec((ts, D), lambda j, i: (i, j)),
            pl.BlockSpec((CONV_HALO, D), lambda j, i: (jnp.minimum((i + 1) * r, nb - 1), j)),
        ],
        out_specs=[pl.BlockSpec((ts, D), lambda j, i: (i, j)), pl.BlockSpec((CONV_WIDTH, D), lambda j, i: (0, j))],
        out_shape=[jax.ShapeDtypeStruct((S, W), BF16), jax.ShapeDtypeStruct((CONV_WIDTH, W), F32)],
        compiler_params=_params(2),
    )(qkv_pre, qkv_pre, qkv_pre, conv_w, dqkvn, dqkvn)


def _softplus(x):
    pos = x > 0.0
    return jnp.where(pos, x, 0.0) + jnp.log(1.0 + jnp.exp(jnp.where(pos, -x, x)))


def _gates(bl, al, alog, dt):
    return jax.nn.sigmoid(bl), -jnp.exp(alog) * _softplus(al + dt)


def _gates_fwd(ba, alog, dt):
    S = ba.shape[0]
    ts = min(ROW_TILE, S)

    def fn(i, n, rv, hv, cv):
        return _gates(rv[0][:, :LANES], rv[0][:, LANES:], cv[0], cv[1]), ()

    return _rowwise("gdn_gates_fwd", fn, S, ts, rows=[ba], consts=[alog, dt], outs=[(LANES, F32), (LANES, F32)])


def _gates_bwd(ba, alog, dt, dbeta, dg, H):
    S = ba.shape[0]
    ts = min(ROW_TILE, S)

    def fn(i, n, rv, hv, cv):
        bav, dbv, dgv = rv
        real = lax.broadcasted_iota(jnp.int32, (1, LANES), 1) < H
        _, vjp = jax.vjp(_gates, bav[:, :LANES], bav[:, LANES:], cv[0], cv[1])
        dbl, dal, dalog, ddt = vjp((jnp.where(real, dbv, 0.0), jnp.where(real, dgv, 0.0)))
        dbl, dal = jnp.where(real, dbl, 0.0), jnp.where(real, dal, 0.0)
        return (jnp.concatenate([dbl, dal], axis=1),), (jnp.where(real, dalog, 0.0), jnp.where(real, ddt, 0.0))

    return _rowwise(
        "gdn_gates_bwd", fn, S, ts, rows=[ba, dbeta, dg], consts=[alog, dt], outs=[(2 * LANES, BF16)],
        accs=[(1, LANES), (1, LANES)],
    )


def _split_bf16(a, n):
    parts, rest = [], a
    for _ in range(n):
        piece = rest.astype(BF16)
        parts.append(piece)
        rest = rest - piece.astype(F32)
    return parts


def _tri_dot(a, b, mode, tri):
    d = lambda u, v: lax.dot_general(u, v, _DIMS[mode], preferred_element_type=F32)
    if tri == 0:
        return sum(d(a.astype(BF16), piece) for piece in _split_bf16(b, 3))
    return sum(d(piece, b.astype(BF16)) for piece in _split_bf16(a, 3))


def _make_dot(exact):
    def raw(a, b, mode):
        if exact:
            a_hi, a_lo = _split_bf16(a, 2)
            b_hi, b_lo = _split_bf16(b, 2)
            d = lambda u, v: lax.dot_general(u, v, _DIMS[mode], preferred_element_type=F32)
            return d(a_hi, b_hi) + (d(a_hi, b_lo) + d(a_lo, b_hi))
        return lax.dot_general(a.astype(BF16), b.astype(BF16), _DIMS[mode], preferred_element_type=F32)

    @functools.partial(jax.custom_vjp, nondiff_argnums=(2,))
    def dot(a, b, mode):
        return raw(a, b, mode)

    def fwd(a, b, mode):
        return raw(a, b, mode), (a, b)

    def bwd(mode, res, ct):
        a, b = res
        if mode == "nn":
            return dot(ct, b, "nt"), dot(a, ct, "tn")
        if mode == "nt":
            return dot(ct, b, "nn"), dot(ct, a, "tn")
        return dot(b, ct, "nt"), dot(a, ct, "nn")

    dot.defvjp(fwd, bwd)
    return dot


_bdot = _make_dot(False)
_fdot = _make_dot(True)


@jax.custom_vjp
def _unit_lower_inverse(a_strict):
    return _unit_lower_inverse_raw(a_strict)


def _unit_lower_inverse_fwd(a_strict):
    t = _unit_lower_inverse_raw(a_strict)
    return t, t


def _unit_lower_inverse_bwd(t, ct):
    return (-_bdot(_bdot(t, ct, "tn"), t, "nt"),)


_unit_lower_inverse.defvjp(_unit_lower_inverse_fwd, _unit_lower_inverse_bwd)


def _unit_lower_inverse_raw(a_strict):
    C = a_strict.shape[0]
    ii = lax.broadcasted_iota(jnp.int32, (C, C), 0)
    jj = lax.broadcasted_iota(jnp.int32, (C, C), 1)
    eye = (ii == jj).astype(F32)
    blk = 16
    same = (ii // blk) == (jj // blk)
    xm = -jnp.where(same, a_strict, 0.0)
    t = eye + xm
    p = xm
    for _ in range(3):
        p = _fdot(p, p, "nn")
        t = t + _fdot(t, p, "nn")
    while blk < C:
        same2 = (ii // (2 * blk)) == (jj // (2 * blk))
        e = jnp.where(jnp.logical_and(same2, jnp.logical_not(same)), a_strict, 0.0)
        t = t - _fdot(_fdot(t, e, "nn"), t, "nn")
        same, blk = same2, 2 * blk
    return t


def _head_chunk(q, k, v, gc_col, gc_row, b_col, s0):
    C = q.shape[0]
    ii = lax.broadcasted_iota(jnp.int32, (C, C), 0)
    jj = lax.broadcasted_iota(jnp.int32, (C, C), 1)
    decay = jnp.where(ii >= jj, jnp.exp(jnp.minimum(gc_col - gc_row, 0.0)), 0.0)
    kb = k * b_col
    a = _bdot(kb, k, "nt") * decay
    t = _unit_lower_inverse(jnp.where(ii > jj, a, 0.0))
    eg = jnp.exp(gc_col)
    u = _bdot(t, v * b_col, "nn")
    w = _bdot(t, kb * eg, "nn")
    qk = _bdot(q, k, "nt") * decay
    rows = lax.broadcasted_iota(jnp.int32, (C, 1), 0)
    g_last = jnp.sum(jnp.where(rows == C - 1, gc_col, 0.0), axis=0, keepdims=True)
    kd = k * jnp.exp(g_last - gc_col)
    v_new = u - _bdot(w, s0, "nn")
    o = _bdot(q * eg, s0, "nn") + _bdot(qk, v_new, "nn")
    s1 = s0 * jnp.exp(g_last) + _bdot(kd, v_new, "tn")
    return o, s1


def _pick_lane(a, h):
    lanes = lax.broadcasted_iota(jnp.int32, a.shape, 1)
    return jnp.sum(jnp.where(lanes == h, a, 0.0), axis=1, keepdims=True)


def _pick_row(a, h):
    rows = lax.broadcasted_iota(jnp.int32, a.shape, 0)
    return jnp.sum(jnp.where(rows == h, a, 0.0), axis=0, keepdims=True)


def _tri(C):
    ii = lax.broadcasted_iota(jnp.int32, (C, C), 0)
    jj = lax.broadcasted_iota(jnp.int32, (C, C), 1)
    return (ii >= jj).astype(F32)


def _delta_fwd(qkvn, g_pad, g_rows, beta_pad):
    S, W = qkvn.shape
    D = W // 3
    H = D // HEAD_DIM
    C = min(CHUNK, S)
    N = S // C

    def body(x_ref, gp_ref, gr_ref, bp_ref, o_ref, sall_ref, st):
        n = pl.program_id(0)

        @pl.when(n == 0)
        def _():
            st[...] = jnp.zeros_like(st)

        low = _tri(C)
        gc_cols = _tri_dot(low, gp_ref[...], "nn", 0)
        gc_rows = _tri_dot(gr_ref[...], low, "nt", 1)
        bcols = bp_ref[...]
        for h in range(H):
            sl = slice(h * HEAD_DIM, (h + 1) * HEAD_DIM)
            s0 = st[h]
            sall_ref[h] = s0
            o, s1 = _head_chunk(
                x_ref[:, sl], x_ref[:, D + h * HEAD_DIM : D + (h + 1) * HEAD_DIM], x_ref[:, 2 * D + h * HEAD_DIM : 2 * D + (h + 1) * HEAD_DIM],
                _pick_lane(gc_cols, h), _pick_row(gc_rows, h), _pick_lane(bcols, h), s0,
            )
            st[h] = s1
            o_ref[:, sl] = o

    return pl.pallas_call(
        body,
        name="gdn_delta_fwd",
        grid=(N,),
        in_specs=[
            pl.BlockSpec((C, W), lambda n: (n, 0)),
            pl.BlockSpec((C, LANES), lambda n: (n, 0)),
            pl.BlockSpec((None, 8, C), lambda n: (n, 0, 0)),
            pl.BlockSpec((C, LANES), lambda n: (n, 0)),
        ],
        out_specs=[pl.BlockSpec((C, D), lambda n: (n, 0)), pl.BlockSpec((None, H, HEAD_DIM, HEAD_DIM), lambda n: (n, 0, 0, 0))],
        out_shape=[jax.ShapeDtypeStruct((S, D), F32), jax.ShapeDtypeStruct((N, H, HEAD_DIM, HEAD_DIM), F32)],
        scratch_shapes=[pltpu.VMEM((H, HEAD_DIM, HEAD_DIM), F32)],
        compiler_params=_params(1),
    )(qkvn, g_pad, g_rows, beta_pad)


def _delta_bwd(qkvn, g_pad, g_rows, beta_pad, s_all, do):
    S, W = qkvn.shape
    D = W // 3
    H = D // HEAD_DIM
    C = min(CHUNK, S)
    N = S // C

    def body(x_ref, gp_ref, gr_ref, bp_ref, sall_ref, do_ref, dx_ref, dgp_ref, dgr_ref, dbp_ref, dst):
        n = pl.program_id(0)

        @pl.when(n == 0)
        def _():
            dst[...] = jnp.zeros_like(dst)

        low = _tri(C)
        gc_cols = _tri_dot(low, gp_ref[...], "nn", 0)
        gc_rows = _tri_dot(gr_ref[...], low, "nt", 1)
        bcols = bp_ref[...]
        lane = lax.broadcasted_iota(jnp.int32, (1, LANES), 1)
        row8 = lax.broadcasted_iota(jnp.int32, (8, 1), 0)
        dgc_cols = jnp.zeros((C, LANES), F32)
        dgc_rows = jnp.zeros((8, C), F32)
        dbcols = jnp.zeros((C, LANES), F32)
        for h in range(H):
            sl = slice(h * HEAD_DIM, (h + 1) * HEAD_DIM)
            slk = slice(D + h * HEAD_DIM, D + (h + 1) * HEAD_DIM)
            slv = slice(2 * D + h * HEAD_DIM, 2 * D + (h + 1) * HEAD_DIM)
            _, vjp = jax.vjp(
                _head_chunk, x_ref[:, sl], x_ref[:, slk], x_ref[:, slv],
                _pick_lane(gc_cols, h), _pick_row(gc_rows, h), _pick_lane(bcols, h), sall_ref[h],
            )
            dq, dk, dv, dgc, dgr, dbc, ds0 = vjp((do_ref[:, sl], dst[h]))
            dst[h] = ds0
            dx_ref[:, sl] = dq
            dx_ref[:, slk] = dk
            dx_ref[:, slv] = dv
            dgc_cols = dgc_cols + dgc * (lane == h).astype(F32)
            dgc_rows = dgc_rows + dgr * (row8 == h).astype(F32)
            dbcols = dbcols + dbc * (lane == h).astype(F32)
        dgp_ref[...] = _tri_dot(low, dgc_cols, "tn", 0)
        dgr_ref[...] = _tri_dot(dgc_rows, low, "nn", 1)
        dbp_ref[...] = dbcols

    rev = lambda n: N - 1 - n
    return pl.pallas_call(
        body,
        name="gdn_delta_bwd",
        grid=(N,),
        in_specs=[
            pl.BlockSpec((C, W), lambda n: (rev(n), 0)),
            pl.BlockSpec((C, LANES), lambda n: (rev(n), 0)),
            pl.BlockSpec((None, 8, C), lambda n: (rev(n), 0, 0)),
            pl.BlockSpec((C, LANES), lambda n: (rev(n), 0)),
            pl.BlockSpec((None, H, HEAD_DIM, HEAD_DIM), lambda n: (rev(n), 0, 0, 0)),
            pl.BlockSpec((C, D), lambda n: (rev(n), 0)),
        ],
        out_specs=[
            pl.BlockSpec((C, W), lambda n: (rev(n), 0)),
            pl.BlockSpec((C, LANES), lambda n: (rev(n), 0)),
            pl.BlockSpec((None, 8, C), lambda n: (rev(n), 0, 0)),
            pl.BlockSpec((C, LANES), lambda n: (rev(n), 0)),
        ],
        out_shape=[
            jax.ShapeDtypeStruct((S, W), F32),
            jax.ShapeDtypeStruct((S, LANES), F32),
            jax.ShapeDtypeStruct((N, 8, C), F32),
            jax.ShapeDtypeStruct((S, LANES), F32),
        ],
        scratch_shapes=[pltpu.VMEM((H, HEAD_DIM, HEAD_DIM), F32)],
        compiler_params=_params(1),
    )(qkvn, g_pad, g_rows, beta_pad, s_all, do)


def _gate_norm_head(o, z, nw):
    return o * lax.rsqrt(jnp.mean(o * o, axis=-1, keepdims=True) + RMS_EPS) * nw * _silu(z)


def _gate_norm_fwd(o, z, nw):
    S, D = o.shape
    H = D // HEAD_DIM
    ts = min(ROW_TILE, S)

    def fn(i, n, rv, hv, cv):
        ov, zv = rv
        parts = [_gate_norm_head(ov[:, h * HEAD_DIM : (h + 1) * HEAD_DIM], zv[:, h * HEAD_DIM : (h + 1) * HEAD_DIM], cv[0]) for h in range(H)]
        return (jnp.concatenate(parts, axis=1),), ()

    return _rowwise("gdn_gate_norm_fwd", fn, S, ts, rows=[o, z], consts=[nw], outs=[(D, BF16)])[0]


def _gate_norm_bwd(dog, o, z, nw):
    S, D = o.shape
    H = D // HEAD_DIM
    ts = min(ROW_TILE, S)

    def fn(i, n, rv, hv, cv):
        dv, ov, zv = rv
        dos, dzs, dnw = [], [], None
        for h in range(H):
            sl = slice(h * HEAD_DIM, (h + 1) * HEAD_DIM)
            _, vjp = jax.vjp(_gate_norm_head, ov[:, sl], zv[:, sl], cv[0])
            a, b_, c_ = vjp(dv[:, sl])
            dos.append(a)
            dzs.append(b_)
            dnw = c_ if dnw is None else dnw + c_
        return (jnp.concatenate(dos, axis=1), jnp.concatenate(dzs, axis=1)), (dnw,)

    return _rowwise("gdn_gate_norm_bwd", fn, S, ts, rows=[dog, o, z], consts=[nw], outs=[(D, F32), (D, BF16)], accs=[(1, HEAD_DIM)])


def _square_bf16(r):
    rf = r.astype(F32)
    return rf * rf


def _mlp_ple_fwd(li, xa, p, w1, w2, wg, bg, wp):
    r = _mm(f"l{li}_mlp_up", xa, w1, "nn", [BF16], epi=lambda acc: (jnp.maximum(acc, 0.0),))
    ff = _mm(f"l{li}_mlp_down", r, w2, "nn", [F32], a_fn=_square_bf16)
    gate = _mm(f"l{li}_ple_gate", xa, wg, "nn", [F32], epi=lambda acc, bias: (acc + bias,), extras=[(bg, "row")])
    pp = _mm(f"l{li}_ple_proj", p, wp, "nn", [F32])
    return r, ff, gate, pp


def _mlp_ple_bwd(li, dh, dhb, xa, p, r, gate, pp, w1, w2, wg):
    dpre = _mm(f"l{li}_mlp_down_bwd", dhb, w2, "nt", [BF16], epi=lambda acc, rr: (acc * (2.0 * rr.astype(F32)),), extras=[(r, "tile")])
    dw2 = _mm(f"l{li}_mlp_dw2", r, dhb, "tn", [F32], a_fn=_square_bf16)
    dgate, dpp, dbg = _ple_bwd(f"l{li}_ple_bwd", dh, gate, pp)
    dw1 = _mm(f"l{li}_mlp_dw1", xa, dpre, "tn", [F32])
    dwg = _mm(f"l{li}_ple_dwg", xa, dgate, "tn", [F32])
    dwp = _mm(f"l{li}_ple_dwp", p, dpp, "tn", [F32])
    t = _mm(f"l{li}_ple_gate_bwd", dgate, wg, "nt", [F32], epi=lambda acc, d: (acc + ALPHA * d,), extras=[(dh, "tile")])
    dxa = _mm(f"l{li}_mlp_up_bwd", dpre, w1, "nt", [F32], epi=lambda acc, d: (acc + d,), extras=[(t, "tile")])
    return dxa, dw1, dw2, dwg, dbg, dwp


def _local_step(x, p, tgt, W):
    S, D = x.shape
    H = D // HEAD_DIM
    C = min(CHUNK, S)
    N = S // C
    lg = lambda i, j: W["ln_gain"][2 * i + j][None, :]
    lb = lambda i, j: W["ln_bias"][2 * i + j][None, :]
    G = {}

    pooled, xh0a, rs0a, x0a = _pool_fwd(x, W["pool_w"], W["pool_b"], W["pool_scale"], lg(0, 0), lb(0, 0))
    r0, ff0, gate0, pp0 = _mlp_ple_fwd(0, x0a, p[0], W["mlp_w1"][0], W["mlp_w2"][0], W["ple_gate_w"][0], W["ple_gate_b"][0:1], W["ple_proj"][0])
    xh0b, rs0b, x0b = _res_ln_ffpe("l0_ln_b", xh0a, ff0, gate0, pp0, lg(0, 0), lb(0, 0), lg(0, 1), lb(0, 1))

    qkv_pre = _mm("gdn_in_qkv", x0b, W["gdn_wqkv"], "nn", [F32])
    z = _mm("gdn_in_z", x0b, W["gdn_wz"], "nn", [F32])
    ba = _mm("gdn_in_ba", x0b, W["gdn_wba"], "nn", [F32])
    qkvn = _conv_fwd(qkv_pre, W["gdn_conv"])
    beta_pad, g_pad = _gates_fwd(ba, W["gdn_a_log"], W["gdn_dt_bias"])
    g_rows = g_pad[:, :8].reshape(N, C, 8).transpose(0, 2, 1)
    o, s_all = _delta_fwd(qkvn, g_pad, g_rows, beta_pad)
    og = _gate_norm_fwd(o, z, W["gdn_norm_w"])
    mix1 = _mm("gdn_out", og, W["gdn_w_out"], "nn", [F32])
    xh1a, rs1a, x1a = _res_ln_mix("l1_ln_a", xh0b, mix1, lg(0, 1), lb(0, 1), lg(1, 0), lb(1, 0))
    r1, ff1, gate1, pp1 = _mlp_ple_fwd(1, x1a, p[1], W["mlp_w1"][1], W["mlp_w2"][1], W["ple_gate_w"][1], W["ple_gate_b"][1:2], W["ple_proj"][1])
    dh1b, dh1b_b, loss_cols, dg11, db11 = _final_ln_loss(xh1a, ff1, gate1, pp1, tgt, lg(1, 0), lb(1, 0), lg(1, 1), lb(1, 1))

    dx1a, dw1_1, dw2_1, dwg_1, dbg_1, dwp_1 = _mlp_ple_bwd(1, dh1b, dh1b_b, x1a, p[1], r1, gate1, pp1, W["mlp_w1"][1], W["mlp_w2"][1], W["ple_gate_w"][1])
    dh1a, dh1a_b, dg10, db10 = _ln_bwd_call("l1_ln_a_bwd", dx1a, xh1a, rs1a, lg(1, 0))
    dog = _mm("gdn_out_bwd", dh1a_b, W["gdn_w_out"], "nt", [F32])
    G["gdn_w_out"] = _mm("gdn_dw_out", og, dh1a_b, "tn", [F32])
    do, dz, dnw = _gate_norm_bwd(dog, o, z, W["gdn_norm_w"])
    dqkvn, dg_col, dg_row, dbeta = _delta_bwd(qkvn, g_pad, g_rows, beta_pad, s_all, do)
    dg_all = dg_col + jnp.pad(dg_row.transpose(0, 2, 1).reshape(S, 8), ((0, 0), (0, LANES - 8)))
    dba, dalog, ddt = _gates_bwd(ba, W["gdn_a_log"], W["gdn_dt_bias"], dbeta, dg_all, H)
    dqkv, dconv = _conv_bwd(qkv_pre, W["gdn_conv"], dqkvn)
    dwqkv = _mm("gdn_dwqkv", x0b, dqkv, "tn", [F32])
    dwz = _mm("gdn_dwz", x0b, dz, "tn", [F32])
    dwba = _mm("gdn_dwba", x0b, dba, "tn", [F32])
    G["gdn_w_in"] = jnp.concatenate([dwqkv, dwz, dwba[:, :H], dwba[:, LANES : LANES + H]], axis=1)
    t = _mm("gdn_in_ba_bwd", dba, W["gdn_wba"], "nt", [F32], epi=lambda acc, d: (acc + ALPHA * d,), extras=[(dh1a, "tile")])
    t = _mm("gdn_in_z_bwd", dz, W["gdn_wz"], "nt", [F32], epi=lambda acc, d: (acc + d,), extras=[(t, "tile")])
    dx0b = _mm("gdn_in_qkv_bwd", dqkv, W["gdn_wqkv"], "nt", [F32], epi=lambda acc, d: (acc + d,), extras=[(t, "tile")])

    dh0b, dh0b_b, dg01, db01 = _ln_bwd_call("l0_ln_b_bwd", dx0b, xh0b, rs0b, lg(0, 1))
    dx0a, dw1_0, dw2_0, dwg_0, dbg_0, dwp_0 = _mlp_ple_bwd(0, dh0b, dh0b_b, x0a, p[0], r0, gate0, pp0, W["mlp_w1"][0], W["mlp_w2"][0], W["ple_gate_w"][0])
    dh0a, _, dg00, db00 = _ln_bwd_call("l0_ln_a_bwd", dx0a, xh0a, rs0a, lg(0, 0))
    grad_x, dyp, dscale, dpb = _pool_bwd(dh0a, pooled, W["pool_w"], W["pool_b"], W["pool_scale"])
    G["pool_w"] = _pool_dw(pooled, dyp)

    G["ln_gain"] = jnp.concatenate([dg00, dg01, dg10, dg11], axis=0)
    G["ln_bias"] = jnp.concatenate([db00, db01, db10, db11], axis=0)
    G["pool_b"] = dpb
    G["pool_scale"] = dscale
    G["gdn_conv"] = dconv
    G["gdn_a_log"] = dalog[:, :H]
    G["gdn_dt_bias"] = ddt[:, :H]
    G["gdn_norm_w"] = dnw
    G["mlp_w1"] = jnp.stack([dw1_0, dw1_1])
    G["mlp_w2"] = jnp.stack([dw2_0, dw2_1])
    G["ple_gate_w"] = jnp.stack([dwg_0, dwg_1])
    G["ple_gate_b"] = jnp.concatenate([dbg_0, dbg_1], axis=0)
    G["ple_proj"] = jnp.stack([dwp_0, dwp_1])
    return loss_cols, grad_x, G


_HBM = pl.BlockSpec(memory_space=pltpu.HBM)


def _all_gather(name, shards):
    T = len(shards)

    def body(*refs):
        ins, outs = refs[:T], refs[T : 2 * T]
        send_sems, recv_sems, local_sems = refs[2 * T :]
        x, y, c = lax.axis_index("x"), lax.axis_index("y"), lax.axis_index("c")
        me, sibling = (x, y, c), (x, y, 1 - c)
        chips = [(1 - x, y), (x, 1 - y), (1 - x, 1 - y)]

        def blk(t, px, py, pc):
            return outs[t].at[4 * px + 2 * py + pc]

        def copy(t, k, block, to, src=None):
            return pltpu.make_async_remote_copy(
                src_ref=blk(t, *block) if src is None else src, dst_ref=blk(t, *block),
                send_sem=send_sems.at[t, k], recv_sem=recv_sems.at[t, k], device_id=to, device_id_type=MESH,
            )

        mine = [pltpu.make_async_copy(ins[t], blk(t, *me), local_sems.at[t]) for t in range(T)]
        for cp in mine:
            cp.start()
        first = []
        for t in range(T):
            first.append(copy(t, 0, me, sibling, src=ins[t]))
            first += [copy(t, 1 + j, me, (*chip, c), src=ins[t]) for j, chip in enumerate(chips)]
        for cp in first:
            cp.start()
        passed = []
        for j, chip in enumerate(chips):
            for t in range(T):
                copy(t, 1 + j, (*chip, c), me).wait_recv()
                fw = copy(t, 4 + j, (*chip, c), sibling)
                fw.start()
                passed.append(fw)
        for t in range(T):
            copy(t, 0, sibling, me).wait_recv()
            for j, chip in enumerate(chips):
                copy(t, 4 + j, (*chip, 1 - c), me).wait_recv()
        for cp in first + passed:
            cp.wait_send()
        for cp in mine:
            cp.wait()

    return pl.pallas_call(
        body,
        name=name,
        in_specs=[_HBM] * T,
        out_specs=[_HBM] * T,
        out_shape=[jax.ShapeDtypeStruct((N_DEV,) + s.shape, s.dtype) for s in shards],
        scratch_shapes=[pltpu.SemaphoreType.DMA((T, 7)), pltpu.SemaphoreType.DMA((T, 7)), pltpu.SemaphoreType.DMA((T,))],
    )(*shards)


def _exchange(name, blocks):
    def body(g_ref, o_ref, send_sems, recv_sems, local_sem):
        x, y, c = lax.axis_index("x"), lax.axis_index("y"), lax.axis_index("c")
        own = pltpu.make_async_copy(g_ref.at[4 * x + 2 * y + c], o_ref.at[N_DEV - 1], local_sem)
        own.start()
        copies = []
        for rel in range(1, N_DEV):
            px = 1 - x if rel & 4 else x
            py = 1 - y if rel & 2 else y
            pc = 1 - c if rel & 1 else c
            copies.append(
                pltpu.make_async_remote_copy(
                    src_ref=g_ref.at[4 * px + 2 * py + pc], dst_ref=o_ref.at[rel - 1],
                    send_sem=send_sems.at[rel - 1], recv_sem=recv_sems.at[rel - 1], device_id=(px, py, pc), device_id_type=MESH,
                )
            )
        for cp in copies:
            cp.start()
        for cp in copies:
            cp.wait_recv()
        for cp in copies:
            cp.wait_send()
        own.wait()

    return pl.pallas_call(
        body,
        name=name,
        in_specs=[_HBM],
        out_specs=_HBM,
        out_shape=jax.ShapeDtypeStruct(blocks.shape, blocks.dtype),
        scratch_shapes=[pltpu.SemaphoreType.DMA((N_DEV - 1,)), pltpu.SemaphoreType.DMA((N_DEV - 1,)), pltpu.SemaphoreType.DMA],
    )(blocks)


def _sum_blocks(name, parts, tr):
    _, R, Cw = parts.shape
    tr = min(tr, R)
    assert R % tr == 0

    def body(p_ref, o_ref):
        acc = p_ref[0].astype(F32)
        for d in range(1, N_DEV):
            acc = acc + p_ref[d].astype(F32)
        o_ref[...] = acc

    return pl.pallas_call(
        body,
        name=name,
        grid=(R // tr,),
        in_specs=[pl.BlockSpec((N_DEV, tr, Cw), lambda i: (0, i, 0))],
        out_specs=pl.BlockSpec((tr, Cw), lambda i: (i, 0)),
        out_shape=jax.ShapeDtypeStruct((R, Cw), F32),
        compiler_params=_params(1),
    )(parts)


def _adamw(name, w, g, m, v):
    shape = w.shape
    cols = shape[-1]
    rows = w.size // cols
    tr = rows if rows <= 512 else 512
    assert rows % tr == 0
    w2, g2, m2, v2 = (a.reshape(rows, cols) for a in (w, g, m, v))

    def body(w_ref, g_ref, m_ref, v_ref, d_ref, mo_ref, vo_ref):
        gv = g_ref[...]
        mn = ADAM_B1 * m_ref[...] + (1.0 - ADAM_B1) * gv
        vn = ADAM_B2 * v_ref[...] + (1.0 - ADAM_B2) * jnp.square(gv)
        m_hat = mn / (1.0 - ADAM_B1**ADAM_STEP)
        v_hat = vn / (1.0 - ADAM_B2**ADAM_STEP)
        d_ref[...] = -ADAM_LR * (m_hat / (jnp.sqrt(v_hat) + ADAM_EPS) + ADAM_WD * w_ref[...])
        mo_ref[...] = mn
        vo_ref[...] = vn

    spec = pl.BlockSpec((tr, cols), lambda i: (i, 0))
    d, mn, vn = pl.pallas_call(
        body,
        name=name,
        grid=(rows // tr,),
        in_specs=[spec] * 4,
        out_specs=[spec] * 3,
        out_shape=[jax.ShapeDtypeStruct((rows, cols), F32)] * 3,
        compiler_params=_params(1),
    )(w2, g2, m2, v2)
    return d.reshape(shape), mn.reshape(shape), vn.reshape(shape)


BIG = ("gdn_w_in", "gdn_w_out", "mlp_w1", "mlp_w2", "ple_gate_w", "ple_proj", "pool_w")
SMALL_SHARDED = ("ln_gain", "ln_bias", "pool_b", "gdn_conv")
SMALL_REPLICATED = ("pool_scale", "gdn_a_log", "gdn_dt_bias", "gdn_norm_w", "ple_gate_b")
WEIGHTS = ("ln_gain", "ln_bias", "pool_w", "pool_b", "pool_scale", "gdn_w_in", "gdn_conv", "gdn_a_log", "gdn_dt_bias",
           "gdn_norm_w", "gdn_w_out", "mlp_w1", "mlp_w2", "ple_gate_w", "ple_gate_b", "ple_proj")
BIG_AXIS = {"gdn_w_in": 1, "gdn_w_out": 0, "mlp_w1": 2, "mlp_w2": 1, "ple_gate_w": 1, "ple_proj": 2, "pool_w": 1}
PACK_ROWS_ALIGN = 512
PACK_PART_ALIGN = 16


def _squeeze_big(name, a):
    return a[0] if name in ("gdn_w_in", "gdn_w_out", "pool_w") else a


def _part_rows(a, width):
    rows = a.size // width
    return rows + (-rows) % PACK_PART_ALIGN


def _pack_rows(parts, width, dtype):
    padded = []
    for a in parts:
        a2 = a.reshape(-1, width).astype(dtype)
        padded.append(jnp.pad(a2, ((0, _part_rows(a, width) - a2.shape[0]), (0, 0))))
    flat = jnp.concatenate(padded, axis=0)
    return jnp.pad(flat, ((0, (-flat.shape[0]) % PACK_ROWS_ALIGN), (0, 0)))


def _split_blocks(name, full):
    ax = BIG_AXIS[name]
    shp = full.shape
    a = full.reshape(shp[:ax] + (N_DEV, shp[ax] // N_DEV) + shp[ax + 1 :])
    return jnp.moveaxis(a, ax, 0)


def _join_blocks(name, blocks):
    ax = BIG_AXIS[name]
    a = jnp.moveaxis(blocks, 0, ax)
    shp = a.shape
    return a.reshape(shp[:ax] + (shp[ax] * shp[ax + 1],) + shp[ax + 2 :])


def _pack_small(parts):
    flat = jnp.concatenate([jnp.pad(a.reshape(-1), (0, (-a.size) % LANES)) for a in parts])
    rows = flat.size // LANES
    return jnp.pad(flat.reshape(rows, LANES), ((0, (-rows) % 8), (0, 0)))


def _unpack_small(packed, shapes):
    flat = packed.reshape(packed.shape[:-2] + (-1,))
    out, off = [], 0
    for shp in shapes:
        size = 1
        for s in shp:
            size *= s
        out.append(flat[..., off : off + size].reshape(flat.shape[:-1] + tuple(shp)))
        off += size + (-size) % LANES
    return out


def kernel(x, p, ln_gain, ln_bias, pool_w, pool_b, pool_scale, gdn_w_in, gdn_conv, gdn_a_log, gdn_dt_bias, gdn_norm_w, gdn_w_out, mlp_w1, mlp_w2, ple_gate_w, ple_gate_b, ple_proj, loss_target, m_ln_gain, m_ln_bias, m_pool_w, m_pool_b, m_pool_scale, m_gdn_w_in, m_gdn_conv, m_gdn_a_log, m_gdn_dt_bias, m_gdn_norm_w, m_gdn_w_out, m_mlp_w1, m_mlp_w2, m_ple_gate_w, m_ple_gate_b, m_ple_proj, v_ln_gain, v_ln_bias, v_pool_w, v_pool_b, v_pool_scale, v_gdn_w_in, v_gdn_conv, v_gdn_a_log, v_gdn_dt_bias, v_gdn_norm_w, v_gdn_w_out, v_mlp_w1, v_mlp_w2, v_ple_gate_w, v_ple_gate_b, v_ple_proj):
    w_sh = dict(ln_gain=ln_gain, ln_bias=ln_bias, pool_w=pool_w, pool_b=pool_b, pool_scale=pool_scale, gdn_w_in=gdn_w_in,
                gdn_conv=gdn_conv, gdn_a_log=gdn_a_log, gdn_dt_bias=gdn_dt_bias, gdn_norm_w=gdn_norm_w, gdn_w_out=gdn_w_out,
                mlp_w1=mlp_w1, mlp_w2=mlp_w2, ple_gate_w=ple_gate_w, ple_gate_b=ple_gate_b, ple_proj=ple_proj)
    m_sh = dict(ln_gain=m_ln_gain, ln_bias=m_ln_bias, pool_w=m_pool_w, pool_b=m_pool_b, pool_scale=m_pool_scale, gdn_w_in=m_gdn_w_in,
                gdn_conv=m_gdn_conv, gdn_a_log=m_gdn_a_log, gdn_dt_bias=m_gdn_dt_bias, gdn_norm_w=m_gdn_norm_w, gdn_w_out=m_gdn_w_out,
                mlp_w1=m_mlp_w1, mlp_w2=m_mlp_w2, ple_gate_w=m_ple_gate_w, ple_gate_b=m_ple_gate_b, ple_proj=m_ple_proj)
    v_sh = dict(ln_gain=v_ln_gain, ln_bias=v_ln_bias, pool_w=v_pool_w, pool_b=v_pool_b, pool_scale=v_pool_scale, gdn_w_in=v_gdn_w_in,
                gdn_conv=v_gdn_conv, gdn_a_log=v_gdn_a_log, gdn_dt_bias=v_gdn_dt_bias, gdn_norm_w=v_gdn_norm_w, gdn_w_out=v_gdn_w_out,
                mlp_w1=v_mlp_w1, mlp_w2=v_mlp_w2, ple_gate_w=v_ple_gate_w, ple_gate_b=v_ple_gate_b, ple_proj=v_ple_proj)
    xs, tg = x[0], loss_target[0]
    ps = p[:, 0]
    S, D = xs.shape
    H = D // HEAD_DIM
    me = 4 * lax.axis_index("x") + 2 * lax.axis_index("y") + lax.axis_index("c")

    big_sh = {n: _squeeze_big(n, w_sh[n]) for n in BIG}
    packed = _pack_rows([big_sh[n] for n in BIG], D, BF16)
    small_packed = _pack_small([w_sh[n] for n in SMALL_SHARDED])
    gathered, small_gathered = _all_gather("gather_weights", [packed, small_packed])
    W, off = {}, 0
    for n in BIG:
        rows = big_sh[n].size // D
        W[n] = _join_blocks(n, gathered[:, off : off + rows].reshape((N_DEV,) + big_sh[n].shape))
        off += _part_rows(big_sh[n], D)
    smalls = _unpack_small(small_gathered, [w_sh[n].shape for n in SMALL_SHARDED])
    for n, a in zip(SMALL_SHARDED, smalls):
        W[n] = jnp.moveaxis(a, 0, -2).reshape(a.shape[1:-1] + (N_DEV * a.shape[-1],))
    W["ln_gain"] = W["ln_gain"].reshape(2 * DEPTH, D)
    W["ln_bias"] = W["ln_bias"].reshape(2 * DEPTH, D)
    W["pool_b"] = W["pool_b"].reshape(1, D)
    W["gdn_conv"] = W["gdn_conv"][0]
    W["pool_scale"] = pool_scale
    W["ple_gate_b"] = ple_gate_b
    W["gdn_norm_w"] = gdn_norm_w
    W["gdn_a_log"] = jnp.pad(gdn_a_log, ((0, 0), (0, LANES - H)))
    W["gdn_dt_bias"] = jnp.pad(gdn_dt_bias, ((0, 0), (0, LANES - H)))
    w_in = W.pop("gdn_w_in")
    W["gdn_wqkv"], W["gdn_wz"] = w_in[:, : 3 * D], w_in[:, 3 * D : 4 * D]
    W["gdn_wba"] = jnp.concatenate(
        [jnp.pad(w_in[:, 4 * D : 4 * D + H], ((0, 0), (0, LANES - H))), jnp.pad(w_in[:, 4 * D + H :], ((0, 0), (0, LANES - H)))], axis=1
    )

    loss_cols, grad_x, G = _local_step(xs, ps, tg, W)
    loss = lax.psum(0.5 * jnp.sum(loss_cols) / D, MESH_AXES)
    g_packed = jnp.stack([_pack_rows([_split_blocks(n, G[n])[d] for n in BIG], D, BF16) for d in range(N_DEV)])
    g_recv = _exchange("exchange_grads", g_packed)
    g_sum = _sum_blocks("sum_grads", g_recv, 512)
    grads, off = {}, 0
    for n in BIG:
        rows = big_sh[n].size // D
        grads[n] = g_sum[off : off + rows].reshape(w_sh[n].shape)
        off += _part_rows(big_sh[n], D)
    small_names = SMALL_SHARDED + SMALL_REPLICATED
    small_full_shapes = {n: G[n].shape for n in small_names}
    gs_packed = _pack_small([G[n] for n in small_names])
    (gs_all,) = _all_gather("gather_small_grads", [gs_packed])
    gs_sum = _sum_blocks("sum_small_grads", gs_all, 512)
    for n, a in zip(small_names, _unpack_small(gs_sum, [small_full_shapes[n] for n in small_names])):
        if n in SMALL_SHARDED:
            width = w_sh[n].shape[-1]
            a = a.reshape(w_sh[n].shape[:-1] + (N_DEV * width,))
            a = lax.dynamic_slice_in_dim(a, me * width, width, axis=a.ndim - 1)
        grads[n] = a.reshape(w_sh[n].shape)

    deltas, new_m, new_v = {}, {}, {}
    for n in WEIGHTS:
        deltas[n], new_m[n], new_v[n] = _adamw(f"adamw_{n}", w_sh[n], grads[n], m_sh[n], v_sh[n])
    return (loss, grad_x[None], *[grads[n] for n in WEIGHTS], *[deltas[n] for n in WEIGHTS],
            *[new_m[n] for n in WEIGHTS], *[new_v[n] for n in WEIGHTS])
```

```python
import functools

import jax
import jax.numpy as jnp
from jax import lax
from jax.experimental import pallas as pl
from jax.experimental.pallas import tpu as pltpu

F32 = jnp.float32
BF16 = jnp.bfloat16
MESH_AXES = ("x", "y", "c")
N_DEV = 8
MESH = pl.DeviceIdType.MESH

DEPTH = 2
ALPHA = (2.0 * DEPTH) ** 0.25
LN_EPS = 1e-5
RMS_EPS = 1e-6
L2_EPS = 1e-6
HEAD_DIM = 128
CONV_WIDTH = 4
POOL_WINDOWS = (2, 4, 8, 16)
POOL_HALO = 16
CONV_HALO = 8
LANES = 128
ADAM_LR = 0.001
ADAM_B1 = 0.9
ADAM_B2 = 0.999
ADAM_EPS = 1e-08
ADAM_WD = 0.01
ADAM_STEP = 10

VMEM_LIMIT = 56 * 1024 * 1024
ROW_TILE = 256
CONV_TILE = 256
CHUNK = 128
MM_TM, MM_TN, MM_TK = 512, 1024, 1024

_DIMS = {
    "nn": (((1,), (0,)), ((), ())),
    "nt": (((1,), (1,)), ((), ())),
    "tn": (((0,), (0,)), ((), ())),
}


def _params(n_axes):
    return pltpu.CompilerParams(dimension_semantics=("arbitrary",) * n_axes, vmem_limit_bytes=VMEM_LIMIT)


def _mm(name, a, b, mode, out_dtypes, epi=None, extras=(), a_fn=None, tm=None, tn=None, tk=None, b_outer=False):
    if mode == "tn":
        K, M = a.shape
    else:
        M, K = a.shape
    N = b.shape[0] if mode == "nt" else b.shape[1]
    tm, tn, tk = min(tm or MM_TM, M), min(tn or MM_TN, N), min(tk or MM_TK, K)
    assert M % tm == 0 and N % tn == 0 and K % tk == 0, (name, M, N, K)
    nk = K // tk

    def at(f):
        return (lambda j, i, k: f(i, j, k)) if b_outer else f

    a_spec = pl.BlockSpec((tk, tm), at(lambda i, j, k: (k, i))) if mode == "tn" else pl.BlockSpec((tm, tk), at(lambda i, j, k: (i, k)))
    b_spec = pl.BlockSpec((tn, tk), at(lambda i, j, k: (j, k))) if mode == "nt" else pl.BlockSpec((tk, tn), at(lambda i, j, k: (k, j)))
    ex_specs = [
        pl.BlockSpec((tm, tn), at(lambda i, j, k: (i, j))) if kind == "tile" else pl.BlockSpec((1, tn), at(lambda i, j, k: (0, j)))
        for _, kind in extras
    ]
    n_ex, n_out = len(extras), len(out_dtypes)

    def body(*refs):
        a_ref, b_ref = refs[0], refs[1]
        ex_refs = refs[2 : 2 + n_ex]
        out_refs = refs[2 + n_ex : 2 + n_ex + n_out]
        av = a_ref[...]
        if a_fn is not None:
            av = a_fn(av)
        part = lax.dot_general(av.astype(BF16), b_ref[...].astype(BF16), _DIMS[mode], preferred_element_type=F32)

        def finish(res):
            vals = epi(res, *[e[...] for e in ex_refs]) if epi is not None else (res,)
            for o_ref, v in zip(out_refs, vals):
                o_ref[...] = v.astype(o_ref.dtype)

        if nk == 1:
            finish(part)
        else:
            acc = refs[-1]
            k = pl.program_id(2)

            @pl.when(k == 0)
            def _():
                acc[...] = part

            @pl.when(k > 0)
            def _():
                acc[...] += part

            @pl.when(k == nk - 1)
            def _():
                finish(acc[...])

    outs = pl.pallas_call(
        body,
        name=name,
        grid=(N // tn, M // tm, nk) if b_outer else (M // tm, N // tn, nk),
        in_specs=[a_spec, b_spec] + ex_specs,
        out_specs=[pl.BlockSpec((tm, tn), at(lambda i, j, k: (i, j))) for _ in out_dtypes],
        out_shape=[jax.ShapeDtypeStruct((M, N), dt) for dt in out_dtypes],
        scratch_shapes=[pltpu.VMEM((tm, tn), F32)] if nk > 1 else [],
        compiler_params=_params(3),
    )(a, b, *[e for e, _ in extras])
    return outs[0] if n_out == 1 else outs


def _rowwise(name, fn, S, ts, rows=(), halos=(), consts=(), outs=(), accs=()):
    ts = min(ts, S)
    assert S % ts == 0
    n = S // ts
    in_specs = [pl.BlockSpec((ts, a.shape[1]), lambda i: (i, 0)) for a in rows]
    for a, kind, hr in halos:
        r, nb = ts // hr, S // hr
        if kind == "prev":
            in_specs.append(pl.BlockSpec((hr, a.shape[1]), lambda i, r=r: (jnp.maximum(i * r - 1, 0), 0)))
        else:
            in_specs.append(pl.BlockSpec((hr, a.shape[1]), lambda i, r=r, nb=nb: (jnp.minimum((i + 1) * r, nb - 1), 0)))
    in_specs += [pl.BlockSpec(a.shape, lambda i, nd=a.ndim: (0,) * nd) for a in consts]
    out_specs = [pl.BlockSpec((ts, w), lambda i: (i, 0)) for w, _ in outs]
    out_specs += [pl.BlockSpec((r, w), lambda i: (0, 0)) for r, w in accs]
    out_shape = [jax.ShapeDtypeStruct((S, w), dt) for w, dt in outs]
    out_shape += [jax.ShapeDtypeStruct((r, w), F32) for r, w in accs]
    nr, nh, nc, no = len(rows), len(halos), len(consts), len(outs)

    def body(*refs):
        i = pl.program_id(0)
        rv = [r[...] for r in refs[:nr]]
        hv = [r[...] for r in refs[nr : nr + nh]]
        cv = [r[...] for r in refs[nr + nh : nr + nh + nc]]
        o_refs = refs[nr + nh + nc : nr + nh + nc + no]
        a_refs = refs[nr + nh + nc + no :]
        ovals, avals = fn(i, n, rv, hv, cv)
        for o_ref, v in zip(o_refs, ovals):
            o_ref[...] = v.astype(o_ref.dtype)
        for a_ref, v in zip(a_refs, avals):

            @pl.when(i == 0)
            def _(a_ref=a_ref, v=v):
                a_ref[...] = v

            @pl.when(i > 0)
            def _(a_ref=a_ref, v=v):
                a_ref[...] += v

    res = pl.pallas_call(
        body,
        name=name,
        grid=(n,),
        in_specs=in_specs,
        out_specs=out_specs,
        out_shape=out_shape,
        compiler_params=_params(1),
    )(*rows, *[h[0] for h in halos], *consts)
    return list(res)


def _ln(h, g, b):
    mu = jnp.mean(h, axis=-1, keepdims=True)
    d = h - mu
    var = jnp.mean(d * d, axis=-1, keepdims=True)
    rstd = lax.rsqrt(var + LN_EPS)
    xhat = d * rstd
    return xhat, rstd, xhat * g + b


def _ln_bwd(dy, xhat, rstd, g):
    dxh = dy * g
    m1 = jnp.mean(dxh, axis=-1, keepdims=True)
    m2 = jnp.mean(dxh * xhat, axis=-1, keepdims=True)
    dh = rstd * (dxh - m1 - xhat * m2)
    return dh, jnp.sum(dy * xhat, axis=0, keepdims=True), jnp.sum(dy, axis=0, keepdims=True)


def _wide(col, ts):
    return jnp.broadcast_to(col, (ts, LANES))


def _pool_fwd(x, wp, pb, ps, g, b):
    S, D = x.shape
    gw = D // len(POOL_WINDOWS)
    ts = min(ROW_TILE, S)

    def fn(i, n, rv, hv, cv):
        (xc,), (xp,) = rv, hv
        wpv, pbv, psv, gv, bv = cv
        xp = jnp.where(i > 0, xp, 0.0)
        xx = jnp.concatenate([xp, xc], axis=0)
        t = i * ts + lax.broadcasted_iota(jnp.int32, (ts, 1), 0)
        pooled, ys = [], []
        for gi, w in enumerate(POOL_WINDOWS):
            s = xx[:, gi * gw : (gi + 1) * gw]
            k = 1
            while k < w:
                s = s + pltpu.roll(s, k, axis=0)
                k *= 2
            cnt = jnp.minimum(t + 1, w).astype(F32)
            pg = (s[POOL_HALO:, :] / cnt - xc[:, gi * gw : (gi + 1) * gw]).astype(BF16)
            pooled.append(pg)
            ys.append(jnp.dot(pg, wpv[gi], preferred_element_type=F32))
        y = jnp.concatenate(ys, axis=1)
        h = ALPHA * xc + (y + pbv) * psv
        xhat, rstd, xa = _ln(h, gv, bv)
        return (jnp.concatenate(pooled, axis=1), xhat, _wide(rstd, ts), xa), ()

    return _rowwise(
        "pool_fwd", fn, S, ts, rows=[x], halos=[(x, "prev", POOL_HALO)], consts=[wp, pb, ps, g, b],
        outs=[(D, BF16), (D, F32), (LANES, F32), (D, BF16)],
    )


def _pool_bwd(dh, pooled, wp, pb, ps):
    S, D = dh.shape
    gw = D // len(POOL_WINDOWS)
    ts = min(ROW_TILE, S)
    te = ts + POOL_HALO

    def fn(i, n, rv, hv, cv):
        (dhc, pc), (dhn,) = rv, hv
        wpv, pbv, psv = cv
        dhn = jnp.where(i < n - 1, dhn, 0.0)
        dy_ext = jnp.concatenate([dhc, dhn], axis=0) * psv
        dyb = dy_ext.astype(BF16)
        t = i * ts + lax.broadcasted_iota(jnp.int32, (te, 1), 0)
        dxs, ys = [], []
        for gi, w in enumerate(POOL_WINDOWS):
            sl = slice(gi * gw, (gi + 1) * gw)
            dp = lax.dot_general(dyb[:, sl], wpv[gi], _DIMS["nt"], preferred_element_type=F32)
            s = dp / jnp.minimum(t + 1, w).astype(F32)
            k = 1
            while k < w:
                s = s + pltpu.roll(s, k, axis=0)
                k *= 2
            s = pltpu.roll(s, POOL_HALO - (w - 1), axis=0)
            dxs.append(s[POOL_HALO:, :] - dp[:ts, :])
            ys.append(jnp.dot(pc[:, sl], wpv[gi], preferred_element_type=F32))
        dx = ALPHA * dhc + jnp.concatenate(dxs, axis=1)
        y = jnp.concatenate(ys, axis=1) + pbv
        dscale = jnp.sum(dhc * y, axis=0, keepdims=True)
        dbias = jnp.sum(dy_ext[:ts, :], axis=0, keepdims=True)
        return (dx, dyb[:ts, :]), (dscale, dbias)

    return _rowwise(
        "pool_bwd", fn, S, ts, rows=[dh, pooled], halos=[(dh, "next", POOL_HALO)], consts=[wp, pb, ps],
        outs=[(D, F32), (D, BF16)], accs=[(1, D), (1, D)],
    )


def _pool_dw(pooled, dy):
    S, D = pooled.shape
    G = len(POOL_WINDOWS)
    gw = D // G
    tk = min(MM_TK, S)
    nk = S // tk

    def body(p_ref, d_ref, o_ref):
        k = pl.program_id(1)
        part = lax.dot_general(p_ref[...], d_ref[...], _DIMS["tn"], preferred_element_type=F32)

        @pl.when(k == 0)
        def _():
            o_ref[...] = part

        @pl.when(k > 0)
        def _():
            o_ref[...] += part

    return pl.pallas_call(
        body,
        name="pool_dw",
        grid=(G, nk),
        in_specs=[pl.BlockSpec((tk, gw), lambda g, k: (k, g)), pl.BlockSpec((tk, gw), lambda g, k: (k, g))],
        out_specs=pl.BlockSpec((None, gw, gw), lambda g, k: (g, 0, 0)),
        out_shape=jax.ShapeDtypeStruct((G, gw, gw), F32),
        compiler_params=_params(2),
    )(pooled, dy)


def _res_ln_mix(name, xhat_p, mix, gp_, bp_, g, b):
    S, D = xhat_p.shape
    ts = min(ROW_TILE, S)

    def fn(i, n, rv, hv, cv):
        xh, m = rv
        gpv, bpv, gv, bv = cv
        xhat, rstd, xo = _ln(ALPHA * (xh * gpv + bpv) + m, gv, bv)
        return (xhat, _wide(rstd, ts), xo), ()

    return _rowwise(name, fn, S, ts, rows=[xhat_p, mix], consts=[gp_, bp_, g, b], outs=[(D, F32), (LANES, F32), (D, BF16)])


def _res_ln_ffpe(name, xhat_p, ff, gate, pp, gp_, bp_, g, b):
    S, D = xhat_p.shape
    ts = min(ROW_TILE, S)

    def fn(i, n, rv, hv, cv):
        xh, f, gt, p_ = rv
        gpv, bpv, gv, bv = cv
        xhat, rstd, xo = _ln(ALPHA * (xh * gpv + bpv) + f + jax.nn.sigmoid(gt) * p_, gv, bv)
        return (xhat, _wide(rstd, ts), xo), ()

    return _rowwise(name, fn, S, ts, rows=[xhat_p, ff, gate, pp], consts=[gp_, bp_, g, b], outs=[(D, F32), (LANES, F32), (D, BF16)])


def _final_ln_loss(xhat_p, ff, gate, pp, tgt, gp_, bp_, g, b):
    S, D = xhat_p.shape
    ts = min(ROW_TILE, S)

    def fn(i, n, rv, hv, cv):
        xh, f, gt, p_, tg = rv
        gpv, bpv, gv, bv = cv
        xhat, rstd, y = _ln(ALPHA * (xh * gpv + bpv) + f + jax.nn.sigmoid(gt) * p_, gv, bv)
        e = y - tg
        dh, dg, db = _ln_bwd(e * (1.0 / D), xhat, rstd, gv)
        return (dh, dh), (jnp.sum(e * e, axis=0, keepdims=True), dg, db)

    return _rowwise(
        "final_ln_loss", fn, S, ts, rows=[xhat_p, ff, gate, pp, tgt], consts=[gp_, bp_, g, b],
        outs=[(D, F32), (D, BF16)], accs=[(1, D), (1, D), (1, D)],
    )


def _ln_bwd_call(name, dy, xhat, rstd, g):
    S, D = dy.shape
    ts = min(ROW_TILE, S)

    def fn(i, n, rv, hv, cv):
        dyv, xh, rs = rv
        dh, dg, db = _ln_bwd(dyv, xh, rs[:, :1], cv[0])
        return (dh, dh), (dg, db)

    return _rowwise(name, fn, S, ts, rows=[dy, xhat, rstd], consts=[g], outs=[(D, F32), (D, BF16)], accs=[(1, D), (1, D)])


def _ple_bwd(name, dh, gate, pp):
    S, D = dh.shape
    ts = min(ROW_TILE, S)

    def fn(i, n, rv, hv, cv):
        d, gt, p_ = rv
        sg = jax.nn.sigmoid(gt)
        dgt = d * p_ * sg * (1.0 - sg)
        return (dgt, d * sg), (jnp.sum(dgt, axis=0, keepdims=True),)

    return _rowwise(name, fn, S, ts, rows=[dh, gate, pp], outs=[(D, BF16), (D, BF16)], accs=[(1, D)])


def _silu(c):
    return c * jax.nn.sigmoid(c)


def _qkv_point(c, is_qk, scale):
    s = _silu(c)
    nrm = s * lax.rsqrt(jnp.sum(s * s, axis=-1, keepdims=True) + L2_EPS) * scale
    return jnp.where(is_qk, nrm, s)


def _conv_rows(xx, wv, lo, rows):
    acc = None
    for j in range(CONV_WIDTH):
        sh = CONV_WIDTH - 1 - j
        term = (pltpu.roll(xx, sh, axis=0) if sh else xx)[lo : lo + rows, :] * wv[j : j + 1, :]
        acc = term if acc is None else acc + term
    return acc


def _conv_fwd(qkv_pre, conv_w):
    S, W = qkv_pre.shape
    D = W // 3
    H = D // HEAD_DIM
    ts = min(CONV_TILE, S)
    r = ts // CONV_HALO

    def body(x_ref, xp_ref, w_ref, o_ref):
        j, i = pl.program_id(0), pl.program_id(1)
        xp = jnp.where(i > 0, xp_ref[...], 0.0)
        xx = jnp.concatenate([xp, x_ref[...]], axis=0)
        c = _conv_rows(xx, w_ref[...], CONV_HALO, ts)
        scale = jnp.where(j == 0, HEAD_DIM**-0.5, 1.0).astype(F32)
        for h in range(H):
            sl = slice(h * HEAD_DIM, (h + 1) * HEAD_DIM)
            o_ref[:, sl] = _qkv_point(c[:, sl], j < 2, scale)

    return pl.pallas_call(
        body,
        name="gdn_conv_fwd",
        grid=(3, S // ts),
        in_specs=[
            pl.BlockSpec((ts, D), lambda j, i: (i, j)),
            pl.BlockSpec((CONV_HALO, D), lambda j, i: (jnp.maximum(i * r - 1, 0), j)),
            pl.BlockSpec((CONV_WIDTH, D), lambda j, i: (0, j)),
        ],
        out_specs=pl.BlockSpec((ts, D), lambda j, i: (i, j)),
        out_shape=jax.ShapeDtypeStruct((S, W), F32),
        compiler_params=_params(2),
    )(qkv_pre, qkv_pre, conv_w)


def _conv_bwd(qkv_pre, conv_w, dqkvn):
    S, W = qkv_pre.shape
    D = W // 3
    H = D // HEAD_DIM
    ts = min(CONV_TILE, S)
    r, nb = ts // CONV_HALO, S // CONV_HALO
    te = ts + CONV_HALO

    def body(x_ref, xp_ref, xn_ref, w_ref, d_ref, dn_ref, dx_ref, dw_ref):
        j, i = pl.program_id(0), pl.program_id(1)
        n = pl.num_programs(1)
        wv = w_ref[...]
        xp = jnp.where(i > 0, xp_ref[...], 0.0)
        xx = jnp.concatenate([xp, x_ref[...], xn_ref[...]], axis=0)
        c = _conv_rows(xx, wv, CONV_HALO, te)
        dn = jnp.where(i < n - 1, dn_ref[...], 0.0)
        dout = jnp.concatenate([d_ref[...], dn], axis=0)
        scale = jnp.where(j == 0, HEAD_DIM**-0.5, 1.0).astype(F32)
        dcs = []
        for h in range(H):
            sl = slice(h * HEAD_DIM, (h + 1) * HEAD_DIM)
            _, vjp = jax.vjp(lambda cc: _qkv_point(cc, j < 2, scale), c[:, sl])
            dcs.append(vjp(dout[:, sl])[0])
        dc = jnp.concatenate(dcs, axis=1)
        dx = None
        dws = []
        for jj in range(CONV_WIDTH):
            sh = CONV_WIDTH - 1 - jj
            term = pltpu.roll(dc, CONV_HALO - sh, axis=0)[CONV_HALO:, :] * wv[jj : jj + 1, :]
            dx = term if dx is None else dx + term
            xs = (pltpu.roll(xx, sh, axis=0) if sh else xx)[CONV_HALO : CONV_HALO + ts, :]
            dws.append(jnp.sum(dc[:ts, :] * xs, axis=0, keepdims=True))
        dx_ref[...] = dx.astype(dx_ref.dtype)
        dw = jnp.concatenate(dws, axis=0)

        @pl.when(i == 0)
        def _():
            dw_ref[...] = dw

        @pl.when(i > 0)
        def _():
            dw_ref[...] += dw

    return pl.pallas_call(
        body,
        name="gdn_conv_bwd",
        grid=(3, S // ts),
        in_specs=[
            pl.BlockSpec((ts, D), lambda j, i: (i, j)),
            pl.BlockSpec((CONV_HALO, D), lambda j, i: (jnp.maximum(i * r - 1, 0), j)),
            pl.BlockSpec((CONV_HALO, D), lambda j, i: (jnp.minimum((i + 1) * r, nb - 1), j)),
            pl.BlockSpec((CONV_WIDTH, D), lambda j, i: (0, j)),
            pl.BlockSpec((ts, D), lambda j, i: (i, j)),
            pl.BlockSpec((CONV_HALO, D), lambda j, i: (jnp.minimum((i + 1) * r, nb - 1), j)),
        ],
        out_specs=[pl.BlockSpec((ts, D), lambda j, i: (i, j)), pl.BlockSpec((CONV_WIDTH, D), lambda j, i: (0, j))],
        out_shape=[jax.ShapeDtypeStruct((S, W), BF16), jax.ShapeDtypeStruct((CONV_WIDTH, W), F32)],
        compiler_params=_params(2),
    )(qkv_pre, qkv_pre, qkv_pre, conv_w, dqkvn, dqkvn)


def _softplus(x):
    pos = x > 0.0
    return jnp.where(pos, x, 0.0) + jnp.log(1.0 + jnp.exp(jnp.where(pos, -x, x)))


def _gates(bl, al, alog, dt):
    return jax.nn.sigmoid(bl), -jnp.exp(alog) * _softplus(al + dt)


def _gates_fwd(ba, alog, dt):
    S = ba.shape[0]
    ts = min(ROW_TILE, S)

    def fn(i, n, rv, hv, cv):
        return _gates(rv[0][:, :LANES], rv[0][:, LANES:], cv[0], cv[1]), ()

    return _rowwise("gdn_gates_fwd", fn, S, ts, rows=[ba], consts=[alog, dt], outs=[(LANES, F32), (LANES, F32)])


def _gates_bwd(ba, alog, dt, dbeta, dg, H):
    S = ba.shape[0]
    ts = min(ROW_TILE, S)

    def fn(i, n, rv, hv, cv):
        bav, dbv, dgv = rv
        real = lax.broadcasted_iota(jnp.int32, (1, LANES), 1) < H
        _, vjp = jax.vjp(_gates, bav[:, :LANES], bav[:, LANES:], cv[0], cv[1])
        dbl, dal, dalog, ddt = vjp((jnp.where(real, dbv, 0.0), jnp.where(real, dgv, 0.0)))
        dbl, dal = jnp.where(real, dbl, 0.0), jnp.where(real, dal, 0.0)
        return (jnp.concatenate([dbl, dal], axis=1),), (jnp.where(real, dalog, 0.0), jnp.where(real, ddt, 0.0))

    return _rowwise(
        "gdn_gates_bwd", fn, S, ts, rows=[ba, dbeta, dg], consts=[alog, dt], outs=[(2 * LANES, BF16)],
        accs=[(1, LANES), (1, LANES)],
    )


def _split_bf16(a, n):
    parts, rest = [], a
    for _ in range(n):
        piece = rest.astype(BF16)
        parts.append(piece)
        rest = rest - piece.astype(F32)
    return parts


def _tri_dot(a, b, mode, tri):
    d = lambda u, v: lax.dot_general(u, v, _DIMS[mode], preferred_element_type=F32)
    if tri == 0:
        return sum(d(a.astype(BF16), piece) for piece in _split_bf16(b, 3))
    return sum(d(piece, b.astype(BF16)) for piece in _split_bf16(a, 3))


def _make_dot(exact):
    def raw(a, b, mode):
        if exact:
            a_hi, a_lo = _split_bf16(a, 2)
            b_hi, b_lo = _split_bf16(b, 2)
            d = lambda u, v: lax.dot_general(u, v, _DIMS[mode], preferred_element_type=F32)
            return d(a_hi, b_hi) + (d(a_hi, b_lo) + d(a_lo, b_hi))
        return lax.dot_general(a.astype(BF16), b.astype(BF16), _DIMS[mode], preferred_element_type=F32)

    @functools.partial(jax.custom_vjp, nondiff_argnums=(2,))
    def dot(a, b, mode):
        return raw(a, b, mode)

    def fwd(a, b, mode):
        return raw(a, b, mode), (a, b)

    def bwd(mode, res, ct):
        a, b = res
        if mode == "nn":
            return dot(ct, b, "nt"), dot(a, ct, "tn")
        if mode == "nt":
            return dot(ct, b, "nn"), dot(ct, a, "tn")
        return dot(b, ct, "nt"), dot(a, ct, "nn")

    dot.defvjp(fwd, bwd)
    return dot


_bdot = _make_dot(False)
_fdot = _make_dot(True)


@jax.custom_vjp
def _unit_lower_inverse(a_strict):
    return _unit_lower_inverse_raw(a_strict)


def _unit_lower_inverse_fwd(a_strict):
    t = _unit_lower_inverse_raw(a_strict)
    return t, t


def _unit_lower_inverse_bwd(t, ct):
    left = [_bdot(ti, ci, "tn") for ti, ci in zip(t, ct)]
    return (tuple(-_bdot(li, ti, "nt") for li, ti in zip(left, t)),)


_unit_lower_inverse.defvjp(_unit_lower_inverse_fwd, _unit_lower_inverse_bwd)


def _unit_lower_inverse_raw(a_strict):
    C = a_strict[0].shape[0]
    ii = lax.broadcasted_iota(jnp.int32, (C, C), 0)
    jj = lax.broadcasted_iota(jnp.int32, (C, C), 1)
    eye = (ii == jj).astype(F32)
    blk = 16
    same = (ii // blk) == (jj // blk)
    p = [-jnp.where(same, a, 0.0) for a in a_strict]
    t = [eye + x for x in p]
    for _ in range(3):
        p = [_fdot(x, x, "nn") for x in p]
        t = [ti + _fdot(ti, x, "nn") for ti, x in zip(t, p)]
    while blk < C:
        same2 = (ii // (2 * blk)) == (jj // (2 * blk))
        off = jnp.logical_and(same2, jnp.logical_not(same))
        te = [_fdot(ti, jnp.where(off, a, 0.0), "nn") for ti, a in zip(t, a_strict)]
        t = [ti - _fdot(x, ti, "nn") for ti, x in zip(t, te)]
        same, blk = same2, 2 * blk
    return tuple(t)


def _chunk_heads(q, k, v, gc_col, gc_row, b_col, s0):
    R = range(len(q))
    C = q[0].shape[0]
    ii = lax.broadcasted_iota(jnp.int32, (C, C), 0)
    jj = lax.broadcasted_iota(jnp.int32, (C, C), 1)
    rows = lax.broadcasted_iota(jnp.int32, (C, 1), 0)
    decay = [jnp.where(ii >= jj, jnp.exp(jnp.minimum(gc_col[h] - gc_row[h], 0.0)), 0.0) for h in R]
    kb = [k[h] * b_col[h] for h in R]
    a = [_bdot(kb[h], k[h], "nt") * decay[h] for h in R]
    qk = [_bdot(q[h], k[h], "nt") * decay[h] for h in R]
    t = _unit_lower_inverse(tuple(jnp.where(ii > jj, a[h], 0.0) for h in R))
    eg = [jnp.exp(gc_col[h]) for h in R]
    u = [_bdot(t[h], v[h] * b_col[h], "nn") for h in R]
    w = [_bdot(t[h], kb[h] * eg[h], "nn") for h in R]
    g_last = [jnp.sum(jnp.where(rows == C - 1, gc_col[h], 0.0), axis=0, keepdims=True) for h in R]
    kd = [k[h] * jnp.exp(g_last[h] - gc_col[h]) for h in R]
    ws = [_bdot(w[h], s0[h], "nn") for h in R]
    qs = [_bdot(q[h] * eg[h], s0[h], "nn") for h in R]
    v_new = [u[h] - ws[h] for h in R]
    o = [qs[h] + _bdot(qk[h], v_new[h], "nn") for h in R]
    s1 = [s0[h] * jnp.exp(g_last[h]) + _bdot(kd[h], v_new[h], "tn") for h in R]
    return tuple(o), tuple(s1)


def _pick_lane(a, h):
    lanes = lax.broadcasted_iota(jnp.int32, a.shape, 1)
    return jnp.sum(jnp.where(lanes == h, a, 0.0), axis=1, keepdims=True)


def _pick_row(a, h):
    rows = lax.broadcasted_iota(jnp.int32, a.shape, 0)
    return jnp.sum(jnp.where(rows == h, a, 0.0), axis=0, keepdims=True)


def _tri(C):
    ii = lax.broadcasted_iota(jnp.int32, (C, C), 0)
    jj = lax.broadcasted_iota(jnp.int32, (C, C), 1)
    return (ii >= jj).astype(F32)


def _delta_fwd(qkvn, g_pad, g_rows, beta_pad):
    S, W = qkvn.shape
    D = W // 3
    H = D // HEAD_DIM
    C = min(CHUNK, S)
    N = S // C

    def body(x_ref, gp_ref, gr_ref, bp_ref, o_ref, sall_ref, st):
        n = pl.program_id(0)

        @pl.when(n == 0)
        def _():
            st[...] = jnp.zeros_like(st)

        low = _tri(C)
        gc_cols = _tri_dot(low, gp_ref[...], "nn", 0)
        gc_rows = _tri_dot(gr_ref[...], low, "nt", 1)
        bcols = bp_ref[...]
        hs = range(H)
        s0 = tuple(st[h] for h in hs)
        for h in hs:
            sall_ref[h] = s0[h]
        o, s1 = _chunk_heads(
            tuple(x_ref[:, h * HEAD_DIM : (h + 1) * HEAD_DIM] for h in hs),
            tuple(x_ref[:, D + h * HEAD_DIM : D + (h + 1) * HEAD_DIM] for h in hs),
            tuple(x_ref[:, 2 * D + h * HEAD_DIM : 2 * D + (h + 1) * HEAD_DIM] for h in hs),
            tuple(_pick_lane(gc_cols, h) for h in hs), tuple(_pick_row(gc_rows, h) for h in hs),
            tuple(_pick_lane(bcols, h) for h in hs), s0,
        )
        for h in hs:
            st[h] = s1[h]
            o_ref[:, h * HEAD_DIM : (h + 1) * HEAD_DIM] = o[h]

    return pl.pallas_call(
        body,
        name="gdn_delta_fwd",
        grid=(N,),
        in_specs=[
            pl.BlockSpec((C, W), lambda n: (n, 0)),
            pl.BlockSpec((C, LANES), lambda n: (n, 0)),
            pl.BlockSpec((None, 8, C), lambda n: (n, 0, 0)),
            pl.BlockSpec((C, LANES), lambda n: (n, 0)),
        ],
        out_specs=[pl.BlockSpec((C, D), lambda n: (n, 0)), pl.BlockSpec((None, H, HEAD_DIM, HEAD_DIM), lambda n: (n, 0, 0, 0))],
        out_shape=[jax.ShapeDtypeStruct((S, D), F32), jax.ShapeDtypeStruct((N, H, HEAD_DIM, HEAD_DIM), F32)],
        scratch_shapes=[pltpu.VMEM((H, HEAD_DIM, HEAD_DIM), F32)],
        compiler_params=_params(1),
    )(qkvn, g_pad, g_rows, beta_pad)


def _delta_bwd(qkvn, g_pad, g_rows, beta_pad, s_all, do):
    S, W = qkvn.shape
    D = W // 3
    H = D // HEAD_DIM
    C = min(CHUNK, S)
    N = S // C

    def body(x_ref, gp_ref, gr_ref, bp_ref, sall_ref, do_ref, dx_ref, dgp_ref, dgr_ref, dbp_ref, dst):
        n = pl.program_id(0)

        @pl.when(n == 0)
        def _():
            dst[...] = jnp.zeros_like(dst)

        low = _tri(C)
        gc_cols = _tri_dot(low, gp_ref[...], "nn", 0)
        gc_rows = _tri_dot(gr_ref[...], low, "nt", 1)
        bcols = bp_ref[...]
        lane = lax.broadcasted_iota(jnp.int32, (1, LANES), 1)
        row8 = lax.broadcasted_iota(jnp.int32, (8, 1), 0)
        dgc_cols = jnp.zeros((C, LANES), F32)
        dgc_rows = jnp.zeros((8, C), F32)
        dbcols = jnp.zeros((C, LANES), F32)
        hs = range(H)
        _, vjp = jax.vjp(
            _chunk_heads,
            tuple(x_ref[:, h * HEAD_DIM : (h + 1) * HEAD_DIM] for h in hs),
            tuple(x_ref[:, D + h * HEAD_DIM : D + (h + 1) * HEAD_DIM] for h in hs),
            tuple(x_ref[:, 2 * D + h * HEAD_DIM : 2 * D + (h + 1) * HEAD_DIM] for h in hs),
            tuple(_pick_lane(gc_cols, h) for h in hs), tuple(_pick_row(gc_rows, h) for h in hs),
            tuple(_pick_lane(bcols, h) for h in hs), tuple(sall_ref[h] for h in hs),
        )
        dq, dk, dv, dgc, dgr, dbc, ds0 = vjp((tuple(do_ref[:, h * HEAD_DIM : (h + 1) * HEAD_DIM] for h in hs), tuple(dst[h] for h in hs)))
        for h in hs:
            dst[h] = ds0[h]
            dx_ref[:, h * HEAD_DIM : (h + 1) * HEAD_DIM] = dq[h]
            dx_ref[:, D + h * HEAD_DIM : D + (h + 1) * HEAD_DIM] = dk[h]
            dx_ref[:, 2 * D + h * HEAD_DIM : 2 * D + (h + 1) * HEAD_DIM] = dv[h]
            dgc_cols = dgc_cols + dgc[h] * (lane == h).astype(F32)
            dgc_rows = dgc_rows + dgr[h] * (row8 == h).astype(F32)
            dbcols = dbcols + dbc[h] * (lane == h).astype(F32)
        dgp_ref[...] = _tri_dot(low, dgc_cols, "tn", 0)
        dgr_ref[...] = _tri_dot(dgc_rows, low, "nn", 1)
        dbp_ref[...] = dbcols

    rev = lambda n: N - 1 - n
    return pl.pallas_call(
        body,
        name="gdn_delta_bwd",
        grid=(N,),
        in_specs=[
            pl.BlockSpec((C, W), lambda n: (rev(n), 0)),
            pl.BlockSpec((C, LANES), lambda n: (rev(n), 0)),
            pl.BlockSpec((None, 8, C), lambda n: (rev(n), 0, 0)),
            pl.BlockSpec((C, LANES), lambda n: (rev(n), 0)),
            pl.BlockSpec((None, H, HEAD_DIM, HEAD_DIM), lambda n: (rev(n), 0, 0, 0)),
            pl.BlockSpec((C, D), lambda n: (rev(n), 0)),
        ],
        out_specs=[
            pl.BlockSpec((C, W), lambda n: (rev(n), 0)),
            pl.BlockSpec((C, LANES), lambda n: (rev(n), 0)),
            pl.BlockSpec((None, 8, C), lambda n: (rev(n), 0, 0)),
            pl.BlockSpec((C, LANES), lambda n: (rev(n), 0)),
        ],
        out_shape=[
            jax.ShapeDtypeStruct((S, W), F32),
            jax.ShapeDtypeStruct((S, LANES), F32),
            jax.ShapeDtypeStruct((N, 8, C), F32),
            jax.ShapeDtypeStruct((S, LANES), F32),
        ],
        scratch_shapes=[pltpu.VMEM((H, HEAD_DIM, HEAD_DIM), F32)],
        compiler_params=_params(1),
    )(qkvn, g_pad, g_rows, beta_pad, s_all, do)


def _gate_norm_head(o, z, nw):
    return o * lax.rsqrt(jnp.mean(o * o, axis=-1, keepdims=True) + RMS_EPS) * nw * _silu(z)


def _gate_norm_fwd(o, z, nw):
    S, D = o.shape
    H = D // HEAD_DIM
    ts = min(ROW_TILE, S)

    def fn(i, n, rv, hv, cv):
        ov, zv = rv
        parts = [_gate_norm_head(ov[:, h * HEAD_DIM : (h + 1) * HEAD_DIM], zv[:, h * HEAD_DIM : (h + 1) * HEAD_DIM], cv[0]) for h in range(H)]
        return (jnp.concatenate(parts, axis=1),), ()

    return _rowwise("gdn_gate_norm_fwd", fn, S, ts, rows=[o, z], consts=[nw], outs=[(D, BF16)])[0]


def _gate_norm_bwd(dog, o, z, nw):
    S, D = o.shape
    H = D // HEAD_DIM
    ts = min(ROW_TILE, S)

    def fn(i, n, rv, hv, cv):
        dv, ov, zv = rv
        dos, dzs, dnw = [], [], None
        for h in range(H):
            sl = slice(h * HEAD_DIM, (h + 1) * HEAD_DIM)
            _, vjp = jax.vjp(_gate_norm_head, ov[:, sl], zv[:, sl], cv[0])
            a, b_, c_ = vjp(dv[:, sl])
            dos.append(a)
            dzs.append(b_)
            dnw = c_ if dnw is None else dnw + c_
        return (jnp.concatenate(dos, axis=1), jnp.concatenate(dzs, axis=1)), (dnw,)

    return _rowwise("gdn_gate_norm_bwd", fn, S, ts, rows=[dog, o, z], consts=[nw], outs=[(D, F32), (D, BF16)], accs=[(1, HEAD_DIM)])


def _square_bf16(r):
    rf = r.astype(F32)
    return rf * rf


def _mlp_ple_fwd(li, xa, p, w1, w2, wg, bg, wp):
    r = _mm(f"l{li}_mlp_up", xa, w1, "nn", [BF16], epi=lambda acc: (jnp.maximum(acc, 0.0),), tm=1024, tn=1024, b_outer=True)
    ff = _mm(f"l{li}_mlp_down", r, w2, "nn", [F32], a_fn=_square_bf16, tm=256, tk=4096)
    gate = _mm(f"l{li}_ple_gate", xa, wg, "nn", [F32], epi=lambda acc, bias: (acc + bias,), extras=[(bg, "row")], tm=1024)
    pp = _mm(f"l{li}_ple_proj", p, wp, "nn", [F32])
    return r, ff, gate, pp


def _mlp_ple_bwd(li, dh, dhb, xa, p, r, gate, pp, w1, w2, wg):
    dpre = _mm(f"l{li}_mlp_down_bwd", dhb, w2, "nt", [BF16], epi=lambda acc, rr: (acc * (2.0 * rr.astype(F32)),), extras=[(r, "tile")], tm=1024, tn=1024, b_outer=True)
    dw2 = _mm(f"l{li}_mlp_dw2", r, dhb, "tn", [F32], a_fn=_square_bf16)
    dgate, dpp, dbg = _ple_bwd(f"l{li}_ple_bwd", dh, gate, pp)
    dw1 = _mm(f"l{li}_mlp_dw1", xa, dpre, "tn", [F32])
    dwg = _mm(f"l{li}_ple_dwg", xa, dgate, "tn", [F32])
    dwp = _mm(f"l{li}_ple_dwp", p, dpp, "tn", [F32])
    t = _mm(f"l{li}_ple_gate_bwd", dgate, wg, "nt", [F32], epi=lambda acc, d: (acc + ALPHA * d,), extras=[(dh, "tile")], tm=1024)
    dxa = _mm(f"l{li}_mlp_up_bwd", dpre, w1, "nt", [F32], epi=lambda acc, d: (acc + d,), extras=[(t, "tile")], tm=256, tk=4096)
    return dxa, dw1, dw2, dwg, dbg, dwp


def _local_step(x, p, tgt, W):
    S, D = x.shape
    H = D // HEAD_DIM
    C = min(CHUNK, S)
    N = S // C
    lg = lambda i, j: W["ln_gain"][2 * i + j][None, :]
    lb = lambda i, j: W["ln_bias"][2 * i + j][None, :]
    G = {}

    pooled, xh0a, rs0a, x0a = _pool_fwd(x, W["pool_w"], W["pool_b"], W["pool_scale"], lg(0, 0), lb(0, 0))
    r0, ff0, gate0, pp0 = _mlp_ple_fwd(0, x0a, p[0], W["mlp_w1"][0], W["mlp_w2"][0], W["ple_gate_w"][0], W["ple_gate_b"][0:1], W["ple_proj"][0])
    xh0b, rs0b, x0b = _res_ln_ffpe("l0_ln_b", xh0a, ff0, gate0, pp0, lg(0, 0), lb(0, 0), lg(0, 1), lb(0, 1))

    qkv_pre = _mm("gdn_in_qkv", x0b, W["gdn_wqkv"], "nn", [F32], tm=1024, tn=1024, b_outer=True)
    z = _mm("gdn_in_z", x0b, W["gdn_wz"], "nn", [F32], tm=1024)
    ba = _mm("gdn_in_ba", x0b, W["gdn_wba"], "nn", [F32])
    qkvn = _conv_fwd(qkv_pre, W["gdn_conv"])
    beta_pad, g_pad = _gates_fwd(ba, W["gdn_a_log"], W["gdn_dt_bias"])
    g_rows = g_pad[:, :8].reshape(N, C, 8).transpose(0, 2, 1)
    o, s_all = _delta_fwd(qkvn, g_pad, g_rows, beta_pad)
    og = _gate_norm_fwd(o, z, W["gdn_norm_w"])
    mix1 = _mm("gdn_out", og, W["gdn_w_out"], "nn", [F32], tm=1024)
    xh1a, rs1a, x1a = _res_ln_mix("l1_ln_a", xh0b, mix1, lg(0, 1), lb(0, 1), lg(1, 0), lb(1, 0))
    r1, ff1, gate1, pp1 = _mlp_ple_fwd(1, x1a, p[1], W["mlp_w1"][1], W["mlp_w2"][1], W["ple_gate_w"][1], W["ple_gate_b"][1:2], W["ple_proj"][1])
    dh1b, dh1b_b, loss_cols, dg11, db11 = _final_ln_loss(xh1a, ff1, gate1, pp1, tgt, lg(1, 0), lb(1, 0), lg(1, 1), lb(1, 1))

    dx1a, dw1_1, dw2_1, dwg_1, dbg_1, dwp_1 = _mlp_ple_bwd(1, dh1b, dh1b_b, x1a, p[1], r1, gate1, pp1, W["mlp_w1"][1], W["mlp_w2"][1], W["ple_gate_w"][1])
    dh1a, dh1a_b, dg10, db10 = _ln_bwd_call("l1_ln_a_bwd", dx1a, xh1a, rs1a, lg(1, 0))
    dog = _mm("gdn_out_bwd", dh1a_b, W["gdn_w_out"], "nt", [F32], tm=1024)
    G["gdn_w_out"] = _mm("gdn_dw_out", og, dh1a_b, "tn", [F32])
    do, dz, dnw = _gate_norm_bwd(dog, o, z, W["gdn_norm_w"])
    dqkvn, dg_col, dg_row, dbeta = _delta_bwd(qkvn, g_pad, g_rows, beta_pad, s_all, do)
    dg_all = dg_col + jnp.pad(dg_row.transpose(0, 2, 1).reshape(S, 8), ((0, 0), (0, LANES - 8)))
    dba, dalog, ddt = _gates_bwd(ba, W["gdn_a_log"], W["gdn_dt_bias"], dbeta, dg_all, H)
    dqkv, dconv = _conv_bwd(qkv_pre, W["gdn_conv"], dqkvn)
    dwqkv = _mm("gdn_dwqkv", x0b, dqkv, "tn", [F32])
    dwz = _mm("gdn_dwz", x0b, dz, "tn", [F32])
    dwba = _mm("gdn_dwba", x0b, dba, "tn", [F32])
    G["gdn_w_in"] = jnp.concatenate([dwqkv, dwz, dwba[:, :H], dwba[:, LANES : LANES + H]], axis=1)
    t = _mm("gdn_in_ba_bwd", dba, W["gdn_wba"], "nt", [F32], epi=lambda acc, d: (acc + ALPHA * d,), extras=[(dh1a, "tile")])
    t = _mm("gdn_in_z_bwd", dz, W["gdn_wz"], "nt", [F32], epi=lambda acc, d: (acc + d,), extras=[(t, "tile")], tm=1024)
    dx0b = _mm("gdn_in_qkv_bwd", dqkv, W["gdn_wqkv"], "nt", [F32], epi=lambda acc, d: (acc + d,), extras=[(t, "tile")], tm=256, tk=3072)

    dh0b, dh0b_b, dg01, db01 = _ln_bwd_call("l0_ln_b_bwd", dx0b, xh0b, rs0b, lg(0, 1))
    dx0a, dw1_0, dw2_0, dwg_0, dbg_0, dwp_0 = _mlp_ple_bwd(0, dh0b, dh0b_b, x0a, p[0], r0, gate0, pp0, W["mlp_w1"][0], W["mlp_w2"][0], W["ple_gate_w"][0])
    dh0a, _, dg00, db00 = _ln_bwd_call("l0_ln_a_bwd", dx0a, xh0a, rs0a, lg(0, 0))
    grad_x, dyp, dscale, dpb = _pool_bwd(dh0a, pooled, W["pool_w"], W["pool_b"], W["pool_scale"])
    G["pool_w"] = _pool_dw(pooled, dyp)

    G["ln_gain"] = jnp.concatenate([dg00, dg01, dg10, dg11], axis=0)
    G["ln_bias"] = jnp.concatenate([db00, db01, db10, db11], axis=0)
    G["pool_b"] = dpb
    G["pool_scale"] = dscale
    G["gdn_conv"] = dconv
    G["gdn_a_log"] = dalog[:, :H]
    G["gdn_dt_bias"] = ddt[:, :H]
    G["gdn_norm_w"] = dnw
    G["mlp_w1"] = jnp.stack([dw1_0, dw1_1])
    G["mlp_w2"] = jnp.stack([dw2_0, dw2_1])
    G["ple_gate_w"] = jnp.stack([dwg_0, dwg_1])
    G["ple_gate_b"] = jnp.concatenate([dbg_0, dbg_1], axis=0)
    G["ple_proj"] = jnp.stack([dwp_0, dwp_1])
    return loss_cols, grad_x, G


_HBM = pl.BlockSpec(memory_space=pltpu.HBM)


def _all_gather(name, shards):
    T = len(shards)

    def body(*refs):
        ins, outs = refs[:T], refs[T : 2 * T]
        send_sems, recv_sems, local_sems = refs[2 * T :]
        x, y, c = lax.axis_index("x"), lax.axis_index("y"), lax.axis_index("c")
        me, sibling = (x, y, c), (x, y, 1 - c)
        chips = [(1 - x, y), (x, 1 - y), (1 - x, 1 - y)]

        def blk(t, px, py, pc):
            return outs[t].at[4 * px + 2 * py + pc]

        def copy(t, k, block, to, src=None):
            return pltpu.make_async_remote_copy(
                src_ref=blk(t, *block) if src is None else src, dst_ref=blk(t, *block),
                send_sem=send_sems.at[t, k], recv_sem=recv_sems.at[t, k], device_id=to, device_id_type=MESH,
            )

        mine = [pltpu.make_async_copy(ins[t], blk(t, *me), local_sems.at[t]) for t in range(T)]
        for cp in mine:
            cp.start()
        first = []
        for t in range(T):
            first.append(copy(t, 0, me, sibling, src=ins[t]))
            first += [copy(t, 1 + j, me, (*chip, c), src=ins[t]) for j, chip in enumerate(chips)]
        for cp in first:
            cp.start()
        passed = []
        for j, chip in enumerate(chips):
            for t in range(T):
                copy(t, 1 + j, (*chip, c), me).wait_recv()
                fw = copy(t, 4 + j, (*chip, c), sibling)
                fw.start()
                passed.append(fw)
        for t in range(T):
            copy(t, 0, sibling, me).wait_recv()
            for j, chip in enumerate(chips):
                copy(t, 4 + j, (*chip, 1 - c), me).wait_recv()
        for cp in first + passed:
            cp.wait_send()
        for cp in mine:
            cp.wait()

    return pl.pallas_call(
        body,
        name=name,
        in_specs=[_HBM] * T,
        out_specs=[_HBM] * T,
        out_shape=[jax.ShapeDtypeStruct((N_DEV,) + s.shape, s.dtype) for s in shards],
        scratch_shapes=[pltpu.SemaphoreType.DMA((T, 7)), pltpu.SemaphoreType.DMA((T, 7)), pltpu.SemaphoreType.DMA((T,))],
    )(*shards)


def _exchange(name, blocks):
    def body(g_ref, o_ref, send_sems, recv_sems, local_sem):
        x, y, c = lax.axis_index("x"), lax.axis_index("y"), lax.axis_index("c")
        own = pltpu.make_async_copy(g_ref.at[4 * x + 2 * y + c], o_ref.at[N_DEV - 1], local_sem)
        own.start()
        copies = []
        for rel in range(1, N_DEV):
            px = 1 - x if rel & 4 else x
            py = 1 - y if rel & 2 else y
            pc = 1 - c if rel & 1 else c
            copies.append(
                pltpu.make_async_remote_copy(
                    src_ref=g_ref.at[4 * px + 2 * py + pc], dst_ref=o_ref.at[rel - 1],
                    send_sem=send_sems.at[rel - 1], recv_sem=recv_sems.at[rel - 1], device_id=(px, py, pc), device_id_type=MESH,
                )
            )
        for cp in copies:
            cp.start()
        for cp in copies:
            cp.wait_recv()
        for cp in copies:
            cp.wait_send()
        own.wait()

    return pl.pallas_call(
        body,
        name=name,
        in_specs=[_HBM],
        out_specs=_HBM,
        out_shape=jax.ShapeDtypeStruct(blocks.shape, blocks.dtype),
        scratch_shapes=[pltpu.SemaphoreType.DMA((N_DEV - 1,)), pltpu.SemaphoreType.DMA((N_DEV - 1,)), pltpu.SemaphoreType.DMA],
    )(blocks)


def _sum_blocks(name, parts, tr):
    _, R, Cw = parts.shape
    tr = min(tr, R)
    assert R % tr == 0

    def body(p_ref, o_ref):
        acc = p_ref[0].astype(F32)
        for d in range(1, N_DEV):
            acc = acc + p_ref[d].astype(F32)
        o_ref[...] = acc

    return pl.pallas_call(
        body,
        name=name,
        grid=(R // tr,),
        in_specs=[pl.BlockSpec((N_DEV, tr, Cw), lambda i: (0, i, 0))],
        out_specs=pl.BlockSpec((tr, Cw), lambda i: (i, 0)),
        out_shape=jax.ShapeDtypeStruct((R, Cw), F32),
        compiler_params=_params(1),
    )(parts)


def _adamw(name, w, g, m, v):
    shape = w.shape
    cols = shape[-1]
    rows = w.size // cols
    tr = rows if rows <= 512 else 512
    assert rows % tr == 0
    w2, g2, m2, v2 = (a.reshape(rows, cols) for a in (w, g, m, v))

    def body(w_ref, g_ref, m_ref, v_ref, d_ref, mo_ref, vo_ref):
        gv = g_ref[...]
        mn = ADAM_B1 * m_ref[...] + (1.0 - ADAM_B1) * gv
        vn = ADAM_B2 * v_ref[...] + (1.0 - ADAM_B2) * jnp.square(gv)
        m_hat = mn / (1.0 - ADAM_B1**ADAM_STEP)
        v_hat = vn / (1.0 - ADAM_B2**ADAM_STEP)
        d_ref[...] = -ADAM_LR * (m_hat / (jnp.sqrt(v_hat) + ADAM_EPS) + ADAM_WD * w_ref[...])
        mo_ref[...] = mn
        vo_ref[...] = vn

    spec = pl.BlockSpec((tr, cols), lambda i: (i, 0))
    d, mn, vn = pl.pallas_call(
        body,
        name=name,
        grid=(rows // tr,),
        in_specs=[spec] * 4,
        out_specs=[spec] * 3,
        out_shape=[jax.ShapeDtypeStruct((rows, cols), F32)] * 3,
        compiler_params=_params(1),
    )(w2, g2, m2, v2)
    return d.reshape(shape), mn.reshape(shape), vn.reshape(shape)


BIG = ("gdn_w_in", "gdn_w_out", "mlp_w1", "mlp_w2", "ple_gate_w", "ple_proj", "pool_w")
SMALL_SHARDED = ("ln_gain", "ln_bias", "pool_b", "gdn_conv")
SMALL_REPLICATED = ("pool_scale", "gdn_a_log", "gdn_dt_bias", "gdn_norm_w", "ple_gate_b")
WEIGHTS = ("ln_gain", "ln_bias", "pool_w", "pool_b", "pool_scale", "gdn_w_in", "gdn_conv", "gdn_a_log", "gdn_dt_bias",
           "gdn_norm_w", "gdn_w_out", "mlp_w1", "mlp_w2", "ple_gate_w", "ple_gate_b", "ple_proj")
BIG_AXIS = {"gdn_w_in": 1, "gdn_w_out": 0, "mlp_w1": 2, "mlp_w2": 1, "ple_gate_w": 1, "ple_proj": 2, "pool_w": 1}
PACK_ROWS_ALIGN = 512
PACK_PART_ALIGN = 16


def _squeeze_big(name, a):
    return a[0] if name in ("gdn_w_in", "gdn_w_out", "pool_w") else a


def _part_rows(a, width):
    rows = a.size // width
    return rows + (-rows) % PACK_PART_ALIGN


def _pack_rows(parts, width, dtype):
    padded = []
    for a in parts:
        a2 = a.reshape(-1, width).astype(dtype)
        padded.append(jnp.pad(a2, ((0, _part_rows(a, width) - a2.shape[0]), (0, 0))))
    flat = jnp.concatenate(padded, axis=0)
    return jnp.pad(flat, ((0, (-flat.shape[0]) % PACK_ROWS_ALIGN), (0, 0)))


def _split_blocks(name, full):
    ax = BIG_AXIS[name]
    shp = full.shape
    a = full.reshape(shp[:ax] + (N_DEV, shp[ax] // N_DEV) + shp[ax + 1 :])
    return jnp.moveaxis(a, ax, 0)


def _join_blocks(name, blocks):
    ax = BIG_AXIS[name]
    a = jnp.moveaxis(blocks, 0, ax)
    shp = a.shape
    return a.reshape(shp[:ax] + (shp[ax] * shp[ax + 1],) + shp[ax + 2 :])


def _pack_small(parts):
    flat = jnp.concatenate([jnp.pad(a.reshape(-1), (0, (-a.size) % LANES)) for a in parts])
    rows = flat.size // LANES
    return jnp.pad(flat.reshape(rows, LANES), ((0, (-rows) % 8), (0, 0)))


def _unpack_small(packed, shapes):
    flat = packed.reshape(packed.shape[:-2] + (-1,))
    out, off = [], 0
    for shp in shapes:
        size = 1
        for s in shp:
            size *= s
        out.append(flat[..., off : off + size].reshape(flat.shape[:-1] + tuple(shp)))
        off += size + (-size) % LANES
    return out


def kernel(x, p, ln_gain, ln_bias, pool_w, pool_b, pool_scale, gdn_w_in, gdn_conv, gdn_a_log, gdn_dt_bias, gdn_norm_w, gdn_w_out, mlp_w1, mlp_w2, ple_gate_w, ple_gate_b, ple_proj, loss_target, m_ln_gain, m_ln_bias, m_pool_w, m_pool_b, m_pool_scale, m_gdn_w_in, m_gdn_conv, m_gdn_a_log, m_gdn_dt_bias, m_gdn_norm_w, m_gdn_w_out, m_mlp_w1, m_mlp_w2, m_ple_gate_w, m_ple_gate_b, m_ple_proj, v_ln_gain, v_ln_bias, v_pool_w, v_pool_b, v_pool_scale, v_gdn_w_in, v_gdn_conv, v_gdn_a_log, v_gdn_dt_bias, v_gdn_norm_w, v_gdn_w_out, v_mlp_w1, v_mlp_w2, v_ple_gate_w, v_ple_gate_b, v_ple_proj):
    w_sh = dict(ln_gain=ln_gain, ln_bias=ln_bias, pool_w=pool_w, pool_b=pool_b, pool_scale=pool_scale, gdn_w_in=gdn_w_in,
                gdn_conv=gdn_conv, gdn_a_log=gdn_a_log, gdn_dt_bias=gdn_dt_bias, gdn_norm_w=gdn_norm_w, gdn_w_out=gdn_w_out,
                mlp_w1=mlp_w1, mlp_w2=mlp_w2, ple_gate_w=ple_gate_w, ple_gate_b=ple_gate_b, ple_proj=ple_proj)
    m_sh = dict(ln_gain=m_ln_gain, ln_bias=m_ln_bias, pool_w=m_pool_w, pool_b=m_pool_b, pool_scale=m_pool_scale, gdn_w_in=m_gdn_w_in,
                gdn_conv=m_gdn_conv, gdn_a_log=m_gdn_a_log, gdn_dt_bias=m_gdn_dt_bias, gdn_norm_w=m_gdn_norm_w, gdn_w_out=m_gdn_w_out,
                mlp_w1=m_mlp_w1, mlp_w2=m_mlp_w2, ple_gate_w=m_ple_gate_w, ple_gate_b=m_ple_gate_b, ple_proj=m_ple_proj)
    v_sh = dict(ln_gain=v_ln_gain, ln_bias=v_ln_bias, pool_w=v_pool_w, pool_b=v_pool_b, pool_scale=v_pool_scale, gdn_w_in=v_gdn_w_in,
                gdn_conv=v_gdn_conv, gdn_a_log=v_gdn_a_log, gdn_dt_bias=v_gdn_dt_bias, gdn_norm_w=v_gdn_norm_w, gdn_w_out=v_gdn_w_out,
                mlp_w1=v_mlp_w1, mlp_w2=v_mlp_w2, ple_gate_w=v_ple_gate_w, ple_gate_b=v_ple_gate_b, ple_proj=v_ple_proj)
    xs, tg = x[0], loss_target[0]
    ps = p[:, 0]
    S, D = xs.shape
    H = D // HEAD_DIM
    me = 4 * lax.axis_index("x") + 2 * lax.axis_index("y") + lax.axis_index("c")

    big_sh = {n: _squeeze_big(n, w_sh[n]) for n in BIG}
    packed = _pack_rows([big_sh[n] for n in BIG], D, BF16)
    small_packed = _pack_small([w_sh[n] for n in SMALL_SHARDED])
    gathered, small_gathered = _all_gather("gather_weights", [packed, small_packed])
    W, off = {}, 0
    for n in BIG:
        rows = big_sh[n].size // D
        W[n] = _join_blocks(n, gathered[:, off : off + rows].reshape((N_DEV,) + big_sh[n].shape))
        off += _part_rows(big_sh[n], D)
    smalls = _unpack_small(small_gathered, [w_sh[n].shape for n in SMALL_SHARDED])
    for n, a in zip(SMALL_SHARDED, smalls):
        W[n] = jnp.moveaxis(a, 0, -2).reshape(a.shape[1:-1] + (N_DEV * a.shape[-1],))
    W["ln_gain"] = W["ln_gain"].reshape(2 * DEPTH, D)
    W["ln_bias"] = W["ln_bias"].reshape(2 * DEPTH, D)
    W["pool_b"] = W["pool_b"].reshape(1, D)
    W["gdn_conv"] = W["gdn_conv"][0]
    W["pool_scale"] = pool_scale
    W["ple_gate_b"] = ple_gate_b
    W["gdn_norm_w"] = gdn_norm_w
    W["gdn_a_log"] = jnp.pad(gdn_a_log, ((0, 0), (0, LANES - H)))
    W["gdn_dt_bias"] = jnp.pad(gdn_dt_bias, ((0, 0), (0, LANES - H)))
    w_in = W.pop("gdn_w_in")
    W["gdn_wqkv"], W["gdn_wz"] = w_in[:, : 3 * D], w_in[:, 3 * D : 4 * D]
    W["gdn_wba"] = jnp.concatenate(
        [jnp.pad(w_in[:, 4 * D : 4 * D + H], ((0, 0), (0, LANES - H))), jnp.pad(w_in[:, 4 * D + H :], ((0, 0), (0, LANES - H)))], axis=1
    )

    loss_cols, grad_x, G = _local_step(xs, ps, tg, W)
    loss = lax.psum(0.5 * jnp.sum(loss_cols) / D, MESH_AXES)
    g_packed = jnp.stack([_pack_rows([_split_blocks(n, G[n])[d] for n in BIG], D, BF16) for d in range(N_DEV)])
    g_recv = _exchange("exchange_grads", g_packed)
    g_sum = _sum_blocks("sum_grads", g_recv, 512)
    grads, off = {}, 0
    for n in BIG:
        rows = big_sh[n].size // D
        grads[n] = g_sum[off : off + rows].reshape(w_sh[n].shape)
        off += _part_rows(big_sh[n], D)
    small_names = SMALL_SHARDED + SMALL_REPLICATED
    small_full_shapes = {n: G[n].shape for n in small_names}
    gs_packed = _pack_small([G[n] for n in small_names])
    (gs_all,) = _all_gather("gather_small_grads", [gs_packed])
    gs_sum = _sum_blocks("sum_small_grads", gs_all, 512)
    for n, a in zip(small_names, _unpack_small(gs_sum, [small_full_shapes[n] for n in small_names])):
        if n in SMALL_SHARDED:
            width = w_sh[n].shape[-1]
            a = a.reshape(w_sh[n].shape[:-1] + (N_DEV * width,))
            a = lax.dynamic_slice_in_dim(a, me * width, width, axis=a.ndim - 1)
        grads[n] = a.reshape(w_sh[n].shape)

    deltas, new_m, new_v = {}, {}, {}
    for n in WEIGHTS:
        deltas[n], new_m[n], new_v[n] = _adamw(f"adamw_{n}", w_sh[n], grads[n], m_sh[n], v_sh[n])
    return (loss, grad_x[None], *[grads[n] for n in WEIGHTS], *[deltas[n] for n in WEIGHTS],
            *[new_m[n] for n in WEIGHTS], *[new_v[n] for n in WEIGHTS])
```

```python
import functools

import jax
import jax.numpy as jnp
from jax import lax
from jax.experimental import pallas as pl
from jax.experimental.pallas import tpu as pltpu

F32 = jnp.float32
BF16 = jnp.bfloat16
MESH_AXES = ("x", "y", "c")
N_DEV = 8
MESH = pl.DeviceIdType.MESH

DEPTH = 2
ALPHA = (2.0 * DEPTH) ** 0.25
LN_EPS = 1e-5
RMS_EPS = 1e-6
L2_EPS = 1e-6
HEAD_DIM = 128
CONV_WIDTH = 4
POOL_WINDOWS = (2, 4, 8, 16)
POOL_HALO = 16
CONV_HALO = 8
LANES = 128
ADAM_LR = 0.001
ADAM_B1 = 0.9
ADAM_B2 = 0.999
ADAM_EPS = 1e-08
ADAM_WD = 0.01
ADAM_STEP = 10

VMEM_LIMIT = 56 * 1024 * 1024
ROW_TILE = 256
CONV_TILE = 256
CHUNK = 128
MM_TM, MM_TN, MM_TK = 512, 1024, 1024

_DIMS = {
    "nn": (((1,), (0,)), ((), ())),
    "nt": (((1,), (1,)), ((), ())),
    "tn": (((0,), (0,)), ((), ())),
}


def _params(n_axes):
    return pltpu.CompilerParams(dimension_semantics=("arbitrary",) * n_axes, vmem_limit_bytes=VMEM_LIMIT)


def _fit(tile, n):
    tile = min(tile, n)
    while n % tile:
        tile //= 2
    return tile


def _mm(name, a, b, mode, out_dtypes, epi=None, extras=(), a_fn=None, tm=None, tn=None, tk=None, b_outer=False):
    if mode == "tn":
        K, M = a.shape
    else:
        M, K = a.shape
    N = b.shape[0] if mode == "nt" else b.shape[1]
    tm, tn, tk = _fit(tm or MM_TM, M), _fit(tn or MM_TN, N), _fit(tk or MM_TK, K)
    nk = K // tk

    def at(f):
        return (lambda j, i, k: f(i, j, k)) if b_outer else f

    a_spec = pl.BlockSpec((tk, tm), at(lambda i, j, k: (k, i))) if mode == "tn" else pl.BlockSpec((tm, tk), at(lambda i, j, k: (i, k)))
    b_spec = pl.BlockSpec((tn, tk), at(lambda i, j, k: (j, k))) if mode == "nt" else pl.BlockSpec((tk, tn), at(lambda i, j, k: (k, j)))
    ex_specs = [
        pl.BlockSpec((tm, tn), at(lambda i, j, k: (i, j))) if kind == "tile" else pl.BlockSpec((1, tn), at(lambda i, j, k: (0, j)))
        for _, kind in extras
    ]
    n_ex, n_out = len(extras), len(out_dtypes)

    def body(*refs):
        a_ref, b_ref = refs[0], refs[1]
        ex_refs = refs[2 : 2 + n_ex]
        out_refs = refs[2 + n_ex : 2 + n_ex + n_out]
        av = a_ref[...]
        if a_fn is not None:
            av = a_fn(av)
        part = lax.dot_general(av.astype(BF16), b_ref[...].astype(BF16), _DIMS[mode], preferred_element_type=F32)

        def finish(res):
            vals = epi(res, *[e[...] for e in ex_refs]) if epi is not None else (res,)
            for o_ref, v in zip(out_refs, vals):
                o_ref[...] = v.astype(o_ref.dtype)

        if nk == 1:
            finish(part)
        else:
            acc = refs[-1]
            k = pl.program_id(2)

            @pl.when(k == 0)
            def _():
                acc[...] = part

            @pl.when(k > 0)
            def _():
                acc[...] += part

            @pl.when(k == nk - 1)
            def _():
                finish(acc[...])

    outs = pl.pallas_call(
        body,
        name=name,
        grid=(N // tn, M // tm, nk) if b_outer else (M // tm, N // tn, nk),
        in_specs=[a_spec, b_spec] + ex_specs,
        out_specs=[pl.BlockSpec((tm, tn), at(lambda i, j, k: (i, j))) for _ in out_dtypes],
        out_shape=[jax.ShapeDtypeStruct((M, N), dt) for dt in out_dtypes],
        scratch_shapes=[pltpu.VMEM((tm, tn), F32)] if nk > 1 else [],
        compiler_params=_params(3),
    )(a, b, *[e for e, _ in extras])
    return outs[0] if n_out == 1 else outs


def _rowwise(name, fn, S, ts, rows=(), halos=(), consts=(), outs=(), accs=()):
    ts = min(ts, S)
    assert S % ts == 0
    n = S // ts
    in_specs = [pl.BlockSpec((ts, a.shape[1]), lambda i: (i, 0)) for a in rows]
    for a, kind, hr in halos:
        r, nb = ts // hr, S // hr
        if kind == "prev":
            in_specs.append(pl.BlockSpec((hr, a.shape[1]), lambda i, r=r: (jnp.maximum(i * r - 1, 0), 0)))
        else:
            in_specs.append(pl.BlockSpec((hr, a.shape[1]), lambda i, r=r, nb=nb: (jnp.minimum((i + 1) * r, nb - 1), 0)))
    in_specs += [pl.BlockSpec(a.shape, lambda i, nd=a.ndim: (0,) * nd) for a in consts]
    out_specs = [pl.BlockSpec((ts, w), lambda i: (i, 0)) for w, _ in outs]
    out_specs += [pl.BlockSpec((r, w), lambda i: (0, 0)) for r, w in accs]
    out_shape = [jax.ShapeDtypeStruct((S, w), dt) for w, dt in outs]
    out_shape += [jax.ShapeDtypeStruct((r, w), F32) for r, w in accs]
    nr, nh, nc, no = len(rows), len(halos), len(consts), len(outs)

    def body(*refs):
        i = pl.program_id(0)
        rv = [r[...] for r in refs[:nr]]
        hv = [r[...] for r in refs[nr : nr + nh]]
        cv = [r[...] for r in refs[nr + nh : nr + nh + nc]]
        o_refs = refs[nr + nh + nc : nr + nh + nc + no]
        a_refs = refs[nr + nh + nc + no :]
        ovals, avals = fn(i, n, rv, hv, cv)
        for o_ref, v in zip(o_refs, ovals):
            o_ref[...] = v.astype(o_ref.dtype)
        for a_ref, v in zip(a_refs, avals):

            @pl.when(i == 0)
            def _(a_ref=a_ref, v=v):
                a_ref[...] = v

            @pl.when(i > 0)
            def _(a_ref=a_ref, v=v):
                a_ref[...] += v

    res = pl.pallas_call(
        body,
        name=name,
        grid=(n,),
        in_specs=in_specs,
        out_specs=out_specs,
        out_shape=out_shape,
        compiler_params=_params(1),
    )(*rows, *[h[0] for h in halos], *consts)
    return list(res)


def _ln(h, g, b):
    mu = jnp.mean(h, axis=-1, keepdims=True)
    d = h - mu
    var = jnp.mean(d * d, axis=-1, keepdims=True)
    rstd = lax.rsqrt(var + LN_EPS)
    xhat = d * rstd
    return xhat, rstd, xhat * g + b


def _ln_bwd(dy, xhat, rstd, g):
    dxh = dy * g
    m1 = jnp.mean(dxh, axis=-1, keepdims=True)
    m2 = jnp.mean(dxh * xhat, axis=-1, keepdims=True)
    dh = rstd * (dxh - m1 - xhat * m2)
    return dh, jnp.sum(dy * xhat, axis=0, keepdims=True), jnp.sum(dy, axis=0, keepdims=True)


def _wide(col, ts):
    return jnp.broadcast_to(col, (ts, LANES))


def _pool_fwd(x, wp, pb, ps, g, b):
    S, D = x.shape
    gw = D // len(POOL_WINDOWS)
    ts = min(ROW_TILE, S)

    def fn(i, n, rv, hv, cv):
        (xc,), (xp,) = rv, hv
        wpv, pbv, psv, gv, bv = cv
        xp = jnp.where(i > 0, xp, 0.0)
        xx = jnp.concatenate([xp, xc], axis=0)
        t = i * ts + lax.broadcasted_iota(jnp.int32, (ts, 1), 0)
        pooled, ys = [], []
        for gi, w in enumerate(POOL_WINDOWS):
            s = xx[:, gi * gw : (gi + 1) * gw]
            k = 1
            while k < w:
                s = s + pltpu.roll(s, k, axis=0)
                k *= 2
            cnt = jnp.minimum(t + 1, w).astype(F32)
            pg = (s[POOL_HALO:, :] / cnt - xc[:, gi * gw : (gi + 1) * gw]).astype(BF16)
            pooled.append(pg)
            ys.append(jnp.dot(pg, wpv[gi], preferred_element_type=F32))
        y = jnp.concatenate(ys, axis=1)
        h = ALPHA * xc + (y + pbv) * psv
        xhat, rstd, xa = _ln(h, gv, bv)
        return (jnp.concatenate(pooled, axis=1), xhat, _wide(rstd, ts), xa), ()

    return _rowwise(
        "pool_fwd", fn, S, ts, rows=[x], halos=[(x, "prev", POOL_HALO)], consts=[wp, pb, ps, g, b],
        outs=[(D, BF16), (D, F32), (LANES, F32), (D, BF16)],
    )


def _pool_bwd(dh, pooled, wp, pb, ps):
    S, D = dh.shape
    gw = D // len(POOL_WINDOWS)
    ts = min(ROW_TILE, S)
    te = ts + POOL_HALO

    def fn(i, n, rv, hv, cv):
        (dhc, pc), (dhn,) = rv, hv
        wpv, pbv, psv = cv
        dhn = jnp.where(i < n - 1, dhn, 0.0)
        dy_ext = jnp.concatenate([dhc, dhn], axis=0) * psv
        dyb = dy_ext.astype(BF16)
        t = i * ts + lax.broadcasted_iota(jnp.int32, (te, 1), 0)
        dxs, ys = [], []
        for gi, w in enumerate(POOL_WINDOWS):
            sl = slice(gi * gw, (gi + 1) * gw)
            dp = lax.dot_general(dyb[:, sl], wpv[gi], _DIMS["nt"], preferred_element_type=F32)
            s = dp / jnp.minimum(t + 1, w).astype(F32)
            k = 1
            while k < w:
                s = s + pltpu.roll(s, k, axis=0)
                k *= 2
            s = pltpu.roll(s, POOL_HALO - (w - 1), axis=0)
            dxs.append(s[POOL_HALO:, :] - dp[:ts, :])
            ys.append(jnp.dot(pc[:, sl], wpv[gi], preferred_element_type=F32))
        dx = ALPHA * dhc + jnp.concatenate(dxs, axis=1)
        y = jnp.concatenate(ys, axis=1) + pbv
        dscale = jnp.sum(dhc * y, axis=0, keepdims=True)
        dbias = jnp.sum(dy_ext[:ts, :], axis=0, keepdims=True)
        return (dx, dyb[:ts, :]), (dscale, dbias)

    return _rowwise(
        "pool_bwd", fn, S, ts, rows=[dh, pooled], halos=[(dh, "next", POOL_HALO)], consts=[wp, pb, ps],
        outs=[(D, F32), (D, BF16)], accs=[(1, D), (1, D)],
    )


def _pool_dw(pooled, dy):
    S, D = pooled.shape
    G = len(POOL_WINDOWS)
    gw = D // G
    tk = min(MM_TK, S)
    nk = S // tk

    def body(p_ref, d_ref, o_ref):
        k = pl.program_id(1)
        part = lax.dot_general(p_ref[...], d_ref[...], _DIMS["tn"], preferred_element_type=F32)

        @pl.when(k == 0)
        def _():
            o_ref[...] = part

        @pl.when(k > 0)
        def _():
            o_ref[...] += part

    return pl.pallas_call(
        body,
        name="pool_dw",
        grid=(G, nk),
        in_specs=[pl.BlockSpec((tk, gw), lambda g, k: (k, g)), pl.BlockSpec((tk, gw), lambda g, k: (k, g))],
        out_specs=pl.BlockSpec((None, gw, gw), lambda g, k: (g, 0, 0)),
        out_shape=jax.ShapeDtypeStruct((G, gw, gw), F32),
        compiler_params=_params(2),
    )(pooled, dy)


def _res_ln_mix(name, xhat_p, mix, gp_, bp_, g, b):
    S, D = xhat_p.shape
    ts = min(ROW_TILE, S)

    def fn(i, n, rv, hv, cv):
        xh, m = rv
        gpv, bpv, gv, bv = cv
        xhat, rstd, xo = _ln(ALPHA * (xh * gpv + bpv) + m, gv, bv)
        return (xhat, _wide(rstd, ts), xo), ()

    return _rowwise(name, fn, S, ts, rows=[xhat_p, mix], consts=[gp_, bp_, g, b], outs=[(D, F32), (LANES, F32), (D, BF16)])


def _res_ln_ffpe(name, xhat_p, ff, gate, pp, gp_, bp_, g, b):
    S, D = xhat_p.shape
    ts = min(ROW_TILE, S)

    def fn(i, n, rv, hv, cv):
        xh, f, gt, p_ = rv
        gpv, bpv, gv, bv = cv
        xhat, rstd, xo = _ln(ALPHA * (xh * gpv + bpv) + f + jax.nn.sigmoid(gt) * p_, gv, bv)
        return (xhat, _wide(rstd, ts), xo), ()

    return _rowwise(name, fn, S, ts, rows=[xhat_p, ff, gate, pp], consts=[gp_, bp_, g, b], outs=[(D, F32), (LANES, F32), (D, BF16)])


def _final_ln_loss(xhat_p, ff, gate, pp, tgt, gp_, bp_, g, b):
    S, D = xhat_p.shape
    ts = min(ROW_TILE, S)

    def fn(i, n, rv, hv, cv):
        xh, f, gt, p_, tg = rv
        gpv, bpv, gv, bv = cv
        xhat, rstd, y = _ln(ALPHA * (xh * gpv + bpv) + f + jax.nn.sigmoid(gt) * p_, gv, bv)
        e = y - tg
        dh, dg, db = _ln_bwd(e * (1.0 / D), xhat, rstd, gv)
        return (dh, dh), (jnp.sum(e * e, axis=0, keepdims=True), dg, db)

    return _rowwise(
        "final_ln_loss", fn, S, ts, rows=[xhat_p, ff, gate, pp, tgt], consts=[gp_, bp_, g, b],
        outs=[(D, F32), (D, BF16)], accs=[(1, D), (1, D), (1, D)],
    )


def _ln_bwd_call(name, dy, xhat, rstd, g):
    S, D = dy.shape
    ts = min(ROW_TILE, S)

    def fn(i, n, rv, hv, cv):
        dyv, xh, rs = rv
        dh, dg, db = _ln_bwd(dyv, xh, rs[:, :1], cv[0])
        return (dh, dh), (dg, db)

    return _rowwise(name, fn, S, ts, rows=[dy, xhat, rstd], consts=[g], outs=[(D, F32), (D, BF16)], accs=[(1, D), (1, D)])


def _ple_bwd(name, dh, gate, pp):
    S, D = dh.shape
    ts = min(ROW_TILE, S)

    def fn(i, n, rv, hv, cv):
        d, gt, p_ = rv
        sg = jax.nn.sigmoid(gt)
        dgt = d * p_ * sg * (1.0 - sg)
        return (dgt, d * sg), (jnp.sum(dgt, axis=0, keepdims=True),)

    return _rowwise(name, fn, S, ts, rows=[dh, gate, pp], outs=[(D, BF16), (D, BF16)], accs=[(1, D)])


def _silu(c):
    return c * jax.nn.sigmoid(c)


def _qkv_point(c, is_qk, scale):
    s = _silu(c)
    nrm = s * lax.rsqrt(jnp.sum(s * s, axis=-1, keepdims=True) + L2_EPS) * scale
    return jnp.where(is_qk, nrm, s)


def _conv_rows(xx, wv, lo, rows):
    acc = None
    for j in range(CONV_WIDTH):
        sh = CONV_WIDTH - 1 - j
        term = (pltpu.roll(xx, sh, axis=0) if sh else xx)[lo : lo + rows, :] * wv[j : j + 1, :]
        acc = term if acc is None else acc + term
    return acc


def _conv_fwd(qkv_pre, conv_w):
    S, W = qkv_pre.shape
    D = W // 3
    H = D // HEAD_DIM
    ts = min(CONV_TILE, S)
    r = ts // CONV_HALO

    def body(x_ref, xp_ref, w_ref, o_ref):
        j, i = pl.program_id(0), pl.program_id(1)
        xp = jnp.where(i > 0, xp_ref[...], 0.0)
        xx = jnp.concatenate([xp, x_ref[...]], axis=0)
        c = _conv_rows(xx, w_ref[...], CONV_HALO, ts)
        scale = jnp.where(j == 0, HEAD_DIM**-0.5, 1.0).astype(F32)
        for h in range(H):
            sl = slice(h * HEAD_DIM, (h + 1) * HEAD_DIM)
            o_ref[:, sl] = _qkv_point(c[:, sl], j < 2, scale)

    return pl.pallas_call(
        body,
        name="gdn_conv_fwd",
        grid=(3, S // ts),
        in_specs=[
            pl.BlockSpec((ts, D), lambda j, i: (i, j)),
            pl.BlockSpec((CONV_HALO, D), lambda j, i: (jnp.maximum(i * r - 1, 0), j)),
            pl.BlockSpec((CONV_WIDTH, D), lambda j, i: (0, j)),
        ],
        out_specs=pl.BlockSpec((ts, D), lambda j, i: (i, j)),
        out_shape=jax.ShapeDtypeStruct((S, W), F32),
        compiler_params=_params(2),
    )(qkv_pre, qkv_pre, conv_w)


def _conv_bwd(qkv_pre, conv_w, dqkvn):
    S, W = qkv_pre.shape
    D = W // 3
    H = D // HEAD_DIM
    ts = min(CONV_TILE, S)
    r, nb = ts // CONV_HALO, S // CONV_HALO
    te = ts + CONV_HALO

    def body(x_ref, xp_ref, xn_ref, w_ref, d_ref, dn_ref, dx_ref, dw_ref):
        j, i = pl.program_id(0), pl.program_id(1)
        n = pl.num_programs(1)
        wv = w_ref[...]
        xp = jnp.where(i > 0, xp_ref[...], 0.0)
        xx = jnp.concatenate([xp, x_ref[...], xn_ref[...]], axis=0)
        c = _conv_rows(xx, wv, CONV_HALO, te)
        dn = jnp.where(i < n - 1, dn_ref[...], 0.0)
        dout = jnp.concatenate([d_ref[...], dn], axis=0)
        scale = jnp.where(j == 0, HEAD_DIM**-0.5, 1.0).astype(F32)
        dcs = []
        for h in range(H):
            sl = slice(h * HEAD_DIM, (h + 1) * HEAD_DIM)
            _, vjp = jax.vjp(lambda cc: _qkv_point(cc, j < 2, scale), c[:, sl])
            dcs.append(vjp(dout[:, sl])[0])
        dc = jnp.concatenate(dcs, axis=1)
        dx = None
        dws = []
        for jj in range(CONV_WIDTH):
            sh = CONV_WIDTH - 1 - jj
            term = pltpu.roll(dc, CONV_HALO - sh, axis=0)[CONV_HALO:, :] * wv[jj : jj + 1, :]
            dx = term if dx is None else dx + term
            xs = (pltpu.roll(xx, sh, axis=0) if sh else xx)[CONV_HALO : CONV_HALO + ts, :]
            dws.append(jnp.sum(dc[:ts, :] * xs, axis=0, keepdims=True))
        dx_ref[...] = dx.astype(dx_ref.dtype)
        dw = jnp.concatenate(dws, axis=0)

        @pl.when(i == 0)
        def _():
            dw_ref[...] = dw

        @pl.when(i > 0)
        def _():
            dw_ref[...] += dw

    return pl.pallas_call(
        body,
        name="gdn_conv_bwd",
        grid=(3, S // ts),
        in_specs=[
            pl.BlockSpec((ts, D), lambda j, i: (i, j)),
            pl.BlockSpec((CONV_HALO, D), lambda j, i: (jnp.maximum(i * r - 1, 0), j)),
            pl.BlockSpec((CONV_HALO, D), lambda j, i: (jnp.minimum((i + 1) * r, nb - 1), j)),
            pl.BlockSpec((CONV_WIDTH, D), lambda j, i: (0, j)),
            pl.BlockSpec((ts, D), lambda j, i: (i, j)),
            pl.BlockSpec((CONV_HALO, D), lambda j, i: (jnp.minimum((i + 1) * r, nb - 1), j)),
        ],
        out_specs=[pl.BlockSpec((ts, D), lambda j, i: (i, j)), pl.BlockSpec((CONV_WIDTH, D), lambda j, i: (0, j))],
        out_shape=[jax.ShapeDtypeStruct((S, W), BF16), jax.ShapeDtypeStruct((CONV_WIDTH, W), F32)],
        compiler_params=_params(2),
    )(qkv_pre, qkv_pre, qkv_pre, conv_w, dqkvn, dqkvn)


def _softplus(x):
    pos = x > 0.0
    return jnp.where(pos, x, 0.0) + jnp.log(1.0 + jnp.exp(jnp.where(pos, -x, x)))


def _gates(bl, al, alog, dt):
    return jax.nn.sigmoid(bl), -jnp.exp(alog) * _softplus(al + dt)


def _gates_fwd(ba, alog, dt):
    S = ba.shape[0]
    ts = min(ROW_TILE, S)

    def fn(i, n, rv, hv, cv):
        return _gates(rv[0][:, :LANES], rv[0][:, LANES:], cv[0], cv[1]), ()

    return _rowwise("gdn_gates_fwd", fn, S, ts, rows=[ba], consts=[alog, dt], outs=[(LANES, F32), (LANES, F32)])


def _gates_bwd(ba, alog, dt, dbeta, dg, H):
    S = ba.shape[0]
    ts = min(ROW_TILE, S)

    def fn(i, n, rv, hv, cv):
        bav, dbv, dgv = rv
        real = lax.broadcasted_iota(jnp.int32, (1, LANES), 1) < H
        _, vjp = jax.vjp(_gates, bav[:, :LANES], bav[:, LANES:], cv[0], cv[1])
        dbl, dal, dalog, ddt = vjp((jnp.where(real, dbv, 0.0), jnp.where(real, dgv, 0.0)))
        dbl, dal = jnp.where(real, dbl, 0.0), jnp.where(real, dal, 0.0)
        return (jnp.concatenate([dbl, dal], axis=1),), (jnp.where(real, dalog, 0.0), jnp.where(real, ddt, 0.0))

    return _rowwise(
        "gdn_gates_bwd", fn, S, ts, rows=[ba, dbeta, dg], consts=[alog, dt], outs=[(2 * LANES, BF16)],
        accs=[(1, LANES), (1, LANES)],
    )


def _split_bf16(a, n):
    parts, rest = [], a
    for _ in range(n):
        piece = rest.astype(BF16)
        parts.append(piece)
        rest = rest - piece.astype(F32)
    return parts


def _tri_dot(a, b, mode, tri):
    d = lambda u, v: lax.dot_general(u, v, _DIMS[mode], preferred_element_type=F32)
    if tri == 0:
        return sum(d(a.astype(BF16), piece) for piece in _split_bf16(b, 3))
    return sum(d(piece, b.astype(BF16)) for piece in _split_bf16(a, 3))


def _make_dot(exact):
    def raw(a, b, mode):
        if exact:
            a_hi, a_lo = _split_bf16(a, 2)
            b_hi, b_lo = _split_bf16(b, 2)
            d = lambda u, v: lax.dot_general(u, v, _DIMS[mode], preferred_element_type=F32)
            return d(a_hi, b_hi) + (d(a_hi, b_lo) + d(a_lo, b_hi))
        return lax.dot_general(a.astype(BF16), b.astype(BF16), _DIMS[mode], preferred_element_type=F32)

    @functools.partial(jax.custom_vjp, nondiff_argnums=(2,))
    def dot(a, b, mode):
        return raw(a, b, mode)

    def fwd(a, b, mode):
        return raw(a, b, mode), (a, b)

    def bwd(mode, res, ct):
        a, b = res
        if mode == "nn":
            return dot(ct, b, "nt"), dot(a, ct, "tn")
        if mode == "nt":
            return dot(ct, b, "nn"), dot(ct, a, "tn")
        return dot(b, ct, "nt"), dot(a, ct, "nn")

    dot.defvjp(fwd, bwd)
    return dot


_bdot = _make_dot(False)
_fdot = _make_dot(True)


@jax.custom_vjp
def _unit_lower_inverse(a_strict):
    return _unit_lower_inverse_raw(a_strict)


def _unit_lower_inverse_fwd(a_strict):
    t = _unit_lower_inverse_raw(a_strict)
    return t, t


def _unit_lower_inverse_bwd(t, ct):
    left = [_bdot(ti, ci, "tn") for ti, ci in zip(t, ct)]
    return (tuple(-_bdot(li, ti, "nt") for li, ti in zip(left, t)),)


_unit_lower_inverse.defvjp(_unit_lower_inverse_fwd, _unit_lower_inverse_bwd)


def _unit_lower_inverse_raw(a_strict):
    C = a_strict[0].shape[0]
    ii = lax.broadcasted_iota(jnp.int32, (C, C), 0)
    jj = lax.broadcasted_iota(jnp.int32, (C, C), 1)
    eye = (ii == jj).astype(F32)
    blk = 16
    same = (ii // blk) == (jj // blk)
    p = [-jnp.where(same, a, 0.0) for a in a_strict]
    t = [eye + x for x in p]
    for _ in range(3):
        p = [_fdot(x, x, "nn") for x in p]
        t = [ti + _fdot(ti, x, "nn") for ti, x in zip(t, p)]
    while blk < C:
        same2 = (ii // (2 * blk)) == (jj // (2 * blk))
        off = jnp.logical_and(same2, jnp.logical_not(same))
        te = [_fdot(ti, jnp.where(off, a, 0.0), "nn") for ti, a in zip(t, a_strict)]
        t = [ti - _fdot(x, ti, "nn") for ti, x in zip(t, te)]
        same, blk = same2, 2 * blk
    return tuple(t)


def _chunk_heads(q, k, v, gc_col, gc_row, b_col, s0):
    R = range(len(q))
    C = q[0].shape[0]
    ii = lax.broadcasted_iota(jnp.int32, (C, C), 0)
    jj = lax.broadcasted_iota(jnp.int32, (C, C), 1)
    rows = lax.broadcasted_iota(jnp.int32, (C, 1), 0)
    decay = [jnp.where(ii >= jj, jnp.exp(jnp.minimum(gc_col[h] - gc_row[h], 0.0)), 0.0) for h in R]
    kb = [k[h] * b_col[h] for h in R]
    a = [_bdot(kb[h], k[h], "nt") * decay[h] for h in R]
    qk = [_bdot(q[h], k[h], "nt") * decay[h] for h in R]
    t = _unit_lower_inverse(tuple(jnp.where(ii > jj, a[h], 0.0) for h in R))
    eg = [jnp.exp(gc_col[h]) for h in R]
    u = [_bdot(t[h], v[h] * b_col[h], "nn") for h in R]
    w = [_bdot(t[h], kb[h] * eg[h], "nn") for h in R]
    g_last = [jnp.sum(jnp.where(rows == C - 1, gc_col[h], 0.0), axis=0, keepdims=True) for h in R]
    kd = [k[h] * jnp.exp(g_last[h] - gc_col[h]) for h in R]
    ws = [_bdot(w[h], s0[h], "nn") for h in R]
    qs = [_bdot(q[h] * eg[h], s0[h], "nn") for h in R]
    v_new = [u[h] - ws[h] for h in R]
    o = [qs[h] + _bdot(qk[h], v_new[h], "nn") for h in R]
    s1 = [s0[h] * jnp.exp(g_last[h]) + _bdot(kd[h], v_new[h], "tn") for h in R]
    return tuple(o), tuple(s1)


def _pick_lane(a, h):
    lanes = lax.broadcasted_iota(jnp.int32, a.shape, 1)
    return jnp.sum(jnp.where(lanes == h, a, 0.0), axis=1, keepdims=True)


def _pick_row(a, h):
    rows = lax.broadcasted_iota(jnp.int32, a.shape, 0)
    return jnp.sum(jnp.where(rows == h, a, 0.0), axis=0, keepdims=True)


def _tri(C):
    ii = lax.broadcasted_iota(jnp.int32, (C, C), 0)
    jj = lax.broadcasted_iota(jnp.int32, (C, C), 1)
    return (ii >= jj).astype(F32)


def _delta_fwd(qkvn, g_pad, g_rows, beta_pad):
    S, W = qkvn.shape
    D = W // 3
    H = D // HEAD_DIM
    C = min(CHUNK, S)
    N = S // C

    def body(x_ref, gp_ref, gr_ref, bp_ref, o_ref, sall_ref, st):
        n = pl.program_id(0)

        @pl.when(n == 0)
        def _():
            st[...] = jnp.zeros_like(st)

        low = _tri(C)
        gc_cols = _tri_dot(low, gp_ref[...], "nn", 0)
        gc_rows = _tri_dot(gr_ref[...], low, "nt", 1)
        bcols = bp_ref[...]
        hs = range(H)
        s0 = tuple(st[h] for h in hs)
        for h in hs:
            sall_ref[h] = s0[h]
        o, s1 = _chunk_heads(
            tuple(x_ref[:, h * HEAD_DIM : (h + 1) * HEAD_DIM] for h in hs),
            tuple(x_ref[:, D + h * HEAD_DIM : D + (h + 1) * HEAD_DIM] for h in hs),
            tuple(x_ref[:, 2 * D + h * HEAD_DIM : 2 * D + (h + 1) * HEAD_DIM] for h in hs),
            tuple(_pick_lane(gc_cols, h) for h in hs), tuple(_pick_row(gc_rows, h) for h in hs),
            tuple(_pick_lane(bcols, h) for h in hs), s0,
        )
        for h in hs:
            st[h] = s1[h]
            o_ref[:, h * HEAD_DIM : (h + 1) * HEAD_DIM] = o[h]

    return pl.pallas_call(
        body,
        name="gdn_delta_fwd",
        grid=(N,),
        in_specs=[
            pl.BlockSpec((C, W), lambda n: (n, 0)),
            pl.BlockSpec((C, LANES), lambda n: (n, 0)),
            pl.BlockSpec((None, 8, C), lambda n: (n, 0, 0)),
            pl.BlockSpec((C, LANES), lambda n: (n, 0)),
        ],
        out_specs=[pl.BlockSpec((C, D), lambda n: (n, 0)), pl.BlockSpec((None, H, HEAD_DIM, HEAD_DIM), lambda n: (n, 0, 0, 0))],
        out_shape=[jax.ShapeDtypeStruct((S, D), F32), jax.ShapeDtypeStruct((N, H, HEAD_DIM, HEAD_DIM), F32)],
        scratch_shapes=[pltpu.VMEM((H, HEAD_DIM, HEAD_DIM), F32)],
        compiler_params=_params(1),
    )(qkvn, g_pad, g_rows, beta_pad)


def _delta_bwd(qkvn, g_pad, g_rows, beta_pad, s_all, do):
    S, W = qkvn.shape
    D = W // 3
    H = D // HEAD_DIM
    C = min(CHUNK, S)
    N = S // C

    def body(x_ref, gp_ref, gr_ref, bp_ref, sall_ref, do_ref, dx_ref, dgp_ref, dgr_ref, dbp_ref, dst):
        n = pl.program_id(0)

        @pl.when(n == 0)
        def _():
            dst[...] = jnp.zeros_like(dst)

        low = _tri(C)
        gc_cols = _tri_dot(low, gp_ref[...], "nn", 0)
        gc_rows = _tri_dot(gr_ref[...], low, "nt", 1)
        bcols = bp_ref[...]
        lane = lax.broadcasted_iota(jnp.int32, (1, LANES), 1)
        row8 = lax.broadcasted_iota(jnp.int32, (8, 1), 0)
        dgc_cols = jnp.zeros((C, LANES), F32)
        dgc_rows = jnp.zeros((8, C), F32)
        dbcols = jnp.zeros((C, LANES), F32)
        hs = range(H)
        _, vjp = jax.vjp(
            _chunk_heads,
            tuple(x_ref[:, h * HEAD_DIM : (h + 1) * HEAD_DIM] for h in hs),
            tuple(x_ref[:, D + h * HEAD_DIM : D + (h + 1) * HEAD_DIM] for h in hs),
            tuple(x_ref[:, 2 * D + h * HEAD_DIM : 2 * D + (h + 1) * HEAD_DIM] for h in hs),
            tuple(_pick_lane(gc_cols, h) for h in hs), tuple(_pick_row(gc_rows, h) for h in hs),
            tuple(_pick_lane(bcols, h) for h in hs), tuple(sall_ref[h] for h in hs),
        )
        dq, dk, dv, dgc, dgr, dbc, ds0 = vjp((tuple(do_ref[:, h * HEAD_DIM : (h + 1) * HEAD_DIM] for h in hs), tuple(dst[h] for h in hs)))
        for h in hs:
            dst[h] = ds0[h]
            dx_ref[:, h * HEAD_DIM : (h + 1) * HEAD_DIM] = dq[h]
            dx_ref[:, D + h * HEAD_DIM : D + (h + 1) * HEAD_DIM] = dk[h]
            dx_ref[:, 2 * D + h * HEAD_DIM : 2 * D + (h + 1) * HEAD_DIM] = dv[h]
            dgc_cols = dgc_cols + dgc[h] * (lane == h).astype(F32)
            dgc_rows = dgc_rows + dgr[h] * (row8 == h).astype(F32)
            dbcols = dbcols + dbc[h] * (lane == h).astype(F32)
        dgp_ref[...] = _tri_dot(low, dgc_cols, "tn", 0)
        dgr_ref[...] = _tri_dot(dgc_rows, low, "nn", 1)
        dbp_ref[...] = dbcols

    rev = lambda n: N - 1 - n
    return pl.pallas_call(
        body,
        name="gdn_delta_bwd",
        grid=(N,),
        in_specs=[
            pl.BlockSpec((C, W), lambda n: (rev(n), 0)),
            pl.BlockSpec((C, LANES), lambda n: (rev(n), 0)),
            pl.BlockSpec((None, 8, C), lambda n: (rev(n), 0, 0)),
            pl.BlockSpec((C, LANES), lambda n: (rev(n), 0)),
            pl.BlockSpec((None, H, HEAD_DIM, HEAD_DIM), lambda n: (rev(n), 0, 0, 0)),
            pl.BlockSpec((C, D), lambda n: (rev(n), 0)),
        ],
        out_specs=[
            pl.BlockSpec((C, W), lambda n: (rev(n), 0)),
            pl.BlockSpec((C, LANES), lambda n: (rev(n), 0)),
            pl.BlockSpec((None, 8, C), lambda n: (rev(n), 0, 0)),
            pl.BlockSpec((C, LANES), lambda n: (rev(n), 0)),
        ],
        out_shape=[
            jax.ShapeDtypeStruct((S, W), F32),
            jax.ShapeDtypeStruct((S, LANES), F32),
            jax.ShapeDtypeStruct((N, 8, C), F32),
            jax.ShapeDtypeStruct((S, LANES), F32),
        ],
        scratch_shapes=[pltpu.VMEM((H, HEAD_DIM, HEAD_DIM), F32)],
        compiler_params=_params(1),
    )(qkvn, g_pad, g_rows, beta_pad, s_all, do)


def _gate_norm_head(o, z, nw):
    return o * lax.rsqrt(jnp.mean(o * o, axis=-1, keepdims=True) + RMS_EPS) * nw * _silu(z)


def _gate_norm_fwd(o, z, nw):
    S, D = o.shape
    H = D // HEAD_DIM
    ts = min(ROW_TILE, S)

    def fn(i, n, rv, hv, cv):
        ov, zv = rv
        parts = [_gate_norm_head(ov[:, h * HEAD_DIM : (h + 1) * HEAD_DIM], zv[:, h * HEAD_DIM : (h + 1) * HEAD_DIM], cv[0]) for h in range(H)]
        return (jnp.concatenate(parts, axis=1),), ()

    return _rowwise("gdn_gate_norm_fwd", fn, S, ts, rows=[o, z], consts=[nw], outs=[(D, BF16)])[0]


def _gate_norm_bwd(dog, o, z, nw):
    S, D = o.shape
    H = D // HEAD_DIM
    ts = min(ROW_TILE, S)

    def fn(i, n, rv, hv, cv):
        dv, ov, zv = rv
        dos, dzs, dnw = [], [], None
        for h in range(H):
            sl = slice(h * HEAD_DIM, (h + 1) * HEAD_DIM)
            _, vjp = jax.vjp(_gate_norm_head, ov[:, sl], zv[:, sl], cv[0])
            a, b_, c_ = vjp(dv[:, sl])
            dos.append(a)
            dzs.append(b_)
            dnw = c_ if dnw is None else dnw + c_
        return (jnp.concatenate(dos, axis=1), jnp.concatenate(dzs, axis=1)), (dnw,)

    return _rowwise("gdn_gate_norm_bwd", fn, S, ts, rows=[dog, o, z], consts=[nw], outs=[(D, F32), (D, BF16)], accs=[(1, HEAD_DIM)])


def _square_bf16(r):
    rf = r.astype(F32)
    return rf * rf


def _mlp_ple_dw(li, dh, dhb, xa, p, r, gate, pp, w2):
    dpre = _mm(f"l{li}_mlp_down_bwd", dhb, w2, "nt", [BF16], epi=lambda acc, rr: (acc * (2.0 * rr.astype(F32)),), extras=[(r, "tile")], tm=1024, tn=1024, b_outer=True)
    dw2 = _mm(f"l{li}_mlp_dw2", r, dhb, "tn", [F32], a_fn=_square_bf16)
    dgate, dpp, dbg = _ple_bwd(f"l{li}_ple_bwd", dh, gate, pp)
    dw1 = _mm(f"l{li}_mlp_dw1", xa, dpre, "tn", [F32])
    dwg = _mm(f"l{li}_ple_dwg", xa, dgate, "tn", [F32])
    dwp = _mm(f"l{li}_ple_dwp", p, dpp, "tn", [F32])
    return dpre, dgate, dw1, dw2, dwg, dbg, dwp


def _mlp_ple_dx(li, dh, dpre, dgate, w1, wg):
    t = _mm(f"l{li}_ple_gate_bwd", dgate, wg, "nt", [F32], epi=lambda acc, d: (acc + ALPHA * d,), extras=[(dh, "tile")], tm=1024)
    return _mm(f"l{li}_mlp_up_bwd", dpre, w1, "nt", [F32], epi=lambda acc, d: (acc + d,), extras=[(t, "tile")], tm=256, tk=4096)


def _local_step(x, p, tgt, W, fetch, emit):
    S, D = x.shape
    H = D // HEAD_DIM
    C = min(CHUNK, S)
    N = S // C
    lg = lambda i, j: W["ln_gain"][2 * i + j][None, :]
    lb = lambda i, j: W["ln_bias"][2 * i + j][None, :]
    G = {}

    pooled, xh0a, rs0a, x0a = _pool_fwd(x, W["pool_w"], W["pool_b"], W["pool_scale"], lg(0, 0), lb(0, 0))
    w0a = fetch("l0a", x0a)
    r0 = _mm("l0_mlp_up", x0a, w0a["mlp_w1"], "nn", [BF16], epi=lambda acc: (jnp.maximum(acc, 0.0),), tm=1024, tn=1024, b_outer=True)
    w0b = fetch("l0b", r0)
    ff0 = _mm("l0_mlp_down", r0, w0b["mlp_w2"], "nn", [F32], a_fn=_square_bf16, tm=256, tk=4096)
    gate0 = _mm("l0_ple_gate", x0a, w0b["ple_gate_w"], "nn", [F32], epi=lambda acc, bias: (acc + bias,), extras=[(W["ple_gate_b"][0:1], "row")], tm=1024)
    pp0 = _mm("l0_ple_proj", p[0], w0b["ple_proj"], "nn", [F32])
    xh0b, rs0b, x0b = _res_ln_ffpe("l0_ln_b", xh0a, ff0, gate0, pp0, lg(0, 0), lb(0, 0), lg(0, 1), lb(0, 1))

    wg_ = fetch("gdn", x0b)
    qkv_pre = _mm("gdn_in_qkv", x0b, wg_["gdn_wqkv"], "nn", [F32], tm=1024, tn=1024, b_outer=True)
    z = _mm("gdn_in_z", x0b, wg_["gdn_wz"], "nn", [F32], tm=1024)
    ba = _mm("gdn_in_ba", x0b, wg_["gdn_wba"], "nn", [F32])
    qkvn = _conv_fwd(qkv_pre, W["gdn_conv"])
    beta_pad, g_pad = _gates_fwd(ba, W["gdn_a_log"], W["gdn_dt_bias"])
    g_rows = g_pad[:, :8].reshape(N, C, 8).transpose(0, 2, 1)
    o, s_all = _delta_fwd(qkvn, g_pad, g_rows, beta_pad)
    og = _gate_norm_fwd(o, z, W["gdn_norm_w"])
    mix1 = _mm("gdn_out", og, wg_["gdn_w_out"], "nn", [F32], tm=1024)
    xh1a, rs1a, x1a = _res_ln_mix("l1_ln_a", xh0b, mix1, lg(0, 1), lb(0, 1), lg(1, 0), lb(1, 0))
    w1_ = fetch("l1", x1a)
    r1 = _mm("l1_mlp_up", x1a, w1_["mlp_w1"], "nn", [BF16], epi=lambda acc: (jnp.maximum(acc, 0.0),), tm=1024, tn=1024, b_outer=True)
    ff1 = _mm("l1_mlp_down", r1, w1_["mlp_w2"], "nn", [F32], a_fn=_square_bf16, tm=256, tk=4096)
    gate1 = _mm("l1_ple_gate", x1a, w1_["ple_gate_w"], "nn", [F32], epi=lambda acc, bias: (acc + bias,), extras=[(W["ple_gate_b"][1:2], "row")], tm=1024)
    pp1 = _mm("l1_ple_proj", p[1], w1_["ple_proj"], "nn", [F32])
    dh1b, dh1b_b, loss_cols, dg11, db11 = _final_ln_loss(xh1a, ff1, gate1, pp1, tgt, lg(1, 0), lb(1, 0), lg(1, 1), lb(1, 1))

    dpre1, dgate1, dw1_1, dw2_1, dwg_1, dbg_1, dwp_1 = _mlp_ple_dw(1, dh1b, dh1b_b, x1a, p[1], r1, gate1, pp1, w1_["mlp_w2"])
    tok = emit("l1", {"mlp_w1": dw1_1, "mlp_w2": dw2_1, "ple_gate_w": dwg_1, "ple_proj": dwp_1})
    dx1a = _mlp_ple_dx(1, dh1b, dpre1, dgate1, w1_["mlp_w1"], w1_["ple_gate_w"] + tok.astype(BF16))
    dh1a, dh1a_b, dg10, db10 = _ln_bwd_call("l1_ln_a_bwd", dx1a, xh1a, rs1a, lg(1, 0))
    dog = _mm("gdn_out_bwd", dh1a_b, wg_["gdn_w_out"], "nt", [F32], tm=1024)
    dw_out = _mm("gdn_dw_out", og, dh1a_b, "tn", [F32])
    do, dz, dnw = _gate_norm_bwd(dog, o, z, W["gdn_norm_w"])
    dqkvn, dg_col, dg_row, dbeta = _delta_bwd(qkvn, g_pad, g_rows, beta_pad, s_all, do)
    dg_all = dg_col + jnp.pad(dg_row.transpose(0, 2, 1).reshape(S, 8), ((0, 0), (0, LANES - 8)))
    dba, dalog, ddt = _gates_bwd(ba, W["gdn_a_log"], W["gdn_dt_bias"], dbeta, dg_all, H)
    dqkv, dconv = _conv_bwd(qkv_pre, W["gdn_conv"], dqkvn)
    dwqkv = _mm("gdn_dwqkv", x0b, dqkv, "tn", [F32])
    dwz = _mm("gdn_dwz", x0b, dz, "tn", [F32])
    dwba = _mm("gdn_dwba", x0b, dba, "tn", [F32])
    dw_in = jnp.concatenate([dwqkv, dwz, dwba[:, :H], dwba[:, LANES : LANES + H]], axis=1)
    tok = emit("gdn", {"gdn_w_in": dw_in, "gdn_w_out": dw_out})
    t = _mm("gdn_in_ba_bwd", dba, wg_["gdn_wba"] + tok.astype(BF16), "nt", [F32], epi=lambda acc, d: (acc + ALPHA * d,), extras=[(dh1a, "tile")])
    t = _mm("gdn_in_z_bwd", dz, wg_["gdn_wz"], "nt", [F32], epi=lambda acc, d: (acc + d,), extras=[(t, "tile")], tm=1024)
    dx0b = _mm("gdn_in_qkv_bwd", dqkv, wg_["gdn_wqkv"], "nt", [F32], epi=lambda acc, d: (acc + d,), extras=[(t, "tile")], tm=256, tk=3072)

    dh0b, dh0b_b, dg01, db01 = _ln_bwd_call("l0_ln_b_bwd", dx0b, xh0b, rs0b, lg(0, 1))
    dpre0, dgate0, dw1_0, dw2_0, dwg_0, dbg_0, dwp_0 = _mlp_ple_dw(0, dh0b, dh0b_b, x0a, p[0], r0, gate0, pp0, w0b["mlp_w2"])
    tok = emit("l0", {"mlp_w1": dw1_0, "mlp_w2": dw2_0, "ple_gate_w": dwg_0, "ple_proj": dwp_0})
    dx0a = _mlp_ple_dx(0, dh0b, dpre0, dgate0, w0a["mlp_w1"], w0b["ple_gate_w"] + tok.astype(BF16))
    dh0a, _, dg00, db00 = _ln_bwd_call("l0_ln_a_bwd", dx0a, xh0a, rs0a, lg(0, 0))
    grad_x, dyp, dscale, dpb = _pool_bwd(dh0a, pooled, W["pool_w"], W["pool_b"], W["pool_scale"])
    G["pool_w"] = _pool_dw(pooled, dyp)

    G["ln_gain"] = jnp.concatenate([dg00, dg01, dg10, dg11], axis=0)
    G["ln_bias"] = jnp.concatenate([db00, db01, db10, db11], axis=0)
    G["pool_b"] = dpb
    G["pool_scale"] = dscale
    G["gdn_conv"] = dconv
    G["gdn_a_log"] = dalog[:, :H]
    G["gdn_dt_bias"] = ddt[:, :H]
    G["gdn_norm_w"] = dnw
    G["ple_gate_b"] = jnp.concatenate([dbg_0, dbg_1], axis=0)
    return loss_cols, grad_x, G


_HBM = pl.BlockSpec(memory_space=pltpu.HBM)


def _all_gather(name, shards):
    T = len(shards)

    def body(*refs):
        ins, outs = refs[:T], refs[T : 2 * T]
        send_sems, recv_sems, local_sems = refs[2 * T :]
        x, y, c = lax.axis_index("x"), lax.axis_index("y"), lax.axis_index("c")
        me, sibling = (x, y, c), (x, y, 1 - c)
        chips = [(1 - x, y), (x, 1 - y), (1 - x, 1 - y)]

        def blk(t, px, py, pc):
            return outs[t].at[4 * px + 2 * py + pc]

        def copy(t, k, block, to, src=None):
            return pltpu.make_async_remote_copy(
                src_ref=blk(t, *block) if src is None else src, dst_ref=blk(t, *block),
                send_sem=send_sems.at[t, k], recv_sem=recv_sems.at[t, k], device_id=to, device_id_type=MESH,
            )

        mine = [pltpu.make_async_copy(ins[t], blk(t, *me), local_sems.at[t]) for t in range(T)]
        for cp in mine:
            cp.start()
        first = []
        for t in range(T):
            first.append(copy(t, 0, me, sibling, src=ins[t]))
            first += [copy(t, 1 + j, me, (*chip, c), src=ins[t]) for j, chip in enumerate(chips)]
        for cp in first:
            cp.start()
        passed = []
        for j, chip in enumerate(chips):
            for t in range(T):
                copy(t, 1 + j, (*chip, c), me).wait_recv()
                fw = copy(t, 4 + j, (*chip, c), sibling)
                fw.start()
                passed.append(fw)
        for t in range(T):
            copy(t, 0, sibling, me).wait_recv()
            for j, chip in enumerate(chips):
                copy(t, 4 + j, (*chip, 1 - c), me).wait_recv()
        for cp in first + passed:
            cp.wait_send()
        for cp in mine:
            cp.wait()

    return pl.pallas_call(
        body,
        name=name,
        in_specs=[_HBM] * T,
        out_specs=[_HBM] * T,
        out_shape=[jax.ShapeDtypeStruct((N_DEV,) + s.shape, s.dtype) for s in shards],
        scratch_shapes=[pltpu.SemaphoreType.DMA((T, 7)), pltpu.SemaphoreType.DMA((T, 7)), pltpu.SemaphoreType.DMA((T,))],
    )(*shards)


def _exchange(name, blocks):
    def body(g_ref, o_ref, send_sems, recv_sems, local_sem):
        x, y, c = lax.axis_index("x"), lax.axis_index("y"), lax.axis_index("c")
        own = pltpu.make_async_copy(g_ref.at[4 * x + 2 * y + c], o_ref.at[N_DEV - 1], local_sem)
        own.start()
        copies = []
        for rel in range(1, N_DEV):
            px = 1 - x if rel & 4 else x
            py = 1 - y if rel & 2 else y
            pc = 1 - c if rel & 1 else c
            copies.append(
                pltpu.make_async_remote_copy(
                    src_ref=g_ref.at[4 * px + 2 * py + pc], dst_ref=o_ref.at[rel - 1],
                    send_sem=send_sems.at[rel - 1], recv_sem=recv_sems.at[rel - 1], device_id=(px, py, pc), device_id_type=MESH,
                )
            )
        for cp in copies:
            cp.start()
        for cp in copies:
            cp.wait_recv()
        for cp in copies:
            cp.wait_send()
        own.wait()

    return pl.pallas_call(
        body,
        name=name,
        in_specs=[_HBM],
        out_specs=_HBM,
        out_shape=jax.ShapeDtypeStruct(blocks.shape, blocks.dtype),
        scratch_shapes=[pltpu.SemaphoreType.DMA((N_DEV - 1,)), pltpu.SemaphoreType.DMA((N_DEV - 1,)), pltpu.SemaphoreType.DMA],
    )(blocks)


_SEM = pl.BlockSpec(memory_space=pltpu.SEMAPHORE)
_ANY = pl.BlockSpec(memory_space=pl.ANY)
_DATAFLOW = pltpu.SideEffectType.DATAFLOW_SIDE_EFFECTING
N_PEERS = N_DEV - 1


def _peer(rel, x, y, c):
    return (1 - x if rel & 4 else x, 1 - y if rel & 2 else y, 1 - c if rel & 1 else c)


def _send_start(name, src, land, gather):
    def body(src_ref, land_ref, send_sems, recv_sems, src_thru, land_thru, token):
        x, y, c = lax.axis_index("x"), lax.axis_index("y"), lax.axis_index("c")
        for rel in range(1, N_DEV):
            px, py, pc = _peer(rel, x, y, c)
            pltpu.make_async_remote_copy(
                src_ref=src_ref if gather else src_ref.at[4 * px + 2 * py + pc],
                dst_ref=land_ref.at[4 * x + 2 * y + c] if gather else land_ref.at[rel - 1],
                send_sem=send_sems.at[rel - 1], recv_sem=recv_sems.at[rel - 1], device_id=(px, py, pc), device_id_type=MESH,
            ).start()
        token[...] = jnp.zeros_like(token)

    return pl.pallas_call(
        body,
        name=name,
        out_shape=(pltpu.SemaphoreType.DMA((N_PEERS,)), pltpu.SemaphoreType.DMA((N_PEERS,)), pltpu.HBM(src.shape, src.dtype),
                   pltpu.HBM(land.shape, land.dtype), jax.ShapeDtypeStruct((8, LANES), F32)),
        in_specs=(_HBM, _HBM),
        out_specs=(_SEM, _SEM, _HBM, _HBM, pl.BlockSpec(memory_space=pltpu.VMEM)),
        input_output_aliases={0: 2, 1: 3},
        compiler_params=pltpu.CompilerParams(has_side_effects=_DATAFLOW),
    )(pltpu.with_memory_space_constraint(src, pltpu.HBM), pltpu.with_memory_space_constraint(land, pltpu.HBM))


def _send_wait(name, started, after, gather):
    send_sems, recv_sems, src_thru, land_thru, _ = started

    def body(src_ref, land_ref, send_sems, recv_sems, after_ref, src_dead, got_ref):
        x, y, c = lax.axis_index("x"), lax.axis_index("y"), lax.axis_index("c")
        for rel in range(1, N_DEV):
            cp = pltpu.make_async_remote_copy(
                src_ref=src_ref if gather else src_ref.at[0], dst_ref=land_ref.at[0],
                send_sem=send_sems.at[rel - 1], recv_sem=recv_sems.at[rel - 1], device_id=_peer(rel, x, y, c), device_id_type=MESH,
            )
            cp.wait_send()
            cp.wait_recv()

    return pl.pallas_call(
        body,
        name=name,
        out_shape=(pltpu.HBM(src_thru.shape, src_thru.dtype), pltpu.HBM(land_thru.shape, land_thru.dtype)),
        in_specs=(_HBM, _HBM, _SEM, _SEM, _ANY),
        out_specs=(_HBM, _HBM),
        input_output_aliases={0: 0, 1: 1},
        compiler_params=pltpu.CompilerParams(has_side_effects=_DATAFLOW),
    )(src_thru, land_thru, send_sems, recv_sems, after)


def _sum_blocks(name, parts, tr):
    n_parts, R, Cw = parts.shape
    tr = tr if R % tr == 0 else R

    def body(p_ref, o_ref):
        acc = p_ref[0].astype(F32)
        for d in range(1, n_parts):
            acc = acc + p_ref[d].astype(F32)
        o_ref[...] = acc

    return pl.pallas_call(
        body,
        name=name,
        grid=(R // tr,),
        in_specs=[pl.BlockSpec((n_parts, tr, Cw), lambda i: (0, i, 0))],
        out_specs=pl.BlockSpec((tr, Cw), lambda i: (i, 0)),
        out_shape=jax.ShapeDtypeStruct((R, Cw), F32),
        compiler_params=_params(1),
    )(parts)


def _adamw(name, w, g, m, v):
    shape = w.shape
    cols = shape[-1]
    rows = w.size // cols
    tr = rows if rows <= 512 else 512
    assert rows % tr == 0
    w2, g2, m2, v2 = (a.reshape(rows, cols) for a in (w, g, m, v))

    def body(w_ref, g_ref, m_ref, v_ref, d_ref, mo_ref, vo_ref):
        gv = g_ref[...]
        mn = ADAM_B1 * m_ref[...] + (1.0 - ADAM_B1) * gv
        vn = ADAM_B2 * v_ref[...] + (1.0 - ADAM_B2) * jnp.square(gv)
        m_hat = mn / (1.0 - ADAM_B1**ADAM_STEP)
        v_hat = vn / (1.0 - ADAM_B2**ADAM_STEP)
        d_ref[...] = -ADAM_LR * (m_hat / (jnp.sqrt(v_hat) + ADAM_EPS) + ADAM_WD * w_ref[...])
        mo_ref[...] = mn
        vo_ref[...] = vn

    spec = pl.BlockSpec((tr, cols), lambda i: (i, 0))
    d, mn, vn = pl.pallas_call(
        body,
        name=name,
        grid=(rows // tr,),
        in_specs=[spec] * 4,
        out_specs=[spec] * 3,
        out_shape=[jax.ShapeDtypeStruct((rows, cols), F32)] * 3,
        compiler_params=_params(1),
    )(w2, g2, m2, v2)
    return d.reshape(shape), mn.reshape(shape), vn.reshape(shape)


SMALL_SHARDED = ("ln_gain", "ln_bias", "pool_b", "gdn_conv")
SMALL_REPLICATED = ("pool_scale", "gdn_a_log", "gdn_dt_bias", "gdn_norm_w", "ple_gate_b")
WEIGHTS = ("ln_gain", "ln_bias", "pool_w", "pool_b", "pool_scale", "gdn_w_in", "gdn_conv", "gdn_a_log", "gdn_dt_bias",
           "gdn_norm_w", "gdn_w_out", "mlp_w1", "mlp_w2", "ple_gate_w", "ple_gate_b", "ple_proj")
BIG_AXIS = {"gdn_w_in": 1, "gdn_w_out": 0, "mlp_w1": 1, "mlp_w2": 0, "ple_gate_w": 0, "ple_proj": 1, "pool_w": 1}
GATHER_GROUPS = {
    "l0a": (("mlp_w1", 0),),
    "l0b": (("mlp_w2", 0), ("ple_gate_w", 0), ("ple_proj", 0)),
    "gdn": (("gdn_w_in", 0), ("gdn_w_out", 0)),
    "l1": (("mlp_w1", 1), ("mlp_w2", 1), ("ple_gate_w", 1), ("ple_proj", 1)),
}
GRAD_GROUPS = {
    "l1": (("mlp_w1", 1), ("mlp_w2", 1), ("ple_gate_w", 1), ("ple_proj", 1)),
    "gdn": (("gdn_w_in", 0), ("gdn_w_out", 0)),
    "l0": (("mlp_w1", 0), ("mlp_w2", 0), ("ple_gate_w", 0), ("ple_proj", 0)),
}
PACK_PART_ALIGN = 16
SUM_TILE = 128


def _part_rows(a, width):
    rows = a.size // width
    return rows + (-rows) % PACK_PART_ALIGN


def _pack_rows(parts, width, dtype, align):
    padded = []
    for a in parts:
        a2 = a.reshape(-1, width).astype(dtype)
        padded.append(jnp.pad(a2, ((0, _part_rows(a, width) - a2.shape[0]), (0, 0))))
    flat = jnp.concatenate(padded, axis=0)
    return jnp.pad(flat, ((0, (-flat.shape[0]) % align), (0, 0)))


def _unpack_rows(packed, shapes, width):
    out, off = [], 0
    for shp in shapes:
        size = 1
        for d in shp:
            size *= d
        out.append(packed[..., off : off + size // width, :].reshape(packed.shape[:-2] + tuple(shp)))
        off += size // width + (-(size // width)) % PACK_PART_ALIGN
    return out


def _split_blocks(name, full):
    ax = BIG_AXIS[name]
    shp = full.shape
    a = full.reshape(shp[:ax] + (N_DEV, shp[ax] // N_DEV) + shp[ax + 1 :])
    return jnp.moveaxis(a, ax, 0)


def _join_blocks(name, blocks):
    ax = BIG_AXIS[name]
    a = jnp.moveaxis(blocks, 0, ax)
    shp = a.shape
    return a.reshape(shp[:ax] + (shp[ax] * shp[ax + 1],) + shp[ax + 2 :])


def _pack_small(parts):
    flat = jnp.concatenate([jnp.pad(a.reshape(-1), (0, (-a.size) % LANES)) for a in parts])
    rows = flat.size // LANES
    return jnp.pad(flat.reshape(rows, LANES), ((0, (-rows) % 8), (0, 0)))


def _unpack_small(packed, shapes):
    flat = packed.reshape(packed.shape[:-2] + (-1,))
    out, off = [], 0
    for shp in shapes:
        size = 1
        for s in shp:
            size *= s
        out.append(flat[..., off : off + size].reshape(flat.shape[:-1] + tuple(shp)))
        off += size + (-size) % LANES
    return out


def _split_w_in(w_in, D, H):
    pad = lambda a: jnp.pad(a, ((0, 0), (0, LANES - H)))
    return w_in[:, : 3 * D], w_in[:, 3 * D : 4 * D], jnp.concatenate([pad(w_in[:, 4 * D : 4 * D + H]), pad(w_in[:, 4 * D + H :])], axis=1)


def kernel(x, p, ln_gain, ln_bias, pool_w, pool_b, pool_scale, gdn_w_in, gdn_conv, gdn_a_log, gdn_dt_bias, gdn_norm_w, gdn_w_out, mlp_w1, mlp_w2, ple_gate_w, ple_gate_b, ple_proj, loss_target, m_ln_gain, m_ln_bias, m_pool_w, m_pool_b, m_pool_scale, m_gdn_w_in, m_gdn_conv, m_gdn_a_log, m_gdn_dt_bias, m_gdn_norm_w, m_gdn_w_out, m_mlp_w1, m_mlp_w2, m_ple_gate_w, m_ple_gate_b, m_ple_proj, v_ln_gain, v_ln_bias, v_pool_w, v_pool_b, v_pool_scale, v_gdn_w_in, v_gdn_conv, v_gdn_a_log, v_gdn_dt_bias, v_gdn_norm_w, v_gdn_w_out, v_mlp_w1, v_mlp_w2, v_ple_gate_w, v_ple_gate_b, v_ple_proj):
    w_sh = dict(ln_gain=ln_gain, ln_bias=ln_bias, pool_w=pool_w, pool_b=pool_b, pool_scale=pool_scale, gdn_w_in=gdn_w_in,
                gdn_conv=gdn_conv, gdn_a_log=gdn_a_log, gdn_dt_bias=gdn_dt_bias, gdn_norm_w=gdn_norm_w, gdn_w_out=gdn_w_out,
                mlp_w1=mlp_w1, mlp_w2=mlp_w2, ple_gate_w=ple_gate_w, ple_gate_b=ple_gate_b, ple_proj=ple_proj)
    m_sh = dict(ln_gain=m_ln_gain, ln_bias=m_ln_bias, pool_w=m_pool_w, pool_b=m_pool_b, pool_scale=m_pool_scale, gdn_w_in=m_gdn_w_in,
                gdn_conv=m_gdn_conv, gdn_a_log=m_gdn_a_log, gdn_dt_bias=m_gdn_dt_bias, gdn_norm_w=m_gdn_norm_w, gdn_w_out=m_gdn_w_out,
                mlp_w1=m_mlp_w1, mlp_w2=m_mlp_w2, ple_gate_w=m_ple_gate_w, ple_gate_b=m_ple_gate_b, ple_proj=m_ple_proj)
    v_sh = dict(ln_gain=v_ln_gain, ln_bias=v_ln_bias, pool_w=v_pool_w, pool_b=v_pool_b, pool_scale=v_pool_scale, gdn_w_in=v_gdn_w_in,
                gdn_conv=v_gdn_conv, gdn_a_log=v_gdn_a_log, gdn_dt_bias=v_gdn_dt_bias, gdn_norm_w=v_gdn_norm_w, gdn_w_out=v_gdn_w_out,
                mlp_w1=v_mlp_w1, mlp_w2=v_mlp_w2, ple_gate_w=v_ple_gate_w, ple_gate_b=v_ple_gate_b, ple_proj=v_ple_proj)
    xs, tg = x[0], loss_target[0]
    ps = p[:, 0]
    S, D = xs.shape
    H = D // HEAD_DIM
    me = 4 * lax.axis_index("x") + 2 * lax.axis_index("y") + lax.axis_index("c")
    layer = lambda n, l: (w_sh[n][0] if n in ("gdn_w_in", "gdn_w_out") else w_sh[n][l])

    started, tokens = {}, jnp.zeros((1, 1), F32)
    for g, members in GATHER_GROUPS.items():
        src = _pack_rows([layer(n, l) for n, l in members], D, BF16, PACK_PART_ALIGN)
        started[g] = tuple(_send_start(f"gather_{g}_start", src, lax.empty((N_DEV,) + src.shape, BF16), True))
        tokens = tokens + started[g][4][0:1, 0:1]

    pool_packed = _pack_rows([w_sh["pool_w"][0]], D, BF16, PACK_PART_ALIGN)
    small_packed = _pack_small([w_sh[n] for n in SMALL_SHARDED])
    pool_gathered, small_gathered = _all_gather("gather_first", [pool_packed, small_packed])
    W = {"pool_w": _join_blocks("pool_w", _unpack_rows(pool_gathered, [w_sh["pool_w"][0].shape], D)[0])}
    smalls = _unpack_small(small_gathered, [w_sh[n].shape for n in SMALL_SHARDED])
    for n, a in zip(SMALL_SHARDED, smalls):
        W[n] = jnp.moveaxis(a, 0, -2).reshape(a.shape[1:-1] + (N_DEV * a.shape[-1],))
    W["ln_gain"] = W["ln_gain"].reshape(2 * DEPTH, D)
    W["ln_bias"] = W["ln_bias"].reshape(2 * DEPTH, D)
    W["pool_b"] = W["pool_b"].reshape(1, D) + tokens
    W["gdn_conv"] = W["gdn_conv"][0]
    W["pool_scale"] = pool_scale
    W["ple_gate_b"] = ple_gate_b
    W["gdn_norm_w"] = gdn_norm_w
    W["gdn_a_log"] = jnp.pad(gdn_a_log, ((0, 0), (0, LANES - H)))
    W["gdn_dt_bias"] = jnp.pad(gdn_dt_bias, ((0, 0), (0, LANES - H)))

    def fetch(g, after):
        members = GATHER_GROUPS[g]
        src, land = _send_wait(f"gather_{g}_wait", started[g], after, True)
        land = lax.dynamic_update_index_in_dim(land, src, me, 0)
        parts = _unpack_rows(land, [layer(n, l).shape for n, l in members], D)
        out = {n: _join_blocks(n, a) for (n, _), a in zip(members, parts)}
        if "gdn_w_in" in out:
            out["gdn_wqkv"], out["gdn_wz"], out["gdn_wba"] = _split_w_in(out.pop("gdn_w_in"), D, H)
        return out

    sent = {}

    def emit(g, grads):
        members = GRAD_GROUPS[g]
        blocks = [_split_blocks(n, grads[n]) for n, _ in members]
        src = jnp.stack([_pack_rows([b[d] for b in blocks], D, BF16, SUM_TILE) for d in range(N_DEV)])
        sent[g] = tuple(_send_start(f"grads_{g}_start", src, lax.empty((N_PEERS,) + src.shape[1:], BF16), False))
        return sent[g][4][0:1, 0:1]

    loss_cols, grad_x, G = _local_step(xs, ps, tg, W, fetch, emit)
    loss = lax.psum(0.5 * jnp.sum(loss_cols) / D, MESH_AXES)

    pool_blocks = _split_blocks("pool_w", G["pool_w"])
    pool_src = jnp.stack([_pack_rows([pool_blocks[d]], D, BF16, PACK_PART_ALIGN) for d in range(N_DEV)])
    pool_sum = _sum_blocks("sum_pool_grads", _exchange("exchange_pool_grads", pool_src), SUM_TILE)
    grads = {"pool_w": _unpack_rows(pool_sum, [w_sh["pool_w"][0].shape], D)[0].reshape(w_sh["pool_w"].shape)}
    small_names = SMALL_SHARDED + SMALL_REPLICATED
    gs_packed = _pack_small([G[n] for n in small_names])
    (gs_all,) = _all_gather("gather_small_grads", [gs_packed])
    gs_sum = _sum_blocks("sum_small_grads", gs_all, SUM_TILE)
    for n, a in zip(small_names, _unpack_small(gs_sum, [G[n].shape for n in small_names])):
        if n in SMALL_SHARDED:
            width = w_sh[n].shape[-1]
            a = a.reshape(w_sh[n].shape[:-1] + (N_DEV * width,))
            a = lax.dynamic_slice_in_dim(a, me * width, width, axis=a.ndim - 1)
        grads[n] = a.reshape(w_sh[n].shape)

    per_layer = {}
    for g, members in GRAD_GROUPS.items():
        src, land = _send_wait(f"grads_{g}_wait", sent[g], grad_x, False)
        own = lax.dynamic_index_in_dim(src, me, 0, keepdims=True)
        total = _sum_blocks(f"sum_grads_{g}", jnp.concatenate([land, own], axis=0), SUM_TILE)
        for (n, l), a in zip(members, _unpack_rows(total, [layer(n, l).shape for n, l in members], D)):
            per_layer[(n, l)] = a
    for n in ("gdn_w_in", "gdn_w_out"):
        grads[n] = per_layer[(n, 0)][None]
    for n in ("mlp_w1", "mlp_w2", "ple_gate_w", "ple_proj"):
        grads[n] = jnp.stack([per_layer[(n, 0)], per_layer[(n, 1)]])

    deltas, new_m, new_v = {}, {}, {}
    for n in WEIGHTS:
        deltas[n], new_m[n], new_v[n] = _adamw(f"adamw_{n}", w_sh[n], grads[n], m_sh[n], v_sh[n])
    return (loss, grad_x[None], *[grads[n] for n in WEIGHTS], *[deltas[n] for n in WEIGHTS],
            *[new_m[n] for n in WEIGHTS], *[new_v[n] for n in WEIGHTS])
```

```python
import functools

import jax
import jax.numpy as jnp
from jax import lax
from jax.experimental import pallas as pl
from jax.experimental.pallas import tpu as pltpu

F32 = jnp.float32
BF16 = jnp.bfloat16
MESH_AXES = ("x", "y", "c")
N_DEV = 8
MESH = pl.DeviceIdType.MESH

DEPTH = 2
ALPHA = (2.0 * DEPTH) ** 0.25
LN_EPS = 1e-5
RMS_EPS = 1e-6
L2_EPS = 1e-6
HEAD_DIM = 128
CONV_WIDTH = 4
POOL_WINDOWS = (2, 4, 8, 16)
POOL_HALO = 16
CONV_HALO = 8
LANES = 128
ADAM_LR = 0.001
ADAM_B1 = 0.9
ADAM_B2 = 0.999
ADAM_EPS = 1e-08
ADAM_WD = 0.01
ADAM_STEP = 10

VMEM_LIMIT = 56 * 1024 * 1024
ROW_TILE = 256
CONV_TILE = 256
CHUNK = 128
MM_TM, MM_TN, MM_TK = 512, 1024, 1024

_DIMS = {
    "nn": (((1,), (0,)), ((), ())),
    "nt": (((1,), (1,)), ((), ())),
    "tn": (((0,), (0,)), ((), ())),
}


def _params(n_axes):
    return pltpu.CompilerParams(dimension_semantics=("arbitrary",) * n_axes, vmem_limit_bytes=VMEM_LIMIT)


def _fit(tile, n):
    tile = min(tile, n)
    while n % tile:
        tile //= 2
    return tile


def _mm(name, a, b, mode, out_dtypes, epi=None, extras=(), a_fn=None, tm=None, tn=None, tk=None, b_outer=False, after=()):
    if mode == "tn":
        K, M = a.shape
    else:
        M, K = a.shape
    N = b.shape[0] if mode == "nt" else b.shape[1]
    tm, tn, tk = _fit(tm or MM_TM, M), _fit(tn or MM_TN, N), _fit(tk or MM_TK, K)
    nk = K // tk

    def at(f):
        return (lambda j, i, k: f(i, j, k)) if b_outer else f

    a_spec = pl.BlockSpec((tk, tm), at(lambda i, j, k: (k, i))) if mode == "tn" else pl.BlockSpec((tm, tk), at(lambda i, j, k: (i, k)))
    b_spec = pl.BlockSpec((tn, tk), at(lambda i, j, k: (j, k))) if mode == "nt" else pl.BlockSpec((tk, tn), at(lambda i, j, k: (k, j)))
    ex_specs = [
        pl.BlockSpec((tm, tn), at(lambda i, j, k: (i, j))) if kind == "tile" else pl.BlockSpec((1, tn), at(lambda i, j, k: (0, j)))
        for _, kind in extras
    ]
    n_ex, n_out, n_after = len(extras), len(out_dtypes), len(after)

    def body(*refs):
        a_ref, b_ref = refs[0], refs[1]
        ex_refs = refs[2 : 2 + n_ex]
        out_refs = refs[2 + n_ex + n_after : 2 + n_ex + n_after + n_out]
        av = a_ref[...]
        if a_fn is not None:
            av = a_fn(av)
        part = lax.dot_general(av.astype(BF16), b_ref[...].astype(BF16), _DIMS[mode], preferred_element_type=F32)

        def finish(res):
            vals = epi(res, *[e[...] for e in ex_refs]) if epi is not None else (res,)
            for o_ref, v in zip(out_refs, vals):
                o_ref[...] = v.astype(o_ref.dtype)

        if nk == 1:
            finish(part)
        else:
            acc = refs[-1]
            k = pl.program_id(2)

            @pl.when(k == 0)
            def _():
                acc[...] = part

            @pl.when(k > 0)
            def _():
                acc[...] += part

            @pl.when(k == nk - 1)
            def _():
                finish(acc[...])

    outs = pl.pallas_call(
        body,
        name=name,
        grid=(N // tn, M // tm, nk) if b_outer else (M // tm, N // tn, nk),
        in_specs=[a_spec, b_spec] + ex_specs + [pl.BlockSpec(memory_space=pl.ANY)] * n_after,
        out_specs=[pl.BlockSpec((tm, tn), at(lambda i, j, k: (i, j))) for _ in out_dtypes],
        out_shape=[jax.ShapeDtypeStruct((M, N), dt) for dt in out_dtypes],
        scratch_shapes=[pltpu.VMEM((tm, tn), F32)] if nk > 1 else [],
        compiler_params=_params(3),
    )(a, b, *[e for e, _ in extras], *after)
    return outs[0] if n_out == 1 else outs


def _rowwise(name, fn, S, ts, rows=(), halos=(), consts=(), outs=(), accs=()):
    ts = min(ts, S)
    assert S % ts == 0
    n = S // ts
    in_specs = [pl.BlockSpec((ts, a.shape[1]), lambda i: (i, 0)) for a in rows]
    for a, kind, hr in halos:
        r, nb = ts // hr, S // hr
        if kind == "prev":
            in_specs.append(pl.BlockSpec((hr, a.shape[1]), lambda i, r=r: (jnp.maximum(i * r - 1, 0), 0)))
        else:
            in_specs.append(pl.BlockSpec((hr, a.shape[1]), lambda i, r=r, nb=nb: (jnp.minimum((i + 1) * r, nb - 1), 0)))
    in_specs += [pl.BlockSpec(a.shape, lambda i, nd=a.ndim: (0,) * nd) for a in consts]
    out_specs = [pl.BlockSpec((ts, w), lambda i: (i, 0)) for w, _ in outs]
    out_specs += [pl.BlockSpec((r, w), lambda i: (0, 0)) for r, w in accs]
    out_shape = [jax.ShapeDtypeStruct((S, w), dt) for w, dt in outs]
    out_shape += [jax.ShapeDtypeStruct((r, w), F32) for r, w in accs]
    nr, nh, nc, no = len(rows), len(halos), len(consts), len(outs)

    def body(*refs):
        i = pl.program_id(0)
        rv = [r[...] for r in refs[:nr]]
        hv = [r[...] for r in refs[nr : nr + nh]]
        cv = [r[...] for r in refs[nr + nh : nr + nh + nc]]
        o_refs = refs[nr + nh + nc : nr + nh + nc + no]
        a_refs = refs[nr + nh + nc + no :]
        ovals, avals = fn(i, n, rv, hv, cv)
        for o_ref, v in zip(o_refs, ovals):
            o_ref[...] = v.astype(o_ref.dtype)
        for a_ref, v in zip(a_refs, avals):

            @pl.when(i == 0)
            def _(a_ref=a_ref, v=v):
                a_ref[...] = v

            @pl.when(i > 0)
            def _(a_ref=a_ref, v=v):
                a_ref[...] += v

    res = pl.pallas_call(
        body,
        name=name,
        grid=(n,),
        in_specs=in_specs,
        out_specs=out_specs,
        out_shape=out_shape,
        compiler_params=_params(1),
    )(*rows, *[h[0] for h in halos], *consts)
    return list(res)


def _ln(h, g, b):
    mu = jnp.mean(h, axis=-1, keepdims=True)
    d = h - mu
    var = jnp.mean(d * d, axis=-1, keepdims=True)
    rstd = lax.rsqrt(var + LN_EPS)
    xhat = d * rstd
    return xhat, rstd, xhat * g + b


def _ln_bwd(dy, xhat, rstd, g):
    dxh = dy * g
    m1 = jnp.mean(dxh, axis=-1, keepdims=True)
    m2 = jnp.mean(dxh * xhat, axis=-1, keepdims=True)
    dh = rstd * (dxh - m1 - xhat * m2)
    return dh, jnp.sum(dy * xhat, axis=0, keepdims=True), jnp.sum(dy, axis=0, keepdims=True)


def _wide(col, ts):
    return jnp.broadcast_to(col, (ts, LANES))


def _pool_fwd(x, wp, pb, ps, g, b):
    S, D = x.shape
    gw = D // len(POOL_WINDOWS)
    ts = min(ROW_TILE, S)

    def fn(i, n, rv, hv, cv):
        (xc,), (xp,) = rv, hv
        wpv, pbv, psv, gv, bv = cv
        xp = jnp.where(i > 0, xp, 0.0)
        xx = jnp.concatenate([xp, xc], axis=0)
        t = i * ts + lax.broadcasted_iota(jnp.int32, (ts, 1), 0)
        pooled, ys = [], []
        for gi, w in enumerate(POOL_WINDOWS):
            s = xx[:, gi * gw : (gi + 1) * gw]
            k = 1
            while k < w:
                s = s + pltpu.roll(s, k, axis=0)
                k *= 2
            cnt = jnp.minimum(t + 1, w).astype(F32)
            pg = (s[POOL_HALO:, :] / cnt - xc[:, gi * gw : (gi + 1) * gw]).astype(BF16)
            pooled.append(pg)
            ys.append(jnp.dot(pg, wpv[gi], preferred_element_type=F32))
        y = jnp.concatenate(ys, axis=1)
        h = ALPHA * xc + (y + pbv) * psv
        xhat, rstd, xa = _ln(h, gv, bv)
        return (jnp.concatenate(pooled, axis=1), xhat, _wide(rstd, ts), xa), ()

    return _rowwise(
        "pool_fwd", fn, S, ts, rows=[x], halos=[(x, "prev", POOL_HALO)], consts=[wp, pb, ps, g, b],
        outs=[(D, BF16), (D, F32), (LANES, F32), (D, BF16)],
    )


def _pool_bwd(dh, pooled, wp, pb, ps):
    S, D = dh.shape
    gw = D // len(POOL_WINDOWS)
    ts = min(ROW_TILE, S)
    te = ts + POOL_HALO

    def fn(i, n, rv, hv, cv):
        (dhc, pc), (dhn,) = rv, hv
        wpv, pbv, psv = cv
        dhn = jnp.where(i < n - 1, dhn, 0.0)
        dy_ext = jnp.concatenate([dhc, dhn], axis=0) * psv
        dyb = dy_ext.astype(BF16)
        t = i * ts + lax.broadcasted_iota(jnp.int32, (te, 1), 0)
        dxs, ys = [], []
        for gi, w in enumerate(POOL_WINDOWS):
            sl = slice(gi * gw, (gi + 1) * gw)
            dp = lax.dot_general(dyb[:, sl], wpv[gi], _DIMS["nt"], preferred_element_type=F32)
            s = dp / jnp.minimum(t + 1, w).astype(F32)
            k = 1
            while k < w:
                s = s + pltpu.roll(s, k, axis=0)
                k *= 2
            s = pltpu.roll(s, POOL_HALO - (w - 1), axis=0)
            dxs.append(s[POOL_HALO:, :] - dp[:ts, :])
            ys.append(jnp.dot(pc[:, sl], wpv[gi], preferred_element_type=F32))
        dx = ALPHA * dhc + jnp.concatenate(dxs, axis=1)
        y = jnp.concatenate(ys, axis=1) + pbv
        dscale = jnp.sum(dhc * y, axis=0, keepdims=True)
        dbias = jnp.sum(dy_ext[:ts, :], axis=0, keepdims=True)
        return (dx, dyb[:ts, :]), (dscale, dbias)

    return _rowwise(
        "pool_bwd", fn, S, ts, rows=[dh, pooled], halos=[(dh, "next", POOL_HALO)], consts=[wp, pb, ps],
        outs=[(D, F32), (D, BF16)], accs=[(1, D), (1, D)],
    )


def _pool_dw(pooled, dy):
    S, D = pooled.shape
    G = len(POOL_WINDOWS)
    gw = D // G
    tk = min(MM_TK, S)
    nk = S // tk

    def body(p_ref, d_ref, o_ref):
        k = pl.program_id(1)
        part = lax.dot_general(p_ref[...], d_ref[...], _DIMS["tn"], preferred_element_type=F32)

        @pl.when(k == 0)
        def _():
            o_ref[...] = part

        @pl.when(k > 0)
        def _():
            o_ref[...] += part

    return pl.pallas_call(
        body,
        name="pool_dw",
        grid=(G, nk),
        in_specs=[pl.BlockSpec((tk, gw), lambda g, k: (k, g)), pl.BlockSpec((tk, gw), lambda g, k: (k, g))],
        out_specs=pl.BlockSpec((None, gw, gw), lambda g, k: (g, 0, 0)),
        out_shape=jax.ShapeDtypeStruct((G, gw, gw), F32),
        compiler_params=_params(2),
    )(pooled, dy)


def _res_ln_mix(name, xhat_p, mix, gp_, bp_, g, b):
    S, D = xhat_p.shape
    ts = min(ROW_TILE, S)

    def fn(i, n, rv, hv, cv):
        xh, m = rv
        gpv, bpv, gv, bv = cv
        xhat, rstd, xo = _ln(ALPHA * (xh * gpv + bpv) + m, gv, bv)
        return (xhat, _wide(rstd, ts), xo), ()

    return _rowwise(name, fn, S, ts, rows=[xhat_p, mix], consts=[gp_, bp_, g, b], outs=[(D, F32), (LANES, F32), (D, BF16)])


def _res_ln_ffpe(name, xhat_p, ff, gate, pp, gp_, bp_, g, b):
    S, D = xhat_p.shape
    ts = min(ROW_TILE, S)

    def fn(i, n, rv, hv, cv):
        xh, f, gt, p_ = rv
        gpv, bpv, gv, bv = cv
        xhat, rstd, xo = _ln(ALPHA * (xh * gpv + bpv) + f + jax.nn.sigmoid(gt) * p_, gv, bv)
        return (xhat, _wide(rstd, ts), xo), ()

    return _rowwise(name, fn, S, ts, rows=[xhat_p, ff, gate, pp], consts=[gp_, bp_, g, b], outs=[(D, F32), (LANES, F32), (D, BF16)])


def _final_ln_loss(xhat_p, ff, gate, pp, tgt, gp_, bp_, g, b):
    S, D = xhat_p.shape
    ts = min(ROW_TILE, S)

    def fn(i, n, rv, hv, cv):
        xh, f, gt, p_, tg = rv
        gpv, bpv, gv, bv = cv
        xhat, rstd, y = _ln(ALPHA * (xh * gpv + bpv) + f + jax.nn.sigmoid(gt) * p_, gv, bv)
        e = y - tg
        dh, dg, db = _ln_bwd(e * (1.0 / D), xhat, rstd, gv)
        return (dh, dh), (jnp.sum(e * e, axis=0, keepdims=True), dg, db)

    return _rowwise(
        "final_ln_loss", fn, S, ts, rows=[xhat_p, ff, gate, pp, tgt], consts=[gp_, bp_, g, b],
        outs=[(D, F32), (D, BF16)], accs=[(1, D), (1, D), (1, D)],
    )


def _ln_bwd_call(name, dy, xhat, rstd, g):
    S, D = dy.shape
    ts = min(ROW_TILE, S)

    def fn(i, n, rv, hv, cv):
        dyv, xh, rs = rv
        dh, dg, db = _ln_bwd(dyv, xh, rs[:, :1], cv[0])
        return (dh, dh), (dg, db)

    return _rowwise(name, fn, S, ts, rows=[dy, xhat, rstd], consts=[g], outs=[(D, F32), (D, BF16)], accs=[(1, D), (1, D)])


def _ple_bwd(name, dh, gate, pp):
    S, D = dh.shape
    ts = min(ROW_TILE, S)

    def fn(i, n, rv, hv, cv):
        d, gt, p_ = rv
        sg = jax.nn.sigmoid(gt)
        dgt = d * p_ * sg * (1.0 - sg)
        return (dgt, d * sg), (jnp.sum(dgt, axis=0, keepdims=True),)

    return _rowwise(name, fn, S, ts, rows=[dh, gate, pp], outs=[(D, BF16), (D, BF16)], accs=[(1, D)])


def _silu(c):
    return c * jax.nn.sigmoid(c)


def _qkv_point(c, is_qk, scale):
    s = _silu(c)
    nrm = s * lax.rsqrt(jnp.sum(s * s, axis=-1, keepdims=True) + L2_EPS) * scale
    return jnp.where(is_qk, nrm, s)


def _conv_rows(xx, wv, lo, rows):
    acc = None
    for j in range(CONV_WIDTH):
        sh = CONV_WIDTH - 1 - j
        term = (pltpu.roll(xx, sh, axis=0) if sh else xx)[lo : lo + rows, :] * wv[j : j + 1, :]
        acc = term if acc is None else acc + term
    return acc


def _conv_fwd(qkv_pre, conv_w):
    S, W = qkv_pre.shape
    D = W // 3
    H = D // HEAD_DIM
    ts = min(CONV_TILE, S)
    r = ts // CONV_HALO

    def body(x_ref, xp_ref, w_ref, o_ref):
        j, i = pl.program_id(0), pl.program_id(1)
        xp = jnp.where(i > 0, xp_ref[...], 0.0)
        xx = jnp.concatenate([xp, x_ref[...]], axis=0)
        c = _conv_rows(xx, w_ref[...], CONV_HALO, ts)
        scale = jnp.where(j == 0, HEAD_DIM**-0.5, 1.0).astype(F32)
        for h in range(H):
            sl = slice(h * HEAD_DIM, (h + 1) * HEAD_DIM)
            o_ref[:, sl] = _qkv_point(c[:, sl], j < 2, scale)

    return pl.pallas_call(
        body,
        name="gdn_conv_fwd",
        grid=(3, S // ts),
        in_specs=[
            pl.BlockSpec((ts, D), lambda j, i: (i, j)),
            pl.BlockSpec((CONV_HALO, D), lambda j, i: (jnp.maximum(i * r - 1, 0), j)),
            pl.BlockSpec((CONV_WIDTH, D), lambda j, i: (0, j)),
        ],
        out_specs=pl.BlockSpec((ts, D), lambda j, i: (i, j)),
        out_shape=jax.ShapeDtypeStruct((S, W), F32),
        compiler_params=_params(2),
    )(qkv_pre, qkv_pre, conv_w)


def _conv_bwd(qkv_pre, conv_w, dqkvn):
    S, W = qkv_pre.shape
    D = W // 3
    H = D // HEAD_DIM
    ts = min(CONV_TILE, S)
    r, nb = ts // CONV_HALO, S // CONV_HALO
    te = ts + CONV_HALO

    def body(x_ref, xp_ref, xn_ref, w_ref, d_ref, dn_ref, dx_ref, dw_ref):
        j, i = pl.program_id(0), pl.program_id(1)
        n = pl.num_programs(1)
        wv = w_ref[...]
        xp = jnp.where(i > 0, xp_ref[...], 0.0)
        xx = jnp.concatenate([xp, x_ref[...], xn_ref[...]], axis=0)
        c = _conv_rows(xx, wv, CONV_HALO, te)
        dn = jnp.where(i < n - 1, dn_ref[...], 0.0)
        dout = jnp.concatenate([d_ref[...], dn], axis=0)
        scale = jnp.where(j == 0, HEAD_DIM**-0.5, 1.0).astype(F32)
        dcs = []
        for h in range(H):
            sl = slice(h * HEAD_DIM, (h + 1) * HEAD_DIM)
            _, vjp = jax.vjp(lambda cc: _qkv_point(cc, j < 2, scale), c[:, sl])
            dcs.append(vjp(dout[:, sl])[0])
        dc = jnp.concatenate(dcs, axis=1)
        dx = None
        dws = []
        for jj in range(CONV_WIDTH):
            sh = CONV_WIDTH - 1 - jj
            term = pltpu.roll(dc, CONV_HALO - sh, axis=0)[CONV_HALO:, :] * wv[jj : jj + 1, :]
            dx = term if dx is None else dx + term
            xs = (pltpu.roll(xx, sh, axis=0) if sh else xx)[CONV_HALO : CONV_HALO + ts, :]
            dws.append(jnp.sum(dc[:ts, :] * xs, axis=0, keepdims=True))
        dx_ref[...] = dx.astype(dx_ref.dtype)
        dw = jnp.concatenate(dws, axis=0)

        @pl.when(i == 0)
        def _():
            dw_ref[...] = dw

        @pl.when(i > 0)
        def _():
            dw_ref[...] += dw

    return pl.pallas_call(
        body,
        name="gdn_conv_bwd",
        grid=(3, S // ts),
        in_specs=[
            pl.BlockSpec((ts, D), lambda j, i: (i, j)),
            pl.BlockSpec((CONV_HALO, D), lambda j, i: (jnp.maximum(i * r - 1, 0), j)),
            pl.BlockSpec((CONV_HALO, D), lambda j, i: (jnp.minimum((i + 1) * r, nb - 1), j)),
            pl.BlockSpec((CONV_WIDTH, D), lambda j, i: (0, j)),
            pl.BlockSpec((ts, D), lambda j, i: (i, j)),
            pl.BlockSpec((CONV_HALO, D), lambda j, i: (jnp.minimum((i + 1) * r, nb - 1), j)),
        ],
        out_specs=[pl.BlockSpec((ts, D), lambda j, i: (i, j)), pl.BlockSpec((CONV_WIDTH, D), lambda j, i: (0, j))],
        out_shape=[jax.ShapeDtypeStruct((S, W), BF16), jax.ShapeDtypeStruct((CONV_WIDTH, W), F32)],
        compiler_params=_params(2),
    )(qkv_pre, qkv_pre, qkv_pre, conv_w, dqkvn, dqkvn)


def _softplus(x):
    pos = x > 0.0
    return jnp.where(pos, x, 0.0) + jnp.log(1.0 + jnp.exp(jnp.where(pos, -x, x)))


def _gates(bl, al, alog, dt):
    return jax.nn.sigmoid(bl), -jnp.exp(alog) * _softplus(al + dt)


def _gates_fwd(ba, alog, dt):
    S = ba.shape[0]
    ts = min(ROW_TILE, S)

    def fn(i, n, rv, hv, cv):
        return _gates(rv[0][:, :LANES], rv[0][:, LANES:], cv[0], cv[1]), ()

    return _rowwise("gdn_gates_fwd", fn, S, ts, rows=[ba], consts=[alog, dt], outs=[(LANES, F32), (LANES, F32)])


def _gates_bwd(ba, alog, dt, dbeta, dg, H):
    S = ba.shape[0]
    ts = min(ROW_TILE, S)

    def fn(i, n, rv, hv, cv):
        bav, dbv, dgv = rv
        real = lax.broadcasted_iota(jnp.int32, (1, LANES), 1) < H
        _, vjp = jax.vjp(_gates, bav[:, :LANES], bav[:, LANES:], cv[0], cv[1])
        dbl, dal, dalog, ddt = vjp((jnp.where(real, dbv, 0.0), jnp.where(real, dgv, 0.0)))
        dbl, dal = jnp.where(real, dbl, 0.0), jnp.where(real, dal, 0.0)
        return (jnp.concatenate([dbl, dal], axis=1),), (jnp.where(real, dalog, 0.0), jnp.where(real, ddt, 0.0))

    return _rowwise(
        "gdn_gates_bwd", fn, S, ts, rows=[ba, dbeta, dg], consts=[alog, dt], outs=[(2 * LANES, BF16)],
        accs=[(1, LANES), (1, LANES)],
    )


def _split_bf16(a, n):
    parts, rest = [], a
    for _ in range(n):
        piece = rest.astype(BF16)
        parts.append(piece)
        rest = rest - piece.astype(F32)
    return parts


def _tri_dot(a, b, mode, tri):
    d = lambda u, v: lax.dot_general(u, v, _DIMS[mode], preferred_element_type=F32)
    if tri == 0:
        return sum(d(a.astype(BF16), piece) for piece in _split_bf16(b, 3))
    return sum(d(piece, b.astype(BF16)) for piece in _split_bf16(a, 3))


def _make_dot(exact):
    def raw(a, b, mode):
        if exact:
            a_hi, a_lo = _split_bf16(a, 2)
            b_hi, b_lo = _split_bf16(b, 2)
            d = lambda u, v: lax.dot_general(u, v, _DIMS[mode], preferred_element_type=F32)
            return d(a_hi, b_hi) + (d(a_hi, b_lo) + d(a_lo, b_hi))
        return lax.dot_general(a.astype(BF16), b.astype(BF16), _DIMS[mode], preferred_element_type=F32)

    @functools.partial(jax.custom_vjp, nondiff_argnums=(2,))
    def dot(a, b, mode):
        return raw(a, b, mode)

    def fwd(a, b, mode):
        return raw(a, b, mode), (a, b)

    def bwd(mode, res, ct):
        a, b = res
        if mode == "nn":
            return dot(ct, b, "nt"), dot(a, ct, "tn")
        if mode == "nt":
            return dot(ct, b, "nn"), dot(ct, a, "tn")
        return dot(b, ct, "nt"), dot(a, ct, "nn")

    dot.defvjp(fwd, bwd)
    return dot


_bdot = _make_dot(False)
_fdot = _make_dot(True)


@jax.custom_vjp
def _unit_lower_inverse(a_strict):
    return _unit_lower_inverse_raw(a_strict)


def _unit_lower_inverse_fwd(a_strict):
    t = _unit_lower_inverse_raw(a_strict)
    return t, t


def _unit_lower_inverse_bwd(t, ct):
    left = [_bdot(ti, ci, "tn") for ti, ci in zip(t, ct)]
    return (tuple(-_bdot(li, ti, "nt") for li, ti in zip(left, t)),)


_unit_lower_inverse.defvjp(_unit_lower_inverse_fwd, _unit_lower_inverse_bwd)


def _unit_lower_inverse_raw(a_strict):
    C = a_strict[0].shape[0]
    ii = lax.broadcasted_iota(jnp.int32, (C, C), 0)
    jj = lax.broadcasted_iota(jnp.int32, (C, C), 1)
    eye = (ii == jj).astype(F32)
    blk = 16
    same = (ii // blk) == (jj // blk)
    p = [-jnp.where(same, a, 0.0) for a in a_strict]
    t = [eye + x for x in p]
    for _ in range(3):
        p = [_fdot(x, x, "nn") for x in p]
        t = [ti + _fdot(ti, x, "nn") for ti, x in zip(t, p)]
    while blk < C:
        same2 = (ii // (2 * blk)) == (jj // (2 * blk))
        off = jnp.logical_and(same2, jnp.logical_not(same))
        te = [_fdot(ti, jnp.where(off, a, 0.0), "nn") for ti, a in zip(t, a_strict)]
        t = [ti - _fdot(x, ti, "nn") for ti, x in zip(t, te)]
        same, blk = same2, 2 * blk
    return tuple(t)


def _chunk_heads(q, k, v, gc_col, gc_row, b_col, s0):
    R = range(len(q))
    C = q[0].shape[0]
    ii = lax.broadcasted_iota(jnp.int32, (C, C), 0)
    jj = lax.broadcasted_iota(jnp.int32, (C, C), 1)
    rows = lax.broadcasted_iota(jnp.int32, (C, 1), 0)
    decay = [jnp.where(ii >= jj, jnp.exp(jnp.minimum(gc_col[h] - gc_row[h], 0.0)), 0.0) for h in R]
    kb = [k[h] * b_col[h] for h in R]
    a = [_bdot(kb[h], k[h], "nt") * decay[h] for h in R]
    qk = [_bdot(q[h], k[h], "nt") * decay[h] for h in R]
    t = _unit_lower_inverse(tuple(jnp.where(ii > jj, a[h], 0.0) for h in R))
    eg = [jnp.exp(gc_col[h]) for h in R]
    u = [_bdot(t[h], v[h] * b_col[h], "nn") for h in R]
    w = [_bdot(t[h], kb[h] * eg[h], "nn") for h in R]
    g_last = [jnp.sum(jnp.where(rows == C - 1, gc_col[h], 0.0), axis=0, keepdims=True) for h in R]
    kd = [k[h] * jnp.exp(g_last[h] - gc_col[h]) for h in R]
    ws = [_bdot(w[h], s0[h], "nn") for h in R]
    qs = [_bdot(q[h] * eg[h], s0[h], "nn") for h in R]
    v_new = [u[h] - ws[h] for h in R]
    o = [qs[h] + _bdot(qk[h], v_new[h], "nn") for h in R]
    s1 = [s0[h] * jnp.exp(g_last[h]) + _bdot(kd[h], v_new[h], "tn") for h in R]
    return tuple(o), tuple(s1)


def _pick_lane(a, h):
    lanes = lax.broadcasted_iota(jnp.int32, a.shape, 1)
    return jnp.sum(jnp.where(lanes == h, a, 0.0), axis=1, keepdims=True)


def _pick_row(a, h):
    rows = lax.broadcasted_iota(jnp.int32, a.shape, 0)
    return jnp.sum(jnp.where(rows == h, a, 0.0), axis=0, keepdims=True)


def _tri(C):
    ii = lax.broadcasted_iota(jnp.int32, (C, C), 0)
    jj = lax.broadcasted_iota(jnp.int32, (C, C), 1)
    return (ii >= jj).astype(F32)


def _delta_fwd(qkvn, g_pad, g_rows, beta_pad):
    S, W = qkvn.shape
    D = W // 3
    H = D // HEAD_DIM
    C = min(CHUNK, S)
    N = S // C

    def body(x_ref, gp_ref, gr_ref, bp_ref, o_ref, sall_ref, st):
        n = pl.program_id(0)

        @pl.when(n == 0)
        def _():
            st[...] = jnp.zeros_like(st)

        low = _tri(C)
        gc_cols = _tri_dot(low, gp_ref[...], "nn", 0)
        gc_rows = _tri_dot(gr_ref[...], low, "nt", 1)
        bcols = bp_ref[...]
        hs = range(H)
        s0 = tuple(st[h] for h in hs)
        for h in hs:
            sall_ref[h] = s0[h]
        o, s1 = _chunk_heads(
            tuple(x_ref[:, h * HEAD_DIM : (h + 1) * HEAD_DIM] for h in hs),
            tuple(x_ref[:, D + h * HEAD_DIM : D + (h + 1) * HEAD_DIM] for h in hs),
            tuple(x_ref[:, 2 * D + h * HEAD_DIM : 2 * D + (h + 1) * HEAD_DIM] for h in hs),
            tuple(_pick_lane(gc_cols, h) for h in hs), tuple(_pick_row(gc_rows, h) for h in hs),
            tuple(_pick_lane(bcols, h) for h in hs), s0,
        )
        for h in hs:
            st[h] = s1[h]
            o_ref[:, h * HEAD_DIM : (h + 1) * HEAD_DIM] = o[h]

    return pl.pallas_call(
        body,
        name="gdn_delta_fwd",
        grid=(N,),
        in_specs=[
            pl.BlockSpec((C, W), lambda n: (n, 0)),
            pl.BlockSpec((C, LANES), lambda n: (n, 0)),
            pl.BlockSpec((None, 8, C), lambda n: (n, 0, 0)),
            pl.BlockSpec((C, LANES), lambda n: (n, 0)),
        ],
        out_specs=[pl.BlockSpec((C, D), lambda n: (n, 0)), pl.BlockSpec((None, H, HEAD_DIM, HEAD_DIM), lambda n: (n, 0, 0, 0))],
        out_shape=[jax.ShapeDtypeStruct((S, D), F32), jax.ShapeDtypeStruct((N, H, HEAD_DIM, HEAD_DIM), F32)],
        scratch_shapes=[pltpu.VMEM((H, HEAD_DIM, HEAD_DIM), F32)],
        compiler_params=_params(1),
    )(qkvn, g_pad, g_rows, beta_pad)


def _delta_bwd(qkvn, g_pad, g_rows, beta_pad, s_all, do):
    S, W = qkvn.shape
    D = W // 3
    H = D // HEAD_DIM
    C = min(CHUNK, S)
    N = S // C

    def body(x_ref, gp_ref, gr_ref, bp_ref, sall_ref, do_ref, dx_ref, dgp_ref, dgr_ref, dbp_ref, dst):
        n = pl.program_id(0)

        @pl.when(n == 0)
        def _():
            dst[...] = jnp.zeros_like(dst)

        low = _tri(C)
        gc_cols = _tri_dot(low, gp_ref[...], "nn", 0)
        gc_rows = _tri_dot(gr_ref[...], low, "nt", 1)
        bcols = bp_ref[...]
        lane = lax.broadcasted_iota(jnp.int32, (1, LANES), 1)
        row8 = lax.broadcasted_iota(jnp.int32, (8, 1), 0)
        dgc_cols = jnp.zeros((C, LANES), F32)
        dgc_rows = jnp.zeros((8, C), F32)
        dbcols = jnp.zeros((C, LANES), F32)
        hs = range(H)
        _, vjp = jax.vjp(
            _chunk_heads,
            tuple(x_ref[:, h * HEAD_DIM : (h + 1) * HEAD_DIM] for h in hs),
            tuple(x_ref[:, D + h * HEAD_DIM : D + (h + 1) * HEAD_DIM] for h in hs),
            tuple(x_ref[:, 2 * D + h * HEAD_DIM : 2 * D + (h + 1) * HEAD_DIM] for h in hs),
            tuple(_pick_lane(gc_cols, h) for h in hs), tuple(_pick_row(gc_rows, h) for h in hs),
            tuple(_pick_lane(bcols, h) for h in hs), tuple(sall_ref[h] for h in hs),
        )
        dq, dk, dv, dgc, dgr, dbc, ds0 = vjp((tuple(do_ref[:, h * HEAD_DIM : (h + 1) * HEAD_DIM] for h in hs), tuple(dst[h] for h in hs)))
        for h in hs:
            dst[h] = ds0[h]
            dx_ref[:, h * HEAD_DIM : (h + 1) * HEAD_DIM] = dq[h]
            dx_ref[:, D + h * HEAD_DIM : D + (h + 1) * HEAD_DIM] = dk[h]
            dx_ref[:, 2 * D + h * HEAD_DIM : 2 * D + (h + 1) * HEAD_DIM] = dv[h]
            dgc_cols = dgc_cols + dgc[h] * (lane == h).astype(F32)
            dgc_rows = dgc_rows + dgr[h] * (row8 == h).astype(F32)
            dbcols = dbcols + dbc[h] * (lane == h).astype(F32)
        dgp_ref[...] = _tri_dot(low, dgc_cols, "tn", 0)
        dgr_ref[...] = _tri_dot(dgc_rows, low, "nn", 1)
        dbp_ref[...] = dbcols

    rev = lambda n: N - 1 - n
    return pl.pallas_call(
        body,
        name="gdn_delta_bwd",
        grid=(N,),
        in_specs=[
            pl.BlockSpec((C, W), lambda n: (rev(n), 0)),
            pl.BlockSpec((C, LANES), lambda n: (rev(n), 0)),
            pl.BlockSpec((None, 8, C), lambda n: (rev(n), 0, 0)),
            pl.BlockSpec((C, LANES), lambda n: (rev(n), 0)),
            pl.BlockSpec((None, H, HEAD_DIM, HEAD_DIM), lambda n: (rev(n), 0, 0, 0)),
            pl.BlockSpec((C, D), lambda n: (rev(n), 0)),
        ],
        out_specs=[
            pl.BlockSpec((C, W), lambda n: (rev(n), 0)),
            pl.BlockSpec((C, LANES), lambda n: (rev(n), 0)),
            pl.BlockSpec((None, 8, C), lambda n: (rev(n), 0, 0)),
            pl.BlockSpec((C, LANES), lambda n: (rev(n), 0)),
        ],
        out_shape=[
            jax.ShapeDtypeStruct((S, W), F32),
            jax.ShapeDtypeStruct((S, LANES), F32),
            jax.ShapeDtypeStruct((N, 8, C), F32),
            jax.ShapeDtypeStruct((S, LANES), F32),
        ],
        scratch_shapes=[pltpu.VMEM((H, HEAD_DIM, HEAD_DIM), F32)],
        compiler_params=_params(1),
    )(qkvn, g_pad, g_rows, beta_pad, s_all, do)


def _gate_norm_head(o, z, nw):
    return o * lax.rsqrt(jnp.mean(o * o, axis=-1, keepdims=True) + RMS_EPS) * nw * _silu(z)


def _gate_norm_fwd(o, z, nw):
    S, D = o.shape
    H = D // HEAD_DIM
    ts = min(ROW_TILE, S)

    def fn(i, n, rv, hv, cv):
        ov, zv = rv
        parts = [_gate_norm_head(ov[:, h * HEAD_DIM : (h + 1) * HEAD_DIM], zv[:, h * HEAD_DIM : (h + 1) * HEAD_DIM], cv[0]) for h in range(H)]
        return (jnp.concatenate(parts, axis=1),), ()

    return _rowwise("gdn_gate_norm_fwd", fn, S, ts, rows=[o, z], consts=[nw], outs=[(D, BF16)])[0]


def _gate_norm_bwd(dog, o, z, nw):
    S, D = o.shape
    H = D // HEAD_DIM
    ts = min(ROW_TILE, S)

    def fn(i, n, rv, hv, cv):
        dv, ov, zv = rv
        dos, dzs, dnw = [], [], None
        for h in range(H):
            sl = slice(h * HEAD_DIM, (h + 1) * HEAD_DIM)
            _, vjp = jax.vjp(_gate_norm_head, ov[:, sl], zv[:, sl], cv[0])
            a, b_, c_ = vjp(dv[:, sl])
            dos.append(a)
            dzs.append(b_)
            dnw = c_ if dnw is None else dnw + c_
        return (jnp.concatenate(dos, axis=1), jnp.concatenate(dzs, axis=1)), (dnw,)

    return _rowwise("gdn_gate_norm_bwd", fn, S, ts, rows=[dog, o, z], consts=[nw], outs=[(D, F32), (D, BF16)], accs=[(1, HEAD_DIM)])


def _square_bf16(r):
    rf = r.astype(F32)
    return rf * rf


def _mlp_ple_dw(li, dh, dhb, xa, p, r, gate, pp, w2):
    dpre = _mm(f"l{li}_mlp_down_bwd", dhb, w2, "nt", [BF16], epi=lambda acc, rr: (acc * (2.0 * rr.astype(F32)),), extras=[(r, "tile")], tm=1024, tn=1024, b_outer=True)
    dw2 = _mm(f"l{li}_mlp_dw2", r, dhb, "tn", [F32], a_fn=_square_bf16)
    dgate, dpp, dbg = _ple_bwd(f"l{li}_ple_bwd", dh, gate, pp)
    dw1 = _mm(f"l{li}_mlp_dw1", xa, dpre, "tn", [F32])
    dwg = _mm(f"l{li}_ple_dwg", xa, dgate, "tn", [F32])
    dwp = _mm(f"l{li}_ple_dwp", p, dpp, "tn", [F32])
    return dpre, dgate, dw1, dw2, dwg, dbg, dwp


def _mlp_ple_dx(li, dh, dpre, dgate, w1, wg, after):
    t = _mm(f"l{li}_ple_gate_bwd", dgate, wg, "nt", [F32], epi=lambda acc, d: (acc + ALPHA * d,), extras=[(dh, "tile")], tm=1024, after=after)
    return _mm(f"l{li}_mlp_up_bwd", dpre, w1, "nt", [F32], epi=lambda acc, d: (acc + d,), extras=[(t, "tile")], tm=256, tk=4096)


def _local_step(x, p, tgt, W, fetch, emit):
    S, D = x.shape
    H = D // HEAD_DIM
    C = min(CHUNK, S)
    N = S // C
    lg = lambda i, j: W["ln_gain"][2 * i + j][None, :]
    lb = lambda i, j: W["ln_bias"][2 * i + j][None, :]
    G = {}

    pooled, xh0a, rs0a, x0a = _pool_fwd(x, W["pool_w"], W["pool_b"], W["pool_scale"], lg(0, 0), lb(0, 0))
    w0a = fetch("l0a", x0a)
    r0 = _mm("l0_mlp_up", x0a, w0a["mlp_w1"], "nn", [BF16], epi=lambda acc: (jnp.maximum(acc, 0.0),), tm=1024, tn=1024, b_outer=True)
    w0b = fetch("l0b", r0)
    ff0 = _mm("l0_mlp_down", r0, w0b["mlp_w2"], "nn", [F32], a_fn=_square_bf16, tm=256, tk=4096)
    gate0 = _mm("l0_ple_gate", x0a, w0b["ple_gate_w"], "nn", [F32], epi=lambda acc, bias: (acc + bias,), extras=[(W["ple_gate_b"][0:1], "row")], tm=1024)
    pp0 = _mm("l0_ple_proj", p[0], w0b["ple_proj"], "nn", [F32])
    xh0b, rs0b, x0b = _res_ln_ffpe("l0_ln_b", xh0a, ff0, gate0, pp0, lg(0, 0), lb(0, 0), lg(0, 1), lb(0, 1))

    wg_ = fetch("gdn", x0b)
    qkv_pre = _mm("gdn_in_qkv", x0b, wg_["gdn_wqkv"], "nn", [F32], tm=1024, tn=1024, b_outer=True)
    z = _mm("gdn_in_z", x0b, wg_["gdn_wz"], "nn", [F32], tm=1024)
    ba = _mm("gdn_in_ba", x0b, wg_["gdn_wba"], "nn", [F32])
    qkvn = _conv_fwd(qkv_pre, W["gdn_conv"])
    beta_pad, g_pad = _gates_fwd(ba, W["gdn_a_log"], W["gdn_dt_bias"])
    g_rows = g_pad[:, :8].reshape(N, C, 8).transpose(0, 2, 1)
    o, s_all = _delta_fwd(qkvn, g_pad, g_rows, beta_pad)
    og = _gate_norm_fwd(o, z, W["gdn_norm_w"])
    mix1 = _mm("gdn_out", og, wg_["gdn_w_out"], "nn", [F32], tm=1024)
    xh1a, rs1a, x1a = _res_ln_mix("l1_ln_a", xh0b, mix1, lg(0, 1), lb(0, 1), lg(1, 0), lb(1, 0))
    w1_ = fetch("l1", x1a)
    r1 = _mm("l1_mlp_up", x1a, w1_["mlp_w1"], "nn", [BF16], epi=lambda acc: (jnp.maximum(acc, 0.0),), tm=1024, tn=1024, b_outer=True)
    ff1 = _mm("l1_mlp_down", r1, w1_["mlp_w2"], "nn", [F32], a_fn=_square_bf16, tm=256, tk=4096)
    gate1 = _mm("l1_ple_gate", x1a, w1_["ple_gate_w"], "nn", [F32], epi=lambda acc, bias: (acc + bias,), extras=[(W["ple_gate_b"][1:2], "row")], tm=1024)
    pp1 = _mm("l1_ple_proj", p[1], w1_["ple_proj"], "nn", [F32])
    dh1b, dh1b_b, loss_cols, dg11, db11 = _final_ln_loss(xh1a, ff1, gate1, pp1, tgt, lg(1, 0), lb(1, 0), lg(1, 1), lb(1, 1))

    dpre1, dgate1, dw1_1, dw2_1, dwg_1, dbg_1, dwp_1 = _mlp_ple_dw(1, dh1b, dh1b_b, x1a, p[1], r1, gate1, pp1, w1_["mlp_w2"])
    tok = emit("l1", {"mlp_w1": dw1_1, "mlp_w2": dw2_1, "ple_gate_w": dwg_1, "ple_proj": dwp_1})
    dx1a = _mlp_ple_dx(1, dh1b, dpre1, dgate1, w1_["mlp_w1"], w1_["ple_gate_w"], [tok])
    dh1a, dh1a_b, dg10, db10 = _ln_bwd_call("l1_ln_a_bwd", dx1a, xh1a, rs1a, lg(1, 0))
    dog = _mm("gdn_out_bwd", dh1a_b, wg_["gdn_w_out"], "nt", [F32], tm=1024)
    dw_out = _mm("gdn_dw_out", og, dh1a_b, "tn", [F32])
    do, dz, dnw = _gate_norm_bwd(dog, o, z, W["gdn_norm_w"])
    dqkvn, dg_col, dg_row, dbeta = _delta_bwd(qkvn, g_pad, g_rows, beta_pad, s_all, do)
    dg_all = dg_col + jnp.pad(dg_row.transpose(0, 2, 1).reshape(S, 8), ((0, 0), (0, LANES - 8)))
    dba, dalog, ddt = _gates_bwd(ba, W["gdn_a_log"], W["gdn_dt_bias"], dbeta, dg_all, H)
    dqkv, dconv = _conv_bwd(qkv_pre, W["gdn_conv"], dqkvn)
    dwqkv = _mm("gdn_dwqkv", x0b, dqkv, "tn", [F32])
    dwz = _mm("gdn_dwz", x0b, dz, "tn", [F32])
    dwba = _mm("gdn_dwba", x0b, dba, "tn", [F32])
    dw_in = jnp.concatenate([dwqkv, dwz, dwba[:, :H], dwba[:, LANES : LANES + H]], axis=1)
    tok = emit("gdn", {"gdn_w_in": dw_in, "gdn_w_out": dw_out})
    t = _mm("gdn_in_ba_bwd", dba, wg_["gdn_wba"], "nt", [F32], epi=lambda acc, d: (acc + ALPHA * d,), extras=[(dh1a, "tile")], after=[tok])
    t = _mm("gdn_in_z_bwd", dz, wg_["gdn_wz"], "nt", [F32], epi=lambda acc, d: (acc + d,), extras=[(t, "tile")], tm=1024)
    dx0b = _mm("gdn_in_qkv_bwd", dqkv, wg_["gdn_wqkv"], "nt", [F32], epi=lambda acc, d: (acc + d,), extras=[(t, "tile")], tm=256, tk=3072)

    dh0b, dh0b_b, dg01, db01 = _ln_bwd_call("l0_ln_b_bwd", dx0b, xh0b, rs0b, lg(0, 1))
    dpre0, dgate0, dw1_0, dw2_0, dwg_0, dbg_0, dwp_0 = _mlp_ple_dw(0, dh0b, dh0b_b, x0a, p[0], r0, gate0, pp0, w0b["mlp_w2"])
    tok = emit("l0", {"mlp_w1": dw1_0, "mlp_w2": dw2_0, "ple_gate_w": dwg_0, "ple_proj": dwp_0})
    dx0a = _mlp_ple_dx(0, dh0b, dpre0, dgate0, w0a["mlp_w1"], w0b["ple_gate_w"], [tok])
    dh0a, _, dg00, db00 = _ln_bwd_call("l0_ln_a_bwd", dx0a, xh0a, rs0a, lg(0, 0))
    grad_x, dyp, dscale, dpb = _pool_bwd(dh0a, pooled, W["pool_w"], W["pool_b"], W["pool_scale"])
    G["pool_w"] = _pool_dw(pooled, dyp)

    G["ln_gain"] = jnp.concatenate([dg00, dg01, dg10, dg11], axis=0)
    G["ln_bias"] = jnp.concatenate([db00, db01, db10, db11], axis=0)
    G["pool_b"] = dpb
    G["pool_scale"] = dscale
    G["gdn_conv"] = dconv
    G["gdn_a_log"] = dalog[:, :H]
    G["gdn_dt_bias"] = ddt[:, :H]
    G["gdn_norm_w"] = dnw
    G["ple_gate_b"] = jnp.concatenate([dbg_0, dbg_1], axis=0)
    return loss_cols, grad_x, G


_HBM = pl.BlockSpec(memory_space=pltpu.HBM)


def _all_gather(name, shards):
    T = len(shards)

    def body(*refs):
        ins, outs = refs[:T], refs[T : 2 * T]
        send_sems, recv_sems, local_sems = refs[2 * T :]
        x, y, c = lax.axis_index("x"), lax.axis_index("y"), lax.axis_index("c")
        me, sibling = (x, y, c), (x, y, 1 - c)
        chips = [(1 - x, y), (x, 1 - y), (1 - x, 1 - y)]

        def blk(t, px, py, pc):
            return outs[t].at[4 * px + 2 * py + pc]

        def copy(t, k, block, to, src=None):
            return pltpu.make_async_remote_copy(
                src_ref=blk(t, *block) if src is None else src, dst_ref=blk(t, *block),
                send_sem=send_sems.at[t, k], recv_sem=recv_sems.at[t, k], device_id=to, device_id_type=MESH,
            )

        mine = [pltpu.make_async_copy(ins[t], blk(t, *me), local_sems.at[t]) for t in range(T)]
        for cp in mine:
            cp.start()
        first = []
        for t in range(T):
            first.append(copy(t, 0, me, sibling, src=ins[t]))
            first += [copy(t, 1 + j, me, (*chip, c), src=ins[t]) for j, chip in enumerate(chips)]
        for cp in first:
            cp.start()
        passed = []
        for j, chip in enumerate(chips):
            for t in range(T):
                copy(t, 1 + j, (*chip, c), me).wait_recv()
                fw = copy(t, 4 + j, (*chip, c), sibling)
                fw.start()
                passed.append(fw)
        for t in range(T):
            copy(t, 0, sibling, me).wait_recv()
            for j, chip in enumerate(chips):
                copy(t, 4 + j, (*chip, 1 - c), me).wait_recv()
        for cp in first + passed:
            cp.wait_send()
        for cp in mine:
            cp.wait()

    return pl.pallas_call(
        body,
        name=name,
        in_specs=[_HBM] * T,
        out_specs=[_HBM] * T,
        out_shape=[jax.ShapeDtypeStruct((N_DEV,) + s.shape, s.dtype) for s in shards],
        scratch_shapes=[pltpu.SemaphoreType.DMA((T, 7)), pltpu.SemaphoreType.DMA((T, 7)), pltpu.SemaphoreType.DMA((T,))],
    )(*shards)


def _exchange(name, blocks):
    def body(g_ref, o_ref, send_sems, recv_sems, local_sem):
        x, y, c = lax.axis_index("x"), lax.axis_index("y"), lax.axis_index("c")
        own = pltpu.make_async_copy(g_ref.at[4 * x + 2 * y + c], o_ref.at[N_DEV - 1], local_sem)
        own.start()
        copies = []
        for rel in range(1, N_DEV):
            px = 1 - x if rel & 4 else x
            py = 1 - y if rel & 2 else y
            pc = 1 - c if rel & 1 else c
            copies.append(
                pltpu.make_async_remote_copy(
                    src_ref=g_ref.at[4 * px + 2 * py + pc], dst_ref=o_ref.at[rel - 1],
                    send_sem=send_sems.at[rel - 1], recv_sem=recv_sems.at[rel - 1], device_id=(px, py, pc), device_id_type=MESH,
                )
            )
        for cp in copies:
            cp.start()
        for cp in copies:
            cp.wait_recv()
        for cp in copies:
            cp.wait_send()
        own.wait()

    return pl.pallas_call(
        body,
        name=name,
        in_specs=[_HBM],
        out_specs=_HBM,
        out_shape=jax.ShapeDtypeStruct(blocks.shape, blocks.dtype),
        scratch_shapes=[pltpu.SemaphoreType.DMA((N_DEV - 1,)), pltpu.SemaphoreType.DMA((N_DEV - 1,)), pltpu.SemaphoreType.DMA],
    )(blocks)


_SEM = pl.BlockSpec(memory_space=pltpu.SEMAPHORE)
_ANY = pl.BlockSpec(memory_space=pl.ANY)
_DATAFLOW = pltpu.SideEffectType.DATAFLOW_SIDE_EFFECTING
N_PEERS = N_DEV - 1


def _peer(rel, x, y, c):
    return (1 - x if rel & 4 else x, 1 - y if rel & 2 else y, 1 - c if rel & 1 else c)


def _send_start(name, src, land, gather, after):
    def body(src_ref, land_ref, after_ref, send_sems, recv_sems, src_thru, land_thru, token):
        x, y, c = lax.axis_index("x"), lax.axis_index("y"), lax.axis_index("c")
        for rel in range(1, N_DEV):
            px, py, pc = _peer(rel, x, y, c)
            pltpu.make_async_remote_copy(
                src_ref=src_ref if gather else src_ref.at[4 * px + 2 * py + pc],
                dst_ref=land_ref.at[4 * x + 2 * y + c] if gather else land_ref.at[rel - 1],
                send_sem=send_sems.at[rel - 1], recv_sem=recv_sems.at[rel - 1], device_id=(px, py, pc), device_id_type=MESH,
            ).start()
        token[...] = jnp.zeros_like(token)

    return pl.pallas_call(
        body,
        name=name,
        out_shape=(pltpu.SemaphoreType.DMA((N_PEERS,)), pltpu.SemaphoreType.DMA((N_PEERS,)), pltpu.HBM(src.shape, src.dtype),
                   pltpu.HBM(land.shape, land.dtype), jax.ShapeDtypeStruct((8, LANES), F32)),
        in_specs=(_HBM, _HBM, _ANY),
        out_specs=(_SEM, _SEM, _HBM, _HBM, pl.BlockSpec(memory_space=pltpu.VMEM)),
        input_output_aliases={0: 2, 1: 3},
        compiler_params=pltpu.CompilerParams(has_side_effects=_DATAFLOW),
    )(pltpu.with_memory_space_constraint(src, pltpu.HBM), pltpu.with_memory_space_constraint(land, pltpu.HBM), after)


def _send_wait(name, started, after, gather):
    send_sems, recv_sems, src_thru, land_thru, _ = started

    def body(src_ref, land_ref, send_sems, recv_sems, after_ref, src_dead, got_ref):
        x, y, c = lax.axis_index("x"), lax.axis_index("y"), lax.axis_index("c")
        for rel in range(1, N_DEV):
            cp = pltpu.make_async_remote_copy(
                src_ref=src_ref if gather else src_ref.at[0], dst_ref=land_ref.at[0],
                send_sem=send_sems.at[rel - 1], recv_sem=recv_sems.at[rel - 1], device_id=_peer(rel, x, y, c), device_id_type=MESH,
            )
            cp.wait_send()
            cp.wait_recv()

    return pl.pallas_call(
        body,
        name=name,
        out_shape=(pltpu.HBM(src_thru.shape, src_thru.dtype), pltpu.HBM(land_thru.shape, land_thru.dtype)),
        in_specs=(_HBM, _HBM, _SEM, _SEM, _ANY),
        out_specs=(_HBM, _HBM),
        input_output_aliases={0: 0, 1: 1},
        compiler_params=pltpu.CompilerParams(has_side_effects=_DATAFLOW),
    )(src_thru, land_thru, send_sems, recv_sems, after)


def _sum_blocks(name, parts, tr):
    n_parts, R, Cw = parts.shape
    tr = tr if R % tr == 0 else R

    def body(p_ref, o_ref):
        acc = p_ref[0].astype(F32)
        for d in range(1, n_parts):
            acc = acc + p_ref[d].astype(F32)
        o_ref[...] = acc

    return pl.pallas_call(
        body,
        name=name,
        grid=(R // tr,),
        in_specs=[pl.BlockSpec((n_parts, tr, Cw), lambda i: (0, i, 0))],
        out_specs=pl.BlockSpec((tr, Cw), lambda i: (i, 0)),
        out_shape=jax.ShapeDtypeStruct((R, Cw), F32),
        compiler_params=_params(1),
    )(parts)


def _adamw(name, w, g, m, v):
    shape = w.shape
    cols = shape[-1]
    rows = w.size // cols
    tr = rows if rows <= 512 else 512
    assert rows % tr == 0
    w2, g2, m2, v2 = (a.reshape(rows, cols) for a in (w, g, m, v))

    def body(w_ref, g_ref, m_ref, v_ref, d_ref, mo_ref, vo_ref):
        gv = g_ref[...]
        mn = ADAM_B1 * m_ref[...] + (1.0 - ADAM_B1) * gv
        vn = ADAM_B2 * v_ref[...] + (1.0 - ADAM_B2) * jnp.square(gv)
        m_hat = mn / (1.0 - ADAM_B1**ADAM_STEP)
        v_hat = vn / (1.0 - ADAM_B2**ADAM_STEP)
        d_ref[...] = -ADAM_LR * (m_hat / (jnp.sqrt(v_hat) + ADAM_EPS) + ADAM_WD * w_ref[...])
        mo_ref[...] = mn
        vo_ref[...] = vn

    spec = pl.BlockSpec((tr, cols), lambda i: (i, 0))
    d, mn, vn = pl.pallas_call(
        body,
        name=name,
        grid=(rows // tr,),
        in_specs=[spec] * 4,
        out_specs=[spec] * 3,
        out_shape=[jax.ShapeDtypeStruct((rows, cols), F32)] * 3,
        compiler_params=_params(1),
    )(w2, g2, m2, v2)
    return d.reshape(shape), mn.reshape(shape), vn.reshape(shape)


SMALL_SHARDED = ("ln_gain", "ln_bias", "pool_b", "gdn_conv")
SMALL_REPLICATED = ("pool_scale", "gdn_a_log", "gdn_dt_bias", "gdn_norm_w", "ple_gate_b")
WEIGHTS = ("ln_gain", "ln_bias", "pool_w", "pool_b", "pool_scale", "gdn_w_in", "gdn_conv", "gdn_a_log", "gdn_dt_bias",
           "gdn_norm_w", "gdn_w_out", "mlp_w1", "mlp_w2", "ple_gate_w", "ple_gate_b", "ple_proj")
BIG_AXIS = {"gdn_w_in": 1, "gdn_w_out": 0, "mlp_w1": 1, "mlp_w2": 0, "ple_gate_w": 0, "ple_proj": 1, "pool_w": 1}
GATHER_GROUPS = {
    "l0a": (("mlp_w1", 0),),
    "l0b": (("mlp_w2", 0), ("ple_gate_w", 0), ("ple_proj", 0)),
    "gdn": (("gdn_w_in", 0), ("gdn_w_out", 0)),
    "l1": (("mlp_w1", 1), ("mlp_w2", 1), ("ple_gate_w", 1), ("ple_proj", 1)),
}
GRAD_GROUPS = {
    "l1": (("mlp_w1", 1), ("mlp_w2", 1), ("ple_gate_w", 1), ("ple_proj", 1)),
    "gdn": (("gdn_w_in", 0), ("gdn_w_out", 0)),
    "l0": (("mlp_w1", 0), ("mlp_w2", 0), ("ple_gate_w", 0), ("ple_proj", 0)),
}
PACK_PART_ALIGN = 16
SUM_TILE = 128


def _part_rows(a, width):
    rows = a.size // width
    return rows + (-rows) % PACK_PART_ALIGN


def _pack_rows(parts, width, dtype, align):
    padded = []
    for a in parts:
        a2 = a.reshape(-1, width).astype(dtype)
        padded.append(jnp.pad(a2, ((0, _part_rows(a, width) - a2.shape[0]), (0, 0))))
    flat = jnp.concatenate(padded, axis=0)
    return jnp.pad(flat, ((0, (-flat.shape[0]) % align), (0, 0)))


def _pack_blocks(parts, width, dtype, align):
    padded = []
    for a in parts:
        a2 = a.reshape(a.shape[0], -1, width).astype(dtype)
        padded.append(jnp.pad(a2, ((0, 0), (0, _part_rows(a[0], width) - a2.shape[1]), (0, 0))))
    flat = jnp.concatenate(padded, axis=1)
    return jnp.pad(flat, ((0, 0), (0, (-flat.shape[1]) % align), (0, 0)))


def _unpack_rows(packed, shapes, width):
    out, off = [], 0
    for shp in shapes:
        size = 1
        for d in shp:
            size *= d
        out.append(packed[..., off : off + size // width, :].reshape(packed.shape[:-2] + tuple(shp)))
        off += size // width + (-(size // width)) % PACK_PART_ALIGN
    return out


def _split_blocks(name, full):
    ax = BIG_AXIS[name]
    shp = full.shape
    a = full.reshape(shp[:ax] + (N_DEV, shp[ax] // N_DEV) + shp[ax + 1 :])
    return jnp.moveaxis(a, ax, 0)


def _join_blocks(name, blocks):
    ax = BIG_AXIS[name]
    a = jnp.moveaxis(blocks, 0, ax)
    shp = a.shape
    return a.reshape(shp[:ax] + (shp[ax] * shp[ax + 1],) + shp[ax + 2 :])


def _pack_small(parts):
    flat = jnp.concatenate([jnp.pad(a.reshape(-1), (0, (-a.size) % LANES)) for a in parts])
    rows = flat.size // LANES
    return jnp.pad(flat.reshape(rows, LANES), ((0, (-rows) % 8), (0, 0)))


def _unpack_small(packed, shapes):
    flat = packed.reshape(packed.shape[:-2] + (-1,))
    out, off = [], 0
    for shp in shapes:
        size = 1
        for s in shp:
            size *= s
        out.append(flat[..., off : off + size].reshape(flat.shape[:-1] + tuple(shp)))
        off += size + (-size) % LANES
    return out


def _split_w_in(w_in, D, H):
    pad = lambda a: jnp.pad(a, ((0, 0), (0, LANES - H)))
    return w_in[:, : 3 * D], w_in[:, 3 * D : 4 * D], jnp.concatenate([pad(w_in[:, 4 * D : 4 * D + H]), pad(w_in[:, 4 * D + H :])], axis=1)


def kernel(x, p, ln_gain, ln_bias, pool_w, pool_b, pool_scale, gdn_w_in, gdn_conv, gdn_a_log, gdn_dt_bias, gdn_norm_w, gdn_w_out, mlp_w1, mlp_w2, ple_gate_w, ple_gate_b, ple_proj, loss_target, m_ln_gain, m_ln_bias, m_pool_w, m_pool_b, m_pool_scale, m_gdn_w_in, m_gdn_conv, m_gdn_a_log, m_gdn_dt_bias, m_gdn_norm_w, m_gdn_w_out, m_mlp_w1, m_mlp_w2, m_ple_gate_w, m_ple_gate_b, m_ple_proj, v_ln_gain, v_ln_bias, v_pool_w, v_pool_b, v_pool_scale, v_gdn_w_in, v_gdn_conv, v_gdn_a_log, v_gdn_dt_bias, v_gdn_norm_w, v_gdn_w_out, v_mlp_w1, v_mlp_w2, v_ple_gate_w, v_ple_gate_b, v_ple_proj):
    w_sh = dict(ln_gain=ln_gain, ln_bias=ln_bias, pool_w=pool_w, pool_b=pool_b, pool_scale=pool_scale, gdn_w_in=gdn_w_in,
                gdn_conv=gdn_conv, gdn_a_log=gdn_a_log, gdn_dt_bias=gdn_dt_bias, gdn_norm_w=gdn_norm_w, gdn_w_out=gdn_w_out,
                mlp_w1=mlp_w1, mlp_w2=mlp_w2, ple_gate_w=ple_gate_w, ple_gate_b=ple_gate_b, ple_proj=ple_proj)
    m_sh = dict(ln_gain=m_ln_gain, ln_bias=m_ln_bias, pool_w=m_pool_w, pool_b=m_pool_b, pool_scale=m_pool_scale, gdn_w_in=m_gdn_w_in,
                gdn_conv=m_gdn_conv, gdn_a_log=m_gdn_a_log, gdn_dt_bias=m_gdn_dt_bias, gdn_norm_w=m_gdn_norm_w, gdn_w_out=m_gdn_w_out,
                mlp_w1=m_mlp_w1, mlp_w2=m_mlp_w2, ple_gate_w=m_ple_gate_w, ple_gate_b=m_ple_gate_b, ple_proj=m_ple_proj)
    v_sh = dict(ln_gain=v_ln_gain, ln_bias=v_ln_bias, pool_w=v_pool_w, pool_b=v_pool_b, pool_scale=v_pool_scale, gdn_w_in=v_gdn_w_in,
                gdn_conv=v_gdn_conv, gdn_a_log=v_gdn_a_log, gdn_dt_bias=v_gdn_dt_bias, gdn_norm_w=v_gdn_norm_w, gdn_w_out=v_gdn_w_out,
                mlp_w1=v_mlp_w1, mlp_w2=v_mlp_w2, ple_gate_w=v_ple_gate_w, ple_gate_b=v_ple_gate_b, ple_proj=v_ple_proj)
    xs, tg = x[0], loss_target[0]
    ps = p[:, 0]
    S, D = xs.shape
    H = D // HEAD_DIM
    me = 4 * lax.axis_index("x") + 2 * lax.axis_index("y") + lax.axis_index("c")
    layer = lambda n, l: (w_sh[n][0] if n in ("gdn_w_in", "gdn_w_out") else w_sh[n][l])

    pool_packed = _pack_rows([w_sh["pool_w"][0]], D, BF16, PACK_PART_ALIGN)
    small_packed = _pack_small([w_sh[n] for n in SMALL_SHARDED])
    pool_gathered, small_gathered = _all_gather("gather_first", [pool_packed, small_packed])
    W = {"pool_w": _join_blocks("pool_w", _unpack_rows(pool_gathered, [w_sh["pool_w"][0].shape], D)[0])}

    started, tokens = {}, jnp.zeros((1, 1), F32)
    for g, members in GATHER_GROUPS.items():
        src = _pack_rows([layer(n, l) for n, l in members], D, BF16, PACK_PART_ALIGN)
        started[g] = tuple(_send_start(f"gather_{g}_start", src, lax.empty((N_DEV,) + src.shape, BF16), True, small_gathered))
        tokens = tokens + started[g][4][0:1, 0:1]
    smalls = _unpack_small(small_gathered, [w_sh[n].shape for n in SMALL_SHARDED])
    for n, a in zip(SMALL_SHARDED, smalls):
        W[n] = jnp.moveaxis(a, 0, -2).reshape(a.shape[1:-1] + (N_DEV * a.shape[-1],))
    W["ln_gain"] = W["ln_gain"].reshape(2 * DEPTH, D)
    W["ln_bias"] = W["ln_bias"].reshape(2 * DEPTH, D)
    W["pool_b"] = W["pool_b"].reshape(1, D) + tokens
    W["gdn_conv"] = W["gdn_conv"][0]
    W["pool_scale"] = pool_scale
    W["ple_gate_b"] = ple_gate_b
    W["gdn_norm_w"] = gdn_norm_w
    W["gdn_a_log"] = jnp.pad(gdn_a_log, ((0, 0), (0, LANES - H)))
    W["gdn_dt_bias"] = jnp.pad(gdn_dt_bias, ((0, 0), (0, LANES - H)))

    def fetch(g, after):
        members = GATHER_GROUPS[g]
        src, land = _send_wait(f"gather_{g}_wait", started[g], after, True)
        land = lax.dynamic_update_index_in_dim(land, src, me, 0)
        parts = _unpack_rows(land, [layer(n, l).shape for n, l in members], D)
        out = {n: _join_blocks(n, a) for (n, _), a in zip(members, parts)}
        if "gdn_w_in" in out:
            out["gdn_wqkv"], out["gdn_wz"], out["gdn_wba"] = _split_w_in(out.pop("gdn_w_in"), D, H)
        return out

    sent = {}

    def emit(g, grads):
        members = GRAD_GROUPS[g]
        blocks = [_split_blocks(n, grads[n]) for n, _ in members]
        src = _pack_blocks(blocks, D, BF16, SUM_TILE)
        sent[g] = tuple(_send_start(f"grads_{g}_start", src, lax.empty((N_PEERS,) + src.shape[1:], BF16), False, blocks[0]))
        return sent[g][4]

    loss_cols, grad_x, G = _local_step(xs, ps, tg, W, fetch, emit)
    loss = lax.psum(0.5 * jnp.sum(loss_cols) / D, MESH_AXES)

    pool_src = _pack_blocks([_split_blocks("pool_w", G["pool_w"])], D, BF16, PACK_PART_ALIGN)
    pool_sum = _sum_blocks("sum_pool_grads", _exchange("exchange_pool_grads", pool_src), SUM_TILE)
    grads = {"pool_w": _unpack_rows(pool_sum, [w_sh["pool_w"][0].shape], D)[0].reshape(w_sh["pool_w"].shape)}
    small_names = SMALL_SHARDED + SMALL_REPLICATED
    gs_packed = _pack_small([G[n] for n in small_names])
    (gs_all,) = _all_gather("gather_small_grads", [gs_packed])
    gs_sum = _sum_blocks("sum_small_grads", gs_all, SUM_TILE)
    for n, a in zip(small_names, _unpack_small(gs_sum, [G[n].shape for n in small_names])):
        if n in SMALL_SHARDED:
            width = w_sh[n].shape[-1]
            a = a.reshape(w_sh[n].shape[:-1] + (N_DEV * width,))
            a = lax.dynamic_slice_in_dim(a, me * width, width, axis=a.ndim - 1)
        grads[n] = a.reshape(w_sh[n].shape)

    per_layer = {}
    for g, members in GRAD_GROUPS.items():
        src, land = _send_wait(f"grads_{g}_wait", sent[g], grad_x, False)
        own = lax.dynamic_index_in_dim(src, me, 0, keepdims=True)
        total = _sum_blocks(f"sum_grads_{g}", jnp.concatenate([land, own], axis=0), SUM_TILE)
        for (n, l), a in zip(members, _unpack_rows(total, [layer(n, l).shape for n, l in members], D)):
            per_layer[(n, l)] = a
    for n in ("gdn_w_in", "gdn_w_out"):
        grads[n] = per_layer[(n, 0)][None]
    for n in ("mlp_w1", "mlp_w2", "ple_gate_w", "ple_proj"):
        grads[n] = jnp.stack([per_layer[(n, 0)], per_layer[(n, 1)]])

    deltas, new_m, new_v = {}, {}, {}
    for n in WEIGHTS:
        deltas[n], new_m[n], new_v[n] = _adamw(f"adamw_{n}", w_sh[n], grads[n], m_sh[n], v_sh[n])
    return (loss, grad_x[None], *[grads[n] for n in WEIGHTS], *[deltas[n] for n in WEIGHTS],
            *[new_m[n] for n in WEIGHTS], *[new_v[n] for n in WEIGHTS])
```

```python
import functools

import jax
import jax.numpy as jnp
from jax import lax
from jax.experimental import pallas as pl
from jax.experimental.pallas import tpu as pltpu

F32 = jnp.float32
BF16 = jnp.bfloat16
MESH_AXES = ("x", "y", "c")
N_DEV = 8
MESH = pl.DeviceIdType.MESH

DEPTH = 2
ALPHA = (2.0 * DEPTH) ** 0.25
LN_EPS = 1e-5
RMS_EPS = 1e-6
L2_EPS = 1e-6
HEAD_DIM = 128
CONV_WIDTH = 4
POOL_WINDOWS = (2, 4, 8, 16)
POOL_HALO = 16
CONV_HALO = 8
LANES = 128
ADAM_LR = 0.001
ADAM_B1 = 0.9
ADAM_B2 = 0.999
ADAM_EPS = 1e-08
ADAM_WD = 0.01
ADAM_STEP = 10

VMEM_LIMIT = 56 * 1024 * 1024
ROW_TILE = 256
CONV_TILE = 256
CHUNK = 128
MM_TM, MM_TN, MM_TK = 512, 1024, 1024
DW_TILES = dict(tm=256, tn=512, tk=8192, b_outer=True)

_DIMS = {
    "nn": (((1,), (0,)), ((), ())),
    "nt": (((1,), (1,)), ((), ())),
    "tn": (((0,), (0,)), ((), ())),
}


def _params(n_axes):
    return pltpu.CompilerParams(dimension_semantics=("arbitrary",) * n_axes, vmem_limit_bytes=VMEM_LIMIT)


def _fit(tile, n):
    tile = min(tile, n)
    while n % tile:
        tile //= 2
    return tile


def _mm(name, a, b, mode, out_dtypes, epi=None, extras=(), a_fn=None, tm=None, tn=None, tk=None, b_outer=False, after=()):
    if mode == "tn":
        K, M = a.shape
    else:
        M, K = a.shape
    N = b.shape[0] if mode == "nt" else b.shape[1]
    tm, tn, tk = _fit(tm or MM_TM, M), _fit(tn or MM_TN, N), _fit(tk or MM_TK, K)
    nk = K // tk

    def at(f):
        return (lambda j, i, k: f(i, j, k)) if b_outer else f

    a_spec = pl.BlockSpec((tk, tm), at(lambda i, j, k: (k, i))) if mode == "tn" else pl.BlockSpec((tm, tk), at(lambda i, j, k: (i, k)))
    b_spec = pl.BlockSpec((tn, tk), at(lambda i, j, k: (j, k))) if mode == "nt" else pl.BlockSpec((tk, tn), at(lambda i, j, k: (k, j)))
    ex_specs = [
        pl.BlockSpec((tm, tn), at(lambda i, j, k: (i, j))) if kind == "tile" else pl.BlockSpec((1, tn), at(lambda i, j, k: (0, j)))
        for _, kind in extras
    ]
    n_ex, n_out, n_after = len(extras), len(out_dtypes), len(after)

    def body(*refs):
        a_ref, b_ref = refs[0], refs[1]
        ex_refs = refs[2 : 2 + n_ex]
        out_refs = refs[2 + n_ex + n_after : 2 + n_ex + n_after + n_out]
        av = a_ref[...]
        if a_fn is not None:
            av = a_fn(av)
        part = lax.dot_general(av.astype(BF16), b_ref[...].astype(BF16), _DIMS[mode], preferred_element_type=F32)

        def finish(res):
            vals = epi(res, *[e[...] for e in ex_refs]) if epi is not None else (res,)
            for o_ref, v in zip(out_refs, vals):
                o_ref[...] = v.astype(o_ref.dtype)

        if nk == 1:
            finish(part)
        else:
            acc = refs[-1]
            k = pl.program_id(2)

            @pl.when(k == 0)
            def _():
                acc[...] = part

            @pl.when(k > 0)
            def _():
                acc[...] += part

            @pl.when(k == nk - 1)
            def _():
                finish(acc[...])

    outs = pl.pallas_call(
        body,
        name=name,
        grid=(N // tn, M // tm, nk) if b_outer else (M // tm, N // tn, nk),
        in_specs=[a_spec, b_spec] + ex_specs + [pl.BlockSpec(memory_space=pl.ANY)] * n_after,
        out_specs=[pl.BlockSpec((tm, tn), at(lambda i, j, k: (i, j))) for _ in out_dtypes],
        out_shape=[jax.ShapeDtypeStruct((M, N), dt) for dt in out_dtypes],
        scratch_shapes=[pltpu.VMEM((tm, tn), F32)] if nk > 1 else [],
        compiler_params=_params(3),
    )(a, b, *[e for e, _ in extras], *after)
    return outs[0] if n_out == 1 else outs


def _rowwise(name, fn, S, ts, rows=(), halos=(), consts=(), outs=(), accs=()):
    ts = min(ts, S)
    assert S % ts == 0
    n = S // ts
    in_specs = [pl.BlockSpec((ts, a.shape[1]), lambda i: (i, 0)) for a in rows]
    for a, kind, hr in halos:
        r, nb = ts // hr, S // hr
        if kind == "prev":
            in_specs.append(pl.BlockSpec((hr, a.shape[1]), lambda i, r=r: (jnp.maximum(i * r - 1, 0), 0)))
        else:
            in_specs.append(pl.BlockSpec((hr, a.shape[1]), lambda i, r=r, nb=nb: (jnp.minimum((i + 1) * r, nb - 1), 0)))
    in_specs += [pl.BlockSpec(a.shape, lambda i, nd=a.ndim: (0,) * nd) for a in consts]
    out_specs = [pl.BlockSpec((ts, w), lambda i: (i, 0)) for w, _ in outs]
    out_specs += [pl.BlockSpec((r, w), lambda i: (0, 0)) for r, w in accs]
    out_shape = [jax.ShapeDtypeStruct((S, w), dt) for w, dt in outs]
    out_shape += [jax.ShapeDtypeStruct((r, w), F32) for r, w in accs]
    nr, nh, nc, no = len(rows), len(halos), len(consts), len(outs)

    def body(*refs):
        i = pl.program_id(0)
        rv = [r[...] for r in refs[:nr]]
        hv = [r[...] for r in refs[nr : nr + nh]]
        cv = [r[...] for r in refs[nr + nh : nr + nh + nc]]
        o_refs = refs[nr + nh + nc : nr + nh + nc + no]
        a_refs = refs[nr + nh + nc + no :]
        ovals, avals = fn(i, n, rv, hv, cv)
        for o_ref, v in zip(o_refs, ovals):
            o_ref[...] = v.astype(o_ref.dtype)
        for a_ref, v in zip(a_refs, avals):

            @pl.when(i == 0)
            def _(a_ref=a_ref, v=v):
                a_ref[...] = v

            @pl.when(i > 0)
            def _(a_ref=a_ref, v=v):
                a_ref[...] += v

    res = pl.pallas_call(
        body,
        name=name,
        grid=(n,),
        in_specs=in_specs,
        out_specs=out_specs,
        out_shape=out_shape,
        compiler_params=_params(1),
    )(*rows, *[h[0] for h in halos], *consts)
    return list(res)


def _ln(h, g, b):
    mu = jnp.mean(h, axis=-1, keepdims=True)
    d = h - mu
    var = jnp.mean(d * d, axis=-1, keepdims=True)
    rstd = lax.rsqrt(var + LN_EPS)
    xhat = d * rstd
    return xhat, rstd, xhat * g + b


def _ln_bwd(dy, xhat, rstd, g):
    dxh = dy * g
    m1 = jnp.mean(dxh, axis=-1, keepdims=True)
    m2 = jnp.mean(dxh * xhat, axis=-1, keepdims=True)
    dh = rstd * (dxh - m1 - xhat * m2)
    return dh, jnp.sum(dy * xhat, axis=0, keepdims=True), jnp.sum(dy, axis=0, keepdims=True)


def _wide(col, ts):
    return jnp.broadcast_to(col, (ts, LANES))


def _pool_fwd(x, wp, pb, ps, g, b):
    S, D = x.shape
    gw = D // len(POOL_WINDOWS)
    ts = min(ROW_TILE, S)

    def fn(i, n, rv, hv, cv):
        (xc,), (xp,) = rv, hv
        wpv, pbv, psv, gv, bv = cv
        xp = jnp.where(i > 0, xp, 0.0)
        xx = jnp.concatenate([xp, xc], axis=0)
        t = i * ts + lax.broadcasted_iota(jnp.int32, (ts, 1), 0)
        pooled, ys = [], []
        for gi, w in enumerate(POOL_WINDOWS):
            s = xx[:, gi * gw : (gi + 1) * gw]
            k = 1
            while k < w:
                s = s + pltpu.roll(s, k, axis=0)
                k *= 2
            cnt = jnp.minimum(t + 1, w).astype(F32)
            pg = (s[POOL_HALO:, :] / cnt - xc[:, gi * gw : (gi + 1) * gw]).astype(BF16)
            pooled.append(pg)
            ys.append(jnp.dot(pg, wpv[gi], preferred_element_type=F32))
        y = jnp.concatenate(ys, axis=1)
        h = ALPHA * xc + (y + pbv) * psv
        xhat, rstd, xa = _ln(h, gv, bv)
        return (jnp.concatenate(pooled, axis=1), xhat, _wide(rstd, ts), xa), ()

    return _rowwise(
        "pool_fwd", fn, S, ts, rows=[x], halos=[(x, "prev", POOL_HALO)], consts=[wp, pb, ps, g, b],
        outs=[(D, BF16), (D, F32), (LANES, F32), (D, BF16)],
    )


def _pool_bwd(dh, pooled, wp, pb, ps):
    S, D = dh.shape
    gw = D // len(POOL_WINDOWS)
    ts = min(ROW_TILE, S)
    te = ts + POOL_HALO

    def fn(i, n, rv, hv, cv):
        (dhc, pc), (dhn,) = rv, hv
        wpv, pbv, psv = cv
        dhn = jnp.where(i < n - 1, dhn, 0.0)
        dy_ext = jnp.concatenate([dhc, dhn], axis=0) * psv
        dyb = dy_ext.astype(BF16)
        t = i * ts + lax.broadcasted_iota(jnp.int32, (te, 1), 0)
        dxs, ys = [], []
        for gi, w in enumerate(POOL_WINDOWS):
            sl = slice(gi * gw, (gi + 1) * gw)
            dp = lax.dot_general(dyb[:, sl], wpv[gi], _DIMS["nt"], preferred_element_type=F32)
            s = dp / jnp.minimum(t + 1, w).astype(F32)
            k = 1
            while k < w:
                s = s + pltpu.roll(s, k, axis=0)
                k *= 2
            s = pltpu.roll(s, POOL_HALO - (w - 1), axis=0)
            dxs.append(s[POOL_HALO:, :] - dp[:ts, :])
            ys.append(jnp.dot(pc[:, sl], wpv[gi], preferred_element_type=F32))
        dx = ALPHA * dhc + jnp.concatenate(dxs, axis=1)
        y = jnp.concatenate(ys, axis=1) + pbv
        dscale = jnp.sum(dhc * y, axis=0, keepdims=True)
        dbias = jnp.sum(dy_ext[:ts, :], axis=0, keepdims=True)
        return (dx, dyb[:ts, :]), (dscale, dbias)

    return _rowwise(
        "pool_bwd", fn, S, ts, rows=[dh, pooled], halos=[(dh, "next", POOL_HALO)], consts=[wp, pb, ps],
        outs=[(D, F32), (D, BF16)], accs=[(1, D), (1, D)],
    )


def _pool_dw(pooled, dy):
    S, D = pooled.shape
    G = len(POOL_WINDOWS)
    gw = D // G
    tk = min(MM_TK, S)
    nk = S // tk

    def body(p_ref, d_ref, o_ref):
        k = pl.program_id(1)
        part = lax.dot_general(p_ref[...], d_ref[...], _DIMS["tn"], preferred_element_type=F32)

        @pl.when(k == 0)
        def _():
            o_ref[...] = part

        @pl.when(k > 0)
        def _():
            o_ref[...] += part

    return pl.pallas_call(
        body,
        name="pool_dw",
        grid=(G, nk),
        in_specs=[pl.BlockSpec((tk, gw), lambda g, k: (k, g)), pl.BlockSpec((tk, gw), lambda g, k: (k, g))],
        out_specs=pl.BlockSpec((None, gw, gw), lambda g, k: (g, 0, 0)),
        out_shape=jax.ShapeDtypeStruct((G, gw, gw), F32),
        compiler_params=_params(2),
    )(pooled, dy)


def _res_ln_mix(name, xhat_p, mix, gp_, bp_, g, b):
    S, D = xhat_p.shape
    ts = min(ROW_TILE, S)

    def fn(i, n, rv, hv, cv):
        xh, m = rv
        gpv, bpv, gv, bv = cv
        xhat, rstd, xo = _ln(ALPHA * (xh * gpv + bpv) + m, gv, bv)
        return (xhat, _wide(rstd, ts), xo), ()

    return _rowwise(name, fn, S, ts, rows=[xhat_p, mix], consts=[gp_, bp_, g, b], outs=[(D, F32), (LANES, F32), (D, BF16)])


def _res_ln_ffpe(name, xhat_p, ff, gate, pp, gp_, bp_, g, b):
    S, D = xhat_p.shape
    ts = min(ROW_TILE, S)

    def fn(i, n, rv, hv, cv):
        xh, f, gt, p_ = rv
        gpv, bpv, gv, bv = cv
        xhat, rstd, xo = _ln(ALPHA * (xh * gpv + bpv) + f + jax.nn.sigmoid(gt) * p_, gv, bv)
        return (xhat, _wide(rstd, ts), xo), ()

    return _rowwise(name, fn, S, ts, rows=[xhat_p, ff, gate, pp], consts=[gp_, bp_, g, b], outs=[(D, F32), (LANES, F32), (D, BF16)])


def _final_ln_loss(xhat_p, ff, gate, pp, tgt, gp_, bp_, g, b):
    S, D = xhat_p.shape
    ts = min(ROW_TILE, S)

    def fn(i, n, rv, hv, cv):
        xh, f, gt, p_, tg = rv
        gpv, bpv, gv, bv = cv
        xhat, rstd, y = _ln(ALPHA * (xh * gpv + bpv) + f + jax.nn.sigmoid(gt) * p_, gv, bv)
        e = y - tg
        dh, dg, db = _ln_bwd(e * (1.0 / D), xhat, rstd, gv)
        return (dh, dh), (jnp.sum(e * e, axis=0, keepdims=True), dg, db)

    return _rowwise(
        "final_ln_loss", fn, S, ts, rows=[xhat_p, ff, gate, pp, tgt], consts=[gp_, bp_, g, b],
        outs=[(D, F32), (D, BF16)], accs=[(1, D), (1, D), (1, D)],
    )


def _ln_bwd_call(name, dy, xhat, rstd, g):
    S, D = dy.shape
    ts = min(ROW_TILE, S)

    def fn(i, n, rv, hv, cv):
        dyv, xh, rs = rv
        dh, dg, db = _ln_bwd(dyv, xh, rs[:, :1], cv[0])
        return (dh, dh), (dg, db)

    return _rowwise(name, fn, S, ts, rows=[dy, xhat, rstd], consts=[g], outs=[(D, F32), (D, BF16)], accs=[(1, D), (1, D)])


def _ple_bwd(name, dh, gate, pp):
    S, D = dh.shape
    ts = min(ROW_TILE, S)

    def fn(i, n, rv, hv, cv):
        d, gt, p_ = rv
        sg = jax.nn.sigmoid(gt)
        dgt = d * p_ * sg * (1.0 - sg)
        return (dgt, d * sg), (jnp.sum(dgt, axis=0, keepdims=True),)

    return _rowwise(name, fn, S, ts, rows=[dh, gate, pp], outs=[(D, BF16), (D, BF16)], accs=[(1, D)])


def _silu(c):
    return c * jax.nn.sigmoid(c)


def _qkv_point(c, is_qk, scale):
    s = _silu(c)
    nrm = s * lax.rsqrt(jnp.sum(s * s, axis=-1, keepdims=True) + L2_EPS) * scale
    return jnp.where(is_qk, nrm, s)


def _conv_rows(xx, wv, lo, rows):
    acc = None
    for j in range(CONV_WIDTH):
        sh = CONV_WIDTH - 1 - j
        term = (pltpu.roll(xx, sh, axis=0) if sh else xx)[lo : lo + rows, :] * wv[j : j + 1, :]
        acc = term if acc is None else acc + term
    return acc


def _conv_fwd(qkv_pre, conv_w):
    S, W = qkv_pre.shape
    D = W // 3
    H = D // HEAD_DIM
    ts = min(CONV_TILE, S)
    r = ts // CONV_HALO

    def body(x_ref, xp_ref, w_ref, o_ref):
        j, i = pl.program_id(0), pl.program_id(1)
        xp = jnp.where(i > 0, xp_ref[...], 0.0)
        xx = jnp.concatenate([xp, x_ref[...]], axis=0)
        c = _conv_rows(xx, w_ref[...], CONV_HALO, ts)
        scale = jnp.where(j == 0, HEAD_DIM**-0.5, 1.0).astype(F32)
        for h in range(H):
            sl = slice(h * HEAD_DIM, (h + 1) * HEAD_DIM)
            o_ref[:, sl] = _qkv_point(c[:, sl], j < 2, scale)

    return pl.pallas_call(
        body,
        name="gdn_conv_fwd",
        grid=(3, S // ts),
        in_specs=[
            pl.BlockSpec((ts, D), lambda j, i: (i, j)),
            pl.BlockSpec((CONV_HALO, D), lambda j, i: (jnp.maximum(i * r - 1, 0), j)),
            pl.BlockSpec((CONV_WIDTH, D), lambda j, i: (0, j)),
        ],
        out_specs=pl.BlockSpec((ts, D), lambda j, i: (i, j)),
        out_shape=jax.ShapeDtypeStruct((S, W), F32),
        compiler_params=_params(2),
    )(qkv_pre, qkv_pre, conv_w)


def _conv_bwd(qkv_pre, conv_w, dqkvn):
    S, W = qkv_pre.shape
    D = W // 3
    H = D // HEAD_DIM
    ts = min(CONV_TILE, S)
    r, nb = ts // CONV_HALO, S // CONV_HALO
    te = ts + CONV_HALO

    def body(x_ref, xp_ref, xn_ref, w_ref, d_ref, dn_ref, dx_ref, dw_ref):
        j, i = pl.program_id(0), pl.program_id(1)
        n = pl.num_programs(1)
        wv = w_ref[...]
        xp = jnp.where(i > 0, xp_ref[...], 0.0)
        xx = jnp.concatenate([xp, x_ref[...], xn_ref[...]], axis=0)
        c = _conv_rows(xx, wv, CONV_HALO, te)
        dn = jnp.where(i < n - 1, dn_ref[...], 0.0)
        dout = jnp.concatenate([d_ref[...], dn], axis=0)
        scale = jnp.where(j == 0, HEAD_DIM**-0.5, 1.0).astype(F32)
        dcs = []
        for h in range(H):
            sl = slice(h * HEAD_DIM, (h + 1) * HEAD_DIM)
            _, vjp = jax.vjp(lambda cc: _qkv_point(cc, j < 2, scale), c[:, sl])
            dcs.append(vjp(dout[:, sl])[0])
        dc = jnp.concatenate(dcs, axis=1)
        dx = None
        dws = []
        for jj in range(CONV_WIDTH):
            sh = CONV_WIDTH - 1 - jj
            term = pltpu.roll(dc, CONV_HALO - sh, axis=0)[CONV_HALO:, :] * wv[jj : jj + 1, :]
            dx = term if dx is None else dx + term
            xs = (pltpu.roll(xx, sh, axis=0) if sh else xx)[CONV_HALO : CONV_HALO + ts, :]
            dws.append(jnp.sum(dc[:ts, :] * xs, axis=0, keepdims=True))
        dx_ref[...] = dx.astype(dx_ref.dtype)
        dw = jnp.concatenate(dws, axis=0)

        @pl.when(i == 0)
        def _():
            dw_ref[...] = dw

        @pl.when(i > 0)
        def _():
            dw_ref[...] += dw

    return pl.pallas_call(
        body,
        name="gdn_conv_bwd",
        grid=(3, S // ts),
        in_specs=[
            pl.BlockSpec((ts, D), lambda j, i: (i, j)),
            pl.BlockSpec((CONV_HALO, D), lambda j, i: (jnp.maximum(i * r - 1, 0), j)),
            pl.BlockSpec((CONV_HALO, D), lambda j, i: (jnp.minimum((i + 1) * r, nb - 1), j)),
            pl.BlockSpec((CONV_WIDTH, D), lambda j, i: (0, j)),
            pl.BlockSpec((ts, D), lambda j, i: (i, j)),
            pl.BlockSpec((CONV_HALO, D), lambda j, i: (jnp.minimum((i + 1) * r, nb - 1), j)),
        ],
        out_specs=[pl.BlockSpec((ts, D), lambda j, i: (i, j)), pl.BlockSpec((CONV_WIDTH, D), lambda j, i: (0, j))],
        out_shape=[jax.ShapeDtypeStruct((S, W), BF16), jax.ShapeDtypeStruct((CONV_WIDTH, W), F32)],
        compiler_params=_params(2),
    )(qkv_pre, qkv_pre, qkv_pre, conv_w, dqkvn, dqkvn)


def _softplus(x):
    pos = x > 0.0
    return jnp.where(pos, x, 0.0) + jnp.log(1.0 + jnp.exp(jnp.where(pos, -x, x)))


def _gates(bl, al, alog, dt):
    return jax.nn.sigmoid(bl), -jnp.exp(alog) * _softplus(al + dt)


def _gates_fwd(ba, alog, dt):
    S = ba.shape[0]
    ts = min(ROW_TILE, S)

    def fn(i, n, rv, hv, cv):
        return _gates(rv[0][:, :LANES], rv[0][:, LANES:], cv[0], cv[1]), ()

    return _rowwise("gdn_gates_fwd", fn, S, ts, rows=[ba], consts=[alog, dt], outs=[(LANES, F32), (LANES, F32)])


def _gates_bwd(ba, alog, dt, dbeta, dg, H):
    S = ba.shape[0]
    ts = min(ROW_TILE, S)

    def fn(i, n, rv, hv, cv):
        bav, dbv, dgv = rv
        real = lax.broadcasted_iota(jnp.int32, (1, LANES), 1) < H
        _, vjp = jax.vjp(_gates, bav[:, :LANES], bav[:, LANES:], cv[0], cv[1])
        dbl, dal, dalog, ddt = vjp((jnp.where(real, dbv, 0.0), jnp.where(real, dgv, 0.0)))
        dbl, dal = jnp.where(real, dbl, 0.0), jnp.where(real, dal, 0.0)
        return (jnp.concatenate([dbl, dal], axis=1),), (jnp.where(real, dalog, 0.0), jnp.where(real, ddt, 0.0))

    return _rowwise(
        "gdn_gates_bwd", fn, S, ts, rows=[ba, dbeta, dg], consts=[alog, dt], outs=[(2 * LANES, BF16)],
        accs=[(1, LANES), (1, LANES)],
    )


def _split_bf16(a, n):
    parts, rest = [], a
    for _ in range(n):
        piece = rest.astype(BF16)
        parts.append(piece)
        rest = rest - piece.astype(F32)
    return parts


def _tri_dot(a, b, mode, tri):
    d = lambda u, v: lax.dot_general(u, v, _DIMS[mode], preferred_element_type=F32)
    if tri == 0:
        return sum(d(a.astype(BF16), piece) for piece in _split_bf16(b, 3))
    return sum(d(piece, b.astype(BF16)) for piece in _split_bf16(a, 3))


def _bdot_raw(a, b, mode):
    return lax.dot_general(a.astype(BF16), b.astype(BF16), _DIMS[mode], preferred_element_type=F32)


@functools.partial(jax.custom_vjp, nondiff_argnums=(2,))
def _bdot(a, b, mode):
    return _bdot_raw(a, b, mode)


def _bdot_fwd(a, b, mode):
    return _bdot_raw(a, b, mode), (a, b)


def _bdot_bwd(mode, res, ct):
    a, b = res
    if mode == "nn":
        return _bdot(ct, b, "nt"), _bdot(a, ct, "tn")
    if mode == "nt":
        return _bdot(ct, b, "nn"), _bdot(ct, a, "tn")
    return _bdot(b, ct, "nt"), _bdot(a, ct, "nn")


_bdot.defvjp(_bdot_fwd, _bdot_bwd)


@jax.custom_vjp
def _unit_lower_inverse(a_strict):
    return _unit_lower_inverse_raw(a_strict)


def _unit_lower_inverse_fwd(a_strict):
    t = _unit_lower_inverse_raw(a_strict)
    return t, t


def _unit_lower_inverse_bwd(t, ct):
    left = [_bdot(ti, ci, "tn") for ti, ci in zip(t, ct)]
    return (tuple(-_bdot(li, ti, "nt") for li, ti in zip(left, t)),)


_unit_lower_inverse.defvjp(_unit_lower_inverse_fwd, _unit_lower_inverse_bwd)


def _unit_lower_inverse_raw(a_strict):
    C = a_strict[0].shape[0]
    ii = lax.broadcasted_iota(jnp.int32, (C, C), 0)
    jj = lax.broadcasted_iota(jnp.int32, (C, C), 1)
    eye = (ii == jj).astype(F32)
    blk = 16
    same = (ii // blk) == (jj // blk)
    p = [-jnp.where(same, a, 0.0) for a in a_strict]
    t = [eye + x for x in p]
    for _ in range(3):
        p = [_bdot(x, x, "nn") for x in p]
        t = [ti + _bdot(ti, x, "nn") for ti, x in zip(t, p)]
    while blk < C:
        same2 = (ii // (2 * blk)) == (jj // (2 * blk))
        off = jnp.logical_and(same2, jnp.logical_not(same))
        te = [_bdot(ti, jnp.where(off, a, 0.0), "nn") for ti, a in zip(t, a_strict)]
        t = [ti - _bdot(x, ti, "nn") for ti, x in zip(t, te)]
        same, blk = same2, 2 * blk
    return tuple(t)


def _chunk_heads(q, k, v, gc_col, gc_row, b_col, s0):
    R = range(len(q))
    C = q[0].shape[0]
    ii = lax.broadcasted_iota(jnp.int32, (C, C), 0)
    jj = lax.broadcasted_iota(jnp.int32, (C, C), 1)
    rows = lax.broadcasted_iota(jnp.int32, (C, 1), 0)
    decay = [jnp.where(ii >= jj, jnp.exp(jnp.minimum(gc_col[h] - gc_row[h], 0.0)), 0.0) for h in R]
    kb = [k[h] * b_col[h] for h in R]
    a = [_bdot(kb[h], k[h], "nt") * decay[h] for h in R]
    qk = [_bdot(q[h], k[h], "nt") * decay[h] for h in R]
    t = _unit_lower_inverse(tuple(jnp.where(ii > jj, a[h], 0.0) for h in R))
    eg = [jnp.exp(gc_col[h]) for h in R]
    u = [_bdot(t[h], v[h] * b_col[h], "nn") for h in R]
    w = [_bdot(t[h], kb[h] * eg[h], "nn") for h in R]
    g_last = [jnp.sum(jnp.where(rows == C - 1, gc_col[h], 0.0), axis=0, keepdims=True) for h in R]
    kd = [k[h] * jnp.exp(g_last[h] - gc_col[h]) for h in R]
    ws = [_bdot(w[h], s0[h], "nn") for h in R]
    qs = [_bdot(q[h] * eg[h], s0[h], "nn") for h in R]
    v_new = [u[h] - ws[h] for h in R]
    o = [qs[h] + _bdot(qk[h], v_new[h], "nn") for h in R]
    s1 = [s0[h] * jnp.exp(g_last[h]) + _bdot(kd[h], v_new[h], "tn") for h in R]
    return tuple(o), tuple(s1)


def _pick_lane(a, h):
    lanes = lax.broadcasted_iota(jnp.int32, a.shape, 1)
    return jnp.sum(jnp.where(lanes == h, a, 0.0), axis=1, keepdims=True)


def _pick_row(a, h):
    rows = lax.broadcasted_iota(jnp.int32, a.shape, 0)
    return jnp.sum(jnp.where(rows == h, a, 0.0), axis=0, keepdims=True)


def _tri(C):
    ii = lax.broadcasted_iota(jnp.int32, (C, C), 0)
    jj = lax.broadcasted_iota(jnp.int32, (C, C), 1)
    return (ii >= jj).astype(F32)


def _delta_fwd(qkvn, g_pad, g_rows, beta_pad):
    S, W = qkvn.shape
    D = W // 3
    H = D // HEAD_DIM
    C = min(CHUNK, S)
    N = S // C

    def body(x_ref, gp_ref, gr_ref, bp_ref, o_ref, sall_ref, st):
        n = pl.program_id(0)

        @pl.when(n == 0)
        def _():
            st[...] = jnp.zeros_like(st)

        low = _tri(C)
        gc_cols = _tri_dot(low, gp_ref[...], "nn", 0)
        gc_rows = _tri_dot(gr_ref[...], low, "nt", 1)
        bcols = bp_ref[...]
        hs = range(H)
        s0 = tuple(st[h] for h in hs)
        for h in hs:
            sall_ref[h] = s0[h]
        o, s1 = _chunk_heads(
            tuple(x_ref[:, h * HEAD_DIM : (h + 1) * HEAD_DIM] for h in hs),
            tuple(x_ref[:, D + h * HEAD_DIM : D + (h + 1) * HEAD_DIM] for h in hs),
            tuple(x_ref[:, 2 * D + h * HEAD_DIM : 2 * D + (h + 1) * HEAD_DIM] for h in hs),
            tuple(_pick_lane(gc_cols, h) for h in hs), tuple(_pick_row(gc_rows, h) for h in hs),
            tuple(_pick_lane(bcols, h) for h in hs), s0,
        )
        for h in hs:
            st[h] = s1[h]
            o_ref[:, h * HEAD_DIM : (h + 1) * HEAD_DIM] = o[h]

    return pl.pallas_call(
        body,
        name="gdn_delta_fwd",
        grid=(N,),
        in_specs=[
            pl.BlockSpec((C, W), lambda n: (n, 0)),
            pl.BlockSpec((C, LANES), lambda n: (n, 0)),
            pl.BlockSpec((None, 8, C), lambda n: (n, 0, 0)),
            pl.BlockSpec((C, LANES), lambda n: (n, 0)),
        ],
        out_specs=[pl.BlockSpec((C, D), lambda n: (n, 0)), pl.BlockSpec((None, H, HEAD_DIM, HEAD_DIM), lambda n: (n, 0, 0, 0))],
        out_shape=[jax.ShapeDtypeStruct((S, D), F32), jax.ShapeDtypeStruct((N, H, HEAD_DIM, HEAD_DIM), F32)],
        scratch_shapes=[pltpu.VMEM((H, HEAD_DIM, HEAD_DIM), F32)],
        compiler_params=_params(1),
    )(qkvn, g_pad, g_rows, beta_pad)


def _delta_bwd(qkvn, g_pad, g_rows, beta_pad, s_all, do):
    S, W = qkvn.shape
    D = W // 3
    H = D // HEAD_DIM
    C = min(CHUNK, S)
    N = S // C

    def body(x_ref, gp_ref, gr_ref, bp_ref, sall_ref, do_ref, dx_ref, dgp_ref, dgr_ref, dbp_ref, dst):
        n = pl.program_id(0)

        @pl.when(n == 0)
        def _():
            dst[...] = jnp.zeros_like(dst)

        low = _tri(C)
        gc_cols = _tri_dot(low, gp_ref[...], "nn", 0)
        gc_rows = _tri_dot(gr_ref[...], low, "nt", 1)
        bcols = bp_ref[...]
        lane = lax.broadcasted_iota(jnp.int32, (1, LANES), 1)
        row8 = lax.broadcasted_iota(jnp.int32, (8, 1), 0)
        dgc_cols = jnp.zeros((C, LANES), F32)
        dgc_rows = jnp.zeros((8, C), F32)
        dbcols = jnp.zeros((C, LANES), F32)
        hs = range(H)
        _, vjp = jax.vjp(
            _chunk_heads,
            tuple(x_ref[:, h * HEAD_DIM : (h + 1) * HEAD_DIM] for h in hs),
            tuple(x_ref[:, D + h * HEAD_DIM : D + (h + 1) * HEAD_DIM] for h in hs),
            tuple(x_ref[:, 2 * D + h * HEAD_DIM : 2 * D + (h + 1) * HEAD_DIM] for h in hs),
            tuple(_pick_lane(gc_cols, h) for h in hs), tuple(_pick_row(gc_rows, h) for h in hs),
            tuple(_pick_lane(bcols, h) for h in hs), tuple(sall_ref[h] for h in hs),
        )
        dq, dk, dv, dgc, dgr, dbc, ds0 = vjp((tuple(do_ref[:, h * HEAD_DIM : (h + 1) * HEAD_DIM] for h in hs), tuple(dst[h] for h in hs)))
        for h in hs:
            dst[h] = ds0[h]
            dx_ref[:, h * HEAD_DIM : (h + 1) * HEAD_DIM] = dq[h]
            dx_ref[:, D + h * HEAD_DIM : D + (h + 1) * HEAD_DIM] = dk[h]
            dx_ref[:, 2 * D + h * HEAD_DIM : 2 * D + (h + 1) * HEAD_DIM] = dv[h]
            dgc_cols = dgc_cols + dgc[h] * (lane == h).astype(F32)
            dgc_rows = dgc_rows + dgr[h] * (row8 == h).astype(F32)
            dbcols = dbcols + dbc[h] * (lane == h).astype(F32)
        dgp_ref[...] = _tri_dot(low, dgc_cols, "tn", 0)
        dgr_ref[...] = _tri_dot(dgc_rows, low, "nn", 1)
        dbp_ref[...] = dbcols

    rev = lambda n: N - 1 - n
    return pl.pallas_call(
        body,
        name="gdn_delta_bwd",
        grid=(N,),
        in_specs=[
            pl.BlockSpec((C, W), lambda n: (rev(n), 0)),
            pl.BlockSpec((C, LANES), lambda n: (rev(n), 0)),
            pl.BlockSpec((None, 8, C), lambda n: (rev(n), 0, 0)),
            pl.BlockSpec((C, LANES), lambda n: (rev(n), 0)),
            pl.BlockSpec((None, H, HEAD_DIM, HEAD_DIM), lambda n: (rev(n), 0, 0, 0)),
            pl.BlockSpec((C, D), lambda n: (rev(n), 0)),
        ],
        out_specs=[
            pl.BlockSpec((C, W), lambda n: (rev(n), 0)),
            pl.BlockSpec((C, LANES), lambda n: (rev(n), 0)),
            pl.BlockSpec((None, 8, C), lambda n: (rev(n), 0, 0)),
            pl.BlockSpec((C, LANES), lambda n: (rev(n), 0)),
        ],
        out_shape=[
            jax.ShapeDtypeStruct((S, W), F32),
            jax.ShapeDtypeStruct((S, LANES), F32),
            jax.ShapeDtypeStruct((N, 8, C), F32),
            jax.ShapeDtypeStruct((S, LANES), F32),
        ],
        scratch_shapes=[pltpu.VMEM((H, HEAD_DIM, HEAD_DIM), F32)],
        compiler_params=_params(1),
    )(qkvn, g_pad, g_rows, beta_pad, s_all, do)


def _gate_norm_head(o, z, nw):
    return o * lax.rsqrt(jnp.mean(o * o, axis=-1, keepdims=True) + RMS_EPS) * nw * _silu(z)


def _gate_norm_fwd(o, z, nw):
    S, D = o.shape
    H = D // HEAD_DIM
    ts = min(ROW_TILE, S)

    def fn(i, n, rv, hv, cv):
        ov, zv = rv
        parts = [_gate_norm_head(ov[:, h * HEAD_DIM : (h + 1) * HEAD_DIM], zv[:, h * HEAD_DIM : (h + 1) * HEAD_DIM], cv[0]) for h in range(H)]
        return (jnp.concatenate(parts, axis=1),), ()

    return _rowwise("gdn_gate_norm_fwd", fn, S, ts, rows=[o, z], consts=[nw], outs=[(D, BF16)])[0]


def _gate_norm_bwd(dog, o, z, nw):
    S, D = o.shape
    H = D // HEAD_DIM
    ts = min(ROW_TILE, S)

    def fn(i, n, rv, hv, cv):
        dv, ov, zv = rv
        dos, dzs, dnw = [], [], None
        for h in range(H):
            sl = slice(h * HEAD_DIM, (h + 1) * HEAD_DIM)
            _, vjp = jax.vjp(_gate_norm_head, ov[:, sl], zv[:, sl], cv[0])
            a, b_, c_ = vjp(dv[:, sl])
            dos.append(a)
            dzs.append(b_)
            dnw = c_ if dnw is None else dnw + c_
        return (jnp.concatenate(dos, axis=1), jnp.concatenate(dzs, axis=1)), (dnw,)

    return _rowwise("gdn_gate_norm_bwd", fn, S, ts, rows=[dog, o, z], consts=[nw], outs=[(D, F32), (D, BF16)], accs=[(1, HEAD_DIM)])


def _square_bf16(r):
    rf = r.astype(F32)
    return rf * rf


def _mlp_ple_dw(li, dh, dhb, xa, p, r, gate, pp, w2):
    dpre = _mm(f"l{li}_mlp_down_bwd", dhb, w2, "nt", [BF16], epi=lambda acc, rr: (acc * (2.0 * rr.astype(F32)),), extras=[(r, "tile")], tm=1024, tn=1024, b_outer=True)
    dw2 = _mm(f"l{li}_mlp_dw2", r, dhb, "tn", [F32], a_fn=_square_bf16, **DW_TILES)
    dgate, dpp, dbg = _ple_bwd(f"l{li}_ple_bwd", dh, gate, pp)
    dw1 = _mm(f"l{li}_mlp_dw1", xa, dpre, "tn", [F32], **DW_TILES)
    dwg = _mm(f"l{li}_ple_dwg", xa, dgate, "tn", [F32], **DW_TILES)
    dwp = _mm(f"l{li}_ple_dwp", p, dpp, "tn", [F32], **DW_TILES)
    return dpre, dgate, dw1, dw2, dwg, dbg, dwp


def _mlp_ple_dx(li, dh, dpre, dgate, w1, wg, after):
    t = _mm(f"l{li}_ple_gate_bwd", dgate, wg, "nt", [F32], epi=lambda acc, d: (acc + ALPHA * d,), extras=[(dh, "tile")], tm=1024, after=after)
    return _mm(f"l{li}_mlp_up_bwd", dpre, w1, "nt", [F32], epi=lambda acc, d: (acc + d,), extras=[(t, "tile")], tm=256, tk=4096)


def _local_step(x, p, tgt, W, fetch, emit):
    S, D = x.shape
    H = D // HEAD_DIM
    C = min(CHUNK, S)
    N = S // C
    lg = lambda i, j: W["ln_gain"][2 * i + j][None, :]
    lb = lambda i, j: W["ln_bias"][2 * i + j][None, :]
    G = {}

    pooled, xh0a, rs0a, x0a = _pool_fwd(x, W["pool_w"], W["pool_b"], W["pool_scale"], lg(0, 0), lb(0, 0))
    w0a = fetch("l0a", x0a)
    r0 = _mm("l0_mlp_up", x0a, w0a["mlp_w1"], "nn", [BF16], epi=lambda acc: (jnp.maximum(acc, 0.0),), tm=1024, tn=1024, b_outer=True)
    w0b = fetch("l0b", r0)
    gate0 = _mm("l0_ple_gate", x0a, w0b["ple_gate_w"], "nn", [F32], epi=lambda acc, bias: (acc + bias,), extras=[(W["ple_gate_b"][0:1], "row")], tm=1024)
    pp0 = _mm("l0_ple_proj", p[0], w0b["ple_proj"], "nn", [F32])
    w0b.update(fetch("l0c", pp0))
    ff0 = _mm("l0_mlp_down", r0, w0b["mlp_w2"], "nn", [F32], a_fn=_square_bf16, tm=256, tk=4096)
    xh0b, rs0b, x0b = _res_ln_ffpe("l0_ln_b", xh0a, ff0, gate0, pp0, lg(0, 0), lb(0, 0), lg(0, 1), lb(0, 1))

    wg_ = fetch("gdn", x0b)
    qkv_pre = _mm("gdn_in_qkv", x0b, wg_["gdn_wqkv"], "nn", [F32], tm=1024, tn=1024, b_outer=True)
    z = _mm("gdn_in_z", x0b, wg_["gdn_wz"], "nn", [F32], tm=1024)
    ba = _mm("gdn_in_ba", x0b, wg_["gdn_wba"], "nn", [F32])
    qkvn = _conv_fwd(qkv_pre, W["gdn_conv"])
    beta_pad, g_pad = _gates_fwd(ba, W["gdn_a_log"], W["gdn_dt_bias"])
    g_rows = g_pad[:, :8].reshape(N, C, 8).transpose(0, 2, 1)
    o, s_all = _delta_fwd(qkvn, g_pad, g_rows, beta_pad)
    og = _gate_norm_fwd(o, z, W["gdn_norm_w"])
    mix1 = _mm("gdn_out", og, wg_["gdn_w_out"], "nn", [F32], tm=1024)
    xh1a, rs1a, x1a = _res_ln_mix("l1_ln_a", xh0b, mix1, lg(0, 1), lb(0, 1), lg(1, 0), lb(1, 0))
    w1_ = fetch("l1", x1a)
    r1 = _mm("l1_mlp_up", x1a, w1_["mlp_w1"], "nn", [BF16], epi=lambda acc: (jnp.maximum(acc, 0.0),), tm=1024, tn=1024, b_outer=True)
    ff1 = _mm("l1_mlp_down", r1, w1_["mlp_w2"], "nn", [F32], a_fn=_square_bf16, tm=256, tk=4096)
    gate1 = _mm("l1_ple_gate", x1a, w1_["ple_gate_w"], "nn", [F32], epi=lambda acc, bias: (acc + bias,), extras=[(W["ple_gate_b"][1:2], "row")], tm=1024)
    pp1 = _mm("l1_ple_proj", p[1], w1_["ple_proj"], "nn", [F32])
    dh1b, dh1b_b, loss_cols, dg11, db11 = _final_ln_loss(xh1a, ff1, gate1, pp1, tgt, lg(1, 0), lb(1, 0), lg(1, 1), lb(1, 1))

    dpre1, dgate1, dw1_1, dw2_1, dwg_1, dbg_1, dwp_1 = _mlp_ple_dw(1, dh1b, dh1b_b, x1a, p[1], r1, gate1, pp1, w1_["mlp_w2"])
    tok = emit("l1", {"mlp_w1": dw1_1, "mlp_w2": dw2_1, "ple_gate_w": dwg_1, "ple_proj": dwp_1})
    dx1a = _mlp_ple_dx(1, dh1b, dpre1, dgate1, w1_["mlp_w1"], w1_["ple_gate_w"], [tok])
    dh1a, dh1a_b, dg10, db10 = _ln_bwd_call("l1_ln_a_bwd", dx1a, xh1a, rs1a, lg(1, 0))
    dog = _mm("gdn_out_bwd", dh1a_b, wg_["gdn_w_out"], "nt", [F32], tm=1024)
    dw_out = _mm("gdn_dw_out", og, dh1a_b, "tn", [F32], **DW_TILES)
    do, dz, dnw = _gate_norm_bwd(dog, o, z, W["gdn_norm_w"])
    dqkvn, dg_col, dg_row, dbeta = _delta_bwd(qkvn, g_pad, g_rows, beta_pad, s_all, do)
    dg_all = dg_col + jnp.pad(dg_row.transpose(0, 2, 1).reshape(S, 8), ((0, 0), (0, LANES - 8)))
    dba, dalog, ddt = _gates_bwd(ba, W["gdn_a_log"], W["gdn_dt_bias"], dbeta, dg_all, H)
    dqkv, dconv = _conv_bwd(qkv_pre, W["gdn_conv"], dqkvn)
    dwqkv = _mm("gdn_dwqkv", x0b, dqkv, "tn", [F32], **DW_TILES)
    dwz = _mm("gdn_dwz", x0b, dz, "tn", [F32], **DW_TILES)
    dwba = _mm("gdn_dwba", x0b, dba, "tn", [F32], **DW_TILES)
    dw_in = jnp.concatenate([dwqkv, dwz, dwba[:, :H], dwba[:, LANES : LANES + H]], axis=1)
    tok = emit("gdn", {"gdn_w_in": dw_in, "gdn_w_out": dw_out})
    t = _mm("gdn_in_ba_bwd", dba, wg_["gdn_wba"], "nt", [F32], epi=lambda acc, d: (acc + ALPHA * d,), extras=[(dh1a, "tile")], after=[tok])
    t = _mm("gdn_in_z_bwd", dz, wg_["gdn_wz"], "nt", [F32], epi=lambda acc, d: (acc + d,), extras=[(t, "tile")], tm=1024)
    dx0b = _mm("gdn_in_qkv_bwd", dqkv, wg_["gdn_wqkv"], "nt", [F32], epi=lambda acc, d: (acc + d,), extras=[(t, "tile")], tm=256, tk=3072)

    dh0b, dh0b_b, dg01, db01 = _ln_bwd_call("l0_ln_b_bwd", dx0b, xh0b, rs0b, lg(0, 1))
    dpre0, dgate0, dw1_0, dw2_0, dwg_0, dbg_0, dwp_0 = _mlp_ple_dw(0, dh0b, dh0b_b, x0a, p[0], r0, gate0, pp0, w0b["mlp_w2"])
    tok = emit("l0", {"mlp_w1": dw1_0, "mlp_w2": dw2_0, "ple_gate_w": dwg_0, "ple_proj": dwp_0})
    dx0a = _mlp_ple_dx(0, dh0b, dpre0, dgate0, w0a["mlp_w1"], w0b["ple_gate_w"], [tok])
    dh0a, _, dg00, db00 = _ln_bwd_call("l0_ln_a_bwd", dx0a, xh0a, rs0a, lg(0, 0))
    grad_x, dyp, dscale, dpb = _pool_bwd(dh0a, pooled, W["pool_w"], W["pool_b"], W["pool_scale"])
    G["pool_w"] = _pool_dw(pooled, dyp)

    G["ln_gain"] = jnp.concatenate([dg00, dg01, dg10, dg11], axis=0)
    G["ln_bias"] = jnp.concatenate([db00, db01, db10, db11], axis=0)
    G["pool_b"] = dpb
    G["pool_scale"] = dscale
    G["gdn_conv"] = dconv
    G["gdn_a_log"] = dalog[:, :H]
    G["gdn_dt_bias"] = ddt[:, :H]
    G["gdn_norm_w"] = dnw
    G["ple_gate_b"] = jnp.concatenate([dbg_0, dbg_1], axis=0)
    return loss_cols, grad_x, G


_HBM = pl.BlockSpec(memory_space=pltpu.HBM)


def _all_gather(name, shards):
    T = len(shards)

    def body(*refs):
        ins, outs = refs[:T], refs[T : 2 * T]
        send_sems, recv_sems, local_sems = refs[2 * T :]
        x, y, c = lax.axis_index("x"), lax.axis_index("y"), lax.axis_index("c")
        me, sibling = (x, y, c), (x, y, 1 - c)
        chips = [(1 - x, y), (x, 1 - y), (1 - x, 1 - y)]

        def blk(t, px, py, pc):
            return outs[t].at[4 * px + 2 * py + pc]

        def copy(t, k, block, to, src=None):
            return pltpu.make_async_remote_copy(
                src_ref=blk(t, *block) if src is None else src, dst_ref=blk(t, *block),
                send_sem=send_sems.at[t, k], recv_sem=recv_sems.at[t, k], device_id=to, device_id_type=MESH,
            )

        mine = [pltpu.make_async_copy(ins[t], blk(t, *me), local_sems.at[t]) for t in range(T)]
        for cp in mine:
            cp.start()
        first = []
        for t in range(T):
            first.append(copy(t, 0, me, sibling, src=ins[t]))
            first += [copy(t, 1 + j, me, (*chip, c), src=ins[t]) for j, chip in enumerate(chips)]
        for cp in first:
            cp.start()
        passed = []
        for j, chip in enumerate(chips):
            for t in range(T):
                copy(t, 1 + j, (*chip, c), me).wait_recv()
                fw = copy(t, 4 + j, (*chip, c), sibling)
                fw.start()
                passed.append(fw)
        for t in range(T):
            copy(t, 0, sibling, me).wait_recv()
            for j, chip in enumerate(chips):
                copy(t, 4 + j, (*chip, 1 - c), me).wait_recv()
        for cp in first + passed:
            cp.wait_send()
        for cp in mine:
            cp.wait()

    return pl.pallas_call(
        body,
        name=name,
        in_specs=[_HBM] * T,
        out_specs=[_HBM] * T,
        out_shape=[jax.ShapeDtypeStruct((N_DEV,) + s.shape, s.dtype) for s in shards],
        scratch_shapes=[pltpu.SemaphoreType.DMA((T, 7)), pltpu.SemaphoreType.DMA((T, 7)), pltpu.SemaphoreType.DMA((T,))],
    )(*shards)


def _exchange(name, blocks):
    def body(g_ref, o_ref, send_sems, recv_sems, local_sem):
        x, y, c = lax.axis_index("x"), lax.axis_index("y"), lax.axis_index("c")
        own = pltpu.make_async_copy(g_ref.at[4 * x + 2 * y + c], o_ref.at[N_DEV - 1], local_sem)
        own.start()
        copies = []
        for rel in range(1, N_DEV):
            px = 1 - x if rel & 4 else x
            py = 1 - y if rel & 2 else y
            pc = 1 - c if rel & 1 else c
            copies.append(
                pltpu.make_async_remote_copy(
                    src_ref=g_ref.at[4 * px + 2 * py + pc], dst_ref=o_ref.at[rel - 1],
                    send_sem=send_sems.at[rel - 1], recv_sem=recv_sems.at[rel - 1], device_id=(px, py, pc), device_id_type=MESH,
                )
            )
        for cp in copies:
            cp.start()
        for cp in copies:
            cp.wait_recv()
        for cp in copies:
            cp.wait_send()
        own.wait()

    return pl.pallas_call(
        body,
        name=name,
        in_specs=[_HBM],
        out_specs=_HBM,
        out_shape=jax.ShapeDtypeStruct(blocks.shape, blocks.dtype),
        scratch_shapes=[pltpu.SemaphoreType.DMA((N_DEV - 1,)), pltpu.SemaphoreType.DMA((N_DEV - 1,)), pltpu.SemaphoreType.DMA],
    )(blocks)


_SEM = pl.BlockSpec(memory_space=pltpu.SEMAPHORE)
_ANY = pl.BlockSpec(memory_space=pl.ANY)
_DATAFLOW = pltpu.SideEffectType.DATAFLOW_SIDE_EFFECTING
N_PEERS = N_DEV - 1


def _peer(rel, x, y, c):
    return (1 - x if rel & 4 else x, 1 - y if rel & 2 else y, 1 - c if rel & 1 else c)


def _send_start(name, src, land, gather, after):
    def body(src_ref, land_ref, after_ref, send_sems, recv_sems, src_thru, land_thru, token):
        x, y, c = lax.axis_index("x"), lax.axis_index("y"), lax.axis_index("c")
        for rel in range(1, N_DEV):
            px, py, pc = _peer(rel, x, y, c)
            pltpu.make_async_remote_copy(
                src_ref=src_ref if gather else src_ref.at[4 * px + 2 * py + pc],
                dst_ref=land_ref.at[4 * x + 2 * y + c] if gather else land_ref.at[rel - 1],
                send_sem=send_sems.at[rel - 1], recv_sem=recv_sems.at[rel - 1], device_id=(px, py, pc), device_id_type=MESH,
            ).start()
        token[...] = jnp.zeros_like(token)

    return pl.pallas_call(
        body,
        name=name,
        out_shape=(pltpu.SemaphoreType.DMA((N_PEERS,)), pltpu.SemaphoreType.DMA((N_PEERS,)), pltpu.HBM(src.shape, src.dtype),
                   pltpu.HBM(land.shape, land.dtype), jax.ShapeDtypeStruct((8, LANES), F32)),
        in_specs=(_HBM, _HBM, _ANY),
        out_specs=(_SEM, _SEM, _HBM, _HBM, pl.BlockSpec(memory_space=pltpu.VMEM)),
        input_output_aliases={0: 2, 1: 3},
        compiler_params=pltpu.CompilerParams(has_side_effects=_DATAFLOW),
    )(pltpu.with_memory_space_constraint(src, pltpu.HBM), pltpu.with_memory_space_constraint(land, pltpu.HBM), after)


def _send_wait(name, started, after, gather):
    send_sems, recv_sems, src_thru, land_thru, _ = started

    def body(src_ref, land_ref, send_sems, recv_sems, after_ref, src_dead, got_ref):
        x, y, c = lax.axis_index("x"), lax.axis_index("y"), lax.axis_index("c")
        for rel in range(1, N_DEV):
            cp = pltpu.make_async_remote_copy(
                src_ref=src_ref if gather else src_ref.at[0], dst_ref=land_ref.at[0],
                send_sem=send_sems.at[rel - 1], recv_sem=recv_sems.at[rel - 1], device_id=_peer(rel, x, y, c), device_id_type=MESH,
            )
            cp.wait_send()
            cp.wait_recv()

    return pl.pallas_call(
        body,
        name=name,
        out_shape=(pltpu.HBM(src_thru.shape, src_thru.dtype), pltpu.HBM(land_thru.shape, land_thru.dtype)),
        in_specs=(_HBM, _HBM, _SEM, _SEM, _ANY),
        out_specs=(_HBM, _HBM),
        input_output_aliases={0: 0, 1: 1},
        compiler_params=pltpu.CompilerParams(has_side_effects=_DATAFLOW),
    )(src_thru, land_thru, send_sems, recv_sems, after)


def _sum_blocks(name, parts, tr):
    n_parts, R, Cw = parts.shape
    tr = tr if R % tr == 0 else R

    def body(p_ref, o_ref):
        acc = p_ref[0].astype(F32)
        for d in range(1, n_parts):
            acc = acc + p_ref[d].astype(F32)
        o_ref[...] = acc

    return pl.pallas_call(
        body,
        name=name,
        grid=(R // tr,),
        in_specs=[pl.BlockSpec((n_parts, tr, Cw), lambda i: (0, i, 0))],
        out_specs=pl.BlockSpec((tr, Cw), lambda i: (i, 0)),
        out_shape=jax.ShapeDtypeStruct((R, Cw), F32),
        compiler_params=_params(1),
    )(parts)


def _adamw(name, w, g, m, v):
    shape = w.shape
    cols = shape[-1]
    rows = w.size // cols
    tr = rows if rows <= 512 else 512
    assert rows % tr == 0
    w2, g2, m2, v2 = (a.reshape(rows, cols) for a in (w, g, m, v))

    def body(w_ref, g_ref, m_ref, v_ref, d_ref, mo_ref, vo_ref):
        gv = g_ref[...]
        mn = ADAM_B1 * m_ref[...] + (1.0 - ADAM_B1) * gv
        vn = ADAM_B2 * v_ref[...] + (1.0 - ADAM_B2) * jnp.square(gv)
        m_hat = mn / (1.0 - ADAM_B1**ADAM_STEP)
        v_hat = vn / (1.0 - ADAM_B2**ADAM_STEP)
        d_ref[...] = -ADAM_LR * (m_hat / (jnp.sqrt(v_hat) + ADAM_EPS) + ADAM_WD * w_ref[...])
        mo_ref[...] = mn
        vo_ref[...] = vn

    spec = pl.BlockSpec((tr, cols), lambda i: (i, 0))
    d, mn, vn = pl.pallas_call(
        body,
        name=name,
        grid=(rows // tr,),
        in_specs=[spec] * 4,
        out_specs=[spec] * 3,
        out_shape=[jax.ShapeDtypeStruct((rows, cols), F32)] * 3,
        compiler_params=_params(1),
    )(w2, g2, m2, v2)
    return d.reshape(shape), mn.reshape(shape), vn.reshape(shape)


SMALL_SHARDED = ("ln_gain", "ln_bias", "pool_b", "gdn_conv")
SMALL_REPLICATED = ("pool_scale", "gdn_a_log", "gdn_dt_bias", "gdn_norm_w", "ple_gate_b")
WEIGHTS = ("ln_gain", "ln_bias", "pool_w", "pool_b", "pool_scale", "gdn_w_in", "gdn_conv", "gdn_a_log", "gdn_dt_bias",
           "gdn_norm_w", "gdn_w_out", "mlp_w1", "mlp_w2", "ple_gate_w", "ple_gate_b", "ple_proj")
BIG_AXIS = {"gdn_w_in": 1, "gdn_w_out": 0, "mlp_w1": 1, "mlp_w2": 0, "ple_gate_w": 0, "ple_proj": 1, "pool_w": 1}
GATHER_GROUPS = {
    "l0a": (("mlp_w1", 0),),
    "l0b": (("ple_gate_w", 0), ("ple_proj", 0)),
    "l0c": (("mlp_w2", 0),),
    "gdn": (("gdn_w_in", 0), ("gdn_w_out", 0)),
    "l1": (("mlp_w1", 1), ("mlp_w2", 1), ("ple_gate_w", 1), ("ple_proj", 1)),
}
GRAD_GROUPS = {
    "l1": (("mlp_w1", 1), ("mlp_w2", 1), ("ple_gate_w", 1), ("ple_proj", 1)),
    "gdn": (("gdn_w_in", 0), ("gdn_w_out", 0)),
    "l0": (("mlp_w1", 0), ("mlp_w2", 0), ("ple_gate_w", 0), ("ple_proj", 0)),
}
PACK_PART_ALIGN = 16
SUM_TILE = 128


def _part_rows(a, width):
    rows = a.size // width
    return rows + (-rows) % PACK_PART_ALIGN


def _pack_rows(parts, width, dtype, align):
    padded = []
    for a in parts:
        a2 = a.reshape(-1, width).astype(dtype)
        padded.append(jnp.pad(a2, ((0, _part_rows(a, width) - a2.shape[0]), (0, 0))))
    flat = jnp.concatenate(padded, axis=0)
    return jnp.pad(flat, ((0, (-flat.shape[0]) % align), (0, 0)))


def _pack_blocks(parts, width, dtype, align):
    padded = []
    for a in parts:
        a2 = a.reshape(a.shape[0], -1, width).astype(dtype)
        padded.append(jnp.pad(a2, ((0, 0), (0, _part_rows(a[0], width) - a2.shape[1]), (0, 0))))
    flat = jnp.concatenate(padded, axis=1)
    return jnp.pad(flat, ((0, 0), (0, (-flat.shape[1]) % align), (0, 0)))


def _unpack_rows(packed, shapes, width):
    out, off = [], 0
    for shp in shapes:
        size = 1
        for d in shp:
            size *= d
        out.append(packed[..., off : off + size // width, :].reshape(packed.shape[:-2] + tuple(shp)))
        off += size // width + (-(size // width)) % PACK_PART_ALIGN
    return out


def _split_blocks(name, full):
    ax = BIG_AXIS[name]
    shp = full.shape
    a = full.reshape(shp[:ax] + (N_DEV, shp[ax] // N_DEV) + shp[ax + 1 :])
    return jnp.moveaxis(a, ax, 0)


def _join_blocks(name, blocks):
    ax = BIG_AXIS[name]
    a = jnp.moveaxis(blocks, 0, ax)
    shp = a.shape
    return a.reshape(shp[:ax] + (shp[ax] * shp[ax + 1],) + shp[ax + 2 :])


def _pack_small(parts):
    flat = jnp.concatenate([jnp.pad(a.reshape(-1), (0, (-a.size) % LANES)) for a in parts])
    rows = flat.size // LANES
    return jnp.pad(flat.reshape(rows, LANES), ((0, (-rows) % 8), (0, 0)))


def _unpack_small(packed, shapes):
    flat = packed.reshape(packed.shape[:-2] + (-1,))
    out, off = [], 0
    for shp in shapes:
        size = 1
        for s in shp:
            size *= s
        out.append(flat[..., off : off + size].reshape(flat.shape[:-1] + tuple(shp)))
        off += size + (-size) % LANES
    return out


def _split_w_in(w_in, D, H):
    pad = lambda a: jnp.pad(a, ((0, 0), (0, LANES - H)))
    return w_in[:, : 3 * D], w_in[:, 3 * D : 4 * D], jnp.concatenate([pad(w_in[:, 4 * D : 4 * D + H]), pad(w_in[:, 4 * D + H :])], axis=1)


def kernel(x, p, ln_gain, ln_bias, pool_w, pool_b, pool_scale, gdn_w_in, gdn_conv, gdn_a_log, gdn_dt_bias, gdn_norm_w, gdn_w_out, mlp_w1, mlp_w2, ple_gate_w, ple_gate_b, ple_proj, loss_target, m_ln_gain, m_ln_bias, m_pool_w, m_pool_b, m_pool_scale, m_gdn_w_in, m_gdn_conv, m_gdn_a_log, m_gdn_dt_bias, m_gdn_norm_w, m_gdn_w_out, m_mlp_w1, m_mlp_w2, m_ple_gate_w, m_ple_gate_b, m_ple_proj, v_ln_gain, v_ln_bias, v_pool_w, v_pool_b, v_pool_scale, v_gdn_w_in, v_gdn_conv, v_gdn_a_log, v_gdn_dt_bias, v_gdn_norm_w, v_gdn_w_out, v_mlp_w1, v_mlp_w2, v_ple_gate_w, v_ple_gate_b, v_ple_proj):
    w_sh = dict(ln_gain=ln_gain, ln_bias=ln_bias, pool_w=pool_w, pool_b=pool_b, pool_scale=pool_scale, gdn_w_in=gdn_w_in,
                gdn_conv=gdn_conv, gdn_a_log=gdn_a_log, gdn_dt_bias=gdn_dt_bias, gdn_norm_w=gdn_norm_w, gdn_w_out=gdn_w_out,
                mlp_w1=mlp_w1, mlp_w2=mlp_w2, ple_gate_w=ple_gate_w, ple_gate_b=ple_gate_b, ple_proj=ple_proj)
    m_sh = dict(ln_gain=m_ln_gain, ln_bias=m_ln_bias, pool_w=m_pool_w, pool_b=m_pool_b, pool_scale=m_pool_scale, gdn_w_in=m_gdn_w_in,
                gdn_conv=m_gdn_conv, gdn_a_log=m_gdn_a_log, gdn_dt_bias=m_gdn_dt_bias, gdn_norm_w=m_gdn_norm_w, gdn_w_out=m_gdn_w_out,
                mlp_w1=m_mlp_w1, mlp_w2=m_mlp_w2, ple_gate_w=m_ple_gate_w, ple_gate_b=m_ple_gate_b, ple_proj=m_ple_proj)
    v_sh = dict(ln_gain=v_ln_gain, ln_bias=v_ln_bias, pool_w=v_pool_w, pool_b=v_pool_b, pool_scale=v_pool_scale, gdn_w_in=v_gdn_w_in,
                gdn_conv=v_gdn_conv, gdn_a_log=v_gdn_a_log, gdn_dt_bias=v_gdn_dt_bias, gdn_norm_w=v_gdn_norm_w, gdn_w_out=v_gdn_w_out,
                mlp_w1=v_mlp_w1, mlp_w2=v_mlp_w2, ple_gate_w=v_ple_gate_w, ple_gate_b=v_ple_gate_b, ple_proj=v_ple_proj)
    xs, tg = x[0], loss_target[0]
    ps = p[:, 0]
    S, D = xs.shape
    H = D // HEAD_DIM
    me = 4 * lax.axis_index("x") + 2 * lax.axis_index("y") + lax.axis_index("c")
    layer = lambda n, l: (w_sh[n][0] if n in ("gdn_w_in", "gdn_w_out") else w_sh[n][l])

    pool_packed = _pack_rows([w_sh["pool_w"][0]], D, BF16, PACK_PART_ALIGN)
    small_packed = _pack_small([w_sh[n] for n in SMALL_SHARDED])
    pool_gathered, small_gathered = _all_gather("gather_first", [pool_packed, small_packed])
    W = {"pool_w": _join_blocks("pool_w", _unpack_rows(pool_gathered, [w_sh["pool_w"][0].shape], D)[0])}

    started, tokens = {}, jnp.zeros((1, 1), F32)
    for g, members in GATHER_GROUPS.items():
        src = _pack_rows([layer(n, l) for n, l in members], D, BF16, PACK_PART_ALIGN)
        started[g] = tuple(_send_start(f"gather_{g}_start", src, lax.empty((N_DEV,) + src.shape, BF16), True, small_gathered))
        tokens = tokens + started[g][4][0:1, 0:1]
    smalls = _unpack_small(small_gathered, [w_sh[n].shape for n in SMALL_SHARDED])
    for n, a in zip(SMALL_SHARDED, smalls):
        W[n] = jnp.moveaxis(a, 0, -2).reshape(a.shape[1:-1] + (N_DEV * a.shape[-1],))
    W["ln_gain"] = W["ln_gain"].reshape(2 * DEPTH, D)
    W["ln_bias"] = W["ln_bias"].reshape(2 * DEPTH, D)
    W["pool_b"] = W["pool_b"].reshape(1, D) + tokens
    W["gdn_conv"] = W["gdn_conv"][0]
    W["pool_scale"] = pool_scale
    W["ple_gate_b"] = ple_gate_b
    W["gdn_norm_w"] = gdn_norm_w
    W["gdn_a_log"] = jnp.pad(gdn_a_log, ((0, 0), (0, LANES - H)))
    W["gdn_dt_bias"] = jnp.pad(gdn_dt_bias, ((0, 0), (0, LANES - H)))

    def fetch(g, after):
        members = GATHER_GROUPS[g]
        src, land = _send_wait(f"gather_{g}_wait", started[g], after, True)
        land = lax.dynamic_update_index_in_dim(land, src, me, 0)
        parts = _unpack_rows(land, [layer(n, l).shape for n, l in members], D)
        out = {n: _join_blocks(n, a) for (n, _), a in zip(members, parts)}
        if "gdn_w_in" in out:
            out["gdn_wqkv"], out["gdn_wz"], out["gdn_wba"] = _split_w_in(out.pop("gdn_w_in"), D, H)
        return out

    sent = {}

    def emit(g, grads):
        members = GRAD_GROUPS[g]
        blocks = [_split_blocks(n, grads[n]) for n, _ in members]
        src = _pack_blocks(blocks, D, BF16, SUM_TILE)
        sent[g] = tuple(_send_start(f"grads_{g}_start", src, lax.empty((N_PEERS,) + src.shape[1:], BF16), False, blocks[0]))
        return sent[g][4]

    loss_cols, grad_x, G = _local_step(xs, ps, tg, W, fetch, emit)
    loss = lax.psum(0.5 * jnp.sum(loss_cols) / D, MESH_AXES)

    pool_src = _pack_blocks([_split_blocks("pool_w", G["pool_w"])], D, BF16, PACK_PART_ALIGN)
    pool_sum = _sum_blocks("sum_pool_grads", _exchange("exchange_pool_grads", pool_src), SUM_TILE)
    grads = {"pool_w": _unpack_rows(pool_sum, [w_sh["pool_w"][0].shape], D)[0].reshape(w_sh["pool_w"].shape)}
    small_names = SMALL_SHARDED + SMALL_REPLICATED
    gs_packed = _pack_small([G[n] for n in small_names])
    (gs_all,) = _all_gather("gather_small_grads", [gs_packed])
    gs_sum = _sum_blocks("sum_small_grads", gs_all, SUM_TILE)
    for n, a in zip(small_names, _unpack_small(gs_sum, [G[n].shape for n in small_names])):
        if n in SMALL_SHARDED:
            width = w_sh[n].shape[-1]
            a = a.reshape(w_sh[n].shape[:-1] + (N_DEV * width,))
            a = lax.dynamic_slice_in_dim(a, me * width, width, axis=a.ndim - 1)
        grads[n] = a.reshape(w_sh[n].shape)

    per_layer = {}
    for g, members in GRAD_GROUPS.items():
        src, land = _send_wait(f"grads_{g}_wait", sent[g], grad_x, False)
        own = lax.dynamic_index_in_dim(src, me, 0, keepdims=True)
        total = _sum_blocks(f"sum_grads_{g}", jnp.concatenate([land, own], axis=0), SUM_TILE)
        for (n, l), a in zip(members, _unpack_rows(total, [layer(n, l).shape for n, l in members], D)):
            per_layer[(n, l)] = a
    for n in ("gdn_w_in", "gdn_w_out"):
        grads[n] = per_layer[(n, 0)][None]
    for n in ("mlp_w1", "mlp_w2", "ple_gate_w", "ple_proj"):
        grads[n] = jnp.stack([per_layer[(n, 0)], per_layer[(n, 1)]])

    deltas, new_m, new_v = {}, {}, {}
    for n in WEIGHTS:
        deltas[n], new_m[n], new_v[n] = _adamw(f"adamw_{n}", w_sh[n], grads[n], m_sh[n], v_sh[n])
    return (loss, grad_x[None], *[grads[n] for n in WEIGHTS], *[deltas[n] for n in WEIGHTS],
            *[new_m[n] for n in WEIGHTS], *[new_v[n] for n in WEIGHTS])
```

```python
import functools

import jax
import jax.numpy as jnp
from jax import lax
from jax.experimental import pallas as pl
from jax.experimental.pallas import tpu as pltpu

F32 = jnp.float32
BF16 = jnp.bfloat16
MESH_AXES = ("x", "y", "c")
N_DEV = 8
MESH = pl.DeviceIdType.MESH

DEPTH = 2
ALPHA = (2.0 * DEPTH) ** 0.25
LN_EPS = 1e-5
RMS_EPS = 1e-6
L2_EPS = 1e-6
HEAD_DIM = 128
CONV_WIDTH = 4
POOL_WINDOWS = (2, 4, 8, 16)
POOL_HALO = 16
CONV_HALO = 8
LANES = 128
ADAM_LR = 0.001
ADAM_B1 = 0.9
ADAM_B2 = 0.999
ADAM_EPS = 1e-08
ADAM_WD = 0.01
ADAM_STEP = 10

VMEM_LIMIT = 56 * 1024 * 1024
ROW_TILE = 256
CONV_TILE = 256
CHUNK = 128
MM_TM, MM_TN, MM_TK = 512, 1024, 1024
DW_TILES = dict(tm=256, tn=512, tk=8192, b_outer=True)

_DIMS = {
    "nn": (((1,), (0,)), ((), ())),
    "nt": (((1,), (1,)), ((), ())),
    "tn": (((0,), (0,)), ((), ())),
}


def _params(n_axes):
    return pltpu.CompilerParams(dimension_semantics=("arbitrary",) * n_axes, vmem_limit_bytes=VMEM_LIMIT)


def _fit(tile, n):
    tile = min(tile, n)
    while n % tile:
        tile //= 2
    return tile


def _mm(name, a, b, mode, out_dtypes, epi=None, extras=(), a_fn=None, tm=None, tn=None, tk=None, b_outer=False, after=(), out_blocks=1):
    if mode == "tn":
        K, M = a.shape
    else:
        M, K = a.shape
    N = b.shape[0] if mode == "nt" else b.shape[1]
    tm, tn, tk = _fit(tm or MM_TM, M), (N // out_blocks if out_blocks > 1 else _fit(tn or MM_TN, N)), _fit(tk or MM_TK, K)
    nk = K // tk

    def at(f):
        return (lambda j, i, k: f(i, j, k)) if b_outer else f

    a_spec = pl.BlockSpec((tk, tm), at(lambda i, j, k: (k, i))) if mode == "tn" else pl.BlockSpec((tm, tk), at(lambda i, j, k: (i, k)))
    b_spec = pl.BlockSpec((tn, tk), at(lambda i, j, k: (j, k))) if mode == "nt" else pl.BlockSpec((tk, tn), at(lambda i, j, k: (k, j)))
    ex_specs = [
        pl.BlockSpec((tm, tn), at(lambda i, j, k: (i, j))) if kind == "tile" else pl.BlockSpec((1, tn), at(lambda i, j, k: (0, j)))
        for _, kind in extras
    ]
    n_ex, n_out, n_after = len(extras), len(out_dtypes), len(after)

    def body(*refs):
        a_ref, b_ref = refs[0], refs[1]
        ex_refs = refs[2 : 2 + n_ex]
        out_refs = refs[2 + n_ex + n_after : 2 + n_ex + n_after + n_out]
        av = a_ref[...]
        if a_fn is not None:
            av = a_fn(av)
        part = lax.dot_general(av.astype(BF16), b_ref[...].astype(BF16), _DIMS[mode], preferred_element_type=F32)

        def finish(res):
            vals = epi(res, *[e[...] for e in ex_refs]) if epi is not None else (res,)
            for o_ref, v in zip(out_refs, vals):
                o_ref[...] = v.astype(o_ref.dtype)

        if nk == 1:
            finish(part)
        else:
            acc = refs[-1]
            k = pl.program_id(2)

            @pl.when(k == 0)
            def _():
                acc[...] = part

            @pl.when(k > 0)
            def _():
                acc[...] += part

            @pl.when(k == nk - 1)
            def _():
                finish(acc[...])

    outs = pl.pallas_call(
        body,
        name=name,
        grid=(N // tn, M // tm, nk) if b_outer else (M // tm, N // tn, nk),
        in_specs=[a_spec, b_spec] + ex_specs + [pl.BlockSpec(memory_space=pl.ANY)] * n_after,
        out_specs=[pl.BlockSpec((tm, tn), at(lambda i, j, k: (i, j))) if out_blocks == 1 else pl.BlockSpec((None, tm, tn), at(lambda i, j, k: (j, i, 0)))
                   for _ in out_dtypes],
        out_shape=[jax.ShapeDtypeStruct((M, N) if out_blocks == 1 else (out_blocks, M, tn), dt) for dt in out_dtypes],
        scratch_shapes=[pltpu.VMEM((tm, tn), F32)] if nk > 1 else [],
        compiler_params=_params(3),
    )(a, b, *[e for e, _ in extras], *after)
    return outs[0] if n_out == 1 else outs


def _rowwise(name, fn, S, ts, rows=(), halos=(), consts=(), outs=(), accs=()):
    ts = min(ts, S)
    assert S % ts == 0
    n = S // ts
    in_specs = [pl.BlockSpec((ts, a.shape[1]), lambda i: (i, 0)) for a in rows]
    for a, kind, hr in halos:
        r, nb = ts // hr, S // hr
        if kind == "prev":
            in_specs.append(pl.BlockSpec((hr, a.shape[1]), lambda i, r=r: (jnp.maximum(i * r - 1, 0), 0)))
        else:
            in_specs.append(pl.BlockSpec((hr, a.shape[1]), lambda i, r=r, nb=nb: (jnp.minimum((i + 1) * r, nb - 1), 0)))
    in_specs += [pl.BlockSpec(a.shape, lambda i, nd=a.ndim: (0,) * nd) for a in consts]
    out_specs = [pl.BlockSpec((ts, w), lambda i: (i, 0)) for w, _ in outs]
    out_specs += [pl.BlockSpec((r, w), lambda i: (0, 0)) for r, w in accs]
    out_shape = [jax.ShapeDtypeStruct((S, w), dt) for w, dt in outs]
    out_shape += [jax.ShapeDtypeStruct((r, w), F32) for r, w in accs]
    nr, nh, nc, no = len(rows), len(halos), len(consts), len(outs)

    def body(*refs):
        i = pl.program_id(0)
        rv = [r[...] for r in refs[:nr]]
        hv = [r[...] for r in refs[nr : nr + nh]]
        cv = [r[...] for r in refs[nr + nh : nr + nh + nc]]
        o_refs = refs[nr + nh + nc : nr + nh + nc + no]
        a_refs = refs[nr + nh + nc + no :]
        ovals, avals = fn(i, n, rv, hv, cv)
        for o_ref, v in zip(o_refs, ovals):
            o_ref[...] = v.astype(o_ref.dtype)
        for a_ref, v in zip(a_refs, avals):

            @pl.when(i == 0)
            def _(a_ref=a_ref, v=v):
                a_ref[...] = v

            @pl.when(i > 0)
            def _(a_ref=a_ref, v=v):
                a_ref[...] += v

    res = pl.pallas_call(
        body,
        name=name,
        grid=(n,),
        in_specs=in_specs,
        out_specs=out_specs,
        out_shape=out_shape,
        compiler_params=_params(1),
    )(*rows, *[h[0] for h in halos], *consts)
    return list(res)


def _ln(h, g, b):
    mu = jnp.mean(h, axis=-1, keepdims=True)
    d = h - mu
    var = jnp.mean(d * d, axis=-1, keepdims=True)
    rstd = lax.rsqrt(var + LN_EPS)
    xhat = d * rstd
    return xhat, rstd, xhat * g + b


def _ln_bwd(dy, xhat, rstd, g):
    dxh = dy * g
    m1 = jnp.mean(dxh, axis=-1, keepdims=True)
    m2 = jnp.mean(dxh * xhat, axis=-1, keepdims=True)
    dh = rstd * (dxh - m1 - xhat * m2)
    return dh, jnp.sum(dy * xhat, axis=0, keepdims=True), jnp.sum(dy, axis=0, keepdims=True)


def _wide(col, ts):
    return jnp.broadcast_to(col, (ts, LANES))


def _pool_fwd(x, wp, pb, ps, g, b):
    S, D = x.shape
    gw = D // len(POOL_WINDOWS)
    ts = min(ROW_TILE, S)

    def fn(i, n, rv, hv, cv):
        (xc,), (xp,) = rv, hv
        wpv, pbv, psv, gv, bv = cv
        xp = jnp.where(i > 0, xp, 0.0)
        xx = jnp.concatenate([xp, xc], axis=0)
        t = i * ts + lax.broadcasted_iota(jnp.int32, (ts, 1), 0)
        pooled, ys = [], []
        for gi, w in enumerate(POOL_WINDOWS):
            s = xx[:, gi * gw : (gi + 1) * gw]
            k = 1
            while k < w:
                s = s + pltpu.roll(s, k, axis=0)
                k *= 2
            cnt = jnp.minimum(t + 1, w).astype(F32)
            pg = (s[POOL_HALO:, :] / cnt - xc[:, gi * gw : (gi + 1) * gw]).astype(BF16)
            pooled.append(pg)
            ys.append(jnp.dot(pg, wpv[gi], preferred_element_type=F32))
        y = jnp.concatenate(ys, axis=1)
        h = ALPHA * xc + (y + pbv) * psv
        xhat, rstd, xa = _ln(h, gv, bv)
        return (jnp.concatenate(pooled, axis=1), xhat, _wide(rstd, ts), xa), ()

    return _rowwise(
        "pool_fwd", fn, S, ts, rows=[x], halos=[(x, "prev", POOL_HALO)], consts=[wp, pb, ps, g, b],
        outs=[(D, BF16), (D, F32), (LANES, F32), (D, BF16)],
    )


def _pool_bwd(dh, pooled, wp, pb, ps):
    S, D = dh.shape
    gw = D // len(POOL_WINDOWS)
    ts = min(ROW_TILE, S)
    te = ts + POOL_HALO

    def fn(i, n, rv, hv, cv):
        (dhc, pc), (dhn,) = rv, hv
        wpv, pbv, psv = cv
        dhn = jnp.where(i < n - 1, dhn, 0.0)
        dy_ext = jnp.concatenate([dhc, dhn], axis=0) * psv
        dyb = dy_ext.astype(BF16)
        t = i * ts + lax.broadcasted_iota(jnp.int32, (te, 1), 0)
        dxs, ys = [], []
        for gi, w in enumerate(POOL_WINDOWS):
            sl = slice(gi * gw, (gi + 1) * gw)
            dp = lax.dot_general(dyb[:, sl], wpv[gi], _DIMS["nt"], preferred_element_type=F32)
            s = dp / jnp.minimum(t + 1, w).astype(F32)
            k = 1
            while k < w:
                s = s + pltpu.roll(s, k, axis=0)
                k *= 2
            s = pltpu.roll(s, POOL_HALO - (w - 1), axis=0)
            dxs.append(s[POOL_HALO:, :] - dp[:ts, :])
            ys.append(jnp.dot(pc[:, sl], wpv[gi], preferred_element_type=F32))
        dx = ALPHA * dhc + jnp.concatenate(dxs, axis=1)
        y = jnp.concatenate(ys, axis=1) + pbv
        dscale = jnp.sum(dhc * y, axis=0, keepdims=True)
        dbias = jnp.sum(dy_ext[:ts, :], axis=0, keepdims=True)
        return (dx, dyb[:ts, :]), (dscale, dbias)

    return _rowwise(
        "pool_bwd", fn, S, ts, rows=[dh, pooled], halos=[(dh, "next", POOL_HALO)], consts=[wp, pb, ps],
        outs=[(D, F32), (D, BF16)], accs=[(1, D), (1, D)],
    )


def _pool_dw(pooled, dy):
    S, D = pooled.shape
    G = len(POOL_WINDOWS)
    gw = D // G
    tk = min(MM_TK, S)
    nk = S // tk

    def body(p_ref, d_ref, o_ref):
        k = pl.program_id(1)
        part = lax.dot_general(p_ref[...], d_ref[...], _DIMS["tn"], preferred_element_type=F32)

        @pl.when(k == 0)
        def _():
            o_ref[...] = part

        @pl.when(k > 0)
        def _():
            o_ref[...] += part

    return pl.pallas_call(
        body,
        name="pool_dw",
        grid=(G, nk),
        in_specs=[pl.BlockSpec((tk, gw), lambda g, k: (k, g)), pl.BlockSpec((tk, gw), lambda g, k: (k, g))],
        out_specs=pl.BlockSpec((None, gw, gw), lambda g, k: (g, 0, 0)),
        out_shape=jax.ShapeDtypeStruct((G, gw, gw), F32),
        compiler_params=_params(2),
    )(pooled, dy)


def _res_ln_mix(name, xhat_p, mix, gp_, bp_, g, b):
    S, D = xhat_p.shape
    ts = min(ROW_TILE, S)

    def fn(i, n, rv, hv, cv):
        xh, m = rv
        gpv, bpv, gv, bv = cv
        xhat, rstd, xo = _ln(ALPHA * (xh * gpv + bpv) + m, gv, bv)
        return (xhat, _wide(rstd, ts), xo), ()

    return _rowwise(name, fn, S, ts, rows=[xhat_p, mix], consts=[gp_, bp_, g, b], outs=[(D, F32), (LANES, F32), (D, BF16)])


def _res_ln_ffpe(name, xhat_p, ff, gate, pp, gp_, bp_, g, b):
    S, D = xhat_p.shape
    ts = min(ROW_TILE, S)

    def fn(i, n, rv, hv, cv):
        xh, f, gt, p_ = rv
        gpv, bpv, gv, bv = cv
        xhat, rstd, xo = _ln(ALPHA * (xh * gpv + bpv) + f + jax.nn.sigmoid(gt) * p_, gv, bv)
        return (xhat, _wide(rstd, ts), xo), ()

    return _rowwise(name, fn, S, ts, rows=[xhat_p, ff, gate, pp], consts=[gp_, bp_, g, b], outs=[(D, F32), (LANES, F32), (D, BF16)])


def _final_ln_loss(xhat_p, ff, gate, pp, tgt, gp_, bp_, g, b):
    S, D = xhat_p.shape
    ts = min(ROW_TILE, S)

    def fn(i, n, rv, hv, cv):
        xh, f, gt, p_, tg = rv
        gpv, bpv, gv, bv = cv
        xhat, rstd, y = _ln(ALPHA * (xh * gpv + bpv) + f + jax.nn.sigmoid(gt) * p_, gv, bv)
        e = y - tg
        dh, dg, db = _ln_bwd(e * (1.0 / D), xhat, rstd, gv)
        return (dh, dh), (jnp.sum(e * e, axis=0, keepdims=True), dg, db)

    return _rowwise(
        "final_ln_loss", fn, S, ts, rows=[xhat_p, ff, gate, pp, tgt], consts=[gp_, bp_, g, b],
        outs=[(D, F32), (D, BF16)], accs=[(1, D), (1, D), (1, D)],
    )


def _ln_bwd_call(name, dy, xhat, rstd, g):
    S, D = dy.shape
    ts = min(ROW_TILE, S)

    def fn(i, n, rv, hv, cv):
        dyv, xh, rs = rv
        dh, dg, db = _ln_bwd(dyv, xh, rs[:, :1], cv[0])
        return (dh, dh), (dg, db)

    return _rowwise(name, fn, S, ts, rows=[dy, xhat, rstd], consts=[g], outs=[(D, F32), (D, BF16)], accs=[(1, D), (1, D)])


def _ple_bwd(name, dh, gate, pp):
    S, D = dh.shape
    ts = min(ROW_TILE, S)

    def fn(i, n, rv, hv, cv):
        d, gt, p_ = rv
        sg = jax.nn.sigmoid(gt)
        dgt = d * p_ * sg * (1.0 - sg)
        return (dgt, d * sg), (jnp.sum(dgt, axis=0, keepdims=True),)

    return _rowwise(name, fn, S, ts, rows=[dh, gate, pp], outs=[(D, BF16), (D, BF16)], accs=[(1, D)])


def _silu(c):
    return c * jax.nn.sigmoid(c)


def _qkv_point(c, is_qk, scale):
    s = _silu(c)
    nrm = s * lax.rsqrt(jnp.sum(s * s, axis=-1, keepdims=True) + L2_EPS) * scale
    return jnp.where(is_qk, nrm, s)


def _conv_rows(xx, wv, lo, rows):
    acc = None
    for j in range(CONV_WIDTH):
        sh = CONV_WIDTH - 1 - j
        term = (pltpu.roll(xx, sh, axis=0) if sh else xx)[lo : lo + rows, :] * wv[j : j + 1, :]
        acc = term if acc is None else acc + term
    return acc


def _conv_fwd(qkv_pre, conv_w):
    S, W = qkv_pre.shape
    D = W // 3
    H = D // HEAD_DIM
    ts = min(CONV_TILE, S)
    r = ts // CONV_HALO

    def body(x_ref, xp_ref, w_ref, o_ref):
        j, i = pl.program_id(0), pl.program_id(1)
        xp = jnp.where(i > 0, xp_ref[...], 0.0)
        xx = jnp.concatenate([xp, x_ref[...]], axis=0)
        c = _conv_rows(xx, w_ref[...], CONV_HALO, ts)
        scale = jnp.where(j == 0, HEAD_DIM**-0.5, 1.0).astype(F32)
        for h in range(H):
            sl = slice(h * HEAD_DIM, (h + 1) * HEAD_DIM)
            o_ref[:, sl] = _qkv_point(c[:, sl], j < 2, scale)

    return pl.pallas_call(
        body,
        name="gdn_conv_fwd",
        grid=(3, S // ts),
        in_specs=[
            pl.BlockSpec((ts, D), lambda j, i: (i, j)),
            pl.BlockSpec((CONV_HALO, D), lambda j, i: (jnp.maximum(i * r - 1, 0), j)),
            pl.BlockSpec((CONV_WIDTH, D), lambda j, i: (0, j)),
        ],
        out_specs=pl.BlockSpec((ts, D), lambda j, i: (i, j)),
        out_shape=jax.ShapeDtypeStruct((S, W), F32),
        compiler_params=_params(2),
    )(qkv_pre, qkv_pre, conv_w)


def _conv_bwd(qkv_pre, conv_w, dqkvn):
    S, W = qkv_pre.shape
    D = W // 3
    H = D // HEAD_DIM
    ts = min(CONV_TILE, S)
    r, nb = ts // CONV_HALO, S // CONV_HALO
    te = ts + CONV_HALO

    def body(x_ref, xp_ref, xn_ref, w_ref, d_ref, dn_ref, dx_ref, dw_ref):
        j, i = pl.program_id(0), pl.program_id(1)
        n = pl.num_programs(1)
        wv = w_ref[...]
        xp = jnp.where(i > 0, xp_ref[...], 0.0)
        xx = jnp.concatenate([xp, x_ref[...], xn_ref[...]], axis=0)
        c = _conv_rows(xx, wv, CONV_HALO, te)
        dn = jnp.where(i < n - 1, dn_ref[...], 0.0)
        dout = jnp.concatenate([d_ref[...], dn], axis=0)
        scale = jnp.where(j == 0, HEAD_DIM**-0.5, 1.0).astype(F32)
        dcs = []
        for h in range(H):
            sl = slice(h * HEAD_DIM, (h + 1) * HEAD_DIM)
            _, vjp = jax.vjp(lambda cc: _qkv_point(cc, j < 2, scale), c[:, sl])
            dcs.append(vjp(dout[:, sl])[0])
        dc = jnp.concatenate(dcs, axis=1)
        dx = None
        dws = []
        for jj in range(CONV_WIDTH):
            sh = CONV_WIDTH - 1 - jj
            term = pltpu.roll(dc, CONV_HALO - sh, axis=0)[CONV_HALO:, :] * wv[jj : jj + 1, :]
            dx = term if dx is None else dx + term
            xs = (pltpu.roll(xx, sh, axis=0) if sh else xx)[CONV_HALO : CONV_HALO + ts, :]
            dws.append(jnp.sum(dc[:ts, :] * xs, axis=0, keepdims=True))
        dx_ref[...] = dx.astype(dx_ref.dtype)
        dw = jnp.concatenate(dws, axis=0)

        @pl.when(i == 0)
        def _():
            dw_ref[...] = dw

        @pl.when(i > 0)
        def _():
            dw_ref[...] += dw

    return pl.pallas_call(
        body,
        name="gdn_conv_bwd",
        grid=(3, S // ts),
        in_specs=[
            pl.BlockSpec((ts, D), lambda j, i: (i, j)),
            pl.BlockSpec((CONV_HALO, D), lambda j, i: (jnp.maximum(i * r - 1, 0), j)),
            pl.BlockSpec((CONV_HALO, D), lambda j, i: (jnp.minimum((i + 1) * r, nb - 1), j)),
            pl.BlockSpec((CONV_WIDTH, D), lambda j, i: (0, j)),
            pl.BlockSpec((ts, D), lambda j, i: (i, j)),
            pl.BlockSpec((CONV_HALO, D), lambda j, i: (jnp.minimum((i + 1) * r, nb - 1), j)),
        ],
        out_specs=[pl.BlockSpec((ts, D), lambda j, i: (i, j)), pl.BlockSpec((CONV_WIDTH, D), lambda j, i: (0, j))],
        out_shape=[jax.ShapeDtypeStruct((S, W), BF16), jax.ShapeDtypeStruct((CONV_WIDTH, W), F32)],
        compiler_params=_params(2),
    )(qkv_pre, qkv_pre, qkv_pre, conv_w, dqkvn, dqkvn)


def _softplus(x):
    pos = x > 0.0
    return jnp.where(pos, x, 0.0) + jnp.log(1.0 + jnp.exp(jnp.where(pos, -x, x)))


def _gates(bl, al, alog, dt):
    return jax.nn.sigmoid(bl), -jnp.exp(alog) * _softplus(al + dt)


def _gates_fwd(ba, alog, dt):
    S = ba.shape[0]
    ts = min(ROW_TILE, S)

    def fn(i, n, rv, hv, cv):
        return _gates(rv[0][:, :LANES], rv[0][:, LANES:], cv[0], cv[1]), ()

    return _rowwise("gdn_gates_fwd", fn, S, ts, rows=[ba], consts=[alog, dt], outs=[(LANES, F32), (LANES, F32)])


def _gates_bwd(ba, alog, dt, dbeta, dg, H):
    S = ba.shape[0]
    ts = min(ROW_TILE, S)

    def fn(i, n, rv, hv, cv):
        bav, dbv, dgv = rv
        real = lax.broadcasted_iota(jnp.int32, (1, LANES), 1) < H
        _, vjp = jax.vjp(_gates, bav[:, :LANES], bav[:, LANES:], cv[0], cv[1])
        dbl, dal, dalog, ddt = vjp((jnp.where(real, dbv, 0.0), jnp.where(real, dgv, 0.0)))
        dbl, dal = jnp.where(real, dbl, 0.0), jnp.where(real, dal, 0.0)
        return (jnp.concatenate([dbl, dal], axis=1),), (jnp.where(real, dalog, 0.0), jnp.where(real, ddt, 0.0))

    return _rowwise(
        "gdn_gates_bwd", fn, S, ts, rows=[ba, dbeta, dg], consts=[alog, dt], outs=[(2 * LANES, BF16)],
        accs=[(1, LANES), (1, LANES)],
    )


def _split_bf16(a, n):
    parts, rest = [], a
    for _ in range(n):
        piece = rest.astype(BF16)
        parts.append(piece)
        rest = rest - piece.astype(F32)
    return parts


def _tri_dot(a, b, mode, tri):
    d = lambda u, v: lax.dot_general(u, v, _DIMS[mode], preferred_element_type=F32)
    if tri == 0:
        return sum(d(a.astype(BF16), piece) for piece in _split_bf16(b, 3))
    return sum(d(piece, b.astype(BF16)) for piece in _split_bf16(a, 3))


def _bdot_raw(a, b, mode):
    return lax.dot_general(a.astype(BF16), b.astype(BF16), _DIMS[mode], preferred_element_type=F32)


@functools.partial(jax.custom_vjp, nondiff_argnums=(2,))
def _bdot(a, b, mode):
    return _bdot_raw(a, b, mode)


def _bdot_fwd(a, b, mode):
    return _bdot_raw(a, b, mode), (a, b)


def _bdot_bwd(mode, res, ct):
    a, b = res
    if mode == "nn":
        return _bdot(ct, b, "nt"), _bdot(a, ct, "tn")
    if mode == "nt":
        return _bdot(ct, b, "nn"), _bdot(ct, a, "tn")
    return _bdot(b, ct, "nt"), _bdot(a, ct, "nn")


_bdot.defvjp(_bdot_fwd, _bdot_bwd)


@jax.custom_vjp
def _unit_lower_inverse(a_strict):
    return _unit_lower_inverse_raw(a_strict)


def _unit_lower_inverse_fwd(a_strict):
    t = _unit_lower_inverse_raw(a_strict)
    return t, t


def _unit_lower_inverse_bwd(t, ct):
    left = [_bdot(ti, ci, "tn") for ti, ci in zip(t, ct)]
    return (tuple(-_bdot(li, ti, "nt") for li, ti in zip(left, t)),)


_unit_lower_inverse.defvjp(_unit_lower_inverse_fwd, _unit_lower_inverse_bwd)


def _unit_lower_inverse_raw(a_strict):
    C = a_strict[0].shape[0]
    ii = lax.broadcasted_iota(jnp.int32, (C, C), 0)
    jj = lax.broadcasted_iota(jnp.int32, (C, C), 1)
    eye = (ii == jj).astype(F32)
    blk = 16
    same = (ii // blk) == (jj // blk)
    p = [-jnp.where(same, a, 0.0) for a in a_strict]
    t = [eye + x for x in p]
    for _ in range(3):
        p = [_bdot(x, x, "nn") for x in p]
        t = [ti + _bdot(ti, x, "nn") for ti, x in zip(t, p)]
    while blk < C:
        same2 = (ii // (2 * blk)) == (jj // (2 * blk))
        off = jnp.logical_and(same2, jnp.logical_not(same))
        te = [_bdot(ti, jnp.where(off, a, 0.0), "nn") for ti, a in zip(t, a_strict)]
        t = [ti - _bdot(x, ti, "nn") for ti, x in zip(t, te)]
        same, blk = same2, 2 * blk
    return tuple(t)


def _chunk_heads(q, k, v, gc_col, gc_row, b_col, s0):
    R = range(len(q))
    C = q[0].shape[0]
    ii = lax.broadcasted_iota(jnp.int32, (C, C), 0)
    jj = lax.broadcasted_iota(jnp.int32, (C, C), 1)
    rows = lax.broadcasted_iota(jnp.int32, (C, 1), 0)
    decay = [jnp.where(ii >= jj, jnp.exp(jnp.minimum(gc_col[h] - gc_row[h], 0.0)), 0.0) for h in R]
    kb = [k[h] * b_col[h] for h in R]
    a = [_bdot(kb[h], k[h], "nt") * decay[h] for h in R]
    qk = [_bdot(q[h], k[h], "nt") * decay[h] for h in R]
    t = _unit_lower_inverse(tuple(jnp.where(ii > jj, a[h], 0.0) for h in R))
    eg = [jnp.exp(gc_col[h]) for h in R]
    u = [_bdot(t[h], v[h] * b_col[h], "nn") for h in R]
    w = [_bdot(t[h], kb[h] * eg[h], "nn") for h in R]
    g_last = [jnp.sum(jnp.where(rows == C - 1, gc_col[h], 0.0), axis=0, keepdims=True) for h in R]
    kd = [k[h] * jnp.exp(g_last[h] - gc_col[h]) for h in R]
    ws = [_bdot(w[h], s0[h], "nn") for h in R]
    qs = [_bdot(q[h] * eg[h], s0[h], "nn") for h in R]
    v_new = [u[h] - ws[h] for h in R]
    o = [qs[h] + _bdot(qk[h], v_new[h], "nn") for h in R]
    s1 = [s0[h] * jnp.exp(g_last[h]) + _bdot(kd[h], v_new[h], "tn") for h in R]
    return tuple(o), tuple(s1)


def _pick_lane(a, h):
    lanes = lax.broadcasted_iota(jnp.int32, a.shape, 1)
    return jnp.sum(jnp.where(lanes == h, a, 0.0), axis=1, keepdims=True)


def _pick_row(a, h):
    rows = lax.broadcasted_iota(jnp.int32, a.shape, 0)
    return jnp.sum(jnp.where(rows == h, a, 0.0), axis=0, keepdims=True)


def _tri(C):
    ii = lax.broadcasted_iota(jnp.int32, (C, C), 0)
    jj = lax.broadcasted_iota(jnp.int32, (C, C), 1)
    return (ii >= jj).astype(F32)


def _delta_fwd(qkvn, g_pad, g_rows, beta_pad):
    S, W = qkvn.shape
    D = W // 3
    H = D // HEAD_DIM
    C = min(CHUNK, S)
    N = S // C

    def body(x_ref, gp_ref, gr_ref, bp_ref, o_ref, sall_ref, st):
        n = pl.program_id(0)

        @pl.when(n == 0)
        def _():
            st[...] = jnp.zeros_like(st)

        low = _tri(C)
        gc_cols = _tri_dot(low, gp_ref[...], "nn", 0)
        gc_rows = _tri_dot(gr_ref[...], low, "nt", 1)
        bcols = bp_ref[...]
        hs = range(H)
        s0 = tuple(st[h] for h in hs)
        for h in hs:
            sall_ref[h] = s0[h]
        o, s1 = _chunk_heads(
            tuple(x_ref[:, h * HEAD_DIM : (h + 1) * HEAD_DIM] for h in hs),
            tuple(x_ref[:, D + h * HEAD_DIM : D + (h + 1) * HEAD_DIM] for h in hs),
            tuple(x_ref[:, 2 * D + h * HEAD_DIM : 2 * D + (h + 1) * HEAD_DIM] for h in hs),
            tuple(_pick_lane(gc_cols, h) for h in hs), tuple(_pick_row(gc_rows, h) for h in hs),
            tuple(_pick_lane(bcols, h) for h in hs), s0,
        )
        for h in hs:
            st[h] = s1[h]
            o_ref[:, h * HEAD_DIM : (h + 1) * HEAD_DIM] = o[h]

    return pl.pallas_call(
        body,
        name="gdn_delta_fwd",
        grid=(N,),
        in_specs=[
            pl.BlockSpec((C, W), lambda n: (n, 0)),
            pl.BlockSpec((C, LANES), lambda n: (n, 0)),
            pl.BlockSpec((None, 8, C), lambda n: (n, 0, 0)),
            pl.BlockSpec((C, LANES), lambda n: (n, 0)),
        ],
        out_specs=[pl.BlockSpec((C, D), lambda n: (n, 0)), pl.BlockSpec((None, H, HEAD_DIM, HEAD_DIM), lambda n: (n, 0, 0, 0))],
        out_shape=[jax.ShapeDtypeStruct((S, D), F32), jax.ShapeDtypeStruct((N, H, HEAD_DIM, HEAD_DIM), F32)],
        scratch_shapes=[pltpu.VMEM((H, HEAD_DIM, HEAD_DIM), F32)],
        compiler_params=_params(1),
    )(qkvn, g_pad, g_rows, beta_pad)


def _delta_bwd(qkvn, g_pad, g_rows, beta_pad, s_all, do):
    S, W = qkvn.shape
    D = W // 3
    H = D // HEAD_DIM
    C = min(CHUNK, S)
    N = S // C

    def body(x_ref, gp_ref, gr_ref, bp_ref, sall_ref, do_ref, dx_ref, dgp_ref, dgr_ref, dbp_ref, dst):
        n = pl.program_id(0)

        @pl.when(n == 0)
        def _():
            dst[...] = jnp.zeros_like(dst)

        low = _tri(C)
        gc_cols = _tri_dot(low, gp_ref[...], "nn", 0)
        gc_rows = _tri_dot(gr_ref[...], low, "nt", 1)
        bcols = bp_ref[...]
        lane = lax.broadcasted_iota(jnp.int32, (1, LANES), 1)
        row8 = lax.broadcasted_iota(jnp.int32, (8, 1), 0)
        dgc_cols = jnp.zeros((C, LANES), F32)
        dgc_rows = jnp.zeros((8, C), F32)
        dbcols = jnp.zeros((C, LANES), F32)
        hs = range(H)
        _, vjp = jax.vjp(
            _chunk_heads,
            tuple(x_ref[:, h * HEAD_DIM : (h + 1) * HEAD_DIM] for h in hs),
            tuple(x_ref[:, D + h * HEAD_DIM : D + (h + 1) * HEAD_DIM] for h in hs),
            tuple(x_ref[:, 2 * D + h * HEAD_DIM : 2 * D + (h + 1) * HEAD_DIM] for h in hs),
            tuple(_pick_lane(gc_cols, h) for h in hs), tuple(_pick_row(gc_rows, h) for h in hs),
            tuple(_pick_lane(bcols, h) for h in hs), tuple(sall_ref[h] for h in hs),
        )
        dq, dk, dv, dgc, dgr, dbc, ds0 = vjp((tuple(do_ref[:, h * HEAD_DIM : (h + 1) * HEAD_DIM] for h in hs), tuple(dst[h] for h in hs)))
        for h in hs:
            dst[h] = ds0[h]
            dx_ref[:, h * HEAD_DIM : (h + 1) * HEAD_DIM] = dq[h]
            dx_ref[:, D + h * HEAD_DIM : D + (h + 1) * HEAD_DIM] = dk[h]
            dx_ref[:, 2 * D + h * HEAD_DIM : 2 * D + (h + 1) * HEAD_DIM] = dv[h]
            dgc_cols = dgc_cols + dgc[h] * (lane == h).astype(F32)
            dgc_rows = dgc_rows + dgr[h] * (row8 == h).astype(F32)
            dbcols = dbcols + dbc[h] * (lane == h).astype(F32)
        dgp_ref[...] = _tri_dot(low, dgc_cols, "tn", 0)
        dgr_ref[...] = _tri_dot(dgc_rows, low, "nn", 1)
        dbp_ref[...] = dbcols

    rev = lambda n: N - 1 - n
    return pl.pallas_call(
        body,
        name="gdn_delta_bwd",
        grid=(N,),
        in_specs=[
            pl.BlockSpec((C, W), lambda n: (rev(n), 0)),
            pl.BlockSpec((C, LANES), lambda n: (rev(n), 0)),
            pl.BlockSpec((None, 8, C), lambda n: (rev(n), 0, 0)),
            pl.BlockSpec((C, LANES), lambda n: (rev(n), 0)),
            pl.BlockSpec((None, H, HEAD_DIM, HEAD_DIM), lambda n: (rev(n), 0, 0, 0)),
            pl.BlockSpec((C, D), lambda n: (rev(n), 0)),
        ],
        out_specs=[
            pl.BlockSpec((C, W), lambda n: (rev(n), 0)),
            pl.BlockSpec((C, LANES), lambda n: (rev(n), 0)),
            pl.BlockSpec((None, 8, C), lambda n: (rev(n), 0, 0)),
            pl.BlockSpec((C, LANES), lambda n: (rev(n), 0)),
        ],
        out_shape=[
            jax.ShapeDtypeStruct((S, W), F32),
            jax.ShapeDtypeStruct((S, LANES), F32),
            jax.ShapeDtypeStruct((N, 8, C), F32),
            jax.ShapeDtypeStruct((S, LANES), F32),
        ],
        scratch_shapes=[pltpu.VMEM((H, HEAD_DIM, HEAD_DIM), F32)],
        compiler_params=_params(1),
    )(qkvn, g_pad, g_rows, beta_pad, s_all, do)


def _gate_norm_head(o, z, nw):
    return o * lax.rsqrt(jnp.mean(o * o, axis=-1, keepdims=True) + RMS_EPS) * nw * _silu(z)


def _gate_norm_fwd(o, z, nw):
    S, D = o.shape
    H = D // HEAD_DIM
    ts = min(ROW_TILE, S)

    def fn(i, n, rv, hv, cv):
        ov, zv = rv
        parts = [_gate_norm_head(ov[:, h * HEAD_DIM : (h + 1) * HEAD_DIM], zv[:, h * HEAD_DIM : (h + 1) * HEAD_DIM], cv[0]) for h in range(H)]
        return (jnp.concatenate(parts, axis=1),), ()

    return _rowwise("gdn_gate_norm_fwd", fn, S, ts, rows=[o, z], consts=[nw], outs=[(D, BF16)])[0]


def _gate_norm_bwd(dog, o, z, nw):
    S, D = o.shape
    H = D // HEAD_DIM
    ts = min(ROW_TILE, S)

    def fn(i, n, rv, hv, cv):
        dv, ov, zv = rv
        dos, dzs, dnw = [], [], None
        for h in range(H):
            sl = slice(h * HEAD_DIM, (h + 1) * HEAD_DIM)
            _, vjp = jax.vjp(_gate_norm_head, ov[:, sl], zv[:, sl], cv[0])
            a, b_, c_ = vjp(dv[:, sl])
            dos.append(a)
            dzs.append(b_)
            dnw = c_ if dnw is None else dnw + c_
        return (jnp.concatenate(dos, axis=1), jnp.concatenate(dzs, axis=1)), (dnw,)

    return _rowwise("gdn_gate_norm_bwd", fn, S, ts, rows=[dog, o, z], consts=[nw], outs=[(D, F32), (D, BF16)], accs=[(1, HEAD_DIM)])


def _square_bf16(r):
    rf = r.astype(F32)
    return rf * rf


def _mlp_ple_dw(li, dh, dhb, xa, p, r, gate, pp, w2):
    dpre = _mm(f"l{li}_mlp_down_bwd", dhb, w2, "nt", [BF16], epi=lambda acc, rr: (acc * (2.0 * rr.astype(F32)),), extras=[(r, "tile")], tm=1024, tn=1024, b_outer=True)
    dw2 = _mm(f"l{li}_mlp_dw2", r, dhb, "tn", [BF16], a_fn=_square_bf16, **DW_TILES)
    dgate, dpp, dbg = _ple_bwd(f"l{li}_ple_bwd", dh, gate, pp)
    dw1 = _mm(f"l{li}_mlp_dw1", xa, dpre, "tn", [BF16], out_blocks=N_DEV, **DW_TILES)
    dwg = _mm(f"l{li}_ple_dwg", xa, dgate, "tn", [BF16], **DW_TILES)
    dwp = _mm(f"l{li}_ple_dwp", p, dpp, "tn", [BF16], out_blocks=N_DEV, **DW_TILES)
    rows = lambda a: a.reshape((N_DEV, a.shape[0] // N_DEV) + a.shape[1:])
    return dpre, dgate, dw1, rows(dw2), rows(dwg), dbg, dwp


def _mlp_ple_dx(li, dh, dpre, dgate, w1, wg, after):
    t = _mm(f"l{li}_ple_gate_bwd", dgate, wg, "nt", [F32], epi=lambda acc, d: (acc + ALPHA * d,), extras=[(dh, "tile")], tm=1024, after=after)
    return _mm(f"l{li}_mlp_up_bwd", dpre, w1, "nt", [F32], epi=lambda acc, d: (acc + d,), extras=[(t, "tile")], tm=256, tk=4096)


def _local_step(x, p, tgt, W, fetch, emit):
    S, D = x.shape
    H = D // HEAD_DIM
    C = min(CHUNK, S)
    N = S // C
    lg = lambda i, j: W["ln_gain"][2 * i + j][None, :]
    lb = lambda i, j: W["ln_bias"][2 * i + j][None, :]
    G = {}

    pooled, xh0a, rs0a, x0a = _pool_fwd(x, W["pool_w"], W["pool_b"], W["pool_scale"], lg(0, 0), lb(0, 0))
    w0a = fetch("l0a", x0a)
    r0 = _mm("l0_mlp_up", x0a, w0a["mlp_w1"], "nn", [BF16], epi=lambda acc: (jnp.maximum(acc, 0.0),), tm=1024, tn=1024, b_outer=True, after=w0a.get("_after", ()))
    w0b = fetch("l0b", r0)
    gate0 = _mm("l0_ple_gate", x0a, w0b["ple_gate_w"], "nn", [F32], epi=lambda acc, bias: (acc + bias,), extras=[(W["ple_gate_b"][0:1], "row")], tm=1024)
    pp0 = _mm("l0_ple_proj", p[0], w0b["ple_proj"], "nn", [F32])
    w0c = fetch("l0c", pp0)
    ff0 = _mm("l0_mlp_down", r0, w0c["mlp_w2"], "nn", [F32], a_fn=_square_bf16, tm=256, tk=4096, after=w0c.get("_after", ()))
    xh0b, rs0b, x0b = _res_ln_ffpe("l0_ln_b", xh0a, ff0, gate0, pp0, lg(0, 0), lb(0, 0), lg(0, 1), lb(0, 1))

    wg_ = fetch("gdn", x0b)
    qkv_pre = _mm("gdn_in_qkv", x0b, wg_["gdn_wqkv"], "nn", [F32], tm=1024, tn=1024, b_outer=True, after=wg_.get("_after", ()))
    z = _mm("gdn_in_z", x0b, wg_["gdn_wz"], "nn", [F32], tm=1024)
    ba = _mm("gdn_in_ba", x0b, wg_["gdn_wba"], "nn", [F32])
    qkvn = _conv_fwd(qkv_pre, W["gdn_conv"])
    beta_pad, g_pad = _gates_fwd(ba, W["gdn_a_log"], W["gdn_dt_bias"])
    g_rows = g_pad[:, :8].reshape(N, C, 8).transpose(0, 2, 1)
    o, s_all = _delta_fwd(qkvn, g_pad, g_rows, beta_pad)
    og = _gate_norm_fwd(o, z, W["gdn_norm_w"])
    mix1 = _mm("gdn_out", og, wg_["gdn_w_out"], "nn", [F32], tm=1024)
    xh1a, rs1a, x1a = _res_ln_mix("l1_ln_a", xh0b, mix1, lg(0, 1), lb(0, 1), lg(1, 0), lb(1, 0))
    w1_ = fetch("l1", x1a)
    r1 = _mm("l1_mlp_up", x1a, w1_["mlp_w1"], "nn", [BF16], epi=lambda acc: (jnp.maximum(acc, 0.0),), tm=1024, tn=1024, b_outer=True)
    ff1 = _mm("l1_mlp_down", r1, w1_["mlp_w2"], "nn", [F32], a_fn=_square_bf16, tm=256, tk=4096)
    gate1 = _mm("l1_ple_gate", x1a, w1_["ple_gate_w"], "nn", [F32], epi=lambda acc, bias: (acc + bias,), extras=[(W["ple_gate_b"][1:2], "row")], tm=1024)
    pp1 = _mm("l1_ple_proj", p[1], w1_["ple_proj"], "nn", [F32])
    dh1b, dh1b_b, loss_cols, dg11, db11 = _final_ln_loss(xh1a, ff1, gate1, pp1, tgt, lg(1, 0), lb(1, 0), lg(1, 1), lb(1, 1))

    dpre1, dgate1, dw1_1, dw2_1, dwg_1, dbg_1, dwp_1 = _mlp_ple_dw(1, dh1b, dh1b_b, x1a, p[1], r1, gate1, pp1, w1_["mlp_w2"])
    tok = emit("l1", {"mlp_w1": dw1_1, "mlp_w2": dw2_1, "ple_gate_w": dwg_1, "ple_proj": dwp_1})
    dx1a = _mlp_ple_dx(1, dh1b, dpre1, dgate1, w1_["mlp_w1"], w1_["ple_gate_w"], [tok])
    dh1a, dh1a_b, dg10, db10 = _ln_bwd_call("l1_ln_a_bwd", dx1a, xh1a, rs1a, lg(1, 0))
    dog = _mm("gdn_out_bwd", dh1a_b, wg_["gdn_w_out"], "nt", [F32], tm=1024)
    dw_out = _mm("gdn_dw_out", og, dh1a_b, "tn", [BF16], **DW_TILES)
    dw_out = dw_out.reshape((N_DEV, dw_out.shape[0] // N_DEV) + dw_out.shape[1:])
    do, dz, dnw = _gate_norm_bwd(dog, o, z, W["gdn_norm_w"])
    dqkvn, dg_col, dg_row, dbeta = _delta_bwd(qkvn, g_pad, g_rows, beta_pad, s_all, do)
    dg_all = dg_col + jnp.pad(dg_row.transpose(0, 2, 1).reshape(S, 8), ((0, 0), (0, LANES - 8)))
    dba, dalog, ddt = _gates_bwd(ba, W["gdn_a_log"], W["gdn_dt_bias"], dbeta, dg_all, H)
    dqkv, dconv = _conv_bwd(qkv_pre, W["gdn_conv"], dqkvn)
    dwqkv = _mm("gdn_dwqkv", x0b, dqkv, "tn", [F32], **DW_TILES)
    dwz = _mm("gdn_dwz", x0b, dz, "tn", [F32], **DW_TILES)
    dwba = _mm("gdn_dwba", x0b, dba, "tn", [F32], **DW_TILES)
    dw_in = jnp.concatenate([dwqkv, dwz, dwba[:, :H], dwba[:, LANES : LANES + H]], axis=1)
    tok = emit("gdn", {"gdn_w_in": _split_blocks("gdn_w_in", dw_in).astype(BF16), "gdn_w_out": dw_out})
    t = _mm("gdn_in_ba_bwd", dba, wg_["gdn_wba"], "nt", [F32], epi=lambda acc, d: (acc + ALPHA * d,), extras=[(dh1a, "tile")], after=[tok])
    t = _mm("gdn_in_z_bwd", dz, wg_["gdn_wz"], "nt", [F32], epi=lambda acc, d: (acc + d,), extras=[(t, "tile")], tm=1024)
    dx0b = _mm("gdn_in_qkv_bwd", dqkv, wg_["gdn_wqkv"], "nt", [F32], epi=lambda acc, d: (acc + d,), extras=[(t, "tile")], tm=256, tk=3072)

    dh0b, dh0b_b, dg01, db01 = _ln_bwd_call("l0_ln_b_bwd", dx0b, xh0b, rs0b, lg(0, 1))
    dpre0, dgate0, dw1_0, dw2_0, dwg_0, dbg_0, dwp_0 = _mlp_ple_dw(0, dh0b, dh0b_b, x0a, p[0], r0, gate0, pp0, w0c["mlp_w2"])
    tok = emit("l0", {"mlp_w1": dw1_0, "mlp_w2": dw2_0, "ple_gate_w": dwg_0, "ple_proj": dwp_0})
    dx0a = _mlp_ple_dx(0, dh0b, dpre0, dgate0, w0a["mlp_w1"], w0b["ple_gate_w"], [tok])
    dh0a, _, dg00, db00 = _ln_bwd_call("l0_ln_a_bwd", dx0a, xh0a, rs0a, lg(0, 0))
    grad_x, dyp, dscale, dpb = _pool_bwd(dh0a, pooled, W["pool_w"], W["pool_b"], W["pool_scale"])
    G["pool_w"] = _pool_dw(pooled, dyp)

    G["ln_gain"] = jnp.concatenate([dg00, dg01, dg10, dg11], axis=0)
    G["ln_bias"] = jnp.concatenate([db00, db01, db10, db11], axis=0)
    G["pool_b"] = dpb
    G["pool_scale"] = dscale
    G["gdn_conv"] = dconv
    G["gdn_a_log"] = dalog[:, :H]
    G["gdn_dt_bias"] = ddt[:, :H]
    G["gdn_norm_w"] = dnw
    G["ple_gate_b"] = jnp.concatenate([dbg_0, dbg_1], axis=0)
    return loss_cols, grad_x, G


_HBM = pl.BlockSpec(memory_space=pltpu.HBM)


def _all_gather(name, shards):
    T = len(shards)

    def body(*refs):
        ins, outs = refs[:T], refs[T : 2 * T]
        send_sems, recv_sems, local_sems = refs[2 * T :]
        x, y, c = lax.axis_index("x"), lax.axis_index("y"), lax.axis_index("c")
        me, sibling = (x, y, c), (x, y, 1 - c)
        chips = [(1 - x, y), (x, 1 - y), (1 - x, 1 - y)]

        def blk(t, px, py, pc):
            return outs[t].at[4 * px + 2 * py + pc]

        def copy(t, k, block, to, src=None):
            return pltpu.make_async_remote_copy(
                src_ref=blk(t, *block) if src is None else src, dst_ref=blk(t, *block),
                send_sem=send_sems.at[t, k], recv_sem=recv_sems.at[t, k], device_id=to, device_id_type=MESH,
            )

        mine = [pltpu.make_async_copy(ins[t], blk(t, *me), local_sems.at[t]) for t in range(T)]
        for cp in mine:
            cp.start()
        first = []
        for t in range(T):
            first.append(copy(t, 0, me, sibling, src=ins[t]))
            first += [copy(t, 1 + j, me, (*chip, c), src=ins[t]) for j, chip in enumerate(chips)]
        for cp in first:
            cp.start()
        passed = []
        for j, chip in enumerate(chips):
            for t in range(T):
                copy(t, 1 + j, (*chip, c), me).wait_recv()
                fw = copy(t, 4 + j, (*chip, c), sibling)
                fw.start()
                passed.append(fw)
        for t in range(T):
            copy(t, 0, sibling, me).wait_recv()
            for j, chip in enumerate(chips):
                copy(t, 4 + j, (*chip, 1 - c), me).wait_recv()
        for cp in first + passed:
            cp.wait_send()
        for cp in mine:
            cp.wait()

    return pl.pallas_call(
        body,
        name=name,
        in_specs=[_HBM] * T,
        out_specs=[_HBM] * T,
        out_shape=[jax.ShapeDtypeStruct((N_DEV,) + s.shape, s.dtype) for s in shards],
        scratch_shapes=[pltpu.SemaphoreType.DMA((T, 7)), pltpu.SemaphoreType.DMA((T, 7)), pltpu.SemaphoreType.DMA((T,))],
    )(*shards)


def _exchange(name, blocks):
    def body(g_ref, o_ref, send_sems, recv_sems, local_sem):
        x, y, c = lax.axis_index("x"), lax.axis_index("y"), lax.axis_index("c")
        own = pltpu.make_async_copy(g_ref.at[4 * x + 2 * y + c], o_ref.at[N_DEV - 1], local_sem)
        own.start()
        copies = []
        for rel in range(1, N_DEV):
            px = 1 - x if rel & 4 else x
            py = 1 - y if rel & 2 else y
            pc = 1 - c if rel & 1 else c
            copies.append(
                pltpu.make_async_remote_copy(
                    src_ref=g_ref.at[4 * px + 2 * py + pc], dst_ref=o_ref.at[rel - 1],
                    send_sem=send_sems.at[rel - 1], recv_sem=recv_sems.at[rel - 1], device_id=(px, py, pc), device_id_type=MESH,
                )
            )
        for cp in copies:
            cp.start()
        for cp in copies:
            cp.wait_recv()
        for cp in copies:
            cp.wait_send()
        own.wait()

    return pl.pallas_call(
        body,
        name=name,
        in_specs=[_HBM],
        out_specs=_HBM,
        out_shape=jax.ShapeDtypeStruct(blocks.shape, blocks.dtype),
        scratch_shapes=[pltpu.SemaphoreType.DMA((N_DEV - 1,)), pltpu.SemaphoreType.DMA((N_DEV - 1,)), pltpu.SemaphoreType.DMA],
    )(blocks)


_SEM = pl.BlockSpec(memory_space=pltpu.SEMAPHORE)
_ANY = pl.BlockSpec(memory_space=pl.ANY)
_DATAFLOW = pltpu.SideEffectType.DATAFLOW_SIDE_EFFECTING
N_PEERS = N_DEV - 1


def _peer(rel, x, y, c):
    return (1 - x if rel & 4 else x, 1 - y if rel & 2 else y, 1 - c if rel & 1 else c)


def _send_start(name, srcs, lands, gather, after):
    T = len(srcs)

    def body(*refs):
        src_refs, land_refs = refs[:T], refs[T : 2 * T]
        send_sems, recv_sems = refs[2 * T + 1], refs[2 * T + 2]
        token = refs[-1]
        x, y, c = lax.axis_index("x"), lax.axis_index("y"), lax.axis_index("c")
        for t in range(T):
            for rel in range(1, N_DEV):
                px, py, pc = _peer(rel, x, y, c)
                pltpu.make_async_remote_copy(
                    src_ref=src_refs[t] if gather else src_refs[t].at[4 * px + 2 * py + pc],
                    dst_ref=land_refs[t].at[4 * x + 2 * y + c] if gather else land_refs[t].at[rel - 1],
                    send_sem=send_sems.at[t * N_PEERS + rel - 1], recv_sem=recv_sems.at[t * N_PEERS + rel - 1], device_id=(px, py, pc), device_id_type=MESH,
                ).start()
        token[...] = jnp.zeros_like(token)

    hbm = lambda a: pltpu.HBM(a.shape, a.dtype)
    return pl.pallas_call(
        body,
        name=name,
        out_shape=(pltpu.SemaphoreType.DMA((T * N_PEERS,)), pltpu.SemaphoreType.DMA((T * N_PEERS,)), *[hbm(a) for a in srcs],
                   *[hbm(a) for a in lands], jax.ShapeDtypeStruct((8, LANES), F32)),
        in_specs=(_HBM,) * (2 * T) + (_ANY,),
        out_specs=(_SEM, _SEM) + (_HBM,) * (2 * T) + (pl.BlockSpec(memory_space=pltpu.VMEM),),
        input_output_aliases={t: 2 + t for t in range(2 * T)},
        compiler_params=pltpu.CompilerParams(has_side_effects=_DATAFLOW),
    )(*[pltpu.with_memory_space_constraint(a, pltpu.HBM) for a in list(srcs) + list(lands)], after)


def _send_wait(name, started, after, gather):
    T = (len(started) - 3) // 2
    send_sems, recv_sems, token = started[0], started[1], started[-1]
    thru = started[2:-1]

    def body(*refs):
        src_refs, land_refs = refs[:T], refs[T : 2 * T]
        send_sems, recv_sems = refs[2 * T], refs[2 * T + 1]
        x, y, c = lax.axis_index("x"), lax.axis_index("y"), lax.axis_index("c")
        for t in range(T):
            for rel in range(1, N_DEV):
                cp = pltpu.make_async_remote_copy(
                    src_ref=src_refs[t] if gather else src_refs[t].at[0], dst_ref=land_refs[t].at[0],
                    send_sem=send_sems.at[t * N_PEERS + rel - 1], recv_sem=recv_sems.at[t * N_PEERS + rel - 1], device_id=_peer(rel, x, y, c), device_id_type=MESH,
                )
                cp.wait_send()
                cp.wait_recv()

    outs = pl.pallas_call(
        body,
        name=name,
        out_shape=tuple(pltpu.HBM(a.shape, a.dtype) for a in thru),
        in_specs=(_HBM,) * (2 * T) + (_SEM, _SEM, _ANY),
        out_specs=(_HBM,) * (2 * T),
        input_output_aliases={t: t for t in range(2 * T)},
        compiler_params=pltpu.CompilerParams(has_side_effects=_DATAFLOW),
    )(*thru, send_sems, recv_sems, after)
    return list(outs[:T]), list(outs[T:])


def _sum_blocks(name, parts, tr):
    _, R, Cw = parts[0].shape
    tr = tr if R % tr == 0 else R

    def body(*refs):
        acc = None
        for p_ref in refs[:-1]:
            for d in range(p_ref.shape[0]):
                v = p_ref[d].astype(F32)
                acc = v if acc is None else acc + v
        refs[-1][...] = acc

    return pl.pallas_call(
        body,
        name=name,
        grid=(R // tr,),
        in_specs=[pl.BlockSpec((a.shape[0], tr, Cw), lambda i: (0, i, 0)) for a in parts],
        out_specs=pl.BlockSpec((tr, Cw), lambda i: (i, 0)),
        out_shape=jax.ShapeDtypeStruct((R, Cw), F32),
        compiler_params=_params(1),
    )(*parts)


def _adamw(name, w, g, m, v):
    shape = w.shape
    cols = shape[-1]
    rows = w.size // cols
    tr = rows if rows <= 512 else 512
    assert rows % tr == 0
    w2, g2, m2, v2 = (a.reshape(rows, cols) for a in (w, g, m, v))

    def body(w_ref, g_ref, m_ref, v_ref, d_ref, mo_ref, vo_ref):
        gv = g_ref[...]
        mn = ADAM_B1 * m_ref[...] + (1.0 - ADAM_B1) * gv
        vn = ADAM_B2 * v_ref[...] + (1.0 - ADAM_B2) * jnp.square(gv)
        m_hat = mn / (1.0 - ADAM_B1**ADAM_STEP)
        v_hat = vn / (1.0 - ADAM_B2**ADAM_STEP)
        d_ref[...] = -ADAM_LR * (m_hat / (jnp.sqrt(v_hat) + ADAM_EPS) + ADAM_WD * w_ref[...])
        mo_ref[...] = mn
        vo_ref[...] = vn

    spec = pl.BlockSpec((tr, cols), lambda i: (i, 0))
    d, mn, vn = pl.pallas_call(
        body,
        name=name,
        grid=(rows // tr,),
        in_specs=[spec] * 4,
        out_specs=[spec] * 3,
        out_shape=[jax.ShapeDtypeStruct((rows, cols), F32)] * 3,
        compiler_params=_params(1),
    )(w2, g2, m2, v2)
    return d.reshape(shape), mn.reshape(shape), vn.reshape(shape)


SMALL_SHARDED = ("ln_gain", "ln_bias", "pool_b", "gdn_conv")
SMALL_REPLICATED = ("pool_scale", "gdn_a_log", "gdn_dt_bias", "gdn_norm_w", "ple_gate_b")
WEIGHTS = ("ln_gain", "ln_bias", "pool_w", "pool_b", "pool_scale", "gdn_w_in", "gdn_conv", "gdn_a_log", "gdn_dt_bias",
           "gdn_norm_w", "gdn_w_out", "mlp_w1", "mlp_w2", "ple_gate_w", "ple_gate_b", "ple_proj")
BIG_AXIS = {"gdn_w_in": 1, "gdn_w_out": 0, "mlp_w1": 1, "mlp_w2": 0, "ple_gate_w": 0, "ple_proj": 1, "pool_w": 1}
GATHER_GROUPS = {
    "l0a": (("mlp_w1", 0),),
    "l0b": (("ple_gate_w", 0), ("ple_proj", 0)),
    "l0c": (("mlp_w2", 0),),
    "gdn": (("gdn_w_in", 0), ("gdn_w_out", 0)),
    "l1": (("mlp_w1", 1), ("mlp_w2", 1), ("ple_gate_w", 1), ("ple_proj", 1)),
}
GATHER_AFTER = {"l0b": "l0a", "l0c": "l0a", "gdn": "l0c", "l1": "gdn"}
GRAD_GROUPS = {
    "l1": (("mlp_w1", 1), ("mlp_w2", 1), ("ple_gate_w", 1), ("ple_proj", 1)),
    "gdn": (("gdn_w_in", 0), ("gdn_w_out", 0)),
    "l0": (("mlp_w1", 0), ("mlp_w2", 0), ("ple_gate_w", 0), ("ple_proj", 0)),
}
PACK_PART_ALIGN = 16
SUM_TILE = 128


def _part_rows(a, width):
    rows = a.size // width
    return rows + (-rows) % PACK_PART_ALIGN


def _pack_rows(parts, width, dtype, align):
    padded = []
    for a in parts:
        a2 = a.reshape(-1, width).astype(dtype)
        padded.append(jnp.pad(a2, ((0, _part_rows(a, width) - a2.shape[0]), (0, 0))))
    flat = jnp.concatenate(padded, axis=0)
    return jnp.pad(flat, ((0, (-flat.shape[0]) % align), (0, 0)))


def _pack_blocks(parts, width, dtype, align):
    padded = []
    for a in parts:
        a2 = a.reshape(a.shape[0], -1, width).astype(dtype)
        padded.append(jnp.pad(a2, ((0, 0), (0, _part_rows(a[0], width) - a2.shape[1]), (0, 0))))
    flat = jnp.concatenate(padded, axis=1)
    return jnp.pad(flat, ((0, 0), (0, (-flat.shape[1]) % align), (0, 0)))


def _unpack_rows(packed, shapes, width):
    out, off = [], 0
    for shp in shapes:
        size = 1
        for d in shp:
            size *= d
        out.append(packed[..., off : off + size // width, :].reshape(packed.shape[:-2] + tuple(shp)))
        off += size // width + (-(size // width)) % PACK_PART_ALIGN
    return out


def _split_blocks(name, full):
    ax = BIG_AXIS[name]
    shp = full.shape
    a = full.reshape(shp[:ax] + (N_DEV, shp[ax] // N_DEV) + shp[ax + 1 :])
    return jnp.moveaxis(a, ax, 0)


def _join_blocks(name, blocks):
    ax = BIG_AXIS[name]
    a = jnp.moveaxis(blocks, 0, ax)
    shp = a.shape
    return a.reshape(shp[:ax] + (shp[ax] * shp[ax + 1],) + shp[ax + 2 :])


def _pack_small(parts):
    flat = jnp.concatenate([jnp.pad(a.reshape(-1), (0, (-a.size) % LANES)) for a in parts])
    rows = flat.size // LANES
    return jnp.pad(flat.reshape(rows, LANES), ((0, (-rows) % 8), (0, 0)))


def _unpack_small(packed, shapes):
    flat = packed.reshape(packed.shape[:-2] + (-1,))
    out, off = [], 0
    for shp in shapes:
        size = 1
        for s in shp:
            size *= s
        out.append(flat[..., off : off + size].reshape(flat.shape[:-1] + tuple(shp)))
        off += size + (-size) % LANES
    return out


def _split_w_in(w_in, D, H):
    pad = lambda a: jnp.pad(a, ((0, 0), (0, LANES - H)))
    return w_in[:, : 3 * D], w_in[:, 3 * D : 4 * D], jnp.concatenate([pad(w_in[:, 4 * D : 4 * D + H]), pad(w_in[:, 4 * D + H :])], axis=1)


def kernel(x, p, ln_gain, ln_bias, pool_w, pool_b, pool_scale, gdn_w_in, gdn_conv, gdn_a_log, gdn_dt_bias, gdn_norm_w, gdn_w_out, mlp_w1, mlp_w2, ple_gate_w, ple_gate_b, ple_proj, loss_target, m_ln_gain, m_ln_bias, m_pool_w, m_pool_b, m_pool_scale, m_gdn_w_in, m_gdn_conv, m_gdn_a_log, m_gdn_dt_bias, m_gdn_norm_w, m_gdn_w_out, m_mlp_w1, m_mlp_w2, m_ple_gate_w, m_ple_gate_b, m_ple_proj, v_ln_gain, v_ln_bias, v_pool_w, v_pool_b, v_pool_scale, v_gdn_w_in, v_gdn_conv, v_gdn_a_log, v_gdn_dt_bias, v_gdn_norm_w, v_gdn_w_out, v_mlp_w1, v_mlp_w2, v_ple_gate_w, v_ple_gate_b, v_ple_proj):
    w_sh = dict(ln_gain=ln_gain, ln_bias=ln_bias, pool_w=pool_w, pool_b=pool_b, pool_scale=pool_scale, gdn_w_in=gdn_w_in,
                gdn_conv=gdn_conv, gdn_a_log=gdn_a_log, gdn_dt_bias=gdn_dt_bias, gdn_norm_w=gdn_norm_w, gdn_w_out=gdn_w_out,
                mlp_w1=mlp_w1, mlp_w2=mlp_w2, ple_gate_w=ple_gate_w, ple_gate_b=ple_gate_b, ple_proj=ple_proj)
    m_sh = dict(ln_gain=m_ln_gain, ln_bias=m_ln_bias, pool_w=m_pool_w, pool_b=m_pool_b, pool_scale=m_pool_scale, gdn_w_in=m_gdn_w_in,
                gdn_conv=m_gdn_conv, gdn_a_log=m_gdn_a_log, gdn_dt_bias=m_gdn_dt_bias, gdn_norm_w=m_gdn_norm_w, gdn_w_out=m_gdn_w_out,
                mlp_w1=m_mlp_w1, mlp_w2=m_mlp_w2, ple_gate_w=m_ple_gate_w, ple_gate_b=m_ple_gate_b, ple_proj=m_ple_proj)
    v_sh = dict(ln_gain=v_ln_gain, ln_bias=v_ln_bias, pool_w=v_pool_w, pool_b=v_pool_b, pool_scale=v_pool_scale, gdn_w_in=v_gdn_w_in,
                gdn_conv=v_gdn_conv, gdn_a_log=v_gdn_a_log, gdn_dt_bias=v_gdn_dt_bias, gdn_norm_w=v_gdn_norm_w, gdn_w_out=v_gdn_w_out,
                mlp_w1=v_mlp_w1, mlp_w2=v_mlp_w2, ple_gate_w=v_ple_gate_w, ple_gate_b=v_ple_gate_b, ple_proj=v_ple_proj)
    xs, tg = x[0], loss_target[0]
    ps = p[:, 0]
    S, D = xs.shape
    H = D // HEAD_DIM
    me = 4 * lax.axis_index("x") + 2 * lax.axis_index("y") + lax.axis_index("c")
    layer = lambda n, l: (w_sh[n][0] if n in ("gdn_w_in", "gdn_w_out") else w_sh[n][l])

    pool_packed = _pack_rows([w_sh["pool_w"][0]], D, BF16, PACK_PART_ALIGN)
    small_packed = _pack_small([w_sh[n] for n in SMALL_SHARDED])
    pool_gathered, small_gathered = _all_gather("gather_first", [pool_packed, small_packed])
    W = {"pool_w": _join_blocks("pool_w", _unpack_rows(pool_gathered, [w_sh["pool_w"][0].shape], D)[0])}

    started = {}

    def start(g, after):
        src = _pack_rows([layer(n, l) for n, l in GATHER_GROUPS[g]], D, BF16, PACK_PART_ALIGN)
        started[g] = tuple(_send_start(f"gather_{g}_start", [src], [lax.empty((N_DEV,) + src.shape, BF16)], True, after))
        return started[g][-1]

    first_token = start("l0a", small_gathered)
    smalls = _unpack_small(small_gathered, [w_sh[n].shape for n in SMALL_SHARDED])
    for n, a in zip(SMALL_SHARDED, smalls):
        W[n] = jnp.moveaxis(a, 0, -2).reshape(a.shape[1:-1] + (N_DEV * a.shape[-1],))
    W["ln_gain"] = W["ln_gain"].reshape(2 * DEPTH, D)
    W["ln_bias"] = W["ln_bias"].reshape(2 * DEPTH, D)
    W["pool_b"] = W["pool_b"].reshape(1, D) + first_token[0:1, 0:1]
    W["gdn_conv"] = W["gdn_conv"][0]
    W["pool_scale"] = pool_scale
    W["ple_gate_b"] = ple_gate_b
    W["gdn_norm_w"] = gdn_norm_w
    W["gdn_a_log"] = jnp.pad(gdn_a_log, ((0, 0), (0, LANES - H)))
    W["gdn_dt_bias"] = jnp.pad(gdn_dt_bias, ((0, 0), (0, LANES - H)))

    def fetch(g, after):
        members = GATHER_GROUPS[g]
        (src,), (land,) = _send_wait(f"gather_{g}_wait", started[g], after, True)
        tokens = [start(nxt, land) for nxt, prev in GATHER_AFTER.items() if prev == g]
        land = lax.dynamic_update_index_in_dim(land, src, me, 0)
        parts = _unpack_rows(land, [layer(n, l).shape for n, l in members], D)
        out = {n: _join_blocks(n, a) for (n, _), a in zip(members, parts)}
        if "gdn_w_in" in out:
            out["gdn_wqkv"], out["gdn_wz"], out["gdn_wba"] = _split_w_in(out.pop("gdn_w_in"), D, H)
        out["_after"] = tokens
        return out

    sent = {}

    def emit(g, grads):
        srcs = [grads[n] for n, _ in GRAD_GROUPS[g]]
        lands = [lax.empty((N_PEERS,) + a.shape[1:], BF16) for a in srcs]
        sent[g] = tuple(_send_start(f"grads_{g}_start", srcs, lands, False, srcs[0]))
        return sent[g][-1]

    loss_cols, grad_x, G = _local_step(xs, ps, tg, W, fetch, emit)
    loss = lax.psum(0.5 * jnp.sum(loss_cols) / D, MESH_AXES)

    pool_src = _pack_blocks([_split_blocks("pool_w", G["pool_w"])], D, BF16, PACK_PART_ALIGN)
    pool_sum = _sum_blocks("sum_pool_grads", [_exchange("exchange_pool_grads", pool_src)], SUM_TILE)
    grads = {"pool_w": _unpack_rows(pool_sum, [w_sh["pool_w"][0].shape], D)[0].reshape(w_sh["pool_w"].shape)}
    small_names = SMALL_SHARDED + SMALL_REPLICATED
    gs_packed = _pack_small([G[n] for n in small_names])
    (gs_all,) = _all_gather("gather_small_grads", [gs_packed])
    gs_sum = _sum_blocks("sum_small_grads", [gs_all], SUM_TILE)
    for n, a in zip(small_names, _unpack_small(gs_sum, [G[n].shape for n in small_names])):
        if n in SMALL_SHARDED:
            width = w_sh[n].shape[-1]
            a = a.reshape(w_sh[n].shape[:-1] + (N_DEV * width,))
            a = lax.dynamic_slice_in_dim(a, me * width, width, axis=a.ndim - 1)
        grads[n] = a.reshape(w_sh[n].shape)

    per_layer = {}
    for g, members in GRAD_GROUPS.items():
        srcs, lands = _send_wait(f"grads_{g}_wait", sent[g], grad_x, False)
        for (n, l), src, land in zip(members, srcs, lands):
            own = lax.dynamic_index_in_dim(src, me, 0, keepdims=True)
            as3d = lambda a: a.reshape(a.shape[0], -1, a.shape[-1])
            per_layer[(n, l)] = _sum_blocks(f"sum_grads_{n}_{l}", [as3d(land), as3d(own)], SUM_TILE).reshape(layer(n, l).shape)
    for n in ("gdn_w_in", "gdn_w_out"):
        grads[n] = per_layer[(n, 0)][None]
    for n in ("mlp_w1", "mlp_w2", "ple_gate_w", "ple_proj"):
        grads[n] = jnp.stack([per_layer[(n, 0)], per_layer[(n, 1)]])

    deltas, new_m, new_v = {}, {}, {}
    for n in WEIGHTS:
        deltas[n], new_m[n], new_v[n] = _adamw(f"adamw_{n}", w_sh[n], grads[n], m_sh[n], v_sh[n])
    return (loss, grad_x[None], *[grads[n] for n in WEIGHTS], *[deltas[n] for n in WEIGHTS],
            *[new_m[n] for n in WEIGHTS], *[new_v[n] for n in WEIGHTS])
```

```python
import functools

import jax
import jax.numpy as jnp
from jax import lax
from jax.experimental import pallas as pl
from jax.experimental.pallas import tpu as pltpu

F32 = jnp.float32
BF16 = jnp.bfloat16
MESH_AXES = ("x", "y", "c")
N_DEV = 8
MESH = pl.DeviceIdType.MESH

DEPTH = 2
ALPHA = (2.0 * DEPTH) ** 0.25
LN_EPS = 1e-5
RMS_EPS = 1e-6
L2_EPS = 1e-6
HEAD_DIM = 128
CONV_WIDTH = 4
POOL_WINDOWS = (2, 4, 8, 16)
POOL_HALO = 16
CONV_HALO = 8
LANES = 128
ADAM_LR = 0.001
ADAM_B1 = 0.9
ADAM_B2 = 0.999
ADAM_EPS = 1e-08
ADAM_WD = 0.01
ADAM_STEP = 10

VMEM_LIMIT = 56 * 1024 * 1024
ROW_TILE = 512
CONV_TILE = 256
CHUNK = 128
MM_TM, MM_TN, MM_TK = 512, 1024, 1024
DW_TILES = dict(tm=256, tn=512, tk=8192, b_outer=True)

_DIMS = {
    "nn": (((1,), (0,)), ((), ())),
    "nt": (((1,), (1,)), ((), ())),
    "tn": (((0,), (0,)), ((), ())),
}


def _params(n_axes):
    return pltpu.CompilerParams(dimension_semantics=("arbitrary",) * n_axes, vmem_limit_bytes=VMEM_LIMIT)


def _fit(tile, n):
    tile = min(tile, n)
    while n % tile:
        tile //= 2
    return tile


def _mm(name, a, b, mode, out_dtypes, epi=None, extras=(), a_fn=None, tm=None, tn=None, tk=None, b_outer=False, after=(), out_blocks=1, accs=0):
    if mode == "tn":
        K, M = a.shape
    else:
        M, K = a.shape
    N = b.shape[0] if mode == "nt" else b.shape[1]
    tm, tn, tk = _fit(tm or MM_TM, M), (N // out_blocks if out_blocks > 1 else N if accs else _fit(tn or MM_TN, N)), _fit(tk or MM_TK, K)
    nk = K // tk

    def at(f):
        return (lambda j, i, k: f(i, j, k)) if b_outer else f

    a_spec = pl.BlockSpec((tk, tm), at(lambda i, j, k: (k, i))) if mode == "tn" else pl.BlockSpec((tm, tk), at(lambda i, j, k: (i, k)))
    b_spec = pl.BlockSpec((tn, tk), at(lambda i, j, k: (j, k))) if mode == "nt" else pl.BlockSpec((tk, tn), at(lambda i, j, k: (k, j)))
    ex_spec = {"tile": pl.BlockSpec((tm, tn), at(lambda i, j, k: (i, j))), "row": pl.BlockSpec((1, tn), at(lambda i, j, k: (0, j))),
               "rows": pl.BlockSpec((tm, LANES), at(lambda i, j, k: (i, 0)))}
    ex_specs = [ex_spec[kind] for _, kind in extras]
    assert accs == 0 or (tn == N and nk == 1 and not b_outer), name
    n_ex, n_out, n_after = len(extras), len(out_dtypes), len(after)

    def body(*refs):
        a_ref, b_ref = refs[0], refs[1]
        ex_refs = refs[2 : 2 + n_ex]
        out_refs = refs[2 + n_ex + n_after : 2 + n_ex + n_after + n_out]
        av = a_ref[...]
        if a_fn is not None:
            av = a_fn(av)
        part = lax.dot_general(av.astype(BF16), b_ref[...].astype(BF16), _DIMS[mode], preferred_element_type=F32)

        def finish(res):
            vals = epi(res, *[e[...] for e in ex_refs]) if epi is not None else (res,)
            for o_ref, v in zip(out_refs, vals[:n_out]):
                o_ref[...] = v.astype(o_ref.dtype)
            for a_ref, v in zip(refs[2 + n_ex + n_after + n_out :], vals[n_out:]):

                @pl.when(pl.program_id(0) == 0)
                def _(a_ref=a_ref, v=v):
                    a_ref[...] = v

                @pl.when(pl.program_id(0) > 0)
                def _(a_ref=a_ref, v=v):
                    a_ref[...] += v

        if nk == 1:
            finish(part)
        else:
            acc = refs[-1]
            k = pl.program_id(2)

            @pl.when(k == 0)
            def _():
                acc[...] = part

            @pl.when(k > 0)
            def _():
                acc[...] += part

            @pl.when(k == nk - 1)
            def _():
                finish(acc[...])

    outs = pl.pallas_call(
        body,
        name=name,
        grid=(N // tn, M // tm, nk) if b_outer else (M // tm, N // tn, nk),
        in_specs=[a_spec, b_spec] + ex_specs + [pl.BlockSpec(memory_space=pl.ANY)] * n_after,
        out_specs=[pl.BlockSpec((tm, tn), at(lambda i, j, k: (i, j))) if out_blocks == 1 else pl.BlockSpec((None, tm, tn), at(lambda i, j, k: (j, i, 0)))
                   for _ in out_dtypes] + [pl.BlockSpec((1, N), lambda i, j, k: (0, 0))] * accs,
        out_shape=[jax.ShapeDtypeStruct((M, N) if out_blocks == 1 else (out_blocks, M, tn), dt) for dt in out_dtypes]
        + [jax.ShapeDtypeStruct((1, N), F32)] * accs,
        scratch_shapes=[pltpu.VMEM((tm, tn), F32)] if nk > 1 else [],
        compiler_params=_params(3),
    )(a, b, *[e for e, _ in extras], *after)
    return outs[0] if n_out + accs == 1 else outs


def _rowwise(name, fn, S, ts, rows=(), halos=(), consts=(), outs=(), accs=()):
    ts = min(ts, S)
    assert S % ts == 0
    n = S // ts
    in_specs = [pl.BlockSpec((ts, a.shape[1]), lambda i: (i, 0)) for a in rows]
    for a, kind, hr in halos:
        r, nb = ts // hr, S // hr
        if kind == "prev":
            in_specs.append(pl.BlockSpec((hr, a.shape[1]), lambda i, r=r: (jnp.maximum(i * r - 1, 0), 0)))
        else:
            in_specs.append(pl.BlockSpec((hr, a.shape[1]), lambda i, r=r, nb=nb: (jnp.minimum((i + 1) * r, nb - 1), 0)))
    in_specs += [pl.BlockSpec(a.shape, lambda i, nd=a.ndim: (0,) * nd) for a in consts]
    out_specs = [pl.BlockSpec((ts, w), lambda i: (i, 0)) for w, _ in outs]
    out_specs += [pl.BlockSpec((r, w), lambda i: (0, 0)) for r, w in accs]
    out_shape = [jax.ShapeDtypeStruct((S, w), dt) for w, dt in outs]
    out_shape += [jax.ShapeDtypeStruct((r, w), F32) for r, w in accs]
    nr, nh, nc, no = len(rows), len(halos), len(consts), len(outs)

    def body(*refs):
        i = pl.program_id(0)
        rv = [r[...] for r in refs[:nr]]
        hv = [r[...] for r in refs[nr : nr + nh]]
        cv = [r[...] for r in refs[nr + nh : nr + nh + nc]]
        o_refs = refs[nr + nh + nc : nr + nh + nc + no]
        a_refs = refs[nr + nh + nc + no :]
        ovals, avals = fn(i, n, rv, hv, cv)
        for o_ref, v in zip(o_refs, ovals):
            o_ref[...] = v.astype(o_ref.dtype)
        for a_ref, v in zip(a_refs, avals):

            @pl.when(i == 0)
            def _(a_ref=a_ref, v=v):
                a_ref[...] = v

            @pl.when(i > 0)
            def _(a_ref=a_ref, v=v):
                a_ref[...] += v

    res = pl.pallas_call(
        body,
        name=name,
        grid=(n,),
        in_specs=in_specs,
        out_specs=out_specs,
        out_shape=out_shape,
        compiler_params=_params(1),
    )(*rows, *[h[0] for h in halos], *consts)
    return list(res)


def _ln(h, g, b):
    mu = jnp.mean(h, axis=-1, keepdims=True)
    d = h - mu
    var = jnp.mean(d * d, axis=-1, keepdims=True)
    rstd = lax.rsqrt(var + LN_EPS)
    xhat = d * rstd
    return xhat, rstd, xhat * g + b


def _ln_bwd(dy, xhat, rstd, g):
    dxh = dy * g
    m1 = jnp.mean(dxh, axis=-1, keepdims=True)
    m2 = jnp.mean(dxh * xhat, axis=-1, keepdims=True)
    dh = rstd * (dxh - m1 - xhat * m2)
    return dh, jnp.sum(dy * xhat, axis=0, keepdims=True), jnp.sum(dy, axis=0, keepdims=True)


def _wide(col, ts):
    return jnp.broadcast_to(col, (ts, LANES))


def _pool_fwd(x, wp, pb, ps, g, b):
    S, D = x.shape
    gw = D // len(POOL_WINDOWS)
    ts = min(ROW_TILE, S)

    def fn(i, n, rv, hv, cv):
        (xc,), (xp,) = rv, hv
        wpv, pbv, psv, gv, bv = cv
        xp = jnp.where(i > 0, xp, 0.0)
        xx = jnp.concatenate([xp, xc], axis=0)
        t = i * ts + lax.broadcasted_iota(jnp.int32, (ts, 1), 0)
        pooled, ys = [], []
        for gi, w in enumerate(POOL_WINDOWS):
            s = xx[:, gi * gw : (gi + 1) * gw]
            k = 1
            while k < w:
                s = s + pltpu.roll(s, k, axis=0)
                k *= 2
            cnt = jnp.minimum(t + 1, w).astype(F32)
            pg = (s[POOL_HALO:, :] / cnt - xc[:, gi * gw : (gi + 1) * gw]).astype(BF16)
            pooled.append(pg)
            ys.append(jnp.dot(pg, wpv[gi], preferred_element_type=F32))
        y = jnp.concatenate(ys, axis=1)
        h = ALPHA * xc + (y + pbv) * psv
        xhat, rstd, xa = _ln(h, gv, bv)
        return (jnp.concatenate(pooled, axis=1), xhat, _wide(rstd, ts), xa), ()

    return _rowwise(
        "pool_fwd", fn, S, ts, rows=[x], halos=[(x, "prev", POOL_HALO)], consts=[wp, pb, ps, g, b],
        outs=[(D, BF16), (D, F32), (LANES, F32), (D, BF16)],
    )


def _pool_bwd(dh, pooled, wp, pb, ps):
    S, D = dh.shape
    gw = D // len(POOL_WINDOWS)
    ts = min(ROW_TILE, S)
    te = ts + POOL_HALO

    def fn(i, n, rv, hv, cv):
        (dhc, pc), (dhn,) = rv, hv
        wpv, pbv, psv = cv
        dhn = jnp.where(i < n - 1, dhn, 0.0)
        dy_ext = jnp.concatenate([dhc, dhn], axis=0) * psv
        dyb = dy_ext.astype(BF16)
        t = i * ts + lax.broadcasted_iota(jnp.int32, (te, 1), 0)
        dxs, ys = [], []
        for gi, w in enumerate(POOL_WINDOWS):
            sl = slice(gi * gw, (gi + 1) * gw)
            dp = lax.dot_general(dyb[:, sl], wpv[gi], _DIMS["nt"], preferred_element_type=F32)
            s = dp / jnp.minimum(t + 1, w).astype(F32)
            k = 1
            while k < w:
                s = s + pltpu.roll(s, k, axis=0)
                k *= 2
            s = pltpu.roll(s, POOL_HALO - (w - 1), axis=0)
            dxs.append(s[POOL_HALO:, :] - dp[:ts, :])
            ys.append(jnp.dot(pc[:, sl], wpv[gi], preferred_element_type=F32))
        dx = ALPHA * dhc + jnp.concatenate(dxs, axis=1)
        y = jnp.concatenate(ys, axis=1) + pbv
        dscale = jnp.sum(dhc * y, axis=0, keepdims=True)
        dbias = jnp.sum(dy_ext[:ts, :], axis=0, keepdims=True)
        return (dx, dyb[:ts, :]), (dscale, dbias)

    return _rowwise(
        "pool_bwd", fn, S, ts, rows=[dh, pooled], halos=[(dh, "next", POOL_HALO)], consts=[wp, pb, ps],
        outs=[(D, F32), (D, BF16)], accs=[(1, D), (1, D)],
    )


def _pool_dw(pooled, dy):
    S, D = pooled.shape
    G = len(POOL_WINDOWS)
    gw = D // G
    tk = min(MM_TK, S)
    nk = S // tk

    def body(p_ref, d_ref, o_ref):
        k = pl.program_id(1)
        part = lax.dot_general(p_ref[...], d_ref[...], _DIMS["tn"], preferred_element_type=F32)

        @pl.when(k == 0)
        def _():
            o_ref[...] = part

        @pl.when(k > 0)
        def _():
            o_ref[...] += part

    return pl.pallas_call(
        body,
        name="pool_dw",
        grid=(G, nk),
        in_specs=[pl.BlockSpec((tk, gw), lambda g, k: (k, g)), pl.BlockSpec((tk, gw), lambda g, k: (k, g))],
        out_specs=pl.BlockSpec((None, gw, gw), lambda g, k: (g, 0, 0)),
        out_shape=jax.ShapeDtypeStruct((G, gw, gw), F32),
        compiler_params=_params(2),
    )(pooled, dy)


def _res_ln_mix(name, xhat_p, mix, gp_, bp_, g, b):
    S, D = xhat_p.shape
    ts = min(ROW_TILE, S)

    def fn(i, n, rv, hv, cv):
        xh, m = rv
        gpv, bpv, gv, bv = cv
        xhat, rstd, xo = _ln(ALPHA * (xh * gpv + bpv) + m, gv, bv)
        return (xhat, _wide(rstd, ts), xo), ()

    return _rowwise(name, fn, S, ts, rows=[xhat_p, mix], consts=[gp_, bp_, g, b], outs=[(D, F32), (LANES, F32), (D, BF16)])


def _res_ln_ffpe(name, xhat_p, ff, gate, pp, gp_, bp_, g, b):
    S, D = xhat_p.shape
    ts = min(ROW_TILE, S)

    def fn(i, n, rv, hv, cv):
        xh, f, gt, p_ = rv
        gpv, bpv, gv, bv = cv
        xhat, rstd, xo = _ln(ALPHA * (xh * gpv + bpv) + f + jax.nn.sigmoid(gt) * p_, gv, bv)
        return (xhat, _wide(rstd, ts), xo), ()

    return _rowwise(name, fn, S, ts, rows=[xhat_p, ff, gate, pp], consts=[gp_, bp_, g, b], outs=[(D, F32), (LANES, F32), (D, BF16)])


def _final_ln_loss(xhat_p, ff, gate, pp, tgt, gp_, bp_, g, b):
    S, D = xhat_p.shape
    ts = min(ROW_TILE, S)

    def fn(i, n, rv, hv, cv):
        xh, f, gt, p_, tg = rv
        gpv, bpv, gv, bv = cv
        xhat, rstd, y = _ln(ALPHA * (xh * gpv + bpv) + f + jax.nn.sigmoid(gt) * p_, gv, bv)
        e = y - tg
        dh, dg, db = _ln_bwd(e * (1.0 / D), xhat, rstd, gv)
        return (dh, dh), (jnp.sum(e * e, axis=0, keepdims=True), dg, db)

    return _rowwise(
        "final_ln_loss", fn, S, ts, rows=[xhat_p, ff, gate, pp, tgt], consts=[gp_, bp_, g, b],
        outs=[(D, F32), (D, BF16)], accs=[(1, D), (1, D), (1, D)],
    )


def _ln_bwd_epi(acc, rest, xhat, rstd, g):
    dh, dg, db = _ln_bwd(acc + rest, xhat, rstd[:, :1], g)
    return dh, dh, dg, db


def _ple_bwd(name, dh, gate, pp):
    S, D = dh.shape
    ts = min(ROW_TILE, S)

    def fn(i, n, rv, hv, cv):
        d, gt, p_ = rv
        sg = jax.nn.sigmoid(gt)
        dgt = d * p_ * sg * (1.0 - sg)
        return (dgt, d * sg), (jnp.sum(dgt, axis=0, keepdims=True),)

    return _rowwise(name, fn, S, ts, rows=[dh, gate, pp], outs=[(D, BF16), (D, BF16)], accs=[(1, D)])


def _silu(c):
    return c * jax.nn.sigmoid(c)


def _qkv_point(c, is_qk, scale):
    s = _silu(c)
    nrm = s * lax.rsqrt(jnp.sum(s * s, axis=-1, keepdims=True) + L2_EPS) * scale
    return jnp.where(is_qk, nrm, s)


def _conv_rows(xx, wv, lo, rows):
    acc = None
    for j in range(CONV_WIDTH):
        sh = CONV_WIDTH - 1 - j
        term = (pltpu.roll(xx, sh, axis=0) if sh else xx)[lo : lo + rows, :] * wv[j : j + 1, :]
        acc = term if acc is None else acc + term
    return acc


def _conv_fwd(qkv_pre, conv_w):
    S, W = qkv_pre.shape
    D = W // 3
    H = D // HEAD_DIM
    ts = min(CONV_TILE, S)
    r = ts // CONV_HALO

    def body(x_ref, xp_ref, w_ref, o_ref):
        j, i = pl.program_id(0), pl.program_id(1)
        xp = jnp.where(i > 0, xp_ref[...], 0.0)
        xx = jnp.concatenate([xp, x_ref[...]], axis=0)
        c = _conv_rows(xx, w_ref[...], CONV_HALO, ts)
        scale = jnp.where(j == 0, HEAD_DIM**-0.5, 1.0).astype(F32)
        for h in range(H):
            sl = slice(h * HEAD_DIM, (h + 1) * HEAD_DIM)
            o_ref[:, sl] = _qkv_point(c[:, sl], j < 2, scale)

    return pl.pallas_call(
        body,
        name="gdn_conv_fwd",
        grid=(3, S // ts),
        in_specs=[
            pl.BlockSpec((ts, D), lambda j, i: (i, j)),
            pl.BlockSpec((CONV_HALO, D), lambda j, i: (jnp.maximum(i * r - 1, 0), j)),
            pl.BlockSpec((CONV_WIDTH, D), lambda j, i: (0, j)),
        ],
        out_specs=pl.BlockSpec((ts, D), lambda j, i: (i, j)),
        out_shape=jax.ShapeDtypeStruct((S, W), F32),
        compiler_params=_params(2),
    )(qkv_pre, qkv_pre, conv_w)


def _conv_bwd(qkv_pre, conv_w, dqkvn):
    S, W = qkv_pre.shape
    D = W // 3
    H = D // HEAD_DIM
    ts = min(CONV_TILE, S)
    r, nb = ts // CONV_HALO, S // CONV_HALO
    te = ts + CONV_HALO

    def body(x_ref, xp_ref, xn_ref, w_ref, d_ref, dn_ref, dx_ref, dw_ref):
        j, i = pl.program_id(0), pl.program_id(1)
        n = pl.num_programs(1)
        wv = w_ref[...]
        xp = jnp.where(i > 0, xp_ref[...], 0.0)
        xx = jnp.concatenate([xp, x_ref[...], xn_ref[...]], axis=0)
        c = _conv_rows(xx, wv, CONV_HALO, te)
        dn = jnp.where(i < n - 1, dn_ref[...], 0.0)
        dout = jnp.concatenate([d_ref[...], dn], axis=0)
        scale = jnp.where(j == 0, HEAD_DIM**-0.5, 1.0).astype(F32)
        dcs = []
        for h in range(H):
            sl = slice(h * HEAD_DIM, (h + 1) * HEAD_DIM)
            _, vjp = jax.vjp(lambda cc: _qkv_point(cc, j < 2, scale), c[:, sl])
            dcs.append(vjp(dout[:, sl])[0])
        dc = jnp.concatenate(dcs, axis=1)
        dx = None
        dws = []
        for jj in range(CONV_WIDTH):
            sh = CONV_WIDTH - 1 - jj
            term = pltpu.roll(dc, CONV_HALO - sh, axis=0)[CONV_HALO:, :] * wv[jj : jj + 1, :]
            dx = term if dx is None else dx + term
            xs = (pltpu.roll(xx, sh, axis=0) if sh else xx)[CONV_HALO : CONV_HALO + ts, :]
            dws.append(jnp.sum(dc[:ts, :] * xs, axis=0, keepdims=True))
        dx_ref[...] = dx.astype(dx_ref.dtype)
        dw = jnp.concatenate(dws, axis=0)

        @pl.when(i == 0)
        def _():
            dw_ref[...] = dw

        @pl.when(i > 0)
        def _():
            dw_ref[...] += dw

    return pl.pallas_call(
        body,
        name="gdn_conv_bwd",
        grid=(3, S // ts),
        in_specs=[
            pl.BlockSpec((ts, D), lambda j, i: (i, j)),
            pl.BlockSpec((CONV_HALO, D), lambda j, i: (jnp.maximum(i * r - 1, 0), j)),
            pl.BlockSpec((CONV_HALO, D), lambda j, i: (jnp.minimum((i + 1) * r, nb - 1), j)),
            pl.BlockSpec((CONV_WIDTH, D), lambda j, i: (0, j)),
            pl.BlockSpec((ts, D), lambda j, i: (i, j)),
            pl.BlockSpec((CONV_HALO, D), lambda j, i: (jnp.minimum((i + 1) * r, nb - 1), j)),
        ],
        out_specs=[pl.BlockSpec((ts, D), lambda j, i: (i, j)), pl.BlockSpec((CONV_WIDTH, D), lambda j, i: (0, j))],
        out_shape=[jax.ShapeDtypeStruct((S, W), BF16), jax.ShapeDtypeStruct((CONV_WIDTH, W), F32)],
        compiler_params=_params(2),
    )(qkv_pre, qkv_pre, qkv_pre, conv_w, dqkvn, dqkvn)


def _softplus(x):
    pos = x > 0.0
    return jnp.where(pos, x, 0.0) + jnp.log(1.0 + jnp.exp(jnp.where(pos, -x, x)))


def _gates(bl, al, alog, dt):
    return jax.nn.sigmoid(bl), -jnp.exp(alog) * _softplus(al + dt)


def _gates_fwd(ba, alog, dt):
    S = ba.shape[0]
    ts = min(ROW_TILE, S)

    def fn(i, n, rv, hv, cv):
        return _gates(rv[0][:, :LANES], rv[0][:, LANES:], cv[0], cv[1]), ()

    return _rowwise("gdn_gates_fwd", fn, S, ts, rows=[ba], consts=[alog, dt], outs=[(LANES, F32), (LANES, F32)])


def _gates_bwd(ba, alog, dt, dbeta, dg, H):
    S = ba.shape[0]
    ts = min(ROW_TILE, S)

    def fn(i, n, rv, hv, cv):
        bav, dbv, dgv = rv
        real = lax.broadcasted_iota(jnp.int32, (1, LANES), 1) < H
        _, vjp = jax.vjp(_gates, bav[:, :LANES], bav[:, LANES:], cv[0], cv[1])
        dbl, dal, dalog, ddt = vjp((jnp.where(real, dbv, 0.0), jnp.where(real, dgv, 0.0)))
        dbl, dal = jnp.where(real, dbl, 0.0), jnp.where(real, dal, 0.0)
        return (jnp.concatenate([dbl, dal], axis=1),), (jnp.where(real, dalog, 0.0), jnp.where(real, ddt, 0.0))

    return _rowwise(
        "gdn_gates_bwd", fn, S, ts, rows=[ba, dbeta, dg], consts=[alog, dt], outs=[(2 * LANES, BF16)],
        accs=[(1, LANES), (1, LANES)],
    )


def _split_bf16(a, n):
    parts, rest = [], a
    for _ in range(n):
        piece = rest.astype(BF16)
        parts.append(piece)
        rest = rest - piece.astype(F32)
    return parts


def _tri_dot(a, b, mode, tri):
    d = lambda u, v: lax.dot_general(u, v, _DIMS[mode], preferred_element_type=F32)
    if tri == 0:
        return sum(d(a.astype(BF16), piece) for piece in _split_bf16(b, 3))
    return sum(d(piece, b.astype(BF16)) for piece in _split_bf16(a, 3))


def _bdot_raw(a, b, mode):
    return lax.dot_general(a.astype(BF16), b.astype(BF16), _DIMS[mode], preferred_element_type=F32)


@functools.partial(jax.custom_vjp, nondiff_argnums=(2,))
def _bdot(a, b, mode):
    return _bdot_raw(a, b, mode)


def _bdot_fwd(a, b, mode):
    return _bdot_raw(a, b, mode), (a, b)


def _bdot_bwd(mode, res, ct):
    a, b = res
    if mode == "nn":
        return _bdot(ct, b, "nt"), _bdot(a, ct, "tn")
    if mode == "nt":
        return _bdot(ct, b, "nn"), _bdot(ct, a, "tn")
    return _bdot(b, ct, "nt"), _bdot(a, ct, "nn")


_bdot.defvjp(_bdot_fwd, _bdot_bwd)


@jax.custom_vjp
def _unit_lower_inverse(a_strict):
    return _unit_lower_inverse_raw(a_strict)


def _unit_lower_inverse_fwd(a_strict):
    t = _unit_lower_inverse_raw(a_strict)
    return t, t


def _unit_lower_inverse_bwd(t, ct):
    left = [_bdot(ti, ci, "tn") for ti, ci in zip(t, ct)]
    return (tuple(-_bdot(li, ti, "nt") for li, ti in zip(left, t)),)


_unit_lower_inverse.defvjp(_unit_lower_inverse_fwd, _unit_lower_inverse_bwd)


@jax.custom_vjp
def _saved_inverse(a_strict, t):
    return t


def _saved_inverse_fwd(a_strict, t):
    return t, t


def _saved_inverse_bwd(t, ct):
    return _unit_lower_inverse_bwd(t, ct) + (tuple(jnp.zeros_like(ti) for ti in t),)


_saved_inverse.defvjp(_saved_inverse_fwd, _saved_inverse_bwd)


def _unit_lower_inverse_raw(a_strict):
    C = a_strict[0].shape[0]
    ii = lax.broadcasted_iota(jnp.int32, (C, C), 0)
    jj = lax.broadcasted_iota(jnp.int32, (C, C), 1)
    eye = (ii == jj).astype(F32)
    blk = 16
    same = (ii // blk) == (jj // blk)
    p = [-jnp.where(same, a, 0.0) for a in a_strict]
    t = [eye + x for x in p]
    for _ in range(3):
        p = [_bdot(x, x, "nn") for x in p]
        t = [ti + _bdot(ti, x, "nn") for ti, x in zip(t, p)]
    while blk < C:
        same2 = (ii // (2 * blk)) == (jj // (2 * blk))
        off = jnp.logical_and(same2, jnp.logical_not(same))
        te = [_bdot(ti, jnp.where(off, a, 0.0), "nn") for ti, a in zip(t, a_strict)]
        t = [ti - _bdot(x, ti, "nn") for ti, x in zip(t, te)]
        same, blk = same2, 2 * blk
    return tuple(t)


def _chunk_heads(q, k, v, gc_col, gc_row, b_col, s0, t_saved=None, with_t=False):
    R = range(len(q))
    C = q[0].shape[0]
    ii = lax.broadcasted_iota(jnp.int32, (C, C), 0)
    jj = lax.broadcasted_iota(jnp.int32, (C, C), 1)
    rows = lax.broadcasted_iota(jnp.int32, (C, 1), 0)
    decay = [jnp.where(ii >= jj, jnp.exp(jnp.minimum(gc_col[h] - gc_row[h], 0.0)), 0.0) for h in R]
    kb = [k[h] * b_col[h] for h in R]
    a = [_bdot(kb[h], k[h], "nt") * decay[h] for h in R]
    qk = [_bdot(q[h], k[h], "nt") * decay[h] for h in R]
    a_strict = tuple(jnp.where(ii > jj, a[h], 0.0) for h in R)
    t = _unit_lower_inverse(a_strict) if t_saved is None else _saved_inverse(a_strict, t_saved)
    eg = [jnp.exp(gc_col[h]) for h in R]
    u = [_bdot(t[h], v[h] * b_col[h], "nn") for h in R]
    w = [_bdot(t[h], kb[h] * eg[h], "nn") for h in R]
    g_last = [jnp.sum(jnp.where(rows == C - 1, gc_col[h], 0.0), axis=0, keepdims=True) for h in R]
    kd = [k[h] * jnp.exp(g_last[h] - gc_col[h]) for h in R]
    ws = [_bdot(w[h], s0[h], "nn") for h in R]
    qs = [_bdot(q[h] * eg[h], s0[h], "nn") for h in R]
    v_new = [u[h] - ws[h] for h in R]
    o = [qs[h] + _bdot(qk[h], v_new[h], "nn") for h in R]
    s1 = [s0[h] * jnp.exp(g_last[h]) + _bdot(kd[h], v_new[h], "tn") for h in R]
    return (tuple(o), tuple(s1), t) if with_t else (tuple(o), tuple(s1))


def _pick_lane(a, h):
    lanes = lax.broadcasted_iota(jnp.int32, a.shape, 1)
    return jnp.sum(jnp.where(lanes == h, a, 0.0), axis=1, keepdims=True)


def _pick_row(a, h):
    rows = lax.broadcasted_iota(jnp.int32, a.shape, 0)
    return jnp.sum(jnp.where(rows == h, a, 0.0), axis=0, keepdims=True)


def _tri(C):
    ii = lax.broadcasted_iota(jnp.int32, (C, C), 0)
    jj = lax.broadcasted_iota(jnp.int32, (C, C), 1)
    return (ii >= jj).astype(F32)


def _delta_fwd(qkvn, g_pad, g_rows, beta_pad):
    S, W = qkvn.shape
    D = W // 3
    H = D // HEAD_DIM
    C = min(CHUNK, S)
    N = S // C

    def body(x_ref, gp_ref, gr_ref, bp_ref, o_ref, sall_ref, tall_ref, st):
        n = pl.program_id(0)

        @pl.when(n == 0)
        def _():
            st[...] = jnp.zeros_like(st)

        low = _tri(C)
        gc_cols = _tri_dot(low, gp_ref[...], "nn", 0)
        gc_rows = _tri_dot(gr_ref[...], low, "nt", 1)
        bcols = bp_ref[...]
        hs = range(H)
        s0 = tuple(st[h] for h in hs)
        for h in hs:
            sall_ref[h] = s0[h]
        o, s1, t = _chunk_heads(
            tuple(x_ref[:, h * HEAD_DIM : (h + 1) * HEAD_DIM] for h in hs),
            tuple(x_ref[:, D + h * HEAD_DIM : D + (h + 1) * HEAD_DIM] for h in hs),
            tuple(x_ref[:, 2 * D + h * HEAD_DIM : 2 * D + (h + 1) * HEAD_DIM] for h in hs),
            tuple(_pick_lane(gc_cols, h) for h in hs), tuple(_pick_row(gc_rows, h) for h in hs),
            tuple(_pick_lane(bcols, h) for h in hs), s0, with_t=True,
        )
        for h in hs:
            st[h] = s1[h]
            o_ref[:, h * HEAD_DIM : (h + 1) * HEAD_DIM] = o[h]
            tall_ref[h] = t[h].astype(tall_ref.dtype)

    return pl.pallas_call(
        body,
        name="gdn_delta_fwd",
        grid=(N,),
        in_specs=[
            pl.BlockSpec((C, W), lambda n: (n, 0)),
            pl.BlockSpec((C, LANES), lambda n: (n, 0)),
            pl.BlockSpec((None, 8, C), lambda n: (n, 0, 0)),
            pl.BlockSpec((C, LANES), lambda n: (n, 0)),
        ],
        out_specs=[pl.BlockSpec((C, D), lambda n: (n, 0)), pl.BlockSpec((None, H, HEAD_DIM, HEAD_DIM), lambda n: (n, 0, 0, 0)),
                   pl.BlockSpec((None, H, C, C), lambda n: (n, 0, 0, 0))],
        out_shape=[jax.ShapeDtypeStruct((S, D), F32), jax.ShapeDtypeStruct((N, H, HEAD_DIM, HEAD_DIM), F32), jax.ShapeDtypeStruct((N, H, C, C), BF16)],
        scratch_shapes=[pltpu.VMEM((H, HEAD_DIM, HEAD_DIM), F32)],
        compiler_params=_params(1),
    )(qkvn, g_pad, g_rows, beta_pad)


def _delta_bwd(qkvn, g_pad, g_rows, beta_pad, s_all, t_all, do):
    S, W = qkvn.shape
    D = W // 3
    H = D // HEAD_DIM
    C = min(CHUNK, S)
    N = S // C

    def body(x_ref, gp_ref, gr_ref, bp_ref, sall_ref, tall_ref, do_ref, dx_ref, dgp_ref, dgr_ref, dbp_ref, dst):
        n = pl.program_id(0)

        @pl.when(n == 0)
        def _():
            dst[...] = jnp.zeros_like(dst)

        low = _tri(C)
        gc_cols = _tri_dot(low, gp_ref[...], "nn", 0)
        gc_rows = _tri_dot(gr_ref[...], low, "nt", 1)
        bcols = bp_ref[...]
        lane = lax.broadcasted_iota(jnp.int32, (1, LANES), 1)
        row8 = lax.broadcasted_iota(jnp.int32, (8, 1), 0)
        dgc_cols = jnp.zeros((C, LANES), F32)
        dgc_rows = jnp.zeros((8, C), F32)
        dbcols = jnp.zeros((C, LANES), F32)
        hs = range(H)
        t_saved = tuple(tall_ref[h].astype(F32) for h in hs)
        _, vjp = jax.vjp(
            lambda *args: _chunk_heads(*args, t_saved=t_saved),
            tuple(x_ref[:, h * HEAD_DIM : (h + 1) * HEAD_DIM] for h in hs),
            tuple(x_ref[:, D + h * HEAD_DIM : D + (h + 1) * HEAD_DIM] for h in hs),
            tuple(x_ref[:, 2 * D + h * HEAD_DIM : 2 * D + (h + 1) * HEAD_DIM] for h in hs),
            tuple(_pick_lane(gc_cols, h) for h in hs), tuple(_pick_row(gc_rows, h) for h in hs),
            tuple(_pick_lane(bcols, h) for h in hs), tuple(sall_ref[h] for h in hs),
        )
        dq, dk, dv, dgc, dgr, dbc, ds0 = vjp((tuple(do_ref[:, h * HEAD_DIM : (h + 1) * HEAD_DIM] for h in hs), tuple(dst[h] for h in hs)))
        for h in hs:
            dst[h] = ds0[h]
            dx_ref[:, h * HEAD_DIM : (h + 1) * HEAD_DIM] = dq[h]
            dx_ref[:, D + h * HEAD_DIM : D + (h + 1) * HEAD_DIM] = dk[h]
            dx_ref[:, 2 * D + h * HEAD_DIM : 2 * D + (h + 1) * HEAD_DIM] = dv[h]
            dgc_cols = dgc_cols + dgc[h] * (lane == h).astype(F32)
            dgc_rows = dgc_rows + dgr[h] * (row8 == h).astype(F32)
            dbcols = dbcols + dbc[h] * (lane == h).astype(F32)
        dgp_ref[...] = _tri_dot(low, dgc_cols, "tn", 0)
        dgr_ref[...] = _tri_dot(dgc_rows, low, "nn", 1)
        dbp_ref[...] = dbcols

    rev = lambda n: N - 1 - n
    return pl.pallas_call(
        body,
        name="gdn_delta_bwd",
        grid=(N,),
        in_specs=[
            pl.BlockSpec((C, W), lambda n: (rev(n), 0)),
            pl.BlockSpec((C, LANES), lambda n: (rev(n), 0)),
            pl.BlockSpec((None, 8, C), lambda n: (rev(n), 0, 0)),
            pl.BlockSpec((C, LANES), lambda n: (rev(n), 0)),
            pl.BlockSpec((None, H, HEAD_DIM, HEAD_DIM), lambda n: (rev(n), 0, 0, 0)),
            pl.BlockSpec((None, H, C, C), lambda n: (rev(n), 0, 0, 0)),
            pl.BlockSpec((C, D), lambda n: (rev(n), 0)),
        ],
        out_specs=[
            pl.BlockSpec((C, W), lambda n: (rev(n), 0)),
            pl.BlockSpec((C, LANES), lambda n: (rev(n), 0)),
            pl.BlockSpec((None, 8, C), lambda n: (rev(n), 0, 0)),
            pl.BlockSpec((C, LANES), lambda n: (rev(n), 0)),
        ],
        out_shape=[
            jax.ShapeDtypeStruct((S, W), F32),
            jax.ShapeDtypeStruct((S, LANES), F32),
            jax.ShapeDtypeStruct((N, 8, C), F32),
            jax.ShapeDtypeStruct((S, LANES), F32),
        ],
        scratch_shapes=[pltpu.VMEM((H, HEAD_DIM, HEAD_DIM), F32)],
        compiler_params=_params(1),
    )(qkvn, g_pad, g_rows, beta_pad, s_all, t_all, do)


def _gate_norm_head(o, z, nw):
    return o * lax.rsqrt(jnp.mean(o * o, axis=-1, keepdims=True) + RMS_EPS) * nw * _silu(z)


def _gate_norm_fwd(o, z, nw):
    S, D = o.shape
    H = D // HEAD_DIM
    ts = min(ROW_TILE, S)

    def fn(i, n, rv, hv, cv):
        ov, zv = rv
        parts = [_gate_norm_head(ov[:, h * HEAD_DIM : (h + 1) * HEAD_DIM], zv[:, h * HEAD_DIM : (h + 1) * HEAD_DIM], cv[0]) for h in range(H)]
        return (jnp.concatenate(parts, axis=1),), ()

    return _rowwise("gdn_gate_norm_fwd", fn, S, ts, rows=[o, z], consts=[nw], outs=[(D, BF16)])[0]


def _gate_norm_bwd(dog, o, z, nw):
    S, D = o.shape
    H = D // HEAD_DIM
    ts = min(ROW_TILE, S)

    def fn(i, n, rv, hv, cv):
        dv, ov, zv = rv
        dos, dzs, dnw = [], [], None
        for h in range(H):
            sl = slice(h * HEAD_DIM, (h + 1) * HEAD_DIM)
            _, vjp = jax.vjp(_gate_norm_head, ov[:, sl], zv[:, sl], cv[0])
            a, b_, c_ = vjp(dv[:, sl])
            dos.append(a)
            dzs.append(b_)
            dnw = c_ if dnw is None else dnw + c_
        return (jnp.concatenate(dos, axis=1), jnp.concatenate(dzs, axis=1)), (dnw,)

    return _rowwise("gdn_gate_norm_bwd", fn, S, ts, rows=[dog, o, z], consts=[nw], outs=[(D, F32), (D, BF16)], accs=[(1, HEAD_DIM)])


def _square_bf16(r):
    rf = r.astype(F32)
    return rf * rf


def _mlp_ple_dw(li, dh, dhb, xa, p, r, gate, pp, w2):
    dpre = _mm(f"l{li}_mlp_down_bwd", dhb, w2, "nt", [BF16], epi=lambda acc, rr: (acc * (2.0 * rr.astype(F32)),), extras=[(r, "tile")], tm=1024, tn=1024, b_outer=True)
    dw2 = _mm(f"l{li}_mlp_dw2", r, dhb, "tn", [BF16], a_fn=_square_bf16, **DW_TILES)
    dgate, dpp, dbg = _ple_bwd(f"l{li}_ple_bwd", dh, gate, pp)
    dw1 = _mm(f"l{li}_mlp_dw1", xa, dpre, "tn", [BF16], out_blocks=N_DEV, **DW_TILES)
    dwg = _mm(f"l{li}_ple_dwg", xa, dgate, "tn", [BF16], **DW_TILES)
    dwp = _mm(f"l{li}_ple_dwp", p, dpp, "tn", [BF16], out_blocks=N_DEV, **DW_TILES)
    rows = lambda a: a.reshape((N_DEV, a.shape[0] // N_DEV) + a.shape[1:])
    return dpre, dgate, dw1, rows(dw2), rows(dwg), dbg, dwp


def _mlp_ple_dx(li, dh, dpre, dgate, w1, wg, after, xhat, rstd, g):
    t = _mm(f"l{li}_ple_gate_bwd", dgate, wg, "nt", [F32], epi=lambda acc, d: (acc + ALPHA * d,), extras=[(dh, "tile")], tm=1024, after=after)
    return _mm(f"l{li}_mlp_up_bwd", dpre, w1, "nt", [F32, BF16], epi=_ln_bwd_epi, extras=[(t, "tile"), (xhat, "tile"), (rstd, "rows"), (g, "row")],
               tm=256, tk=4096, accs=2)


def _local_step(x, p, tgt, W, fetch, emit):
    S, D = x.shape
    H = D // HEAD_DIM
    C = min(CHUNK, S)
    N = S // C
    lg = lambda i, j: W["ln_gain"][2 * i + j][None, :]
    lb = lambda i, j: W["ln_bias"][2 * i + j][None, :]
    G = {}

    pooled, xh0a, rs0a, x0a = _pool_fwd(x, W["pool_w"], W["pool_b"], W["pool_scale"], lg(0, 0), lb(0, 0))
    w0a = fetch("l0a", x0a)
    r0 = _mm("l0_mlp_up", x0a, w0a["mlp_w1"], "nn", [BF16], epi=lambda acc: (jnp.maximum(acc, 0.0),), tm=1024, tn=1024, b_outer=True, after=w0a.get("_after", ()))
    w0b = fetch("l0b", r0)
    gate0 = _mm("l0_ple_gate", x0a, w0b["ple_gate_w"], "nn", [F32], epi=lambda acc, bias: (acc + bias,), extras=[(W["ple_gate_b"][0:1], "row")], tm=1024)
    pp0 = _mm("l0_ple_proj", p[0], w0b["ple_proj"], "nn", [F32])
    w0c = fetch("l0c", pp0)
    ff0 = _mm("l0_mlp_down", r0, w0c["mlp_w2"], "nn", [F32], a_fn=_square_bf16, tm=256, tk=4096, after=w0c.get("_after", ()))
    xh0b, rs0b, x0b = _res_ln_ffpe("l0_ln_b", xh0a, ff0, gate0, pp0, lg(0, 0), lb(0, 0), lg(0, 1), lb(0, 1))

    wg_ = fetch("gdn", x0b)
    qkv_pre = _mm("gdn_in_qkv", x0b, wg_["gdn_wqkv"], "nn", [F32], tm=1024, tn=1024, b_outer=True, after=wg_.get("_after", ()))
    z = _mm("gdn_in_z", x0b, wg_["gdn_wz"], "nn", [F32], tm=1024)
    ba = _mm("gdn_in_ba", x0b, wg_["gdn_wba"], "nn", [F32])
    qkvn = _conv_fwd(qkv_pre, W["gdn_conv"])
    beta_pad, g_pad = _gates_fwd(ba, W["gdn_a_log"], W["gdn_dt_bias"])
    g_rows = g_pad[:, :8].reshape(N, C, 8).transpose(0, 2, 1)
    o, s_all, t_all = _delta_fwd(qkvn, g_pad, g_rows, beta_pad)
    og = _gate_norm_fwd(o, z, W["gdn_norm_w"])
    mix1 = _mm("gdn_out", og, wg_["gdn_w_out"], "nn", [F32], tm=1024)
    xh1a, rs1a, x1a = _res_ln_mix("l1_ln_a", xh0b, mix1, lg(0, 1), lb(0, 1), lg(1, 0), lb(1, 0))
    w1_ = fetch("l1", x1a)
    r1 = _mm("l1_mlp_up", x1a, w1_["mlp_w1"], "nn", [BF16], epi=lambda acc: (jnp.maximum(acc, 0.0),), tm=1024, tn=1024, b_outer=True)
    ff1 = _mm("l1_mlp_down", r1, w1_["mlp_w2"], "nn", [F32], a_fn=_square_bf16, tm=256, tk=4096)
    gate1 = _mm("l1_ple_gate", x1a, w1_["ple_gate_w"], "nn", [F32], epi=lambda acc, bias: (acc + bias,), extras=[(W["ple_gate_b"][1:2], "row")], tm=1024)
    pp1 = _mm("l1_ple_proj", p[1], w1_["ple_proj"], "nn", [F32])
    dh1b, dh1b_b, loss_cols, dg11, db11 = _final_ln_loss(xh1a, ff1, gate1, pp1, tgt, lg(1, 0), lb(1, 0), lg(1, 1), lb(1, 1))

    dpre1, dgate1, dw1_1, dw2_1, dwg_1, dbg_1, dwp_1 = _mlp_ple_dw(1, dh1b, dh1b_b, x1a, p[1], r1, gate1, pp1, w1_["mlp_w2"])
    tok = emit("l1", {"mlp_w1": dw1_1, "mlp_w2": dw2_1, "ple_gate_w": dwg_1, "ple_proj": dwp_1})
    dh1a, dh1a_b, dg10, db10 = _mlp_ple_dx(1, dh1b, dpre1, dgate1, w1_["mlp_w1"], w1_["ple_gate_w"], [tok], xh1a, rs1a, lg(1, 0))
    dog = _mm("gdn_out_bwd", dh1a_b, wg_["gdn_w_out"], "nt", [F32], tm=1024)
    dw_out = _mm("gdn_dw_out", og, dh1a_b, "tn", [BF16], **DW_TILES)
    dw_out = dw_out.reshape((N_DEV, dw_out.shape[0] // N_DEV) + dw_out.shape[1:])
    do, dz, dnw = _gate_norm_bwd(dog, o, z, W["gdn_norm_w"])
    dqkvn, dg_col, dg_row, dbeta = _delta_bwd(qkvn, g_pad, g_rows, beta_pad, s_all, t_all, do)
    dg_all = dg_col + jnp.pad(dg_row.transpose(0, 2, 1).reshape(S, 8), ((0, 0), (0, LANES - 8)))
    dba, dalog, ddt = _gates_bwd(ba, W["gdn_a_log"], W["gdn_dt_bias"], dbeta, dg_all, H)
    dqkv, dconv = _conv_bwd(qkv_pre, W["gdn_conv"], dqkvn)
    dwqkv = _mm("gdn_dwqkv", x0b, dqkv, "tn", [F32], **DW_TILES)
    dwz = _mm("gdn_dwz", x0b, dz, "tn", [F32], **DW_TILES)
    dwba = _mm("gdn_dwba", x0b, dba, "tn", [F32], **DW_TILES)
    dw_in = jnp.concatenate([dwqkv, dwz, dwba[:, :H], dwba[:, LANES : LANES + H]], axis=1)
    tok = emit("gdn", {"gdn_w_in": _split_blocks("gdn_w_in", dw_in).astype(BF16), "gdn_w_out": dw_out})
    t = _mm("gdn_in_ba_bwd", dba, wg_["gdn_wba"], "nt", [F32], epi=lambda acc, d: (acc + ALPHA * d,), extras=[(dh1a, "tile")], after=[tok])
    t = _mm("gdn_in_z_bwd", dz, wg_["gdn_wz"], "nt", [F32], epi=lambda acc, d: (acc + d,), extras=[(t, "tile")], tm=1024)
    dh0b, dh0b_b, dg01, db01 = _mm("gdn_in_qkv_bwd", dqkv, wg_["gdn_wqkv"], "nt", [F32, BF16], epi=_ln_bwd_epi,
                                    extras=[(t, "tile"), (xh0b, "tile"), (rs0b, "rows"), (lg(0, 1), "row")], tm=256, tk=3072, accs=2)

    dpre0, dgate0, dw1_0, dw2_0, dwg_0, dbg_0, dwp_0 = _mlp_ple_dw(0, dh0b, dh0b_b, x0a, p[0], r0, gate0, pp0, w0c["mlp_w2"])
    tok = emit("l0", {"mlp_w1": dw1_0, "mlp_w2": dw2_0, "ple_gate_w": dwg_0, "ple_proj": dwp_0})
    dh0a, _, dg00, db00 = _mlp_ple_dx(0, dh0b, dpre0, dgate0, w0a["mlp_w1"], w0b["ple_gate_w"], [tok], xh0a, rs0a, lg(0, 0))
    grad_x, dyp, dscale, dpb = _pool_bwd(dh0a, pooled, W["pool_w"], W["pool_b"], W["pool_scale"])
    G["pool_w"] = _pool_dw(pooled, dyp)

    G["ln_gain"] = jnp.concatenate([dg00, dg01, dg10, dg11], axis=0)
    G["ln_bias"] = jnp.concatenate([db00, db01, db10, db11], axis=0)
    G["pool_b"] = dpb
    G["pool_scale"] = dscale
    G["gdn_conv"] = dconv
    G["gdn_a_log"] = dalog[:, :H]
    G["gdn_dt_bias"] = ddt[:, :H]
    G["gdn_norm_w"] = dnw
    G["ple_gate_b"] = jnp.concatenate([dbg_0, dbg_1], axis=0)
    return loss_cols, grad_x, G


_HBM = pl.BlockSpec(memory_space=pltpu.HBM)


def _all_gather(name, shards):
    T = len(shards)

    def body(*refs):
        ins, outs = refs[:T], refs[T : 2 * T]
        send_sems, recv_sems, local_sems = refs[2 * T :]
        x, y, c = lax.axis_index("x"), lax.axis_index("y"), lax.axis_index("c")
        me, sibling = (x, y, c), (x, y, 1 - c)
        chips = [(1 - x, y), (x, 1 - y), (1 - x, 1 - y)]

        def blk(t, px, py, pc):
            return outs[t].at[4 * px + 2 * py + pc]

        def copy(t, k, block, to, src=None):
            return pltpu.make_async_remote_copy(
                src_ref=blk(t, *block) if src is None else src, dst_ref=blk(t, *block),
                send_sem=send_sems.at[t, k], recv_sem=recv_sems.at[t, k], device_id=to, device_id_type=MESH,
            )

        mine = [pltpu.make_async_copy(ins[t], blk(t, *me), local_sems.at[t]) for t in range(T)]
        for cp in mine:
            cp.start()
        first = []
        for t in range(T):
            first.append(copy(t, 0, me, sibling, src=ins[t]))
            first += [copy(t, 1 + j, me, (*chip, c), src=ins[t]) for j, chip in enumerate(chips)]
        for cp in first:
            cp.start()
        passed = []
        for j, chip in enumerate(chips):
            for t in range(T):
                copy(t, 1 + j, (*chip, c), me).wait_recv()
                fw = copy(t, 4 + j, (*chip, c), sibling)
                fw.start()
                passed.append(fw)
        for t in range(T):
            copy(t, 0, sibling, me).wait_recv()
            for j, chip in enumerate(chips):
                copy(t, 4 + j, (*chip, 1 - c), me).wait_recv()
        for cp in first + passed:
            cp.wait_send()
        for cp in mine:
            cp.wait()

    return pl.pallas_call(
        body,
        name=name,
        in_specs=[_HBM] * T,
        out_specs=[_HBM] * T,
        out_shape=[jax.ShapeDtypeStruct((N_DEV,) + s.shape, s.dtype) for s in shards],
        scratch_shapes=[pltpu.SemaphoreType.DMA((T, 7)), pltpu.SemaphoreType.DMA((T, 7)), pltpu.SemaphoreType.DMA((T,))],
    )(*shards)


def _exchange(name, blocks):
    def body(g_ref, o_ref, send_sems, recv_sems, local_sem):
        x, y, c = lax.axis_index("x"), lax.axis_index("y"), lax.axis_index("c")
        own = pltpu.make_async_copy(g_ref.at[4 * x + 2 * y + c], o_ref.at[N_DEV - 1], local_sem)
        own.start()
        copies = []
        for rel in range(1, N_DEV):
            px = 1 - x if rel & 4 else x
            py = 1 - y if rel & 2 else y
            pc = 1 - c if rel & 1 else c
            copies.append(
                pltpu.make_async_remote_copy(
                    src_ref=g_ref.at[4 * px + 2 * py + pc], dst_ref=o_ref.at[rel - 1],
                    send_sem=send_sems.at[rel - 1], recv_sem=recv_sems.at[rel - 1], device_id=(px, py, pc), device_id_type=MESH,
                )
            )
        for cp in copies:
            cp.start()
        for cp in copies:
            cp.wait_recv()
        for cp in copies:
            cp.wait_send()
        own.wait()

    return pl.pallas_call(
        body,
        name=name,
        in_specs=[_HBM],
        out_specs=_HBM,
        out_shape=jax.ShapeDtypeStruct(blocks.shape, blocks.dtype),
        scratch_shapes=[pltpu.SemaphoreType.DMA((N_DEV - 1,)), pltpu.SemaphoreType.DMA((N_DEV - 1,)), pltpu.SemaphoreType.DMA],
    )(blocks)


_SEM = pl.BlockSpec(memory_space=pltpu.SEMAPHORE)
_ANY = pl.BlockSpec(memory_space=pl.ANY)
_DATAFLOW = pltpu.SideEffectType.DATAFLOW_SIDE_EFFECTING
N_PEERS = N_DEV - 1


def _peer(rel, x, y, c):
    return (1 - x if rel & 4 else x, 1 - y if rel & 2 else y, 1 - c if rel & 1 else c)


def _send_start(name, srcs, lands, gather, after):
    T = len(srcs)

    def body(*refs):
        src_refs, land_refs = refs[:T], refs[T : 2 * T]
        send_sems, recv_sems = refs[2 * T + 1], refs[2 * T + 2]
        token = refs[-1]
        x, y, c = lax.axis_index("x"), lax.axis_index("y"), lax.axis_index("c")
        for t in range(T):
            for rel in range(1, N_DEV):
                px, py, pc = _peer(rel, x, y, c)
                pltpu.make_async_remote_copy(
                    src_ref=src_refs[t] if gather else src_refs[t].at[4 * px + 2 * py + pc],
                    dst_ref=land_refs[t].at[4 * x + 2 * y + c] if gather else land_refs[t].at[rel - 1],
                    send_sem=send_sems.at[t * N_PEERS + rel - 1], recv_sem=recv_sems.at[t * N_PEERS + rel - 1], device_id=(px, py, pc), device_id_type=MESH,
                ).start()
        token[...] = jnp.zeros_like(token)

    hbm = lambda a: pltpu.HBM(a.shape, a.dtype)
    return pl.pallas_call(
        body,
        name=name,
        out_shape=(pltpu.SemaphoreType.DMA((T * N_PEERS,)), pltpu.SemaphoreType.DMA((T * N_PEERS,)), *[hbm(a) for a in srcs],
                   *[hbm(a) for a in lands], jax.ShapeDtypeStruct((8, LANES), F32)),
        in_specs=(_HBM,) * (2 * T) + (_ANY,),
        out_specs=(_SEM, _SEM) + (_HBM,) * (2 * T) + (pl.BlockSpec(memory_space=pltpu.VMEM),),
        input_output_aliases={t: 2 + t for t in range(2 * T)},
        compiler_params=pltpu.CompilerParams(has_side_effects=_DATAFLOW),
    )(*[pltpu.with_memory_space_constraint(a, pltpu.HBM) for a in list(srcs) + list(lands)], after)


def _send_wait(name, started, after, gather):
    T = (len(started) - 3) // 2
    send_sems, recv_sems, token = started[0], started[1], started[-1]
    thru = started[2:-1]

    def body(*refs):
        src_refs, land_refs = refs[:T], refs[T : 2 * T]
        send_sems, recv_sems = refs[2 * T], refs[2 * T + 1]
        x, y, c = lax.axis_index("x"), lax.axis_index("y"), lax.axis_index("c")
        for t in range(T):
            for rel in range(1, N_DEV):
                cp = pltpu.make_async_remote_copy(
                    src_ref=src_refs[t] if gather else src_refs[t].at[0], dst_ref=land_refs[t].at[0],
                    send_sem=send_sems.at[t * N_PEERS + rel - 1], recv_sem=recv_sems.at[t * N_PEERS + rel - 1], device_id=_peer(rel, x, y, c), device_id_type=MESH,
                )
                cp.wait_send()
                cp.wait_recv()

    outs = pl.pallas_call(
        body,
        name=name,
        out_shape=tuple(pltpu.HBM(a.shape, a.dtype) for a in thru),
        in_specs=(_HBM,) * (2 * T) + (_SEM, _SEM, _ANY),
        out_specs=(_HBM,) * (2 * T),
        input_output_aliases={t: t for t in range(2 * T)},
        compiler_params=pltpu.CompilerParams(has_side_effects=_DATAFLOW),
    )(*thru, send_sems, recv_sems, after)
    return list(outs[:T]), list(outs[T:])


def _sum_blocks(name, parts, tr):
    _, R, Cw = parts[0].shape
    tr = tr if R % tr == 0 else R

    def body(*refs):
        acc = None
        for p_ref in refs[:-1]:
            for d in range(p_ref.shape[0]):
                v = p_ref[d].astype(F32)
                acc = v if acc is None else acc + v
        refs[-1][...] = acc

    return pl.pallas_call(
        body,
        name=name,
        grid=(R // tr,),
        in_specs=[pl.BlockSpec((a.shape[0], tr, Cw), lambda i: (0, i, 0)) for a in parts],
        out_specs=pl.BlockSpec((tr, Cw), lambda i: (i, 0)),
        out_shape=jax.ShapeDtypeStruct((R, Cw), F32),
        compiler_params=_params(1),
    )(*parts)


def _adamw(name, w, g, m, v):
    shape = w.shape
    cols = shape[-1]
    rows = w.size // cols
    tr = rows if rows <= 512 else 512
    assert rows % tr == 0
    w2, g2, m2, v2 = (a.reshape(rows, cols) for a in (w, g, m, v))

    def body(w_ref, g_ref, m_ref, v_ref, d_ref, mo_ref, vo_ref):
        gv = g_ref[...]
        mn = ADAM_B1 * m_ref[...] + (1.0 - ADAM_B1) * gv
        vn = ADAM_B2 * v_ref[...] + (1.0 - ADAM_B2) * jnp.square(gv)
        m_hat = mn / (1.0 - ADAM_B1**ADAM_STEP)
        v_hat = vn / (1.0 - ADAM_B2**ADAM_STEP)
        d_ref[...] = -ADAM_LR * (m_hat / (jnp.sqrt(v_hat) + ADAM_EPS) + ADAM_WD * w_ref[...])
        mo_ref[...] = mn
        vo_ref[...] = vn

    spec = pl.BlockSpec((tr, cols), lambda i: (i, 0))
    d, mn, vn = pl.pallas_call(
        body,
        name=name,
        grid=(rows // tr,),
        in_specs=[spec] * 4,
        out_specs=[spec] * 3,
        out_shape=[jax.ShapeDtypeStruct((rows, cols), F32)] * 3,
        compiler_params=_params(1),
    )(w2, g2, m2, v2)
    return d.reshape(shape), mn.reshape(shape), vn.reshape(shape)


SMALL_SHARDED = ("ln_gain", "ln_bias", "pool_b", "gdn_conv")
SMALL_REPLICATED = ("pool_scale", "gdn_a_log", "gdn_dt_bias", "gdn_norm_w", "ple_gate_b")
WEIGHTS = ("ln_gain", "ln_bias", "pool_w", "pool_b", "pool_scale", "gdn_w_in", "gdn_conv", "gdn_a_log", "gdn_dt_bias",
           "gdn_norm_w", "gdn_w_out", "mlp_w1", "mlp_w2", "ple_gate_w", "ple_gate_b", "ple_proj")
BIG_AXIS = {"gdn_w_in": 1, "gdn_w_out": 0, "mlp_w1": 1, "mlp_w2": 0, "ple_gate_w": 0, "ple_proj": 1, "pool_w": 1}
GATHER_GROUPS = {
    "l0a": (("mlp_w1", 0),),
    "l0b": (("ple_gate_w", 0), ("ple_proj", 0)),
    "l0c": (("mlp_w2", 0),),
    "gdn": (("gdn_w_in", 0), ("gdn_w_out", 0)),
    "l1": (("mlp_w1", 1), ("mlp_w2", 1), ("ple_gate_w", 1), ("ple_proj", 1)),
}
GATHER_AFTER = {"l0b": "l0a", "l0c": "l0a", "gdn": "l0c", "l1": "gdn"}
GRAD_GROUPS = {
    "l1": (("mlp_w1", 1), ("mlp_w2", 1), ("ple_gate_w", 1), ("ple_proj", 1)),
    "gdn": (("gdn_w_in", 0), ("gdn_w_out", 0)),
    "l0": (("mlp_w1", 0), ("mlp_w2", 0), ("ple_gate_w", 0), ("ple_proj", 0)),
}
PACK_PART_ALIGN = 16
SUM_TILE = 128


def _part_rows(a, width):
    rows = a.size // width
    return rows + (-rows) % PACK_PART_ALIGN


def _pack_rows(parts, width, dtype, align):
    padded = []
    for a in parts:
        a2 = a.reshape(-1, width).astype(dtype)
        padded.append(jnp.pad(a2, ((0, _part_rows(a, width) - a2.shape[0]), (0, 0))))
    flat = jnp.concatenate(padded, axis=0)
    return jnp.pad(flat, ((0, (-flat.shape[0]) % align), (0, 0)))


def _pack_blocks(parts, width, dtype, align):
    padded = []
    for a in parts:
        a2 = a.reshape(a.shape[0], -1, width).astype(dtype)
        padded.append(jnp.pad(a2, ((0, 0), (0, _part_rows(a[0], width) - a2.shape[1]), (0, 0))))
    flat = jnp.concatenate(padded, axis=1)
    return jnp.pad(flat, ((0, 0), (0, (-flat.shape[1]) % align), (0, 0)))


def _unpack_rows(packed, shapes, width):
    out, off = [], 0
    for shp in shapes:
        size = 1
        for d in shp:
            size *= d
        out.append(packed[..., off : off + size // width, :].reshape(packed.shape[:-2] + tuple(shp)))
        off += size // width + (-(size // width)) % PACK_PART_ALIGN
    return out


def _split_blocks(name, full):
    ax = BIG_AXIS[name]
    shp = full.shape
    a = full.reshape(shp[:ax] + (N_DEV, shp[ax] // N_DEV) + shp[ax + 1 :])
    return jnp.moveaxis(a, ax, 0)


def _join_blocks(name, blocks):
    ax = BIG_AXIS[name]
    a = jnp.moveaxis(blocks, 0, ax)
    shp = a.shape
    return a.reshape(shp[:ax] + (shp[ax] * shp[ax + 1],) + shp[ax + 2 :])


def _pack_small(parts):
    flat = jnp.concatenate([jnp.pad(a.reshape(-1), (0, (-a.size) % LANES)) for a in parts])
    rows = flat.size // LANES
    return jnp.pad(flat.reshape(rows, LANES), ((0, (-rows) % 8), (0, 0)))


def _unpack_small(packed, shapes):
    flat = packed.reshape(packed.shape[:-2] + (-1,))
    out, off = [], 0
    for shp in shapes:
        size = 1
        for s in shp:
            size *= s
        out.append(flat[..., off : off + size].reshape(flat.shape[:-1] + tuple(shp)))
        off += size + (-size) % LANES
    return out


def _split_w_in(w_in, D, H):
    pad = lambda a: jnp.pad(a, ((0, 0), (0, LANES - H)))
    return w_in[:, : 3 * D], w_in[:, 3 * D : 4 * D], jnp.concatenate([pad(w_in[:, 4 * D : 4 * D + H]), pad(w_in[:, 4 * D + H :])], axis=1)


def kernel(x, p, ln_gain, ln_bias, pool_w, pool_b, pool_scale, gdn_w_in, gdn_conv, gdn_a_log, gdn_dt_bias, gdn_norm_w, gdn_w_out, mlp_w1, mlp_w2, ple_gate_w, ple_gate_b, ple_proj, loss_target, m_ln_gain, m_ln_bias, m_pool_w, m_pool_b, m_pool_scale, m_gdn_w_in, m_gdn_conv, m_gdn_a_log, m_gdn_dt_bias, m_gdn_norm_w, m_gdn_w_out, m_mlp_w1, m_mlp_w2, m_ple_gate_w, m_ple_gate_b, m_ple_proj, v_ln_gain, v_ln_bias, v_pool_w, v_pool_b, v_pool_scale, v_gdn_w_in, v_gdn_conv, v_gdn_a_log, v_gdn_dt_bias, v_gdn_norm_w, v_gdn_w_out, v_mlp_w1, v_mlp_w2, v_ple_gate_w, v_ple_gate_b, v_ple_proj):
    w_sh = dict(ln_gain=ln_gain, ln_bias=ln_bias, pool_w=pool_w, pool_b=pool_b, pool_scale=pool_scale, gdn_w_in=gdn_w_in,
                gdn_conv=gdn_conv, gdn_a_log=gdn_a_log, gdn_dt_bias=gdn_dt_bias, gdn_norm_w=gdn_norm_w, gdn_w_out=gdn_w_out,
                mlp_w1=mlp_w1, mlp_w2=mlp_w2, ple_gate_w=ple_gate_w, ple_gate_b=ple_gate_b, ple_proj=ple_proj)
    m_sh = dict(ln_gain=m_ln_gain, ln_bias=m_ln_bias, pool_w=m_pool_w, pool_b=m_pool_b, pool_scale=m_pool_scale, gdn_w_in=m_gdn_w_in,
                gdn_conv=m_gdn_conv, gdn_a_log=m_gdn_a_log, gdn_dt_bias=m_gdn_dt_bias, gdn_norm_w=m_gdn_norm_w, gdn_w_out=m_gdn_w_out,
                mlp_w1=m_mlp_w1, mlp_w2=m_mlp_w2, ple_gate_w=m_ple_gate_w, ple_gate_b=m_ple_gate_b, ple_proj=m_ple_proj)
    v_sh = dict(ln_gain=v_ln_gain, ln_bias=v_ln_bias, pool_w=v_pool_w, pool_b=v_pool_b, pool_scale=v_pool_scale, gdn_w_in=v_gdn_w_in,
                gdn_conv=v_gdn_conv, gdn_a_log=v_gdn_a_log, gdn_dt_bias=v_gdn_dt_bias, gdn_norm_w=v_gdn_norm_w, gdn_w_out=v_gdn_w_out,
                mlp_w1=v_mlp_w1, mlp_w2=v_mlp_w2, ple_gate_w=v_ple_gate_w, ple_gate_b=v_ple_gate_b, ple_proj=v_ple_proj)
    xs, tg = x[0], loss_target[0]
    ps = p[:, 0]
    S, D = xs.shape
    H = D // HEAD_DIM
    me = 4 * lax.axis_index("x") + 2 * lax.axis_index("y") + lax.axis_index("c")
    layer = lambda n, l: (w_sh[n][0] if n in ("gdn_w_in", "gdn_w_out") else w_sh[n][l])

    pool_packed = _pack_rows([w_sh["pool_w"][0]], D, BF16, PACK_PART_ALIGN)
    small_packed = _pack_small([w_sh[n] for n in SMALL_SHARDED])
    pool_gathered, small_gathered = _all_gather("gather_first", [pool_packed, small_packed])
    W = {"pool_w": _join_blocks("pool_w", _unpack_rows(pool_gathered, [w_sh["pool_w"][0].shape], D)[0])}

    started = {}

    def start(g, after):
        src = _pack_rows([layer(n, l) for n, l in GATHER_GROUPS[g]], D, BF16, PACK_PART_ALIGN)
        started[g] = tuple(_send_start(f"gather_{g}_start", [src], [lax.empty((N_DEV,) + src.shape, BF16)], True, after))
        return started[g][-1]

    first_token = start("l0a", small_gathered)
    smalls = _unpack_small(small_gathered, [w_sh[n].shape for n in SMALL_SHARDED])
    for n, a in zip(SMALL_SHARDED, smalls):
        W[n] = jnp.moveaxis(a, 0, -2).reshape(a.shape[1:-1] + (N_DEV * a.shape[-1],))
    W["ln_gain"] = W["ln_gain"].reshape(2 * DEPTH, D)
    W["ln_bias"] = W["ln_bias"].reshape(2 * DEPTH, D)
    W["pool_b"] = W["pool_b"].reshape(1, D) + first_token[0:1, 0:1]
    W["gdn_conv"] = W["gdn_conv"][0]
    W["pool_scale"] = pool_scale
    W["ple_gate_b"] = ple_gate_b
    W["gdn_norm_w"] = gdn_norm_w
    W["gdn_a_log"] = jnp.pad(gdn_a_log, ((0, 0), (0, LANES - H)))
    W["gdn_dt_bias"] = jnp.pad(gdn_dt_bias, ((0, 0), (0, LANES - H)))

    def fetch(g, after):
        members = GATHER_GROUPS[g]
        (src,), (land,) = _send_wait(f"gather_{g}_wait", started[g], after, True)
        tokens = [start(nxt, land) for nxt, prev in GATHER_AFTER.items() if prev == g]
        land = lax.dynamic_update_index_in_dim(land, src, me, 0)
        parts = _unpack_rows(land, [layer(n, l).shape for n, l in members], D)
        out = {n: _join_blocks(n, a) for (n, _), a in zip(members, parts)}
        if "gdn_w_in" in out:
            out["gdn_wqkv"], out["gdn_wz"], out["gdn_wba"] = _split_w_in(out.pop("gdn_w_in"), D, H)
        out["_after"] = tokens
        return out

    sent = {}

    def emit(g, grads):
        srcs = [grads[n] for n, _ in GRAD_GROUPS[g]]
        lands = [lax.empty((N_PEERS,) + a.shape[1:], BF16) for a in srcs]
        sent[g] = tuple(_send_start(f"grads_{g}_start", srcs, lands, False, srcs[0]))
        return sent[g][-1]

    loss_cols, grad_x, G = _local_step(xs, ps, tg, W, fetch, emit)
    loss = lax.psum(0.5 * jnp.sum(loss_cols) / D, MESH_AXES)

    pool_src = _pack_blocks([_split_blocks("pool_w", G["pool_w"])], D, BF16, PACK_PART_ALIGN)
    pool_sum = _sum_blocks("sum_pool_grads", [_exchange("exchange_pool_grads", pool_src)], SUM_TILE)
    grads = {"pool_w": _unpack_rows(pool_sum, [w_sh["pool_w"][0].shape], D)[0].reshape(w_sh["pool_w"].shape)}
    small_names = SMALL_SHARDED + SMALL_REPLICATED
    gs_packed = _pack_small([G[n] for n in small_names])
    (gs_all,) = _all_gather("gather_small_grads", [gs_packed])
    gs_sum = _sum_blocks("sum_small_grads", [gs_all], SUM_TILE)
    for n, a in zip(small_names, _unpack_small(gs_sum, [G[n].shape for n in small_names])):
        if n in SMALL_SHARDED:
            width = w_sh[n].shape[-1]
            a = a.reshape(w_sh[n].shape[:-1] + (N_DEV * width,))
            a = lax.dynamic_slice_in_dim(a, me * width, width, axis=a.ndim - 1)
        grads[n] = a.reshape(w_sh[n].shape)

    per_layer = {}
    for g, members in GRAD_GROUPS.items():
        srcs, lands = _send_wait(f"grads_{g}_wait", sent[g], grad_x, False)
        for (n, l), src, land in zip(members, srcs, lands):
            own = lax.dynamic_index_in_dim(src, me, 0, keepdims=True)
            as3d = lambda a: a.reshape(a.shape[0], -1, a.shape[-1])
            per_layer[(n, l)] = _sum_blocks(f"sum_grads_{n}_{l}", [as3d(land), as3d(own)], SUM_TILE).reshape(layer(n, l).shape)
    for n in ("gdn_w_in", "gdn_w_out"):
        grads[n] = per_layer[(n, 0)][None]
    for n in ("mlp_w1", "mlp_w2", "ple_gate_w", "ple_proj"):
        grads[n] = jnp.stack([per_layer[(n, 0)], per_layer[(n, 1)]])

    deltas, new_m, new_v = {}, {}, {}
    for n in WEIGHTS:
        deltas[n], new_m[n], new_v[n] = _adamw(f"adamw_{n}", w_sh[n], grads[n], m_sh[n], v_sh[n])
    return (loss, grad_x[None], *[grads[n] for n in WEIGHTS], *[deltas[n] for n in WEIGHTS],
            *[new_m[n] for n in WEIGHTS], *[new_v[n] for n in WEIGHTS])
```

```python
import functools

import jax
import jax.numpy as jnp
from jax import lax
from jax.experimental import pallas as pl
from jax.experimental.pallas import tpu as pltpu

F32 = jnp.float32
BF16 = jnp.bfloat16
MESH_AXES = ("x", "y", "c")
N_DEV = 8
MESH = pl.DeviceIdType.MESH

DEPTH = 2
ALPHA = (2.0 * DEPTH) ** 0.25
LN_EPS = 1e-5
RMS_EPS = 1e-6
L2_EPS = 1e-6
HEAD_DIM = 128
CONV_WIDTH = 4
POOL_WINDOWS = (2, 4, 8, 16)
POOL_HALO = 16
CONV_HALO = 8
LANES = 128
ADAM_LR = 0.001
ADAM_B1 = 0.9
ADAM_B2 = 0.999
ADAM_EPS = 1e-08
ADAM_WD = 0.01
ADAM_STEP = 10

VMEM_LIMIT = 56 * 1024 * 1024
ROW_TILE = 512
CONV_TILE = 256
CHUNK = 128
MM_TM, MM_TN, MM_TK = 512, 1024, 1024
DW_TILES = dict(tm=256, tn=512, tk=8192, b_outer=True)

_DIMS = {
    "nn": (((1,), (0,)), ((), ())),
    "nt": (((1,), (1,)), ((), ())),
    "tn": (((0,), (0,)), ((), ())),
}


def _params(n_axes):
    return pltpu.CompilerParams(dimension_semantics=("arbitrary",) * n_axes, vmem_limit_bytes=VMEM_LIMIT)


def _fit(tile, n):
    tile = min(tile, n)
    while n % tile:
        tile //= 2
    return tile


def _mm(name, a, b, mode, out_dtypes, epi=None, extras=(), a_fn=None, tm=None, tn=None, tk=None, b_outer=False, after=(), out_blocks=1, accs=0):
    if mode == "tn":
        K, M = a.shape
    else:
        M, K = a.shape
    N = b.shape[0] if mode == "nt" else b.shape[1]
    tm, tn, tk = _fit(tm or MM_TM, M), (N // out_blocks if out_blocks > 1 else N if accs else _fit(tn or MM_TN, N)), _fit(tk or MM_TK, K)
    nk = K // tk

    def at(f):
        return (lambda j, i, k: f(i, j, k)) if b_outer else f

    a_spec = pl.BlockSpec((tk, tm), at(lambda i, j, k: (k, i))) if mode == "tn" else pl.BlockSpec((tm, tk), at(lambda i, j, k: (i, k)))
    b_spec = pl.BlockSpec((tn, tk), at(lambda i, j, k: (j, k))) if mode == "nt" else pl.BlockSpec((tk, tn), at(lambda i, j, k: (k, j)))
    ex_spec = {"tile": pl.BlockSpec((tm, tn), at(lambda i, j, k: (i, j))), "row": pl.BlockSpec((1, tn), at(lambda i, j, k: (0, j))),
               "rows": pl.BlockSpec((tm, LANES), at(lambda i, j, k: (i, 0)))}
    ex_specs = [ex_spec[kind] for _, kind in extras]
    assert accs == 0 or (tn == N and nk == 1 and not b_outer), name
    n_ex, n_out, n_after = len(extras), len(out_dtypes), len(after)

    def body(*refs):
        a_ref, b_ref = refs[0], refs[1]
        ex_refs = refs[2 : 2 + n_ex]
        out_refs = refs[2 + n_ex + n_after : 2 + n_ex + n_after + n_out]
        av = a_ref[...]
        if a_fn is not None:
            av = a_fn(av)
        part = lax.dot_general(av.astype(BF16), b_ref[...].astype(BF16), _DIMS[mode], preferred_element_type=F32)

        def finish(res):
            vals = epi(res, *[e[...] for e in ex_refs]) if epi is not None else (res,)
            for o_ref, v in zip(out_refs, vals[:n_out]):
                o_ref[...] = v.astype(o_ref.dtype)
            for a_ref, v in zip(refs[2 + n_ex + n_after + n_out :], vals[n_out:]):

                @pl.when(pl.program_id(0) == 0)
                def _(a_ref=a_ref, v=v):
                    a_ref[...] = v

                @pl.when(pl.program_id(0) > 0)
                def _(a_ref=a_ref, v=v):
                    a_ref[...] += v

        if nk == 1:
            finish(part)
        else:
            acc = refs[-1]
            k = pl.program_id(2)

            @pl.when(k == 0)
            def _():
                acc[...] = part

            @pl.when(k > 0)
            def _():
                acc[...] += part

            @pl.when(k == nk - 1)
            def _():
                finish(acc[...])

    outs = pl.pallas_call(
        body,
        name=name,
        grid=(N // tn, M // tm, nk) if b_outer else (M // tm, N // tn, nk),
        in_specs=[a_spec, b_spec] + ex_specs + [pl.BlockSpec(memory_space=pl.ANY)] * n_after,
        out_specs=[pl.BlockSpec((tm, LANES), at(lambda i, j, k: (i, 0))) if isinstance(dt, tuple)
                   else pl.BlockSpec((tm, tn), at(lambda i, j, k: (i, j))) if out_blocks == 1
                   else pl.BlockSpec((None, tm, tn), at(lambda i, j, k: (j, i, 0))) for dt in out_dtypes]
        + [pl.BlockSpec((1, N), lambda i, j, k: (0, 0))] * accs,
        out_shape=[jax.ShapeDtypeStruct((M, LANES), dt[0]) if isinstance(dt, tuple)
                   else jax.ShapeDtypeStruct((M, N) if out_blocks == 1 else (out_blocks, M, tn), dt) for dt in out_dtypes]
        + [jax.ShapeDtypeStruct((1, N), F32)] * accs,
        scratch_shapes=[pltpu.VMEM((tm, tn), F32)] if nk > 1 else [],
        compiler_params=_params(3),
    )(a, b, *[e for e, _ in extras], *after)
    return outs[0] if n_out + accs == 1 else outs


def _rowwise(name, fn, S, ts, rows=(), halos=(), consts=(), outs=(), accs=()):
    ts = min(ts, S)
    assert S % ts == 0
    n = S // ts
    in_specs = [pl.BlockSpec((ts, a.shape[1]), lambda i: (i, 0)) for a in rows]
    for a, kind, hr in halos:
        r, nb = ts // hr, S // hr
        if kind == "prev":
            in_specs.append(pl.BlockSpec((hr, a.shape[1]), lambda i, r=r: (jnp.maximum(i * r - 1, 0), 0)))
        else:
            in_specs.append(pl.BlockSpec((hr, a.shape[1]), lambda i, r=r, nb=nb: (jnp.minimum((i + 1) * r, nb - 1), 0)))
    in_specs += [pl.BlockSpec(a.shape, lambda i, nd=a.ndim: (0,) * nd) for a in consts]
    out_specs = [pl.BlockSpec((ts, w), lambda i: (i, 0)) for w, _ in outs]
    out_specs += [pl.BlockSpec((r, w), lambda i: (0, 0)) for r, w in accs]
    out_shape = [jax.ShapeDtypeStruct((S, w), dt) for w, dt in outs]
    out_shape += [jax.ShapeDtypeStruct((r, w), F32) for r, w in accs]
    nr, nh, nc, no = len(rows), len(halos), len(consts), len(outs)

    def body(*refs):
        i = pl.program_id(0)
        rv = [r[...] for r in refs[:nr]]
        hv = [r[...] for r in refs[nr : nr + nh]]
        cv = [r[...] for r in refs[nr + nh : nr + nh + nc]]
        o_refs = refs[nr + nh + nc : nr + nh + nc + no]
        a_refs = refs[nr + nh + nc + no :]
        ovals, avals = fn(i, n, rv, hv, cv)
        for o_ref, v in zip(o_refs, ovals):
            o_ref[...] = v.astype(o_ref.dtype)
        for a_ref, v in zip(a_refs, avals):

            @pl.when(i == 0)
            def _(a_ref=a_ref, v=v):
                a_ref[...] = v

            @pl.when(i > 0)
            def _(a_ref=a_ref, v=v):
                a_ref[...] += v

    res = pl.pallas_call(
        body,
        name=name,
        grid=(n,),
        in_specs=in_specs,
        out_specs=out_specs,
        out_shape=out_shape,
        compiler_params=_params(1),
    )(*rows, *[h[0] for h in halos], *consts)
    return list(res)


def _ln(h, g, b):
    mu = jnp.mean(h, axis=-1, keepdims=True)
    d = h - mu
    var = jnp.mean(d * d, axis=-1, keepdims=True)
    rstd = lax.rsqrt(var + LN_EPS)
    xhat = d * rstd
    return xhat, rstd, xhat * g + b


def _ln_bwd(dy, xhat, rstd, g):
    dxh = dy * g
    m1 = jnp.mean(dxh, axis=-1, keepdims=True)
    m2 = jnp.mean(dxh * xhat, axis=-1, keepdims=True)
    dh = rstd * (dxh - m1 - xhat * m2)
    return dh, jnp.sum(dy * xhat, axis=0, keepdims=True), jnp.sum(dy, axis=0, keepdims=True)


def _wide(col, ts):
    return jnp.broadcast_to(col, (ts, LANES))


def _pool_fwd(x, wp, pb, ps, g, b):
    S, D = x.shape
    gw = D // len(POOL_WINDOWS)
    ts = min(ROW_TILE, S)

    def fn(i, n, rv, hv, cv):
        (xc,), (xp,) = rv, hv
        wpv, pbv, psv, gv, bv = cv
        xp = jnp.where(i > 0, xp, 0.0)
        xx = jnp.concatenate([xp, xc], axis=0)
        t = i * ts + lax.broadcasted_iota(jnp.int32, (ts, 1), 0)
        pooled, ys = [], []
        for gi, w in enumerate(POOL_WINDOWS):
            s = xx[:, gi * gw : (gi + 1) * gw]
            k = 1
            while k < w:
                s = s + pltpu.roll(s, k, axis=0)
                k *= 2
            cnt = jnp.minimum(t + 1, w).astype(F32)
            pg = (s[POOL_HALO:, :] / cnt - xc[:, gi * gw : (gi + 1) * gw]).astype(BF16)
            pooled.append(pg)
            ys.append(jnp.dot(pg, wpv[gi], preferred_element_type=F32))
        y = jnp.concatenate(ys, axis=1)
        h = ALPHA * xc + (y + pbv) * psv
        xhat, rstd, xa = _ln(h, gv, bv)
        return (jnp.concatenate(pooled, axis=1), xhat, _wide(rstd, ts), xa), ()

    return _rowwise(
        "pool_fwd", fn, S, ts, rows=[x], halos=[(x, "prev", POOL_HALO)], consts=[wp, pb, ps, g, b],
        outs=[(D, BF16), (D, F32), (LANES, F32), (D, BF16)],
    )


def _pool_bwd(dh, pooled, wp, pb, ps):
    S, D = dh.shape
    gw = D // len(POOL_WINDOWS)
    ts = min(ROW_TILE, S)
    te = ts + POOL_HALO

    def fn(i, n, rv, hv, cv):
        (dhc, pc), (dhn,) = rv, hv
        wpv, pbv, psv = cv
        dhn = jnp.where(i < n - 1, dhn, 0.0)
        dy_ext = jnp.concatenate([dhc, dhn], axis=0) * psv
        dyb = dy_ext.astype(BF16)
        t = i * ts + lax.broadcasted_iota(jnp.int32, (te, 1), 0)
        dxs, ys = [], []
        for gi, w in enumerate(POOL_WINDOWS):
            sl = slice(gi * gw, (gi + 1) * gw)
            dp = lax.dot_general(dyb[:, sl], wpv[gi], _DIMS["nt"], preferred_element_type=F32)
            s = dp / jnp.minimum(t + 1, w).astype(F32)
            k = 1
            while k < w:
                s = s + pltpu.roll(s, k, axis=0)
                k *= 2
            s = pltpu.roll(s, POOL_HALO - (w - 1), axis=0)
            dxs.append(s[POOL_HALO:, :] - dp[:ts, :])
            ys.append(jnp.dot(pc[:, sl], wpv[gi], preferred_element_type=F32))
        dx = ALPHA * dhc + jnp.concatenate(dxs, axis=1)
        y = jnp.concatenate(ys, axis=1) + pbv
        dscale = jnp.sum(dhc * y, axis=0, keepdims=True)
        dbias = jnp.sum(dy_ext[:ts, :], axis=0, keepdims=True)
        return (dx, dyb[:ts, :]), (dscale, dbias)

    return _rowwise(
        "pool_bwd", fn, S, ts, rows=[dh, pooled], halos=[(dh, "next", POOL_HALO)], consts=[wp, pb, ps],
        outs=[(D, F32), (D, BF16)], accs=[(1, D), (1, D)],
    )


def _pool_dw(pooled, dy):
    S, D = pooled.shape
    G = len(POOL_WINDOWS)
    gw = D // G
    tk = min(MM_TK, S)
    nk = S // tk

    def body(p_ref, d_ref, o_ref):
        k = pl.program_id(1)
        part = lax.dot_general(p_ref[...], d_ref[...], _DIMS["tn"], preferred_element_type=F32)

        @pl.when(k == 0)
        def _():
            o_ref[...] = part

        @pl.when(k > 0)
        def _():
            o_ref[...] += part

    return pl.pallas_call(
        body,
        name="pool_dw",
        grid=(G, nk),
        in_specs=[pl.BlockSpec((tk, gw), lambda g, k: (k, g)), pl.BlockSpec((tk, gw), lambda g, k: (k, g))],
        out_specs=pl.BlockSpec((None, gw, gw), lambda g, k: (g, 0, 0)),
        out_shape=jax.ShapeDtypeStruct((G, gw, gw), F32),
        compiler_params=_params(2),
    )(pooled, dy)


def _res_ln_epi(acc, xh, gp_, bp_, g, b):
    xhat, rstd, xo = _ln(ALPHA * (xh * gp_ + bp_) + acc, g, b)
    return xhat, _wide(rstd, acc.shape[0]), xo


def _res_ln_ffpe_epi(acc, xh, gate, pp, gp_, bp_, g, b):
    return _res_ln_epi(acc + jax.nn.sigmoid(gate) * pp, xh, gp_, bp_, g, b)


def _final_ln_loss_epi(acc, xh, gate, pp, tgt, gp_, bp_, g, b):
    xhat, rstd, y = _ln(ALPHA * (xh * gp_ + bp_) + acc + jax.nn.sigmoid(gate) * pp, g, b)
    e = y - tgt
    dh, dg, db = _ln_bwd(e * (1.0 / acc.shape[1]), xhat, rstd, g)
    return dh, dh, jnp.sum(e * e, axis=0, keepdims=True), dg, db


def _ln_bwd_epi(acc, rest, xhat, rstd, g):
    dh, dg, db = _ln_bwd(acc + rest, xhat, rstd[:, :1], g)
    return dh, dh, dg, db


def _ple_bwd(name, dh, gate, pp):
    S, D = dh.shape
    ts = min(ROW_TILE, S)

    def fn(i, n, rv, hv, cv):
        d, gt, p_ = rv
        sg = jax.nn.sigmoid(gt)
        dgt = d * p_ * sg * (1.0 - sg)
        return (dgt, d * sg), (jnp.sum(dgt, axis=0, keepdims=True),)

    return _rowwise(name, fn, S, ts, rows=[dh, gate, pp], outs=[(D, BF16), (D, BF16)], accs=[(1, D)])


def _silu(c):
    return c * jax.nn.sigmoid(c)


def _qkv_point(c, is_qk, scale):
    s = _silu(c)
    nrm = s * lax.rsqrt(jnp.sum(s * s, axis=-1, keepdims=True) + L2_EPS) * scale
    return jnp.where(is_qk, nrm, s)


def _conv_rows(xx, wv, lo, rows):
    acc = None
    for j in range(CONV_WIDTH):
        sh = CONV_WIDTH - 1 - j
        term = (pltpu.roll(xx, sh, axis=0) if sh else xx)[lo : lo + rows, :] * wv[j : j + 1, :]
        acc = term if acc is None else acc + term
    return acc


def _conv_fwd(qkv_pre, conv_w):
    S, W = qkv_pre.shape
    D = W // 3
    H = D // HEAD_DIM
    ts = min(CONV_TILE, S)
    r = ts // CONV_HALO

    def body(x_ref, xp_ref, w_ref, o_ref):
        j, i = pl.program_id(0), pl.program_id(1)
        xp = jnp.where(i > 0, xp_ref[...], 0.0)
        xx = jnp.concatenate([xp, x_ref[...]], axis=0)
        c = _conv_rows(xx, w_ref[...], CONV_HALO, ts)
        scale = jnp.where(j == 0, HEAD_DIM**-0.5, 1.0).astype(F32)
        for h in range(H):
            sl = slice(h * HEAD_DIM, (h + 1) * HEAD_DIM)
            o_ref[:, sl] = _qkv_point(c[:, sl], j < 2, scale)

    return pl.pallas_call(
        body,
        name="gdn_conv_fwd",
        grid=(3, S // ts),
        in_specs=[
            pl.BlockSpec((ts, D), lambda j, i: (i, j)),
            pl.BlockSpec((CONV_HALO, D), lambda j, i: (jnp.maximum(i * r - 1, 0), j)),
            pl.BlockSpec((CONV_WIDTH, D), lambda j, i: (0, j)),
        ],
        out_specs=pl.BlockSpec((ts, D), lambda j, i: (i, j)),
        out_shape=jax.ShapeDtypeStruct((S, W), F32),
        compiler_params=_params(2),
    )(qkv_pre, qkv_pre, conv_w)


def _conv_bwd(qkv_pre, conv_w, dqkvn):
    S, W = qkv_pre.shape
    D = W // 3
    H = D // HEAD_DIM
    ts = min(CONV_TILE, S)
    r, nb = ts // CONV_HALO, S // CONV_HALO
    te = ts + CONV_HALO

    def body(x_ref, xp_ref, xn_ref, w_ref, d_ref, dn_ref, dx_ref, dw_ref):
        j, i = pl.program_id(0), pl.program_id(1)
        n = pl.num_programs(1)
        wv = w_ref[...]
        xp = jnp.where(i > 0, xp_ref[...], 0.0)
        xx = jnp.concatenate([xp, x_ref[...], xn_ref[...]], axis=0)
        xr = [pltpu.roll(xx, sh, axis=0) if sh else xx for sh in range(CONV_WIDTH)]
        c = None
        for jj in range(CONV_WIDTH):
            term = xr[CONV_WIDTH - 1 - jj][CONV_HALO : CONV_HALO + te, :] * wv[jj : jj + 1, :]
            c = term if c is None else c + term
        dn = jnp.where(i < n - 1, dn_ref[...], 0.0)
        dout = jnp.concatenate([d_ref[...], dn], axis=0)
        scale = jnp.where(j == 0, HEAD_DIM**-0.5, 1.0).astype(F32)
        dcs = []
        for h in range(H):
            sl = slice(h * HEAD_DIM, (h + 1) * HEAD_DIM)
            _, vjp = jax.vjp(lambda cc: _qkv_point(cc, j < 2, scale), c[:, sl])
            dcs.append(vjp(dout[:, sl])[0])
        dc = jnp.concatenate(dcs, axis=1)
        dx = None
        dws = []
        for jj in range(CONV_WIDTH):
            sh = CONV_WIDTH - 1 - jj
            term = pltpu.roll(dc, CONV_HALO - sh, axis=0)[CONV_HALO:, :] * wv[jj : jj + 1, :]
            dx = term if dx is None else dx + term
            dws.append(jnp.sum(dc[:ts, :] * xr[sh][CONV_HALO : CONV_HALO + ts, :], axis=0, keepdims=True))
        dx_ref[...] = dx.astype(dx_ref.dtype)
        dw = jnp.concatenate(dws, axis=0)

        @pl.when(i == 0)
        def _():
            dw_ref[...] = dw

        @pl.when(i > 0)
        def _():
            dw_ref[...] += dw

    return pl.pallas_call(
        body,
        name="gdn_conv_bwd",
        grid=(3, S // ts),
        in_specs=[
            pl.BlockSpec((ts, D), lambda j, i: (i, j)),
            pl.BlockSpec((CONV_HALO, D), lambda j, i: (jnp.maximum(i * r - 1, 0), j)),
            pl.BlockSpec((CONV_HALO, D), lambda j, i: (jnp.minimum((i + 1) * r, nb - 1), j)),
            pl.BlockSpec((CONV_WIDTH, D), lambda j, i: (0, j)),
            pl.BlockSpec((ts, D), lambda j, i: (i, j)),
            pl.BlockSpec((CONV_HALO, D), lambda j, i: (jnp.minimum((i + 1) * r, nb - 1), j)),
        ],
        out_specs=[pl.BlockSpec((ts, D), lambda j, i: (i, j)), pl.BlockSpec((CONV_WIDTH, D), lambda j, i: (0, j))],
        out_shape=[jax.ShapeDtypeStruct((S, W), BF16), jax.ShapeDtypeStruct((CONV_WIDTH, W), F32)],
        compiler_params=_params(2),
    )(qkv_pre, qkv_pre, qkv_pre, conv_w, dqkvn, dqkvn)


def _softplus(x):
    pos = x > 0.0
    return jnp.where(pos, x, 0.0) + jnp.log(1.0 + jnp.exp(jnp.where(pos, -x, x)))


def _gates(bl, al, alog, dt):
    return jax.nn.sigmoid(bl), -jnp.exp(alog) * _softplus(al + dt)


def _gates_fwd(ba, alog, dt):
    S = ba.shape[0]
    ts = min(ROW_TILE, S)

    def fn(i, n, rv, hv, cv):
        return _gates(rv[0][:, :LANES], rv[0][:, LANES:], cv[0], cv[1]), ()

    return _rowwise("gdn_gates_fwd", fn, S, ts, rows=[ba], consts=[alog, dt], outs=[(LANES, F32), (LANES, F32)])


def _gates_bwd(ba, alog, dt, dbeta, dg, H):
    S = ba.shape[0]
    ts = min(ROW_TILE, S)

    def fn(i, n, rv, hv, cv):
        bav, dbv, dgv = rv
        real = lax.broadcasted_iota(jnp.int32, (1, LANES), 1) < H
        _, vjp = jax.vjp(_gates, bav[:, :LANES], bav[:, LANES:], cv[0], cv[1])
        dbl, dal, dalog, ddt = vjp((jnp.where(real, dbv, 0.0), jnp.where(real, dgv, 0.0)))
        dbl, dal = jnp.where(real, dbl, 0.0), jnp.where(real, dal, 0.0)
        return (jnp.concatenate([dbl, dal], axis=1),), (jnp.where(real, dalog, 0.0), jnp.where(real, ddt, 0.0))

    return _rowwise(
        "gdn_gates_bwd", fn, S, ts, rows=[ba, dbeta, dg], consts=[alog, dt], outs=[(2 * LANES, BF16)],
        accs=[(1, LANES), (1, LANES)],
    )


def _split_bf16(a, n):
    parts, rest = [], a
    for _ in range(n):
        piece = rest.astype(BF16)
        parts.append(piece)
        rest = rest - piece.astype(F32)
    return parts


def _tri_dot(a, b, mode, tri):
    d = lambda u, v: lax.dot_general(u, v, _DIMS[mode], preferred_element_type=F32)
    if tri == 0:
        return sum(d(a.astype(BF16), piece) for piece in _split_bf16(b, 3))
    return sum(d(piece, b.astype(BF16)) for piece in _split_bf16(a, 3))


def _bdot_raw(a, b, mode):
    return lax.dot_general(a.astype(BF16), b.astype(BF16), _DIMS[mode], preferred_element_type=F32)


@functools.partial(jax.custom_vjp, nondiff_argnums=(2,))
def _bdot(a, b, mode):
    return _bdot_raw(a, b, mode)


def _bdot_fwd(a, b, mode):
    return _bdot_raw(a, b, mode), (a, b)


def _bdot_bwd(mode, res, ct):
    a, b = res
    if mode == "nn":
        return _bdot(ct, b, "nt"), _bdot(a, ct, "tn")
    if mode == "nt":
        return _bdot(ct, b, "nn"), _bdot(ct, a, "tn")
    return _bdot(b, ct, "nt"), _bdot(a, ct, "nn")


_bdot.defvjp(_bdot_fwd, _bdot_bwd)


@jax.custom_vjp
def _unit_lower_inverse(a_strict):
    return _unit_lower_inverse_raw(a_strict)


def _unit_lower_inverse_fwd(a_strict):
    t = _unit_lower_inverse_raw(a_strict)
    return t, t


def _unit_lower_inverse_bwd(t, ct):
    left = [_bdot(ti, ci, "tn") for ti, ci in zip(t, ct)]
    return (tuple(-_bdot(li, ti, "nt") for li, ti in zip(left, t)),)


_unit_lower_inverse.defvjp(_unit_lower_inverse_fwd, _unit_lower_inverse_bwd)


@jax.custom_vjp
def _saved_inverse(a_strict, t):
    return t


def _saved_inverse_fwd(a_strict, t):
    return t, t


def _saved_inverse_bwd(t, ct):
    return _unit_lower_inverse_bwd(t, ct) + (tuple(jnp.zeros_like(ti) for ti in t),)


_saved_inverse.defvjp(_saved_inverse_fwd, _saved_inverse_bwd)


def _unit_lower_inverse_raw(a_strict):
    C = a_strict[0].shape[0]
    ii = lax.broadcasted_iota(jnp.int32, (C, C), 0)
    jj = lax.broadcasted_iota(jnp.int32, (C, C), 1)
    eye = (ii == jj).astype(F32)
    blk = 16
    same = (ii // blk) == (jj // blk)
    p = [-jnp.where(same, a, 0.0) for a in a_strict]
    t = [eye + x for x in p]
    for _ in range(3):
        p = [_bdot(x, x, "nn") for x in p]
        t = [ti + _bdot(ti, x, "nn") for ti, x in zip(t, p)]
    while blk < C:
        same2 = (ii // (2 * blk)) == (jj // (2 * blk))
        off = jnp.logical_and(same2, jnp.logical_not(same))
        te = [_bdot(ti, jnp.where(off, a, 0.0), "nn") for ti, a in zip(t, a_strict)]
        t = [ti - _bdot(x, ti, "nn") for ti, x in zip(t, te)]
        same, blk = same2, 2 * blk
    return tuple(t)


def _chunk_heads(q, k, v, gc_col, gc_row, b_col, s0, t_saved=None, with_t=False):
    R = range(len(q))
    C = q[0].shape[0]
    ii = lax.broadcasted_iota(jnp.int32, (C, C), 0)
    jj = lax.broadcasted_iota(jnp.int32, (C, C), 1)
    rows = lax.broadcasted_iota(jnp.int32, (C, 1), 0)
    decay = [jnp.where(ii >= jj, jnp.exp(jnp.minimum(gc_col[h] - gc_row[h], 0.0)), 0.0) for h in R]
    kb = [k[h] * b_col[h] for h in R]
    a = [_bdot(kb[h], k[h], "nt") * decay[h] for h in R]
    qk = [_bdot(q[h], k[h], "nt") * decay[h] for h in R]
    a_strict = tuple(jnp.where(ii > jj, a[h], 0.0) for h in R)
    t = _unit_lower_inverse(a_strict) if t_saved is None else _saved_inverse(a_strict, t_saved)
    eg = [jnp.exp(gc_col[h]) for h in R]
    u = [_bdot(t[h], v[h] * b_col[h], "nn") for h in R]
    w = [_bdot(t[h], kb[h] * eg[h], "nn") for h in R]
    g_last = [jnp.sum(jnp.where(rows == C - 1, gc_col[h], 0.0), axis=0, keepdims=True) for h in R]
    kd = [k[h] * jnp.exp(g_last[h] - gc_col[h]) for h in R]
    ws = [_bdot(w[h], s0[h], "nn") for h in R]
    qs = [_bdot(q[h] * eg[h], s0[h], "nn") for h in R]
    v_new = [u[h] - ws[h] for h in R]
    o = [qs[h] + _bdot(qk[h], v_new[h], "nn") for h in R]
    s1 = [s0[h] * jnp.exp(g_last[h]) + _bdot(kd[h], v_new[h], "tn") for h in R]
    return (tuple(o), tuple(s1), t) if with_t else (tuple(o), tuple(s1))


def _pick_lane(a, h):
    lanes = lax.broadcasted_iota(jnp.int32, a.shape, 1)
    return jnp.sum(jnp.where(lanes == h, a, 0.0), axis=1, keepdims=True)


def _pick_row(a, h):
    rows = lax.broadcasted_iota(jnp.int32, a.shape, 0)
    return jnp.sum(jnp.where(rows == h, a, 0.0), axis=0, keepdims=True)


def _tri(C):
    ii = lax.broadcasted_iota(jnp.int32, (C, C), 0)
    jj = lax.broadcasted_iota(jnp.int32, (C, C), 1)
    return (ii >= jj).astype(F32)


def _delta_fwd(qkvn, g_pad, g_rows, beta_pad):
    S, W = qkvn.shape
    D = W // 3
    H = D // HEAD_DIM
    C = min(CHUNK, S)
    N = S // C

    def body(x_ref, gp_ref, gr_ref, bp_ref, o_ref, sall_ref, tall_ref, st):
        n = pl.program_id(0)

        @pl.when(n == 0)
        def _():
            st[...] = jnp.zeros_like(st)

        low = _tri(C)
        gc_cols = _tri_dot(low, gp_ref[...], "nn", 0)
        gc_rows = _tri_dot(gr_ref[...], low, "nt", 1)
        bcols = bp_ref[...]
        hs = range(H)
        s0 = tuple(st[h] for h in hs)
        for h in hs:
            sall_ref[h] = s0[h]
        o, s1, t = _chunk_heads(
            tuple(x_ref[:, h * HEAD_DIM : (h + 1) * HEAD_DIM] for h in hs),
            tuple(x_ref[:, D + h * HEAD_DIM : D + (h + 1) * HEAD_DIM] for h in hs),
            tuple(x_ref[:, 2 * D + h * HEAD_DIM : 2 * D + (h + 1) * HEAD_DIM] for h in hs),
            tuple(_pick_lane(gc_cols, h) for h in hs), tuple(_pick_row(gc_rows, h) for h in hs),
            tuple(_pick_lane(bcols, h) for h in hs), s0, with_t=True,
        )
        for h in hs:
            st[h] = s1[h]
            o_ref[:, h * HEAD_DIM : (h + 1) * HEAD_DIM] = o[h]
            tall_ref[h] = t[h].astype(tall_ref.dtype)

    return pl.pallas_call(
        body,
        name="gdn_delta_fwd",
        grid=(N,),
        in_specs=[
            pl.BlockSpec((C, W), lambda n: (n, 0)),
            pl.BlockSpec((C, LANES), lambda n: (n, 0)),
            pl.BlockSpec((None, 8, C), lambda n: (n, 0, 0)),
            pl.BlockSpec((C, LANES), lambda n: (n, 0)),
        ],
        out_specs=[pl.BlockSpec((C, D), lambda n: (n, 0)), pl.BlockSpec((None, H, HEAD_DIM, HEAD_DIM), lambda n: (n, 0, 0, 0)),
                   pl.BlockSpec((None, H, C, C), lambda n: (n, 0, 0, 0))],
        out_shape=[jax.ShapeDtypeStruct((S, D), F32), jax.ShapeDtypeStruct((N, H, HEAD_DIM, HEAD_DIM), F32), jax.ShapeDtypeStruct((N, H, C, C), BF16)],
        scratch_shapes=[pltpu.VMEM((H, HEAD_DIM, HEAD_DIM), F32)],
        compiler_params=_params(1),
    )(qkvn, g_pad, g_rows, beta_pad)


def _delta_bwd(qkvn, g_pad, g_rows, beta_pad, s_all, t_all, do):
    S, W = qkvn.shape
    D = W // 3
    H = D // HEAD_DIM
    C = min(CHUNK, S)
    N = S // C

    def body(x_ref, gp_ref, gr_ref, bp_ref, sall_ref, tall_ref, do_ref, dx_ref, dgp_ref, dgr_ref, dbp_ref, dst):
        n = pl.program_id(0)

        @pl.when(n == 0)
        def _():
            dst[...] = jnp.zeros_like(dst)

        low = _tri(C)
        gc_cols = _tri_dot(low, gp_ref[...], "nn", 0)
        gc_rows = _tri_dot(gr_ref[...], low, "nt", 1)
        bcols = bp_ref[...]
        lane = lax.broadcasted_iota(jnp.int32, (1, LANES), 1)
        row8 = lax.broadcasted_iota(jnp.int32, (8, 1), 0)
        dgc_cols = jnp.zeros((C, LANES), F32)
        dgc_rows = jnp.zeros((8, C), F32)
        dbcols = jnp.zeros((C, LANES), F32)
        hs = range(H)
        t_saved = tuple(tall_ref[h].astype(F32) for h in hs)
        _, vjp = jax.vjp(
            lambda *args: _chunk_heads(*args, t_saved=t_saved),
            tuple(x_ref[:, h * HEAD_DIM : (h + 1) * HEAD_DIM] for h in hs),
            tuple(x_ref[:, D + h * HEAD_DIM : D + (h + 1) * HEAD_DIM] for h in hs),
            tuple(x_ref[:, 2 * D + h * HEAD_DIM : 2 * D + (h + 1) * HEAD_DIM] for h in hs),
            tuple(_pick_lane(gc_cols, h) for h in hs), tuple(_pick_row(gc_rows, h) for h in hs),
            tuple(_pick_lane(bcols, h) for h in hs), tuple(sall_ref[h] for h in hs),
        )
        dq, dk, dv, dgc, dgr, dbc, ds0 = vjp((tuple(do_ref[:, h * HEAD_DIM : (h + 1) * HEAD_DIM] for h in hs), tuple(dst[h] for h in hs)))
        for h in hs:
            dst[h] = ds0[h]
            dx_ref[:, h * HEAD_DIM : (h + 1) * HEAD_DIM] = dq[h]
            dx_ref[:, D + h * HEAD_DIM : D + (h + 1) * HEAD_DIM] = dk[h]
            dx_ref[:, 2 * D + h * HEAD_DIM : 2 * D + (h + 1) * HEAD_DIM] = dv[h]
            dgc_cols = dgc_cols + dgc[h] * (lane == h).astype(F32)
            dgc_rows = dgc_rows + dgr[h] * (row8 == h).astype(F32)
            dbcols = dbcols + dbc[h] * (lane == h).astype(F32)
        dgp_ref[...] = _tri_dot(low, dgc_cols, "tn", 0)
        dgr_ref[...] = _tri_dot(dgc_rows, low, "nn", 1)
        dbp_ref[...] = dbcols

    rev = lambda n: N - 1 - n
    return pl.pallas_call(
        body,
        name="gdn_delta_bwd",
        grid=(N,),
        in_specs=[
            pl.BlockSpec((C, W), lambda n: (rev(n), 0)),
            pl.BlockSpec((C, LANES), lambda n: (rev(n), 0)),
            pl.BlockSpec((None, 8, C), lambda n: (rev(n), 0, 0)),
            pl.BlockSpec((C, LANES), lambda n: (rev(n), 0)),
            pl.BlockSpec((None, H, HEAD_DIM, HEAD_DIM), lambda n: (rev(n), 0, 0, 0)),
            pl.BlockSpec((None, H, C, C), lambda n: (rev(n), 0, 0, 0)),
            pl.BlockSpec((C, D), lambda n: (rev(n), 0)),
        ],
        out_specs=[
            pl.BlockSpec((C, W), lambda n: (rev(n), 0)),
            pl.BlockSpec((C, LANES), lambda n: (rev(n), 0)),
            pl.BlockSpec((None, 8, C), lambda n: (rev(n), 0, 0)),
            pl.BlockSpec((C, LANES), lambda n: (rev(n), 0)),
        ],
        out_shape=[
            jax.ShapeDtypeStruct((S, W), F32),
            jax.ShapeDtypeStruct((S, LANES), F32),
            jax.ShapeDtypeStruct((N, 8, C), F32),
            jax.ShapeDtypeStruct((S, LANES), F32),
        ],
        scratch_shapes=[pltpu.VMEM((H, HEAD_DIM, HEAD_DIM), F32)],
        compiler_params=_params(1),
    )(qkvn, g_pad, g_rows, beta_pad, s_all, t_all, do)


def _gate_norm_head(o, z, nw):
    return o * lax.rsqrt(jnp.mean(o * o, axis=-1, keepdims=True) + RMS_EPS) * nw * _silu(z)


def _gate_norm_fwd(o, z, nw):
    S, D = o.shape
    H = D // HEAD_DIM
    ts = min(ROW_TILE, S)

    def fn(i, n, rv, hv, cv):
        ov, zv = rv
        parts = [_gate_norm_head(ov[:, h * HEAD_DIM : (h + 1) * HEAD_DIM], zv[:, h * HEAD_DIM : (h + 1) * HEAD_DIM], cv[0]) for h in range(H)]
        return (jnp.concatenate(parts, axis=1),), ()

    return _rowwise("gdn_gate_norm_fwd", fn, S, ts, rows=[o, z], consts=[nw], outs=[(D, BF16)])[0]


def _gate_norm_bwd(dog, o, z, nw):
    S, D = o.shape
    H = D // HEAD_DIM
    ts = min(ROW_TILE, S)

    def fn(i, n, rv, hv, cv):
        dv, ov, zv = rv
        dos, dzs, dnw = [], [], None
        for h in range(H):
            sl = slice(h * HEAD_DIM, (h + 1) * HEAD_DIM)
            _, vjp = jax.vjp(_gate_norm_head, ov[:, sl], zv[:, sl], cv[0])
            a, b_, c_ = vjp(dv[:, sl])
            dos.append(a)
            dzs.append(b_)
            dnw = c_ if dnw is None else dnw + c_
        return (jnp.concatenate(dos, axis=1), jnp.concatenate(dzs, axis=1)), (dnw,)

    return _rowwise("gdn_gate_norm_bwd", fn, S, ts, rows=[dog, o, z], consts=[nw], outs=[(D, F32), (D, BF16)], accs=[(1, HEAD_DIM)])


def _square_bf16(r):
    rf = r.astype(F32)
    return rf * rf


def _mlp_ple_dw(li, dh, dhb, xa, p, r, gate, pp, w2):
    dpre = _mm(f"l{li}_mlp_down_bwd", dhb, w2, "nt", [BF16], epi=lambda acc, rr: (acc * (2.0 * rr.astype(F32)),), extras=[(r, "tile")], tm=1024, tn=1024, b_outer=True)
    dw2 = _mm(f"l{li}_mlp_dw2", r, dhb, "tn", [BF16], a_fn=_square_bf16, **DW_TILES)
    dgate, dpp, dbg = _ple_bwd(f"l{li}_ple_bwd", dh, gate, pp)
    dw1 = _mm(f"l{li}_mlp_dw1", xa, dpre, "tn", [BF16], out_blocks=N_DEV, **DW_TILES)
    dwg = _mm(f"l{li}_ple_dwg", xa, dgate, "tn", [BF16], **DW_TILES)
    dwp = _mm(f"l{li}_ple_dwp", p, dpp, "tn", [BF16], out_blocks=N_DEV, **DW_TILES)
    rows = lambda a: a.reshape((N_DEV, a.shape[0] // N_DEV) + a.shape[1:])
    return dpre, dgate, dw1, rows(dw2), rows(dwg), dbg, dwp


def _mlp_ple_dx(li, dh, dpre, dgate, w1, wg, after, xhat, rstd, g):
    t = _mm(f"l{li}_ple_gate_bwd", dgate, wg, "nt", [F32], epi=lambda acc, d: (acc + ALPHA * d,), extras=[(dh, "tile")], tm=1024, after=after)
    return _mm(f"l{li}_mlp_up_bwd", dpre, w1, "nt", [F32, BF16], epi=_ln_bwd_epi, extras=[(t, "tile"), (xhat, "tile"), (rstd, "rows"), (g, "row")],
               tm=256, tk=4096, accs=2)


def _local_step(x, p, tgt, W, fetch, emit):
    S, D = x.shape
    H = D // HEAD_DIM
    C = min(CHUNK, S)
    N = S // C
    lg = lambda i, j: W["ln_gain"][2 * i + j][None, :]
    lb = lambda i, j: W["ln_bias"][2 * i + j][None, :]
    G = {}

    pooled, xh0a, rs0a, x0a = _pool_fwd(x, W["pool_w"], W["pool_b"], W["pool_scale"], lg(0, 0), lb(0, 0))
    w0a = fetch("l0a", x0a)
    r0 = _mm("l0_mlp_up", x0a, w0a["mlp_w1"], "nn", [BF16], epi=lambda acc: (jnp.maximum(acc, 0.0),), tm=1024, tn=1024, b_outer=True, after=w0a.get("_after", ()))
    w0b = fetch("l0b", r0)
    gate0 = _mm("l0_ple_gate", x0a, w0b["ple_gate_w"], "nn", [F32], epi=lambda acc, bias: (acc + bias,), extras=[(W["ple_gate_b"][0:1], "row")], tm=1024)
    pp0 = _mm("l0_ple_proj", p[0], w0b["ple_proj"], "nn", [F32])
    w0c = fetch("l0c", pp0)
    ln_rows = lambda i, j, i2, j2: [(lg(i, j), "row"), (lb(i, j), "row"), (lg(i2, j2), "row"), (lb(i2, j2), "row")]
    xh0b, rs0b, x0b = _mm("l0_mlp_down", r0, w0c["mlp_w2"], "nn", [F32, (F32, LANES), BF16], a_fn=_square_bf16, tm=256, tn=D, tk=4096, epi=_res_ln_ffpe_epi,
                          extras=[(xh0a, "tile"), (gate0, "tile"), (pp0, "tile")] + ln_rows(0, 0, 0, 1), after=w0c.get("_after", ()))

    wg_ = fetch("gdn", x0b)
    qkv_pre = _mm("gdn_in_qkv", x0b, wg_["gdn_wqkv"], "nn", [F32], tm=1024, tn=1024, b_outer=True, after=wg_.get("_after", ()))
    z = _mm("gdn_in_z", x0b, wg_["gdn_wz"], "nn", [F32], tm=1024)
    ba = _mm("gdn_in_ba", x0b, wg_["gdn_wba"], "nn", [F32])
    qkvn = _conv_fwd(qkv_pre, W["gdn_conv"])
    beta_pad, g_pad = _gates_fwd(ba, W["gdn_a_log"], W["gdn_dt_bias"])
    g_rows = g_pad[:, :8].reshape(N, C, 8).transpose(0, 2, 1)
    o, s_all, t_all = _delta_fwd(qkvn, g_pad, g_rows, beta_pad)
    og = _gate_norm_fwd(o, z, W["gdn_norm_w"])
    xh1a, rs1a, x1a = _mm("gdn_out", og, wg_["gdn_w_out"], "nn", [F32, (F32, LANES), BF16], tm=512, tn=D, epi=_res_ln_epi,
                          extras=[(xh0b, "tile")] + ln_rows(0, 1, 1, 0))
    w1_ = fetch("l1", x1a)
    r1 = _mm("l1_mlp_up", x1a, w1_["mlp_w1"], "nn", [BF16], epi=lambda acc: (jnp.maximum(acc, 0.0),), tm=1024, tn=1024, b_outer=True)
    gate1 = _mm("l1_ple_gate", x1a, w1_["ple_gate_w"], "nn", [F32], epi=lambda acc, bias: (acc + bias,), extras=[(W["ple_gate_b"][1:2], "row")], tm=1024)
    pp1 = _mm("l1_ple_proj", p[1], w1_["ple_proj"], "nn", [F32])
    dh1b, dh1b_b, loss_cols, dg11, db11 = _mm("l1_mlp_down", r1, w1_["mlp_w2"], "nn", [F32, BF16], a_fn=_square_bf16, tm=256, tn=D, tk=4096, epi=_final_ln_loss_epi,
                                               extras=[(xh1a, "tile"), (gate1, "tile"), (pp1, "tile"), (tgt, "tile")] + ln_rows(1, 0, 1, 1), accs=3)

    dpre1, dgate1, dw1_1, dw2_1, dwg_1, dbg_1, dwp_1 = _mlp_ple_dw(1, dh1b, dh1b_b, x1a, p[1], r1, gate1, pp1, w1_["mlp_w2"])
    tok = emit("l1", {"mlp_w1": dw1_1, "mlp_w2": dw2_1, "ple_gate_w": dwg_1, "ple_proj": dwp_1})
    dh1a, dh1a_b, dg10, db10 = _mlp_ple_dx(1, dh1b, dpre1, dgate1, w1_["mlp_w1"], w1_["ple_gate_w"], [tok], xh1a, rs1a, lg(1, 0))
    dog = _mm("gdn_out_bwd", dh1a_b, wg_["gdn_w_out"], "nt", [F32], tm=1024)
    dw_out = _mm("gdn_dw_out", og, dh1a_b, "tn", [BF16], **DW_TILES)
    dw_out = dw_out.reshape((N_DEV, dw_out.shape[0] // N_DEV) + dw_out.shape[1:])
    do, dz, dnw = _gate_norm_bwd(dog, o, z, W["gdn_norm_w"])
    dqkvn, dg_col, dg_row, dbeta = _delta_bwd(qkvn, g_pad, g_rows, beta_pad, s_all, t_all, do)
    dg_all = dg_col + jnp.pad(dg_row.transpose(0, 2, 1).reshape(S, 8), ((0, 0), (0, LANES - 8)))
    dba, dalog, ddt = _gates_bwd(ba, W["gdn_a_log"], W["gdn_dt_bias"], dbeta, dg_all, H)
    dqkv, dconv = _conv_bwd(qkv_pre, W["gdn_conv"], dqkvn)
    dwqkv = _mm("gdn_dwqkv", x0b, dqkv, "tn", [F32], **DW_TILES)
    dwz = _mm("gdn_dwz", x0b, dz, "tn", [F32], **DW_TILES)
    dwba = _mm("gdn_dwba", x0b, dba, "tn", [F32], **DW_TILES)
    dw_in = jnp.concatenate([dwqkv, dwz, dwba[:, :H], dwba[:, LANES : LANES + H]], axis=1)
    tok = emit("gdn", {"gdn_w_in": _split_blocks("gdn_w_in", dw_in).astype(BF16), "gdn_w_out": dw_out})
    t = _mm("gdn_in_ba_bwd", dba, wg_["gdn_wba"], "nt", [F32], epi=lambda acc, d: (acc + ALPHA * d,), extras=[(dh1a, "tile")], after=[tok])
    t = _mm("gdn_in_z_bwd", dz, wg_["gdn_wz"], "nt", [F32], epi=lambda acc, d: (acc + d,), extras=[(t, "tile")], tm=1024)
    dh0b, dh0b_b, dg01, db01 = _mm("gdn_in_qkv_bwd", dqkv, wg_["gdn_wqkv"], "nt", [F32, BF16], epi=_ln_bwd_epi,
                                    extras=[(t, "tile"), (xh0b, "tile"), (rs0b, "rows"), (lg(0, 1), "row")], tm=256, tk=3072, accs=2)

    dpre0, dgate0, dw1_0, dw2_0, dwg_0, dbg_0, dwp_0 = _mlp_ple_dw(0, dh0b, dh0b_b, x0a, p[0], r0, gate0, pp0, w0c["mlp_w2"])
    tok = emit("l0", {"mlp_w1": dw1_0, "mlp_w2": dw2_0, "ple_gate_w": dwg_0, "ple_proj": dwp_0})
    dh0a, _, dg00, db00 = _mlp_ple_dx(0, dh0b, dpre0, dgate0, w0a["mlp_w1"], w0b["ple_gate_w"], [tok], xh0a, rs0a, lg(0, 0))
    grad_x, dyp, dscale, dpb = _pool_bwd(dh0a, pooled, W["pool_w"], W["pool_b"], W["pool_scale"])
    G["pool_w"] = _pool_dw(pooled, dyp)

    G["ln_gain"] = jnp.concatenate([dg00, dg01, dg10, dg11], axis=0)
    G["ln_bias"] = jnp.concatenate([db00, db01, db10, db11], axis=0)
    G["pool_b"] = dpb
    G["pool_scale"] = dscale
    G["gdn_conv"] = dconv
    G["gdn_a_log"] = dalog[:, :H]
    G["gdn_dt_bias"] = ddt[:, :H]
    G["gdn_norm_w"] = dnw
    G["ple_gate_b"] = jnp.concatenate([dbg_0, dbg_1], axis=0)
    return loss_cols, grad_x, G


_HBM = pl.BlockSpec(memory_space=pltpu.HBM)


def _all_gather(name, shards):
    T = len(shards)

    def body(*refs):
        ins, outs = refs[:T], refs[T : 2 * T]
        send_sems, recv_sems, local_sems = refs[2 * T :]
        x, y, c = lax.axis_index("x"), lax.axis_index("y"), lax.axis_index("c")
        me, sibling = (x, y, c), (x, y, 1 - c)
        chips = [(1 - x, y), (x, 1 - y), (1 - x, 1 - y)]

        def blk(t, px, py, pc):
            return outs[t].at[4 * px + 2 * py + pc]

        def copy(t, k, block, to, src=None):
            return pltpu.make_async_remote_copy(
                src_ref=blk(t, *block) if src is None else src, dst_ref=blk(t, *block),
                send_sem=send_sems.at[t, k], recv_sem=recv_sems.at[t, k], device_id=to, device_id_type=MESH,
            )

        mine = [pltpu.make_async_copy(ins[t], blk(t, *me), local_sems.at[t]) for t in range(T)]
        for cp in mine:
            cp.start()
        first = []
        for t in range(T):
            first.append(copy(t, 0, me, sibling, src=ins[t]))
            first += [copy(t, 1 + j, me, (*chip, c), src=ins[t]) for j, chip in enumerate(chips)]
        for cp in first:
            cp.start()
        passed = []
        for j, chip in enumerate(chips):
            for t in range(T):
                copy(t, 1 + j, (*chip, c), me).wait_recv()
                fw = copy(t, 4 + j, (*chip, c), sibling)
                fw.start()
                passed.append(fw)
        for t in range(T):
            copy(t, 0, sibling, me).wait_recv()
            for j, chip in enumerate(chips):
                copy(t, 4 + j, (*chip, 1 - c), me).wait_recv()
        for cp in first + passed:
            cp.wait_send()
        for cp in mine:
            cp.wait()

    return pl.pallas_call(
        body,
        name=name,
        in_specs=[_HBM] * T,
        out_specs=[_HBM] * T,
        out_shape=[jax.ShapeDtypeStruct((N_DEV,) + s.shape, s.dtype) for s in shards],
        scratch_shapes=[pltpu.SemaphoreType.DMA((T, 7)), pltpu.SemaphoreType.DMA((T, 7)), pltpu.SemaphoreType.DMA((T,))],
    )(*shards)


def _exchange(name, blocks):
    def body(g_ref, o_ref, send_sems, recv_sems, local_sem):
        x, y, c = lax.axis_index("x"), lax.axis_index("y"), lax.axis_index("c")
        own = pltpu.make_async_copy(g_ref.at[4 * x + 2 * y + c], o_ref.at[N_DEV - 1], local_sem)
        own.start()
        copies = []
        for rel in range(1, N_DEV):
            px = 1 - x if rel & 4 else x
            py = 1 - y if rel & 2 else y
            pc = 1 - c if rel & 1 else c
            copies.append(
                pltpu.make_async_remote_copy(
                    src_ref=g_ref.at[4 * px + 2 * py + pc], dst_ref=o_ref.at[rel - 1],
                    send_sem=send_sems.at[rel - 1], recv_sem=recv_sems.at[rel - 1], device_id=(px, py, pc), device_id_type=MESH,
                )
            )
        for cp in copies:
            cp.start()
        for cp in copies:
            cp.wait_recv()
        for cp in copies:
            cp.wait_send()
        own.wait()

    return pl.pallas_call(
        body,
        name=name,
        in_specs=[_HBM],
        out_specs=_HBM,
        out_shape=jax.ShapeDtypeStruct(blocks.shape, blocks.dtype),
        scratch_shapes=[pltpu.SemaphoreType.DMA((N_DEV - 1,)), pltpu.SemaphoreType.DMA((N_DEV - 1,)), pltpu.SemaphoreType.DMA],
    )(blocks)


_SEM = pl.BlockSpec(memory_space=pltpu.SEMAPHORE)
_ANY = pl.BlockSpec(memory_space=pl.ANY)
_DATAFLOW = pltpu.SideEffectType.DATAFLOW_SIDE_EFFECTING
N_PEERS = N_DEV - 1


def _peer(rel, x, y, c):
    return (1 - x if rel & 4 else x, 1 - y if rel & 2 else y, 1 - c if rel & 1 else c)


def _send_start(name, srcs, lands, gather, after):
    T = len(srcs)

    def body(*refs):
        src_refs, land_refs = refs[:T], refs[T : 2 * T]
        send_sems, recv_sems = refs[2 * T + 1], refs[2 * T + 2]
        token = refs[-1]
        x, y, c = lax.axis_index("x"), lax.axis_index("y"), lax.axis_index("c")
        for t in range(T):
            for rel in range(1, N_DEV):
                px, py, pc = _peer(rel, x, y, c)
                pltpu.make_async_remote_copy(
                    src_ref=src_refs[t] if gather else src_refs[t].at[4 * px + 2 * py + pc],
                    dst_ref=land_refs[t].at[4 * x + 2 * y + c] if gather else land_refs[t].at[rel - 1],
                    send_sem=send_sems.at[t * N_PEERS + rel - 1], recv_sem=recv_sems.at[t * N_PEERS + rel - 1], device_id=(px, py, pc), device_id_type=MESH,
                ).start()
        token[...] = jnp.zeros_like(token)

    hbm = lambda a: pltpu.HBM(a.shape, a.dtype)
    return pl.pallas_call(
        body,
        name=name,
        out_shape=(pltpu.SemaphoreType.DMA((T * N_PEERS,)), pltpu.SemaphoreType.DMA((T * N_PEERS,)), *[hbm(a) for a in srcs],
                   *[hbm(a) for a in lands], jax.ShapeDtypeStruct((8, LANES), F32)),
        in_specs=(_HBM,) * (2 * T) + (_ANY,),
        out_specs=(_SEM, _SEM) + (_HBM,) * (2 * T) + (pl.BlockSpec(memory_space=pltpu.VMEM),),
        input_output_aliases={t: 2 + t for t in range(2 * T)},
        compiler_params=pltpu.CompilerParams(has_side_effects=_DATAFLOW),
    )(*[pltpu.with_memory_space_constraint(a, pltpu.HBM) for a in list(srcs) + list(lands)], after)


def _send_wait(name, started, after, gather):
    T = (len(started) - 3) // 2
    send_sems, recv_sems, token = started[0], started[1], started[-1]
    thru = started[2:-1]

    def body(*refs):
        src_refs, land_refs = refs[:T], refs[T : 2 * T]
        send_sems, recv_sems = refs[2 * T], refs[2 * T + 1]
        x, y, c = lax.axis_index("x"), lax.axis_index("y"), lax.axis_index("c")
        for t in range(T):
            for rel in range(1, N_DEV):
                cp = pltpu.make_async_remote_copy(
                    src_ref=src_refs[t] if gather else src_refs[t].at[0], dst_ref=land_refs[t].at[0],
                    send_sem=send_sems.at[t * N_PEERS + rel - 1], recv_sem=recv_sems.at[t * N_PEERS + rel - 1], device_id=_peer(rel, x, y, c), device_id_type=MESH,
                )
                cp.wait_send()
                cp.wait_recv()

    outs = pl.pallas_call(
        body,
        name=name,
        out_shape=tuple(pltpu.HBM(a.shape, a.dtype) for a in thru),
        in_specs=(_HBM,) * (2 * T) + (_SEM, _SEM, _ANY),
        out_specs=(_HBM,) * (2 * T),
        input_output_aliases={t: t for t in range(2 * T)},
        compiler_params=pltpu.CompilerParams(has_side_effects=_DATAFLOW),
    )(*thru, send_sems, recv_sems, after)
    return list(outs[:T]), list(outs[T:])


def _sum_blocks(name, parts, tr):
    _, R, Cw = parts[0].shape
    tr = tr if R % tr == 0 else R

    def body(*refs):
        acc = None
        for p_ref in refs[:-1]:
            for d in range(p_ref.shape[0]):
                v = p_ref[d].astype(F32)
                acc = v if acc is None else acc + v
        refs[-1][...] = acc

    return pl.pallas_call(
        body,
        name=name,
        grid=(R // tr,),
        in_specs=[pl.BlockSpec((a.shape[0], tr, Cw), lambda i: (0, i, 0)) for a in parts],
        out_specs=pl.BlockSpec((tr, Cw), lambda i: (i, 0)),
        out_shape=jax.ShapeDtypeStruct((R, Cw), F32),
        compiler_params=_params(1),
    )(*parts)


def _adamw(name, w, g, m, v):
    shape = w.shape
    cols = shape[-1]
    rows = w.size // cols
    tr = rows if rows <= 512 else 512
    assert rows % tr == 0
    w2, g2, m2, v2 = (a.reshape(rows, cols) for a in (w, g, m, v))

    def body(w_ref, g_ref, m_ref, v_ref, d_ref, mo_ref, vo_ref):
        gv = g_ref[...]
        mn = ADAM_B1 * m_ref[...] + (1.0 - ADAM_B1) * gv
        vn = ADAM_B2 * v_ref[...] + (1.0 - ADAM_B2) * jnp.square(gv)
        m_hat = mn / (1.0 - ADAM_B1**ADAM_STEP)
        v_hat = vn / (1.0 - ADAM_B2**ADAM_STEP)
        d_ref[...] = -ADAM_LR * (m_hat / (jnp.sqrt(v_hat) + ADAM_EPS) + ADAM_WD * w_ref[...])
        mo_ref[...] = mn
        vo_ref[...] = vn

    spec = pl.BlockSpec((tr, cols), lambda i: (i, 0))
    d, mn, vn = pl.pallas_call(
        body,
        name=name,
        grid=(rows // tr,),
        in_specs=[spec] * 4,
        out_specs=[spec] * 3,
        out_shape=[jax.ShapeDtypeStruct((rows, cols), F32)] * 3,
        compiler_params=_params(1),
    )(w2, g2, m2, v2)
    return d.reshape(shape), mn.reshape(shape), vn.reshape(shape)


SMALL_SHARDED = ("ln_gain", "ln_bias", "pool_b", "gdn_conv")
SMALL_REPLICATED = ("pool_scale", "gdn_a_log", "gdn_dt_bias", "gdn_norm_w", "ple_gate_b")
WEIGHTS = ("ln_gain", "ln_bias", "pool_w", "pool_b", "pool_scale", "gdn_w_in", "gdn_conv", "gdn_a_log", "gdn_dt_bias",
           "gdn_norm_w", "gdn_w_out", "mlp_w1", "mlp_w2", "ple_gate_w", "ple_gate_b", "ple_proj")
BIG_AXIS = {"gdn_w_in": 1, "gdn_w_out": 0, "mlp_w1": 1, "mlp_w2": 0, "ple_gate_w": 0, "ple_proj": 1, "pool_w": 1}
GATHER_GROUPS = {
    "l0a": (("mlp_w1", 0),),
    "l0b": (("ple_gate_w", 0), ("ple_proj", 0)),
    "l0c": (("mlp_w2", 0),),
    "gdn": (("gdn_w_in", 0), ("gdn_w_out", 0)),
    "l1": (("mlp_w1", 1), ("mlp_w2", 1), ("ple_gate_w", 1), ("ple_proj", 1)),
}
GATHER_AFTER = {"l0b": "l0a", "l0c": "l0a", "gdn": "l0c", "l1": "gdn"}
GRAD_GROUPS = {
    "l1": (("mlp_w1", 1), ("mlp_w2", 1), ("ple_gate_w", 1), ("ple_proj", 1)),
    "gdn": (("gdn_w_in", 0), ("gdn_w_out", 0)),
    "l0": (("mlp_w1", 0), ("mlp_w2", 0), ("ple_gate_w", 0), ("ple_proj", 0)),
}
PACK_PART_ALIGN = 16
SUM_TILE = 128


def _part_rows(a, width):
    rows = a.size // width
    return rows + (-rows) % PACK_PART_ALIGN


def _pack_rows(parts, width, dtype, align):
    padded = []
    for a in parts:
        a2 = a.reshape(-1, width).astype(dtype)
        padded.append(jnp.pad(a2, ((0, _part_rows(a, width) - a2.shape[0]), (0, 0))))
    flat = jnp.concatenate(padded, axis=0)
    return jnp.pad(flat, ((0, (-flat.shape[0]) % align), (0, 0)))


def _pack_blocks(parts, width, dtype, align):
    padded = []
    for a in parts:
        a2 = a.reshape(a.shape[0], -1, width).astype(dtype)
        padded.append(jnp.pad(a2, ((0, 0), (0, _part_rows(a[0], width) - a2.shape[1]), (0, 0))))
    flat = jnp.concatenate(padded, axis=1)
    return jnp.pad(flat, ((0, 0), (0, (-flat.shape[1]) % align), (0, 0)))


def _unpack_rows(packed, shapes, width):
    out, off = [], 0
    for shp in shapes:
        size = 1
        for d in shp:
            size *= d
        out.append(packed[..., off : off + size // width, :].reshape(packed.shape[:-2] + tuple(shp)))
        off += size // width + (-(size // width)) % PACK_PART_ALIGN
    return out


def _split_blocks(name, full):
    ax = BIG_AXIS[name]
    shp = full.shape
    a = full.reshape(shp[:ax] + (N_DEV, shp[ax] // N_DEV) + shp[ax + 1 :])
    return jnp.moveaxis(a, ax, 0)


def _join_blocks(name, blocks):
    ax = BIG_AXIS[name]
    a = jnp.moveaxis(blocks, 0, ax)
    shp = a.shape
    return a.reshape(shp[:ax] + (shp[ax] * shp[ax + 1],) + shp[ax + 2 :])


def _pack_small(parts):
    flat = jnp.concatenate([jnp.pad(a.reshape(-1), (0, (-a.size) % LANES)) for a in parts])
    rows = flat.size // LANES
    return jnp.pad(flat.reshape(rows, LANES), ((0, (-rows) % 8), (0, 0)))


def _unpack_small(packed, shapes):
    flat = packed.reshape(packed.shape[:-2] + (-1,))
    out, off = [], 0
    for shp in shapes:
        size = 1
        for s in shp:
            size *= s
        out.append(flat[..., off : off + size].reshape(flat.shape[:-1] + tuple(shp)))
        off += size + (-size) % LANES
    return out


def _split_w_in(w_in, D, H):
    pad = lambda a: jnp.pad(a, ((0, 0), (0, LANES - H)))
    return w_in[:, : 3 * D], w_in[:, 3 * D : 4 * D], jnp.concatenate([pad(w_in[:, 4 * D : 4 * D + H]), pad(w_in[:, 4 * D + H :])], axis=1)


def kernel(x, p, ln_gain, ln_bias, pool_w, pool_b, pool_scale, gdn_w_in, gdn_conv, gdn_a_log, gdn_dt_bias, gdn_norm_w, gdn_w_out, mlp_w1, mlp_w2, ple_gate_w, ple_gate_b, ple_proj, loss_target, m_ln_gain, m_ln_bias, m_pool_w, m_pool_b, m_pool_scale, m_gdn_w_in, m_gdn_conv, m_gdn_a_log, m_gdn_dt_bias, m_gdn_norm_w, m_gdn_w_out, m_mlp_w1, m_mlp_w2, m_ple_gate_w, m_ple_gate_b, m_ple_proj, v_ln_gain, v_ln_bias, v_pool_w, v_pool_b, v_pool_scale, v_gdn_w_in, v_gdn_conv, v_gdn_a_log, v_gdn_dt_bias, v_gdn_norm_w, v_gdn_w_out, v_mlp_w1, v_mlp_w2, v_ple_gate_w, v_ple_gate_b, v_ple_proj):
    w_sh = dict(ln_gain=ln_gain, ln_bias=ln_bias, pool_w=pool_w, pool_b=pool_b, pool_scale=pool_scale, gdn_w_in=gdn_w_in,
                gdn_conv=gdn_conv, gdn_a_log=gdn_a_log, gdn_dt_bias=gdn_dt_bias, gdn_norm_w=gdn_norm_w, gdn_w_out=gdn_w_out,
                mlp_w1=mlp_w1, mlp_w2=mlp_w2, ple_gate_w=ple_gate_w, ple_gate_b=ple_gate_b, ple_proj=ple_proj)
    m_sh = dict(ln_gain=m_ln_gain, ln_bias=m_ln_bias, pool_w=m_pool_w, pool_b=m_pool_b, pool_scale=m_pool_scale, gdn_w_in=m_gdn_w_in,
                gdn_conv=m_gdn_conv, gdn_a_log=m_gdn_a_log, gdn_dt_bias=m_gdn_dt_bias, gdn_norm_w=m_gdn_norm_w, gdn_w_out=m_gdn_w_out,
                mlp_w1=m_mlp_w1, mlp_w2=m_mlp_w2, ple_gate_w=m_ple_gate_w, ple_gate_b=m_ple_gate_b, ple_proj=m_ple_proj)
    v_sh = dict(ln_gain=v_ln_gain, ln_bias=v_ln_bias, pool_w=v_pool_w, pool_b=v_pool_b, pool_scale=v_pool_scale, gdn_w_in=v_gdn_w_in,
                gdn_conv=v_gdn_conv, gdn_a_log=v_gdn_a_log, gdn_dt_bias=v_gdn_dt_bias, gdn_norm_w=v_gdn_norm_w, gdn_w_out=v_gdn_w_out,
                mlp_w1=v_mlp_w1, mlp_w2=v_mlp_w2, ple_gate_w=v_ple_gate_w, ple_gate_b=v_ple_gate_b, ple_proj=v_ple_proj)
    xs, tg = x[0], loss_target[0]
    ps = p[:, 0]
    S, D = xs.shape
    H = D // HEAD_DIM
    me = 4 * lax.axis_index("x") + 2 * lax.axis_index("y") + lax.axis_index("c")
    layer = lambda n, l: (w_sh[n][0] if n in ("gdn_w_in", "gdn_w_out") else w_sh[n][l])

    pool_packed = _pack_rows([w_sh["pool_w"][0]], D, BF16, PACK_PART_ALIGN)
    small_packed = _pack_small([w_sh[n] for n in SMALL_SHARDED])
    pool_gathered, small_gathered = _all_gather("gather_first", [pool_packed, small_packed])
    W = {"pool_w": _join_blocks("pool_w", _unpack_rows(pool_gathered, [w_sh["pool_w"][0].shape], D)[0])}

    started = {}

    def start(g, after):
        src = _pack_rows([layer(n, l) for n, l in GATHER_GROUPS[g]], D, BF16, PACK_PART_ALIGN)
        started[g] = tuple(_send_start(f"gather_{g}_start", [src], [lax.empty((N_DEV,) + src.shape, BF16)], True, after))
        return started[g][-1]

    first_token = start("l0a", small_gathered)
    smalls = _unpack_small(small_gathered, [w_sh[n].shape for n in SMALL_SHARDED])
    for n, a in zip(SMALL_SHARDED, smalls):
        W[n] = jnp.moveaxis(a, 0, -2).reshape(a.shape[1:-1] + (N_DEV * a.shape[-1],))
    W["ln_gain"] = W["ln_gain"].reshape(2 * DEPTH, D)
    W["ln_bias"] = W["ln_bias"].reshape(2 * DEPTH, D)
    W["pool_b"] = W["pool_b"].reshape(1, D) + first_token[0:1, 0:1]
    W["gdn_conv"] = W["gdn_conv"][0]
    W["pool_scale"] = pool_scale
    W["ple_gate_b"] = ple_gate_b
    W["gdn_norm_w"] = gdn_norm_w
    W["gdn_a_log"] = jnp.pad(gdn_a_log, ((0, 0), (0, LANES - H)))
    W["gdn_dt_bias"] = jnp.pad(gdn_dt_bias, ((0, 0), (0, LANES - H)))

    def fetch(g, after):
        members = GATHER_GROUPS[g]
        (src,), (land,) = _send_wait(f"gather_{g}_wait", started[g], after, True)
        tokens = [start(nxt, land) for nxt, prev in GATHER_AFTER.items() if prev == g]
        land = lax.dynamic_update_index_in_dim(land, src, me, 0)
        parts = _unpack_rows(land, [layer(n, l).shape for n, l in members], D)
        out = {n: _join_blocks(n, a) for (n, _), a in zip(members, parts)}
        if "gdn_w_in" in out:
            out["gdn_wqkv"], out["gdn_wz"], out["gdn_wba"] = _split_w_in(out.pop("gdn_w_in"), D, H)
        out["_after"] = tokens
        return out

    sent = {}

    def emit(g, grads):
        srcs = [grads[n] for n, _ in GRAD_GROUPS[g]]
        lands = [lax.empty((N_PEERS,) + a.shape[1:], BF16) for a in srcs]
        sent[g] = tuple(_send_start(f"grads_{g}_start", srcs, lands, False, srcs[0]))
        return sent[g][-1]

    loss_cols, grad_x, G = _local_step(xs, ps, tg, W, fetch, emit)
    loss = lax.psum(0.5 * jnp.sum(loss_cols) / D, MESH_AXES)

    pool_src = _pack_blocks([_split_blocks("pool_w", G["pool_w"])], D, BF16, PACK_PART_ALIGN)
    pool_sum = _sum_blocks("sum_pool_grads", [_exchange("exchange_pool_grads", pool_src)], SUM_TILE)
    grads = {"pool_w": _unpack_rows(pool_sum, [w_sh["pool_w"][0].shape], D)[0].reshape(w_sh["pool_w"].shape)}
    small_names = SMALL_SHARDED + SMALL_REPLICATED
    gs_packed = _pack_small([G[n] for n in small_names])
    (gs_all,) = _all_gather("gather_small_grads", [gs_packed])
    gs_sum = _sum_blocks("sum_small_grads", [gs_all], SUM_TILE)
    for n, a in zip(small_names, _unpack_small(gs_sum, [G[n].shape for n in small_names])):
        if n in SMALL_SHARDED:
            width = w_sh[n].shape[-1]
            a = a.reshape(w_sh[n].shape[:-1] + (N_DEV * width,))
            a = lax.dynamic_slice_in_dim(a, me * width, width, axis=a.ndim - 1)
        grads[n] = a.reshape(w_sh[n].shape)

    per_layer = {}
    for g, members in GRAD_GROUPS.items():
        srcs, lands = _send_wait(f"grads_{g}_wait", sent[g], grad_x, False)
        for (n, l), src, land in zip(members, srcs, lands):
            own = lax.dynamic_index_in_dim(src, me, 0, keepdims=True)
            as3d = lambda a: a.reshape(a.shape[0], -1, a.shape[-1])
            per_layer[(n, l)] = _sum_blocks(f"sum_grads_{n}_{l}", [as3d(land), as3d(own)], SUM_TILE).reshape(layer(n, l).shape)
    for n in ("gdn_w_in", "gdn_w_out"):
        grads[n] = per_layer[(n, 0)][None]
    for n in ("mlp_w1", "mlp_w2", "ple_gate_w", "ple_proj"):
        grads[n] = jnp.stack([per_layer[(n, 0)], per_layer[(n, 1)]])

    deltas, new_m, new_v = {}, {}, {}
    for n in WEIGHTS:
        deltas[n], new_m[n], new_v[n] = _adamw(f"adamw_{n}", w_sh[n], grads[n], m_sh[n], v_sh[n])
    return (loss, grad_x[None], *[grads[n] for n in WEIGHTS], *[deltas[n] for n in WEIGHTS],
            *[new_m[n] for n in WEIGHTS], *[new_v[n] for n in WEIGHTS])
```

```python
import functools

import jax
import jax.numpy as jnp
from jax import lax
from jax.experimental import pallas as pl
from jax.experimental.pallas import tpu as pltpu

F32 = jnp.float32
BF16 = jnp.bfloat16
MESH_AXES = ("x", "y", "c")
N_DEV = 8
MESH = pl.DeviceIdType.MESH

DEPTH = 2
ALPHA = (2.0 * DEPTH) ** 0.25
LN_EPS = 1e-5
RMS_EPS = 1e-6
L2_EPS = 1e-6
HEAD_DIM = 128
CONV_WIDTH = 4
POOL_WINDOWS = (2, 4, 8, 16)
POOL_HALO = 16
CONV_HALO = 8
LANES = 128
ADAM_LR = 0.001
ADAM_B1 = 0.9
ADAM_B2 = 0.999
ADAM_EPS = 1e-08
ADAM_WD = 0.01
ADAM_STEP = 10

VMEM_LIMIT = 56 * 1024 * 1024
ROW_TILE = 512
CONV_TILE = 256
CHUNK = 128
MM_TM, MM_TN, MM_TK = 512, 1024, 1024
DW_TILES = dict(tm=256, tn=512, tk=8192, b_outer=True)

_DIMS = {
    "nn": (((1,), (0,)), ((), ())),
    "nt": (((1,), (1,)), ((), ())),
    "tn": (((0,), (0,)), ((), ())),
}


def _params(n_axes):
    return pltpu.CompilerParams(dimension_semantics=("arbitrary",) * n_axes, vmem_limit_bytes=VMEM_LIMIT)


def _fit(tile, n):
    tile = min(tile, n)
    while n % tile:
        tile //= 2
    return tile


def _mm(name, a, b, mode, out_dtypes, epi=None, extras=(), a_fn=None, tm=None, tn=None, tk=None, b_outer=False, after=(), out_blocks=1, accs=0):
    if mode == "tn":
        K, M = a.shape
    else:
        M, K = a.shape
    N = b.shape[0] if mode == "nt" else b.shape[1]
    tm, tn, tk = _fit(tm or MM_TM, M), (N // out_blocks if out_blocks > 1 else N if accs else _fit(tn or MM_TN, N)), _fit(tk or MM_TK, K)
    nk = K // tk

    def at(f):
        return (lambda j, i, k: f(i, j, k)) if b_outer else f

    a_spec = pl.BlockSpec((tk, tm), at(lambda i, j, k: (k, i))) if mode == "tn" else pl.BlockSpec((tm, tk), at(lambda i, j, k: (i, k)))
    b_spec = pl.BlockSpec((tn, tk), at(lambda i, j, k: (j, k))) if mode == "nt" else pl.BlockSpec((tk, tn), at(lambda i, j, k: (k, j)))
    ex_spec = {"tile": pl.BlockSpec((tm, tn), at(lambda i, j, k: (i, j))), "row": pl.BlockSpec((1, tn), at(lambda i, j, k: (0, j))),
               "rows": pl.BlockSpec((tm, LANES), at(lambda i, j, k: (i, 0)))}
    ex_specs = [ex_spec[kind] for _, kind in extras]
    assert accs == 0 or (tn == N and nk == 1 and not b_outer), name
    n_ex, n_out, n_after = len(extras), len(out_dtypes), len(after)

    def body(*refs):
        a_ref, b_ref = refs[0], refs[1]
        ex_refs = refs[2 : 2 + n_ex]
        out_refs = refs[2 + n_ex + n_after : 2 + n_ex + n_after + n_out]
        av = a_ref[...]
        if a_fn is not None:
            av = a_fn(av)
        part = lax.dot_general(av.astype(BF16), b_ref[...].astype(BF16), _DIMS[mode], preferred_element_type=F32)

        def finish(res):
            vals = epi(res, *[e[...] for e in ex_refs]) if epi is not None else (res,)
            for o_ref, v in zip(out_refs, vals[:n_out]):
                o_ref[...] = v.astype(o_ref.dtype)
            for a_ref, v in zip(refs[2 + n_ex + n_after + n_out :], vals[n_out:]):

                @pl.when(pl.program_id(0) == 0)
                def _(a_ref=a_ref, v=v):
                    a_ref[...] = v

                @pl.when(pl.program_id(0) > 0)
                def _(a_ref=a_ref, v=v):
                    a_ref[...] += v

        if nk == 1:
            finish(part)
        else:
            acc = refs[-1]
            k = pl.program_id(2)

            @pl.when(k == 0)
            def _():
                acc[...] = part

            @pl.when(k > 0)
            def _():
                acc[...] += part

            @pl.when(k == nk - 1)
            def _():
                finish(acc[...])

    outs = pl.pallas_call(
        body,
        name=name,
        grid=(N // tn, M // tm, nk) if b_outer else (M // tm, N // tn, nk),
        in_specs=[a_spec, b_spec] + ex_specs + [pl.BlockSpec(memory_space=pl.ANY)] * n_after,
        out_specs=[pl.BlockSpec((tm, LANES), at(lambda i, j, k: (i, 0))) if isinstance(dt, tuple)
                   else pl.BlockSpec((tm, tn), at(lambda i, j, k: (i, j))) if out_blocks == 1
                   else pl.BlockSpec((None, tm, tn), at(lambda i, j, k: (j, i, 0))) for dt in out_dtypes]
        + [pl.BlockSpec((1, N), lambda i, j, k: (0, 0))] * accs,
        out_shape=[jax.ShapeDtypeStruct((M, LANES), dt[0]) if isinstance(dt, tuple)
                   else jax.ShapeDtypeStruct((M, N) if out_blocks == 1 else (out_blocks, M, tn), dt) for dt in out_dtypes]
        + [jax.ShapeDtypeStruct((1, N), F32)] * accs,
        scratch_shapes=[pltpu.VMEM((tm, tn), F32)] if nk > 1 else [],
        compiler_params=_params(3),
    )(a, b, *[e for e, _ in extras], *after)
    return outs[0] if n_out + accs == 1 else outs


def _rowwise(name, fn, S, ts, rows=(), halos=(), consts=(), outs=(), accs=()):
    ts = min(ts, S)
    assert S % ts == 0
    n = S // ts
    in_specs = [pl.BlockSpec((ts, a.shape[1]), lambda i: (i, 0)) for a in rows]
    for a, kind, hr in halos:
        r, nb = ts // hr, S // hr
        if kind == "prev":
            in_specs.append(pl.BlockSpec((hr, a.shape[1]), lambda i, r=r: (jnp.maximum(i * r - 1, 0), 0)))
        else:
            in_specs.append(pl.BlockSpec((hr, a.shape[1]), lambda i, r=r, nb=nb: (jnp.minimum((i + 1) * r, nb - 1), 0)))
    in_specs += [pl.BlockSpec(a.shape, lambda i, nd=a.ndim: (0,) * nd) for a in consts]
    out_specs = [pl.BlockSpec((ts, w), lambda i: (i, 0)) for w, _ in outs]
    out_specs += [pl.BlockSpec((r, w), lambda i: (0, 0)) for r, w in accs]
    out_shape = [jax.ShapeDtypeStruct((S, w), dt) for w, dt in outs]
    out_shape += [jax.ShapeDtypeStruct((r, w), F32) for r, w in accs]
    nr, nh, nc, no = len(rows), len(halos), len(consts), len(outs)

    def body(*refs):
        i = pl.program_id(0)
        rv = [r[...] for r in refs[:nr]]
        hv = [r[...] for r in refs[nr : nr + nh]]
        cv = [r[...] for r in refs[nr + nh : nr + nh + nc]]
        o_refs = refs[nr + nh + nc : nr + nh + nc + no]
        a_refs = refs[nr + nh + nc + no :]
        ovals, avals = fn(i, n, rv, hv, cv)
        for o_ref, v in zip(o_refs, ovals):
            o_ref[...] = v.astype(o_ref.dtype)
        for a_ref, v in zip(a_refs, avals):

            @pl.when(i == 0)
            def _(a_ref=a_ref, v=v):
                a_ref[...] = v

            @pl.when(i > 0)
            def _(a_ref=a_ref, v=v):
                a_ref[...] += v

    res = pl.pallas_call(
        body,
        name=name,
        grid=(n,),
        in_specs=in_specs,
        out_specs=out_specs,
        out_shape=out_shape,
        compiler_params=_params(1),
    )(*rows, *[h[0] for h in halos], *consts)
    return list(res)


def _ln(h, g, b):
    mu = jnp.mean(h, axis=-1, keepdims=True)
    d = h - mu
    var = jnp.mean(d * d, axis=-1, keepdims=True)
    rstd = lax.rsqrt(var + LN_EPS)
    xhat = d * rstd
    return xhat, rstd, xhat * g + b


def _ln_bwd(dy, xhat, rstd, g):
    dxh = dy * g
    m1 = jnp.mean(dxh, axis=-1, keepdims=True)
    m2 = jnp.mean(dxh * xhat, axis=-1, keepdims=True)
    dh = rstd * (dxh - m1 - xhat * m2)
    return dh, jnp.sum(dy * xhat, axis=0, keepdims=True), jnp.sum(dy, axis=0, keepdims=True)


def _wide(col, ts):
    return jnp.broadcast_to(col, (ts, LANES))


def _pool_fwd(x, wp, pb, ps, g, b):
    S, D = x.shape
    gw = D // len(POOL_WINDOWS)
    ts = min(ROW_TILE, S)

    def fn(i, n, rv, hv, cv):
        (xc,), (xp,) = rv, hv
        wpv, pbv, psv, gv, bv = cv
        xp = jnp.where(i > 0, xp, 0.0)
        xx = jnp.concatenate([xp, xc], axis=0)
        t = i * ts + lax.broadcasted_iota(jnp.int32, (ts, 1), 0)
        pooled, ys = [], []
        for gi, w in enumerate(POOL_WINDOWS):
            s = xx[:, gi * gw : (gi + 1) * gw]
            k = 1
            while k < w:
                s = s + pltpu.roll(s, k, axis=0)
                k *= 2
            cnt = jnp.minimum(t + 1, w).astype(F32)
            pg = (s[POOL_HALO:, :] / cnt - xc[:, gi * gw : (gi + 1) * gw]).astype(BF16)
            pooled.append(pg)
            ys.append(jnp.dot(pg, wpv[gi], preferred_element_type=F32))
        y = jnp.concatenate(ys, axis=1)
        h = ALPHA * xc + (y + pbv) * psv
        xhat, rstd, xa = _ln(h, gv, bv)
        return (jnp.concatenate(pooled, axis=1), xhat, _wide(rstd, ts), xa), ()

    return _rowwise(
        "pool_fwd", fn, S, ts, rows=[x], halos=[(x, "prev", POOL_HALO)], consts=[wp, pb, ps, g, b],
        outs=[(D, BF16), (D, F32), (LANES, F32), (D, BF16)],
    )


def _pool_bwd(dh, pooled, wp, pb, ps):
    S, D = dh.shape
    gw = D // len(POOL_WINDOWS)
    ts = min(ROW_TILE, S)
    te = ts + POOL_HALO

    def fn(i, n, rv, hv, cv):
        (dhc, pc), (dhn,) = rv, hv
        wpv, pbv, psv = cv
        dhn = jnp.where(i < n - 1, dhn, 0.0)
        dy_ext = jnp.concatenate([dhc, dhn], axis=0) * psv
        dyb = dy_ext.astype(BF16)
        t = i * ts + lax.broadcasted_iota(jnp.int32, (te, 1), 0)
        dxs, ys = [], []
        for gi, w in enumerate(POOL_WINDOWS):
            sl = slice(gi * gw, (gi + 1) * gw)
            dp = lax.dot_general(dyb[:, sl], wpv[gi], _DIMS["nt"], preferred_element_type=F32)
            s = dp / jnp.minimum(t + 1, w).astype(F32)
            k = 1
            while k < w:
                s = s + pltpu.roll(s, k, axis=0)
                k *= 2
            s = pltpu.roll(s, POOL_HALO - (w - 1), axis=0)
            dxs.append(s[POOL_HALO:, :] - dp[:ts, :])
            ys.append(jnp.dot(pc[:, sl], wpv[gi], preferred_element_type=F32))
        dx = ALPHA * dhc + jnp.concatenate(dxs, axis=1)
        y = jnp.concatenate(ys, axis=1) + pbv
        dscale = jnp.sum(dhc * y, axis=0, keepdims=True)
        dbias = jnp.sum(dy_ext[:ts, :], axis=0, keepdims=True)
        return (dx, dyb[:ts, :]), (dscale, dbias)

    return _rowwise(
        "pool_bwd", fn, S, ts, rows=[dh, pooled], halos=[(dh, "next", POOL_HALO)], consts=[wp, pb, ps],
        outs=[(D, F32), (D, BF16)], accs=[(1, D), (1, D)],
    )


def _pool_dw(pooled, dy):
    S, D = pooled.shape
    G = len(POOL_WINDOWS)
    gw = D // G
    tk = min(MM_TK, S)
    nk = S // tk

    def body(p_ref, d_ref, o_ref):
        k = pl.program_id(1)
        part = lax.dot_general(p_ref[...], d_ref[...], _DIMS["tn"], preferred_element_type=F32)

        @pl.when(k == 0)
        def _():
            o_ref[...] = part

        @pl.when(k > 0)
        def _():
            o_ref[...] += part

    return pl.pallas_call(
        body,
        name="pool_dw",
        grid=(G, nk),
        in_specs=[pl.BlockSpec((tk, gw), lambda g, k: (k, g)), pl.BlockSpec((tk, gw), lambda g, k: (k, g))],
        out_specs=pl.BlockSpec((None, gw, gw), lambda g, k: (g, 0, 0)),
        out_shape=jax.ShapeDtypeStruct((G, gw, gw), F32),
        compiler_params=_params(2),
    )(pooled, dy)


def _res_ln_epi(acc, xh, gp_, bp_, g, b):
    xhat, rstd, xo = _ln(ALPHA * (xh * gp_ + bp_) + acc, g, b)
    return xhat, _wide(rstd, acc.shape[0]), xo


def _res_ln_ffpe_epi(acc, xh, gate, pp, gp_, bp_, g, b):
    return _res_ln_epi(acc + jax.nn.sigmoid(gate) * pp, xh, gp_, bp_, g, b)


def _final_ln_loss_epi(acc, xh, gate, pp, tgt, gp_, bp_, g, b):
    sg = jax.nn.sigmoid(gate)
    xhat, rstd, y = _ln(ALPHA * (xh * gp_ + bp_) + acc + sg * pp, g, b)
    e = y - tgt
    dh, dg, db = _ln_bwd(e * (1.0 / acc.shape[1]), xhat, rstd, g)
    dgt, dpp, dbg = _ple_grads(dh, sg, pp)
    return dh, dh, dgt, dpp, jnp.sum(e * e, axis=0, keepdims=True), dg, db, dbg


def _ln_bwd_epi(acc, rest, xhat, rstd, g):
    dh, dg, db = _ln_bwd(acc + rest, xhat, rstd[:, :1], g)
    return dh, dh, dg, db


def _ple_grads(dh, sg, pp):
    dgt = dh * pp * sg * (1.0 - sg)
    return dgt, dh * sg, jnp.sum(dgt, axis=0, keepdims=True)


def _ln_bwd_ple_epi(acc, rest, xhat, rstd, gate, pp, g):
    dh, dg, db = _ln_bwd(acc + rest, xhat, rstd[:, :1], g)
    dgt, dpp, dbg = _ple_grads(dh, jax.nn.sigmoid(gate), pp)
    return dh, dh, dgt, dpp, dg, db, dbg


def _silu(c):
    return c * jax.nn.sigmoid(c)


def _qkv_point(c, is_qk, scale):
    s = _silu(c)
    nrm = s * lax.rsqrt(jnp.sum(s * s, axis=-1, keepdims=True) + L2_EPS) * scale
    return jnp.where(is_qk, nrm, s)


def _conv_rows(xx, wv, lo, rows):
    acc = None
    for j in range(CONV_WIDTH):
        sh = CONV_WIDTH - 1 - j
        term = (pltpu.roll(xx, sh, axis=0) if sh else xx)[lo : lo + rows, :] * wv[j : j + 1, :]
        acc = term if acc is None else acc + term
    return acc


def _conv_fwd(qkv_pre, conv_w):
    S, W = qkv_pre.shape
    D = W // 3
    H = D // HEAD_DIM
    ts = min(CONV_TILE, S)
    r = ts // CONV_HALO

    def body(x_ref, xp_ref, w_ref, o_ref):
        j, i = pl.program_id(0), pl.program_id(1)
        xp = jnp.where(i > 0, xp_ref[...], 0.0)
        xx = jnp.concatenate([xp, x_ref[...]], axis=0)
        c = _conv_rows(xx, w_ref[...], CONV_HALO, ts)
        scale = jnp.where(j == 0, HEAD_DIM**-0.5, 1.0).astype(F32)
        for h in range(H):
            sl = slice(h * HEAD_DIM, (h + 1) * HEAD_DIM)
            o_ref[:, sl] = _qkv_point(c[:, sl], j < 2, scale)

    return pl.pallas_call(
        body,
        name="gdn_conv_fwd",
        grid=(3, S // ts),
        in_specs=[
            pl.BlockSpec((ts, D), lambda j, i: (i, j)),
            pl.BlockSpec((CONV_HALO, D), lambda j, i: (jnp.maximum(i * r - 1, 0), j)),
            pl.BlockSpec((CONV_WIDTH, D), lambda j, i: (0, j)),
        ],
        out_specs=pl.BlockSpec((ts, D), lambda j, i: (i, j)),
        out_shape=jax.ShapeDtypeStruct((S, W), F32),
        compiler_params=_params(2),
    )(qkv_pre, qkv_pre, conv_w)


def _conv_bwd(qkv_pre, conv_w, dqkvn):
    S, W = qkv_pre.shape
    D = W // 3
    H = D // HEAD_DIM
    ts = min(CONV_TILE, S)
    r, nb = ts // CONV_HALO, S // CONV_HALO
    te = ts + CONV_HALO

    def body(x_ref, xp_ref, xn_ref, w_ref, d_ref, dn_ref, dx_ref, dw_ref):
        j, i = pl.program_id(0), pl.program_id(1)
        n = pl.num_programs(1)
        wv = w_ref[...]
        xp = jnp.where(i > 0, xp_ref[...], 0.0)
        xx = jnp.concatenate([xp, x_ref[...], xn_ref[...]], axis=0)
        xr = [pltpu.roll(xx, sh, axis=0) if sh else xx for sh in range(CONV_WIDTH)]
        c = None
        for jj in range(CONV_WIDTH):
            term = xr[CONV_WIDTH - 1 - jj][CONV_HALO : CONV_HALO + te, :] * wv[jj : jj + 1, :]
            c = term if c is None else c + term
        dn = jnp.where(i < n - 1, dn_ref[...], 0.0)
        dout = jnp.concatenate([d_ref[...], dn], axis=0)
        scale = jnp.where(j == 0, HEAD_DIM**-0.5, 1.0).astype(F32)
        dcs = []
        for h in range(H):
            sl = slice(h * HEAD_DIM, (h + 1) * HEAD_DIM)
            _, vjp = jax.vjp(lambda cc: _qkv_point(cc, j < 2, scale), c[:, sl])
            dcs.append(vjp(dout[:, sl])[0])
        dc = jnp.concatenate(dcs, axis=1)
        dx = None
        dws = []
        for jj in range(CONV_WIDTH):
            sh = CONV_WIDTH - 1 - jj
            term = pltpu.roll(dc, CONV_HALO - sh, axis=0)[CONV_HALO:, :] * wv[jj : jj + 1, :]
            dx = term if dx is None else dx + term
            dws.append(jnp.sum(dc[:ts, :] * xr[sh][CONV_HALO : CONV_HALO + ts, :], axis=0, keepdims=True))
        dx_ref[...] = dx.astype(dx_ref.dtype)
        dw = jnp.concatenate(dws, axis=0)

        @pl.when(i == 0)
        def _():
            dw_ref[...] = dw

        @pl.when(i > 0)
        def _():
            dw_ref[...] += dw

    return pl.pallas_call(
        body,
        name="gdn_conv_bwd",
        grid=(3, S // ts),
        in_specs=[
            pl.BlockSpec((ts, D), lambda j, i: (i, j)),
            pl.BlockSpec((CONV_HALO, D), lambda j, i: (jnp.maximum(i * r - 1, 0), j)),
            pl.BlockSpec((CONV_HALO, D), lambda j, i: (jnp.minimum((i + 1) * r, nb - 1), j)),
            pl.BlockSpec((CONV_WIDTH, D), lambda j, i: (0, j)),
            pl.BlockSpec((ts, D), lambda j, i: (i, j)),
            pl.BlockSpec((CONV_HALO, D), lambda j, i: (jnp.minimum((i + 1) * r, nb - 1), j)),
        ],
        out_specs=[pl.BlockSpec((ts, D), lambda j, i: (i, j)), pl.BlockSpec((CONV_WIDTH, D), lambda j, i: (0, j))],
        out_shape=[jax.ShapeDtypeStruct((S, W), BF16), jax.ShapeDtypeStruct((CONV_WIDTH, W), F32)],
        compiler_params=_params(2),
    )(qkv_pre, qkv_pre, qkv_pre, conv_w, dqkvn, dqkvn)


def _softplus(x):
    pos = x > 0.0
    return jnp.where(pos, x, 0.0) + jnp.log(1.0 + jnp.exp(jnp.where(pos, -x, x)))


def _gates(bl, al, alog, dt):
    return jax.nn.sigmoid(bl), -jnp.exp(alog) * _softplus(al + dt)


def _gates_fwd(ba, alog, dt):
    S = ba.shape[0]
    ts = min(ROW_TILE, S)

    def fn(i, n, rv, hv, cv):
        return _gates(rv[0][:, :LANES], rv[0][:, LANES:], cv[0], cv[1]), ()

    return _rowwise("gdn_gates_fwd", fn, S, ts, rows=[ba], consts=[alog, dt], outs=[(LANES, F32), (LANES, F32)])


def _gates_bwd(ba, alog, dt, dbeta, dg, H):
    S = ba.shape[0]
    ts = min(ROW_TILE, S)

    def fn(i, n, rv, hv, cv):
        bav, dbv, dgv = rv
        real = lax.broadcasted_iota(jnp.int32, (1, LANES), 1) < H
        _, vjp = jax.vjp(_gates, bav[:, :LANES], bav[:, LANES:], cv[0], cv[1])
        dbl, dal, dalog, ddt = vjp((jnp.where(real, dbv, 0.0), jnp.where(real, dgv, 0.0)))
        dbl, dal = jnp.where(real, dbl, 0.0), jnp.where(real, dal, 0.0)
        return (jnp.concatenate([dbl, dal], axis=1),), (jnp.where(real, dalog, 0.0), jnp.where(real, ddt, 0.0))

    return _rowwise(
        "gdn_gates_bwd", fn, S, ts, rows=[ba, dbeta, dg], consts=[alog, dt], outs=[(2 * LANES, BF16)],
        accs=[(1, LANES), (1, LANES)],
    )


def _split_bf16(a, n):
    parts, rest = [], a
    for _ in range(n):
        piece = rest.astype(BF16)
        parts.append(piece)
        rest = rest - piece.astype(F32)
    return parts


def _tri_dot(a, b, mode, tri):
    d = lambda u, v: lax.dot_general(u, v, _DIMS[mode], preferred_element_type=F32)
    if tri == 0:
        return sum(d(a.astype(BF16), piece) for piece in _split_bf16(b, 3))
    return sum(d(piece, b.astype(BF16)) for piece in _split_bf16(a, 3))


def _bdot_raw(a, b, mode):
    return lax.dot_general(a.astype(BF16), b.astype(BF16), _DIMS[mode], preferred_element_type=F32)


@functools.partial(jax.custom_vjp, nondiff_argnums=(2,))
def _bdot(a, b, mode):
    return _bdot_raw(a, b, mode)


def _bdot_fwd(a, b, mode):
    return _bdot_raw(a, b, mode), (a, b)


def _bdot_bwd(mode, res, ct):
    a, b = res
    if mode == "nn":
        return _bdot(ct, b, "nt"), _bdot(a, ct, "tn")
    if mode == "nt":
        return _bdot(ct, b, "nn"), _bdot(ct, a, "tn")
    return _bdot(b, ct, "nt"), _bdot(a, ct, "nn")


_bdot.defvjp(_bdot_fwd, _bdot_bwd)


@jax.custom_vjp
def _unit_lower_inverse(a_strict):
    return _unit_lower_inverse_raw(a_strict)


def _unit_lower_inverse_fwd(a_strict):
    t = _unit_lower_inverse_raw(a_strict)
    return t, t


def _unit_lower_inverse_bwd(t, ct):
    left = [_bdot(ti, ci, "tn") for ti, ci in zip(t, ct)]
    return (tuple(-_bdot(li, ti, "nt") for li, ti in zip(left, t)),)


_unit_lower_inverse.defvjp(_unit_lower_inverse_fwd, _unit_lower_inverse_bwd)


@jax.custom_vjp
def _saved_inverse(a_strict, t):
    return t


def _saved_inverse_fwd(a_strict, t):
    return t, t


def _saved_inverse_bwd(t, ct):
    return _unit_lower_inverse_bwd(t, ct) + (tuple(jnp.zeros_like(ti) for ti in t),)


_saved_inverse.defvjp(_saved_inverse_fwd, _saved_inverse_bwd)


def _unit_lower_inverse_raw(a_strict):
    C = a_strict[0].shape[0]
    ii = lax.broadcasted_iota(jnp.int32, (C, C), 0)
    jj = lax.broadcasted_iota(jnp.int32, (C, C), 1)
    eye = (ii == jj).astype(F32)
    blk = 16
    same = (ii // blk) == (jj // blk)
    p = [-jnp.where(same, a, 0.0) for a in a_strict]
    t = [eye + x for x in p]
    for _ in range(3):
        p = [_bdot(x, x, "nn") for x in p]
        t = [ti + _bdot(ti, x, "nn") for ti, x in zip(t, p)]
    while blk < C:
        same2 = (ii // (2 * blk)) == (jj // (2 * blk))
        off = jnp.logical_and(same2, jnp.logical_not(same))
        te = [_bdot(ti, jnp.where(off, a, 0.0), "nn") for ti, a in zip(t, a_strict)]
        t = [ti - _bdot(x, ti, "nn") for ti, x in zip(t, te)]
        same, blk = same2, 2 * blk
    return tuple(t)


def _chunk_heads(q, k, v, gc_col, gc_row, b_col, s0, t_saved=None, with_t=False):
    R = range(len(q))
    C = q[0].shape[0]
    ii = lax.broadcasted_iota(jnp.int32, (C, C), 0)
    jj = lax.broadcasted_iota(jnp.int32, (C, C), 1)
    rows = lax.broadcasted_iota(jnp.int32, (C, 1), 0)
    decay = [jnp.where(ii >= jj, jnp.exp(jnp.minimum(gc_col[h] - gc_row[h], 0.0)), 0.0) for h in R]
    kb = [k[h] * b_col[h] for h in R]
    a = [_bdot(kb[h], k[h], "nt") * decay[h] for h in R]
    qk = [_bdot(q[h], k[h], "nt") * decay[h] for h in R]
    a_strict = tuple(jnp.where(ii > jj, a[h], 0.0) for h in R)
    t = _unit_lower_inverse(a_strict) if t_saved is None else _saved_inverse(a_strict, t_saved)
    eg = [jnp.exp(gc_col[h]) for h in R]
    u = [_bdot(t[h], v[h] * b_col[h], "nn") for h in R]
    w = [_bdot(t[h], kb[h] * eg[h], "nn") for h in R]
    g_last = [jnp.sum(jnp.where(rows == C - 1, gc_col[h], 0.0), axis=0, keepdims=True) for h in R]
    kd = [k[h] * jnp.exp(g_last[h] - gc_col[h]) for h in R]
    ws = [_bdot(w[h], s0[h], "nn") for h in R]
    qs = [_bdot(q[h] * eg[h], s0[h], "nn") for h in R]
    v_new = [u[h] - ws[h] for h in R]
    o = [qs[h] + _bdot(qk[h], v_new[h], "nn") for h in R]
    s1 = [s0[h] * jnp.exp(g_last[h]) + _bdot(kd[h], v_new[h], "tn") for h in R]
    return (tuple(o), tuple(s1), t) if with_t else (tuple(o), tuple(s1))


def _pick_lane(a, h):
    lanes = lax.broadcasted_iota(jnp.int32, a.shape, 1)
    return jnp.sum(jnp.where(lanes == h, a, 0.0), axis=1, keepdims=True)


def _pick_row(a, h):
    rows = lax.broadcasted_iota(jnp.int32, a.shape, 0)
    return jnp.sum(jnp.where(rows == h, a, 0.0), axis=0, keepdims=True)


def _tri(C):
    ii = lax.broadcasted_iota(jnp.int32, (C, C), 0)
    jj = lax.broadcasted_iota(jnp.int32, (C, C), 1)
    return (ii >= jj).astype(F32)


def _delta_fwd(qkvn, g_pad, g_rows, beta_pad):
    S, W = qkvn.shape
    D = W // 3
    H = D // HEAD_DIM
    C = min(CHUNK, S)
    N = S // C

    def body(x_ref, gp_ref, gr_ref, bp_ref, o_ref, sall_ref, tall_ref, st):
        n = pl.program_id(0)

        @pl.when(n == 0)
        def _():
            st[...] = jnp.zeros_like(st)

        low = _tri(C)
        gc_cols = _tri_dot(low, gp_ref[...], "nn", 0)
        gc_rows = _tri_dot(gr_ref[...], low, "nt", 1)
        bcols = bp_ref[...]
        hs = range(H)
        s0 = tuple(st[h] for h in hs)
        for h in hs:
            sall_ref[h] = s0[h]
        o, s1, t = _chunk_heads(
            tuple(x_ref[:, h * HEAD_DIM : (h + 1) * HEAD_DIM] for h in hs),
            tuple(x_ref[:, D + h * HEAD_DIM : D + (h + 1) * HEAD_DIM] for h in hs),
            tuple(x_ref[:, 2 * D + h * HEAD_DIM : 2 * D + (h + 1) * HEAD_DIM] for h in hs),
            tuple(_pick_lane(gc_cols, h) for h in hs), tuple(_pick_row(gc_rows, h) for h in hs),
            tuple(_pick_lane(bcols, h) for h in hs), s0, with_t=True,
        )
        for h in hs:
            st[h] = s1[h]
            o_ref[:, h * HEAD_DIM : (h + 1) * HEAD_DIM] = o[h]
            tall_ref[h] = t[h].astype(tall_ref.dtype)

    return pl.pallas_call(
        body,
        name="gdn_delta_fwd",
        grid=(N,),
        in_specs=[
            pl.BlockSpec((C, W), lambda n: (n, 0)),
            pl.BlockSpec((C, LANES), lambda n: (n, 0)),
            pl.BlockSpec((None, 8, C), lambda n: (n, 0, 0)),
            pl.BlockSpec((C, LANES), lambda n: (n, 0)),
        ],
        out_specs=[pl.BlockSpec((C, D), lambda n: (n, 0)), pl.BlockSpec((None, H, HEAD_DIM, HEAD_DIM), lambda n: (n, 0, 0, 0)),
                   pl.BlockSpec((None, H, C, C), lambda n: (n, 0, 0, 0))],
        out_shape=[jax.ShapeDtypeStruct((S, D), F32), jax.ShapeDtypeStruct((N, H, HEAD_DIM, HEAD_DIM), F32), jax.ShapeDtypeStruct((N, H, C, C), BF16)],
        scratch_shapes=[pltpu.VMEM((H, HEAD_DIM, HEAD_DIM), F32)],
        compiler_params=_params(1),
    )(qkvn, g_pad, g_rows, beta_pad)


def _delta_bwd(qkvn, g_pad, g_rows, beta_pad, s_all, t_all, do):
    S, W = qkvn.shape
    D = W // 3
    H = D // HEAD_DIM
    C = min(CHUNK, S)
    N = S // C

    def body(x_ref, gp_ref, gr_ref, bp_ref, sall_ref, tall_ref, do_ref, dx_ref, dgp_ref, dgr_ref, dbp_ref, dst):
        n = pl.program_id(0)

        @pl.when(n == 0)
        def _():
            dst[...] = jnp.zeros_like(dst)

        low = _tri(C)
        gc_cols = _tri_dot(low, gp_ref[...], "nn", 0)
        gc_rows = _tri_dot(gr_ref[...], low, "nt", 1)
        bcols = bp_ref[...]
        lane = lax.broadcasted_iota(jnp.int32, (1, LANES), 1)
        row8 = lax.broadcasted_iota(jnp.int32, (8, 1), 0)
        dgc_cols = jnp.zeros((C, LANES), F32)
        dgc_rows = jnp.zeros((8, C), F32)
        dbcols = jnp.zeros((C, LANES), F32)
        hs = range(H)
        t_saved = tuple(tall_ref[h].astype(F32) for h in hs)
        _, vjp = jax.vjp(
            lambda *args: _chunk_heads(*args, t_saved=t_saved),
            tuple(x_ref[:, h * HEAD_DIM : (h + 1) * HEAD_DIM] for h in hs),
            tuple(x_ref[:, D + h * HEAD_DIM : D + (h + 1) * HEAD_DIM] for h in hs),
            tuple(x_ref[:, 2 * D + h * HEAD_DIM : 2 * D + (h + 1) * HEAD_DIM] for h in hs),
            tuple(_pick_lane(gc_cols, h) for h in hs), tuple(_pick_row(gc_rows, h) for h in hs),
            tuple(_pick_lane(bcols, h) for h in hs), tuple(sall_ref[h] for h in hs),
        )
        dq, dk, dv, dgc, dgr, dbc, ds0 = vjp((tuple(do_ref[:, h * HEAD_DIM : (h + 1) * HEAD_DIM] for h in hs), tuple(dst[h] for h in hs)))
        for h in hs:
            dst[h] = ds0[h]
            dx_ref[:, h * HEAD_DIM : (h + 1) * HEAD_DIM] = dq[h]
            dx_ref[:, D + h * HEAD_DIM : D + (h + 1) * HEAD_DIM] = dk[h]
            dx_ref[:, 2 * D + h * HEAD_DIM : 2 * D + (h + 1) * HEAD_DIM] = dv[h]
            dgc_cols = dgc_cols + dgc[h] * (lane == h).astype(F32)
            dgc_rows = dgc_rows + dgr[h] * (row8 == h).astype(F32)
            dbcols = dbcols + dbc[h] * (lane == h).astype(F32)
        dgp_ref[...] = _tri_dot(low, dgc_cols, "tn", 0)
        dgr_ref[...] = _tri_dot(dgc_rows, low, "nn", 1)
        dbp_ref[...] = dbcols

    rev = lambda n: N - 1 - n
    return pl.pallas_call(
        body,
        name="gdn_delta_bwd",
        grid=(N,),
        in_specs=[
            pl.BlockSpec((C, W), lambda n: (rev(n), 0)),
            pl.BlockSpec((C, LANES), lambda n: (rev(n), 0)),
            pl.BlockSpec((None, 8, C), lambda n: (rev(n), 0, 0)),
            pl.BlockSpec((C, LANES), lambda n: (rev(n), 0)),
            pl.BlockSpec((None, H, HEAD_DIM, HEAD_DIM), lambda n: (rev(n), 0, 0, 0)),
            pl.BlockSpec((None, H, C, C), lambda n: (rev(n), 0, 0, 0)),
            pl.BlockSpec((C, D), lambda n: (rev(n), 0)),
        ],
        out_specs=[
            pl.BlockSpec((C, W), lambda n: (rev(n), 0)),
            pl.BlockSpec((C, LANES), lambda n: (rev(n), 0)),
            pl.BlockSpec((None, 8, C), lambda n: (rev(n), 0, 0)),
            pl.BlockSpec((C, LANES), lambda n: (rev(n), 0)),
        ],
        out_shape=[
            jax.ShapeDtypeStruct((S, W), F32),
            jax.ShapeDtypeStruct((S, LANES), F32),
            jax.ShapeDtypeStruct((N, 8, C), F32),
            jax.ShapeDtypeStruct((S, LANES), F32),
        ],
        scratch_shapes=[pltpu.VMEM((H, HEAD_DIM, HEAD_DIM), F32)],
        compiler_params=_params(1),
    )(qkvn, g_pad, g_rows, beta_pad, s_all, t_all, do)


def _gate_norm_head(o, z, nw):
    return o * lax.rsqrt(jnp.mean(o * o, axis=-1, keepdims=True) + RMS_EPS) * nw * _silu(z)


def _gate_norm_fwd(o, z, nw):
    S, D = o.shape
    H = D // HEAD_DIM
    ts = min(ROW_TILE, S)

    def fn(i, n, rv, hv, cv):
        ov, zv = rv
        parts = [_gate_norm_head(ov[:, h * HEAD_DIM : (h + 1) * HEAD_DIM], zv[:, h * HEAD_DIM : (h + 1) * HEAD_DIM], cv[0]) for h in range(H)]
        return (jnp.concatenate(parts, axis=1),), ()

    return _rowwise("gdn_gate_norm_fwd", fn, S, ts, rows=[o, z], consts=[nw], outs=[(D, BF16)])[0]


def _gate_norm_bwd(dog, o, z, nw):
    S, D = o.shape
    H = D // HEAD_DIM
    ts = min(ROW_TILE, S)

    def fn(i, n, rv, hv, cv):
        dv, ov, zv = rv
        dos, dzs, dnw = [], [], None
        for h in range(H):
            sl = slice(h * HEAD_DIM, (h + 1) * HEAD_DIM)
            _, vjp = jax.vjp(_gate_norm_head, ov[:, sl], zv[:, sl], cv[0])
            a, b_, c_ = vjp(dv[:, sl])
            dos.append(a)
            dzs.append(b_)
            dnw = c_ if dnw is None else dnw + c_
        return (jnp.concatenate(dos, axis=1), jnp.concatenate(dzs, axis=1)), (dnw,)

    return _rowwise("gdn_gate_norm_bwd", fn, S, ts, rows=[dog, o, z], consts=[nw], outs=[(D, F32), (D, BF16)], accs=[(1, HEAD_DIM)])


def _square_bf16(r):
    rf = r.astype(F32)
    return rf * rf


def _mlp_ple_dw(li, dhb, dgate, dpp, xa, p, r, w2):
    dpre = _mm(f"l{li}_mlp_down_bwd", dhb, w2, "nt", [BF16], epi=lambda acc, rr: (acc * (2.0 * rr.astype(F32)),), extras=[(r, "tile")], tm=1024, tn=1024, b_outer=True)
    dw2 = _mm(f"l{li}_mlp_dw2", r, dhb, "tn", [BF16], a_fn=_square_bf16, **DW_TILES)
    dw1 = _mm(f"l{li}_mlp_dw1", xa, dpre, "tn", [BF16], out_blocks=N_DEV, **DW_TILES)
    dwg = _mm(f"l{li}_ple_dwg", xa, dgate, "tn", [BF16], **DW_TILES)
    dwp = _mm(f"l{li}_ple_dwp", p, dpp, "tn", [BF16], out_blocks=N_DEV, **DW_TILES)
    rows = lambda a: a.reshape((N_DEV, a.shape[0] // N_DEV) + a.shape[1:])
    return dpre, dw1, rows(dw2), rows(dwg), dwp


def _mlp_ple_dx(li, dh, dpre, dgate, w1, wg, after, xhat, rstd, g):
    t = _mm(f"l{li}_ple_gate_bwd", dgate, wg, "nt", [F32], epi=lambda acc, d: (acc + ALPHA * d,), extras=[(dh, "tile")], tm=1024, after=after)
    return _mm(f"l{li}_mlp_up_bwd", dpre, w1, "nt", [F32, BF16], epi=_ln_bwd_epi, extras=[(t, "tile"), (xhat, "tile"), (rstd, "rows"), (g, "row")],
               tm=256, tk=4096, accs=2)


def _local_step(x, p, tgt, W, fetch, emit):
    S, D = x.shape
    H = D // HEAD_DIM
    C = min(CHUNK, S)
    N = S // C
    lg = lambda i, j: W["ln_gain"][2 * i + j][None, :]
    lb = lambda i, j: W["ln_bias"][2 * i + j][None, :]
    G = {}

    pooled, xh0a, rs0a, x0a = _pool_fwd(x, W["pool_w"], W["pool_b"], W["pool_scale"], lg(0, 0), lb(0, 0))
    w0a = fetch("l0a", x0a)
    r0 = _mm("l0_mlp_up", x0a, w0a["mlp_w1"], "nn", [BF16], epi=lambda acc: (jnp.maximum(acc, 0.0),), tm=1024, tn=1024, b_outer=True, after=w0a.get("_after", ()))
    w0b = fetch("l0b", r0)
    gate0 = _mm("l0_ple_gate", x0a, w0b["ple_gate_w"], "nn", [F32], epi=lambda acc, bias: (acc + bias,), extras=[(W["ple_gate_b"][0:1], "row")], tm=1024)
    pp0 = _mm("l0_ple_proj", p[0], w0b["ple_proj"], "nn", [F32])
    w0c = fetch("l0c", pp0)
    ln_rows = lambda i, j, i2, j2: [(lg(i, j), "row"), (lb(i, j), "row"), (lg(i2, j2), "row"), (lb(i2, j2), "row")]
    xh0b, rs0b, x0b = _mm("l0_mlp_down", r0, w0c["mlp_w2"], "nn", [F32, (F32, LANES), BF16], a_fn=_square_bf16, tm=256, tn=D, tk=4096, epi=_res_ln_ffpe_epi,
                          extras=[(xh0a, "tile"), (gate0, "tile"), (pp0, "tile")] + ln_rows(0, 0, 0, 1), after=w0c.get("_after", ()))

    wg_ = fetch("gdn", x0b)
    qkv_pre = _mm("gdn_in_qkv", x0b, wg_["gdn_wqkv"], "nn", [F32], tm=1024, tn=1024, b_outer=True, after=wg_.get("_after", ()))
    z = _mm("gdn_in_z", x0b, wg_["gdn_wz"], "nn", [F32], tm=1024)
    ba = _mm("gdn_in_ba", x0b, wg_["gdn_wba"], "nn", [F32])
    qkvn = _conv_fwd(qkv_pre, W["gdn_conv"])
    beta_pad, g_pad = _gates_fwd(ba, W["gdn_a_log"], W["gdn_dt_bias"])
    g_rows = g_pad[:, :8].reshape(N, C, 8).transpose(0, 2, 1)
    o, s_all, t_all = _delta_fwd(qkvn, g_pad, g_rows, beta_pad)
    og = _gate_norm_fwd(o, z, W["gdn_norm_w"])
    xh1a, rs1a, x1a = _mm("gdn_out", og, wg_["gdn_w_out"], "nn", [F32, (F32, LANES), BF16], tm=512, tn=D, epi=_res_ln_epi,
                          extras=[(xh0b, "tile")] + ln_rows(0, 1, 1, 0))
    w1_ = fetch("l1", x1a)
    r1 = _mm("l1_mlp_up", x1a, w1_["mlp_w1"], "nn", [BF16], epi=lambda acc: (jnp.maximum(acc, 0.0),), tm=1024, tn=1024, b_outer=True)
    gate1 = _mm("l1_ple_gate", x1a, w1_["ple_gate_w"], "nn", [F32], epi=lambda acc, bias: (acc + bias,), extras=[(W["ple_gate_b"][1:2], "row")], tm=1024)
    pp1 = _mm("l1_ple_proj", p[1], w1_["ple_proj"], "nn", [F32])
    dh1b, dh1b_b, dgate1, dpp1, loss_cols, dg11, db11, dbg_1 = _mm(
        "l1_mlp_down", r1, w1_["mlp_w2"], "nn", [F32, BF16, BF16, BF16], a_fn=_square_bf16, tm=256, tn=D, tk=4096, epi=_final_ln_loss_epi,
        extras=[(xh1a, "tile"), (gate1, "tile"), (pp1, "tile"), (tgt, "tile")] + ln_rows(1, 0, 1, 1), accs=4)

    dpre1, dw1_1, dw2_1, dwg_1, dwp_1 = _mlp_ple_dw(1, dh1b_b, dgate1, dpp1, x1a, p[1], r1, w1_["mlp_w2"])
    tok = emit("l1", {"mlp_w1": dw1_1, "mlp_w2": dw2_1, "ple_gate_w": dwg_1, "ple_proj": dwp_1})
    dh1a, dh1a_b, dg10, db10 = _mlp_ple_dx(1, dh1b, dpre1, dgate1, w1_["mlp_w1"], w1_["ple_gate_w"], [tok], xh1a, rs1a, lg(1, 0))
    dog = _mm("gdn_out_bwd", dh1a_b, wg_["gdn_w_out"], "nt", [F32], tm=1024)
    dw_out = _mm("gdn_dw_out", og, dh1a_b, "tn", [BF16], **DW_TILES)
    dw_out = dw_out.reshape((N_DEV, dw_out.shape[0] // N_DEV) + dw_out.shape[1:])
    do, dz, dnw = _gate_norm_bwd(dog, o, z, W["gdn_norm_w"])
    dqkvn, dg_col, dg_row, dbeta = _delta_bwd(qkvn, g_pad, g_rows, beta_pad, s_all, t_all, do)
    dg_all = dg_col + jnp.pad(dg_row.transpose(0, 2, 1).reshape(S, 8), ((0, 0), (0, LANES - 8)))
    dba, dalog, ddt = _gates_bwd(ba, W["gdn_a_log"], W["gdn_dt_bias"], dbeta, dg_all, H)
    dqkv, dconv = _conv_bwd(qkv_pre, W["gdn_conv"], dqkvn)
    dwqkv = _mm("gdn_dwqkv", x0b, dqkv, "tn", [F32], **DW_TILES)
    dwz = _mm("gdn_dwz", x0b, dz, "tn", [F32], **DW_TILES)
    dwba = _mm("gdn_dwba", x0b, dba, "tn", [F32], **DW_TILES)
    dw_in = jnp.concatenate([dwqkv, dwz, dwba[:, :H], dwba[:, LANES : LANES + H]], axis=1)
    tok = emit("gdn", {"gdn_w_in": _split_blocks("gdn_w_in", dw_in).astype(BF16), "gdn_w_out": dw_out})
    t = _mm("gdn_in_ba_bwd", dba, wg_["gdn_wba"], "nt", [F32], epi=lambda acc, d: (acc + ALPHA * d,), extras=[(dh1a, "tile")], after=[tok])
    t = _mm("gdn_in_z_bwd", dz, wg_["gdn_wz"], "nt", [F32], epi=lambda acc, d: (acc + d,), extras=[(t, "tile")], tm=1024)
    dh0b, dh0b_b, dgate0, dpp0, dg01, db01, dbg_0 = _mm(
        "gdn_in_qkv_bwd", dqkv, wg_["gdn_wqkv"], "nt", [F32, BF16, BF16, BF16], epi=_ln_bwd_ple_epi,
        extras=[(t, "tile"), (xh0b, "tile"), (rs0b, "rows"), (gate0, "tile"), (pp0, "tile"), (lg(0, 1), "row")], tm=256, tk=3072, accs=3)

    dpre0, dw1_0, dw2_0, dwg_0, dwp_0 = _mlp_ple_dw(0, dh0b_b, dgate0, dpp0, x0a, p[0], r0, w0c["mlp_w2"])
    tok = emit("l0", {"mlp_w1": dw1_0, "mlp_w2": dw2_0, "ple_gate_w": dwg_0, "ple_proj": dwp_0})
    dh0a, _, dg00, db00 = _mlp_ple_dx(0, dh0b, dpre0, dgate0, w0a["mlp_w1"], w0b["ple_gate_w"], [tok], xh0a, rs0a, lg(0, 0))
    grad_x, dyp, dscale, dpb = _pool_bwd(dh0a, pooled, W["pool_w"], W["pool_b"], W["pool_scale"])
    G["pool_w"] = _pool_dw(pooled, dyp)

    G["ln_gain"] = jnp.concatenate([dg00, dg01, dg10, dg11], axis=0)
    G["ln_bias"] = jnp.concatenate([db00, db01, db10, db11], axis=0)
    G["pool_b"] = dpb
    G["pool_scale"] = dscale
    G["gdn_conv"] = dconv
    G["gdn_a_log"] = dalog[:, :H]
    G["gdn_dt_bias"] = ddt[:, :H]
    G["gdn_norm_w"] = dnw
    G["ple_gate_b"] = jnp.concatenate([dbg_0, dbg_1], axis=0)
    return loss_cols, grad_x, G


_HBM = pl.BlockSpec(memory_space=pltpu.HBM)


def _all_gather(name, shards):
    T = len(shards)

    def body(*refs):
        ins, outs = refs[:T], refs[T : 2 * T]
        send_sems, recv_sems, local_sems = refs[2 * T :]
        x, y, c = lax.axis_index("x"), lax.axis_index("y"), lax.axis_index("c")
        me, sibling = (x, y, c), (x, y, 1 - c)
        chips = [(1 - x, y), (x, 1 - y), (1 - x, 1 - y)]

        def blk(t, px, py, pc):
            return outs[t].at[4 * px + 2 * py + pc]

        def copy(t, k, block, to, src=None):
            return pltpu.make_async_remote_copy(
                src_ref=blk(t, *block) if src is None else src, dst_ref=blk(t, *block),
                send_sem=send_sems.at[t, k], recv_sem=recv_sems.at[t, k], device_id=to, device_id_type=MESH,
            )

        mine = [pltpu.make_async_copy(ins[t], blk(t, *me), local_sems.at[t]) for t in range(T)]
        for cp in mine:
            cp.start()
        first = []
        for t in range(T):
            first.append(copy(t, 0, me, sibling, src=ins[t]))
            first += [copy(t, 1 + j, me, (*chip, c), src=ins[t]) for j, chip in enumerate(chips)]
        for cp in first:
            cp.start()
        passed = []
        for j, chip in enumerate(chips):
            for t in range(T):
                copy(t, 1 + j, (*chip, c), me).wait_recv()
                fw = copy(t, 4 + j, (*chip, c), sibling)
                fw.start()
                passed.append(fw)
        for t in range(T):
            copy(t, 0, sibling, me).wait_recv()
            for j, chip in enumerate(chips):
                copy(t, 4 + j, (*chip, 1 - c), me).wait_recv()
        for cp in first + passed:
            cp.wait_send()
        for cp in mine:
            cp.wait()

    return pl.pallas_call(
        body,
        name=name,
        in_specs=[_HBM] * T,
        out_specs=[_HBM] * T,
        out_shape=[jax.ShapeDtypeStruct((N_DEV,) + s.shape, s.dtype) for s in shards],
        scratch_shapes=[pltpu.SemaphoreType.DMA((T, 7)), pltpu.SemaphoreType.DMA((T, 7)), pltpu.SemaphoreType.DMA((T,))],
    )(*shards)


def _exchange(name, blocks):
    def body(g_ref, o_ref, send_sems, recv_sems, local_sem):
        x, y, c = lax.axis_index("x"), lax.axis_index("y"), lax.axis_index("c")
        own = pltpu.make_async_copy(g_ref.at[4 * x + 2 * y + c], o_ref.at[N_DEV - 1], local_sem)
        own.start()
        copies = []
        for rel in range(1, N_DEV):
            px = 1 - x if rel & 4 else x
            py = 1 - y if rel & 2 else y
            pc = 1 - c if rel & 1 else c
            copies.append(
                pltpu.make_async_remote_copy(
                    src_ref=g_ref.at[4 * px + 2 * py + pc], dst_ref=o_ref.at[rel - 1],
                    send_sem=send_sems.at[rel - 1], recv_sem=recv_sems.at[rel - 1], device_id=(px, py, pc), device_id_type=MESH,
                )
            )
        for cp in copies:
            cp.start()
        for cp in copies:
            cp.wait_recv()
        for cp in copies:
            cp.wait_send()
        own.wait()

    return pl.pallas_call(
        body,
        name=name,
        in_specs=[_HBM],
        out_specs=_HBM,
        out_shape=jax.ShapeDtypeStruct(blocks.shape, blocks.dtype),
        scratch_shapes=[pltpu.SemaphoreType.DMA((N_DEV - 1,)), pltpu.SemaphoreType.DMA((N_DEV - 1,)), pltpu.SemaphoreType.DMA],
    )(blocks)


_SEM = pl.BlockSpec(memory_space=pltpu.SEMAPHORE)
_ANY = pl.BlockSpec(memory_space=pl.ANY)
_DATAFLOW = pltpu.SideEffectType.DATAFLOW_SIDE_EFFECTING
N_PEERS = N_DEV - 1


def _peer(rel, x, y, c):
    return (1 - x if rel & 4 else x, 1 - y if rel & 2 else y, 1 - c if rel & 1 else c)


def _send_start(name, srcs, lands, gather, after):
    T = len(srcs)

    def body(*refs):
        src_refs, land_refs = refs[:T], refs[T : 2 * T]
        send_sems, recv_sems = refs[2 * T + 1], refs[2 * T + 2]
        token = refs[-1]
        x, y, c = lax.axis_index("x"), lax.axis_index("y"), lax.axis_index("c")
        for t in range(T):
            for rel in range(1, N_DEV):
                px, py, pc = _peer(rel, x, y, c)
                pltpu.make_async_remote_copy(
                    src_ref=src_refs[t] if gather else src_refs[t].at[4 * px + 2 * py + pc],
                    dst_ref=land_refs[t].at[4 * x + 2 * y + c] if gather else land_refs[t].at[rel - 1],
                    send_sem=send_sems.at[t * N_PEERS + rel - 1], recv_sem=recv_sems.at[t * N_PEERS + rel - 1], device_id=(px, py, pc), device_id_type=MESH,
                ).start()
        token[...] = jnp.zeros_like(token)

    hbm = lambda a: pltpu.HBM(a.shape, a.dtype)
    return pl.pallas_call(
        body,
        name=name,
        out_shape=(pltpu.SemaphoreType.DMA((T * N_PEERS,)), pltpu.SemaphoreType.DMA((T * N_PEERS,)), *[hbm(a) for a in srcs],
                   *[hbm(a) for a in lands], jax.ShapeDtypeStruct((8, LANES), F32)),
        in_specs=(_HBM,) * (2 * T) + (_ANY,),
        out_specs=(_SEM, _SEM) + (_HBM,) * (2 * T) + (pl.BlockSpec(memory_space=pltpu.VMEM),),
        input_output_aliases={t: 2 + t for t in range(2 * T)},
        compiler_params=pltpu.CompilerParams(has_side_effects=_DATAFLOW),
    )(*[pltpu.with_memory_space_constraint(a, pltpu.HBM) for a in list(srcs) + list(lands)], after)


def _send_wait(name, started, after, gather):
    T = (len(started) - 3) // 2
    send_sems, recv_sems, token = started[0], started[1], started[-1]
    thru = started[2:-1]

    def body(*refs):
        src_refs, land_refs = refs[:T], refs[T : 2 * T]
        send_sems, recv_sems = refs[2 * T], refs[2 * T + 1]
        x, y, c = lax.axis_index("x"), lax.axis_index("y"), lax.axis_index("c")
        for t in range(T):
            for rel in range(1, N_DEV):
                cp = pltpu.make_async_remote_copy(
                    src_ref=src_refs[t] if gather else src_refs[t].at[0], dst_ref=land_refs[t].at[0],
                    send_sem=send_sems.at[t * N_PEERS + rel - 1], recv_sem=recv_sems.at[t * N_PEERS + rel - 1], device_id=_peer(rel, x, y, c), device_id_type=MESH,
                )
                cp.wait_send()
                cp.wait_recv()

    outs = pl.pallas_call(
        body,
        name=name,
        out_shape=tuple(pltpu.HBM(a.shape, a.dtype) for a in thru),
        in_specs=(_HBM,) * (2 * T) + (_SEM, _SEM, _ANY),
        out_specs=(_HBM,) * (2 * T),
        input_output_aliases={t: t for t in range(2 * T)},
        compiler_params=pltpu.CompilerParams(has_side_effects=_DATAFLOW),
    )(*thru, send_sems, recv_sems, after)
    return list(outs[:T]), list(outs[T:])


def _sum_blocks(name, parts, tr):
    _, R, Cw = parts[0].shape
    tr = tr if R % tr == 0 else R

    def body(*refs):
        acc = None
        for p_ref in refs[:-1]:
            for d in range(p_ref.shape[0]):
                v = p_ref[d].astype(F32)
                acc = v if acc is None else acc + v
        refs[-1][...] = acc

    return pl.pallas_call(
        body,
        name=name,
        grid=(R // tr,),
        in_specs=[pl.BlockSpec((a.shape[0], tr, Cw), lambda i: (0, i, 0)) for a in parts],
        out_specs=pl.BlockSpec((tr, Cw), lambda i: (i, 0)),
        out_shape=jax.ShapeDtypeStruct((R, Cw), F32),
        compiler_params=_params(1),
    )(*parts)


def _adamw(name, w, g, m, v):
    shape = w.shape
    cols = shape[-1]
    rows = w.size // cols
    tr = rows if rows <= 512 else 512
    assert rows % tr == 0
    w2, g2, m2, v2 = (a.reshape(rows, cols) for a in (w, g, m, v))

    def body(w_ref, g_ref, m_ref, v_ref, d_ref, mo_ref, vo_ref):
        gv = g_ref[...]
        mn = ADAM_B1 * m_ref[...] + (1.0 - ADAM_B1) * gv
        vn = ADAM_B2 * v_ref[...] + (1.0 - ADAM_B2) * jnp.square(gv)
        m_hat = mn / (1.0 - ADAM_B1**ADAM_STEP)
        v_hat = vn / (1.0 - ADAM_B2**ADAM_STEP)
        d_ref[...] = -ADAM_LR * (m_hat / (jnp.sqrt(v_hat) + ADAM_EPS) + ADAM_WD * w_ref[...])
        mo_ref[...] = mn
        vo_ref[...] = vn

    spec = pl.BlockSpec((tr, cols), lambda i: (i, 0))
    d, mn, vn = pl.pallas_call(
        body,
        name=name,
        grid=(rows // tr,),
        in_specs=[spec] * 4,
        out_specs=[spec] * 3,
        out_shape=[jax.ShapeDtypeStruct((rows, cols), F32)] * 3,
        compiler_params=_params(1),
    )(w2, g2, m2, v2)
    return d.reshape(shape), mn.reshape(shape), vn.reshape(shape)


SMALL_SHARDED = ("ln_gain", "ln_bias", "pool_b", "gdn_conv")
SMALL_REPLICATED = ("pool_scale", "gdn_a_log", "gdn_dt_bias", "gdn_norm_w", "ple_gate_b")
WEIGHTS = ("ln_gain", "ln_bias", "pool_w", "pool_b", "pool_scale", "gdn_w_in", "gdn_conv", "gdn_a_log", "gdn_dt_bias",
           "gdn_norm_w", "gdn_w_out", "mlp_w1", "mlp_w2", "ple_gate_w", "ple_gate_b", "ple_proj")
BIG_AXIS = {"gdn_w_in": 1, "gdn_w_out": 0, "mlp_w1": 1, "mlp_w2": 0, "ple_gate_w": 0, "ple_proj": 1, "pool_w": 1}
GATHER_GROUPS = {
    "l0a": (("mlp_w1", 0),),
    "l0b": (("ple_gate_w", 0), ("ple_proj", 0)),
    "l0c": (("mlp_w2", 0),),
    "gdn": (("gdn_w_in", 0), ("gdn_w_out", 0)),
    "l1": (("mlp_w1", 1), ("mlp_w2", 1), ("ple_gate_w", 1), ("ple_proj", 1)),
}
GATHER_AFTER = {"l0b": "l0a", "l0c": "l0a", "gdn": "l0c", "l1": "gdn"}
GRAD_GROUPS = {
    "l1": (("mlp_w1", 1), ("mlp_w2", 1), ("ple_gate_w", 1), ("ple_proj", 1)),
    "gdn": (("gdn_w_in", 0), ("gdn_w_out", 0)),
    "l0": (("mlp_w1", 0), ("mlp_w2", 0), ("ple_gate_w", 0), ("ple_proj", 0)),
}
PACK_PART_ALIGN = 16
SUM_TILE = 128


def _part_rows(a, width):
    rows = a.size // width
    return rows + (-rows) % PACK_PART_ALIGN


def _pack_rows(parts, width, dtype, align):
    padded = []
    for a in parts:
        a2 = a.reshape(-1, width).astype(dtype)
        padded.append(jnp.pad(a2, ((0, _part_rows(a, width) - a2.shape[0]), (0, 0))))
    flat = jnp.concatenate(padded, axis=0)
    return jnp.pad(flat, ((0, (-flat.shape[0]) % align), (0, 0)))


def _pack_blocks(parts, width, dtype, align):
    padded = []
    for a in parts:
        a2 = a.reshape(a.shape[0], -1, width).astype(dtype)
        padded.append(jnp.pad(a2, ((0, 0), (0, _part_rows(a[0], width) - a2.shape[1]), (0, 0))))
    flat = jnp.concatenate(padded, axis=1)
    return jnp.pad(flat, ((0, 0), (0, (-flat.shape[1]) % align), (0, 0)))


def _unpack_rows(packed, shapes, width):
    out, off = [], 0
    for shp in shapes:
        size = 1
        for d in shp:
            size *= d
        out.append(packed[..., off : off + size // width, :].reshape(packed.shape[:-2] + tuple(shp)))
        off += size // width + (-(size // width)) % PACK_PART_ALIGN
    return out


def _split_blocks(name, full):
    ax = BIG_AXIS[name]
    shp = full.shape
    a = full.reshape(shp[:ax] + (N_DEV, shp[ax] // N_DEV) + shp[ax + 1 :])
    return jnp.moveaxis(a, ax, 0)


def _join_blocks(name, blocks):
    ax = BIG_AXIS[name]
    a = jnp.moveaxis(blocks, 0, ax)
    shp = a.shape
    return a.reshape(shp[:ax] + (shp[ax] * shp[ax + 1],) + shp[ax + 2 :])


def _pack_small(parts):
    flat = jnp.concatenate([jnp.pad(a.reshape(-1), (0, (-a.size) % LANES)) for a in parts])
    rows = flat.size // LANES
    return jnp.pad(flat.reshape(rows, LANES), ((0, (-rows) % 8), (0, 0)))


def _unpack_small(packed, shapes):
    flat = packed.reshape(packed.shape[:-2] + (-1,))
    out, off = [], 0
    for shp in shapes:
        size = 1
        for s in shp:
            size *= s
        out.append(flat[..., off : off + size].reshape(flat.shape[:-1] + tuple(shp)))
        off += size + (-size) % LANES
    return out


def _split_w_in(w_in, D, H):
    pad = lambda a: jnp.pad(a, ((0, 0), (0, LANES - H)))
    return w_in[:, : 3 * D], w_in[:, 3 * D : 4 * D], jnp.concatenate([pad(w_in[:, 4 * D : 4 * D + H]), pad(w_in[:, 4 * D + H :])], axis=1)


def kernel(x, p, ln_gain, ln_bias, pool_w, pool_b, pool_scale, gdn_w_in, gdn_conv, gdn_a_log, gdn_dt_bias, gdn_norm_w, gdn_w_out, mlp_w1, mlp_w2, ple_gate_w, ple_gate_b, ple_proj, loss_target, m_ln_gain, m_ln_bias, m_pool_w, m_pool_b, m_pool_scale, m_gdn_w_in, m_gdn_conv, m_gdn_a_log, m_gdn_dt_bias, m_gdn_norm_w, m_gdn_w_out, m_mlp_w1, m_mlp_w2, m_ple_gate_w, m_ple_gate_b, m_ple_proj, v_ln_gain, v_ln_bias, v_pool_w, v_pool_b, v_pool_scale, v_gdn_w_in, v_gdn_conv, v_gdn_a_log, v_gdn_dt_bias, v_gdn_norm_w, v_gdn_w_out, v_mlp_w1, v_mlp_w2, v_ple_gate_w, v_ple_gate_b, v_ple_proj):
    w_sh = dict(ln_gain=ln_gain, ln_bias=ln_bias, pool_w=pool_w, pool_b=pool_b, pool_scale=pool_scale, gdn_w_in=gdn_w_in,
                gdn_conv=gdn_conv, gdn_a_log=gdn_a_log, gdn_dt_bias=gdn_dt_bias, gdn_norm_w=gdn_norm_w, gdn_w_out=gdn_w_out,
                mlp_w1=mlp_w1, mlp_w2=mlp_w2, ple_gate_w=ple_gate_w, ple_gate_b=ple_gate_b, ple_proj=ple_proj)
    m_sh = dict(ln_gain=m_ln_gain, ln_bias=m_ln_bias, pool_w=m_pool_w, pool_b=m_pool_b, pool_scale=m_pool_scale, gdn_w_in=m_gdn_w_in,
                gdn_conv=m_gdn_conv, gdn_a_log=m_gdn_a_log, gdn_dt_bias=m_gdn_dt_bias, gdn_norm_w=m_gdn_norm_w, gdn_w_out=m_gdn_w_out,
                mlp_w1=m_mlp_w1, mlp_w2=m_mlp_w2, ple_gate_w=m_ple_gate_w, ple_gate_b=m_ple_gate_b, ple_proj=m_ple_proj)
    v_sh = dict(ln_gain=v_ln_gain, ln_bias=v_ln_bias, pool_w=v_pool_w, pool_b=v_pool_b, pool_scale=v_pool_scale, gdn_w_in=v_gdn_w_in,
                gdn_conv=v_gdn_conv, gdn_a_log=v_gdn_a_log, gdn_dt_bias=v_gdn_dt_bias, gdn_norm_w=v_gdn_norm_w, gdn_w_out=v_gdn_w_out,
                mlp_w1=v_mlp_w1, mlp_w2=v_mlp_w2, ple_gate_w=v_ple_gate_w, ple_gate_b=v_ple_gate_b, ple_proj=v_ple_proj)
    xs, tg = x[0], loss_target[0]
    ps = p[:, 0]
    S, D = xs.shape
    H = D // HEAD_DIM
    me = 4 * lax.axis_index("x") + 2 * lax.axis_index("y") + lax.axis_index("c")
    layer = lambda n, l: (w_sh[n][0] if n in ("gdn_w_in", "gdn_w_out") else w_sh[n][l])

    pool_packed = _pack_rows([w_sh["pool_w"][0]], D, BF16, PACK_PART_ALIGN)
    small_packed = _pack_small([w_sh[n] for n in SMALL_SHARDED])
    pool_gathered, small_gathered = _all_gather("gather_first", [pool_packed, small_packed])
    W = {"pool_w": _join_blocks("pool_w", _unpack_rows(pool_gathered, [w_sh["pool_w"][0].shape], D)[0])}

    started = {}

    def start(g, after):
        src = _pack_rows([layer(n, l) for n, l in GATHER_GROUPS[g]], D, BF16, PACK_PART_ALIGN)
        started[g] = tuple(_send_start(f"gather_{g}_start", [src], [lax.empty((N_DEV,) + src.shape, BF16)], True, after))
        return started[g][-1]

    first_token = start("l0a", small_gathered)
    smalls = _unpack_small(small_gathered, [w_sh[n].shape for n in SMALL_SHARDED])
    for n, a in zip(SMALL_SHARDED, smalls):
        W[n] = jnp.moveaxis(a, 0, -2).reshape(a.shape[1:-1] + (N_DEV * a.shape[-1],))
    W["ln_gain"] = W["ln_gain"].reshape(2 * DEPTH, D)
    W["ln_bias"] = W["ln_bias"].reshape(2 * DEPTH, D)
    W["pool_b"] = W["pool_b"].reshape(1, D) + first_token[0:1, 0:1]
    W["gdn_conv"] = W["gdn_conv"][0]
    W["pool_scale"] = pool_scale
    W["ple_gate_b"] = ple_gate_b
    W["gdn_norm_w"] = gdn_norm_w
    W["gdn_a_log"] = jnp.pad(gdn_a_log, ((0, 0), (0, LANES - H)))
    W["gdn_dt_bias"] = jnp.pad(gdn_dt_bias, ((0, 0), (0, LANES - H)))

    def fetch(g, after):
        members = GATHER_GROUPS[g]
        (src,), (land,) = _send_wait(f"gather_{g}_wait", started[g], after, True)
        tokens = [start(nxt, land) for nxt, prev in GATHER_AFTER.items() if prev == g]
        land = lax.dynamic_update_index_in_dim(land, src, me, 0)
        parts = _unpack_rows(land, [layer(n, l).shape for n, l in members], D)
        out = {n: _join_blocks(n, a) for (n, _), a in zip(members, parts)}
        if "gdn_w_in" in out:
            out["gdn_wqkv"], out["gdn_wz"], out["gdn_wba"] = _split_w_in(out.pop("gdn_w_in"), D, H)
        out["_after"] = tokens
        return out

    sent = {}

    def emit(g, grads):
        srcs = [grads[n] for n, _ in GRAD_GROUPS[g]]
        lands = [lax.empty((N_PEERS,) + a.shape[1:], BF16) for a in srcs]
        sent[g] = tuple(_send_start(f"grads_{g}_start", srcs, lands, False, srcs[0]))
        return sent[g][-1]

    loss_cols, grad_x, G = _local_step(xs, ps, tg, W, fetch, emit)
    loss = lax.psum(0.5 * jnp.sum(loss_cols) / D, MESH_AXES)

    pool_src = _pack_blocks([_split_blocks("pool_w", G["pool_w"])], D, BF16, PACK_PART_ALIGN)
    pool_sum = _sum_blocks("sum_pool_grads", [_exchange("exchange_pool_grads", pool_src)], SUM_TILE)
    grads = {"pool_w": _unpack_rows(pool_sum, [w_sh["pool_w"][0].shape], D)[0].reshape(w_sh["pool_w"].shape)}
    small_names = SMALL_SHARDED + SMALL_REPLICATED
    gs_packed = _pack_small([G[n] for n in small_names])
    (gs_all,) = _all_gather("gather_small_grads", [gs_packed])
    gs_sum = _sum_blocks("sum_small_grads", [gs_all], SUM_TILE)
    for n, a in zip(small_names, _unpack_small(gs_sum, [G[n].shape for n in small_names])):
        if n in SMALL_SHARDED:
            width = w_sh[n].shape[-1]
            a = a.reshape(w_sh[n].shape[:-1] + (N_DEV * width,))
            a = lax.dynamic_slice_in_dim(a, me * width, width, axis=a.ndim - 1)
        grads[n] = a.reshape(w_sh[n].shape)

    per_layer = {}
    for g, members in GRAD_GROUPS.items():
        srcs, lands = _send_wait(f"grads_{g}_wait", sent[g], grad_x, False)
        for (n, l), src, land in zip(members, srcs, lands):
            own = lax.dynamic_index_in_dim(src, me, 0, keepdims=True)
            as3d = lambda a: a.reshape(a.shape[0], -1, a.shape[-1])
            per_layer[(n, l)] = _sum_blocks(f"sum_grads_{n}_{l}", [as3d(land), as3d(own)], SUM_TILE).reshape(layer(n, l).shape)
    for n in ("gdn_w_in", "gdn_w_out"):
        grads[n] = per_layer[(n, 0)][None]
    for n in ("mlp_w1", "mlp_w2", "ple_gate_w", "ple_proj"):
        grads[n] = jnp.stack([per_layer[(n, 0)], per_layer[(n, 1)]])

    deltas, new_m, new_v = {}, {}, {}
    for n in WEIGHTS:
        deltas[n], new_m[n], new_v[n] = _adamw(f"adamw_{n}", w_sh[n], grads[n], m_sh[n], v_sh[n])
    return (loss, grad_x[None], *[grads[n] for n in WEIGHTS], *[deltas[n] for n in WEIGHTS],
            *[new_m[n] for n in WEIGHTS], *[new_v[n] for n in WEIGHTS])
```

```python
import functools

import jax
import jax.numpy as jnp
from jax import lax
from jax.experimental import pallas as pl
from jax.experimental.pallas import tpu as pltpu

F32 = jnp.float32
BF16 = jnp.bfloat16
MESH_AXES = ("x", "y", "c")
N_DEV = 8
MESH = pl.DeviceIdType.MESH

DEPTH = 2
ALPHA = (2.0 * DEPTH) ** 0.25
LN_EPS = 1e-5
RMS_EPS = 1e-6
L2_EPS = 1e-6
HEAD_DIM = 128
CONV_WIDTH = 4
POOL_WINDOWS = (2, 4, 8, 16)
POOL_HALO = 16
CONV_HALO = 8
LANES = 128
ADAM_LR = 0.001
ADAM_B1 = 0.9
ADAM_B2 = 0.999
ADAM_EPS = 1e-08
ADAM_WD = 0.01
ADAM_STEP = 10

VMEM_LIMIT = 56 * 1024 * 1024
ROW_TILE = 512
CONV_TILE = 256
CHUNK = 128
MM_TM, MM_TN, MM_TK = 512, 1024, 1024
DW_TILES = dict(tm=512, tn=512, tk=8192, b_outer=True)

_DIMS = {
    "nn": (((1,), (0,)), ((), ())),
    "nt": (((1,), (1,)), ((), ())),
    "tn": (((0,), (0,)), ((), ())),
}


def _params(n_axes):
    return pltpu.CompilerParams(dimension_semantics=("arbitrary",) * n_axes, vmem_limit_bytes=VMEM_LIMIT)


def _fit(tile, n):
    tile = min(tile, n)
    while n % tile:
        tile //= 2
    return tile


def _mm(name, a, b, mode, out_dtypes, epi=None, extras=(), a_fn=None, tm=None, tn=None, tk=None, b_outer=False, after=(), out_blocks=1, accs=0):
    if mode == "tn":
        K, M = a.shape
    else:
        M, K = a.shape
    N = b.shape[0] if mode == "nt" else b.shape[1]
    tm, tn, tk = _fit(tm or MM_TM, M), (N // out_blocks if out_blocks > 1 else N if accs else _fit(tn or MM_TN, N)), _fit(tk or MM_TK, K)
    nk = K // tk

    def at(f):
        return (lambda j, i, k: f(i, j, k)) if b_outer else f

    a_spec = pl.BlockSpec((tk, tm), at(lambda i, j, k: (k, i))) if mode == "tn" else pl.BlockSpec((tm, tk), at(lambda i, j, k: (i, k)))
    b_spec = pl.BlockSpec((tn, tk), at(lambda i, j, k: (j, k))) if mode == "nt" else pl.BlockSpec((tk, tn), at(lambda i, j, k: (k, j)))
    ex_spec = {"tile": pl.BlockSpec((tm, tn), at(lambda i, j, k: (i, j))), "row": pl.BlockSpec((1, tn), at(lambda i, j, k: (0, j))),
               "rows": pl.BlockSpec((tm, LANES), at(lambda i, j, k: (i, 0)))}
    ex_specs = [pl.BlockSpec(e.shape, lambda i, j, k: (0, 0)) if kind == "whole" else ex_spec[kind] for e, kind in extras]
    assert accs == 0 or (tn == N and nk == 1 and not b_outer), name
    n_ex, n_out, n_after = len(extras), len(out_dtypes), len(after)

    def body(*refs):
        a_ref, b_ref = refs[0], refs[1]
        ex_refs = refs[2 : 2 + n_ex]
        out_refs = refs[2 + n_ex + n_after : 2 + n_ex + n_after + n_out]
        av = a_ref[...]
        if a_fn is not None:
            av = a_fn(av)
        part = lax.dot_general(av.astype(BF16), b_ref[...].astype(BF16), _DIMS[mode], preferred_element_type=F32)

        def finish(res):
            vals = epi(res, *[e[...] for e in ex_refs]) if epi is not None else (res,)
            for o_ref, v in zip(out_refs, vals[:n_out]):
                o_ref[...] = v.astype(o_ref.dtype)
            for a_ref, v in zip(refs[2 + n_ex + n_after + n_out :], vals[n_out:]):

                @pl.when(pl.program_id(0) == 0)
                def _(a_ref=a_ref, v=v):
                    a_ref[...] = v

                @pl.when(pl.program_id(0) > 0)
                def _(a_ref=a_ref, v=v):
                    a_ref[...] += v

        if nk == 1:
            finish(part)
        else:
            acc = refs[-1]
            k = pl.program_id(2)

            @pl.when(k == 0)
            def _():
                acc[...] = part

            @pl.when(k > 0)
            def _():
                acc[...] += part

            @pl.when(k == nk - 1)
            def _():
                finish(acc[...])

    outs = pl.pallas_call(
        body,
        name=name,
        grid=(N // tn, M // tm, nk) if b_outer else (M // tm, N // tn, nk),
        in_specs=[a_spec, b_spec] + ex_specs + [pl.BlockSpec(memory_space=pl.ANY)] * n_after,
        out_specs=[pl.BlockSpec((tm, LANES), at(lambda i, j, k: (i, 0))) if isinstance(dt, tuple)
                   else pl.BlockSpec((tm, tn), at(lambda i, j, k: (i, j))) if out_blocks == 1
                   else pl.BlockSpec((None, tm, tn), at(lambda i, j, k: (j, i, 0))) for dt in out_dtypes]
        + [pl.BlockSpec((1, N), lambda i, j, k: (0, 0))] * accs,
        out_shape=[jax.ShapeDtypeStruct((M, LANES), dt[0]) if isinstance(dt, tuple)
                   else jax.ShapeDtypeStruct((M, N) if out_blocks == 1 else (out_blocks, M, tn), dt) for dt in out_dtypes]
        + [jax.ShapeDtypeStruct((1, N), F32)] * accs,
        scratch_shapes=[pltpu.VMEM((tm, tn), F32)] if nk > 1 else [],
        compiler_params=_params(3),
    )(a, b, *[e for e, _ in extras], *after)
    return outs[0] if n_out + accs == 1 else outs


def _rowwise(name, fn, S, ts, rows=(), halos=(), consts=(), outs=(), accs=()):
    ts = min(ts, S)
    assert S % ts == 0
    n = S // ts
    in_specs = [pl.BlockSpec((ts, a.shape[1]), lambda i: (i, 0)) for a in rows]
    for a, kind, hr in halos:
        r, nb = ts // hr, S // hr
        if kind == "prev":
            in_specs.append(pl.BlockSpec((hr, a.shape[1]), lambda i, r=r: (jnp.maximum(i * r - 1, 0), 0)))
        else:
            in_specs.append(pl.BlockSpec((hr, a.shape[1]), lambda i, r=r, nb=nb: (jnp.minimum((i + 1) * r, nb - 1), 0)))
    in_specs += [pl.BlockSpec(a.shape, lambda i, nd=a.ndim: (0,) * nd) for a in consts]
    out_specs = [pl.BlockSpec((ts, w), lambda i: (i, 0)) for w, _ in outs]
    out_specs += [pl.BlockSpec((r, w), lambda i: (0, 0)) for r, w in accs]
    out_shape = [jax.ShapeDtypeStruct((S, w), dt) for w, dt in outs]
    out_shape += [jax.ShapeDtypeStruct((r, w), F32) for r, w in accs]
    nr, nh, nc, no = len(rows), len(halos), len(consts), len(outs)

    def body(*refs):
        i = pl.program_id(0)
        rv = [r[...] for r in refs[:nr]]
        hv = [r[...] for r in refs[nr : nr + nh]]
        cv = [r[...] for r in refs[nr + nh : nr + nh + nc]]
        o_refs = refs[nr + nh + nc : nr + nh + nc + no]
        a_refs = refs[nr + nh + nc + no :]
        ovals, avals = fn(i, n, rv, hv, cv)
        for o_ref, v in zip(o_refs, ovals):
            o_ref[...] = v.astype(o_ref.dtype)
        for a_ref, v in zip(a_refs, avals):

            @pl.when(i == 0)
            def _(a_ref=a_ref, v=v):
                a_ref[...] = v

            @pl.when(i > 0)
            def _(a_ref=a_ref, v=v):
                a_ref[...] += v

    res = pl.pallas_call(
        body,
        name=name,
        grid=(n,),
        in_specs=in_specs,
        out_specs=out_specs,
        out_shape=out_shape,
        compiler_params=_params(1),
    )(*rows, *[h[0] for h in halos], *consts)
    return list(res)


def _ln(h, g, b):
    mu = jnp.mean(h, axis=-1, keepdims=True)
    d = h - mu
    var = jnp.mean(d * d, axis=-1, keepdims=True)
    rstd = lax.rsqrt(var + LN_EPS)
    xhat = d * rstd
    return xhat, rstd, xhat * g + b


def _ln_bwd(dy, xhat, rstd, g):
    dxh = dy * g
    m1 = jnp.mean(dxh, axis=-1, keepdims=True)
    m2 = jnp.mean(dxh * xhat, axis=-1, keepdims=True)
    dh = rstd * (dxh - m1 - xhat * m2)
    return dh, jnp.sum(dy * xhat, axis=0, keepdims=True), jnp.sum(dy, axis=0, keepdims=True)


def _wide(col, ts):
    return jnp.broadcast_to(col, (ts, LANES))


def _pool_fwd(x, wp, pb, ps, g, b):
    S, D = x.shape
    gw = D // len(POOL_WINDOWS)
    ts = min(ROW_TILE, S)

    def fn(i, n, rv, hv, cv):
        (xc,), (xp,) = rv, hv
        wpv, pbv, psv, gv, bv = cv
        xp = jnp.where(i > 0, xp, 0.0)
        xx = jnp.concatenate([xp, xc], axis=0)
        t = i * ts + lax.broadcasted_iota(jnp.int32, (ts, 1), 0)
        pooled, ys = [], []
        for gi, w in enumerate(POOL_WINDOWS):
            s = xx[:, gi * gw : (gi + 1) * gw]
            k = 1
            while k < w:
                s = s + pltpu.roll(s, k, axis=0)
                k *= 2
            cnt = jnp.minimum(t + 1, w).astype(F32)
            pg = (s[POOL_HALO:, :] / cnt - xc[:, gi * gw : (gi + 1) * gw]).astype(BF16)
            pooled.append(pg)
            ys.append(jnp.dot(pg, wpv[gi], preferred_element_type=F32))
        y = jnp.concatenate(ys, axis=1)
        h = ALPHA * xc + (y + pbv) * psv
        xhat, rstd, xa = _ln(h, gv, bv)
        return (jnp.concatenate(pooled, axis=1), xhat, _wide(rstd, ts), xa), ()

    return _rowwise(
        "pool_fwd", fn, S, ts, rows=[x], halos=[(x, "prev", POOL_HALO)], consts=[wp, pb, ps, g, b],
        outs=[(D, BF16), (D, F32), (LANES, F32), (D, BF16)],
    )


def _pool_bwd(dh, pooled, wp, pb, ps):
    S, D = dh.shape
    gw = D // len(POOL_WINDOWS)
    ts = min(ROW_TILE, S)
    te = ts + POOL_HALO

    def fn(i, n, rv, hv, cv):
        (dhc, pc), (dhn,) = rv, hv
        wpv, pbv, psv = cv
        dhn = jnp.where(i < n - 1, dhn, 0.0)
        dy_ext = jnp.concatenate([dhc, dhn], axis=0) * psv
        dyb = dy_ext.astype(BF16)
        t = i * ts + lax.broadcasted_iota(jnp.int32, (te, 1), 0)
        dxs, ys = [], []
        for gi, w in enumerate(POOL_WINDOWS):
            sl = slice(gi * gw, (gi + 1) * gw)
            dp = lax.dot_general(dyb[:, sl], wpv[gi], _DIMS["nt"], preferred_element_type=F32)
            s = dp / jnp.minimum(t + 1, w).astype(F32)
            k = 1
            while k < w:
                s = s + pltpu.roll(s, k, axis=0)
                k *= 2
            s = pltpu.roll(s, POOL_HALO - (w - 1), axis=0)
            dxs.append(s[POOL_HALO:, :] - dp[:ts, :])
            ys.append(jnp.dot(pc[:, sl], wpv[gi], preferred_element_type=F32))
        dx = ALPHA * dhc + jnp.concatenate(dxs, axis=1)
        y = jnp.concatenate(ys, axis=1) + pbv
        dscale = jnp.sum(dhc * y, axis=0, keepdims=True)
        dbias = jnp.sum(dy_ext[:ts, :], axis=0, keepdims=True)
        return (dx, dyb[:ts, :]), (dscale, dbias)

    return _rowwise(
        "pool_bwd", fn, S, ts, rows=[dh, pooled], halos=[(dh, "next", POOL_HALO)], consts=[wp, pb, ps],
        outs=[(D, F32), (D, BF16)], accs=[(1, D), (1, D)],
    )


def _pool_dw(pooled, dy):
    S, D = pooled.shape
    G = len(POOL_WINDOWS)
    gw = D // G
    tk = min(MM_TK, S)
    nk = S // tk

    def body(p_ref, d_ref, o_ref):
        k = pl.program_id(1)
        part = lax.dot_general(p_ref[...], d_ref[...], _DIMS["tn"], preferred_element_type=F32)

        @pl.when(k == 0)
        def _():
            o_ref[...] = part

        @pl.when(k > 0)
        def _():
            o_ref[...] += part

    return pl.pallas_call(
        body,
        name="pool_dw",
        grid=(G, nk),
        in_specs=[pl.BlockSpec((tk, gw), lambda g, k: (k, g)), pl.BlockSpec((tk, gw), lambda g, k: (k, g))],
        out_specs=pl.BlockSpec((None, gw, gw), lambda g, k: (g, 0, 0)),
        out_shape=jax.ShapeDtypeStruct((G, gw, gw), F32),
        compiler_params=_params(2),
    )(pooled, dy)


def _res_ln_epi(acc, xh, gp_, bp_, g, b):
    xhat, rstd, xo = _ln(ALPHA * (xh * gp_ + bp_) + acc, g, b)
    return xhat, _wide(rstd, acc.shape[0]), xo


def _res_ln_ffpe_epi(acc, xh, gate, pp, gp_, bp_, g, b):
    return _res_ln_epi(acc + jax.nn.sigmoid(gate) * pp, xh, gp_, bp_, g, b)


def _final_ln_loss_epi(acc, xh, gate, pp, tgt, gp_, bp_, g, b):
    sg = jax.nn.sigmoid(gate)
    xhat, rstd, y = _ln(ALPHA * (xh * gp_ + bp_) + acc + sg * pp, g, b)
    e = y - tgt
    dh, dg, db = _ln_bwd(e * (1.0 / acc.shape[1]), xhat, rstd, g)
    dgt, dpp, dbg = _ple_grads(dh, sg, pp)
    return dh, dh, dgt, dpp, jnp.sum(e * e, axis=0, keepdims=True), dg, db, dbg


def _ln_bwd_epi(acc, rest, xhat, rstd, g):
    dh, dg, db = _ln_bwd(acc + rest, xhat, rstd[:, :1], g)
    return dh, dh, dg, db


def _ple_grads(dh, sg, pp):
    dgt = dh * pp * sg * (1.0 - sg)
    return dgt, dh * sg, jnp.sum(dgt, axis=0, keepdims=True)


def _ln_bwd_ple_epi(acc, rest, xhat, rstd, gate, pp, g):
    dh, dg, db = _ln_bwd(acc + rest, xhat, rstd[:, :1], g)
    dgt, dpp, dbg = _ple_grads(dh, jax.nn.sigmoid(gate), pp)
    return dh, dh, dgt, dpp, dg, db, dbg


def _silu(c):
    return c * jax.nn.sigmoid(c)


def _qkv_point(c, is_qk, scale):
    s = _silu(c)
    nrm = s * lax.rsqrt(jnp.sum(s * s, axis=-1, keepdims=True) + L2_EPS) * scale
    return jnp.where(is_qk, nrm, s)


def _conv_rows(xx, wv, lo, rows):
    acc = None
    for j in range(CONV_WIDTH):
        sh = CONV_WIDTH - 1 - j
        term = (pltpu.roll(xx, sh, axis=0) if sh else xx)[lo : lo + rows, :] * wv[j : j + 1, :]
        acc = term if acc is None else acc + term
    return acc


def _conv_fwd(qkv_pre, conv_w):
    S, W = qkv_pre.shape
    D = W // 3
    H = D // HEAD_DIM
    ts = min(CONV_TILE, S)
    r = ts // CONV_HALO

    def body(x_ref, xp_ref, w_ref, o_ref):
        j, i = pl.program_id(0), pl.program_id(1)
        xp = jnp.where(i > 0, xp_ref[...], 0.0)
        xx = jnp.concatenate([xp, x_ref[...]], axis=0)
        c = _conv_rows(xx, w_ref[...], CONV_HALO, ts)
        scale = jnp.where(j == 0, HEAD_DIM**-0.5, 1.0).astype(F32)
        for h in range(H):
            sl = slice(h * HEAD_DIM, (h + 1) * HEAD_DIM)
            o_ref[:, sl] = _qkv_point(c[:, sl], j < 2, scale)

    return pl.pallas_call(
        body,
        name="gdn_conv_fwd",
        grid=(3, S // ts),
        in_specs=[
            pl.BlockSpec((ts, D), lambda j, i: (i, j)),
            pl.BlockSpec((CONV_HALO, D), lambda j, i: (jnp.maximum(i * r - 1, 0), j)),
            pl.BlockSpec((CONV_WIDTH, D), lambda j, i: (0, j)),
        ],
        out_specs=pl.BlockSpec((ts, D), lambda j, i: (i, j)),
        out_shape=jax.ShapeDtypeStruct((S, W), F32),
        compiler_params=_params(2),
    )(qkv_pre, qkv_pre, conv_w)


def _conv_bwd(qkv_pre, conv_w, dqkvn):
    S, W = qkv_pre.shape
    D = W // 3
    H = D // HEAD_DIM
    ts = min(CONV_TILE, S)
    r, nb = ts // CONV_HALO, S // CONV_HALO
    te = ts + CONV_HALO

    def body(x_ref, xp_ref, xn_ref, w_ref, d_ref, dn_ref, dx_ref, dw_ref):
        j, i = pl.program_id(0), pl.program_id(1)
        n = pl.num_programs(1)
        wv = w_ref[...]
        xp = jnp.where(i > 0, xp_ref[...], 0.0)
        xx = jnp.concatenate([xp, x_ref[...], xn_ref[...]], axis=0)
        xr = [pltpu.roll(xx, sh, axis=0) if sh else xx for sh in range(CONV_WIDTH)]
        c = None
        for jj in range(CONV_WIDTH):
            term = xr[CONV_WIDTH - 1 - jj][CONV_HALO : CONV_HALO + te, :] * wv[jj : jj + 1, :]
            c = term if c is None else c + term
        dn = jnp.where(i < n - 1, dn_ref[...], 0.0)
        dout = jnp.concatenate([d_ref[...], dn], axis=0)
        scale = jnp.where(j == 0, HEAD_DIM**-0.5, 1.0).astype(F32)
        dcs = []
        for h in range(H):
            sl = slice(h * HEAD_DIM, (h + 1) * HEAD_DIM)
            _, vjp = jax.vjp(lambda cc: _qkv_point(cc, j < 2, scale), c[:, sl])
            dcs.append(vjp(dout[:, sl])[0])
        dc = jnp.concatenate(dcs, axis=1)
        dx = None
        dws = []
        for jj in range(CONV_WIDTH):
            sh = CONV_WIDTH - 1 - jj
            term = pltpu.roll(dc, CONV_HALO - sh, axis=0)[CONV_HALO:, :] * wv[jj : jj + 1, :]
            dx = term if dx is None else dx + term
            dws.append(jnp.sum(dc[:ts, :] * xr[sh][CONV_HALO : CONV_HALO + ts, :], axis=0, keepdims=True))
        dx_ref[...] = dx.astype(dx_ref.dtype)
        dw = jnp.concatenate(dws, axis=0)

        @pl.when(i == 0)
        def _():
            dw_ref[...] = dw

        @pl.when(i > 0)
        def _():
            dw_ref[...] += dw

    return pl.pallas_call(
        body,
        name="gdn_conv_bwd",
        grid=(3, S // ts),
        in_specs=[
            pl.BlockSpec((ts, D), lambda j, i: (i, j)),
            pl.BlockSpec((CONV_HALO, D), lambda j, i: (jnp.maximum(i * r - 1, 0), j)),
            pl.BlockSpec((CONV_HALO, D), lambda j, i: (jnp.minimum((i + 1) * r, nb - 1), j)),
            pl.BlockSpec((CONV_WIDTH, D), lambda j, i: (0, j)),
            pl.BlockSpec((ts, D), lambda j, i: (i, j)),
            pl.BlockSpec((CONV_HALO, D), lambda j, i: (jnp.minimum((i + 1) * r, nb - 1), j)),
        ],
        out_specs=[pl.BlockSpec((ts, D), lambda j, i: (i, j)), pl.BlockSpec((CONV_WIDTH, D), lambda j, i: (0, j))],
        out_shape=[jax.ShapeDtypeStruct((S, W), BF16), jax.ShapeDtypeStruct((CONV_WIDTH, W), F32)],
        compiler_params=_params(2),
    )(qkv_pre, qkv_pre, qkv_pre, conv_w, dqkvn, dqkvn)


def _softplus(x):
    pos = x > 0.0
    return jnp.where(pos, x, 0.0) + jnp.log(1.0 + jnp.exp(jnp.where(pos, -x, x)))


def _gates(bl, al, alog, dt):
    return jax.nn.sigmoid(bl), -jnp.exp(alog) * _softplus(al + dt)


def _gates_fwd(ba, alog, dt):
    S = ba.shape[0]
    ts = min(ROW_TILE, S)

    def fn(i, n, rv, hv, cv):
        return _gates(rv[0][:, :LANES], rv[0][:, LANES:], cv[0], cv[1]), ()

    return _rowwise("gdn_gates_fwd", fn, S, ts, rows=[ba], consts=[alog, dt], outs=[(LANES, F32), (LANES, F32)])


def _gates_bwd(ba, alog, dt, dbeta, dg, H):
    S = ba.shape[0]
    ts = min(ROW_TILE, S)

    def fn(i, n, rv, hv, cv):
        bav, dbv, dgv = rv
        real = lax.broadcasted_iota(jnp.int32, (1, LANES), 1) < H
        _, vjp = jax.vjp(_gates, bav[:, :LANES], bav[:, LANES:], cv[0], cv[1])
        dbl, dal, dalog, ddt = vjp((jnp.where(real, dbv, 0.0), jnp.where(real, dgv, 0.0)))
        dbl, dal = jnp.where(real, dbl, 0.0), jnp.where(real, dal, 0.0)
        return (jnp.concatenate([dbl, dal], axis=1),), (jnp.where(real, dalog, 0.0), jnp.where(real, ddt, 0.0))

    return _rowwise(
        "gdn_gates_bwd", fn, S, ts, rows=[ba, dbeta, dg], consts=[alog, dt], outs=[(2 * LANES, BF16)],
        accs=[(1, LANES), (1, LANES)],
    )


def _split_bf16(a, n):
    parts, rest = [], a
    for _ in range(n):
        piece = rest.astype(BF16)
        parts.append(piece)
        rest = rest - piece.astype(F32)
    return parts


def _tri_dot(a, b, mode, tri):
    d = lambda u, v: lax.dot_general(u, v, _DIMS[mode], preferred_element_type=F32)
    if tri == 0:
        return sum(d(a.astype(BF16), piece) for piece in _split_bf16(b, 3))
    return sum(d(piece, b.astype(BF16)) for piece in _split_bf16(a, 3))


def _bdot_raw(a, b, mode):
    return lax.dot_general(a.astype(BF16), b.astype(BF16), _DIMS[mode], preferred_element_type=F32)


@functools.partial(jax.custom_vjp, nondiff_argnums=(2,))
def _bdot(a, b, mode):
    return _bdot_raw(a, b, mode)


def _bdot_fwd(a, b, mode):
    return _bdot_raw(a, b, mode), (a, b)


def _bdot_bwd(mode, res, ct):
    a, b = res
    if mode == "nn":
        return _bdot(ct, b, "nt"), _bdot(a, ct, "tn")
    if mode == "nt":
        return _bdot(ct, b, "nn"), _bdot(ct, a, "tn")
    return _bdot(b, ct, "nt"), _bdot(a, ct, "nn")


_bdot.defvjp(_bdot_fwd, _bdot_bwd)


@jax.custom_vjp
def _unit_lower_inverse(a_strict):
    return _unit_lower_inverse_raw(a_strict)


def _unit_lower_inverse_fwd(a_strict):
    t = _unit_lower_inverse_raw(a_strict)
    return t, t


def _unit_lower_inverse_bwd(t, ct):
    left = [_bdot(ti, ci, "tn") for ti, ci in zip(t, ct)]
    return (tuple(-_bdot(li, ti, "nt") for li, ti in zip(left, t)),)


_unit_lower_inverse.defvjp(_unit_lower_inverse_fwd, _unit_lower_inverse_bwd)


@jax.custom_vjp
def _saved_inverse(a_strict, t):
    return t


def _saved_inverse_fwd(a_strict, t):
    return t, t


def _saved_inverse_bwd(t, ct):
    return _unit_lower_inverse_bwd(t, ct) + (tuple(jnp.zeros_like(ti) for ti in t),)


_saved_inverse.defvjp(_saved_inverse_fwd, _saved_inverse_bwd)


def _unit_lower_inverse_raw(a_strict):
    C = a_strict[0].shape[0]
    ii = lax.broadcasted_iota(jnp.int32, (C, C), 0)
    jj = lax.broadcasted_iota(jnp.int32, (C, C), 1)
    eye = (ii == jj).astype(F32)
    blk = 16
    same = (ii // blk) == (jj // blk)
    p = [-jnp.where(same, a, 0.0) for a in a_strict]
    t = [eye + x for x in p]
    for _ in range(3):
        p = [_bdot(x, x, "nn") for x in p]
        t = [ti + _bdot(ti, x, "nn") for ti, x in zip(t, p)]
    while blk < C:
        same2 = (ii // (2 * blk)) == (jj // (2 * blk))
        off = jnp.logical_and(same2, jnp.logical_not(same))
        te = [_bdot(ti, jnp.where(off, a, 0.0), "nn") for ti, a in zip(t, a_strict)]
        t = [ti - _bdot(x, ti, "nn") for ti, x in zip(t, te)]
        same, blk = same2, 2 * blk
    return tuple(t)


def _chunk_heads(q, k, v, gc_col, gc_row, b_col, s0, t_saved=None, with_t=False):
    R = range(len(q))
    C = q[0].shape[0]
    ii = lax.broadcasted_iota(jnp.int32, (C, C), 0)
    jj = lax.broadcasted_iota(jnp.int32, (C, C), 1)
    rows = lax.broadcasted_iota(jnp.int32, (C, 1), 0)
    decay = [jnp.where(ii >= jj, jnp.exp(jnp.minimum(gc_col[h] - gc_row[h], 0.0)), 0.0) for h in R]
    kb = [k[h] * b_col[h] for h in R]
    a = [_bdot(kb[h], k[h], "nt") * decay[h] for h in R]
    qk = [_bdot(q[h], k[h], "nt") * decay[h] for h in R]
    a_strict = tuple(jnp.where(ii > jj, a[h], 0.0) for h in R)
    t = _unit_lower_inverse(a_strict) if t_saved is None else _saved_inverse(a_strict, t_saved)
    eg = [jnp.exp(gc_col[h]) for h in R]
    u = [_bdot(t[h], v[h] * b_col[h], "nn") for h in R]
    w = [_bdot(t[h], kb[h] * eg[h], "nn") for h in R]
    g_last = [jnp.sum(jnp.where(rows == C - 1, gc_col[h], 0.0), axis=0, keepdims=True) for h in R]
    kd = [k[h] * jnp.exp(g_last[h] - gc_col[h]) for h in R]
    ws = [_bdot(w[h], s0[h], "nn") for h in R]
    qs = [_bdot(q[h] * eg[h], s0[h], "nn") for h in R]
    v_new = [u[h] - ws[h] for h in R]
    o = [qs[h] + _bdot(qk[h], v_new[h], "nn") for h in R]
    s1 = [s0[h] * jnp.exp(g_last[h]) + _bdot(kd[h], v_new[h], "tn") for h in R]
    return (tuple(o), tuple(s1), t) if with_t else (tuple(o), tuple(s1))


def _pick_lane(a, h):
    lanes = lax.broadcasted_iota(jnp.int32, a.shape, 1)
    return jnp.sum(jnp.where(lanes == h, a, 0.0), axis=1, keepdims=True)


def _pick_row(a, h):
    rows = lax.broadcasted_iota(jnp.int32, a.shape, 0)
    return jnp.sum(jnp.where(rows == h, a, 0.0), axis=0, keepdims=True)


def _tri(C):
    ii = lax.broadcasted_iota(jnp.int32, (C, C), 0)
    jj = lax.broadcasted_iota(jnp.int32, (C, C), 1)
    return (ii >= jj).astype(F32)


def _delta_fwd(qkvn, g_pad, g_rows, beta_pad):
    S, W = qkvn.shape
    D = W // 3
    H = D // HEAD_DIM
    C = min(CHUNK, S)
    N = S // C

    def body(x_ref, gp_ref, gr_ref, bp_ref, o_ref, sall_ref, tall_ref, st):
        n = pl.program_id(0)

        @pl.when(n == 0)
        def _():
            st[...] = jnp.zeros_like(st)

        low = _tri(C)
        gc_cols = _tri_dot(low, gp_ref[...], "nn", 0)
        gc_rows = _tri_dot(gr_ref[...], low, "nt", 1)
        bcols = bp_ref[...]
        hs = range(H)
        s0 = tuple(st[h] for h in hs)
        for h in hs:
            sall_ref[h] = s0[h]
        o, s1, t = _chunk_heads(
            tuple(x_ref[:, h * HEAD_DIM : (h + 1) * HEAD_DIM] for h in hs),
            tuple(x_ref[:, D + h * HEAD_DIM : D + (h + 1) * HEAD_DIM] for h in hs),
            tuple(x_ref[:, 2 * D + h * HEAD_DIM : 2 * D + (h + 1) * HEAD_DIM] for h in hs),
            tuple(_pick_lane(gc_cols, h) for h in hs), tuple(_pick_row(gc_rows, h) for h in hs),
            tuple(_pick_lane(bcols, h) for h in hs), s0, with_t=True,
        )
        for h in hs:
            st[h] = s1[h]
            o_ref[:, h * HEAD_DIM : (h + 1) * HEAD_DIM] = o[h]
            tall_ref[h] = t[h].astype(tall_ref.dtype)

    return pl.pallas_call(
        body,
        name="gdn_delta_fwd",
        grid=(N,),
        in_specs=[
            pl.BlockSpec((C, W), lambda n: (n, 0)),
            pl.BlockSpec((C, LANES), lambda n: (n, 0)),
            pl.BlockSpec((None, 8, C), lambda n: (n, 0, 0)),
            pl.BlockSpec((C, LANES), lambda n: (n, 0)),
        ],
        out_specs=[pl.BlockSpec((C, D), lambda n: (n, 0)), pl.BlockSpec((None, H, HEAD_DIM, HEAD_DIM), lambda n: (n, 0, 0, 0)),
                   pl.BlockSpec((None, H, C, C), lambda n: (n, 0, 0, 0))],
        out_shape=[jax.ShapeDtypeStruct((S, D), F32), jax.ShapeDtypeStruct((N, H, HEAD_DIM, HEAD_DIM), F32), jax.ShapeDtypeStruct((N, H, C, C), BF16)],
        scratch_shapes=[pltpu.VMEM((H, HEAD_DIM, HEAD_DIM), F32)],
        compiler_params=_params(1),
    )(qkvn, g_pad, g_rows, beta_pad)


def _delta_bwd(qkvn, g_pad, g_rows, beta_pad, s_all, t_all, do):
    S, W = qkvn.shape
    D = W // 3
    H = D // HEAD_DIM
    C = min(CHUNK, S)
    N = S // C

    def body(x_ref, gp_ref, gr_ref, bp_ref, sall_ref, tall_ref, do_ref, dx_ref, dgp_ref, dgr_ref, dbp_ref, dst):
        n = pl.program_id(0)

        @pl.when(n == 0)
        def _():
            dst[...] = jnp.zeros_like(dst)

        low = _tri(C)
        gc_cols = _tri_dot(low, gp_ref[...], "nn", 0)
        gc_rows = _tri_dot(gr_ref[...], low, "nt", 1)
        bcols = bp_ref[...]
        lane = lax.broadcasted_iota(jnp.int32, (1, LANES), 1)
        row8 = lax.broadcasted_iota(jnp.int32, (8, 1), 0)
        dgc_cols = jnp.zeros((C, LANES), F32)
        dgc_rows = jnp.zeros((8, C), F32)
        dbcols = jnp.zeros((C, LANES), F32)
        hs = range(H)
        t_saved = tuple(tall_ref[h].astype(F32) for h in hs)
        _, vjp = jax.vjp(
            lambda *args: _chunk_heads(*args, t_saved=t_saved),
            tuple(x_ref[:, h * HEAD_DIM : (h + 1) * HEAD_DIM] for h in hs),
            tuple(x_ref[:, D + h * HEAD_DIM : D + (h + 1) * HEAD_DIM] for h in hs),
            tuple(x_ref[:, 2 * D + h * HEAD_DIM : 2 * D + (h + 1) * HEAD_DIM] for h in hs),
            tuple(_pick_lane(gc_cols, h) for h in hs), tuple(_pick_row(gc_rows, h) for h in hs),
            tuple(_pick_lane(bcols, h) for h in hs), tuple(sall_ref[h] for h in hs),
        )
        dq, dk, dv, dgc, dgr, dbc, ds0 = vjp((tuple(do_ref[:, h * HEAD_DIM : (h + 1) * HEAD_DIM] for h in hs), tuple(dst[h] for h in hs)))
        for h in hs:
            dst[h] = ds0[h]
            dx_ref[:, h * HEAD_DIM : (h + 1) * HEAD_DIM] = dq[h]
            dx_ref[:, D + h * HEAD_DIM : D + (h + 1) * HEAD_DIM] = dk[h]
            dx_ref[:, 2 * D + h * HEAD_DIM : 2 * D + (h + 1) * HEAD_DIM] = dv[h]
            dgc_cols = dgc_cols + dgc[h] * (lane == h).astype(F32)
            dgc_rows = dgc_rows + dgr[h] * (row8 == h).astype(F32)
            dbcols = dbcols + dbc[h] * (lane == h).astype(F32)
        dgp_ref[...] = _tri_dot(low, dgc_cols, "tn", 0)
        dgr_ref[...] = _tri_dot(dgc_rows, low, "nn", 1)
        dbp_ref[...] = dbcols

    rev = lambda n: N - 1 - n
    return pl.pallas_call(
        body,
        name="gdn_delta_bwd",
        grid=(N,),
        in_specs=[
            pl.BlockSpec((C, W), lambda n: (rev(n), 0)),
            pl.BlockSpec((C, LANES), lambda n: (rev(n), 0)),
            pl.BlockSpec((None, 8, C), lambda n: (rev(n), 0, 0)),
            pl.BlockSpec((C, LANES), lambda n: (rev(n), 0)),
            pl.BlockSpec((None, H, HEAD_DIM, HEAD_DIM), lambda n: (rev(n), 0, 0, 0)),
            pl.BlockSpec((None, H, C, C), lambda n: (rev(n), 0, 0, 0)),
            pl.BlockSpec((C, D), lambda n: (rev(n), 0)),
        ],
        out_specs=[
            pl.BlockSpec((C, W), lambda n: (rev(n), 0)),
            pl.BlockSpec((C, LANES), lambda n: (rev(n), 0)),
            pl.BlockSpec((None, 8, C), lambda n: (rev(n), 0, 0)),
            pl.BlockSpec((C, LANES), lambda n: (rev(n), 0)),
        ],
        out_shape=[
            jax.ShapeDtypeStruct((S, W), F32),
            jax.ShapeDtypeStruct((S, LANES), F32),
            jax.ShapeDtypeStruct((N, 8, C), F32),
            jax.ShapeDtypeStruct((S, LANES), F32),
        ],
        scratch_shapes=[pltpu.VMEM((H, HEAD_DIM, HEAD_DIM), F32)],
        compiler_params=_params(1),
    )(qkvn, g_pad, g_rows, beta_pad, s_all, t_all, do)


def _gate_norm_head(o, z, nw):
    return o * lax.rsqrt(jnp.mean(o * o, axis=-1, keepdims=True) + RMS_EPS) * nw * _silu(z)


def _gate_norm_fwd(o, z, nw):
    S, D = o.shape
    H = D // HEAD_DIM
    ts = min(ROW_TILE, S)

    def fn(i, n, rv, hv, cv):
        ov, zv = rv
        parts = [_gate_norm_head(ov[:, h * HEAD_DIM : (h + 1) * HEAD_DIM], zv[:, h * HEAD_DIM : (h + 1) * HEAD_DIM], cv[0]) for h in range(H)]
        return (jnp.concatenate(parts, axis=1),), ()

    return _rowwise("gdn_gate_norm_fwd", fn, S, ts, rows=[o, z], consts=[nw], outs=[(D, BF16)])[0]


def _gate_norm_bwd_epi(dog, o, z, nw):
    D = dog.shape[1]
    dos, dzs, dnw = [], [], None
    for h in range(D // HEAD_DIM):
        sl = slice(h * HEAD_DIM, (h + 1) * HEAD_DIM)
        _, vjp = jax.vjp(_gate_norm_head, o[:, sl], z[:, sl], nw)
        a, b_, c_ = vjp(dog[:, sl])
        dos.append(a)
        dzs.append(b_)
        dnw = c_ if dnw is None else dnw + c_
    wide = jnp.concatenate([dnw, jnp.zeros((1, D - HEAD_DIM), F32)], axis=1) if D > HEAD_DIM else dnw
    return jnp.concatenate(dos, axis=1), jnp.concatenate(dzs, axis=1), wide


def _square_bf16(r):
    return r * r


def _mlp_ple_dw(li, dhb, dgate, dpp, xa, p, r, w2):
    dpre = _mm(f"l{li}_mlp_down_bwd", dhb, w2, "nt", [BF16], epi=lambda acc, rr: (acc * (2.0 * rr.astype(F32)),), extras=[(r, "tile")], tm=1024, tn=1024, b_outer=True)
    dw2 = _mm(f"l{li}_mlp_dw2", r, dhb, "tn", [BF16], a_fn=_square_bf16, **DW_TILES)
    dw1 = _mm(f"l{li}_mlp_dw1", xa, dpre, "tn", [BF16], out_blocks=N_DEV, **DW_TILES)
    dwg = _mm(f"l{li}_ple_dwg", xa, dgate, "tn", [BF16], **DW_TILES)
    dwp = _mm(f"l{li}_ple_dwp", p, dpp, "tn", [BF16], out_blocks=N_DEV, **DW_TILES)
    rows = lambda a: a.reshape((N_DEV, a.shape[0] // N_DEV) + a.shape[1:])
    return dpre, dw1, rows(dw2), rows(dwg), dwp


def _mlp_ple_dx(li, dh, dpre, dgate, w1, wg, after, xhat, rstd, g):
    t = _mm(f"l{li}_ple_gate_bwd", dgate, wg, "nt", [F32], epi=lambda acc, d: (acc + ALPHA * d,), extras=[(dh, "tile")], tm=1024, after=after)
    return _mm(f"l{li}_mlp_up_bwd", dpre, w1, "nt", [F32, BF16], epi=_ln_bwd_epi, extras=[(t, "tile"), (xhat, "tile"), (rstd, "rows"), (g, "row")],
               tm=256, tk=4096, accs=2)


def _local_step(x, p, tgt, W, fetch, emit):
    S, D = x.shape
    H = D // HEAD_DIM
    C = min(CHUNK, S)
    N = S // C
    lg = lambda i, j: W["ln_gain"][2 * i + j][None, :]
    lb = lambda i, j: W["ln_bias"][2 * i + j][None, :]
    G = {}

    pooled, xh0a, rs0a, x0a = _pool_fwd(x, W["pool_w"], W["pool_b"], W["pool_scale"], lg(0, 0), lb(0, 0))
    w0a = fetch("l0a", x0a)
    r0 = _mm("l0_mlp_up", x0a, w0a["mlp_w1"], "nn", [BF16], epi=lambda acc: (jnp.maximum(acc, 0.0),), tm=1024, tn=1024, b_outer=True, after=w0a.get("_after", ()))
    w0b = fetch("l0b", r0)
    gate0 = _mm("l0_ple_gate", x0a, w0b["ple_gate_w"], "nn", [F32], epi=lambda acc, bias: (acc + bias,), extras=[(W["ple_gate_b"][0:1], "row")], tm=1024)
    pp0 = _mm("l0_ple_proj", p[0], w0b["ple_proj"], "nn", [F32])
    w0c = fetch("l0c", pp0)
    ln_rows = lambda i, j, i2, j2: [(lg(i, j), "row"), (lb(i, j), "row"), (lg(i2, j2), "row"), (lb(i2, j2), "row")]
    xh0b, rs0b, x0b = _mm("l0_mlp_down", r0, w0c["mlp_w2"], "nn", [F32, (F32, LANES), BF16], a_fn=_square_bf16, tm=256, tn=D, tk=4096, epi=_res_ln_ffpe_epi,
                          extras=[(xh0a, "tile"), (gate0, "tile"), (pp0, "tile")] + ln_rows(0, 0, 0, 1), after=w0c.get("_after", ()))

    wg_ = fetch("gdn", x0b)
    qkv_pre = _mm("gdn_in_qkv", x0b, wg_["gdn_wqkv"], "nn", [F32], tm=1024, tn=1024, b_outer=True, after=wg_.get("_after", ()))
    z = _mm("gdn_in_z", x0b, wg_["gdn_wz"], "nn", [F32], tm=1024)
    ba = _mm("gdn_in_ba", x0b, wg_["gdn_wba"], "nn", [F32])
    qkvn = _conv_fwd(qkv_pre, W["gdn_conv"])
    beta_pad, g_pad = _gates_fwd(ba, W["gdn_a_log"], W["gdn_dt_bias"])
    g_rows = g_pad[:, :8].reshape(N, C, 8).transpose(0, 2, 1)
    o, s_all, t_all = _delta_fwd(qkvn, g_pad, g_rows, beta_pad)
    og = _gate_norm_fwd(o, z, W["gdn_norm_w"])
    xh1a, rs1a, x1a = _mm("gdn_out", og, wg_["gdn_w_out"], "nn", [F32, (F32, LANES), BF16], tm=512, tn=D, epi=_res_ln_epi,
                          extras=[(xh0b, "tile")] + ln_rows(0, 1, 1, 0))
    w1_ = fetch("l1", x1a)
    r1 = _mm("l1_mlp_up", x1a, w1_["mlp_w1"], "nn", [BF16], epi=lambda acc: (jnp.maximum(acc, 0.0),), tm=1024, tn=1024, b_outer=True)
    gate1 = _mm("l1_ple_gate", x1a, w1_["ple_gate_w"], "nn", [F32], epi=lambda acc, bias: (acc + bias,), extras=[(W["ple_gate_b"][1:2], "row")], tm=1024)
    pp1 = _mm("l1_ple_proj", p[1], w1_["ple_proj"], "nn", [F32])
    dh1b, dh1b_b, dgate1, dpp1, loss_cols, dg11, db11, dbg_1 = _mm(
        "l1_mlp_down", r1, w1_["mlp_w2"], "nn", [F32, BF16, BF16, BF16], a_fn=_square_bf16, tm=256, tn=D, tk=4096, epi=_final_ln_loss_epi,
        extras=[(xh1a, "tile"), (gate1, "tile"), (pp1, "tile"), (tgt, "tile")] + ln_rows(1, 0, 1, 1), accs=4)

    dpre1, dw1_1, dw2_1, dwg_1, dwp_1 = _mlp_ple_dw(1, dh1b_b, dgate1, dpp1, x1a, p[1], r1, w1_["mlp_w2"])
    tok = emit("l1", {"mlp_w1": dw1_1, "mlp_w2": dw2_1, "ple_gate_w": dwg_1, "ple_proj": dwp_1})
    dh1a, dh1a_b, dg10, db10 = _mlp_ple_dx(1, dh1b, dpre1, dgate1, w1_["mlp_w1"], w1_["ple_gate_w"], [tok], xh1a, rs1a, lg(1, 0))
    do, dz, dnw = _mm("gdn_out_bwd", dh1a_b, wg_["gdn_w_out"], "nt", [F32, BF16], tm=256, tn=D, tk=D, epi=_gate_norm_bwd_epi,
                      extras=[(o, "tile"), (z, "tile"), (W["gdn_norm_w"], "whole")], accs=1)
    dnw = dnw[:, :HEAD_DIM]
    dw_out = _mm("gdn_dw_out", og, dh1a_b, "tn", [BF16], **DW_TILES)
    dw_out = dw_out.reshape((N_DEV, dw_out.shape[0] // N_DEV) + dw_out.shape[1:])
    dqkvn, dg_col, dg_row, dbeta = _delta_bwd(qkvn, g_pad, g_rows, beta_pad, s_all, t_all, do)
    dg_all = dg_col + jnp.pad(dg_row.transpose(0, 2, 1).reshape(S, 8), ((0, 0), (0, LANES - 8)))
    dba, dalog, ddt = _gates_bwd(ba, W["gdn_a_log"], W["gdn_dt_bias"], dbeta, dg_all, H)
    dqkv, dconv = _conv_bwd(qkv_pre, W["gdn_conv"], dqkvn)
    dwqkv = _mm("gdn_dwqkv", x0b, dqkv, "tn", [F32], **DW_TILES)
    dwz = _mm("gdn_dwz", x0b, dz, "tn", [F32], **DW_TILES)
    dwba = _mm("gdn_dwba", x0b, dba, "tn", [F32], **DW_TILES)
    dw_in = jnp.concatenate([dwqkv, dwz, dwba[:, :H], dwba[:, LANES : LANES + H]], axis=1)
    tok = emit("gdn", {"gdn_w_in": _split_blocks("gdn_w_in", dw_in).astype(BF16), "gdn_w_out": dw_out})
    t = _mm("gdn_in_ba_bwd", dba, wg_["gdn_wba"], "nt", [F32], epi=lambda acc, d: (acc + ALPHA * d,), extras=[(dh1a, "tile")], after=[tok])
    t = _mm("gdn_in_z_bwd", dz, wg_["gdn_wz"], "nt", [F32], epi=lambda acc, d: (acc + d,), extras=[(t, "tile")], tm=1024)
    dh0b, dh0b_b, dgate0, dpp0, dg01, db01, dbg_0 = _mm(
        "gdn_in_qkv_bwd", dqkv, wg_["gdn_wqkv"], "nt", [F32, BF16, BF16, BF16], epi=_ln_bwd_ple_epi,
        extras=[(t, "tile"), (xh0b, "tile"), (rs0b, "rows"), (gate0, "tile"), (pp0, "tile"), (lg(0, 1), "row")], tm=256, tk=3072, accs=3)

    dpre0, dw1_0, dw2_0, dwg_0, dwp_0 = _mlp_ple_dw(0, dh0b_b, dgate0, dpp0, x0a, p[0], r0, w0c["mlp_w2"])
    tok = emit("l0", {"mlp_w1": dw1_0, "mlp_w2": dw2_0, "ple_gate_w": dwg_0, "ple_proj": dwp_0})
    dh0a, _, dg00, db00 = _mlp_ple_dx(0, dh0b, dpre0, dgate0, w0a["mlp_w1"], w0b["ple_gate_w"], [tok], xh0a, rs0a, lg(0, 0))
    grad_x, dyp, dscale, dpb = _pool_bwd(dh0a, pooled, W["pool_w"], W["pool_b"], W["pool_scale"])
    G["pool_w"] = _pool_dw(pooled, dyp)

    G["ln_gain"] = jnp.concatenate([dg00, dg01, dg10, dg11], axis=0)
    G["ln_bias"] = jnp.concatenate([db00, db01, db10, db11], axis=0)
    G["pool_b"] = dpb
    G["pool_scale"] = dscale
    G["gdn_conv"] = dconv
    G["gdn_a_log"] = dalog[:, :H]
    G["gdn_dt_bias"] = ddt[:, :H]
    G["gdn_norm_w"] = dnw
    G["ple_gate_b"] = jnp.concatenate([dbg_0, dbg_1], axis=0)
    return loss_cols, grad_x, G


_HBM = pl.BlockSpec(memory_space=pltpu.HBM)


def _all_gather(name, shards):
    T = len(shards)

    def body(*refs):
        ins, outs = refs[:T], refs[T : 2 * T]
        send_sems, recv_sems, local_sems = refs[2 * T :]
        x, y, c = lax.axis_index("x"), lax.axis_index("y"), lax.axis_index("c")
        me, sibling = (x, y, c), (x, y, 1 - c)
        chips = [(1 - x, y), (x, 1 - y), (1 - x, 1 - y)]

        def blk(t, px, py, pc):
            return outs[t].at[4 * px + 2 * py + pc]

        def copy(t, k, block, to, src=None):
            return pltpu.make_async_remote_copy(
                src_ref=blk(t, *block) if src is None else src, dst_ref=blk(t, *block),
                send_sem=send_sems.at[t, k], recv_sem=recv_sems.at[t, k], device_id=to, device_id_type=MESH,
            )

        mine = [pltpu.make_async_copy(ins[t], blk(t, *me), local_sems.at[t]) for t in range(T)]
        for cp in mine:
            cp.start()
        first = []
        for t in range(T):
            first.append(copy(t, 0, me, sibling, src=ins[t]))
            first += [copy(t, 1 + j, me, (*chip, c), src=ins[t]) for j, chip in enumerate(chips)]
        for cp in first:
            cp.start()
        passed = []
        for j, chip in enumerate(chips):
            for t in range(T):
                copy(t, 1 + j, (*chip, c), me).wait_recv()
                fw = copy(t, 4 + j, (*chip, c), sibling)
                fw.start()
                passed.append(fw)
        for t in range(T):
            copy(t, 0, sibling, me).wait_recv()
            for j, chip in enumerate(chips):
                copy(t, 4 + j, (*chip, 1 - c), me).wait_recv()
        for cp in first + passed:
            cp.wait_send()
        for cp in mine:
            cp.wait()

    return pl.pallas_call(
        body,
        name=name,
        in_specs=[_HBM] * T,
        out_specs=[_HBM] * T,
        out_shape=[jax.ShapeDtypeStruct((N_DEV,) + s.shape, s.dtype) for s in shards],
        scratch_shapes=[pltpu.SemaphoreType.DMA((T, 7)), pltpu.SemaphoreType.DMA((T, 7)), pltpu.SemaphoreType.DMA((T,))],
    )(*shards)


def _exchange(name, blocks):
    def body(g_ref, o_ref, send_sems, recv_sems, local_sem):
        x, y, c = lax.axis_index("x"), lax.axis_index("y"), lax.axis_index("c")
        own = pltpu.make_async_copy(g_ref.at[4 * x + 2 * y + c], o_ref.at[N_DEV - 1], local_sem)
        own.start()
        copies = []
        for rel in range(1, N_DEV):
            px = 1 - x if rel & 4 else x
            py = 1 - y if rel & 2 else y
            pc = 1 - c if rel & 1 else c
            copies.append(
                pltpu.make_async_remote_copy(
                    src_ref=g_ref.at[4 * px + 2 * py + pc], dst_ref=o_ref.at[rel - 1],
                    send_sem=send_sems.at[rel - 1], recv_sem=recv_sems.at[rel - 1], device_id=(px, py, pc), device_id_type=MESH,
                )
            )
        for cp in copies:
            cp.start()
        for cp in copies:
            cp.wait_recv()
        for cp in copies:
            cp.wait_send()
        own.wait()

    return pl.pallas_call(
        body,
        name=name,
        in_specs=[_HBM],
        out_specs=_HBM,
        out_shape=jax.ShapeDtypeStruct(blocks.shape, blocks.dtype),
        scratch_shapes=[pltpu.SemaphoreType.DMA((N_DEV - 1,)), pltpu.SemaphoreType.DMA((N_DEV - 1,)), pltpu.SemaphoreType.DMA],
    )(blocks)


_SEM = pl.BlockSpec(memory_space=pltpu.SEMAPHORE)
_ANY = pl.BlockSpec(memory_space=pl.ANY)
_DATAFLOW = pltpu.SideEffectType.DATAFLOW_SIDE_EFFECTING
N_PEERS = N_DEV - 1


def _peer(rel, x, y, c):
    return (1 - x if rel & 4 else x, 1 - y if rel & 2 else y, 1 - c if rel & 1 else c)


def _send_start(name, srcs, lands, gather, after):
    T = len(srcs)

    def body(*refs):
        src_refs, land_refs = refs[:T], refs[T : 2 * T]
        send_sems, recv_sems = refs[2 * T + 1], refs[2 * T + 2]
        token = refs[-1]
        x, y, c = lax.axis_index("x"), lax.axis_index("y"), lax.axis_index("c")
        for t in range(T):
            for rel in range(1, N_DEV):
                px, py, pc = _peer(rel, x, y, c)
                pltpu.make_async_remote_copy(
                    src_ref=src_refs[t] if gather else src_refs[t].at[4 * px + 2 * py + pc],
                    dst_ref=land_refs[t].at[4 * x + 2 * y + c] if gather else land_refs[t].at[rel - 1],
                    send_sem=send_sems.at[t * N_PEERS + rel - 1], recv_sem=recv_sems.at[t * N_PEERS + rel - 1], device_id=(px, py, pc), device_id_type=MESH,
                ).start()
        token[...] = jnp.zeros_like(token)

    hbm = lambda a: pltpu.HBM(a.shape, a.dtype)
    return pl.pallas_call(
        body,
        name=name,
        out_shape=(pltpu.SemaphoreType.DMA((T * N_PEERS,)), pltpu.SemaphoreType.DMA((T * N_PEERS,)), *[hbm(a) for a in srcs],
                   *[hbm(a) for a in lands], jax.ShapeDtypeStruct((8, LANES), F32)),
        in_specs=(_HBM,) * (2 * T) + (_ANY,),
        out_specs=(_SEM, _SEM) + (_HBM,) * (2 * T) + (pl.BlockSpec(memory_space=pltpu.VMEM),),
        input_output_aliases={t: 2 + t for t in range(2 * T)},
        compiler_params=pltpu.CompilerParams(has_side_effects=_DATAFLOW),
    )(*[pltpu.with_memory_space_constraint(a, pltpu.HBM) for a in list(srcs) + list(lands)], after)


def _send_wait(name, started, after, gather):
    T = (len(started) - 3) // 2
    send_sems, recv_sems, token = started[0], started[1], started[-1]
    thru = started[2:-1]

    def body(*refs):
        src_refs, land_refs = refs[:T], refs[T : 2 * T]
        send_sems, recv_sems = refs[2 * T], refs[2 * T + 1]
        x, y, c = lax.axis_index("x"), lax.axis_index("y"), lax.axis_index("c")
        for t in range(T):
            for rel in range(1, N_DEV):
                cp = pltpu.make_async_remote_copy(
                    src_ref=src_refs[t] if gather else src_refs[t].at[0], dst_ref=land_refs[t].at[0],
                    send_sem=send_sems.at[t * N_PEERS + rel - 1], recv_sem=recv_sems.at[t * N_PEERS + rel - 1], device_id=_peer(rel, x, y, c), device_id_type=MESH,
                )
                cp.wait_send()
                cp.wait_recv()

    outs = pl.pallas_call(
        body,
        name=name,
        out_shape=tuple(pltpu.HBM(a.shape, a.dtype) for a in thru),
        in_specs=(_HBM,) * (2 * T) + (_SEM, _SEM, _ANY),
        out_specs=(_HBM,) * (2 * T),
        input_output_aliases={t: t for t in range(2 * T)},
        compiler_params=pltpu.CompilerParams(has_side_effects=_DATAFLOW),
    )(*thru, send_sems, recv_sems, after)
    return list(outs[:T]), list(outs[T:])


def _sum_blocks(name, parts, tr):
    _, R, Cw = parts[0].shape
    tr = tr if R % tr == 0 else R

    def body(*refs):
        acc = None
        for p_ref in refs[:-1]:
            for d in range(p_ref.shape[0]):
                v = p_ref[d].astype(F32)
                acc = v if acc is None else acc + v
        refs[-1][...] = acc

    return pl.pallas_call(
        body,
        name=name,
        grid=(R // tr,),
        in_specs=[pl.BlockSpec((a.shape[0], tr, Cw), lambda i: (0, i, 0)) for a in parts],
        out_specs=pl.BlockSpec((tr, Cw), lambda i: (i, 0)),
        out_shape=jax.ShapeDtypeStruct((R, Cw), F32),
        compiler_params=_params(1),
    )(*parts)


def _adamw(name, w, g, m, v):
    shape = w.shape
    cols = shape[-1]
    rows = w.size // cols
    tr = rows if rows <= 512 else 512
    assert rows % tr == 0
    w2, g2, m2, v2 = (a.reshape(rows, cols) for a in (w, g, m, v))

    def body(w_ref, g_ref, m_ref, v_ref, d_ref, mo_ref, vo_ref):
        gv = g_ref[...]
        mn = ADAM_B1 * m_ref[...] + (1.0 - ADAM_B1) * gv
        vn = ADAM_B2 * v_ref[...] + (1.0 - ADAM_B2) * jnp.square(gv)
        m_hat = mn / (1.0 - ADAM_B1**ADAM_STEP)
        v_hat = vn / (1.0 - ADAM_B2**ADAM_STEP)
        d_ref[...] = -ADAM_LR * (m_hat / (jnp.sqrt(v_hat) + ADAM_EPS) + ADAM_WD * w_ref[...])
        mo_ref[...] = mn
        vo_ref[...] = vn

    spec = pl.BlockSpec((tr, cols), lambda i: (i, 0))
    d, mn, vn = pl.pallas_call(
        body,
        name=name,
        grid=(rows // tr,),
        in_specs=[spec] * 4,
        out_specs=[spec] * 3,
        out_shape=[jax.ShapeDtypeStruct((rows, cols), F32)] * 3,
        compiler_params=_params(1),
    )(w2, g2, m2, v2)
    return d.reshape(shape), mn.reshape(shape), vn.reshape(shape)


SMALL_SHARDED = ("ln_gain", "ln_bias", "pool_b", "gdn_conv")
SMALL_REPLICATED = ("pool_scale", "gdn_a_log", "gdn_dt_bias", "gdn_norm_w", "ple_gate_b")
WEIGHTS = ("ln_gain", "ln_bias", "pool_w", "pool_b", "pool_scale", "gdn_w_in", "gdn_conv", "gdn_a_log", "gdn_dt_bias",
           "gdn_norm_w", "gdn_w_out", "mlp_w1", "mlp_w2", "ple_gate_w", "ple_gate_b", "ple_proj")
BIG_AXIS = {"gdn_w_in": 1, "gdn_w_out": 0, "mlp_w1": 1, "mlp_w2": 0, "ple_gate_w": 0, "ple_proj": 1, "pool_w": 1}
GATHER_GROUPS = {
    "l0a": (("mlp_w1", 0),),
    "l0b": (("ple_gate_w", 0), ("ple_proj", 0)),
    "l0c": (("mlp_w2", 0),),
    "gdn": (("gdn_w_in", 0), ("gdn_w_out", 0)),
    "l1": (("mlp_w1", 1), ("mlp_w2", 1), ("ple_gate_w", 1), ("ple_proj", 1)),
}
GATHER_AFTER = {"l0b": "l0a", "l0c": "l0a", "gdn": "l0c", "l1": "gdn"}
GRAD_GROUPS = {
    "l1": (("mlp_w1", 1), ("mlp_w2", 1), ("ple_gate_w", 1), ("ple_proj", 1)),
    "gdn": (("gdn_w_in", 0), ("gdn_w_out", 0)),
    "l0": (("mlp_w1", 0), ("mlp_w2", 0), ("ple_gate_w", 0), ("ple_proj", 0)),
}
PACK_PART_ALIGN = 16
SUM_TILE = 128


def _part_rows(a, width):
    rows = a.size // width
    return rows + (-rows) % PACK_PART_ALIGN


def _pack_rows(parts, width, dtype, align):
    padded = []
    for a in parts:
        a2 = a.reshape(-1, width).astype(dtype)
        padded.append(jnp.pad(a2, ((0, _part_rows(a, width) - a2.shape[0]), (0, 0))))
    flat = jnp.concatenate(padded, axis=0)
    return jnp.pad(flat, ((0, (-flat.shape[0]) % align), (0, 0)))


def _pack_blocks(parts, width, dtype, align):
    padded = []
    for a in parts:
        a2 = a.reshape(a.shape[0], -1, width).astype(dtype)
        padded.append(jnp.pad(a2, ((0, 0), (0, _part_rows(a[0], width) - a2.shape[1]), (0, 0))))
    flat = jnp.concatenate(padded, axis=1)
    return jnp.pad(flat, ((0, 0), (0, (-flat.shape[1]) % align), (0, 0)))


def _unpack_rows(packed, shapes, width):
    out, off = [], 0
    for shp in shapes:
        size = 1
        for d in shp:
            size *= d
        out.append(packed[..., off : off + size // width, :].reshape(packed.shape[:-2] + tuple(shp)))
        off += size // width + (-(size // width)) % PACK_PART_ALIGN
    return out


def _split_blocks(name, full):
    ax = BIG_AXIS[name]
    shp = full.shape
    a = full.reshape(shp[:ax] + (N_DEV, shp[ax] // N_DEV) + shp[ax + 1 :])
    return jnp.moveaxis(a, ax, 0)


def _join_blocks(name, blocks):
    ax = BIG_AXIS[name]
    a = jnp.moveaxis(blocks, 0, ax)
    shp = a.shape
    return a.reshape(shp[:ax] + (shp[ax] * shp[ax + 1],) + shp[ax + 2 :])


def _pack_small(parts):
    flat = jnp.concatenate([jnp.pad(a.reshape(-1), (0, (-a.size) % LANES)) for a in parts])
    rows = flat.size // LANES
    return jnp.pad(flat.reshape(rows, LANES), ((0, (-rows) % 8), (0, 0)))


def _unpack_small(packed, shapes):
    flat = packed.reshape(packed.shape[:-2] + (-1,))
    out, off = [], 0
    for shp in shapes:
        size = 1
        for s in shp:
            size *= s
        out.append(flat[..., off : off + size].reshape(flat.shape[:-1] + tuple(shp)))
        off += size + (-size) % LANES
    return out


def _split_w_in(w_in, D, H):
    pad = lambda a: jnp.pad(a, ((0, 0), (0, LANES - H)))
    return w_in[:, : 3 * D], w_in[:, 3 * D : 4 * D], jnp.concatenate([pad(w_in[:, 4 * D : 4 * D + H]), pad(w_in[:, 4 * D + H :])], axis=1)


def kernel(x, p, ln_gain, ln_bias, pool_w, pool_b, pool_scale, gdn_w_in, gdn_conv, gdn_a_log, gdn_dt_bias, gdn_norm_w, gdn_w_out, mlp_w1, mlp_w2, ple_gate_w, ple_gate_b, ple_proj, loss_target, m_ln_gain, m_ln_bias, m_pool_w, m_pool_b, m_pool_scale, m_gdn_w_in, m_gdn_conv, m_gdn_a_log, m_gdn_dt_bias, m_gdn_norm_w, m_gdn_w_out, m_mlp_w1, m_mlp_w2, m_ple_gate_w, m_ple_gate_b, m_ple_proj, v_ln_gain, v_ln_bias, v_pool_w, v_pool_b, v_pool_scale, v_gdn_w_in, v_gdn_conv, v_gdn_a_log, v_gdn_dt_bias, v_gdn_norm_w, v_gdn_w_out, v_mlp_w1, v_mlp_w2, v_ple_gate_w, v_ple_gate_b, v_ple_proj):
    w_sh = dict(ln_gain=ln_gain, ln_bias=ln_bias, pool_w=pool_w, pool_b=pool_b, pool_scale=pool_scale, gdn_w_in=gdn_w_in,
                gdn_conv=gdn_conv, gdn_a_log=gdn_a_log, gdn_dt_bias=gdn_dt_bias, gdn_norm_w=gdn_norm_w, gdn_w_out=gdn_w_out,
                mlp_w1=mlp_w1, mlp_w2=mlp_w2, ple_gate_w=ple_gate_w, ple_gate_b=ple_gate_b, ple_proj=ple_proj)
    m_sh = dict(ln_gain=m_ln_gain, ln_bias=m_ln_bias, pool_w=m_pool_w, pool_b=m_pool_b, pool_scale=m_pool_scale, gdn_w_in=m_gdn_w_in,
                gdn_conv=m_gdn_conv, gdn_a_log=m_gdn_a_log, gdn_dt_bias=m_gdn_dt_bias, gdn_norm_w=m_gdn_norm_w, gdn_w_out=m_gdn_w_out,
                mlp_w1=m_mlp_w1, mlp_w2=m_mlp_w2, ple_gate_w=m_ple_gate_w, ple_gate_b=m_ple_gate_b, ple_proj=m_ple_proj)
    v_sh = dict(ln_gain=v_ln_gain, ln_bias=v_ln_bias, pool_w=v_pool_w, pool_b=v_pool_b, pool_scale=v_pool_scale, gdn_w_in=v_gdn_w_in,
                gdn_conv=v_gdn_conv, gdn_a_log=v_gdn_a_log, gdn_dt_bias=v_gdn_dt_bias, gdn_norm_w=v_gdn_norm_w, gdn_w_out=v_gdn_w_out,
                mlp_w1=v_mlp_w1, mlp_w2=v_mlp_w2, ple_gate_w=v_ple_gate_w, ple_gate_b=v_ple_gate_b, ple_proj=v_ple_proj)
    xs, tg = x[0], loss_target[0]
    ps = p[:, 0]
    S, D = xs.shape
    H = D // HEAD_DIM
    me = 4 * lax.axis_index("x") + 2 * lax.axis_index("y") + lax.axis_index("c")
    layer = lambda n, l: (w_sh[n][0] if n in ("gdn_w_in", "gdn_w_out") else w_sh[n][l])

    pool_packed = _pack_rows([w_sh["pool_w"][0]], D, BF16, PACK_PART_ALIGN)
    small_packed = _pack_small([w_sh[n] for n in SMALL_SHARDED])
    pool_gathered, small_gathered = _all_gather("gather_first", [pool_packed, small_packed])
    W = {"pool_w": _join_blocks("pool_w", _unpack_rows(pool_gathered, [w_sh["pool_w"][0].shape], D)[0])}

    started = {}

    def start(g, after):
        src = _pack_rows([layer(n, l) for n, l in GATHER_GROUPS[g]], D, BF16, PACK_PART_ALIGN)
        started[g] = tuple(_send_start(f"gather_{g}_start", [src], [lax.empty((N_DEV,) + src.shape, BF16)], True, after))
        return started[g][-1]

    first_token = start("l0a", small_gathered)
    smalls = _unpack_small(small_gathered, [w_sh[n].shape for n in SMALL_SHARDED])
    for n, a in zip(SMALL_SHARDED, smalls):
        W[n] = jnp.moveaxis(a, 0, -2).reshape(a.shape[1:-1] + (N_DEV * a.shape[-1],))
    W["ln_gain"] = W["ln_gain"].reshape(2 * DEPTH, D)
    W["ln_bias"] = W["ln_bias"].reshape(2 * DEPTH, D)
    W["pool_b"] = W["pool_b"].reshape(1, D) + first_token[0:1, 0:1]
    W["gdn_conv"] = W["gdn_conv"][0]
    W["pool_scale"] = pool_scale
    W["ple_gate_b"] = ple_gate_b
    W["gdn_norm_w"] = gdn_norm_w
    W["gdn_a_log"] = jnp.pad(gdn_a_log, ((0, 0), (0, LANES - H)))
    W["gdn_dt_bias"] = jnp.pad(gdn_dt_bias, ((0, 0), (0, LANES - H)))

    def fetch(g, after):
        members = GATHER_GROUPS[g]
        (src,), (land,) = _send_wait(f"gather_{g}_wait", started[g], after, True)
        tokens = [start(nxt, land) for nxt, prev in GATHER_AFTER.items() if prev == g]
        land = lax.dynamic_update_index_in_dim(land, src, me, 0)
        parts = _unpack_rows(land, [layer(n, l).shape for n, l in members], D)
        out = {n: _join_blocks(n, a) for (n, _), a in zip(members, parts)}
        if "gdn_w_in" in out:
            out["gdn_wqkv"], out["gdn_wz"], out["gdn_wba"] = _split_w_in(out.pop("gdn_w_in"), D, H)
        out["_after"] = tokens
        return out

    sent = {}

    def emit(g, grads):
        srcs = [grads[n] for n, _ in GRAD_GROUPS[g]]
        lands = [lax.empty((N_PEERS,) + a.shape[1:], BF16) for a in srcs]
        sent[g] = tuple(_send_start(f"grads_{g}_start", srcs, lands, False, srcs[0]))
        return sent[g][-1]

    loss_cols, grad_x, G = _local_step(xs, ps, tg, W, fetch, emit)
    loss = lax.psum(0.5 * jnp.sum(loss_cols) / D, MESH_AXES)

    pool_src = _pack_blocks([_split_blocks("pool_w", G["pool_w"])], D, BF16, PACK_PART_ALIGN)
    pool_sum = _sum_blocks("sum_pool_grads", [_exchange("exchange_pool_grads", pool_src)], SUM_TILE)
    grads = {"pool_w": _unpack_rows(pool_sum, [w_sh["pool_w"][0].shape], D)[0].reshape(w_sh["pool_w"].shape)}
    small_names = SMALL_SHARDED + SMALL_REPLICATED
    gs_packed = _pack_small([G[n] for n in small_names])
    (gs_all,) = _all_gather("gather_small_grads", [gs_packed])
    gs_sum = _sum_blocks("sum_small_grads", [gs_all], SUM_TILE)
    for n, a in zip(small_names, _unpack_small(gs_sum, [G[n].shape for n in small_names])):
        if n in SMALL_SHARDED:
            width = w_sh[n].shape[-1]
            a = a.reshape(w_sh[n].shape[:-1] + (N_DEV * width,))
            a = lax.dynamic_slice_in_dim(a, me * width, width, axis=a.ndim - 1)
        grads[n] = a.reshape(w_sh[n].shape)

    per_layer = {}
    for g, members in GRAD_GROUPS.items():
        srcs, lands = _send_wait(f"grads_{g}_wait", sent[g], grad_x, False)
        for (n, l), src, land in zip(members, srcs, lands):
            own = lax.dynamic_index_in_dim(src, me, 0, keepdims=True)
            as3d = lambda a: a.reshape(a.shape[0], -1, a.shape[-1])
            per_layer[(n, l)] = _sum_blocks(f"sum_grads_{n}_{l}", [as3d(land), as3d(own)], SUM_TILE).reshape(layer(n, l).shape)
    for n in ("gdn_w_in", "gdn_w_out"):
        grads[n] = per_layer[(n, 0)][None]
    for n in ("mlp_w1", "mlp_w2", "ple_gate_w", "ple_proj"):
        grads[n] = jnp.stack([per_layer[(n, 0)], per_layer[(n, 1)]])

    deltas, new_m, new_v = {}, {}, {}
    for n in WEIGHTS:
        deltas[n], new_m[n], new_v[n] = _adamw(f"adamw_{n}", w_sh[n], grads[n], m_sh[n], v_sh[n])
    return (loss, grad_x[None], *[grads[n] for n in WEIGHTS], *[deltas[n] for n in WEIGHTS],
            *[new_m[n] for n in WEIGHTS], *[new_v[n] for n in WEIGHTS])
```

```python
import functools

import jax
import jax.numpy as jnp
from jax import lax
from jax.experimental import pallas as pl
from jax.experimental.pallas import tpu as pltpu

F32 = jnp.float32
BF16 = jnp.bfloat16
MESH_AXES = ("x", "y", "c")
N_DEV = 8
MESH = pl.DeviceIdType.MESH

DEPTH = 2
ALPHA = (2.0 * DEPTH) ** 0.25
LN_EPS = 1e-5
RMS_EPS = 1e-6
L2_EPS = 1e-6
HEAD_DIM = 128
CONV_WIDTH = 4
POOL_WINDOWS = (2, 4, 8, 16)
POOL_HALO = 16
CONV_HALO = 8
LANES = 128
ADAM_LR = 0.001
ADAM_B1 = 0.9
ADAM_B2 = 0.999
ADAM_EPS = 1e-08
ADAM_WD = 0.01
ADAM_STEP = 10

VMEM_LIMIT = 56 * 1024 * 1024
ROW_TILE = 512
CONV_TILE = 256
CHUNK = 128
MM_TM, MM_TN, MM_TK = 512, 1024, 1024
DW_TILES = dict(tm=512, tn=512, tk=8192, b_outer=True)

_DIMS = {
    "nn": (((1,), (0,)), ((), ())),
    "nt": (((1,), (1,)), ((), ())),
    "tn": (((0,), (0,)), ((), ())),
}


def _params(n_axes):
    return pltpu.CompilerParams(dimension_semantics=("arbitrary",) * n_axes, vmem_limit_bytes=VMEM_LIMIT)


def _fit(tile, n):
    tile = min(tile, n)
    while n % tile:
        tile //= 2
    return tile


def _mm(name, a, b, mode, out_dtypes, epi=None, extras=(), a_fn=None, tm=None, tn=None, tk=None, b_outer=False, after=(), out_blocks=1, accs=0):
    if mode == "tn":
        K, M = a.shape
    else:
        M, K = a.shape
    N = b.shape[0] if mode == "nt" else b.shape[1]
    tm, tn, tk = _fit(tm or MM_TM, M), (N // out_blocks if out_blocks > 1 else N if accs else _fit(tn or MM_TN, N)), _fit(tk or MM_TK, K)
    nk = K // tk

    def at(f):
        return (lambda j, i, k: f(i, j, k)) if b_outer else f

    a_spec = pl.BlockSpec((tk, tm), at(lambda i, j, k: (k, i))) if mode == "tn" else pl.BlockSpec((tm, tk), at(lambda i, j, k: (i, k)))
    b_spec = pl.BlockSpec((tn, tk), at(lambda i, j, k: (j, k))) if mode == "nt" else pl.BlockSpec((tk, tn), at(lambda i, j, k: (k, j)))
    ex_spec = {"tile": pl.BlockSpec((tm, tn), at(lambda i, j, k: (i, j))), "row": pl.BlockSpec((1, tn), at(lambda i, j, k: (0, j))),
               "rows": pl.BlockSpec((tm, LANES), at(lambda i, j, k: (i, 0)))}
    ex_specs = [pl.BlockSpec(e.shape, lambda i, j, k: (0, 0)) if kind == "whole" else ex_spec[kind] for e, kind in extras]
    assert accs == 0 or (tn == N and nk == 1 and not b_outer), name
    n_ex, n_out, n_after = len(extras), len(out_dtypes), len(after)

    def body(*refs):
        a_ref, b_ref = refs[0], refs[1]
        ex_refs = refs[2 : 2 + n_ex]
        out_refs = refs[2 + n_ex + n_after : 2 + n_ex + n_after + n_out]
        av = a_ref[...]
        if a_fn is not None:
            av = a_fn(av)
        part = lax.dot_general(av.astype(BF16), b_ref[...].astype(BF16), _DIMS[mode], preferred_element_type=F32)

        def finish(res):
            vals = epi(res, *[e[...] for e in ex_refs]) if epi is not None else (res,)
            for o_ref, v in zip(out_refs, vals[:n_out]):
                o_ref[...] = v.astype(o_ref.dtype)
            for a_ref, v in zip(refs[2 + n_ex + n_after + n_out :], vals[n_out:]):

                @pl.when(pl.program_id(0) == 0)
                def _(a_ref=a_ref, v=v):
                    a_ref[...] = v

                @pl.when(pl.program_id(0) > 0)
                def _(a_ref=a_ref, v=v):
                    a_ref[...] += v

        if nk == 1:
            finish(part)
        else:
            acc = refs[-1]
            k = pl.program_id(2)

            @pl.when(k == 0)
            def _():
                acc[...] = part

            @pl.when(k > 0)
            def _():
                acc[...] += part

            @pl.when(k == nk - 1)
            def _():
                finish(acc[...])

    outs = pl.pallas_call(
        body,
        name=name,
        grid=(N // tn, M // tm, nk) if b_outer else (M // tm, N // tn, nk),
        in_specs=[a_spec, b_spec] + ex_specs + [pl.BlockSpec(memory_space=pl.ANY)] * n_after,
        out_specs=[pl.BlockSpec((tm, LANES), at(lambda i, j, k: (i, 0))) if isinstance(dt, tuple)
                   else pl.BlockSpec((tm, tn), at(lambda i, j, k: (i, j))) if out_blocks == 1
                   else pl.BlockSpec((None, tm, tn), at(lambda i, j, k: (j, i, 0))) for dt in out_dtypes]
        + [pl.BlockSpec((1, N), lambda i, j, k: (0, 0))] * accs,
        out_shape=[jax.ShapeDtypeStruct((M, LANES), dt[0]) if isinstance(dt, tuple)
                   else jax.ShapeDtypeStruct((M, N) if out_blocks == 1 else (out_blocks, M, tn), dt) for dt in out_dtypes]
        + [jax.ShapeDtypeStruct((1, N), F32)] * accs,
        scratch_shapes=[pltpu.VMEM((tm, tn), F32)] if nk > 1 else [],
        compiler_params=_params(3),
    )(a, b, *[e for e, _ in extras], *after)
    return outs[0] if n_out + accs == 1 else outs


def _rowwise(name, fn, S, ts, rows=(), halos=(), consts=(), outs=(), accs=()):
    ts = min(ts, S)
    assert S % ts == 0
    n = S // ts
    in_specs = [pl.BlockSpec((ts, a.shape[1]), lambda i: (i, 0)) for a in rows]
    for a, kind, hr in halos:
        r, nb = ts // hr, S // hr
        if kind == "prev":
            in_specs.append(pl.BlockSpec((hr, a.shape[1]), lambda i, r=r: (jnp.maximum(i * r - 1, 0), 0)))
        else:
            in_specs.append(pl.BlockSpec((hr, a.shape[1]), lambda i, r=r, nb=nb: (jnp.minimum((i + 1) * r, nb - 1), 0)))
    in_specs += [pl.BlockSpec(a.shape, lambda i, nd=a.ndim: (0,) * nd) for a in consts]
    out_specs = [pl.BlockSpec((ts, w), lambda i: (i, 0)) for w, _ in outs]
    out_specs += [pl.BlockSpec((r, w), lambda i: (0, 0)) for r, w in accs]
    out_shape = [jax.ShapeDtypeStruct((S, w), dt) for w, dt in outs]
    out_shape += [jax.ShapeDtypeStruct((r, w), F32) for r, w in accs]
    nr, nh, nc, no = len(rows), len(halos), len(consts), len(outs)

    def body(*refs):
        i = pl.program_id(0)
        rv = [r[...] for r in refs[:nr]]
        hv = [r[...] for r in refs[nr : nr + nh]]
        cv = [r[...] for r in refs[nr + nh : nr + nh + nc]]
        o_refs = refs[nr + nh + nc : nr + nh + nc + no]
        a_refs = refs[nr + nh + nc + no :]
        ovals, avals = fn(i, n, rv, hv, cv)
        for o_ref, v in zip(o_refs, ovals):
            o_ref[...] = v.astype(o_ref.dtype)
        for a_ref, v in zip(a_refs, avals):

            @pl.when(i == 0)
            def _(a_ref=a_ref, v=v):
                a_ref[...] = v

            @pl.when(i > 0)
            def _(a_ref=a_ref, v=v):
                a_ref[...] += v

    res = pl.pallas_call(
        body,
        name=name,
        grid=(n,),
        in_specs=in_specs,
        out_specs=out_specs,
        out_shape=out_shape,
        compiler_params=_params(1),
    )(*rows, *[h[0] for h in halos], *consts)
    return list(res)


def _ln(h, g, b):
    mu = jnp.mean(h, axis=-1, keepdims=True)
    d = h - mu
    var = jnp.mean(d * d, axis=-1, keepdims=True)
    rstd = lax.rsqrt(var + LN_EPS)
    xhat = d * rstd
    return xhat, rstd, xhat * g + b


def _ln_bwd(dy, xhat, rstd, g):
    dxh = dy * g
    m1 = jnp.mean(dxh, axis=-1, keepdims=True)
    m2 = jnp.mean(dxh * xhat, axis=-1, keepdims=True)
    dh = rstd * (dxh - m1 - xhat * m2)
    return dh, jnp.sum(dy * xhat, axis=0, keepdims=True), jnp.sum(dy, axis=0, keepdims=True)


def _wide(col, ts):
    return jnp.broadcast_to(col, (ts, LANES))


def _pool_fwd(x, wp, pb, ps, g, b):
    S, D = x.shape
    gw = D // len(POOL_WINDOWS)
    ts = min(ROW_TILE, S)

    def fn(i, n, rv, hv, cv):
        (xc,), (xp,) = rv, hv
        wpv, pbv, psv, gv, bv = cv
        xp = jnp.where(i > 0, xp, 0.0)
        xx = jnp.concatenate([xp, xc], axis=0)
        t = i * ts + lax.broadcasted_iota(jnp.int32, (ts, 1), 0)
        pooled, ys = [], []
        for gi, w in enumerate(POOL_WINDOWS):
            s = xx[:, gi * gw : (gi + 1) * gw]
            k = 1
            while k < w:
                s = s + pltpu.roll(s, k, axis=0)
                k *= 2
            cnt = jnp.minimum(t + 1, w).astype(F32)
            pg = (s[POOL_HALO:, :] / cnt - xc[:, gi * gw : (gi + 1) * gw]).astype(BF16)
            pooled.append(pg)
            ys.append(jnp.dot(pg, wpv[gi], preferred_element_type=F32))
        y = jnp.concatenate(ys, axis=1)
        h = ALPHA * xc + (y + pbv) * psv
        xhat, rstd, xa = _ln(h, gv, bv)
        return (jnp.concatenate(pooled, axis=1), xhat, _wide(rstd, ts), xa), ()

    return _rowwise(
        "pool_fwd", fn, S, ts, rows=[x], halos=[(x, "prev", POOL_HALO)], consts=[wp, pb, ps, g, b],
        outs=[(D, BF16), (D, F32), (LANES, F32), (D, BF16)],
    )


def _pool_bwd(dh, pooled, wp, pb, ps):
    S, D = dh.shape
    gw = D // len(POOL_WINDOWS)
    ts = min(ROW_TILE, S)
    te = ts + POOL_HALO

    def fn(i, n, rv, hv, cv):
        (dhc, pc), (dhn,) = rv, hv
        wpv, pbv, psv = cv
        dhn = jnp.where(i < n - 1, dhn, 0.0)
        dy_ext = jnp.concatenate([dhc, dhn], axis=0) * psv
        dyb = dy_ext.astype(BF16)
        t = i * ts + lax.broadcasted_iota(jnp.int32, (te, 1), 0)
        dxs, ys = [], []
        for gi, w in enumerate(POOL_WINDOWS):
            sl = slice(gi * gw, (gi + 1) * gw)
            dp = lax.dot_general(dyb[:, sl], wpv[gi], _DIMS["nt"], preferred_element_type=F32)
            s = dp / jnp.minimum(t + 1, w).astype(F32)
            k = 1
            while k < w:
                s = s + pltpu.roll(s, k, axis=0)
                k *= 2
            s = pltpu.roll(s, POOL_HALO - (w - 1), axis=0)
            dxs.append(s[POOL_HALO:, :] - dp[:ts, :])
            ys.append(jnp.dot(pc[:, sl], wpv[gi], preferred_element_type=F32))
        dx = ALPHA * dhc + jnp.concatenate(dxs, axis=1)
        y = jnp.concatenate(ys, axis=1) + pbv
        dscale = jnp.sum(dhc * y, axis=0, keepdims=True)
        dbias = jnp.sum(dy_ext[:ts, :], axis=0, keepdims=True)
        return (dx, dyb[:ts, :]), (dscale, dbias)

    return _rowwise(
        "pool_bwd", fn, S, ts, rows=[dh, pooled], halos=[(dh, "next", POOL_HALO)], consts=[wp, pb, ps],
        outs=[(D, F32), (D, BF16)], accs=[(1, D), (1, D)],
    )


def _pool_dw(pooled, dy):
    S, D = pooled.shape
    G = len(POOL_WINDOWS)
    gw = D // G
    tk = min(MM_TK, S)
    nk = S // tk

    def body(p_ref, d_ref, o_ref):
        k = pl.program_id(1)
        part = lax.dot_general(p_ref[...], d_ref[...], _DIMS["tn"], preferred_element_type=F32)

        @pl.when(k == 0)
        def _():
            o_ref[...] = part

        @pl.when(k > 0)
        def _():
            o_ref[...] += part

    return pl.pallas_call(
        body,
        name="pool_dw",
        grid=(G, nk),
        in_specs=[pl.BlockSpec((tk, gw), lambda g, k: (k, g)), pl.BlockSpec((tk, gw), lambda g, k: (k, g))],
        out_specs=pl.BlockSpec((None, gw, gw), lambda g, k: (g, 0, 0)),
        out_shape=jax.ShapeDtypeStruct((G, gw, gw), F32),
        compiler_params=_params(2),
    )(pooled, dy)


def _res_ln_epi(acc, xh, gp_, bp_, g, b):
    xhat, rstd, xo = _ln(ALPHA * (xh * gp_ + bp_) + acc, g, b)
    return xhat, _wide(rstd, acc.shape[0]), xo


def _res_ln_ffpe_epi(acc, xh, gate, pp, gp_, bp_, g, b):
    return _res_ln_epi(acc + jax.nn.sigmoid(gate) * pp, xh, gp_, bp_, g, b)


def _final_ln_loss_epi(acc, xh, gate, pp, tgt, gp_, bp_, g, b):
    sg = jax.nn.sigmoid(gate)
    xhat, rstd, y = _ln(ALPHA * (xh * gp_ + bp_) + acc + sg * pp, g, b)
    e = y - tgt
    dh, dg, db = _ln_bwd(e * (1.0 / acc.shape[1]), xhat, rstd, g)
    dgt, dpp, dbg = _ple_grads(dh, sg, pp)
    return dh, dh, dgt, dpp, jnp.sum(e * e, axis=0, keepdims=True), dg, db, dbg


def _ln_bwd_epi(acc, rest, xhat, rstd, g):
    dh, dg, db = _ln_bwd(acc + rest, xhat, rstd[:, :1], g)
    return dh, dh, dg, db


def _ple_grads(dh, sg, pp):
    dgt = dh * pp * sg * (1.0 - sg)
    return dgt, dh * sg, jnp.sum(dgt, axis=0, keepdims=True)


def _ln_bwd_ple_epi(acc, rest, xhat, rstd, gate, pp, g):
    dh, dg, db = _ln_bwd(acc + rest, xhat, rstd[:, :1], g)
    dgt, dpp, dbg = _ple_grads(dh, jax.nn.sigmoid(gate), pp)
    return dh, dh, dgt, dpp, dg, db, dbg


def _silu(c):
    return c * jax.nn.sigmoid(c)


def _qkv_point(c, is_qk, scale):
    s = _silu(c)
    nrm = s * lax.rsqrt(jnp.sum(s * s, axis=-1, keepdims=True) + L2_EPS) * scale
    return jnp.where(is_qk, nrm, s)


def _conv_rows(xx, wv, lo, rows):
    acc = None
    for j in range(CONV_WIDTH):
        sh = CONV_WIDTH - 1 - j
        term = (pltpu.roll(xx, sh, axis=0) if sh else xx)[lo : lo + rows, :] * wv[j : j + 1, :]
        acc = term if acc is None else acc + term
    return acc


def _conv_fwd(qkv_pre, conv_w):
    S, W = qkv_pre.shape
    D = W // 3
    H = D // HEAD_DIM
    ts = min(CONV_TILE, S)
    r = ts // CONV_HALO

    def body(x_ref, xp_ref, w_ref, o_ref):
        j, i = pl.program_id(0), pl.program_id(1)
        xp = jnp.where(i > 0, xp_ref[...], 0.0)
        xx = jnp.concatenate([xp, x_ref[...]], axis=0)
        c = _conv_rows(xx, w_ref[...], CONV_HALO, ts)
        scale = jnp.where(j == 0, HEAD_DIM**-0.5, 1.0).astype(F32)
        for h in range(H):
            sl = slice(h * HEAD_DIM, (h + 1) * HEAD_DIM)
            o_ref[:, sl] = _qkv_point(c[:, sl], j < 2, scale)

    return pl.pallas_call(
        body,
        name="gdn_conv_fwd",
        grid=(3, S // ts),
        in_specs=[
            pl.BlockSpec((ts, D), lambda j, i: (i, j)),
            pl.BlockSpec((CONV_HALO, D), lambda j, i: (jnp.maximum(i * r - 1, 0), j)),
            pl.BlockSpec((CONV_WIDTH, D), lambda j, i: (0, j)),
        ],
        out_specs=pl.BlockSpec((ts, D), lambda j, i: (i, j)),
        out_shape=jax.ShapeDtypeStruct((S, W), F32),
        compiler_params=_params(2),
    )(qkv_pre, qkv_pre, conv_w)


def _conv_bwd(qkv_pre, conv_w, dqkvn):
    S, W = qkv_pre.shape
    D = W // 3
    H = D // HEAD_DIM
    ts = min(CONV_TILE, S)
    r, nb = ts // CONV_HALO, S // CONV_HALO
    te = ts + CONV_HALO

    def body(x_ref, xp_ref, xn_ref, w_ref, d_ref, dn_ref, dx_ref, dw_ref):
        j, i = pl.program_id(0), pl.program_id(1)
        n = pl.num_programs(1)
        wv = w_ref[...]
        xp = jnp.where(i > 0, xp_ref[...], 0.0)
        xx = jnp.concatenate([xp, x_ref[...], xn_ref[...]], axis=0)
        xr = [pltpu.roll(xx, sh, axis=0) if sh else xx for sh in range(CONV_WIDTH)]
        c = None
        for jj in range(CONV_WIDTH):
            term = xr[CONV_WIDTH - 1 - jj][CONV_HALO : CONV_HALO + te, :] * wv[jj : jj + 1, :]
            c = term if c is None else c + term
        dn = jnp.where(i < n - 1, dn_ref[...], 0.0)
        dout = jnp.concatenate([d_ref[...], dn], axis=0)
        scale = jnp.where(j == 0, HEAD_DIM**-0.5, 1.0).astype(F32)
        dcs = []
        for h in range(H):
            sl = slice(h * HEAD_DIM, (h + 1) * HEAD_DIM)
            _, vjp = jax.vjp(lambda cc: _qkv_point(cc, j < 2, scale), c[:, sl])
            dcs.append(vjp(dout[:, sl])[0])
        dc = jnp.concatenate(dcs, axis=1)
        dx = None
        dws = []
        for jj in range(CONV_WIDTH):
            sh = CONV_WIDTH - 1 - jj
            term = pltpu.roll(dc, CONV_HALO - sh, axis=0)[CONV_HALO:, :] * wv[jj : jj + 1, :]
            dx = term if dx is None else dx + term
            dws.append(jnp.sum(dc[:ts, :] * xr[sh][CONV_HALO : CONV_HALO + ts, :], axis=0, keepdims=True))
        dx_ref[...] = dx.astype(dx_ref.dtype)
        dw = jnp.concatenate(dws, axis=0)

        @pl.when(i == 0)
        def _():
            dw_ref[...] = dw

        @pl.when(i > 0)
        def _():
            dw_ref[...] += dw

    return pl.pallas_call(
        body,
        name="gdn_conv_bwd",
        grid=(3, S // ts),
        in_specs=[
            pl.BlockSpec((ts, D), lambda j, i: (i, j)),
            pl.BlockSpec((CONV_HALO, D), lambda j, i: (jnp.maximum(i * r - 1, 0), j)),
            pl.BlockSpec((CONV_HALO, D), lambda j, i: (jnp.minimum((i + 1) * r, nb - 1), j)),
            pl.BlockSpec((CONV_WIDTH, D), lambda j, i: (0, j)),
            pl.BlockSpec((ts, D), lambda j, i: (i, j)),
            pl.BlockSpec((CONV_HALO, D), lambda j, i: (jnp.minimum((i + 1) * r, nb - 1), j)),
        ],
        out_specs=[pl.BlockSpec((ts, D), lambda j, i: (i, j)), pl.BlockSpec((CONV_WIDTH, D), lambda j, i: (0, j))],
        out_shape=[jax.ShapeDtypeStruct((S, W), BF16), jax.ShapeDtypeStruct((CONV_WIDTH, W), F32)],
        compiler_params=_params(2),
    )(qkv_pre, qkv_pre, qkv_pre, conv_w, dqkvn, dqkvn)


def _softplus(x):
    pos = x > 0.0
    return jnp.where(pos, x, 0.0) + jnp.log(1.0 + jnp.exp(jnp.where(pos, -x, x)))


def _gates(bl, al, alog, dt):
    return jax.nn.sigmoid(bl), -jnp.exp(alog) * _softplus(al + dt)


def _gates_fwd(ba, alog, dt):
    S = ba.shape[0]
    ts = min(ROW_TILE, S)

    def fn(i, n, rv, hv, cv):
        return _gates(rv[0][:, :LANES], rv[0][:, LANES:], cv[0], cv[1]), ()

    return _rowwise("gdn_gates_fwd", fn, S, ts, rows=[ba], consts=[alog, dt], outs=[(LANES, F32), (LANES, F32)])


def _gates_bwd(ba, alog, dt, dbeta, dg, H):
    S = ba.shape[0]
    ts = min(ROW_TILE, S)

    def fn(i, n, rv, hv, cv):
        bav, dbv, dgv = rv
        real = lax.broadcasted_iota(jnp.int32, (1, LANES), 1) < H
        _, vjp = jax.vjp(_gates, bav[:, :LANES], bav[:, LANES:], cv[0], cv[1])
        dbl, dal, dalog, ddt = vjp((jnp.where(real, dbv, 0.0), jnp.where(real, dgv, 0.0)))
        dbl, dal = jnp.where(real, dbl, 0.0), jnp.where(real, dal, 0.0)
        return (jnp.concatenate([dbl, dal], axis=1),), (jnp.where(real, dalog, 0.0), jnp.where(real, ddt, 0.0))

    return _rowwise(
        "gdn_gates_bwd", fn, S, ts, rows=[ba, dbeta, dg], consts=[alog, dt], outs=[(2 * LANES, BF16)],
        accs=[(1, LANES), (1, LANES)],
    )


def _split_bf16(a, n):
    parts, rest = [], a
    for _ in range(n):
        piece = rest.astype(BF16)
        parts.append(piece)
        rest = rest - piece.astype(F32)
    return parts


def _tri_dot(a, b, mode, tri):
    d = lambda u, v: lax.dot_general(u, v, _DIMS[mode], preferred_element_type=F32)
    if tri == 0:
        return sum(d(a.astype(BF16), piece) for piece in _split_bf16(b, 3))
    return sum(d(piece, b.astype(BF16)) for piece in _split_bf16(a, 3))


def _bdot_raw(a, b, mode):
    return lax.dot_general(a.astype(BF16), b.astype(BF16), _DIMS[mode], preferred_element_type=F32)


@functools.partial(jax.custom_vjp, nondiff_argnums=(2,))
def _bdot(a, b, mode):
    return _bdot_raw(a, b, mode)


def _bdot_fwd(a, b, mode):
    return _bdot_raw(a, b, mode), (a, b)


def _bdot_bwd(mode, res, ct):
    a, b = res
    if mode == "nn":
        return _bdot(ct, b, "nt"), _bdot(a, ct, "tn")
    if mode == "nt":
        return _bdot(ct, b, "nn"), _bdot(ct, a, "tn")
    return _bdot(b, ct, "nt"), _bdot(a, ct, "nn")


_bdot.defvjp(_bdot_fwd, _bdot_bwd)


@jax.custom_vjp
def _unit_lower_inverse(a_strict):
    return _unit_lower_inverse_raw(a_strict)


def _unit_lower_inverse_fwd(a_strict):
    t = _unit_lower_inverse_raw(a_strict)
    return t, t


def _unit_lower_inverse_bwd(t, ct):
    left = [_bdot(ti, ci, "tn") for ti, ci in zip(t, ct)]
    return (tuple(-_bdot(li, ti, "nt") for li, ti in zip(left, t)),)


_unit_lower_inverse.defvjp(_unit_lower_inverse_fwd, _unit_lower_inverse_bwd)


@jax.custom_vjp
def _saved_inverse(a_strict, t):
    return t


def _saved_inverse_fwd(a_strict, t):
    return t, t


def _saved_inverse_bwd(t, ct):
    return _unit_lower_inverse_bwd(t, ct) + (tuple(jnp.zeros_like(ti) for ti in t),)


_saved_inverse.defvjp(_saved_inverse_fwd, _saved_inverse_bwd)


def _unit_lower_inverse_raw(a_strict):
    C = a_strict[0].shape[0]
    ii = lax.broadcasted_iota(jnp.int32, (C, C), 0)
    jj = lax.broadcasted_iota(jnp.int32, (C, C), 1)
    eye = (ii == jj).astype(F32)
    blk = 16
    same = (ii // blk) == (jj // blk)
    p = [-jnp.where(same, a, 0.0) for a in a_strict]
    t = [eye + x for x in p]
    for _ in range(3):
        p = [_bdot(x, x, "nn") for x in p]
        t = [ti + _bdot(ti, x, "nn") for ti, x in zip(t, p)]
    while blk < C:
        same2 = (ii // (2 * blk)) == (jj // (2 * blk))
        off = jnp.logical_and(same2, jnp.logical_not(same))
        te = [_bdot(ti, jnp.where(off, a, 0.0), "nn") for ti, a in zip(t, a_strict)]
        t = [ti - _bdot(x, ti, "nn") for ti, x in zip(t, te)]
        same, blk = same2, 2 * blk
    return tuple(t)


def _chunk_heads(q, k, v, gc_col, gc_row, b_col, s0, t_saved=None, with_t=False):
    R = range(len(q))
    C = q[0].shape[0]
    ii = lax.broadcasted_iota(jnp.int32, (C, C), 0)
    jj = lax.broadcasted_iota(jnp.int32, (C, C), 1)
    rows = lax.broadcasted_iota(jnp.int32, (C, 1), 0)
    decay = [jnp.where(ii >= jj, jnp.exp(jnp.minimum(gc_col[h] - gc_row[h], 0.0)), 0.0) for h in R]
    kb = [k[h] * b_col[h] for h in R]
    a = [_bdot(kb[h], k[h], "nt") * decay[h] for h in R]
    qk = [_bdot(q[h], k[h], "nt") * decay[h] for h in R]
    a_strict = tuple(jnp.where(ii > jj, a[h], 0.0) for h in R)
    t = _unit_lower_inverse(a_strict) if t_saved is None else _saved_inverse(a_strict, t_saved)
    eg = [jnp.exp(gc_col[h]) for h in R]
    u = [_bdot(t[h], v[h] * b_col[h], "nn") for h in R]
    w = [_bdot(t[h], kb[h] * eg[h], "nn") for h in R]
    g_last = [jnp.sum(jnp.where(rows == C - 1, gc_col[h], 0.0), axis=0, keepdims=True) for h in R]
    kd = [k[h] * jnp.exp(g_last[h] - gc_col[h]) for h in R]
    ws = [_bdot(w[h], s0[h], "nn") for h in R]
    qs = [_bdot(q[h] * eg[h], s0[h], "nn") for h in R]
    v_new = [u[h] - ws[h] for h in R]
    o = [qs[h] + _bdot(qk[h], v_new[h], "nn") for h in R]
    s1 = [s0[h] * jnp.exp(g_last[h]) + _bdot(kd[h], v_new[h], "tn") for h in R]
    return (tuple(o), tuple(s1), t) if with_t else (tuple(o), tuple(s1))


def _pick_lane(a, h):
    lanes = lax.broadcasted_iota(jnp.int32, a.shape, 1)
    return jnp.sum(jnp.where(lanes == h, a, 0.0), axis=1, keepdims=True)


def _pick_row(a, h):
    rows = lax.broadcasted_iota(jnp.int32, a.shape, 0)
    return jnp.sum(jnp.where(rows == h, a, 0.0), axis=0, keepdims=True)


def _tri(C):
    ii = lax.broadcasted_iota(jnp.int32, (C, C), 0)
    jj = lax.broadcasted_iota(jnp.int32, (C, C), 1)
    return (ii >= jj).astype(F32)


def _delta_fwd(qkvn, g_pad, g_rows, beta_pad):
    S, W = qkvn.shape
    D = W // 3
    H = D // HEAD_DIM
    C = min(CHUNK, S)
    N = S // C

    def body(x_ref, gp_ref, gr_ref, bp_ref, o_ref, sall_ref, tall_ref, st):
        n = pl.program_id(0)

        @pl.when(n == 0)
        def _():
            st[...] = jnp.zeros_like(st)

        low = _tri(C)
        gc_cols = _tri_dot(low, gp_ref[...], "nn", 0)
        gc_rows = _tri_dot(gr_ref[...], low, "nt", 1)
        bcols = bp_ref[...]
        hs = range(H)
        s0 = tuple(st[h] for h in hs)
        for h in hs:
            sall_ref[h] = s0[h]
        o, s1, t = _chunk_heads(
            tuple(x_ref[:, h * HEAD_DIM : (h + 1) * HEAD_DIM] for h in hs),
            tuple(x_ref[:, D + h * HEAD_DIM : D + (h + 1) * HEAD_DIM] for h in hs),
            tuple(x_ref[:, 2 * D + h * HEAD_DIM : 2 * D + (h + 1) * HEAD_DIM] for h in hs),
            tuple(_pick_lane(gc_cols, h) for h in hs), tuple(_pick_row(gc_rows, h) for h in hs),
            tuple(_pick_lane(bcols, h) for h in hs), s0, with_t=True,
        )
        for h in hs:
            st[h] = s1[h]
            o_ref[:, h * HEAD_DIM : (h + 1) * HEAD_DIM] = o[h]
            tall_ref[h] = t[h].astype(tall_ref.dtype)

    return pl.pallas_call(
        body,
        name="gdn_delta_fwd",
        grid=(N,),
        in_specs=[
            pl.BlockSpec((C, W), lambda n: (n, 0)),
            pl.BlockSpec((C, LANES), lambda n: (n, 0)),
            pl.BlockSpec((None, 8, C), lambda n: (n, 0, 0)),
            pl.BlockSpec((C, LANES), lambda n: (n, 0)),
        ],
        out_specs=[pl.BlockSpec((C, D), lambda n: (n, 0)), pl.BlockSpec((None, H, HEAD_DIM, HEAD_DIM), lambda n: (n, 0, 0, 0)),
                   pl.BlockSpec((None, H, C, C), lambda n: (n, 0, 0, 0))],
        out_shape=[jax.ShapeDtypeStruct((S, D), F32), jax.ShapeDtypeStruct((N, H, HEAD_DIM, HEAD_DIM), F32), jax.ShapeDtypeStruct((N, H, C, C), BF16)],
        scratch_shapes=[pltpu.VMEM((H, HEAD_DIM, HEAD_DIM), F32)],
        compiler_params=_params(1),
    )(qkvn, g_pad, g_rows, beta_pad)


def _delta_bwd(qkvn, g_pad, g_rows, beta_pad, s_all, t_all, do):
    S, W = qkvn.shape
    D = W // 3
    H = D // HEAD_DIM
    C = min(CHUNK, S)
    N = S // C

    def body(x_ref, gp_ref, gr_ref, bp_ref, sall_ref, tall_ref, do_ref, dx_ref, dgp_ref, dgr_ref, dbp_ref, dst):
        n = pl.program_id(0)

        @pl.when(n == 0)
        def _():
            dst[...] = jnp.zeros_like(dst)

        low = _tri(C)
        gc_cols = _tri_dot(low, gp_ref[...], "nn", 0)
        gc_rows = _tri_dot(gr_ref[...], low, "nt", 1)
        bcols = bp_ref[...]
        lane = lax.broadcasted_iota(jnp.int32, (1, LANES), 1)
        row8 = lax.broadcasted_iota(jnp.int32, (8, 1), 0)
        dgc_cols = jnp.zeros((C, LANES), F32)
        dgc_rows = jnp.zeros((8, C), F32)
        dbcols = jnp.zeros((C, LANES), F32)
        hs = range(H)
        t_saved = tuple(tall_ref[h].astype(F32) for h in hs)
        _, vjp = jax.vjp(
            lambda *args: _chunk_heads(*args, t_saved=t_saved),
            tuple(x_ref[:, h * HEAD_DIM : (h + 1) * HEAD_DIM] for h in hs),
            tuple(x_ref[:, D + h * HEAD_DIM : D + (h + 1) * HEAD_DIM] for h in hs),
            tuple(x_ref[:, 2 * D + h * HEAD_DIM : 2 * D + (h + 1) * HEAD_DIM] for h in hs),
            tuple(_pick_lane(gc_cols, h) for h in hs), tuple(_pick_row(gc_rows, h) for h in hs),
            tuple(_pick_lane(bcols, h) for h in hs), tuple(sall_ref[h] for h in hs),
        )
        dq, dk, dv, dgc, dgr, dbc, ds0 = vjp((tuple(do_ref[:, h * HEAD_DIM : (h + 1) * HEAD_DIM] for h in hs), tuple(dst[h] for h in hs)))
        for h in hs:
            dst[h] = ds0[h]
            dx_ref[:, h * HEAD_DIM : (h + 1) * HEAD_DIM] = dq[h]
            dx_ref[:, D + h * HEAD_DIM : D + (h + 1) * HEAD_DIM] = dk[h]
            dx_ref[:, 2 * D + h * HEAD_DIM : 2 * D + (h + 1) * HEAD_DIM] = dv[h]
            dgc_cols = dgc_cols + dgc[h] * (lane == h).astype(F32)
            dgc_rows = dgc_rows + dgr[h] * (row8 == h).astype(F32)
            dbcols = dbcols + dbc[h] * (lane == h).astype(F32)
        dgp_ref[...] = _tri_dot(low, dgc_cols, "tn", 0)
        dgr_ref[...] = _tri_dot(dgc_rows, low, "nn", 1)
        dbp_ref[...] = dbcols

    rev = lambda n: N - 1 - n
    return pl.pallas_call(
        body,
        name="gdn_delta_bwd",
        grid=(N,),
        in_specs=[
            pl.BlockSpec((C, W), lambda n: (rev(n), 0)),
            pl.BlockSpec((C, LANES), lambda n: (rev(n), 0)),
            pl.BlockSpec((None, 8, C), lambda n: (rev(n), 0, 0)),
            pl.BlockSpec((C, LANES), lambda n: (rev(n), 0)),
            pl.BlockSpec((None, H, HEAD_DIM, HEAD_DIM), lambda n: (rev(n), 0, 0, 0)),
            pl.BlockSpec((None, H, C, C), lambda n: (rev(n), 0, 0, 0)),
            pl.BlockSpec((C, D), lambda n: (rev(n), 0)),
        ],
        out_specs=[
            pl.BlockSpec((C, W), lambda n: (rev(n), 0)),
            pl.BlockSpec((C, LANES), lambda n: (rev(n), 0)),
            pl.BlockSpec((None, 8, C), lambda n: (rev(n), 0, 0)),
            pl.BlockSpec((C, LANES), lambda n: (rev(n), 0)),
        ],
        out_shape=[
            jax.ShapeDtypeStruct((S, W), F32),
            jax.ShapeDtypeStruct((S, LANES), F32),
            jax.ShapeDtypeStruct((N, 8, C), F32),
            jax.ShapeDtypeStruct((S, LANES), F32),
        ],
        scratch_shapes=[pltpu.VMEM((H, HEAD_DIM, HEAD_DIM), F32)],
        compiler_params=_params(1),
    )(qkvn, g_pad, g_rows, beta_pad, s_all, t_all, do)


def _gate_norm_head(o, z, nw):
    return o * lax.rsqrt(jnp.mean(o * o, axis=-1, keepdims=True) + RMS_EPS) * nw * _silu(z)


def _gate_norm_fwd(o, z, nw):
    S, D = o.shape
    H = D // HEAD_DIM
    ts = min(ROW_TILE, S)

    def fn(i, n, rv, hv, cv):
        ov, zv = rv
        parts = [_gate_norm_head(ov[:, h * HEAD_DIM : (h + 1) * HEAD_DIM], zv[:, h * HEAD_DIM : (h + 1) * HEAD_DIM], cv[0]) for h in range(H)]
        return (jnp.concatenate(parts, axis=1),), ()

    return _rowwise("gdn_gate_norm_fwd", fn, S, ts, rows=[o, z], consts=[nw], outs=[(D, BF16)])[0]


def _gate_norm_bwd_epi(dog, o, z, nw):
    D = dog.shape[1]
    dos, dzs, dnw = [], [], None
    for h in range(D // HEAD_DIM):
        sl = slice(h * HEAD_DIM, (h + 1) * HEAD_DIM)
        _, vjp = jax.vjp(_gate_norm_head, o[:, sl], z[:, sl], nw)
        a, b_, c_ = vjp(dog[:, sl])
        dos.append(a)
        dzs.append(b_)
        dnw = c_ if dnw is None else dnw + c_
    wide = jnp.concatenate([dnw, jnp.zeros((1, D - HEAD_DIM), F32)], axis=1) if D > HEAD_DIM else dnw
    return jnp.concatenate(dos, axis=1), jnp.concatenate(dzs, axis=1), wide


def _square_bf16(r):
    return r * r


def _mlp_ple_dw(li, dhb, dgate, dpp, xa, p, r, w2):
    dpre = _mm(f"l{li}_mlp_down_bwd", dhb, w2, "nt", [BF16], epi=lambda acc, rr: (acc * (2.0 * rr.astype(F32)),), extras=[(r, "tile")], tm=1024, tn=1024, b_outer=True)
    dw2 = _mm(f"l{li}_mlp_dw2", r, dhb, "tn", [BF16], a_fn=_square_bf16, **DW_TILES)
    dw1 = _mm(f"l{li}_mlp_dw1", xa, dpre, "tn", [BF16], out_blocks=N_DEV, **DW_TILES)
    dwg = _mm(f"l{li}_ple_dwg", xa, dgate, "tn", [BF16], **DW_TILES)
    dwp = _mm(f"l{li}_ple_dwp", p, dpp, "tn", [BF16], out_blocks=N_DEV, **DW_TILES)
    rows = lambda a: a.reshape((N_DEV, a.shape[0] // N_DEV) + a.shape[1:])
    return dpre, dw1, rows(dw2), rows(dwg), dwp


def _mlp_ple_dx(li, dh, dpre, dgate, w1, wg, after, xhat, rstd, g):
    t = _mm(f"l{li}_ple_gate_bwd", dgate, wg, "nt", [F32], epi=lambda acc, d: (acc + ALPHA * d,), extras=[(dh, "tile")], tm=1024, after=after)
    return _mm(f"l{li}_mlp_up_bwd", dpre, w1, "nt", [F32, BF16], epi=_ln_bwd_epi, extras=[(t, "tile"), (xhat, "tile"), (rstd, "rows"), (g, "row")],
               tm=256, tk=4096, accs=2)


def _local_step(x, p, tgt, W, fetch, emit):
    S, D = x.shape
    H = D // HEAD_DIM
    C = min(CHUNK, S)
    N = S // C
    lg = lambda i, j: W["ln_gain"][2 * i + j][None, :]
    lb = lambda i, j: W["ln_bias"][2 * i + j][None, :]
    G = {}

    pooled, xh0a, rs0a, x0a = _pool_fwd(x, W["pool_w"], W["pool_b"], W["pool_scale"], lg(0, 0), lb(0, 0))
    w0a = fetch("l0a", x0a)
    r0 = _mm("l0_mlp_up", x0a, w0a["mlp_w1"], "nn", [BF16], epi=lambda acc: (jnp.maximum(acc, 0.0),), tm=1024, tn=1024, b_outer=True, after=w0a.get("_after", ()))
    w0b = fetch("l0b", r0)
    gate0 = _mm("l0_ple_gate", x0a, w0b["ple_gate_w"], "nn", [F32], epi=lambda acc, bias: (acc + bias,), extras=[(W["ple_gate_b"][0:1], "row")], tm=1024)
    pp0 = _mm("l0_ple_proj", p[0], w0b["ple_proj"], "nn", [F32])
    w0c = fetch("l0c", pp0)
    ln_rows = lambda i, j, i2, j2: [(lg(i, j), "row"), (lb(i, j), "row"), (lg(i2, j2), "row"), (lb(i2, j2), "row")]
    xh0b, rs0b, x0b = _mm("l0_mlp_down", r0, w0c["mlp_w2"], "nn", [F32, (F32, LANES), BF16], a_fn=_square_bf16, tm=256, tn=D, tk=4096, epi=_res_ln_ffpe_epi,
                          extras=[(xh0a, "tile"), (gate0, "tile"), (pp0, "tile")] + ln_rows(0, 0, 0, 1), after=w0c.get("_after", ()))

    wg_ = fetch("gdn", x0b)
    qkv_pre = _mm("gdn_in_qkv", x0b, wg_["gdn_wqkv"], "nn", [F32], tm=1024, tn=1024, b_outer=True, after=wg_.get("_after", ()))
    z = _mm("gdn_in_z", x0b, wg_["gdn_wz"], "nn", [F32], tm=1024)
    ba = _mm("gdn_in_ba", x0b, wg_["gdn_wba"], "nn", [F32])
    qkvn = _conv_fwd(qkv_pre, W["gdn_conv"])
    beta_pad, g_pad = _gates_fwd(ba, W["gdn_a_log"], W["gdn_dt_bias"])
    g_rows = g_pad[:, :8].reshape(N, C, 8).transpose(0, 2, 1)
    o, s_all, t_all = _delta_fwd(qkvn, g_pad, g_rows, beta_pad)
    og = _gate_norm_fwd(o, z, W["gdn_norm_w"])
    xh1a, rs1a, x1a = _mm("gdn_out", og, wg_["gdn_w_out"], "nn", [F32, (F32, LANES), BF16], tm=512, tn=D, epi=_res_ln_epi,
                          extras=[(xh0b, "tile")] + ln_rows(0, 1, 1, 0))
    w1_ = fetch("l1", x1a)
    r1 = _mm("l1_mlp_up", x1a, w1_["mlp_w1"], "nn", [BF16], epi=lambda acc: (jnp.maximum(acc, 0.0),), tm=1024, tn=1024, b_outer=True)
    gate1 = _mm("l1_ple_gate", x1a, w1_["ple_gate_w"], "nn", [F32], epi=lambda acc, bias: (acc + bias,), extras=[(W["ple_gate_b"][1:2], "row")], tm=1024)
    pp1 = _mm("l1_ple_proj", p[1], w1_["ple_proj"], "nn", [F32])
    dh1b, dh1b_b, dgate1, dpp1, loss_cols, dg11, db11, dbg_1 = _mm(
        "l1_mlp_down", r1, w1_["mlp_w2"], "nn", [F32, BF16, BF16, BF16], a_fn=_square_bf16, tm=256, tn=D, tk=4096, epi=_final_ln_loss_epi,
        extras=[(xh1a, "tile"), (gate1, "tile"), (pp1, "tile"), (tgt, "tile")] + ln_rows(1, 0, 1, 1), accs=4)

    dpre1, dw1_1, dw2_1, dwg_1, dwp_1 = _mlp_ple_dw(1, dh1b_b, dgate1, dpp1, x1a, p[1], r1, w1_["mlp_w2"])
    tok = emit("l1", {"mlp_w1": dw1_1, "mlp_w2": dw2_1, "ple_gate_w": dwg_1, "ple_proj": dwp_1})
    dh1a, dh1a_b, dg10, db10 = _mlp_ple_dx(1, dh1b, dpre1, dgate1, w1_["mlp_w1"], w1_["ple_gate_w"], [tok], xh1a, rs1a, lg(1, 0))
    do, dz, dnw = _mm("gdn_out_bwd", dh1a_b, wg_["gdn_w_out"], "nt", [F32, BF16], tm=256, tn=D, tk=D, epi=_gate_norm_bwd_epi,
                      extras=[(o, "tile"), (z, "tile"), (W["gdn_norm_w"], "whole")], accs=1)
    dnw = dnw[:, :HEAD_DIM]
    dw_out = _mm("gdn_dw_out", og, dh1a_b, "tn", [BF16], **DW_TILES)
    dw_out = dw_out.reshape((N_DEV, dw_out.shape[0] // N_DEV) + dw_out.shape[1:])
    dqkvn, dg_col, dg_row, dbeta = _delta_bwd(qkvn, g_pad, g_rows, beta_pad, s_all, t_all, do)
    dg_all = dg_col + jnp.pad(dg_row.transpose(0, 2, 1).reshape(S, 8), ((0, 0), (0, LANES - 8)))
    dba, dalog, ddt = _gates_bwd(ba, W["gdn_a_log"], W["gdn_dt_bias"], dbeta, dg_all, H)
    dqkv, dconv = _conv_bwd(qkv_pre, W["gdn_conv"], dqkvn)
    dwqkv = _mm("gdn_dwqkv", x0b, dqkv, "tn", [F32], **DW_TILES)
    dwz = _mm("gdn_dwz", x0b, dz, "tn", [F32], **DW_TILES)
    dwba = _mm("gdn_dwba", x0b, dba, "tn", [F32], **DW_TILES)
    dw_in = jnp.concatenate([dwqkv, dwz, dwba[:, :H], dwba[:, LANES : LANES + H]], axis=1)
    tok = emit("gdn", {"gdn_w_in": _split_blocks("gdn_w_in", dw_in).astype(BF16), "gdn_w_out": dw_out})
    t = _mm("gdn_in_ba_bwd", dba, wg_["gdn_wba"], "nt", [F32], epi=lambda acc, d: (acc + ALPHA * d,), extras=[(dh1a, "tile")], after=[tok])
    t = _mm("gdn_in_z_bwd", dz, wg_["gdn_wz"], "nt", [F32], epi=lambda acc, d: (acc + d,), extras=[(t, "tile")], tm=1024)
    dh0b, dh0b_b, dgate0, dpp0, dg01, db01, dbg_0 = _mm(
        "gdn_in_qkv_bwd", dqkv, wg_["gdn_wqkv"], "nt", [F32, BF16, BF16, BF16], epi=_ln_bwd_ple_epi,
        extras=[(t, "tile"), (xh0b, "tile"), (rs0b, "rows"), (gate0, "tile"), (pp0, "tile"), (lg(0, 1), "row")], tm=256, tk=3072, accs=3)

    dpre0, dw1_0, dw2_0, dwg_0, dwp_0 = _mlp_ple_dw(0, dh0b_b, dgate0, dpp0, x0a, p[0], r0, w0c["mlp_w2"])
    tok = emit("l0", {"mlp_w1": dw1_0, "mlp_w2": dw2_0, "ple_gate_w": dwg_0, "ple_proj": dwp_0})
    dh0a, _, dg00, db00 = _mlp_ple_dx(0, dh0b, dpre0, dgate0, w0a["mlp_w1"], w0b["ple_gate_w"], [tok], xh0a, rs0a, lg(0, 0))
    grad_x, dyp, dscale, dpb = _pool_bwd(dh0a, pooled, W["pool_w"], W["pool_b"], W["pool_scale"])
    G["pool_w"] = _pool_dw(pooled, dyp)

    G["ln_gain"] = jnp.concatenate([dg00, dg01, dg10, dg11], axis=0)
    G["ln_bias"] = jnp.concatenate([db00, db01, db10, db11], axis=0)
    G["pool_b"] = dpb
    G["pool_scale"] = dscale
    G["gdn_conv"] = dconv
    G["gdn_a_log"] = dalog[:, :H]
    G["gdn_dt_bias"] = ddt[:, :H]
    G["gdn_norm_w"] = dnw
    G["ple_gate_b"] = jnp.concatenate([dbg_0, dbg_1], axis=0)
    return loss_cols, grad_x, G


_HBM = pl.BlockSpec(memory_space=pltpu.HBM)


def _all_gather(name, shards):
    T = len(shards)

    def body(*refs):
        ins, outs = refs[:T], refs[T : 2 * T]
        send_sems, recv_sems, local_sems = refs[2 * T :]
        x, y, c = lax.axis_index("x"), lax.axis_index("y"), lax.axis_index("c")
        me, sibling = (x, y, c), (x, y, 1 - c)
        chips = [(1 - x, y), (x, 1 - y), (1 - x, 1 - y)]

        def blk(t, px, py, pc):
            return outs[t].at[4 * px + 2 * py + pc]

        def copy(t, k, block, to, src=None):
            return pltpu.make_async_remote_copy(
                src_ref=blk(t, *block) if src is None else src, dst_ref=blk(t, *block),
                send_sem=send_sems.at[t, k], recv_sem=recv_sems.at[t, k], device_id=to, device_id_type=MESH,
            )

        mine = [pltpu.make_async_copy(ins[t], blk(t, *me), local_sems.at[t]) for t in range(T)]
        for cp in mine:
            cp.start()
        first = []
        for t in range(T):
            first.append(copy(t, 0, me, sibling, src=ins[t]))
            first += [copy(t, 1 + j, me, (*chip, c), src=ins[t]) for j, chip in enumerate(chips)]
        for cp in first:
            cp.start()
        passed = []
        for j, chip in enumerate(chips):
            for t in range(T):
                copy(t, 1 + j, (*chip, c), me).wait_recv()
                fw = copy(t, 4 + j, (*chip, c), sibling)
                fw.start()
                passed.append(fw)
        for t in range(T):
            copy(t, 0, sibling, me).wait_recv()
            for j, chip in enumerate(chips):
                copy(t, 4 + j, (*chip, 1 - c), me).wait_recv()
        for cp in first + passed:
            cp.wait_send()
        for cp in mine:
            cp.wait()

    return pl.pallas_call(
        body,
        name=name,
        in_specs=[_HBM] * T,
        out_specs=[_HBM] * T,
        out_shape=[jax.ShapeDtypeStruct((N_DEV,) + s.shape, s.dtype) for s in shards],
        scratch_shapes=[pltpu.SemaphoreType.DMA((T, 7)), pltpu.SemaphoreType.DMA((T, 7)), pltpu.SemaphoreType.DMA((T,))],
    )(*shards)


def _exchange(name, blocks):
    def body(g_ref, o_ref, send_sems, recv_sems, local_sem):
        x, y, c = lax.axis_index("x"), lax.axis_index("y"), lax.axis_index("c")
        own = pltpu.make_async_copy(g_ref.at[4 * x + 2 * y + c], o_ref.at[N_DEV - 1], local_sem)
        own.start()
        copies = []
        for rel in range(1, N_DEV):
            px = 1 - x if rel & 4 else x
            py = 1 - y if rel & 2 else y
            pc = 1 - c if rel & 1 else c
            copies.append(
                pltpu.make_async_remote_copy(
                    src_ref=g_ref.at[4 * px + 2 * py + pc], dst_ref=o_ref.at[rel - 1],
                    send_sem=send_sems.at[rel - 1], recv_sem=recv_sems.at[rel - 1], device_id=(px, py, pc), device_id_type=MESH,
                )
            )
        for cp in copies:
            cp.start()
        for cp in copies:
            cp.wait_recv()
        for cp in copies:
            cp.wait_send()
        own.wait()

    return pl.pallas_call(
        body,
        name=name,
        in_specs=[_HBM],
        out_specs=_HBM,
        out_shape=jax.ShapeDtypeStruct(blocks.shape, blocks.dtype),
        scratch_shapes=[pltpu.SemaphoreType.DMA((N_DEV - 1,)), pltpu.SemaphoreType.DMA((N_DEV - 1,)), pltpu.SemaphoreType.DMA],
    )(blocks)


_SEM = pl.BlockSpec(memory_space=pltpu.SEMAPHORE)
_ANY = pl.BlockSpec(memory_space=pl.ANY)
_DATAFLOW = pltpu.SideEffectType.DATAFLOW_SIDE_EFFECTING
N_PEERS = N_DEV - 1


def _peer(rel, x, y, c):
    return (1 - x if rel & 4 else x, 1 - y if rel & 2 else y, 1 - c if rel & 1 else c)


def _send_start(name, srcs, lands, gather, after):
    T = len(srcs)

    def body(*refs):
        src_refs, land_refs = refs[:T], refs[T : 2 * T]
        send_sems, recv_sems = refs[2 * T + 1], refs[2 * T + 2]
        token = refs[-1]
        x, y, c = lax.axis_index("x"), lax.axis_index("y"), lax.axis_index("c")
        for t in range(T):
            for rel in range(1, N_DEV):
                px, py, pc = _peer(rel, x, y, c)
                pltpu.make_async_remote_copy(
                    src_ref=src_refs[t] if gather else src_refs[t].at[4 * px + 2 * py + pc],
                    dst_ref=land_refs[t].at[4 * x + 2 * y + c] if gather else land_refs[t].at[rel - 1],
                    send_sem=send_sems.at[t * N_PEERS + rel - 1], recv_sem=recv_sems.at[t * N_PEERS + rel - 1], device_id=(px, py, pc), device_id_type=MESH,
                ).start()
        token[...] = jnp.zeros_like(token)

    hbm = lambda a: pltpu.HBM(a.shape, a.dtype)
    return pl.pallas_call(
        body,
        name=name,
        out_shape=(pltpu.SemaphoreType.DMA((T * N_PEERS,)), pltpu.SemaphoreType.DMA((T * N_PEERS,)), *[hbm(a) for a in srcs],
                   *[hbm(a) for a in lands], jax.ShapeDtypeStruct((8, LANES), F32)),
        in_specs=(_HBM,) * (2 * T) + (_ANY,),
        out_specs=(_SEM, _SEM) + (_HBM,) * (2 * T) + (pl.BlockSpec(memory_space=pltpu.VMEM),),
        input_output_aliases={t: 2 + t for t in range(2 * T)},
        compiler_params=pltpu.CompilerParams(has_side_effects=_DATAFLOW),
    )(*[pltpu.with_memory_space_constraint(a, pltpu.HBM) for a in list(srcs) + list(lands)], after)


def _send_wait(name, started, after, gather):
    T = (len(started) - 3) // 2
    send_sems, recv_sems, token = started[0], started[1], started[-1]
    thru = started[2:-1]

    def body(*refs):
        src_refs, land_refs = refs[:T], refs[T : 2 * T]
        send_sems, recv_sems = refs[2 * T], refs[2 * T + 1]
        x, y, c = lax.axis_index("x"), lax.axis_index("y"), lax.axis_index("c")
        for t in range(T):
            for rel in range(1, N_DEV):
                cp = pltpu.make_async_remote_copy(
                    src_ref=src_refs[t] if gather else src_refs[t].at[0], dst_ref=land_refs[t].at[0],
                    send_sem=send_sems.at[t * N_PEERS + rel - 1], recv_sem=recv_sems.at[t * N_PEERS + rel - 1], device_id=_peer(rel, x, y, c), device_id_type=MESH,
                )
                cp.wait_send()
                cp.wait_recv()

    outs = pl.pallas_call(
        body,
        name=name,
        out_shape=tuple(pltpu.HBM(a.shape, a.dtype) for a in thru),
        in_specs=(_HBM,) * (2 * T) + (_SEM, _SEM, _ANY),
        out_specs=(_HBM,) * (2 * T),
        input_output_aliases={t: t for t in range(2 * T)},
        compiler_params=pltpu.CompilerParams(has_side_effects=_DATAFLOW),
    )(*thru, send_sems, recv_sems, after)
    return list(outs[:T]), list(outs[T:])


def _sum_blocks(name, parts, tr):
    _, R, Cw = parts[0].shape
    tr = tr if R % tr == 0 else R

    def body(*refs):
        acc = None
        for p_ref in refs[:-1]:
            for d in range(p_ref.shape[0]):
                v = p_ref[d].astype(F32)
                acc = v if acc is None else acc + v
        refs[-1][...] = acc

    return pl.pallas_call(
        body,
        name=name,
        grid=(R // tr,),
        in_specs=[pl.BlockSpec((a.shape[0], tr, Cw), lambda i: (0, i, 0)) for a in parts],
        out_specs=pl.BlockSpec((tr, Cw), lambda i: (i, 0)),
        out_shape=jax.ShapeDtypeStruct((R, Cw), F32),
        compiler_params=_params(1),
    )(*parts)


def _adamw(name, w, g, m, v):
    shape = w.shape
    cols = shape[-1]
    rows = w.size // cols
    tr = rows if rows <= 512 else 512
    assert rows % tr == 0
    w2, g2, m2, v2 = (a.reshape(rows, cols) for a in (w, g, m, v))

    def body(w_ref, g_ref, m_ref, v_ref, d_ref, mo_ref, vo_ref):
        gv = g_ref[...]
        mn = ADAM_B1 * m_ref[...] + (1.0 - ADAM_B1) * gv
        vn = ADAM_B2 * v_ref[...] + (1.0 - ADAM_B2) * jnp.square(gv)
        m_hat = mn / (1.0 - ADAM_B1**ADAM_STEP)
        v_hat = vn / (1.0 - ADAM_B2**ADAM_STEP)
        d_ref[...] = -ADAM_LR * (m_hat / (jnp.sqrt(v_hat) + ADAM_EPS) + ADAM_WD * w_ref[...])
        mo_ref[...] = mn
        vo_ref[...] = vn

    spec = pl.BlockSpec((tr, cols), lambda i: (i, 0))
    d, mn, vn = pl.pallas_call(
        body,
        name=name,
        grid=(rows // tr,),
        in_specs=[spec] * 4,
        out_specs=[spec] * 3,
        out_shape=[jax.ShapeDtypeStruct((rows, cols), F32)] * 3,
        compiler_params=_params(1),
    )(w2, g2, m2, v2)
    return d.reshape(shape), mn.reshape(shape), vn.reshape(shape)


SMALL_SHARDED = ("ln_gain", "ln_bias", "pool_b", "gdn_conv")
SMALL_REPLICATED = ("pool_scale", "gdn_a_log", "gdn_dt_bias", "gdn_norm_w", "ple_gate_b")
WEIGHTS = ("ln_gain", "ln_bias", "pool_w", "pool_b", "pool_scale", "gdn_w_in", "gdn_conv", "gdn_a_log", "gdn_dt_bias",
           "gdn_norm_w", "gdn_w_out", "mlp_w1", "mlp_w2", "ple_gate_w", "ple_gate_b", "ple_proj")
BIG_AXIS = {"gdn_w_in": 1, "gdn_w_out": 0, "mlp_w1": 1, "mlp_w2": 0, "ple_gate_w": 0, "ple_proj": 1, "pool_w": 1}
GATHER_GROUPS = {
    "l0a": (("mlp_w1", 0),),
    "l0b": (("ple_gate_w", 0), ("ple_proj", 0)),
    "l0c": (("mlp_w2", 0),),
    "gdn": (("gdn_w_in", 0), ("gdn_w_out", 0)),
    "l1": (("mlp_w1", 1), ("mlp_w2", 1), ("ple_gate_w", 1), ("ple_proj", 1)),
}
GATHER_AFTER = {"l0b": "l0a", "l0c": "l0a", "gdn": "l0c", "l1": "gdn"}
GRAD_GROUPS = {
    "l1": (("mlp_w1", 1), ("mlp_w2", 1), ("ple_gate_w", 1), ("ple_proj", 1)),
    "gdn": (("gdn_w_in", 0), ("gdn_w_out", 0)),
    "l0": (("mlp_w1", 0), ("mlp_w2", 0), ("ple_gate_w", 0), ("ple_proj", 0)),
}
PACK_PART_ALIGN = 16
SUM_TILE = 128


def _part_rows(a, width):
    rows = a.size // width
    return rows + (-rows) % PACK_PART_ALIGN


def _pack_rows(parts, width, dtype, align):
    padded = []
    for a in parts:
        a2 = a.reshape(-1, width).astype(dtype)
        padded.append(jnp.pad(a2, ((0, _part_rows(a, width) - a2.shape[0]), (0, 0))))
    flat = jnp.concatenate(padded, axis=0)
    return jnp.pad(flat, ((0, (-flat.shape[0]) % align), (0, 0)))


def _pack_blocks(parts, width, dtype, align):
    padded = []
    for a in parts:
        a2 = a.reshape(a.shape[0], -1, width).astype(dtype)
        padded.append(jnp.pad(a2, ((0, 0), (0, _part_rows(a[0], width) - a2.shape[1]), (0, 0))))
    flat = jnp.concatenate(padded, axis=1)
    return jnp.pad(flat, ((0, 0), (0, (-flat.shape[1]) % align), (0, 0)))


def _unpack_rows(packed, shapes, width):
    out, off = [], 0
    for shp in shapes:
        size = 1
        for d in shp:
            size *= d
        out.append(packed[..., off : off + size // width, :].reshape(packed.shape[:-2] + tuple(shp)))
        off += size // width + (-(size // width)) % PACK_PART_ALIGN
    return out


def _split_blocks(name, full):
    ax = BIG_AXIS[name]
    shp = full.shape
    a = full.reshape(shp[:ax] + (N_DEV, shp[ax] // N_DEV) + shp[ax + 1 :])
    return jnp.moveaxis(a, ax, 0)


def _join_blocks(name, blocks):
    ax = BIG_AXIS[name]
    a = jnp.moveaxis(blocks, 0, ax)
    shp = a.shape
    return a.reshape(shp[:ax] + (shp[ax] * shp[ax + 1],) + shp[ax + 2 :])


def _pack_small(parts):
    flat = jnp.concatenate([jnp.pad(a.reshape(-1), (0, (-a.size) % LANES)) for a in parts])
    rows = flat.size // LANES
    return jnp.pad(flat.reshape(rows, LANES), ((0, (-rows) % 8), (0, 0)))


def _unpack_small(packed, shapes):
    flat = packed.reshape(packed.shape[:-2] + (-1,))
    out, off = [], 0
    for shp in shapes:
        size = 1
        for s in shp:
            size *= s
        out.append(flat[..., off : off + size].reshape(flat.shape[:-1] + tuple(shp)))
        off += size + (-size) % LANES
    return out


def _split_w_in(w_in, D, H):
    pad = lambda a: jnp.pad(a, ((0, 0), (0, LANES - H)))
    return w_in[:, : 3 * D], w_in[:, 3 * D : 4 * D], jnp.concatenate([pad(w_in[:, 4 * D : 4 * D + H]), pad(w_in[:, 4 * D + H :])], axis=1)


def kernel(x, p, ln_gain, ln_bias, pool_w, pool_b, pool_scale, gdn_w_in, gdn_conv, gdn_a_log, gdn_dt_bias, gdn_norm_w, gdn_w_out, mlp_w1, mlp_w2, ple_gate_w, ple_gate_b, ple_proj, loss_target, m_ln_gain, m_ln_bias, m_pool_w, m_pool_b, m_pool_scale, m_gdn_w_in, m_gdn_conv, m_gdn_a_log, m_gdn_dt_bias, m_gdn_norm_w, m_gdn_w_out, m_mlp_w1, m_mlp_w2, m_ple_gate_w, m_ple_gate_b, m_ple_proj, v_ln_gain, v_ln_bias, v_pool_w, v_pool_b, v_pool_scale, v_gdn_w_in, v_gdn_conv, v_gdn_a_log, v_gdn_dt_bias, v_gdn_norm_w, v_gdn_w_out, v_mlp_w1, v_mlp_w2, v_ple_gate_w, v_ple_gate_b, v_ple_proj):
    w_sh = dict(ln_gain=ln_gain, ln_bias=ln_bias, pool_w=pool_w, pool_b=pool_b, pool_scale=pool_scale, gdn_w_in=gdn_w_in,
                gdn_conv=gdn_conv, gdn_a_log=gdn_a_log, gdn_dt_bias=gdn_dt_bias, gdn_norm_w=gdn_norm_w, gdn_w_out=gdn_w_out,
                mlp_w1=mlp_w1, mlp_w2=mlp_w2, ple_gate_w=ple_gate_w, ple_gate_b=ple_gate_b, ple_proj=ple_proj)
    m_sh = dict(ln_gain=m_ln_gain, ln_bias=m_ln_bias, pool_w=m_pool_w, pool_b=m_pool_b, pool_scale=m_pool_scale, gdn_w_in=m_gdn_w_in,
                gdn_conv=m_gdn_conv, gdn_a_log=m_gdn_a_log, gdn_dt_bias=m_gdn_dt_bias, gdn_norm_w=m_gdn_norm_w, gdn_w_out=m_gdn_w_out,
                mlp_w1=m_mlp_w1, mlp_w2=m_mlp_w2, ple_gate_w=m_ple_gate_w, ple_gate_b=m_ple_gate_b, ple_proj=m_ple_proj)
    v_sh = dict(ln_gain=v_ln_gain, ln_bias=v_ln_bias, pool_w=v_pool_w, pool_b=v_pool_b, pool_scale=v_pool_scale, gdn_w_in=v_gdn_w_in,
                gdn_conv=v_gdn_conv, gdn_a_log=v_gdn_a_log, gdn_dt_bias=v_gdn_dt_bias, gdn_norm_w=v_gdn_norm_w, gdn_w_out=v_gdn_w_out,
                mlp_w1=v_mlp_w1, mlp_w2=v_mlp_w2, ple_gate_w=v_ple_gate_w, ple_gate_b=v_ple_gate_b, ple_proj=v_ple_proj)
    xs, tg = x[0], loss_target[0]
    ps = p[:, 0]
    S, D = xs.shape
    H = D // HEAD_DIM
    me = 4 * lax.axis_index("x") + 2 * lax.axis_index("y") + lax.axis_index("c")
    layer = lambda n, l: (w_sh[n][0] if n in ("gdn_w_in", "gdn_w_out") else w_sh[n][l])

    pool_packed = _pack_rows([w_sh["pool_w"][0]], D, BF16, PACK_PART_ALIGN)
    small_packed = _pack_small([w_sh[n] for n in SMALL_SHARDED])
    pool_gathered, small_gathered = _all_gather("gather_first", [pool_packed, small_packed])
    W = {"pool_w": _join_blocks("pool_w", _unpack_rows(pool_gathered, [w_sh["pool_w"][0].shape], D)[0])}

    started = {}

    def start(g, after):
        srcs = [layer(n, l).astype(BF16) for n, l in GATHER_GROUPS[g]]
        started[g] = tuple(_send_start(f"gather_{g}_start", srcs, [lax.empty((N_DEV,) + a.shape, BF16) for a in srcs], True, after))
        return started[g][-1]

    first_token = start("l0a", small_gathered)
    smalls = _unpack_small(small_gathered, [w_sh[n].shape for n in SMALL_SHARDED])
    for n, a in zip(SMALL_SHARDED, smalls):
        W[n] = jnp.moveaxis(a, 0, -2).reshape(a.shape[1:-1] + (N_DEV * a.shape[-1],))
    W["ln_gain"] = W["ln_gain"].reshape(2 * DEPTH, D)
    W["ln_bias"] = W["ln_bias"].reshape(2 * DEPTH, D)
    W["pool_b"] = W["pool_b"].reshape(1, D) + first_token[0:1, 0:1]
    W["gdn_conv"] = W["gdn_conv"][0]
    W["pool_scale"] = pool_scale
    W["ple_gate_b"] = ple_gate_b
    W["gdn_norm_w"] = gdn_norm_w
    W["gdn_a_log"] = jnp.pad(gdn_a_log, ((0, 0), (0, LANES - H)))
    W["gdn_dt_bias"] = jnp.pad(gdn_dt_bias, ((0, 0), (0, LANES - H)))

    def fetch(g, after):
        members = GATHER_GROUPS[g]
        srcs, lands = _send_wait(f"gather_{g}_wait", started[g], after, True)
        tokens = [start(nxt, lands[0]) for nxt, prev in GATHER_AFTER.items() if prev == g]
        out = {n: _join_blocks(n, lax.dynamic_update_index_in_dim(land, src, me, 0)) for (n, _), src, land in zip(members, srcs, lands)}
        if "gdn_w_in" in out:
            out["gdn_wqkv"], out["gdn_wz"], out["gdn_wba"] = _split_w_in(out.pop("gdn_w_in"), D, H)
        out["_after"] = tokens
        return out

    sent = {}

    def emit(g, grads):
        srcs = [grads[n] for n, _ in GRAD_GROUPS[g]]
        lands = [lax.empty((N_PEERS,) + a.shape[1:], BF16) for a in srcs]
        sent[g] = tuple(_send_start(f"grads_{g}_start", srcs, lands, False, srcs[0]))
        return sent[g][-1]

    loss_cols, grad_x, G = _local_step(xs, ps, tg, W, fetch, emit)
    loss = lax.psum(0.5 * jnp.sum(loss_cols) / D, MESH_AXES)

    pool_src = _pack_blocks([_split_blocks("pool_w", G["pool_w"])], D, BF16, PACK_PART_ALIGN)
    pool_sum = _sum_blocks("sum_pool_grads", [_exchange("exchange_pool_grads", pool_src)], SUM_TILE)
    grads = {"pool_w": _unpack_rows(pool_sum, [w_sh["pool_w"][0].shape], D)[0].reshape(w_sh["pool_w"].shape)}
    small_names = SMALL_SHARDED + SMALL_REPLICATED
    gs_packed = _pack_small([G[n] for n in small_names])
    (gs_all,) = _all_gather("gather_small_grads", [gs_packed])
    gs_sum = _sum_blocks("sum_small_grads", [gs_all], SUM_TILE)
    for n, a in zip(small_names, _unpack_small(gs_sum, [G[n].shape for n in small_names])):
        if n in SMALL_SHARDED:
            width = w_sh[n].shape[-1]
            a = a.reshape(w_sh[n].shape[:-1] + (N_DEV * width,))
            a = lax.dynamic_slice_in_dim(a, me * width, width, axis=a.ndim - 1)
        grads[n] = a.reshape(w_sh[n].shape)

    per_layer = {}
    for g, members in GRAD_GROUPS.items():
        srcs, lands = _send_wait(f"grads_{g}_wait", sent[g], grad_x, False)
        for (n, l), src, land in zip(members, srcs, lands):
            own = lax.dynamic_index_in_dim(src, me, 0, keepdims=True)
            as3d = lambda a: a.reshape(a.shape[0], -1, a.shape[-1])
            per_layer[(n, l)] = _sum_blocks(f"sum_grads_{n}_{l}", [as3d(land), as3d(own)], SUM_TILE).reshape(layer(n, l).shape)
    for n in ("gdn_w_in", "gdn_w_out"):
        grads[n] = per_layer[(n, 0)][None]
    for n in ("mlp_w1", "mlp_w2", "ple_gate_w", "ple_proj"):
        grads[n] = jnp.stack([per_layer[(n, 0)], per_layer[(n, 1)]])

    deltas, new_m, new_v = {}, {}, {}
    for n in WEIGHTS:
        deltas[n], new_m[n], new_v[n] = _adamw(f"adamw_{n}", w_sh[n], grads[n], m_sh[n], v_sh[n])
    return (loss, grad_x[None], *[grads[n] for n in WEIGHTS], *[deltas[n] for n in WEIGHTS],
            *[new_m[n] for n in WEIGHTS], *[new_v[n] for n in WEIGHTS])
```

```python
import functools

import jax
import jax.numpy as jnp
from jax import lax
from jax.experimental import pallas as pl
from jax.experimental.pallas import tpu as pltpu

F32 = jnp.float32
BF16 = jnp.bfloat16
MESH_AXES = ("x", "y", "c")
N_DEV = 8
MESH = pl.DeviceIdType.MESH

DEPTH = 2
ALPHA = (2.0 * DEPTH) ** 0.25
LN_EPS = 1e-5
RMS_EPS = 1e-6
L2_EPS = 1e-6
HEAD_DIM = 128
CONV_WIDTH = 4
POOL_WINDOWS = (2, 4, 8, 16)
POOL_HALO = 16
CONV_HALO = 8
LANES = 128
ADAM_LR = 0.001
ADAM_B1 = 0.9
ADAM_B2 = 0.999
ADAM_EPS = 1e-08
ADAM_WD = 0.01
ADAM_STEP = 10

VMEM_LIMIT = 56 * 1024 * 1024
ROW_TILE = 512
CONV_TILE = 256
CHUNK = 128
MM_TM, MM_TN, MM_TK = 512, 1024, 1024
DW_TILES = dict(tm=512, tn=512, tk=8192, b_outer=True)

_DIMS = {
    "nn": (((1,), (0,)), ((), ())),
    "nt": (((1,), (1,)), ((), ())),
    "tn": (((0,), (0,)), ((), ())),
}


def _params(n_axes):
    return pltpu.CompilerParams(dimension_semantics=("arbitrary",) * n_axes, vmem_limit_bytes=VMEM_LIMIT)


def _fit(tile, n):
    tile = min(tile, n)
    while n % tile:
        tile //= 2
    return tile


def _mm(name, a, b, mode, out_dtypes, epi=None, extras=(), a_fn=None, tm=None, tn=None, tk=None, b_outer=False, after=(), out_blocks=1, accs=0):
    if mode == "tn":
        K, M = a.shape
    else:
        M, K = a.shape
    N = b.shape[0] if mode == "nt" else b.shape[1]
    tm, tn, tk = _fit(tm or MM_TM, M), (N // out_blocks if out_blocks > 1 else N if accs else _fit(tn or MM_TN, N)), _fit(tk or MM_TK, K)
    nk = K // tk

    def at(f):
        return (lambda j, i, k: f(i, j, k)) if b_outer else f

    a_spec = pl.BlockSpec((tk, tm), at(lambda i, j, k: (k, i))) if mode == "tn" else pl.BlockSpec((tm, tk), at(lambda i, j, k: (i, k)))
    b_spec = pl.BlockSpec((tn, tk), at(lambda i, j, k: (j, k))) if mode == "nt" else pl.BlockSpec((tk, tn), at(lambda i, j, k: (k, j)))
    ex_spec = {"tile": pl.BlockSpec((tm, tn), at(lambda i, j, k: (i, j))), "row": pl.BlockSpec((1, tn), at(lambda i, j, k: (0, j))),
               "rows": pl.BlockSpec((tm, LANES), at(lambda i, j, k: (i, 0)))}
    ex_specs = [pl.BlockSpec(e.shape, lambda i, j, k: (0, 0)) if kind == "whole" else ex_spec[kind] for e, kind in extras]
    assert accs == 0 or (tn == N and nk == 1 and not b_outer), name
    n_ex, n_out, n_after = len(extras), len(out_dtypes), len(after)

    def body(*refs):
        a_ref, b_ref = refs[0], refs[1]
        ex_refs = refs[2 : 2 + n_ex]
        out_refs = refs[2 + n_ex + n_after : 2 + n_ex + n_after + n_out]
        av = a_ref[...]
        if a_fn is not None:
            av = a_fn(av)
        part = lax.dot_general(av.astype(BF16), b_ref[...].astype(BF16), _DIMS[mode], preferred_element_type=F32)

        def finish(res):
            vals = epi(res, *[e[...] for e in ex_refs]) if epi is not None else (res,)
            for o_ref, v in zip(out_refs, vals[:n_out]):
                o_ref[...] = v.astype(o_ref.dtype)
            for a_ref, v in zip(refs[2 + n_ex + n_after + n_out :], vals[n_out:]):

                @pl.when(pl.program_id(0) == 0)
                def _(a_ref=a_ref, v=v):
                    a_ref[...] = v

                @pl.when(pl.program_id(0) > 0)
                def _(a_ref=a_ref, v=v):
                    a_ref[...] += v

        if nk == 1:
            finish(part)
        else:
            acc = refs[-1]
            k = pl.program_id(2)

            @pl.when(k == 0)
            def _():
                acc[...] = part

            @pl.when(k > 0)
            def _():
                acc[...] += part

            @pl.when(k == nk - 1)
            def _():
                finish(acc[...])

    outs = pl.pallas_call(
        body,
        name=name,
        grid=(N // tn, M // tm, nk) if b_outer else (M // tm, N // tn, nk),
        in_specs=[a_spec, b_spec] + ex_specs + [pl.BlockSpec(memory_space=pl.ANY)] * n_after,
        out_specs=[pl.BlockSpec((tm, LANES), at(lambda i, j, k: (i, 0))) if isinstance(dt, tuple)
                   else pl.BlockSpec((tm, tn), at(lambda i, j, k: (i, j))) if out_blocks == 1
                   else pl.BlockSpec((None, tm, tn), at(lambda i, j, k: (j, i, 0))) for dt in out_dtypes]
        + [pl.BlockSpec((1, N), lambda i, j, k: (0, 0))] * accs,
        out_shape=[jax.ShapeDtypeStruct((M, LANES), dt[0]) if isinstance(dt, tuple)
                   else jax.ShapeDtypeStruct((M, N) if out_blocks == 1 else (out_blocks, M, tn), dt) for dt in out_dtypes]
        + [jax.ShapeDtypeStruct((1, N), F32)] * accs,
        scratch_shapes=[pltpu.VMEM((tm, tn), F32)] if nk > 1 else [],
        compiler_params=_params(3),
    )(a, b, *[e for e, _ in extras], *after)
    return outs[0] if n_out + accs == 1 else outs


def _rowwise(name, fn, S, ts, rows=(), halos=(), consts=(), outs=(), accs=()):
    ts = min(ts, S)
    assert S % ts == 0
    n = S // ts
    in_specs = [pl.BlockSpec((ts, a.shape[1]), lambda i: (i, 0)) for a in rows]
    for a, kind, hr in halos:
        r, nb = ts // hr, S // hr
        if kind == "prev":
            in_specs.append(pl.BlockSpec((hr, a.shape[1]), lambda i, r=r: (jnp.maximum(i * r - 1, 0), 0)))
        else:
            in_specs.append(pl.BlockSpec((hr, a.shape[1]), lambda i, r=r, nb=nb: (jnp.minimum((i + 1) * r, nb - 1), 0)))
    in_specs += [pl.BlockSpec(a.shape, lambda i, nd=a.ndim: (0,) * nd) for a in consts]
    out_specs = [pl.BlockSpec((ts, w), lambda i: (i, 0)) for w, _ in outs]
    out_specs += [pl.BlockSpec((r, w), lambda i: (0, 0)) for r, w in accs]
    out_shape = [jax.ShapeDtypeStruct((S, w), dt) for w, dt in outs]
    out_shape += [jax.ShapeDtypeStruct((r, w), F32) for r, w in accs]
    nr, nh, nc, no = len(rows), len(halos), len(consts), len(outs)

    def body(*refs):
        i = pl.program_id(0)
        rv = [r[...] for r in refs[:nr]]
        hv = [r[...] for r in refs[nr : nr + nh]]
        cv = [r[...] for r in refs[nr + nh : nr + nh + nc]]
        o_refs = refs[nr + nh + nc : nr + nh + nc + no]
        a_refs = refs[nr + nh + nc + no :]
        ovals, avals = fn(i, n, rv, hv, cv)
        for o_ref, v in zip(o_refs, ovals):
            o_ref[...] = v.astype(o_ref.dtype)
        for a_ref, v in zip(a_refs, avals):

            @pl.when(i == 0)
            def _(a_ref=a_ref, v=v):
                a_ref[...] = v

            @pl.when(i > 0)
            def _(a_ref=a_ref, v=v):
                a_ref[...] += v

    res = pl.pallas_call(
        body,
        name=name,
        grid=(n,),
        in_specs=in_specs,
        out_specs=out_specs,
        out_shape=out_shape,
        compiler_params=_params(1),
    )(*rows, *[h[0] for h in halos], *consts)
    return list(res)


def _ln(h, g, b):
    mu = jnp.mean(h, axis=-1, keepdims=True)
    d = h - mu
    var = jnp.mean(d * d, axis=-1, keepdims=True)
    rstd = lax.rsqrt(var + LN_EPS)
    xhat = d * rstd
    return xhat, rstd, xhat * g + b


def _ln_bwd(dy, xhat, rstd, g):
    dxh = dy * g
    m1 = jnp.mean(dxh, axis=-1, keepdims=True)
    m2 = jnp.mean(dxh * xhat, axis=-1, keepdims=True)
    dh = rstd * (dxh - m1 - xhat * m2)
    return dh, jnp.sum(dy * xhat, axis=0, keepdims=True), jnp.sum(dy, axis=0, keepdims=True)


def _wide(col, ts):
    return jnp.broadcast_to(col, (ts, LANES))


def _pool_fwd(x, wp, pb, ps, g, b):
    S, D = x.shape
    gw = D // len(POOL_WINDOWS)
    ts = min(ROW_TILE, S)

    def fn(i, n, rv, hv, cv):
        (xc,), (xp,) = rv, hv
        wpv, pbv, psv, gv, bv = cv
        xp = jnp.where(i > 0, xp, 0.0)
        xx = jnp.concatenate([xp, xc], axis=0)
        t = i * ts + lax.broadcasted_iota(jnp.int32, (ts, 1), 0)
        pooled, ys = [], []
        for gi, w in enumerate(POOL_WINDOWS):
            s = xx[:, gi * gw : (gi + 1) * gw]
            k = 1
            while k < w:
                s = s + pltpu.roll(s, k, axis=0)
                k *= 2
            cnt = jnp.minimum(t + 1, w).astype(F32)
            pg = (s[POOL_HALO:, :] / cnt - xc[:, gi * gw : (gi + 1) * gw]).astype(BF16)
            pooled.append(pg)
            ys.append(jnp.dot(pg, wpv[gi], preferred_element_type=F32))
        y = jnp.concatenate(ys, axis=1)
        h = ALPHA * xc + (y + pbv) * psv
        xhat, rstd, xa = _ln(h, gv, bv)
        return (jnp.concatenate(pooled, axis=1), xhat, _wide(rstd, ts), xa), ()

    return _rowwise(
        "pool_fwd", fn, S, ts, rows=[x], halos=[(x, "prev", POOL_HALO)], consts=[wp, pb, ps, g, b],
        outs=[(D, BF16), (D, F32), (LANES, F32), (D, BF16)],
    )


def _pool_bwd(dh, pooled, wp, pb, ps):
    S, D = dh.shape
    gw = D // len(POOL_WINDOWS)
    ts = min(ROW_TILE, S)
    te = ts + POOL_HALO

    def fn(i, n, rv, hv, cv):
        (dhc, pc), (dhn,) = rv, hv
        wpv, pbv, psv = cv
        dhn = jnp.where(i < n - 1, dhn, 0.0)
        dy_ext = jnp.concatenate([dhc, dhn], axis=0) * psv
        dyb = dy_ext.astype(BF16)
        t = i * ts + lax.broadcasted_iota(jnp.int32, (te, 1), 0)
        dxs, ys = [], []
        for gi, w in enumerate(POOL_WINDOWS):
            sl = slice(gi * gw, (gi + 1) * gw)
            dp = lax.dot_general(dyb[:, sl], wpv[gi], _DIMS["nt"], preferred_element_type=F32)
            s = dp / jnp.minimum(t + 1, w).astype(F32)
            k = 1
            while k < w:
                s = s + pltpu.roll(s, k, axis=0)
                k *= 2
            s = pltpu.roll(s, POOL_HALO - (w - 1), axis=0)
            dxs.append(s[POOL_HALO:, :] - dp[:ts, :])
            ys.append(jnp.dot(pc[:, sl], wpv[gi], preferred_element_type=F32))
        dx = ALPHA * dhc + jnp.concatenate(dxs, axis=1)
        y = jnp.concatenate(ys, axis=1) + pbv
        dscale = jnp.sum(dhc * y, axis=0, keepdims=True)
        dbias = jnp.sum(dy_ext[:ts, :], axis=0, keepdims=True)
        return (dx, dyb[:ts, :]), (dscale, dbias)

    return _rowwise(
        "pool_bwd", fn, S, ts, rows=[dh, pooled], halos=[(dh, "next", POOL_HALO)], consts=[wp, pb, ps],
        outs=[(D, F32), (D, BF16)], accs=[(1, D), (1, D)],
    )


def _pool_dw(pooled, dy):
    S, D = pooled.shape
    G = len(POOL_WINDOWS)
    gw = D // G
    tk = min(MM_TK, S)
    nk = S // tk

    def body(p_ref, d_ref, o_ref):
        k = pl.program_id(1)
        part = lax.dot_general(p_ref[...], d_ref[...], _DIMS["tn"], preferred_element_type=F32)

        @pl.when(k == 0)
        def _():
            o_ref[...] = part

        @pl.when(k > 0)
        def _():
            o_ref[...] += part

    return pl.pallas_call(
        body,
        name="pool_dw",
        grid=(G, nk),
        in_specs=[pl.BlockSpec((tk, gw), lambda g, k: (k, g)), pl.BlockSpec((tk, gw), lambda g, k: (k, g))],
        out_specs=pl.BlockSpec((None, gw, gw), lambda g, k: (g, 0, 0)),
        out_shape=jax.ShapeDtypeStruct((G, gw, gw), F32),
        compiler_params=_params(2),
    )(pooled, dy)


def _res_ln_epi(acc, xh, gp_, bp_, g, b):
    xhat, rstd, xo = _ln(ALPHA * (xh * gp_ + bp_) + acc, g, b)
    return xhat, _wide(rstd, acc.shape[0]), xo


def _res_ln_ffpe_epi(acc, xh, gate, pp, gp_, bp_, g, b):
    return _res_ln_epi(acc + jax.nn.sigmoid(gate) * pp, xh, gp_, bp_, g, b)


def _final_ln_loss_epi(acc, xh, gate, pp, tgt, gp_, bp_, g, b):
    sg = jax.nn.sigmoid(gate)
    xhat, rstd, y = _ln(ALPHA * (xh * gp_ + bp_) + acc + sg * pp, g, b)
    e = y - tgt
    dh, dg, db = _ln_bwd(e * (1.0 / acc.shape[1]), xhat, rstd, g)
    dgt, dpp, dbg = _ple_grads(dh, sg, pp)
    return dh, dh, dgt, dpp, jnp.sum(e * e, axis=0, keepdims=True), dg, db, dbg


def _ln_bwd_epi(acc, rest, xhat, rstd, g):
    dh, dg, db = _ln_bwd(acc + rest, xhat, rstd[:, :1], g)
    return dh, dh, dg, db


def _ple_grads(dh, sg, pp):
    dgt = dh * pp * sg * (1.0 - sg)
    return dgt, dh * sg, jnp.sum(dgt, axis=0, keepdims=True)


def _ln_bwd_ple_epi(acc, rest, xhat, rstd, gate, pp, g):
    dh, dg, db = _ln_bwd(acc + rest, xhat, rstd[:, :1], g)
    dgt, dpp, dbg = _ple_grads(dh, jax.nn.sigmoid(gate), pp)
    return dh, dh, dgt, dpp, dg, db, dbg


def _silu(c):
    return c * jax.nn.sigmoid(c)


def _qkv_point(c, is_qk, scale):
    s = _silu(c)
    nrm = s * lax.rsqrt(jnp.sum(s * s, axis=-1, keepdims=True) + L2_EPS) * scale
    return jnp.where(is_qk, nrm, s)


def _conv_rows(xx, wv, lo, rows):
    acc = None
    for j in range(CONV_WIDTH):
        sh = CONV_WIDTH - 1 - j
        term = (pltpu.roll(xx, sh, axis=0) if sh else xx)[lo : lo + rows, :] * wv[j : j + 1, :]
        acc = term if acc is None else acc + term
    return acc


def _conv_fwd(qkv_pre, conv_w):
    S, W = qkv_pre.shape
    D = W // 3
    H = D // HEAD_DIM
    ts = min(CONV_TILE, S)
    r = ts // CONV_HALO

    def body(x_ref, xp_ref, w_ref, o_ref):
        j, i = pl.program_id(0), pl.program_id(1)
        xp = jnp.where(i > 0, xp_ref[...], 0.0)
        xx = jnp.concatenate([xp, x_ref[...]], axis=0)
        c = _conv_rows(xx, w_ref[...], CONV_HALO, ts)
        scale = jnp.where(j == 0, HEAD_DIM**-0.5, 1.0).astype(F32)
        for h in range(H):
            sl = slice(h * HEAD_DIM, (h + 1) * HEAD_DIM)
            o_ref[:, sl] = _qkv_point(c[:, sl], j < 2, scale)

    return pl.pallas_call(
        body,
        name="gdn_conv_fwd",
        grid=(3, S // ts),
        in_specs=[
            pl.BlockSpec((ts, D), lambda j, i: (i, j)),
            pl.BlockSpec((CONV_HALO, D), lambda j, i: (jnp.maximum(i * r - 1, 0), j)),
            pl.BlockSpec((CONV_WIDTH, D), lambda j, i: (0, j)),
        ],
        out_specs=pl.BlockSpec((ts, D), lambda j, i: (i, j)),
        out_shape=jax.ShapeDtypeStruct((S, W), F32),
        compiler_params=_params(2),
    )(qkv_pre, qkv_pre, conv_w)


def _conv_bwd(qkv_pre, conv_w, dqkvn):
    S, W = qkv_pre.shape
    D = W // 3
    H = D // HEAD_DIM
    ts = min(CONV_TILE, S)
    r, nb = ts // CONV_HALO, S // CONV_HALO
    te = ts + CONV_HALO

    def body(x_ref, xp_ref, xn_ref, w_ref, d_ref, dn_ref, dx_ref, dw_ref):
        j, i = pl.program_id(0), pl.program_id(1)
        n = pl.num_programs(1)
        wv = w_ref[...]
        xp = jnp.where(i > 0, xp_ref[...], 0.0)
        xx = jnp.concatenate([xp, x_ref[...], xn_ref[...]], axis=0)
        xr = [pltpu.roll(xx, sh, axis=0) if sh else xx for sh in range(CONV_WIDTH)]
        c = None
        for jj in range(CONV_WIDTH):
            term = xr[CONV_WIDTH - 1 - jj][CONV_HALO : CONV_HALO + te, :] * wv[jj : jj + 1, :]
            c = term if c is None else c + term
        dn = jnp.where(i < n - 1, dn_ref[...], 0.0)
        dout = jnp.concatenate([d_ref[...], dn], axis=0)
        scale = jnp.where(j == 0, HEAD_DIM**-0.5, 1.0).astype(F32)
        dcs = []
        for h in range(H):
            sl = slice(h * HEAD_DIM, (h + 1) * HEAD_DIM)
            _, vjp = jax.vjp(lambda cc: _qkv_point(cc, j < 2, scale), c[:, sl])
            dcs.append(vjp(dout[:, sl])[0])
        dc = jnp.concatenate(dcs, axis=1)
        dx = None
        dws = []
        for jj in range(CONV_WIDTH):
            sh = CONV_WIDTH - 1 - jj
            term = pltpu.roll(dc, CONV_HALO - sh, axis=0)[CONV_HALO:, :] * wv[jj : jj + 1, :]
            dx = term if dx is None else dx + term
            dws.append(jnp.sum(dc[:ts, :] * xr[sh][CONV_HALO : CONV_HALO + ts, :], axis=0, keepdims=True))
        dx_ref[...] = dx.astype(dx_ref.dtype)
        dw = jnp.concatenate(dws, axis=0)

        @pl.when(i == 0)
        def _():
            dw_ref[...] = dw

        @pl.when(i > 0)
        def _():
            dw_ref[...] += dw

    return pl.pallas_call(
        body,
        name="gdn_conv_bwd",
        grid=(3, S // ts),
        in_specs=[
            pl.BlockSpec((ts, D), lambda j, i: (i, j)),
            pl.BlockSpec((CONV_HALO, D), lambda j, i: (jnp.maximum(i * r - 1, 0), j)),
            pl.BlockSpec((CONV_HALO, D), lambda j, i: (jnp.minimum((i + 1) * r, nb - 1), j)),
            pl.BlockSpec((CONV_WIDTH, D), lambda j, i: (0, j)),
            pl.BlockSpec((ts, D), lambda j, i: (i, j)),
            pl.BlockSpec((CONV_HALO, D), lambda j, i: (jnp.minimum((i + 1) * r, nb - 1), j)),
        ],
        out_specs=[pl.BlockSpec((ts, D), lambda j, i: (i, j)), pl.BlockSpec((CONV_WIDTH, D), lambda j, i: (0, j))],
        out_shape=[jax.ShapeDtypeStruct((S, W), BF16), jax.ShapeDtypeStruct((CONV_WIDTH, W), F32)],
        compiler_params=_params(2),
    )(qkv_pre, qkv_pre, qkv_pre, conv_w, dqkvn, dqkvn)


def _softplus(x):
    pos = x > 0.0
    return jnp.where(pos, x, 0.0) + jnp.log(1.0 + jnp.exp(jnp.where(pos, -x, x)))


def _gates(bl, al, alog, dt):
    return jax.nn.sigmoid(bl), -jnp.exp(alog) * _softplus(al + dt)


def _gates_fwd(ba, alog, dt):
    S = ba.shape[0]
    ts = min(ROW_TILE, S)

    def fn(i, n, rv, hv, cv):
        return _gates(rv[0][:, :LANES], rv[0][:, LANES:], cv[0], cv[1]), ()

    return _rowwise("gdn_gates_fwd", fn, S, ts, rows=[ba], consts=[alog, dt], outs=[(LANES, F32), (LANES, F32)])


def _gates_bwd(ba, alog, dt, dbeta, dg, H):
    S = ba.shape[0]
    ts = min(ROW_TILE, S)

    def fn(i, n, rv, hv, cv):
        bav, dbv, dgv = rv
        real = lax.broadcasted_iota(jnp.int32, (1, LANES), 1) < H
        _, vjp = jax.vjp(_gates, bav[:, :LANES], bav[:, LANES:], cv[0], cv[1])
        dbl, dal, dalog, ddt = vjp((jnp.where(real, dbv, 0.0), jnp.where(real, dgv, 0.0)))
        dbl, dal = jnp.where(real, dbl, 0.0), jnp.where(real, dal, 0.0)
        return (jnp.concatenate([dbl, dal], axis=1),), (jnp.where(real, dalog, 0.0), jnp.where(real, ddt, 0.0))

    return _rowwise(
        "gdn_gates_bwd", fn, S, ts, rows=[ba, dbeta, dg], consts=[alog, dt], outs=[(2 * LANES, BF16)],
        accs=[(1, LANES), (1, LANES)],
    )


def _split_bf16(a, n):
    parts, rest = [], a
    for _ in range(n):
        piece = rest.astype(BF16)
        parts.append(piece)
        rest = rest - piece.astype(F32)
    return parts


def _tri_dot(a, b, mode, tri):
    d = lambda u, v: lax.dot_general(u, v, _DIMS[mode], preferred_element_type=F32)
    if tri == 0:
        return sum(d(a.astype(BF16), piece) for piece in _split_bf16(b, 3))
    return sum(d(piece, b.astype(BF16)) for piece in _split_bf16(a, 3))


def _bdot_raw(a, b, mode):
    return lax.dot_general(a.astype(BF16), b.astype(BF16), _DIMS[mode], preferred_element_type=F32)


@functools.partial(jax.custom_vjp, nondiff_argnums=(2,))
def _bdot(a, b, mode):
    return _bdot_raw(a, b, mode)


def _bdot_fwd(a, b, mode):
    return _bdot_raw(a, b, mode), (a, b)


def _bdot_bwd(mode, res, ct):
    a, b = res
    if mode == "nn":
        return _bdot(ct, b, "nt"), _bdot(a, ct, "tn")
    if mode == "nt":
        return _bdot(ct, b, "nn"), _bdot(ct, a, "tn")
    return _bdot(b, ct, "nt"), _bdot(a, ct, "nn")


_bdot.defvjp(_bdot_fwd, _bdot_bwd)


@jax.custom_vjp
def _unit_lower_inverse(a_strict):
    return _unit_lower_inverse_raw(a_strict)


def _unit_lower_inverse_fwd(a_strict):
    t = _unit_lower_inverse_raw(a_strict)
    return t, t


def _unit_lower_inverse_bwd(t, ct):
    left = [_bdot(ti, ci, "tn") for ti, ci in zip(t, ct)]
    return (tuple(-_bdot(li, ti, "nt") for li, ti in zip(left, t)),)


_unit_lower_inverse.defvjp(_unit_lower_inverse_fwd, _unit_lower_inverse_bwd)


@jax.custom_vjp
def _saved_inverse(a_strict, t):
    return t


def _saved_inverse_fwd(a_strict, t):
    return t, t


def _saved_inverse_bwd(t, ct):
    return _unit_lower_inverse_bwd(t, ct) + (tuple(jnp.zeros_like(ti) for ti in t),)


_saved_inverse.defvjp(_saved_inverse_fwd, _saved_inverse_bwd)


def _unit_lower_inverse_raw(a_strict):
    C = a_strict[0].shape[0]
    ii = lax.broadcasted_iota(jnp.int32, (C, C), 0)
    jj = lax.broadcasted_iota(jnp.int32, (C, C), 1)
    eye = (ii == jj).astype(F32)
    blk = 16
    same = (ii // blk) == (jj // blk)
    p = [-jnp.where(same, a, 0.0) for a in a_strict]
    t = [eye + x for x in p]
    for _ in range(3):
        p = [_bdot(x, x, "nn") for x in p]
        t = [ti + _bdot(ti, x, "nn") for ti, x in zip(t, p)]
    while blk < C:
        same2 = (ii // (2 * blk)) == (jj // (2 * blk))
        off = jnp.logical_and(same2, jnp.logical_not(same))
        te = [_bdot(ti, jnp.where(off, a, 0.0), "nn") for ti, a in zip(t, a_strict)]
        t = [ti - _bdot(x, ti, "nn") for ti, x in zip(t, te)]
        same, blk = same2, 2 * blk
    return tuple(t)


def _chunk_heads(q, k, v, gc_col, gc_row, b_col, s0, t_saved=None, with_t=False):
    R = range(len(q))
    C = q[0].shape[0]
    ii = lax.broadcasted_iota(jnp.int32, (C, C), 0)
    jj = lax.broadcasted_iota(jnp.int32, (C, C), 1)
    rows = lax.broadcasted_iota(jnp.int32, (C, 1), 0)
    decay = [jnp.where(ii >= jj, jnp.exp(jnp.minimum(gc_col[h] - gc_row[h], 0.0)), 0.0) for h in R]
    kb = [k[h] * b_col[h] for h in R]
    a = [_bdot(kb[h], k[h], "nt") * decay[h] for h in R]
    qk = [_bdot(q[h], k[h], "nt") * decay[h] for h in R]
    a_strict = tuple(jnp.where(ii > jj, a[h], 0.0) for h in R)
    t = _unit_lower_inverse(a_strict) if t_saved is None else _saved_inverse(a_strict, t_saved)
    eg = [jnp.exp(gc_col[h]) for h in R]
    u = [_bdot(t[h], v[h] * b_col[h], "nn") for h in R]
    w = [_bdot(t[h], kb[h] * eg[h], "nn") for h in R]
    g_last = [jnp.sum(jnp.where(rows == C - 1, gc_col[h], 0.0), axis=0, keepdims=True) for h in R]
    kd = [k[h] * jnp.exp(g_last[h] - gc_col[h]) for h in R]
    ws = [_bdot(w[h], s0[h], "nn") for h in R]
    qs = [_bdot(q[h] * eg[h], s0[h], "nn") for h in R]
    v_new = [u[h] - ws[h] for h in R]
    o = [qs[h] + _bdot(qk[h], v_new[h], "nn") for h in R]
    s1 = [s0[h] * jnp.exp(g_last[h]) + _bdot(kd[h], v_new[h], "tn") for h in R]
    return (tuple(o), tuple(s1), t) if with_t else (tuple(o), tuple(s1))


def _pick_lane(a, h):
    lanes = lax.broadcasted_iota(jnp.int32, a.shape, 1)
    return jnp.sum(jnp.where(lanes == h, a, 0.0), axis=1, keepdims=True)


def _pick_row(a, h):
    rows = lax.broadcasted_iota(jnp.int32, a.shape, 0)
    return jnp.sum(jnp.where(rows == h, a, 0.0), axis=0, keepdims=True)


def _tri(C):
    ii = lax.broadcasted_iota(jnp.int32, (C, C), 0)
    jj = lax.broadcasted_iota(jnp.int32, (C, C), 1)
    return (ii >= jj).astype(F32)


def _delta_fwd(qkvn, g_pad, g_rows, beta_pad):
    S, W = qkvn.shape
    D = W // 3
    H = D // HEAD_DIM
    C = min(CHUNK, S)
    N = S // C

    def body(x_ref, gp_ref, gr_ref, bp_ref, o_ref, sall_ref, tall_ref, st):
        n = pl.program_id(0)

        @pl.when(n == 0)
        def _():
            st[...] = jnp.zeros_like(st)

        low = _tri(C)
        gc_cols = _tri_dot(low, gp_ref[...], "nn", 0)
        gc_rows = _tri_dot(gr_ref[...], low, "nt", 1)
        bcols = bp_ref[...]
        hs = range(H)
        s0 = tuple(st[h] for h in hs)
        for h in hs:
            sall_ref[h] = s0[h]
        o, s1, t = _chunk_heads(
            tuple(x_ref[:, h * HEAD_DIM : (h + 1) * HEAD_DIM] for h in hs),
            tuple(x_ref[:, D + h * HEAD_DIM : D + (h + 1) * HEAD_DIM] for h in hs),
            tuple(x_ref[:, 2 * D + h * HEAD_DIM : 2 * D + (h + 1) * HEAD_DIM] for h in hs),
            tuple(_pick_lane(gc_cols, h) for h in hs), tuple(_pick_row(gc_rows, h) for h in hs),
            tuple(_pick_lane(bcols, h) for h in hs), s0, with_t=True,
        )
        for h in hs:
            st[h] = s1[h]
            o_ref[:, h * HEAD_DIM : (h + 1) * HEAD_DIM] = o[h]
            tall_ref[h] = t[h].astype(tall_ref.dtype)

    return pl.pallas_call(
        body,
        name="gdn_delta_fwd",
        grid=(N,),
        in_specs=[
            pl.BlockSpec((C, W), lambda n: (n, 0)),
            pl.BlockSpec((C, LANES), lambda n: (n, 0)),
            pl.BlockSpec((None, 8, C), lambda n: (n, 0, 0)),
            pl.BlockSpec((C, LANES), lambda n: (n, 0)),
        ],
        out_specs=[pl.BlockSpec((C, D), lambda n: (n, 0)), pl.BlockSpec((None, H, HEAD_DIM, HEAD_DIM), lambda n: (n, 0, 0, 0)),
                   pl.BlockSpec((None, H, C, C), lambda n: (n, 0, 0, 0))],
        out_shape=[jax.ShapeDtypeStruct((S, D), F32), jax.ShapeDtypeStruct((N, H, HEAD_DIM, HEAD_DIM), F32), jax.ShapeDtypeStruct((N, H, C, C), BF16)],
        scratch_shapes=[pltpu.VMEM((H, HEAD_DIM, HEAD_DIM), F32)],
        compiler_params=_params(1),
    )(qkvn, g_pad, g_rows, beta_pad)


def _delta_bwd(qkvn, g_pad, g_rows, beta_pad, s_all, t_all, do):
    S, W = qkvn.shape
    D = W // 3
    H = D // HEAD_DIM
    C = min(CHUNK, S)
    N = S // C

    def body(x_ref, gp_ref, gr_ref, bp_ref, sall_ref, tall_ref, do_ref, dx_ref, dgp_ref, dgr_ref, dbp_ref, dst):
        n = pl.program_id(0)

        @pl.when(n == 0)
        def _():
            dst[...] = jnp.zeros_like(dst)

        low = _tri(C)
        gc_cols = _tri_dot(low, gp_ref[...], "nn", 0)
        gc_rows = _tri_dot(gr_ref[...], low, "nt", 1)
        bcols = bp_ref[...]
        lane = lax.broadcasted_iota(jnp.int32, (1, LANES), 1)
        row8 = lax.broadcasted_iota(jnp.int32, (8, 1), 0)
        dgc_cols = jnp.zeros((C, LANES), F32)
        dgc_rows = jnp.zeros((8, C), F32)
        dbcols = jnp.zeros((C, LANES), F32)
        hs = range(H)
        t_saved = tuple(tall_ref[h].astype(F32) for h in hs)
        _, vjp = jax.vjp(
            lambda *args: _chunk_heads(*args, t_saved=t_saved),
            tuple(x_ref[:, h * HEAD_DIM : (h + 1) * HEAD_DIM] for h in hs),
            tuple(x_ref[:, D + h * HEAD_DIM : D + (h + 1) * HEAD_DIM] for h in hs),
            tuple(x_ref[:, 2 * D + h * HEAD_DIM : 2 * D + (h + 1) * HEAD_DIM] for h in hs),
            tuple(_pick_lane(gc_cols, h) for h in hs), tuple(_pick_row(gc_rows, h) for h in hs),
            tuple(_pick_lane(bcols, h) for h in hs), tuple(sall_ref[h] for h in hs),
        )
        dq, dk, dv, dgc, dgr, dbc, ds0 = vjp((tuple(do_ref[:, h * HEAD_DIM : (h + 1) * HEAD_DIM] for h in hs), tuple(dst[h] for h in hs)))
        for h in hs:
            dst[h] = ds0[h]
            dx_ref[:, h * HEAD_DIM : (h + 1) * HEAD_DIM] = dq[h]
            dx_ref[:, D + h * HEAD_DIM : D + (h + 1) * HEAD_DIM] = dk[h]
            dx_ref[:, 2 * D + h * HEAD_DIM : 2 * D + (h + 1) * HEAD_DIM] = dv[h]
            dgc_cols = dgc_cols + dgc[h] * (lane == h).astype(F32)
            dgc_rows = dgc_rows + dgr[h] * (row8 == h).astype(F32)
            dbcols = dbcols + dbc[h] * (lane == h).astype(F32)
        dgp_ref[...] = _tri_dot(low, dgc_cols, "tn", 0)
        dgr_ref[...] = _tri_dot(dgc_rows, low, "nn", 1)
        dbp_ref[...] = dbcols

    rev = lambda n: N - 1 - n
    return pl.pallas_call(
        body,
        name="gdn_delta_bwd",
        grid=(N,),
        in_specs=[
            pl.BlockSpec((C, W), lambda n: (rev(n), 0)),
            pl.BlockSpec((C, LANES), lambda n: (rev(n), 0)),
            pl.BlockSpec((None, 8, C), lambda n: (rev(n), 0, 0)),
            pl.BlockSpec((C, LANES), lambda n: (rev(n), 0)),
            pl.BlockSpec((None, H, HEAD_DIM, HEAD_DIM), lambda n: (rev(n), 0, 0, 0)),
            pl.BlockSpec((None, H, C, C), lambda n: (rev(n), 0, 0, 0)),
            pl.BlockSpec((C, D), lambda n: (rev(n), 0)),
        ],
        out_specs=[
            pl.BlockSpec((C, W), lambda n: (rev(n), 0)),
            pl.BlockSpec((C, LANES), lambda n: (rev(n), 0)),
            pl.BlockSpec((None, 8, C), lambda n: (rev(n), 0, 0)),
            pl.BlockSpec((C, LANES), lambda n: (rev(n), 0)),
        ],
        out_shape=[
            jax.ShapeDtypeStruct((S, W), F32),
            jax.ShapeDtypeStruct((S, LANES), F32),
            jax.ShapeDtypeStruct((N, 8, C), F32),
            jax.ShapeDtypeStruct((S, LANES), F32),
        ],
        scratch_shapes=[pltpu.VMEM((H, HEAD_DIM, HEAD_DIM), F32)],
        compiler_params=_params(1),
    )(qkvn, g_pad, g_rows, beta_pad, s_all, t_all, do)


def _gate_norm_head(o, z, nw):
    return o * lax.rsqrt(jnp.mean(o * o, axis=-1, keepdims=True) + RMS_EPS) * nw * _silu(z)


def _gate_norm_fwd(o, z, nw):
    S, D = o.shape
    H = D // HEAD_DIM
    ts = min(ROW_TILE, S)

    def fn(i, n, rv, hv, cv):
        ov, zv = rv
        parts = [_gate_norm_head(ov[:, h * HEAD_DIM : (h + 1) * HEAD_DIM], zv[:, h * HEAD_DIM : (h + 1) * HEAD_DIM], cv[0]) for h in range(H)]
        return (jnp.concatenate(parts, axis=1),), ()

    return _rowwise("gdn_gate_norm_fwd", fn, S, ts, rows=[o, z], consts=[nw], outs=[(D, BF16)])[0]


def _gate_norm_bwd_epi(dog, o, z, nw):
    D = dog.shape[1]
    dos, dzs, dnw = [], [], None
    for h in range(D // HEAD_DIM):
        sl = slice(h * HEAD_DIM, (h + 1) * HEAD_DIM)
        _, vjp = jax.vjp(_gate_norm_head, o[:, sl], z[:, sl], nw)
        a, b_, c_ = vjp(dog[:, sl])
        dos.append(a)
        dzs.append(b_)
        dnw = c_ if dnw is None else dnw + c_
    wide = jnp.concatenate([dnw, jnp.zeros((1, D - HEAD_DIM), F32)], axis=1) if D > HEAD_DIM else dnw
    return jnp.concatenate(dos, axis=1), jnp.concatenate(dzs, axis=1), wide


def _square_bf16(r):
    return r * r


def _mlp_ple_dw(li, dhb, dgate, dpp, xa, p, r, w2):
    dpre = _mm(f"l{li}_mlp_down_bwd", dhb, w2, "nt", [BF16], epi=lambda acc, rr: (acc * (2.0 * rr.astype(F32)),), extras=[(r, "tile")], tm=1024, tn=1024, b_outer=True)
    dw2 = _mm(f"l{li}_mlp_dw2", r, dhb, "tn", [BF16], a_fn=_square_bf16, **DW_TILES)
    dw1 = _mm(f"l{li}_mlp_dw1", xa, dpre, "tn", [BF16], out_blocks=N_DEV, **DW_TILES)
    dwg = _mm(f"l{li}_ple_dwg", xa, dgate, "tn", [BF16], **DW_TILES)
    dwp = _mm(f"l{li}_ple_dwp", p, dpp, "tn", [BF16], out_blocks=N_DEV, **DW_TILES)
    rows = lambda a: a.reshape((N_DEV, a.shape[0] // N_DEV) + a.shape[1:])
    return dpre, dw1, rows(dw2), rows(dwg), dwp


def _mlp_ple_dx(li, dh, dpre, dgate, w1, wg, after, xhat, rstd, g):
    t = _mm(f"l{li}_ple_gate_bwd", dgate, wg, "nt", [F32], epi=lambda acc, d: (acc + ALPHA * d,), extras=[(dh, "tile")], tm=1024, after=after)
    return _mm(f"l{li}_mlp_up_bwd", dpre, w1, "nt", [F32, BF16], epi=_ln_bwd_epi, extras=[(t, "tile"), (xhat, "tile"), (rstd, "rows"), (g, "row")],
               tm=256, tk=4096, accs=2)


def _local_step(x, p, tgt, W, fetch, emit):
    S, D = x.shape
    H = D // HEAD_DIM
    C = min(CHUNK, S)
    N = S // C
    lg = lambda i, j: W["ln_gain"][2 * i + j][None, :]
    lb = lambda i, j: W["ln_bias"][2 * i + j][None, :]
    G = {}

    pooled, xh0a, rs0a, x0a = _pool_fwd(x, W["pool_w"], W["pool_b"], W["pool_scale"], lg(0, 0), lb(0, 0))
    w0a = fetch("l0a", x0a)
    r0 = _mm("l0_mlp_up", x0a, w0a["mlp_w1"], "nn", [BF16], epi=lambda acc: (jnp.maximum(acc, 0.0),), tm=1024, tn=1024, b_outer=True, after=w0a.get("_after", ()))
    w0b = fetch("l0b", r0)
    gate0 = _mm("l0_ple_gate", x0a, w0b["ple_gate_w"], "nn", [F32], epi=lambda acc, bias: (acc + bias,), extras=[(W["ple_gate_b"][0:1], "row")], tm=1024)
    pp0 = _mm("l0_ple_proj", p[0], w0b["ple_proj"], "nn", [F32])
    w0c = fetch("l0c", pp0)
    ln_rows = lambda i, j, i2, j2: [(lg(i, j), "row"), (lb(i, j), "row"), (lg(i2, j2), "row"), (lb(i2, j2), "row")]
    xh0b, rs0b, x0b = _mm("l0_mlp_down", r0, w0c["mlp_w2"], "nn", [F32, (F32, LANES), BF16], a_fn=_square_bf16, tm=256, tn=D, tk=4096, epi=_res_ln_ffpe_epi,
                          extras=[(xh0a, "tile"), (gate0, "tile"), (pp0, "tile")] + ln_rows(0, 0, 0, 1), after=w0c.get("_after", ()))

    wg_ = fetch("gdn", x0b)
    qkv_pre = _mm("gdn_in_qkv", x0b, wg_["gdn_wqkv"], "nn", [F32], tm=1024, tn=1024, b_outer=True, after=wg_.get("_after", ()))
    z = _mm("gdn_in_z", x0b, wg_["gdn_wz"], "nn", [F32], tm=1024)
    ba = _mm("gdn_in_ba", x0b, wg_["gdn_wba"], "nn", [F32])
    qkvn = _conv_fwd(qkv_pre, W["gdn_conv"])
    beta_pad, g_pad = _gates_fwd(ba, W["gdn_a_log"], W["gdn_dt_bias"])
    g_rows = g_pad[:, :8].reshape(N, C, 8).transpose(0, 2, 1)
    o, s_all, t_all = _delta_fwd(qkvn, g_pad, g_rows, beta_pad)
    og = _gate_norm_fwd(o, z, W["gdn_norm_w"])
    wg_.update(fetch("gdo", og))
    xh1a, rs1a, x1a = _mm("gdn_out", og, wg_["gdn_w_out"], "nn", [F32, (F32, LANES), BF16], tm=512, tn=D, epi=_res_ln_epi,
                          extras=[(xh0b, "tile")] + ln_rows(0, 1, 1, 0), after=wg_.get("_after", ()))
    w1_ = fetch("l1", x1a)
    r1 = _mm("l1_mlp_up", x1a, w1_["mlp_w1"], "nn", [BF16], epi=lambda acc: (jnp.maximum(acc, 0.0),), tm=1024, tn=1024, b_outer=True)
    gate1 = _mm("l1_ple_gate", x1a, w1_["ple_gate_w"], "nn", [F32], epi=lambda acc, bias: (acc + bias,), extras=[(W["ple_gate_b"][1:2], "row")], tm=1024)
    pp1 = _mm("l1_ple_proj", p[1], w1_["ple_proj"], "nn", [F32])
    dh1b, dh1b_b, dgate1, dpp1, loss_cols, dg11, db11, dbg_1 = _mm(
        "l1_mlp_down", r1, w1_["mlp_w2"], "nn", [F32, BF16, BF16, BF16], a_fn=_square_bf16, tm=256, tn=D, tk=4096, epi=_final_ln_loss_epi,
        extras=[(xh1a, "tile"), (gate1, "tile"), (pp1, "tile"), (tgt, "tile")] + ln_rows(1, 0, 1, 1), accs=4)

    dpre1, dw1_1, dw2_1, dwg_1, dwp_1 = _mlp_ple_dw(1, dh1b_b, dgate1, dpp1, x1a, p[1], r1, w1_["mlp_w2"])
    tok = emit("l1", {"mlp_w1": dw1_1, "mlp_w2": dw2_1, "ple_gate_w": dwg_1, "ple_proj": dwp_1})
    dh1a, dh1a_b, dg10, db10 = _mlp_ple_dx(1, dh1b, dpre1, dgate1, w1_["mlp_w1"], w1_["ple_gate_w"], [tok], xh1a, rs1a, lg(1, 0))
    do, dz, dnw = _mm("gdn_out_bwd", dh1a_b, wg_["gdn_w_out"], "nt", [F32, BF16], tm=256, tn=D, tk=D, epi=_gate_norm_bwd_epi,
                      extras=[(o, "tile"), (z, "tile"), (W["gdn_norm_w"], "whole")], accs=1)
    dnw = dnw[:, :HEAD_DIM]
    dw_out = _mm("gdn_dw_out", og, dh1a_b, "tn", [BF16], **DW_TILES)
    dw_out = dw_out.reshape((N_DEV, dw_out.shape[0] // N_DEV) + dw_out.shape[1:])
    dqkvn, dg_col, dg_row, dbeta = _delta_bwd(qkvn, g_pad, g_rows, beta_pad, s_all, t_all, do)
    dg_all = dg_col + jnp.pad(dg_row.transpose(0, 2, 1).reshape(S, 8), ((0, 0), (0, LANES - 8)))
    dba, dalog, ddt = _gates_bwd(ba, W["gdn_a_log"], W["gdn_dt_bias"], dbeta, dg_all, H)
    dqkv, dconv = _conv_bwd(qkv_pre, W["gdn_conv"], dqkvn)
    dwqkv = _mm("gdn_dwqkv", x0b, dqkv, "tn", [F32], **DW_TILES)
    dwz = _mm("gdn_dwz", x0b, dz, "tn", [F32], **DW_TILES)
    dwba = _mm("gdn_dwba", x0b, dba, "tn", [F32], **DW_TILES)
    dw_in = jnp.concatenate([dwqkv, dwz, dwba[:, :H], dwba[:, LANES : LANES + H]], axis=1)
    tok = emit("gdn", {"gdn_w_in": _split_blocks("gdn_w_in", dw_in).astype(BF16), "gdn_w_out": dw_out})
    t = _mm("gdn_in_ba_bwd", dba, wg_["gdn_wba"], "nt", [F32], epi=lambda acc, d: (acc + ALPHA * d,), extras=[(dh1a, "tile")], after=[tok])
    t = _mm("gdn_in_z_bwd", dz, wg_["gdn_wz"], "nt", [F32], epi=lambda acc, d: (acc + d,), extras=[(t, "tile")], tm=1024)
    dh0b, dh0b_b, dgate0, dpp0, dg01, db01, dbg_0 = _mm(
        "gdn_in_qkv_bwd", dqkv, wg_["gdn_wqkv"], "nt", [F32, BF16, BF16, BF16], epi=_ln_bwd_ple_epi,
        extras=[(t, "tile"), (xh0b, "tile"), (rs0b, "rows"), (gate0, "tile"), (pp0, "tile"), (lg(0, 1), "row")], tm=256, tk=3072, accs=3)

    dpre0, dw1_0, dw2_0, dwg_0, dwp_0 = _mlp_ple_dw(0, dh0b_b, dgate0, dpp0, x0a, p[0], r0, w0c["mlp_w2"])
    tok = emit("l0", {"mlp_w1": dw1_0, "mlp_w2": dw2_0, "ple_gate_w": dwg_0, "ple_proj": dwp_0})
    dh0a, _, dg00, db00 = _mlp_ple_dx(0, dh0b, dpre0, dgate0, w0a["mlp_w1"], w0b["ple_gate_w"], [tok], xh0a, rs0a, lg(0, 0))
    grad_x, dyp, dscale, dpb = _pool_bwd(dh0a, pooled, W["pool_w"], W["pool_b"], W["pool_scale"])
    G["pool_w"] = _pool_dw(pooled, dyp)

    G["ln_gain"] = jnp.concatenate([dg00, dg01, dg10, dg11], axis=0)
    G["ln_bias"] = jnp.concatenate([db00, db01, db10, db11], axis=0)
    G["pool_b"] = dpb
    G["pool_scale"] = dscale
    G["gdn_conv"] = dconv
    G["gdn_a_log"] = dalog[:, :H]
    G["gdn_dt_bias"] = ddt[:, :H]
    G["gdn_norm_w"] = dnw
    G["ple_gate_b"] = jnp.concatenate([dbg_0, dbg_1], axis=0)
    return loss_cols, grad_x, G


_HBM = pl.BlockSpec(memory_space=pltpu.HBM)


def _all_gather(name, shards):
    T = len(shards)

    def body(*refs):
        ins, outs = refs[:T], refs[T : 2 * T]
        send_sems, recv_sems, local_sems = refs[2 * T :]
        x, y, c = lax.axis_index("x"), lax.axis_index("y"), lax.axis_index("c")
        me, sibling = (x, y, c), (x, y, 1 - c)
        chips = [(1 - x, y), (x, 1 - y), (1 - x, 1 - y)]

        def blk(t, px, py, pc):
            return outs[t].at[4 * px + 2 * py + pc]

        def copy(t, k, block, to, src=None):
            return pltpu.make_async_remote_copy(
                src_ref=blk(t, *block) if src is None else src, dst_ref=blk(t, *block),
                send_sem=send_sems.at[t, k], recv_sem=recv_sems.at[t, k], device_id=to, device_id_type=MESH,
            )

        mine = [pltpu.make_async_copy(ins[t], blk(t, *me), local_sems.at[t]) for t in range(T)]
        for cp in mine:
            cp.start()
        first = []
        for t in range(T):
            first.append(copy(t, 0, me, sibling, src=ins[t]))
            first += [copy(t, 1 + j, me, (*chip, c), src=ins[t]) for j, chip in enumerate(chips)]
        for cp in first:
            cp.start()
        passed = []
        for j, chip in enumerate(chips):
            for t in range(T):
                copy(t, 1 + j, (*chip, c), me).wait_recv()
                fw = copy(t, 4 + j, (*chip, c), sibling)
                fw.start()
                passed.append(fw)
        for t in range(T):
            copy(t, 0, sibling, me).wait_recv()
            for j, chip in enumerate(chips):
                copy(t, 4 + j, (*chip, 1 - c), me).wait_recv()
        for cp in first + passed:
            cp.wait_send()
        for cp in mine:
            cp.wait()

    return pl.pallas_call(
        body,
        name=name,
        in_specs=[_HBM] * T,
        out_specs=[_HBM] * T,
        out_shape=[jax.ShapeDtypeStruct((N_DEV,) + s.shape, s.dtype) for s in shards],
        scratch_shapes=[pltpu.SemaphoreType.DMA((T, 7)), pltpu.SemaphoreType.DMA((T, 7)), pltpu.SemaphoreType.DMA((T,))],
    )(*shards)


def _exchange(name, blocks):
    def body(g_ref, o_ref, send_sems, recv_sems, local_sem):
        x, y, c = lax.axis_index("x"), lax.axis_index("y"), lax.axis_index("c")
        own = pltpu.make_async_copy(g_ref.at[4 * x + 2 * y + c], o_ref.at[N_DEV - 1], local_sem)
        own.start()
        copies = []
        for rel in range(1, N_DEV):
            px = 1 - x if rel & 4 else x
            py = 1 - y if rel & 2 else y
            pc = 1 - c if rel & 1 else c
            copies.append(
                pltpu.make_async_remote_copy(
                    src_ref=g_ref.at[4 * px + 2 * py + pc], dst_ref=o_ref.at[rel - 1],
                    send_sem=send_sems.at[rel - 1], recv_sem=recv_sems.at[rel - 1], device_id=(px, py, pc), device_id_type=MESH,
                )
            )
        for cp in copies:
            cp.start()
        for cp in copies:
            cp.wait_recv()
        for cp in copies:
            cp.wait_send()
        own.wait()

    return pl.pallas_call(
        body,
        name=name,
        in_specs=[_HBM],
        out_specs=_HBM,
        out_shape=jax.ShapeDtypeStruct(blocks.shape, blocks.dtype),
        scratch_shapes=[pltpu.SemaphoreType.DMA((N_DEV - 1,)), pltpu.SemaphoreType.DMA((N_DEV - 1,)), pltpu.SemaphoreType.DMA],
    )(blocks)


_SEM = pl.BlockSpec(memory_space=pltpu.SEMAPHORE)
_ANY = pl.BlockSpec(memory_space=pl.ANY)
_DATAFLOW = pltpu.SideEffectType.DATAFLOW_SIDE_EFFECTING
N_PEERS = N_DEV - 1


def _peer(rel, x, y, c):
    return (1 - x if rel & 4 else x, 1 - y if rel & 2 else y, 1 - c if rel & 1 else c)


def _send_start(name, srcs, lands, gather, after):
    T = len(srcs)

    def body(*refs):
        src_refs, land_refs = refs[:T], refs[T : 2 * T]
        send_sems, recv_sems = refs[2 * T + 1], refs[2 * T + 2]
        token = refs[-1]
        x, y, c = lax.axis_index("x"), lax.axis_index("y"), lax.axis_index("c")
        for t in range(T):
            for rel in range(1, N_DEV):
                px, py, pc = _peer(rel, x, y, c)
                pltpu.make_async_remote_copy(
                    src_ref=src_refs[t] if gather else src_refs[t].at[4 * px + 2 * py + pc],
                    dst_ref=land_refs[t].at[4 * x + 2 * y + c] if gather else land_refs[t].at[rel - 1],
                    send_sem=send_sems.at[t * N_PEERS + rel - 1], recv_sem=recv_sems.at[t * N_PEERS + rel - 1], device_id=(px, py, pc), device_id_type=MESH,
                ).start()
        token[...] = jnp.zeros_like(token)

    hbm = lambda a: pltpu.HBM(a.shape, a.dtype)
    return pl.pallas_call(
        body,
        name=name,
        out_shape=(pltpu.SemaphoreType.DMA((T * N_PEERS,)), pltpu.SemaphoreType.DMA((T * N_PEERS,)), *[hbm(a) for a in srcs],
                   *[hbm(a) for a in lands], jax.ShapeDtypeStruct((8, LANES), F32)),
        in_specs=(_HBM,) * (2 * T) + (_ANY,),
        out_specs=(_SEM, _SEM) + (_HBM,) * (2 * T) + (pl.BlockSpec(memory_space=pltpu.VMEM),),
        input_output_aliases={t: 2 + t for t in range(2 * T)},
        compiler_params=pltpu.CompilerParams(has_side_effects=_DATAFLOW),
    )(*[pltpu.with_memory_space_constraint(a, pltpu.HBM) for a in list(srcs) + list(lands)], after)


def _send_wait(name, started, after, gather):
    T = (len(started) - 3) // 2
    send_sems, recv_sems, token = started[0], started[1], started[-1]
    thru = started[2:-1]

    def body(*refs):
        src_refs, land_refs = refs[:T], refs[T : 2 * T]
        send_sems, recv_sems = refs[2 * T], refs[2 * T + 1]
        x, y, c = lax.axis_index("x"), lax.axis_index("y"), lax.axis_index("c")
        for t in range(T):
            for rel in range(1, N_DEV):
                cp = pltpu.make_async_remote_copy(
                    src_ref=src_refs[t] if gather else src_refs[t].at[0], dst_ref=land_refs[t].at[0],
                    send_sem=send_sems.at[t * N_PEERS + rel - 1], recv_sem=recv_sems.at[t * N_PEERS + rel - 1], device_id=_peer(rel, x, y, c), device_id_type=MESH,
                )
                cp.wait_send()
                cp.wait_recv()

    outs = pl.pallas_call(
        body,
        name=name,
        out_shape=tuple(pltpu.HBM(a.shape, a.dtype) for a in thru),
        in_specs=(_HBM,) * (2 * T) + (_SEM, _SEM, _ANY),
        out_specs=(_HBM,) * (2 * T),
        input_output_aliases={t: t for t in range(2 * T)},
        compiler_params=pltpu.CompilerParams(has_side_effects=_DATAFLOW),
    )(*thru, send_sems, recv_sems, after)
    return list(outs[:T]), list(outs[T:])


def _sum_blocks(name, parts, tr):
    _, R, Cw = parts[0].shape
    tr = tr if R % tr == 0 else R

    def body(*refs):
        acc = None
        for p_ref in refs[:-1]:
            for d in range(p_ref.shape[0]):
                v = p_ref[d].astype(F32)
                acc = v if acc is None else acc + v
        refs[-1][...] = acc

    return pl.pallas_call(
        body,
        name=name,
        grid=(R // tr,),
        in_specs=[pl.BlockSpec((a.shape[0], tr, Cw), lambda i: (0, i, 0)) for a in parts],
        out_specs=pl.BlockSpec((tr, Cw), lambda i: (i, 0)),
        out_shape=jax.ShapeDtypeStruct((R, Cw), F32),
        compiler_params=_params(1),
    )(*parts)


def _adamw(name, w, g, m, v):
    shape = w.shape
    cols = shape[-1]
    rows = w.size // cols
    tr = rows if rows <= 512 else 512
    assert rows % tr == 0
    w2, g2, m2, v2 = (a.reshape(rows, cols) for a in (w, g, m, v))

    def body(w_ref, g_ref, m_ref, v_ref, d_ref, mo_ref, vo_ref):
        gv = g_ref[...]
        mn = ADAM_B1 * m_ref[...] + (1.0 - ADAM_B1) * gv
        vn = ADAM_B2 * v_ref[...] + (1.0 - ADAM_B2) * jnp.square(gv)
        m_hat = mn / (1.0 - ADAM_B1**ADAM_STEP)
        v_hat = vn / (1.0 - ADAM_B2**ADAM_STEP)
        d_ref[...] = -ADAM_LR * (m_hat / (jnp.sqrt(v_hat) + ADAM_EPS) + ADAM_WD * w_ref[...])
        mo_ref[...] = mn
        vo_ref[...] = vn

    spec = pl.BlockSpec((tr, cols), lambda i: (i, 0))
    d, mn, vn = pl.pallas_call(
        body,
        name=name,
        grid=(rows // tr,),
        in_specs=[spec] * 4,
        out_specs=[spec] * 3,
        out_shape=[jax.ShapeDtypeStruct((rows, cols), F32)] * 3,
        compiler_params=_params(1),
    )(w2, g2, m2, v2)
    return d.reshape(shape), mn.reshape(shape), vn.reshape(shape)


SMALL_SHARDED = ("ln_gain", "ln_bias", "pool_b", "gdn_conv")
SMALL_REPLICATED = ("pool_scale", "gdn_a_log", "gdn_dt_bias", "gdn_norm_w", "ple_gate_b")
WEIGHTS = ("ln_gain", "ln_bias", "pool_w", "pool_b", "pool_scale", "gdn_w_in", "gdn_conv", "gdn_a_log", "gdn_dt_bias",
           "gdn_norm_w", "gdn_w_out", "mlp_w1", "mlp_w2", "ple_gate_w", "ple_gate_b", "ple_proj")
BIG_AXIS = {"gdn_w_in": 1, "gdn_w_out": 0, "mlp_w1": 1, "mlp_w2": 0, "ple_gate_w": 0, "ple_proj": 1, "pool_w": 1}
GATHER_GROUPS = {
    "l0a": (("mlp_w1", 0),),
    "l0b": (("ple_gate_w", 0), ("ple_proj", 0)),
    "l0c": (("mlp_w2", 0),),
    "gdn": (("gdn_w_in", 0),),
    "gdo": (("gdn_w_out", 0),),
    "l1": (("mlp_w1", 1), ("mlp_w2", 1), ("ple_gate_w", 1), ("ple_proj", 1)),
}
GATHER_AFTER = {"l0b": "l0a", "l0c": "l0b", "gdn": "l0c", "gdo": "gdn", "l1": "gdo"}
GRAD_GROUPS = {
    "l1": (("mlp_w1", 1), ("mlp_w2", 1), ("ple_gate_w", 1), ("ple_proj", 1)),
    "gdn": (("gdn_w_in", 0), ("gdn_w_out", 0)),
    "l0": (("mlp_w1", 0), ("mlp_w2", 0), ("ple_gate_w", 0), ("ple_proj", 0)),
}
PACK_PART_ALIGN = 16
SUM_TILE = 128


def _part_rows(a, width):
    rows = a.size // width
    return rows + (-rows) % PACK_PART_ALIGN


def _pack_rows(parts, width, dtype, align):
    padded = []
    for a in parts:
        a2 = a.reshape(-1, width).astype(dtype)
        padded.append(jnp.pad(a2, ((0, _part_rows(a, width) - a2.shape[0]), (0, 0))))
    flat = jnp.concatenate(padded, axis=0)
    return jnp.pad(flat, ((0, (-flat.shape[0]) % align), (0, 0)))


def _pack_blocks(parts, width, dtype, align):
    padded = []
    for a in parts:
        a2 = a.reshape(a.shape[0], -1, width).astype(dtype)
        padded.append(jnp.pad(a2, ((0, 0), (0, _part_rows(a[0], width) - a2.shape[1]), (0, 0))))
    flat = jnp.concatenate(padded, axis=1)
    return jnp.pad(flat, ((0, 0), (0, (-flat.shape[1]) % align), (0, 0)))


def _unpack_rows(packed, shapes, width):
    out, off = [], 0
    for shp in shapes:
        size = 1
        for d in shp:
            size *= d
        out.append(packed[..., off : off + size // width, :].reshape(packed.shape[:-2] + tuple(shp)))
        off += size // width + (-(size // width)) % PACK_PART_ALIGN
    return out


def _split_blocks(name, full):
    ax = BIG_AXIS[name]
    shp = full.shape
    a = full.reshape(shp[:ax] + (N_DEV, shp[ax] // N_DEV) + shp[ax + 1 :])
    return jnp.moveaxis(a, ax, 0)


def _join_blocks(name, blocks):
    ax = BIG_AXIS[name]
    a = jnp.moveaxis(blocks, 0, ax)
    shp = a.shape
    return a.reshape(shp[:ax] + (shp[ax] * shp[ax + 1],) + shp[ax + 2 :])


def _pack_small(parts):
    flat = jnp.concatenate([jnp.pad(a.reshape(-1), (0, (-a.size) % LANES)) for a in parts])
    rows = flat.size // LANES
    return jnp.pad(flat.reshape(rows, LANES), ((0, (-rows) % 8), (0, 0)))


def _unpack_small(packed, shapes):
    flat = packed.reshape(packed.shape[:-2] + (-1,))
    out, off = [], 0
    for shp in shapes:
        size = 1
        for s in shp:
            size *= s
        out.append(flat[..., off : off + size].reshape(flat.shape[:-1] + tuple(shp)))
        off += size + (-size) % LANES
    return out


def _split_w_in(w_in, D, H):
    pad = lambda a: jnp.pad(a, ((0, 0), (0, LANES - H)))
    return w_in[:, : 3 * D], w_in[:, 3 * D : 4 * D], jnp.concatenate([pad(w_in[:, 4 * D : 4 * D + H]), pad(w_in[:, 4 * D + H :])], axis=1)


def kernel(x, p, ln_gain, ln_bias, pool_w, pool_b, pool_scale, gdn_w_in, gdn_conv, gdn_a_log, gdn_dt_bias, gdn_norm_w, gdn_w_out, mlp_w1, mlp_w2, ple_gate_w, ple_gate_b, ple_proj, loss_target, m_ln_gain, m_ln_bias, m_pool_w, m_pool_b, m_pool_scale, m_gdn_w_in, m_gdn_conv, m_gdn_a_log, m_gdn_dt_bias, m_gdn_norm_w, m_gdn_w_out, m_mlp_w1, m_mlp_w2, m_ple_gate_w, m_ple_gate_b, m_ple_proj, v_ln_gain, v_ln_bias, v_pool_w, v_pool_b, v_pool_scale, v_gdn_w_in, v_gdn_conv, v_gdn_a_log, v_gdn_dt_bias, v_gdn_norm_w, v_gdn_w_out, v_mlp_w1, v_mlp_w2, v_ple_gate_w, v_ple_gate_b, v_ple_proj):
    w_sh = dict(ln_gain=ln_gain, ln_bias=ln_bias, pool_w=pool_w, pool_b=pool_b, pool_scale=pool_scale, gdn_w_in=gdn_w_in,
                gdn_conv=gdn_conv, gdn_a_log=gdn_a_log, gdn_dt_bias=gdn_dt_bias, gdn_norm_w=gdn_norm_w, gdn_w_out=gdn_w_out,
                mlp_w1=mlp_w1, mlp_w2=mlp_w2, ple_gate_w=ple_gate_w, ple_gate_b=ple_gate_b, ple_proj=ple_proj)
    m_sh = dict(ln_gain=m_ln_gain, ln_bias=m_ln_bias, pool_w=m_pool_w, pool_b=m_pool_b, pool_scale=m_pool_scale, gdn_w_in=m_gdn_w_in,
                gdn_conv=m_gdn_conv, gdn_a_log=m_gdn_a_log, gdn_dt_bias=m_gdn_dt_bias, gdn_norm_w=m_gdn_norm_w, gdn_w_out=m_gdn_w_out,
                mlp_w1=m_mlp_w1, mlp_w2=m_mlp_w2, ple_gate_w=m_ple_gate_w, ple_gate_b=m_ple_gate_b, ple_proj=m_ple_proj)
    v_sh = dict(ln_gain=v_ln_gain, ln_bias=v_ln_bias, pool_w=v_pool_w, pool_b=v_pool_b, pool_scale=v_pool_scale, gdn_w_in=v_gdn_w_in,
                gdn_conv=v_gdn_conv, gdn_a_log=v_gdn_a_log, gdn_dt_bias=v_gdn_dt_bias, gdn_norm_w=v_gdn_norm_w, gdn_w_out=v_gdn_w_out,
                mlp_w1=v_mlp_w1, mlp_w2=v_mlp_w2, ple_gate_w=v_ple_gate_w, ple_gate_b=v_ple_gate_b, ple_proj=v_ple_proj)
    xs, tg = x[0], loss_target[0]
    ps = p[:, 0]
    S, D = xs.shape
    H = D // HEAD_DIM
    me = 4 * lax.axis_index("x") + 2 * lax.axis_index("y") + lax.axis_index("c")
    layer = lambda n, l: (w_sh[n][0] if n in ("gdn_w_in", "gdn_w_out") else w_sh[n][l])

    pool_packed = _pack_rows([w_sh["pool_w"][0]], D, BF16, PACK_PART_ALIGN)
    small_packed = _pack_small([w_sh[n] for n in SMALL_SHARDED])
    pool_gathered, small_gathered = _all_gather("gather_first", [pool_packed, small_packed])
    W = {"pool_w": _join_blocks("pool_w", _unpack_rows(pool_gathered, [w_sh["pool_w"][0].shape], D)[0])}

    started = {}

    def start(g, after):
        srcs = [layer(n, l).astype(BF16) for n, l in GATHER_GROUPS[g]]
        started[g] = tuple(_send_start(f"gather_{g}_start", srcs, [lax.empty((N_DEV,) + a.shape, BF16) for a in srcs], True, after))
        return started[g][-1]

    first_token = start("l0a", small_gathered)
    smalls = _unpack_small(small_gathered, [w_sh[n].shape for n in SMALL_SHARDED])
    for n, a in zip(SMALL_SHARDED, smalls):
        W[n] = jnp.moveaxis(a, 0, -2).reshape(a.shape[1:-1] + (N_DEV * a.shape[-1],))
    W["ln_gain"] = W["ln_gain"].reshape(2 * DEPTH, D)
    W["ln_bias"] = W["ln_bias"].reshape(2 * DEPTH, D)
    W["pool_b"] = W["pool_b"].reshape(1, D) + first_token[0:1, 0:1]
    W["gdn_conv"] = W["gdn_conv"][0]
    W["pool_scale"] = pool_scale
    W["ple_gate_b"] = ple_gate_b
    W["gdn_norm_w"] = gdn_norm_w
    W["gdn_a_log"] = jnp.pad(gdn_a_log, ((0, 0), (0, LANES - H)))
    W["gdn_dt_bias"] = jnp.pad(gdn_dt_bias, ((0, 0), (0, LANES - H)))

    def fetch(g, after):
        members = GATHER_GROUPS[g]
        srcs, lands = _send_wait(f"gather_{g}_wait", started[g], after, True)
        tokens = [start(nxt, lands[0]) for nxt, prev in GATHER_AFTER.items() if prev == g]
        out = {n: _join_blocks(n, lax.dynamic_update_index_in_dim(land, src, me, 0)) for (n, _), src, land in zip(members, srcs, lands)}
        if "gdn_w_in" in out:
            out["gdn_wqkv"], out["gdn_wz"], out["gdn_wba"] = _split_w_in(out.pop("gdn_w_in"), D, H)
        out["_after"] = tokens
        return out

    sent = {}

    def emit(g, grads):
        srcs = [grads[n] for n, _ in GRAD_GROUPS[g]]
        lands = [lax.empty((N_PEERS,) + a.shape[1:], BF16) for a in srcs]
        sent[g] = tuple(_send_start(f"grads_{g}_start", srcs, lands, False, srcs[0]))
        return sent[g][-1]

    loss_cols, grad_x, G = _local_step(xs, ps, tg, W, fetch, emit)
    loss = lax.psum(0.5 * jnp.sum(loss_cols) / D, MESH_AXES)

    pool_src = _pack_blocks([_split_blocks("pool_w", G["pool_w"])], D, BF16, PACK_PART_ALIGN)
    pool_sum = _sum_blocks("sum_pool_grads", [_exchange("exchange_pool_grads", pool_src)], SUM_TILE)
    grads = {"pool_w": _unpack_rows(pool_sum, [w_sh["pool_w"][0].shape], D)[0].reshape(w_sh["pool_w"].shape)}
    small_names = SMALL_SHARDED + SMALL_REPLICATED
    gs_packed = _pack_small([G[n] for n in small_names])
    (gs_all,) = _all_gather("gather_small_grads", [gs_packed])
    gs_sum = _sum_blocks("sum_small_grads", [gs_all], SUM_TILE)
    for n, a in zip(small_names, _unpack_small(gs_sum, [G[n].shape for n in small_names])):
        if n in SMALL_SHARDED:
            width = w_sh[n].shape[-1]
            a = a.reshape(w_sh[n].shape[:-1] + (N_DEV * width,))
            a = lax.dynamic_slice_in_dim(a, me * width, width, axis=a.ndim - 1)
        grads[n] = a.reshape(w_sh[n].shape)

    per_layer = {}
    for g, members in GRAD_GROUPS.items():
        srcs, lands = _send_wait(f"grads_{g}_wait", sent[g], grad_x, False)
        for (n, l), src, land in zip(members, srcs, lands):
            own = lax.dynamic_index_in_dim(src, me, 0, keepdims=True)
            as3d = lambda a: a.reshape(a.shape[0], -1, a.shape[-1])
            per_layer[(n, l)] = _sum_blocks(f"sum_grads_{n}_{l}", [as3d(land), as3d(own)], SUM_TILE).reshape(layer(n, l).shape)
    for n in ("gdn_w_in", "gdn_w_out"):
        grads[n] = per_layer[(n, 0)][None]
    for n in ("mlp_w1", "mlp_w2", "ple_gate_w", "ple_proj"):
        grads[n] = jnp.stack([per_layer[(n, 0)], per_layer[(n, 1)]])

    deltas, new_m, new_v = {}, {}, {}
    for n in WEIGHTS:
        deltas[n], new_m[n], new_v[n] = _adamw(f"adamw_{n}", w_sh[n], grads[n], m_sh[n], v_sh[n])
    return (loss, grad_x[None], *[grads[n] for n in WEIGHTS], *[deltas[n] for n in WEIGHTS],
            *[new_m[n] for n in WEIGHTS], *[new_v[n] for n in WEIGHTS])
```

```python
import functools

import jax
import jax.numpy as jnp
from jax import lax
from jax.experimental import pallas as pl
from jax.experimental.pallas import tpu as pltpu

F32 = jnp.float32
BF16 = jnp.bfloat16
MESH_AXES = ("x", "y", "c")
N_DEV = 8
MESH = pl.DeviceIdType.MESH

DEPTH = 2
ALPHA = (2.0 * DEPTH) ** 0.25
LN_EPS = 1e-5
RMS_EPS = 1e-6
L2_EPS = 1e-6
HEAD_DIM = 128
CONV_WIDTH = 4
POOL_WINDOWS = (2, 4, 8, 16)
POOL_HALO = 16
CONV_HALO = 8
LANES = 128
ADAM_LR = 0.001
ADAM_B1 = 0.9
ADAM_B2 = 0.999
ADAM_EPS = 1e-08
ADAM_WD = 0.01
ADAM_STEP = 10

VMEM_LIMIT = 56 * 1024 * 1024
ROW_TILE = 512
CONV_TILE = 256
CHUNK = 128
MM_TM, MM_TN, MM_TK = 512, 1024, 1024
DW_TILES = dict(tm=512, tn=512, tk=8192, b_outer=True)

_DIMS = {
    "nn": (((1,), (0,)), ((), ())),
    "nt": (((1,), (1,)), ((), ())),
    "tn": (((0,), (0,)), ((), ())),
}


def _params(n_axes):
    return pltpu.CompilerParams(dimension_semantics=("arbitrary",) * n_axes, vmem_limit_bytes=VMEM_LIMIT)


def _fit(tile, n):
    tile = min(tile, n)
    while n % tile:
        tile //= 2
    return tile


def _mm(name, a, b, mode, out_dtypes, epi=None, extras=(), a_fn=None, tm=None, tn=None, tk=None, b_outer=False, after=(), out_blocks=1, accs=0):
    if mode == "tn":
        K, M = a.shape
    else:
        M, K = a.shape
    N = b.shape[0] if mode == "nt" else b.shape[1]
    tm, tn, tk = _fit(tm or MM_TM, M), (N // out_blocks if out_blocks > 1 else N if accs else _fit(tn or MM_TN, N)), _fit(tk or MM_TK, K)
    nk = K // tk

    def at(f):
        return (lambda j, i, k: f(i, j, k)) if b_outer else f

    a_spec = pl.BlockSpec((tk, tm), at(lambda i, j, k: (k, i))) if mode == "tn" else pl.BlockSpec((tm, tk), at(lambda i, j, k: (i, k)))
    b_spec = pl.BlockSpec((tn, tk), at(lambda i, j, k: (j, k))) if mode == "nt" else pl.BlockSpec((tk, tn), at(lambda i, j, k: (k, j)))
    ex_spec = {"tile": pl.BlockSpec((tm, tn), at(lambda i, j, k: (i, j))), "row": pl.BlockSpec((1, tn), at(lambda i, j, k: (0, j))),
               "rows": pl.BlockSpec((tm, LANES), at(lambda i, j, k: (i, 0)))}
    ex_specs = [pl.BlockSpec(e.shape, lambda i, j, k: (0, 0)) if kind == "whole" else ex_spec[kind] for e, kind in extras]
    assert accs == 0 or (tn == N and nk == 1 and not b_outer), name
    n_ex, n_out, n_after = len(extras), len(out_dtypes), len(after)

    def body(*refs):
        a_ref, b_ref = refs[0], refs[1]
        ex_refs = refs[2 : 2 + n_ex]
        out_refs = refs[2 + n_ex + n_after : 2 + n_ex + n_after + n_out]
        av = a_ref[...]
        if a_fn is not None:
            av = a_fn(av)
        part = lax.dot_general(av.astype(BF16), b_ref[...].astype(BF16), _DIMS[mode], preferred_element_type=F32)

        def finish(res):
            vals = epi(res, *[e[...] for e in ex_refs]) if epi is not None else (res,)
            for o_ref, v in zip(out_refs, vals[:n_out]):
                o_ref[...] = v.astype(o_ref.dtype)
            for a_ref, v in zip(refs[2 + n_ex + n_after + n_out :], vals[n_out:]):

                @pl.when(pl.program_id(0) == 0)
                def _(a_ref=a_ref, v=v):
                    a_ref[...] = v

                @pl.when(pl.program_id(0) > 0)
                def _(a_ref=a_ref, v=v):
                    a_ref[...] += v

        if nk == 1:
            finish(part)
        else:
            acc = refs[-1]
            k = pl.program_id(2)

            @pl.when(k == 0)
            def _():
                acc[...] = part

            @pl.when(k > 0)
            def _():
                acc[...] += part

            @pl.when(k == nk - 1)
            def _():
                finish(acc[...])

    outs = pl.pallas_call(
        body,
        name=name,
        grid=(N // tn, M // tm, nk) if b_outer else (M // tm, N // tn, nk),
        in_specs=[a_spec, b_spec] + ex_specs + [pl.BlockSpec(memory_space=pl.ANY)] * n_after,
        out_specs=[pl.BlockSpec((tm, LANES), at(lambda i, j, k: (i, 0))) if isinstance(dt, tuple)
                   else pl.BlockSpec((tm, tn), at(lambda i, j, k: (i, j))) if out_blocks == 1
                   else pl.BlockSpec((None, tm, tn), at(lambda i, j, k: (j, i, 0))) for dt in out_dtypes]
        + [pl.BlockSpec((1, N), lambda i, j, k: (0, 0))] * accs,
        out_shape=[jax.ShapeDtypeStruct((M, LANES), dt[0]) if isinstance(dt, tuple)
                   else jax.ShapeDtypeStruct((M, N) if out_blocks == 1 else (out_blocks, M, tn), dt) for dt in out_dtypes]
        + [jax.ShapeDtypeStruct((1, N), F32)] * accs,
        scratch_shapes=[pltpu.VMEM((tm, tn), F32)] if nk > 1 else [],
        compiler_params=_params(3),
    )(a, b, *[e for e, _ in extras], *after)
    return outs[0] if n_out + accs == 1 else outs


def _rowwise(name, fn, S, ts, rows=(), halos=(), consts=(), outs=(), accs=()):
    ts = min(ts, S)
    assert S % ts == 0
    n = S // ts
    in_specs = [pl.BlockSpec((ts, a.shape[1]), lambda i: (i, 0)) for a in rows]
    for a, kind, hr in halos:
        r, nb = ts // hr, S // hr
        if kind == "prev":
            in_specs.append(pl.BlockSpec((hr, a.shape[1]), lambda i, r=r: (jnp.maximum(i * r - 1, 0), 0)))
        else:
            in_specs.append(pl.BlockSpec((hr, a.shape[1]), lambda i, r=r, nb=nb: (jnp.minimum((i + 1) * r, nb - 1), 0)))
    in_specs += [pl.BlockSpec(a.shape, lambda i, nd=a.ndim: (0,) * nd) for a in consts]
    out_specs = [pl.BlockSpec((ts, w), lambda i: (i, 0)) for w, _ in outs]
    out_specs += [pl.BlockSpec((r, w), lambda i: (0, 0)) for r, w in accs]
    out_shape = [jax.ShapeDtypeStruct((S, w), dt) for w, dt in outs]
    out_shape += [jax.ShapeDtypeStruct((r, w), F32) for r, w in accs]
    nr, nh, nc, no = len(rows), len(halos), len(consts), len(outs)

    def body(*refs):
        i = pl.program_id(0)
        rv = [r[...] for r in refs[:nr]]
        hv = [r[...] for r in refs[nr : nr + nh]]
        cv = [r[...] for r in refs[nr + nh : nr + nh + nc]]
        o_refs = refs[nr + nh + nc : nr + nh + nc + no]
        a_refs = refs[nr + nh + nc + no :]
        ovals, avals = fn(i, n, rv, hv, cv)
        for o_ref, v in zip(o_refs, ovals):
            o_ref[...] = v.astype(o_ref.dtype)
        for a_ref, v in zip(a_refs, avals):

            @pl.when(i == 0)
            def _(a_ref=a_ref, v=v):
                a_ref[...] = v

            @pl.when(i > 0)
            def _(a_ref=a_ref, v=v):
                a_ref[...] += v

    res = pl.pallas_call(
        body,
        name=name,
        grid=(n,),
        in_specs=in_specs,
        out_specs=out_specs,
        out_shape=out_shape,
        compiler_params=_params(1),
    )(*rows, *[h[0] for h in halos], *consts)
    return list(res)


def _ln(h, g, b):
    mu = jnp.mean(h, axis=-1, keepdims=True)
    d = h - mu
    var = jnp.mean(d * d, axis=-1, keepdims=True)
    rstd = lax.rsqrt(var + LN_EPS)
    xhat = d * rstd
    return xhat, rstd, xhat * g + b


def _ln_bwd(dy, xhat, rstd, g):
    dxh = dy * g
    m1 = jnp.mean(dxh, axis=-1, keepdims=True)
    m2 = jnp.mean(dxh * xhat, axis=-1, keepdims=True)
    dh = rstd * (dxh - m1 - xhat * m2)
    return dh, jnp.sum(dy * xhat, axis=0, keepdims=True), jnp.sum(dy, axis=0, keepdims=True)


def _wide(col, ts):
    return jnp.broadcast_to(col, (ts, LANES))


def _pool_fwd(x, wp, pb, ps, g, b):
    S, D = x.shape
    gw = D // len(POOL_WINDOWS)
    ts = min(ROW_TILE, S)

    def fn(i, n, rv, hv, cv):
        (xc,), (xp,) = rv, hv
        wpv, pbv, psv, gv, bv = cv
        xp = jnp.where(i > 0, xp, 0.0)
        xx = jnp.concatenate([xp, xc], axis=0)
        t = i * ts + lax.broadcasted_iota(jnp.int32, (ts, 1), 0)
        pooled, ys = [], []
        for gi, w in enumerate(POOL_WINDOWS):
            s = xx[:, gi * gw : (gi + 1) * gw]
            k = 1
            while k < w:
                s = s + pltpu.roll(s, k, axis=0)
                k *= 2
            cnt = jnp.minimum(t + 1, w).astype(F32)
            pg = (s[POOL_HALO:, :] / cnt - xc[:, gi * gw : (gi + 1) * gw]).astype(BF16)
            pooled.append(pg)
            ys.append(jnp.dot(pg, wpv[gi], preferred_element_type=F32))
        y = jnp.concatenate(ys, axis=1)
        h = ALPHA * xc + (y + pbv) * psv
        xhat, rstd, xa = _ln(h, gv, bv)
        return (jnp.concatenate(pooled, axis=1), xhat, _wide(rstd, ts), xa), ()

    return _rowwise(
        "pool_fwd", fn, S, ts, rows=[x], halos=[(x, "prev", POOL_HALO)], consts=[wp, pb, ps, g, b],
        outs=[(D, BF16), (D, F32), (LANES, F32), (D, BF16)],
    )


def _pool_bwd(dh, pooled, wp, pb, ps):
    S, D = dh.shape
    gw = D // len(POOL_WINDOWS)
    ts = min(ROW_TILE, S)
    te = ts + POOL_HALO

    def fn(i, n, rv, hv, cv):
        (dhc, pc), (dhn,) = rv, hv
        wpv, pbv, psv = cv
        dhn = jnp.where(i < n - 1, dhn, 0.0)
        dy_ext = jnp.concatenate([dhc, dhn], axis=0) * psv
        dyb = dy_ext.astype(BF16)
        t = i * ts + lax.broadcasted_iota(jnp.int32, (te, 1), 0)
        dxs, ys = [], []
        for gi, w in enumerate(POOL_WINDOWS):
            sl = slice(gi * gw, (gi + 1) * gw)
            dp = lax.dot_general(dyb[:, sl], wpv[gi], _DIMS["nt"], preferred_element_type=F32)
            s = dp / jnp.minimum(t + 1, w).astype(F32)
            k = 1
            while k < w:
                s = s + pltpu.roll(s, k, axis=0)
                k *= 2
            s = pltpu.roll(s, POOL_HALO - (w - 1), axis=0)
            dxs.append(s[POOL_HALO:, :] - dp[:ts, :])
            ys.append(jnp.dot(pc[:, sl], wpv[gi], preferred_element_type=F32))
        dx = ALPHA * dhc + jnp.concatenate(dxs, axis=1)
        y = jnp.concatenate(ys, axis=1) + pbv
        dscale = jnp.sum(dhc * y, axis=0, keepdims=True)
        dbias = jnp.sum(dy_ext[:ts, :], axis=0, keepdims=True)
        return (dx, dyb[:ts, :]), (dscale, dbias)

    return _rowwise(
        "pool_bwd", fn, S, ts, rows=[dh, pooled], halos=[(dh, "next", POOL_HALO)], consts=[wp, pb, ps],
        outs=[(D, F32), (D, BF16)], accs=[(1, D), (1, D)],
    )


def _pool_dw(pooled, dy):
    S, D = pooled.shape
    G = len(POOL_WINDOWS)
    gw = D // G
    tk = min(MM_TK, S)
    nk = S // tk

    def body(p_ref, d_ref, o_ref):
        k = pl.program_id(1)
        part = lax.dot_general(p_ref[...], d_ref[...], _DIMS["tn"], preferred_element_type=F32)

        @pl.when(k == 0)
        def _():
            o_ref[...] = part

        @pl.when(k > 0)
        def _():
            o_ref[...] += part

    return pl.pallas_call(
        body,
        name="pool_dw",
        grid=(G, nk),
        in_specs=[pl.BlockSpec((tk, gw), lambda g, k: (k, g)), pl.BlockSpec((tk, gw), lambda g, k: (k, g))],
        out_specs=pl.BlockSpec((None, gw, gw), lambda g, k: (g, 0, 0)),
        out_shape=jax.ShapeDtypeStruct((G, gw, gw), F32),
        compiler_params=_params(2),
    )(pooled, dy)


def _res_ln_epi(acc, xh, gp_, bp_, g, b):
    xhat, rstd, xo = _ln(ALPHA * (xh * gp_ + bp_) + acc, g, b)
    return xhat, _wide(rstd, acc.shape[0]), xo


def _res_ln_ffpe_epi(acc, xh, gate, pp, gp_, bp_, g, b):
    return _res_ln_epi(acc + jax.nn.sigmoid(gate) * pp, xh, gp_, bp_, g, b)


def _final_ln_loss_epi(acc, xh, gate, pp, tgt, gp_, bp_, g, b):
    sg = jax.nn.sigmoid(gate)
    xhat, rstd, y = _ln(ALPHA * (xh * gp_ + bp_) + acc + sg * pp, g, b)
    e = y - tgt
    dh, dg, db = _ln_bwd(e * (1.0 / acc.shape[1]), xhat, rstd, g)
    dgt, dpp, dbg = _ple_grads(dh, sg, pp)
    return dh, dh, dgt, dpp, jnp.sum(e * e, axis=0, keepdims=True), dg, db, dbg


def _ln_bwd_epi(acc, rest, xhat, rstd, g):
    dh, dg, db = _ln_bwd(acc + rest, xhat, rstd[:, :1], g)
    return dh, dh, dg, db


def _ple_grads(dh, sg, pp):
    dgt = dh * pp * sg * (1.0 - sg)
    return dgt, dh * sg, jnp.sum(dgt, axis=0, keepdims=True)


def _ln_bwd_ple_epi(acc, rest, xhat, rstd, gate, pp, g):
    dh, dg, db = _ln_bwd(acc + rest, xhat, rstd[:, :1], g)
    dgt, dpp, dbg = _ple_grads(dh, jax.nn.sigmoid(gate), pp)
    return dh, dh, dgt, dpp, dg, db, dbg


def _silu(c):
    return c * jax.nn.sigmoid(c)


def _qkv_point(c, is_qk, scale):
    s = _silu(c)
    nrm = s * lax.rsqrt(jnp.sum(s * s, axis=-1, keepdims=True) + L2_EPS) * scale
    return jnp.where(is_qk, nrm, s)


def _conv_rows(xx, wv, lo, rows):
    acc = None
    for j in range(CONV_WIDTH):
        sh = CONV_WIDTH - 1 - j
        term = (pltpu.roll(xx, sh, axis=0) if sh else xx)[lo : lo + rows, :] * wv[j : j + 1, :]
        acc = term if acc is None else acc + term
    return acc


def _conv_fwd(qkv_pre, conv_w):
    S, W = qkv_pre.shape
    D = W // 3
    H = D // HEAD_DIM
    ts = min(CONV_TILE, S)
    r = ts // CONV_HALO

    def body(x_ref, xp_ref, w_ref, o_ref):
        j, i = pl.program_id(0), pl.program_id(1)
        xp = jnp.where(i > 0, xp_ref[...], 0.0)
        xx = jnp.concatenate([xp, x_ref[...]], axis=0)
        c = _conv_rows(xx, w_ref[...], CONV_HALO, ts)
        scale = jnp.where(j == 0, HEAD_DIM**-0.5, 1.0).astype(F32)
        for h in range(H):
            sl = slice(h * HEAD_DIM, (h + 1) * HEAD_DIM)
            o_ref[:, sl] = _qkv_point(c[:, sl], j < 2, scale)

    return pl.pallas_call(
        body,
        name="gdn_conv_fwd",
        grid=(3, S // ts),
        in_specs=[
            pl.BlockSpec((ts, D), lambda j, i: (i, j)),
            pl.BlockSpec((CONV_HALO, D), lambda j, i: (jnp.maximum(i * r - 1, 0), j)),
            pl.BlockSpec((CONV_WIDTH, D), lambda j, i: (0, j)),
        ],
        out_specs=pl.BlockSpec((ts, D), lambda j, i: (i, j)),
        out_shape=jax.ShapeDtypeStruct((S, W), F32),
        compiler_params=_params(2),
    )(qkv_pre, qkv_pre, conv_w)


def _conv_bwd(qkv_pre, conv_w, dqkvn):
    S, W = qkv_pre.shape
    D = W // 3
    H = D // HEAD_DIM
    ts = min(CONV_TILE, S)
    r, nb = ts // CONV_HALO, S // CONV_HALO
    te = ts + CONV_HALO

    def body(x_ref, xp_ref, xn_ref, w_ref, d_ref, dn_ref, dx_ref, dw_ref):
        j, i = pl.program_id(0), pl.program_id(1)
        n = pl.num_programs(1)
        wv = w_ref[...]
        xp = jnp.where(i > 0, xp_ref[...], 0.0)
        xx = jnp.concatenate([xp, x_ref[...], xn_ref[...]], axis=0)
        xr = [pltpu.roll(xx, sh, axis=0) if sh else xx for sh in range(CONV_WIDTH)]
        c = None
        for jj in range(CONV_WIDTH):
            term = xr[CONV_WIDTH - 1 - jj][CONV_HALO : CONV_HALO + te, :] * wv[jj : jj + 1, :]
            c = term if c is None else c + term
        dn = jnp.where(i < n - 1, dn_ref[...], 0.0)
        dout = jnp.concatenate([d_ref[...], dn], axis=0)
        scale = jnp.where(j == 0, HEAD_DIM**-0.5, 1.0).astype(F32)
        dcs = []
        for h in range(H):
            sl = slice(h * HEAD_DIM, (h + 1) * HEAD_DIM)
            _, vjp = jax.vjp(lambda cc: _qkv_point(cc, j < 2, scale), c[:, sl])
            dcs.append(vjp(dout[:, sl])[0])
        dc = jnp.concatenate(dcs, axis=1)
        dx = None
        dws = []
        for jj in range(CONV_WIDTH):
            sh = CONV_WIDTH - 1 - jj
            term = pltpu.roll(dc, CONV_HALO - sh, axis=0)[CONV_HALO:, :] * wv[jj : jj + 1, :]
            dx = term if dx is None else dx + term
            dws.append(jnp.sum(dc[:ts, :] * xr[sh][CONV_HALO : CONV_HALO + ts, :], axis=0, keepdims=True))
        dx_ref[...] = dx.astype(dx_ref.dtype)
        dw = jnp.concatenate(dws, axis=0)

        @pl.when(i == 0)
        def _():
            dw_ref[...] = dw

        @pl.when(i > 0)
        def _():
            dw_ref[...] += dw

    return pl.pallas_call(
        body,
        name="gdn_conv_bwd",
        grid=(3, S // ts),
        in_specs=[
            pl.BlockSpec((ts, D), lambda j, i: (i, j)),
            pl.BlockSpec((CONV_HALO, D), lambda j, i: (jnp.maximum(i * r - 1, 0), j)),
            pl.BlockSpec((CONV_HALO, D), lambda j, i: (jnp.minimum((i + 1) * r, nb - 1), j)),
            pl.BlockSpec((CONV_WIDTH, D), lambda j, i: (0, j)),
            pl.BlockSpec((ts, D), lambda j, i: (i, j)),
            pl.BlockSpec((CONV_HALO, D), lambda j, i: (jnp.minimum((i + 1) * r, nb - 1), j)),
        ],
        out_specs=[pl.BlockSpec((ts, D), lambda j, i: (i, j)), pl.BlockSpec((CONV_WIDTH, D), lambda j, i: (0, j))],
        out_shape=[jax.ShapeDtypeStruct((S, W), BF16), jax.ShapeDtypeStruct((CONV_WIDTH, W), F32)],
        compiler_params=_params(2),
    )(qkv_pre, qkv_pre, qkv_pre, conv_w, dqkvn, dqkvn)


def _softplus(x):
    pos = x > 0.0
    return jnp.where(pos, x, 0.0) + jnp.log(1.0 + jnp.exp(jnp.where(pos, -x, x)))


def _gates(bl, al, alog, dt):
    return jax.nn.sigmoid(bl), -jnp.exp(alog) * _softplus(al + dt)


def _gates_fwd(ba, alog, dt):
    S = ba.shape[0]
    ts = min(ROW_TILE, S)

    def fn(i, n, rv, hv, cv):
        return _gates(rv[0][:, :LANES], rv[0][:, LANES:], cv[0], cv[1]), ()

    return _rowwise("gdn_gates_fwd", fn, S, ts, rows=[ba], consts=[alog, dt], outs=[(LANES, F32), (LANES, F32)])


def _gates_bwd(ba, alog, dt, dbeta, dg, H):
    S = ba.shape[0]
    ts = min(ROW_TILE, S)

    def fn(i, n, rv, hv, cv):
        bav, dbv, dgv = rv
        real = lax.broadcasted_iota(jnp.int32, (1, LANES), 1) < H
        _, vjp = jax.vjp(_gates, bav[:, :LANES], bav[:, LANES:], cv[0], cv[1])
        dbl, dal, dalog, ddt = vjp((jnp.where(real, dbv, 0.0), jnp.where(real, dgv, 0.0)))
        dbl, dal = jnp.where(real, dbl, 0.0), jnp.where(real, dal, 0.0)
        return (jnp.concatenate([dbl, dal], axis=1),), (jnp.where(real, dalog, 0.0), jnp.where(real, ddt, 0.0))

    return _rowwise(
        "gdn_gates_bwd", fn, S, ts, rows=[ba, dbeta, dg], consts=[alog, dt], outs=[(2 * LANES, BF16)],
        accs=[(1, LANES), (1, LANES)],
    )


def _split_bf16(a, n):
    parts, rest = [], a
    for _ in range(n):
        piece = rest.astype(BF16)
        parts.append(piece)
        rest = rest - piece.astype(F32)
    return parts


def _tri_dot(a, b, mode, tri):
    d = lambda u, v: lax.dot_general(u, v, _DIMS[mode], preferred_element_type=F32)
    if tri == 0:
        return sum(d(a.astype(BF16), piece) for piece in _split_bf16(b, 3))
    return sum(d(piece, b.astype(BF16)) for piece in _split_bf16(a, 3))


def _bdot_raw(a, b, mode):
    return lax.dot_general(a.astype(BF16), b.astype(BF16), _DIMS[mode], preferred_element_type=F32)


@functools.partial(jax.custom_vjp, nondiff_argnums=(2,))
def _bdot(a, b, mode):
    return _bdot_raw(a, b, mode)


def _bdot_fwd(a, b, mode):
    return _bdot_raw(a, b, mode), (a, b)


def _bdot_bwd(mode, res, ct):
    a, b = res
    if mode == "nn":
        return _bdot(ct, b, "nt"), _bdot(a, ct, "tn")
    if mode == "nt":
        return _bdot(ct, b, "nn"), _bdot(ct, a, "tn")
    return _bdot(b, ct, "nt"), _bdot(a, ct, "nn")


_bdot.defvjp(_bdot_fwd, _bdot_bwd)


@jax.custom_vjp
def _unit_lower_inverse(a_strict):
    return _unit_lower_inverse_raw(a_strict)


def _unit_lower_inverse_fwd(a_strict):
    t = _unit_lower_inverse_raw(a_strict)
    return t, t


def _unit_lower_inverse_bwd(t, ct):
    left = [_bdot(ti, ci, "tn") for ti, ci in zip(t, ct)]
    return (tuple(-_bdot(li, ti, "nt") for li, ti in zip(left, t)),)


_unit_lower_inverse.defvjp(_unit_lower_inverse_fwd, _unit_lower_inverse_bwd)


@jax.custom_vjp
def _saved_inverse(a_strict, t):
    return t


def _saved_inverse_fwd(a_strict, t):
    return t, t


def _saved_inverse_bwd(t, ct):
    return _unit_lower_inverse_bwd(t, ct) + (tuple(jnp.zeros_like(ti) for ti in t),)


_saved_inverse.defvjp(_saved_inverse_fwd, _saved_inverse_bwd)


def _unit_lower_inverse_raw(a_strict):
    C = a_strict[0].shape[0]
    ii = lax.broadcasted_iota(jnp.int32, (C, C), 0)
    jj = lax.broadcasted_iota(jnp.int32, (C, C), 1)
    eye = (ii == jj).astype(F32)
    blk = 16
    same = (ii // blk) == (jj // blk)
    p = [-jnp.where(same, a, 0.0) for a in a_strict]
    t = [eye + x for x in p]
    for _ in range(3):
        p = [_bdot(x, x, "nn") for x in p]
        t = [ti + _bdot(ti, x, "nn") for ti, x in zip(t, p)]
    while blk < C:
        same2 = (ii // (2 * blk)) == (jj // (2 * blk))
        off = jnp.logical_and(same2, jnp.logical_not(same))
        te = [_bdot(ti, jnp.where(off, a, 0.0), "nn") for ti, a in zip(t, a_strict)]
        t = [ti - _bdot(x, ti, "nn") for ti, x in zip(t, te)]
        same, blk = same2, 2 * blk
    return tuple(t)


def _chunk_heads(q, k, v, gc_col, gc_row, b_col, s0, t_saved=None, with_t=False):
    R = range(len(q))
    C = q[0].shape[0]
    ii = lax.broadcasted_iota(jnp.int32, (C, C), 0)
    jj = lax.broadcasted_iota(jnp.int32, (C, C), 1)
    rows = lax.broadcasted_iota(jnp.int32, (C, 1), 0)
    decay = [jnp.where(ii >= jj, jnp.exp(jnp.minimum(gc_col[h] - gc_row[h], 0.0)), 0.0) for h in R]
    kb = [k[h] * b_col[h] for h in R]
    a = [_bdot(kb[h], k[h], "nt") * decay[h] for h in R]
    qk = [_bdot(q[h], k[h], "nt") * decay[h] for h in R]
    a_strict = tuple(jnp.where(ii > jj, a[h], 0.0) for h in R)
    t = _unit_lower_inverse(a_strict) if t_saved is None else _saved_inverse(a_strict, t_saved)
    eg = [jnp.exp(gc_col[h]) for h in R]
    u = [_bdot(t[h], v[h] * b_col[h], "nn") for h in R]
    w = [_bdot(t[h], kb[h] * eg[h], "nn") for h in R]
    g_last = [jnp.sum(jnp.where(rows == C - 1, gc_col[h], 0.0), axis=0, keepdims=True) for h in R]
    kd = [k[h] * jnp.exp(g_last[h] - gc_col[h]) for h in R]
    ws = [_bdot(w[h], s0[h], "nn") for h in R]
    qs = [_bdot(q[h] * eg[h], s0[h], "nn") for h in R]
    v_new = [u[h] - ws[h] for h in R]
    o = [qs[h] + _bdot(qk[h], v_new[h], "nn") for h in R]
    s1 = [s0[h] * jnp.exp(g_last[h]) + _bdot(kd[h], v_new[h], "tn") for h in R]
    return (tuple(o), tuple(s1), t) if with_t else (tuple(o), tuple(s1))


def _pick_lane(a, h):
    lanes = lax.broadcasted_iota(jnp.int32, a.shape, 1)
    return jnp.sum(jnp.where(lanes == h, a, 0.0), axis=1, keepdims=True)


def _pick_row(a, h):
    rows = lax.broadcasted_iota(jnp.int32, a.shape, 0)
    return jnp.sum(jnp.where(rows == h, a, 0.0), axis=0, keepdims=True)


def _tri(C):
    ii = lax.broadcasted_iota(jnp.int32, (C, C), 0)
    jj = lax.broadcasted_iota(jnp.int32, (C, C), 1)
    return (ii >= jj).astype(F32)


def _delta_fwd(qkvn, g_pad, g_rows, beta_pad):
    S, W = qkvn.shape
    D = W // 3
    H = D // HEAD_DIM
    C = min(CHUNK, S)
    N = S // C

    def body(x_ref, gp_ref, gr_ref, bp_ref, o_ref, sall_ref, tall_ref, st):
        n = pl.program_id(0)

        @pl.when(n == 0)
        def _():
            st[...] = jnp.zeros_like(st)

        low = _tri(C)
        gc_cols = _tri_dot(low, gp_ref[...], "nn", 0)
        gc_rows = _tri_dot(gr_ref[...], low, "nt", 1)
        bcols = bp_ref[...]
        hs = range(H)
        s0 = tuple(st[h] for h in hs)
        for h in hs:
            sall_ref[h] = s0[h]
        o, s1, t = _chunk_heads(
            tuple(x_ref[:, h * HEAD_DIM : (h + 1) * HEAD_DIM] for h in hs),
            tuple(x_ref[:, D + h * HEAD_DIM : D + (h + 1) * HEAD_DIM] for h in hs),
            tuple(x_ref[:, 2 * D + h * HEAD_DIM : 2 * D + (h + 1) * HEAD_DIM] for h in hs),
            tuple(_pick_lane(gc_cols, h) for h in hs), tuple(_pick_row(gc_rows, h) for h in hs),
            tuple(_pick_lane(bcols, h) for h in hs), s0, with_t=True,
        )
        for h in hs:
            st[h] = s1[h]
            o_ref[:, h * HEAD_DIM : (h + 1) * HEAD_DIM] = o[h]
            tall_ref[h] = t[h].astype(tall_ref.dtype)

    return pl.pallas_call(
        body,
        name="gdn_delta_fwd",
        grid=(N,),
        in_specs=[
            pl.BlockSpec((C, W), lambda n: (n, 0)),
            pl.BlockSpec((C, LANES), lambda n: (n, 0)),
            pl.BlockSpec((None, 8, C), lambda n: (n, 0, 0)),
            pl.BlockSpec((C, LANES), lambda n: (n, 0)),
        ],
        out_specs=[pl.BlockSpec((C, D), lambda n: (n, 0)), pl.BlockSpec((None, H, HEAD_DIM, HEAD_DIM), lambda n: (n, 0, 0, 0)),
                   pl.BlockSpec((None, H, C, C), lambda n: (n, 0, 0, 0))],
        out_shape=[jax.ShapeDtypeStruct((S, D), F32), jax.ShapeDtypeStruct((N, H, HEAD_DIM, HEAD_DIM), F32), jax.ShapeDtypeStruct((N, H, C, C), BF16)],
        scratch_shapes=[pltpu.VMEM((H, HEAD_DIM, HEAD_DIM), F32)],
        compiler_params=_params(1),
    )(qkvn, g_pad, g_rows, beta_pad)


def _delta_bwd(qkvn, g_pad, g_rows, beta_pad, s_all, t_all, do):
    S, W = qkvn.shape
    D = W // 3
    H = D // HEAD_DIM
    C = min(CHUNK, S)
    N = S // C

    def body(x_ref, gp_ref, gr_ref, bp_ref, sall_ref, tall_ref, do_ref, dx_ref, dgp_ref, dgr_ref, dbp_ref, dst):
        n = pl.program_id(0)

        @pl.when(n == 0)
        def _():
            dst[...] = jnp.zeros_like(dst)

        low = _tri(C)
        gc_cols = _tri_dot(low, gp_ref[...], "nn", 0)
        gc_rows = _tri_dot(gr_ref[...], low, "nt", 1)
        bcols = bp_ref[...]
        lane = lax.broadcasted_iota(jnp.int32, (1, LANES), 1)
        row8 = lax.broadcasted_iota(jnp.int32, (8, 1), 0)
        dgc_cols = jnp.zeros((C, LANES), F32)
        dgc_rows = jnp.zeros((8, C), F32)
        dbcols = jnp.zeros((C, LANES), F32)
        hs = range(H)
        t_saved = tuple(tall_ref[h].astype(F32) for h in hs)
        _, vjp = jax.vjp(
            lambda *args: _chunk_heads(*args, t_saved=t_saved),
            tuple(x_ref[:, h * HEAD_DIM : (h + 1) * HEAD_DIM] for h in hs),
            tuple(x_ref[:, D + h * HEAD_DIM : D + (h + 1) * HEAD_DIM] for h in hs),
            tuple(x_ref[:, 2 * D + h * HEAD_DIM : 2 * D + (h + 1) * HEAD_DIM] for h in hs),
            tuple(_pick_lane(gc_cols, h) for h in hs), tuple(_pick_row(gc_rows, h) for h in hs),
            tuple(_pick_lane(bcols, h) for h in hs), tuple(sall_ref[h] for h in hs),
        )
        dq, dk, dv, dgc, dgr, dbc, ds0 = vjp((tuple(do_ref[:, h * HEAD_DIM : (h + 1) * HEAD_DIM] for h in hs), tuple(dst[h] for h in hs)))
        for h in hs:
            dst[h] = ds0[h]
            dx_ref[:, h * HEAD_DIM : (h + 1) * HEAD_DIM] = dq[h]
            dx_ref[:, D + h * HEAD_DIM : D + (h + 1) * HEAD_DIM] = dk[h]
            dx_ref[:, 2 * D + h * HEAD_DIM : 2 * D + (h + 1) * HEAD_DIM] = dv[h]
            dgc_cols = dgc_cols + dgc[h] * (lane == h).astype(F32)
            dgc_rows = dgc_rows + dgr[h] * (row8 == h).astype(F32)
            dbcols = dbcols + dbc[h] * (lane == h).astype(F32)
        dgp_ref[...] = _tri_dot(low, dgc_cols, "tn", 0)
        dgr_ref[...] = _tri_dot(dgc_rows, low, "nn", 1)
        dbp_ref[...] = dbcols

    rev = lambda n: N - 1 - n
    return pl.pallas_call(
        body,
        name="gdn_delta_bwd",
        grid=(N,),
        in_specs=[
            pl.BlockSpec((C, W), lambda n: (rev(n), 0)),
            pl.BlockSpec((C, LANES), lambda n: (rev(n), 0)),
            pl.BlockSpec((None, 8, C), lambda n: (rev(n), 0, 0)),
            pl.BlockSpec((C, LANES), lambda n: (rev(n), 0)),
            pl.BlockSpec((None, H, HEAD_DIM, HEAD_DIM), lambda n: (rev(n), 0, 0, 0)),
            pl.BlockSpec((None, H, C, C), lambda n: (rev(n), 0, 0, 0)),
            pl.BlockSpec((C, D), lambda n: (rev(n), 0)),
        ],
        out_specs=[
            pl.BlockSpec((C, W), lambda n: (rev(n), 0)),
            pl.BlockSpec((C, LANES), lambda n: (rev(n), 0)),
            pl.BlockSpec((None, 8, C), lambda n: (rev(n), 0, 0)),
            pl.BlockSpec((C, LANES), lambda n: (rev(n), 0)),
        ],
        out_shape=[
            jax.ShapeDtypeStruct((S, W), F32),
            jax.ShapeDtypeStruct((S, LANES), F32),
            jax.ShapeDtypeStruct((N, 8, C), F32),
            jax.ShapeDtypeStruct((S, LANES), F32),
        ],
        scratch_shapes=[pltpu.VMEM((H, HEAD_DIM, HEAD_DIM), F32)],
        compiler_params=_params(1),
    )(qkvn, g_pad, g_rows, beta_pad, s_all, t_all, do)


def _gate_norm_head(o, z, nw):
    return o * lax.rsqrt(jnp.mean(o * o, axis=-1, keepdims=True) + RMS_EPS) * nw * _silu(z)


def _gate_norm_fwd(o, z, nw):
    S, D = o.shape
    H = D // HEAD_DIM
    ts = min(ROW_TILE, S)

    def fn(i, n, rv, hv, cv):
        ov, zv = rv
        parts = [_gate_norm_head(ov[:, h * HEAD_DIM : (h + 1) * HEAD_DIM], zv[:, h * HEAD_DIM : (h + 1) * HEAD_DIM], cv[0]) for h in range(H)]
        return (jnp.concatenate(parts, axis=1),), ()

    return _rowwise("gdn_gate_norm_fwd", fn, S, ts, rows=[o, z], consts=[nw], outs=[(D, BF16)])[0]


def _gate_norm_bwd_epi(dog, o, z, nw):
    D = dog.shape[1]
    dos, dzs, dnw = [], [], None
    for h in range(D // HEAD_DIM):
        sl = slice(h * HEAD_DIM, (h + 1) * HEAD_DIM)
        _, vjp = jax.vjp(_gate_norm_head, o[:, sl], z[:, sl], nw)
        a, b_, c_ = vjp(dog[:, sl])
        dos.append(a)
        dzs.append(b_)
        dnw = c_ if dnw is None else dnw + c_
    wide = jnp.concatenate([dnw, jnp.zeros((1, D - HEAD_DIM), F32)], axis=1) if D > HEAD_DIM else dnw
    return jnp.concatenate(dos, axis=1), jnp.concatenate(dzs, axis=1), wide


def _square_bf16(r):
    return r * r


def _mlp_ple_dw(li, dhb, dgate, dpp, xa, p, r, w2):
    dpre = _mm(f"l{li}_mlp_down_bwd", dhb, w2, "nt", [BF16], epi=lambda acc, rr: (acc * (2.0 * rr.astype(F32)),), extras=[(r, "tile")], tm=1024, tn=1024, b_outer=True)
    dw2 = _mm(f"l{li}_mlp_dw2", r, dhb, "tn", [BF16], a_fn=_square_bf16, **DW_TILES)
    dw1 = _mm(f"l{li}_mlp_dw1", xa, dpre, "tn", [BF16], out_blocks=N_DEV, **DW_TILES)
    dwg = _mm(f"l{li}_ple_dwg", xa, dgate, "tn", [BF16], **DW_TILES)
    dwp = _mm(f"l{li}_ple_dwp", p, dpp, "tn", [BF16], out_blocks=N_DEV, **DW_TILES)
    rows = lambda a: a.reshape((N_DEV, a.shape[0] // N_DEV) + a.shape[1:])
    return dpre, dw1, rows(dw2), rows(dwg), dwp


def _mlp_ple_dx(li, dh, dpre, dgate, w1, wg, after, xhat, rstd, g):
    t = _mm(f"l{li}_ple_gate_bwd", dgate, wg, "nt", [F32], epi=lambda acc, d: (acc + ALPHA * d,), extras=[(dh, "tile")], tm=1024, after=after)
    return _mm(f"l{li}_mlp_up_bwd", dpre, w1, "nt", [F32, BF16], epi=_ln_bwd_epi, extras=[(t, "tile"), (xhat, "tile"), (rstd, "rows"), (g, "row")],
               tm=256, tk=4096, accs=2)


def _local_step(x, p, tgt, W, fetch, emit):
    S, D = x.shape
    H = D // HEAD_DIM
    C = min(CHUNK, S)
    N = S // C
    lg = lambda i, j: W["ln_gain"][2 * i + j][None, :]
    lb = lambda i, j: W["ln_bias"][2 * i + j][None, :]
    G = {}

    pooled, xh0a, rs0a, x0a = _pool_fwd(x, W["pool_w"], W["pool_b"], W["pool_scale"], lg(0, 0), lb(0, 0))
    w0a = fetch("l0a", x0a)
    r0 = _mm("l0_mlp_up", x0a, w0a["mlp_w1"], "nn", [BF16], epi=lambda acc: (jnp.maximum(acc, 0.0),), tm=1024, tn=1024, b_outer=True, after=w0a.get("_after", ()))
    w0b = fetch("l0b", r0)
    gate0 = _mm("l0_ple_gate", x0a, w0b["ple_gate_w"], "nn", [F32], epi=lambda acc, bias: (acc + bias,), extras=[(W["ple_gate_b"][0:1], "row")], tm=1024)
    pp0 = _mm("l0_ple_proj", p[0], w0b["ple_proj"], "nn", [F32])
    w0c = fetch("l0c", pp0)
    ln_rows = lambda i, j, i2, j2: [(lg(i, j), "row"), (lb(i, j), "row"), (lg(i2, j2), "row"), (lb(i2, j2), "row")]
    xh0b, rs0b, x0b = _mm("l0_mlp_down", r0, w0c["mlp_w2"], "nn", [F32, (F32, LANES), BF16], a_fn=_square_bf16, tm=256, tn=D, tk=4096, epi=_res_ln_ffpe_epi,
                          extras=[(xh0a, "tile"), (gate0, "tile"), (pp0, "tile")] + ln_rows(0, 0, 0, 1), after=w0c.get("_after", ()))

    wg_ = fetch("gdn", x0b)
    qkv_pre = _mm("gdn_in_qkv", x0b, wg_["gdn_wqkv"], "nn", [F32], tm=1024, tn=1024, b_outer=True, after=wg_.get("_after", ()))
    z = _mm("gdn_in_z", x0b, wg_["gdn_wz"], "nn", [F32], tm=1024)
    ba = _mm("gdn_in_ba", x0b, wg_["gdn_wba"], "nn", [F32])
    qkvn = _conv_fwd(qkv_pre, W["gdn_conv"])
    beta_pad, g_pad = _gates_fwd(ba, W["gdn_a_log"], W["gdn_dt_bias"])
    g_rows = g_pad[:, :8].reshape(N, C, 8).transpose(0, 2, 1)
    o, s_all, t_all = _delta_fwd(qkvn, g_pad, g_rows, beta_pad)
    og = _gate_norm_fwd(o, z, W["gdn_norm_w"])
    wg_.update(fetch("gdo", og))
    xh1a, rs1a, x1a = _mm("gdn_out", og, wg_["gdn_w_out"], "nn", [F32, (F32, LANES), BF16], tm=512, tn=D, epi=_res_ln_epi,
                          extras=[(xh0b, "tile")] + ln_rows(0, 1, 1, 0), after=wg_.get("_after", ()))
    w1_ = fetch("l1", x1a)
    r1 = _mm("l1_mlp_up", x1a, w1_["mlp_w1"], "nn", [BF16], epi=lambda acc: (jnp.maximum(acc, 0.0),), tm=1024, tn=1024, b_outer=True)
    gate1 = _mm("l1_ple_gate", x1a, w1_["ple_gate_w"], "nn", [F32], epi=lambda acc, bias: (acc + bias,), extras=[(W["ple_gate_b"][1:2], "row")], tm=1024)
    pp1 = _mm("l1_ple_proj", p[1], w1_["ple_proj"], "nn", [F32])
    dh1b, dh1b_b, dgate1, dpp1, loss_cols, dg11, db11, dbg_1 = _mm(
        "l1_mlp_down", r1, w1_["mlp_w2"], "nn", [F32, BF16, BF16, BF16], a_fn=_square_bf16, tm=256, tn=D, tk=4096, epi=_final_ln_loss_epi,
        extras=[(xh1a, "tile"), (gate1, "tile"), (pp1, "tile"), (tgt, "tile")] + ln_rows(1, 0, 1, 1), accs=4)

    dpre1, dw1_1, dw2_1, dwg_1, dwp_1 = _mlp_ple_dw(1, dh1b_b, dgate1, dpp1, x1a, p[1], r1, w1_["mlp_w2"])
    tok = emit("l1", {"mlp_w1": dw1_1, "mlp_w2": dw2_1, "ple_gate_w": dwg_1, "ple_proj": dwp_1})
    dh1a, dh1a_b, dg10, db10 = _mlp_ple_dx(1, dh1b, dpre1, dgate1, w1_["mlp_w1"], w1_["ple_gate_w"], [tok], xh1a, rs1a, lg(1, 0))
    do, dz, dnw = _mm("gdn_out_bwd", dh1a_b, wg_["gdn_w_out"], "nt", [F32, BF16], tm=256, tn=D, tk=D, epi=_gate_norm_bwd_epi,
                      extras=[(o, "tile"), (z, "tile"), (W["gdn_norm_w"], "whole")], accs=1)
    dnw = dnw[:, :HEAD_DIM]
    dw_out = _mm("gdn_dw_out", og, dh1a_b, "tn", [BF16], **DW_TILES)
    dw_out = dw_out.reshape((N_DEV, dw_out.shape[0] // N_DEV) + dw_out.shape[1:])
    dqkvn, dg_col, dg_row, dbeta = _delta_bwd(qkvn, g_pad, g_rows, beta_pad, s_all, t_all, do)
    dg_all = dg_col + jnp.pad(dg_row.transpose(0, 2, 1).reshape(S, 8), ((0, 0), (0, LANES - 8)))
    dba, dalog, ddt = _gates_bwd(ba, W["gdn_a_log"], W["gdn_dt_bias"], dbeta, dg_all, H)
    dqkv, dconv = _conv_bwd(qkv_pre, W["gdn_conv"], dqkvn)
    dwqkv = _mm("gdn_dwqkv", x0b, dqkv, "tn", [F32], **DW_TILES)
    dwz = _mm("gdn_dwz", x0b, dz, "tn", [F32], **DW_TILES)
    dwba = _mm("gdn_dwba", x0b, dba, "tn", [F32], **DW_TILES)
    dw_in = jnp.concatenate([dwqkv, dwz, dwba[:, :H], dwba[:, LANES : LANES + H]], axis=1)
    tok = emit("gdn", {"gdn_w_in": _split_blocks("gdn_w_in", dw_in).astype(BF16), "gdn_w_out": dw_out})
    t = _mm("gdn_in_ba_bwd", dba, wg_["gdn_wba"], "nt", [F32], epi=lambda acc, d: (acc + ALPHA * d,), extras=[(dh1a, "tile")], after=[tok])
    t = _mm("gdn_in_z_bwd", dz, wg_["gdn_wz"], "nt", [F32], epi=lambda acc, d: (acc + d,), extras=[(t, "tile")], tm=1024)
    dh0b, dh0b_b, dgate0, dpp0, dg01, db01, dbg_0 = _mm(
        "gdn_in_qkv_bwd", dqkv, wg_["gdn_wqkv"], "nt", [F32, BF16, BF16, BF16], epi=_ln_bwd_ple_epi,
        extras=[(t, "tile"), (xh0b, "tile"), (rs0b, "rows"), (gate0, "tile"), (pp0, "tile"), (lg(0, 1), "row")], tm=256, tk=3072, accs=3)

    dpre0, dw1_0, dw2_0, dwg_0, dwp_0 = _mlp_ple_dw(0, dh0b_b, dgate0, dpp0, x0a, p[0], r0, w0c["mlp_w2"])
    tok = emit("l0", {"mlp_w1": dw1_0, "mlp_w2": dw2_0, "ple_gate_w": dwg_0, "ple_proj": dwp_0})
    dh0a, _, dg00, db00 = _mlp_ple_dx(0, dh0b, dpre0, dgate0, w0a["mlp_w1"], w0b["ple_gate_w"], [tok], xh0a, rs0a, lg(0, 0))
    grad_x, dyp, dscale, dpb = _pool_bwd(dh0a, pooled, W["pool_w"], W["pool_b"], W["pool_scale"])
    G["pool_w"] = _pool_dw(pooled, dyp)

    G["ln_gain"] = jnp.concatenate([dg00, dg01, dg10, dg11], axis=0)
    G["ln_bias"] = jnp.concatenate([db00, db01, db10, db11], axis=0)
    G["pool_b"] = dpb
    G["pool_scale"] = dscale
    G["gdn_conv"] = dconv
    G["gdn_a_log"] = dalog[:, :H]
    G["gdn_dt_bias"] = ddt[:, :H]
    G["gdn_norm_w"] = dnw
    G["ple_gate_b"] = jnp.concatenate([dbg_0, dbg_1], axis=0)
    return loss_cols, grad_x, G


_HBM = pl.BlockSpec(memory_space=pltpu.HBM)


def _all_gather(name, shards):
    T = len(shards)

    def body(*refs):
        ins, outs = refs[:T], refs[T : 2 * T]
        send_sems, recv_sems, local_sems = refs[2 * T :]
        x, y, c = lax.axis_index("x"), lax.axis_index("y"), lax.axis_index("c")
        me, sibling = (x, y, c), (x, y, 1 - c)
        chips = [(1 - x, y), (x, 1 - y), (1 - x, 1 - y)]

        def blk(t, px, py, pc):
            return outs[t].at[4 * px + 2 * py + pc]

        def copy(t, k, block, to, src=None):
            return pltpu.make_async_remote_copy(
                src_ref=blk(t, *block) if src is None else src, dst_ref=blk(t, *block),
                send_sem=send_sems.at[t, k], recv_sem=recv_sems.at[t, k], device_id=to, device_id_type=MESH,
            )

        mine = [pltpu.make_async_copy(ins[t], blk(t, *me), local_sems.at[t]) for t in range(T)]
        for cp in mine:
            cp.start()
        first = []
        for t in range(T):
            first.append(copy(t, 0, me, sibling, src=ins[t]))
            first += [copy(t, 1 + j, me, (*chip, c), src=ins[t]) for j, chip in enumerate(chips)]
        for cp in first:
            cp.start()
        passed = []
        for j, chip in enumerate(chips):
            for t in range(T):
                copy(t, 1 + j, (*chip, c), me).wait_recv()
                fw = copy(t, 4 + j, (*chip, c), sibling)
                fw.start()
                passed.append(fw)
        for t in range(T):
            copy(t, 0, sibling, me).wait_recv()
            for j, chip in enumerate(chips):
                copy(t, 4 + j, (*chip, 1 - c), me).wait_recv()
        for cp in first + passed:
            cp.wait_send()
        for cp in mine:
            cp.wait()

    return pl.pallas_call(
        body,
        name=name,
        in_specs=[_HBM] * T,
        out_specs=[_HBM] * T,
        out_shape=[jax.ShapeDtypeStruct((N_DEV,) + s.shape, s.dtype) for s in shards],
        scratch_shapes=[pltpu.SemaphoreType.DMA((T, 7)), pltpu.SemaphoreType.DMA((T, 7)), pltpu.SemaphoreType.DMA((T,))],
    )(*shards)


def _exchange(name, blocks):
    def body(g_ref, o_ref, send_sems, recv_sems, local_sem):
        x, y, c = lax.axis_index("x"), lax.axis_index("y"), lax.axis_index("c")
        own = pltpu.make_async_copy(g_ref.at[4 * x + 2 * y + c], o_ref.at[N_DEV - 1], local_sem)
        own.start()
        copies = []
        for rel in range(1, N_DEV):
            px = 1 - x if rel & 4 else x
            py = 1 - y if rel & 2 else y
            pc = 1 - c if rel & 1 else c
            copies.append(
                pltpu.make_async_remote_copy(
                    src_ref=g_ref.at[4 * px + 2 * py + pc], dst_ref=o_ref.at[rel - 1],
                    send_sem=send_sems.at[rel - 1], recv_sem=recv_sems.at[rel - 1], device_id=(px, py, pc), device_id_type=MESH,
                )
            )
        for cp in copies:
            cp.start()
        for cp in copies:
            cp.wait_recv()
        for cp in copies:
            cp.wait_send()
        own.wait()

    return pl.pallas_call(
        body,
        name=name,
        in_specs=[_HBM],
        out_specs=_HBM,
        out_shape=jax.ShapeDtypeStruct(blocks.shape, blocks.dtype),
        scratch_shapes=[pltpu.SemaphoreType.DMA((N_DEV - 1,)), pltpu.SemaphoreType.DMA((N_DEV - 1,)), pltpu.SemaphoreType.DMA],
    )(blocks)


_SEM = pl.BlockSpec(memory_space=pltpu.SEMAPHORE)
_ANY = pl.BlockSpec(memory_space=pl.ANY)
_DATAFLOW = pltpu.SideEffectType.DATAFLOW_SIDE_EFFECTING
N_PEERS = N_DEV - 1


def _peer(rel, x, y, c):
    return (1 - x if rel & 4 else x, 1 - y if rel & 2 else y, 1 - c if rel & 1 else c)


def _send_start(name, srcs, lands, gather, after):
    T = len(srcs)

    def body(*refs):
        src_refs, land_refs = refs[:T], refs[T : 2 * T]
        send_sems, recv_sems = refs[2 * T + 1], refs[2 * T + 2]
        token = refs[-1]
        x, y, c = lax.axis_index("x"), lax.axis_index("y"), lax.axis_index("c")
        for t in range(T):
            for rel in range(1, N_DEV):
                px, py, pc = _peer(rel, x, y, c)
                pltpu.make_async_remote_copy(
                    src_ref=src_refs[t] if gather else src_refs[t].at[4 * px + 2 * py + pc],
                    dst_ref=land_refs[t].at[4 * x + 2 * y + c] if gather else land_refs[t].at[rel - 1],
                    send_sem=send_sems.at[t * N_PEERS + rel - 1], recv_sem=recv_sems.at[t * N_PEERS + rel - 1], device_id=(px, py, pc), device_id_type=MESH,
                ).start()
        token[...] = jnp.zeros_like(token)

    hbm = lambda a: pltpu.HBM(a.shape, a.dtype)
    return pl.pallas_call(
        body,
        name=name,
        out_shape=(pltpu.SemaphoreType.DMA((T * N_PEERS,)), pltpu.SemaphoreType.DMA((T * N_PEERS,)), *[hbm(a) for a in srcs],
                   *[hbm(a) for a in lands], jax.ShapeDtypeStruct((8, LANES), F32)),
        in_specs=(_HBM,) * (2 * T) + (_ANY,),
        out_specs=(_SEM, _SEM) + (_HBM,) * (2 * T) + (pl.BlockSpec(memory_space=pltpu.VMEM),),
        input_output_aliases={t: 2 + t for t in range(2 * T)},
        compiler_params=pltpu.CompilerParams(has_side_effects=_DATAFLOW),
    )(*[pltpu.with_memory_space_constraint(a, pltpu.HBM) for a in list(srcs) + list(lands)], after)


def _send_wait(name, started, after, gather):
    T = (len(started) - 3) // 2
    send_sems, recv_sems, token = started[0], started[1], started[-1]
    thru = started[2:-1]

    def body(*refs):
        src_refs, land_refs = refs[:T], refs[T : 2 * T]
        send_sems, recv_sems = refs[2 * T], refs[2 * T + 1]
        x, y, c = lax.axis_index("x"), lax.axis_index("y"), lax.axis_index("c")
        for t in range(T):
            for rel in range(1, N_DEV):
                cp = pltpu.make_async_remote_copy(
                    src_ref=src_refs[t] if gather else src_refs[t].at[0], dst_ref=land_refs[t].at[0],
                    send_sem=send_sems.at[t * N_PEERS + rel - 1], recv_sem=recv_sems.at[t * N_PEERS + rel - 1], device_id=_peer(rel, x, y, c), device_id_type=MESH,
                )
                cp.wait_send()
                cp.wait_recv()

    outs = pl.pallas_call(
        body,
        name=name,
        out_shape=tuple(pltpu.HBM(a.shape, a.dtype) for a in thru),
        in_specs=(_HBM,) * (2 * T) + (_SEM, _SEM, _ANY),
        out_specs=(_HBM,) * (2 * T),
        input_output_aliases={t: t for t in range(2 * T)},
        compiler_params=pltpu.CompilerParams(has_side_effects=_DATAFLOW),
    )(*thru, send_sems, recv_sems, after)
    return list(outs[:T]), list(outs[T:])


def _sum_blocks(name, parts, tr):
    _, R, Cw = parts[0].shape
    tr = tr if R % tr == 0 else R

    def body(*refs):
        acc = None
        for p_ref in refs[:-1]:
            for d in range(p_ref.shape[0]):
                v = p_ref[d].astype(F32)
                acc = v if acc is None else acc + v
        refs[-1][...] = acc

    return pl.pallas_call(
        body,
        name=name,
        grid=(R // tr,),
        in_specs=[pl.BlockSpec((a.shape[0], tr, Cw), lambda i: (0, i, 0)) for a in parts],
        out_specs=pl.BlockSpec((tr, Cw), lambda i: (i, 0)),
        out_shape=jax.ShapeDtypeStruct((R, Cw), F32),
        compiler_params=_params(1),
    )(*parts)


def _adamw(name, w, g, m, v):
    shape = w.shape
    cols = shape[-1]
    rows = w.size // cols
    tr = rows if rows <= 512 else 512
    assert rows % tr == 0
    w2, g2, m2, v2 = (a.reshape(rows, cols) for a in (w, g, m, v))

    def body(w_ref, g_ref, m_ref, v_ref, d_ref, mo_ref, vo_ref):
        gv = g_ref[...]
        mn = ADAM_B1 * m_ref[...] + (1.0 - ADAM_B1) * gv
        vn = ADAM_B2 * v_ref[...] + (1.0 - ADAM_B2) * jnp.square(gv)
        m_hat = mn / (1.0 - ADAM_B1**ADAM_STEP)
        v_hat = vn / (1.0 - ADAM_B2**ADAM_STEP)
        d_ref[...] = -ADAM_LR * (m_hat / (jnp.sqrt(v_hat) + ADAM_EPS) + ADAM_WD * w_ref[...])
        mo_ref[...] = mn
        vo_ref[...] = vn

    spec = pl.BlockSpec((tr, cols), lambda i: (i, 0))
    d, mn, vn = pl.pallas_call(
        body,
        name=name,
        grid=(rows // tr,),
        in_specs=[spec] * 4,
        out_specs=[spec] * 3,
        out_shape=[jax.ShapeDtypeStruct((rows, cols), F32)] * 3,
        compiler_params=_params(1),
    )(w2, g2, m2, v2)
    return d.reshape(shape), mn.reshape(shape), vn.reshape(shape)


SMALL_SHARDED = ("ln_gain", "ln_bias", "pool_b", "gdn_conv")
SMALL_REPLICATED = ("pool_scale", "gdn_a_log", "gdn_dt_bias", "gdn_norm_w", "ple_gate_b")
WEIGHTS = ("ln_gain", "ln_bias", "pool_w", "pool_b", "pool_scale", "gdn_w_in", "gdn_conv", "gdn_a_log", "gdn_dt_bias",
           "gdn_norm_w", "gdn_w_out", "mlp_w1", "mlp_w2", "ple_gate_w", "ple_gate_b", "ple_proj")
BIG_AXIS = {"gdn_w_in": 1, "gdn_w_out": 0, "mlp_w1": 1, "mlp_w2": 0, "ple_gate_w": 0, "ple_proj": 1, "pool_w": 1}
GATHER_GROUPS = {
    "l0a": (("mlp_w1", 0),),
    "l0b": (("ple_gate_w", 0), ("ple_proj", 0)),
    "l0c": (("mlp_w2", 0),),
    "gdn": (("gdn_w_in", 0),),
    "gdo": (("gdn_w_out", 0),),
    "l1": (("mlp_w1", 1), ("mlp_w2", 1), ("ple_gate_w", 1), ("ple_proj", 1)),
}
GATHER_AFTER = {"l0b": "l0a", "l0c": "l0a", "gdn": "l0c", "gdo": "gdn", "l1": "gdn"}
GRAD_GROUPS = {
    "l1": (("mlp_w1", 1), ("mlp_w2", 1), ("ple_gate_w", 1), ("ple_proj", 1)),
    "gdn": (("gdn_w_in", 0), ("gdn_w_out", 0)),
    "l0": (("mlp_w1", 0), ("mlp_w2", 0), ("ple_gate_w", 0), ("ple_proj", 0)),
}
PACK_PART_ALIGN = 16
SUM_TILE = 128


def _part_rows(a, width):
    rows = a.size // width
    return rows + (-rows) % PACK_PART_ALIGN


def _pack_rows(parts, width, dtype, align):
    padded = []
    for a in parts:
        a2 = a.reshape(-1, width).astype(dtype)
        padded.append(jnp.pad(a2, ((0, _part_rows(a, width) - a2.shape[0]), (0, 0))))
    flat = jnp.concatenate(padded, axis=0)
    return jnp.pad(flat, ((0, (-flat.shape[0]) % align), (0, 0)))


def _pack_blocks(parts, width, dtype, align):
    padded = []
    for a in parts:
        a2 = a.reshape(a.shape[0], -1, width).astype(dtype)
        padded.append(jnp.pad(a2, ((0, 0), (0, _part_rows(a[0], width) - a2.shape[1]), (0, 0))))
    flat = jnp.concatenate(padded, axis=1)
    return jnp.pad(flat, ((0, 0), (0, (-flat.shape[1]) % align), (0, 0)))


def _unpack_rows(packed, shapes, width):
    out, off = [], 0
    for shp in shapes:
        size = 1
        for d in shp:
            size *= d
        out.append(packed[..., off : off + size // width, :].reshape(packed.shape[:-2] + tuple(shp)))
        off += size // width + (-(size // width)) % PACK_PART_ALIGN
    return out


def _split_blocks(name, full):
    ax = BIG_AXIS[name]
    shp = full.shape
    a = full.reshape(shp[:ax] + (N_DEV, shp[ax] // N_DEV) + shp[ax + 1 :])
    return jnp.moveaxis(a, ax, 0)


def _join_blocks(name, blocks):
    ax = BIG_AXIS[name]
    a = jnp.moveaxis(blocks, 0, ax)
    shp = a.shape
    return a.reshape(shp[:ax] + (shp[ax] * shp[ax + 1],) + shp[ax + 2 :])


def _pack_small(parts):
    flat = jnp.concatenate([jnp.pad(a.reshape(-1), (0, (-a.size) % LANES)) for a in parts])
    rows = flat.size // LANES
    return jnp.pad(flat.reshape(rows, LANES), ((0, (-rows) % 8), (0, 0)))


def _unpack_small(packed, shapes):
    flat = packed.reshape(packed.shape[:-2] + (-1,))
    out, off = [], 0
    for shp in shapes:
        size = 1
        for s in shp:
            size *= s
        out.append(flat[..., off : off + size].reshape(flat.shape[:-1] + tuple(shp)))
        off += size + (-size) % LANES
    return out


def _split_w_in(w_in, D, H):
    pad = lambda a: jnp.pad(a, ((0, 0), (0, LANES - H)))
    return w_in[:, : 3 * D], w_in[:, 3 * D : 4 * D], jnp.concatenate([pad(w_in[:, 4 * D : 4 * D + H]), pad(w_in[:, 4 * D + H :])], axis=1)


def kernel(x, p, ln_gain, ln_bias, pool_w, pool_b, pool_scale, gdn_w_in, gdn_conv, gdn_a_log, gdn_dt_bias, gdn_norm_w, gdn_w_out, mlp_w1, mlp_w2, ple_gate_w, ple_gate_b, ple_proj, loss_target, m_ln_gain, m_ln_bias, m_pool_w, m_pool_b, m_pool_scale, m_gdn_w_in, m_gdn_conv, m_gdn_a_log, m_gdn_dt_bias, m_gdn_norm_w, m_gdn_w_out, m_mlp_w1, m_mlp_w2, m_ple_gate_w, m_ple_gate_b, m_ple_proj, v_ln_gain, v_ln_bias, v_pool_w, v_pool_b, v_pool_scale, v_gdn_w_in, v_gdn_conv, v_gdn_a_log, v_gdn_dt_bias, v_gdn_norm_w, v_gdn_w_out, v_mlp_w1, v_mlp_w2, v_ple_gate_w, v_ple_gate_b, v_ple_proj):
    w_sh = dict(ln_gain=ln_gain, ln_bias=ln_bias, pool_w=pool_w, pool_b=pool_b, pool_scale=pool_scale, gdn_w_in=gdn_w_in,
                gdn_conv=gdn_conv, gdn_a_log=gdn_a_log, gdn_dt_bias=gdn_dt_bias, gdn_norm_w=gdn_norm_w, gdn_w_out=gdn_w_out,
                mlp_w1=mlp_w1, mlp_w2=mlp_w2, ple_gate_w=ple_gate_w, ple_gate_b=ple_gate_b, ple_proj=ple_proj)
    m_sh = dict(ln_gain=m_ln_gain, ln_bias=m_ln_bias, pool_w=m_pool_w, pool_b=m_pool_b, pool_scale=m_pool_scale, gdn_w_in=m_gdn_w_in,
                gdn_conv=m_gdn_conv, gdn_a_log=m_gdn_a_log, gdn_dt_bias=m_gdn_dt_bias, gdn_norm_w=m_gdn_norm_w, gdn_w_out=m_gdn_w_out,
                mlp_w1=m_mlp_w1, mlp_w2=m_mlp_w2, ple_gate_w=m_ple_gate_w, ple_gate_b=m_ple_gate_b, ple_proj=m_ple_proj)
    v_sh = dict(ln_gain=v_ln_gain, ln_bias=v_ln_bias, pool_w=v_pool_w, pool_b=v_pool_b, pool_scale=v_pool_scale, gdn_w_in=v_gdn_w_in,
                gdn_conv=v_gdn_conv, gdn_a_log=v_gdn_a_log, gdn_dt_bias=v_gdn_dt_bias, gdn_norm_w=v_gdn_norm_w, gdn_w_out=v_gdn_w_out,
                mlp_w1=v_mlp_w1, mlp_w2=v_mlp_w2, ple_gate_w=v_ple_gate_w, ple_gate_b=v_ple_gate_b, ple_proj=v_ple_proj)
    xs, tg = x[0], loss_target[0]
    ps = p[:, 0]
    S, D = xs.shape
    H = D // HEAD_DIM
    me = 4 * lax.axis_index("x") + 2 * lax.axis_index("y") + lax.axis_index("c")
    layer = lambda n, l: (w_sh[n][0] if n in ("gdn_w_in", "gdn_w_out") else w_sh[n][l])

    pool_packed = _pack_rows([w_sh["pool_w"][0]], D, BF16, PACK_PART_ALIGN)
    small_packed = _pack_small([w_sh[n] for n in SMALL_SHARDED])
    pool_gathered, small_gathered = _all_gather("gather_first", [pool_packed, small_packed])
    W = {"pool_w": _join_blocks("pool_w", _unpack_rows(pool_gathered, [w_sh["pool_w"][0].shape], D)[0])}

    started = {}

    def start(g, after):
        srcs = [layer(n, l).astype(BF16) for n, l in GATHER_GROUPS[g]]
        started[g] = tuple(_send_start(f"gather_{g}_start", srcs, [lax.empty((N_DEV,) + a.shape, BF16) for a in srcs], True, after))
        return started[g][-1]

    first_token = start("l0a", small_gathered)
    smalls = _unpack_small(small_gathered, [w_sh[n].shape for n in SMALL_SHARDED])
    for n, a in zip(SMALL_SHARDED, smalls):
        W[n] = jnp.moveaxis(a, 0, -2).reshape(a.shape[1:-1] + (N_DEV * a.shape[-1],))
    W["ln_gain"] = W["ln_gain"].reshape(2 * DEPTH, D)
    W["ln_bias"] = W["ln_bias"].reshape(2 * DEPTH, D)
    W["pool_b"] = W["pool_b"].reshape(1, D) + first_token[0:1, 0:1]
    W["gdn_conv"] = W["gdn_conv"][0]
    W["pool_scale"] = pool_scale
    W["ple_gate_b"] = ple_gate_b
    W["gdn_norm_w"] = gdn_norm_w
    W["gdn_a_log"] = jnp.pad(gdn_a_log, ((0, 0), (0, LANES - H)))
    W["gdn_dt_bias"] = jnp.pad(gdn_dt_bias, ((0, 0), (0, LANES - H)))

    def fetch(g, after):
        members = GATHER_GROUPS[g]
        srcs, lands = _send_wait(f"gather_{g}_wait", started[g], after, True)
        tokens = [start(nxt, lands[0]) for nxt, prev in GATHER_AFTER.items() if prev == g]
        out = {n: _join_blocks(n, lax.dynamic_update_index_in_dim(land, src, me, 0)) for (n, _), src, land in zip(members, srcs, lands)}
        if "gdn_w_in" in out:
            out["gdn_wqkv"], out["gdn_wz"], out["gdn_wba"] = _split_w_in(out.pop("gdn_w_in"), D, H)
        out["_after"] = tokens
        return out

    sent = {}

    def emit(g, grads):
        srcs = [grads[n] for n, _ in GRAD_GROUPS[g]]
        lands = [lax.empty((N_PEERS,) + a.shape[1:], BF16) for a in srcs]
        sent[g] = tuple(_send_start(f"grads_{g}_start", srcs, lands, False, srcs[0]))
        return sent[g][-1]

    loss_cols, grad_x, G = _local_step(xs, ps, tg, W, fetch, emit)
    loss = lax.psum(0.5 * jnp.sum(loss_cols) / D, MESH_AXES)

    pool_src = _pack_blocks([_split_blocks("pool_w", G["pool_w"])], D, BF16, PACK_PART_ALIGN)
    pool_sum = _sum_blocks("sum_pool_grads", [_exchange("exchange_pool_grads", pool_src)], SUM_TILE)
    grads = {"pool_w": _unpack_rows(pool_sum, [w_sh["pool_w"][0].shape], D)[0].reshape(w_sh["pool_w"].shape)}
    small_names = SMALL_SHARDED + SMALL_REPLICATED
    gs_packed = _pack_small([G[n] for n in small_names])
    (gs_all,) = _all_gather("gather_small_grads", [gs_packed])
    gs_sum = _sum_blocks("sum_small_grads", [gs_all], SUM_TILE)
    for n, a in zip(small_names, _unpack_small(gs_sum, [G[n].shape for n in small_names])):
        if n in SMALL_SHARDED:
            width = w_sh[n].shape[-1]
            a = a.reshape(w_sh[n].shape[:-1] + (N_DEV * width,))
            a = lax.dynamic_slice_in_dim(a, me * width, width, axis=a.ndim - 1)
        grads[n] = a.reshape(w_sh[n].shape)

    per_layer = {}
    for g, members in GRAD_GROUPS.items():
        srcs, lands = _send_wait(f"grads_{g}_wait", sent[g], grad_x, False)
        for (n, l), src, land in zip(members, srcs, lands):
            own = lax.dynamic_index_in_dim(src, me, 0, keepdims=True)
            as3d = lambda a: a.reshape(a.shape[0], -1, a.shape[-1])
            per_layer[(n, l)] = _sum_blocks(f"sum_grads_{n}_{l}", [as3d(land), as3d(own)], SUM_TILE).reshape(layer(n, l).shape)
    for n in ("gdn_w_in", "gdn_w_out"):
        grads[n] = per_layer[(n, 0)][None]
    for n in ("mlp_w1", "mlp_w2", "ple_gate_w", "ple_proj"):
        grads[n] = jnp.stack([per_layer[(n, 0)], per_layer[(n, 1)]])

    deltas, new_m, new_v = {}, {}, {}
    for n in WEIGHTS:
        deltas[n], new_m[n], new_v[n] = _adamw(f"adamw_{n}", w_sh[n], grads[n], m_sh[n], v_sh[n])
    return (loss, grad_x[None], *[grads[n] for n in WEIGHTS], *[deltas[n] for n in WEIGHTS],
            *[new_m[n] for n in WEIGHTS], *[new_v[n] for n in WEIGHTS])
```

```python
import functools

import jax
import jax.numpy as jnp
from jax import lax
from jax.experimental import pallas as pl
from jax.experimental.pallas import tpu as pltpu

F32 = jnp.float32
BF16 = jnp.bfloat16
MESH_AXES = ("x", "y", "c")
N_DEV = 8
MESH = pl.DeviceIdType.MESH

DEPTH = 2
ALPHA = (2.0 * DEPTH) ** 0.25
LN_EPS = 1e-5
RMS_EPS = 1e-6
L2_EPS = 1e-6
HEAD_DIM = 128
CONV_WIDTH = 4
POOL_WINDOWS = (2, 4, 8, 16)
POOL_HALO = 16
CONV_HALO = 8
LANES = 128
ADAM_LR = 0.001
ADAM_B1 = 0.9
ADAM_B2 = 0.999
ADAM_EPS = 1e-08
ADAM_WD = 0.01
ADAM_STEP = 10

VMEM_LIMIT = 56 * 1024 * 1024
ROW_TILE = 512
CONV_TILE = 256
CHUNK = 128
MM_TM, MM_TN, MM_TK = 512, 1024, 1024
DW_TILES = dict(tm=512, tn=512, tk=8192, b_outer=True)

_DIMS = {
    "nn": (((1,), (0,)), ((), ())),
    "nt": (((1,), (1,)), ((), ())),
    "tn": (((0,), (0,)), ((), ())),
}


def _params(n_axes):
    return pltpu.CompilerParams(dimension_semantics=("arbitrary",) * n_axes, vmem_limit_bytes=VMEM_LIMIT)


def _fit(tile, n):
    tile = min(tile, n)
    while n % tile:
        tile //= 2
    return tile


def _mm(name, a, b, mode, out_dtypes, epi=None, extras=(), a_fn=None, tm=None, tn=None, tk=None, b_outer=False, after=(), out_blocks=1, accs=0):
    if mode == "tn":
        K, M = a.shape
    else:
        M, K = a.shape
    N = b.shape[0] if mode == "nt" else b.shape[1]
    tm, tn, tk = _fit(tm or MM_TM, M), (N // out_blocks if out_blocks > 1 else N if accs else _fit(tn or MM_TN, N)), _fit(tk or MM_TK, K)
    nk = K // tk

    def at(f):
        return (lambda j, i, k: f(i, j, k)) if b_outer else f

    a_spec = pl.BlockSpec((tk, tm), at(lambda i, j, k: (k, i))) if mode == "tn" else pl.BlockSpec((tm, tk), at(lambda i, j, k: (i, k)))
    b_spec = pl.BlockSpec((tn, tk), at(lambda i, j, k: (j, k))) if mode == "nt" else pl.BlockSpec((tk, tn), at(lambda i, j, k: (k, j)))
    ex_spec = {"tile": pl.BlockSpec((tm, tn), at(lambda i, j, k: (i, j))), "row": pl.BlockSpec((1, tn), at(lambda i, j, k: (0, j))),
               "rows": pl.BlockSpec((tm, LANES), at(lambda i, j, k: (i, 0)))}
    ex_specs = [pl.BlockSpec(e.shape, lambda i, j, k: (0, 0)) if kind == "whole" else ex_spec[kind] for e, kind in extras]
    assert accs == 0 or (tn == N and nk == 1 and not b_outer), name
    n_ex, n_out, n_after = len(extras), len(out_dtypes), len(after)

    def body(*refs):
        a_ref, b_ref = refs[0], refs[1]
        ex_refs = refs[2 : 2 + n_ex]
        out_refs = refs[2 + n_ex + n_after : 2 + n_ex + n_after + n_out]
        av = a_ref[...]
        if a_fn is not None:
            av = a_fn(av)
        part = lax.dot_general(av.astype(BF16), b_ref[...].astype(BF16), _DIMS[mode], preferred_element_type=F32)

        def finish(res):
            vals = epi(res, *[e[...] for e in ex_refs]) if epi is not None else (res,)
            for o_ref, v in zip(out_refs, vals[:n_out]):
                o_ref[...] = v.astype(o_ref.dtype)
            for a_ref, v in zip(refs[2 + n_ex + n_after + n_out :], vals[n_out:]):

                @pl.when(pl.program_id(0) == 0)
                def _(a_ref=a_ref, v=v):
                    a_ref[...] = v

                @pl.when(pl.program_id(0) > 0)
                def _(a_ref=a_ref, v=v):
                    a_ref[...] += v

        if nk == 1:
            finish(part)
        else:
            acc = refs[-1]
            k = pl.program_id(2)

            @pl.when(k == 0)
            def _():
                acc[...] = part

            @pl.when(k > 0)
            def _():
                acc[...] += part

            @pl.when(k == nk - 1)
            def _():
                finish(acc[...])

    outs = pl.pallas_call(
        body,
        name=name,
        grid=(N // tn, M // tm, nk) if b_outer else (M // tm, N // tn, nk),
        in_specs=[a_spec, b_spec] + ex_specs + [pl.BlockSpec(memory_space=pl.ANY)] * n_after,
        out_specs=[pl.BlockSpec((tm, LANES), at(lambda i, j, k: (i, 0))) if isinstance(dt, tuple)
                   else pl.BlockSpec((tm, tn), at(lambda i, j, k: (i, j))) if out_blocks == 1
                   else pl.BlockSpec((None, tm, tn), at(lambda i, j, k: (j, i, 0))) for dt in out_dtypes]
        + [pl.BlockSpec((1, N), lambda i, j, k: (0, 0))] * accs,
        out_shape=[jax.ShapeDtypeStruct((M, LANES), dt[0]) if isinstance(dt, tuple)
                   else jax.ShapeDtypeStruct((M, N) if out_blocks == 1 else (out_blocks, M, tn), dt) for dt in out_dtypes]
        + [jax.ShapeDtypeStruct((1, N), F32)] * accs,
        scratch_shapes=[pltpu.VMEM((tm, tn), F32)] if nk > 1 else [],
        compiler_params=_params(3),
    )(a, b, *[e for e, _ in extras], *after)
    return outs[0] if n_out + accs == 1 else outs


def _rowwise(name, fn, S, ts, rows=(), halos=(), consts=(), outs=(), accs=()):
    ts = min(ts, S)
    assert S % ts == 0
    n = S // ts
    in_specs = [pl.BlockSpec((ts, a.shape[1]), lambda i: (i, 0)) for a in rows]
    for a, kind, hr in halos:
        r, nb = ts // hr, S // hr
        if kind == "prev":
            in_specs.append(pl.BlockSpec((hr, a.shape[1]), lambda i, r=r: (jnp.maximum(i * r - 1, 0), 0)))
        else:
            in_specs.append(pl.BlockSpec((hr, a.shape[1]), lambda i, r=r, nb=nb: (jnp.minimum((i + 1) * r, nb - 1), 0)))
    in_specs += [pl.BlockSpec(a.shape, lambda i, nd=a.ndim: (0,) * nd) for a in consts]
    out_specs = [pl.BlockSpec((ts, w), lambda i: (i, 0)) for w, _ in outs]
    out_specs += [pl.BlockSpec((r, w), lambda i: (0, 0)) for r, w in accs]
    out_shape = [jax.ShapeDtypeStruct((S, w), dt) for w, dt in outs]
    out_shape += [jax.ShapeDtypeStruct((r, w), F32) for r, w in accs]
    nr, nh, nc, no = len(rows), len(halos), len(consts), len(outs)

    def body(*refs):
        i = pl.program_id(0)
        rv = [r[...] for r in refs[:nr]]
        hv = [r[...] for r in refs[nr : nr + nh]]
        cv = [r[...] for r in refs[nr + nh : nr + nh + nc]]
        o_refs = refs[nr + nh + nc : nr + nh + nc + no]
        a_refs = refs[nr + nh + nc + no :]
        ovals, avals = fn(i, n, rv, hv, cv)
        for o_ref, v in zip(o_refs, ovals):
            o_ref[...] = v.astype(o_ref.dtype)
        for a_ref, v in zip(a_refs, avals):

            @pl.when(i == 0)
            def _(a_ref=a_ref, v=v):
                a_ref[...] = v

            @pl.when(i > 0)
            def _(a_ref=a_ref, v=v):
                a_ref[...] += v

    res = pl.pallas_call(
        body,
        name=name,
        grid=(n,),
        in_specs=in_specs,
        out_specs=out_specs,
        out_shape=out_shape,
        compiler_params=_params(1),
    )(*rows, *[h[0] for h in halos], *consts)
    return list(res)


def _ln(h, g, b):
    mu = jnp.mean(h, axis=-1, keepdims=True)
    d = h - mu
    var = jnp.mean(d * d, axis=-1, keepdims=True)
    rstd = lax.rsqrt(var + LN_EPS)
    xhat = d * rstd
    return xhat, rstd, xhat * g + b


def _ln_bwd(dy, xhat, rstd, g):
    dxh = dy * g
    m1 = jnp.mean(dxh, axis=-1, keepdims=True)
    m2 = jnp.mean(dxh * xhat, axis=-1, keepdims=True)
    dh = rstd * (dxh - m1 - xhat * m2)
    return dh, jnp.sum(dy * xhat, axis=0, keepdims=True), jnp.sum(dy, axis=0, keepdims=True)


def _wide(col, ts):
    return jnp.broadcast_to(col, (ts, LANES))


def _pool_fwd(x, wp, pb, ps, g, b):
    S, D = x.shape
    gw = D // len(POOL_WINDOWS)
    ts = min(ROW_TILE, S)

    def fn(i, n, rv, hv, cv):
        (xc,), (xp,) = rv, hv
        wpv, pbv, psv, gv, bv = cv
        xp = jnp.where(i > 0, xp, 0.0)
        xx = jnp.concatenate([xp, xc], axis=0)
        t = i * ts + lax.broadcasted_iota(jnp.int32, (ts, 1), 0)
        pooled, ys = [], []
        for gi, w in enumerate(POOL_WINDOWS):
            s = xx[:, gi * gw : (gi + 1) * gw]
            k = 1
            while k < w:
                s = s + pltpu.roll(s, k, axis=0)
                k *= 2
            cnt = jnp.minimum(t + 1, w).astype(F32)
            pg = (s[POOL_HALO:, :] / cnt - xc[:, gi * gw : (gi + 1) * gw]).astype(BF16)
            pooled.append(pg)
            ys.append(jnp.dot(pg, wpv[gi], preferred_element_type=F32))
        y = jnp.concatenate(ys, axis=1)
        h = ALPHA * xc + (y + pbv) * psv
        xhat, rstd, xa = _ln(h, gv, bv)
        return (jnp.concatenate(pooled, axis=1), xhat, _wide(rstd, ts), xa), ()

    return _rowwise(
        "pool_fwd", fn, S, ts, rows=[x], halos=[(x, "prev", POOL_HALO)], consts=[wp, pb, ps, g, b],
        outs=[(D, BF16), (D, F32), (LANES, F32), (D, BF16)],
    )


def _pool_bwd(dh, pooled, wp, pb, ps):
    S, D = dh.shape
    gw = D // len(POOL_WINDOWS)
    ts = min(ROW_TILE, S)
    te = ts + POOL_HALO

    def fn(i, n, rv, hv, cv):
        (dhc, pc), (dhn,) = rv, hv
        wpv, pbv, psv = cv
        dhn = jnp.where(i < n - 1, dhn, 0.0)
        dy_ext = jnp.concatenate([dhc, dhn], axis=0) * psv
        dyb = dy_ext.astype(BF16)
        t = i * ts + lax.broadcasted_iota(jnp.int32, (te, 1), 0)
        dxs, ys = [], []
        for gi, w in enumerate(POOL_WINDOWS):
            sl = slice(gi * gw, (gi + 1) * gw)
            dp = lax.dot_general(dyb[:, sl], wpv[gi], _DIMS["nt"], preferred_element_type=F32)
            s = dp / jnp.minimum(t + 1, w).astype(F32)
            k = 1
            while k < w:
                s = s + pltpu.roll(s, k, axis=0)
                k *= 2
            s = pltpu.roll(s, POOL_HALO - (w - 1), axis=0)
            dxs.append(s[POOL_HALO:, :] - dp[:ts, :])
            ys.append(jnp.dot(pc[:, sl], wpv[gi], preferred_element_type=F32))
        dx = ALPHA * dhc + jnp.concatenate(dxs, axis=1)
        y = jnp.concatenate(ys, axis=1) + pbv
        dscale = jnp.sum(dhc * y, axis=0, keepdims=True)
        dbias = jnp.sum(dy_ext[:ts, :], axis=0, keepdims=True)
        return (dx, dyb[:ts, :]), (dscale, dbias)

    return _rowwise(
        "pool_bwd", fn, S, ts, rows=[dh, pooled], halos=[(dh, "next", POOL_HALO)], consts=[wp, pb, ps],
        outs=[(D, F32), (D, BF16)], accs=[(1, D), (1, D)],
    )


def _pool_dw(pooled, dy):
    S, D = pooled.shape
    G = len(POOL_WINDOWS)
    gw = D // G
    tk = min(MM_TK, S)
    nk = S // tk

    def body(p_ref, d_ref, o_ref):
        k = pl.program_id(1)
        part = lax.dot_general(p_ref[...], d_ref[...], _DIMS["tn"], preferred_element_type=F32)

        @pl.when(k == 0)
        def _():
            o_ref[...] = part

        @pl.when(k > 0)
        def _():
            o_ref[...] += part

    return pl.pallas_call(
        body,
        name="pool_dw",
        grid=(G, nk),
        in_specs=[pl.BlockSpec((tk, gw), lambda g, k: (k, g)), pl.BlockSpec((tk, gw), lambda g, k: (k, g))],
        out_specs=pl.BlockSpec((None, gw, gw), lambda g, k: (g, 0, 0)),
        out_shape=jax.ShapeDtypeStruct((G, gw, gw), F32),
        compiler_params=_params(2),
    )(pooled, dy)


def _res_ln_epi(acc, xh, gp_, bp_, g, b):
    xhat, rstd, xo = _ln(ALPHA * (xh * gp_ + bp_) + acc, g, b)
    return xhat, _wide(rstd, acc.shape[0]), xo


def _res_ln_ffpe_epi(acc, xh, gate, pp, gp_, bp_, g, b):
    return _res_ln_epi(acc + jax.nn.sigmoid(gate) * pp, xh, gp_, bp_, g, b)


def _final_ln_loss_epi(acc, xh, gate, pp, tgt, gp_, bp_, g, b):
    sg = jax.nn.sigmoid(gate)
    xhat, rstd, y = _ln(ALPHA * (xh * gp_ + bp_) + acc + sg * pp, g, b)
    e = y - tgt
    dh, dg, db = _ln_bwd(e * (1.0 / acc.shape[1]), xhat, rstd, g)
    dgt, dpp, dbg = _ple_grads(dh, sg, pp)
    return dh, dh, dgt, dpp, jnp.sum(e * e, axis=0, keepdims=True), dg, db, dbg


def _ln_bwd_epi(acc, rest, xhat, rstd, g):
    dh, dg, db = _ln_bwd(acc + rest, xhat, rstd[:, :1], g)
    return dh, dh, dg, db


def _ple_grads(dh, sg, pp):
    dgt = dh * pp * sg * (1.0 - sg)
    return dgt, dh * sg, jnp.sum(dgt, axis=0, keepdims=True)


def _ln_bwd_ple_epi(acc, rest, xhat, rstd, gate, pp, g):
    dh, dg, db = _ln_bwd(acc + rest, xhat, rstd[:, :1], g)
    dgt, dpp, dbg = _ple_grads(dh, jax.nn.sigmoid(gate), pp)
    return dh, dh, dgt, dpp, dg, db, dbg


def _silu(c):
    return c * jax.nn.sigmoid(c)


def _qkv_point(c, is_qk, scale):
    s = _silu(c)
    nrm = s * lax.rsqrt(jnp.sum(s * s, axis=-1, keepdims=True) + L2_EPS) * scale
    return jnp.where(is_qk, nrm, s)


def _conv_rows(xx, wv, lo, rows):
    acc = None
    for j in range(CONV_WIDTH):
        sh = CONV_WIDTH - 1 - j
        term = (pltpu.roll(xx, sh, axis=0) if sh else xx)[lo : lo + rows, :] * wv[j : j + 1, :]
        acc = term if acc is None else acc + term
    return acc


def _conv_fwd(qkv_pre, conv_w):
    S, W = qkv_pre.shape
    D = W // 3
    H = D // HEAD_DIM
    ts = min(CONV_TILE, S)
    r = ts // CONV_HALO

    def body(x_ref, xp_ref, w_ref, o_ref):
        j, i = pl.program_id(0), pl.program_id(1)
        xp = jnp.where(i > 0, xp_ref[...], 0.0)
        xx = jnp.concatenate([xp, x_ref[...]], axis=0)
        c = _conv_rows(xx, w_ref[...], CONV_HALO, ts)
        scale = jnp.where(j == 0, HEAD_DIM**-0.5, 1.0).astype(F32)
        for h in range(H):
            sl = slice(h * HEAD_DIM, (h + 1) * HEAD_DIM)
            o_ref[:, sl] = _qkv_point(c[:, sl], j < 2, scale)

    return pl.pallas_call(
        body,
        name="gdn_conv_fwd",
        grid=(3, S // ts),
        in_specs=[
            pl.BlockSpec((ts, D), lambda j, i: (i, j)),
            pl.BlockSpec((CONV_HALO, D), lambda j, i: (jnp.maximum(i * r - 1, 0), j)),
            pl.BlockSpec((CONV_WIDTH, D), lambda j, i: (0, j)),
        ],
        out_specs=pl.BlockSpec((ts, D), lambda j, i: (i, j)),
        out_shape=jax.ShapeDtypeStruct((S, W), F32),
        compiler_params=_params(2),
    )(qkv_pre, qkv_pre, conv_w)


def _conv_bwd(qkv_pre, conv_w, dqkvn):
    S, W = qkv_pre.shape
    D = W // 3
    H = D // HEAD_DIM
    ts = min(CONV_TILE, S)
    r, nb = ts // CONV_HALO, S // CONV_HALO
    te = ts + CONV_HALO

    def body(x_ref, xp_ref, xn_ref, w_ref, d_ref, dn_ref, dx_ref, dw_ref):
        j, i = pl.program_id(0), pl.program_id(1)
        n = pl.num_programs(1)
        wv = w_ref[...]
        xp = jnp.where(i > 0, xp_ref[...], 0.0)
        xx = jnp.concatenate([xp, x_ref[...], xn_ref[...]], axis=0)
        xr = [pltpu.roll(xx, sh, axis=0) if sh else xx for sh in range(CONV_WIDTH)]
        c = None
        for jj in range(CONV_WIDTH):
            term = xr[CONV_WIDTH - 1 - jj][CONV_HALO : CONV_HALO + te, :] * wv[jj : jj + 1, :]
            c = term if c is None else c + term
        dn = jnp.where(i < n - 1, dn_ref[...], 0.0)
        dout = jnp.concatenate([d_ref[...], dn], axis=0)
        scale = jnp.where(j == 0, HEAD_DIM**-0.5, 1.0).astype(F32)
        dcs = []
        for h in range(H):
            sl = slice(h * HEAD_DIM, (h + 1) * HEAD_DIM)
            _, vjp = jax.vjp(lambda cc: _qkv_point(cc, j < 2, scale), c[:, sl])
            dcs.append(vjp(dout[:, sl])[0])
        dc = jnp.concatenate(dcs, axis=1)
        dx = None
        dws = []
        for jj in range(CONV_WIDTH):
            sh = CONV_WIDTH - 1 - jj
            term = pltpu.roll(dc, CONV_HALO - sh, axis=0)[CONV_HALO:, :] * wv[jj : jj + 1, :]
            dx = term if dx is None else dx + term
            dws.append(jnp.sum(dc[:ts, :] * xr[sh][CONV_HALO : CONV_HALO + ts, :], axis=0, keepdims=True))
        dx_ref[...] = dx.astype(dx_ref.dtype)
        dw = jnp.concatenate(dws, axis=0)

        @pl.when(i == 0)
        def _():
            dw_ref[...] = dw

        @pl.when(i > 0)
        def _():
            dw_ref[...] += dw

    return pl.pallas_call(
        body,
        name="gdn_conv_bwd",
        grid=(3, S // ts),
        in_specs=[
            pl.BlockSpec((ts, D), lambda j, i: (i, j)),
            pl.BlockSpec((CONV_HALO, D), lambda j, i: (jnp.maximum(i * r - 1, 0), j)),
            pl.BlockSpec((CONV_HALO, D), lambda j, i: (jnp.minimum((i + 1) * r, nb - 1), j)),
            pl.BlockSpec((CONV_WIDTH, D), lambda j, i: (0, j)),
            pl.BlockSpec((ts, D), lambda j, i: (i, j)),
            pl.BlockSpec((CONV_HALO, D), lambda j, i: (jnp.minimum((i + 1) * r, nb - 1), j)),
        ],
        out_specs=[pl.BlockSpec((ts, D), lambda j, i: (i, j)), pl.BlockSpec((CONV_WIDTH, D), lambda j, i: (0, j))],
        out_shape=[jax.ShapeDtypeStruct((S, W), BF16), jax.ShapeDtypeStruct((CONV_WIDTH, W), F32)],
        compiler_params=_params(2),
    )(qkv_pre, qkv_pre, qkv_pre, conv_w, dqkvn, dqkvn)


def _softplus(x):
    pos = x > 0.0
    return jnp.where(pos, x, 0.0) + jnp.log(1.0 + jnp.exp(jnp.where(pos, -x, x)))


def _gates(bl, al, alog, dt):
    return jax.nn.sigmoid(bl), -jnp.exp(alog) * _softplus(al + dt)


def _gates_fwd(ba, alog, dt):
    S = ba.shape[0]
    ts = min(ROW_TILE, S)

    def fn(i, n, rv, hv, cv):
        return _gates(rv[0][:, :LANES], rv[0][:, LANES:], cv[0], cv[1]), ()

    return _rowwise("gdn_gates_fwd", fn, S, ts, rows=[ba], consts=[alog, dt], outs=[(LANES, F32), (LANES, F32)])


def _gates_bwd(ba, alog, dt, dbeta, dg, H):
    S = ba.shape[0]
    ts = min(ROW_TILE, S)

    def fn(i, n, rv, hv, cv):
        bav, dbv, dgv = rv
        real = lax.broadcasted_iota(jnp.int32, (1, LANES), 1) < H
        _, vjp = jax.vjp(_gates, bav[:, :LANES], bav[:, LANES:], cv[0], cv[1])
        dbl, dal, dalog, ddt = vjp((jnp.where(real, dbv, 0.0), jnp.where(real, dgv, 0.0)))
        dbl, dal = jnp.where(real, dbl, 0.0), jnp.where(real, dal, 0.0)
        return (jnp.concatenate([dbl, dal], axis=1),), (jnp.where(real, dalog, 0.0), jnp.where(real, ddt, 0.0))

    return _rowwise(
        "gdn_gates_bwd", fn, S, ts, rows=[ba, dbeta, dg], consts=[alog, dt], outs=[(2 * LANES, BF16)],
        accs=[(1, LANES), (1, LANES)],
    )


def _split_bf16(a, n):
    parts, rest = [], a
    for _ in range(n):
        piece = rest.astype(BF16)
        parts.append(piece)
        rest = rest - piece.astype(F32)
    return parts


def _tri_dot(a, b, mode, tri):
    d = lambda u, v: lax.dot_general(u, v, _DIMS[mode], preferred_element_type=F32)
    if tri == 0:
        return sum(d(a.astype(BF16), piece) for piece in _split_bf16(b, 3))
    return sum(d(piece, b.astype(BF16)) for piece in _split_bf16(a, 3))


def _bdot_raw(a, b, mode):
    return lax.dot_general(a.astype(BF16), b.astype(BF16), _DIMS[mode], preferred_element_type=F32)


@functools.partial(jax.custom_vjp, nondiff_argnums=(2,))
def _bdot(a, b, mode):
    return _bdot_raw(a, b, mode)


def _bdot_fwd(a, b, mode):
    return _bdot_raw(a, b, mode), (a, b)


def _bdot_bwd(mode, res, ct):
    a, b = res
    if mode == "nn":
        return _bdot(ct, b, "nt"), _bdot(a, ct, "tn")
    if mode == "nt":
        return _bdot(ct, b, "nn"), _bdot(ct, a, "tn")
    return _bdot(b, ct, "nt"), _bdot(a, ct, "nn")


_bdot.defvjp(_bdot_fwd, _bdot_bwd)


@jax.custom_vjp
def _unit_lower_inverse(a_strict):
    return _unit_lower_inverse_raw(a_strict)


def _unit_lower_inverse_fwd(a_strict):
    t = _unit_lower_inverse_raw(a_strict)
    return t, t


def _unit_lower_inverse_bwd(t, ct):
    left = [_bdot(ti, ci, "tn") for ti, ci in zip(t, ct)]
    return (tuple(-_bdot(li, ti, "nt") for li, ti in zip(left, t)),)


_unit_lower_inverse.defvjp(_unit_lower_inverse_fwd, _unit_lower_inverse_bwd)


@jax.custom_vjp
def _saved_inverse(a_strict, t):
    return t


def _saved_inverse_fwd(a_strict, t):
    return t, t


def _saved_inverse_bwd(t, ct):
    return _unit_lower_inverse_bwd(t, ct) + (tuple(jnp.zeros_like(ti) for ti in t),)


_saved_inverse.defvjp(_saved_inverse_fwd, _saved_inverse_bwd)


def _unit_lower_inverse_raw(a_strict):
    C = a_strict[0].shape[0]
    ii = lax.broadcasted_iota(jnp.int32, (C, C), 0)
    jj = lax.broadcasted_iota(jnp.int32, (C, C), 1)
    eye = (ii == jj).astype(F32)
    blk = 16
    same = (ii // blk) == (jj // blk)
    p = [-jnp.where(same, a, 0.0) for a in a_strict]
    t = [eye + x for x in p]
    for _ in range(3):
        p = [_bdot(x, x, "nn") for x in p]
        t = [ti + _bdot(ti, x, "nn") for ti, x in zip(t, p)]
    while blk < C:
        same2 = (ii // (2 * blk)) == (jj // (2 * blk))
        off = jnp.logical_and(same2, jnp.logical_not(same))
        te = [_bdot(ti, jnp.where(off, a, 0.0), "nn") for ti, a in zip(t, a_strict)]
        t = [ti - _bdot(x, ti, "nn") for ti, x in zip(t, te)]
        same, blk = same2, 2 * blk
    return tuple(t)


def _chunk_heads(q, k, v, gc_col, gc_row, b_col, s0, t_saved=None, with_t=False):
    R = range(len(q))
    C = q[0].shape[0]
    ii = lax.broadcasted_iota(jnp.int32, (C, C), 0)
    jj = lax.broadcasted_iota(jnp.int32, (C, C), 1)
    rows = lax.broadcasted_iota(jnp.int32, (C, 1), 0)
    decay = [jnp.where(ii >= jj, jnp.exp(jnp.minimum(gc_col[h] - gc_row[h], 0.0)), 0.0) for h in R]
    kb = [k[h] * b_col[h] for h in R]
    a = [_bdot(kb[h], k[h], "nt") * decay[h] for h in R]
    qk = [_bdot(q[h], k[h], "nt") * decay[h] for h in R]
    a_strict = tuple(jnp.where(ii > jj, a[h], 0.0) for h in R)
    t = _unit_lower_inverse(a_strict) if t_saved is None else _saved_inverse(a_strict, t_saved)
    eg = [jnp.exp(gc_col[h]) for h in R]
    u = [_bdot(t[h], v[h] * b_col[h], "nn") for h in R]
    w = [_bdot(t[h], kb[h] * eg[h], "nn") for h in R]
    g_last = [jnp.sum(jnp.where(rows == C - 1, gc_col[h], 0.0), axis=0, keepdims=True) for h in R]
    kd = [k[h] * jnp.exp(g_last[h] - gc_col[h]) for h in R]
    ws = [_bdot(w[h], s0[h], "nn") for h in R]
    qs = [_bdot(q[h] * eg[h], s0[h], "nn") for h in R]
    v_new = [u[h] - ws[h] for h in R]
    o = [qs[h] + _bdot(qk[h], v_new[h], "nn") for h in R]
    s1 = [s0[h] * jnp.exp(g_last[h]) + _bdot(kd[h], v_new[h], "tn") for h in R]
    return (tuple(o), tuple(s1), t) if with_t else (tuple(o), tuple(s1))


def _pick_lane(a, h):
    lanes = lax.broadcasted_iota(jnp.int32, a.shape, 1)
    return jnp.sum(jnp.where(lanes == h, a, 0.0), axis=1, keepdims=True)


def _pick_row(a, h):
    rows = lax.broadcasted_iota(jnp.int32, a.shape, 0)
    return jnp.sum(jnp.where(rows == h, a, 0.0), axis=0, keepdims=True)


def _tri(C):
    ii = lax.broadcasted_iota(jnp.int32, (C, C), 0)
    jj = lax.broadcasted_iota(jnp.int32, (C, C), 1)
    return (ii >= jj).astype(F32)


def _delta_fwd(qkvn, g_pad, g_rows, beta_pad):
    S, W = qkvn.shape
    D = W // 3
    H = D // HEAD_DIM
    C = min(CHUNK, S)
    N = S // C

    def body(x_ref, gp_ref, gr_ref, bp_ref, o_ref, sall_ref, tall_ref, st):
        n = pl.program_id(0)

        @pl.when(n == 0)
        def _():
            st[...] = jnp.zeros_like(st)

        low = _tri(C)
        gc_cols = _tri_dot(low, gp_ref[...], "nn", 0)
        gc_rows = _tri_dot(gr_ref[...], low, "nt", 1)
        bcols = bp_ref[...]
        hs = range(H)
        s0 = tuple(st[h] for h in hs)
        for h in hs:
            sall_ref[h] = s0[h]
        o, s1, t = _chunk_heads(
            tuple(x_ref[:, h * HEAD_DIM : (h + 1) * HEAD_DIM] for h in hs),
            tuple(x_ref[:, D + h * HEAD_DIM : D + (h + 1) * HEAD_DIM] for h in hs),
            tuple(x_ref[:, 2 * D + h * HEAD_DIM : 2 * D + (h + 1) * HEAD_DIM] for h in hs),
            tuple(_pick_lane(gc_cols, h) for h in hs), tuple(_pick_row(gc_rows, h) for h in hs),
            tuple(_pick_lane(bcols, h) for h in hs), s0, with_t=True,
        )
        for h in hs:
            st[h] = s1[h]
            o_ref[:, h * HEAD_DIM : (h + 1) * HEAD_DIM] = o[h]
            tall_ref[h] = t[h].astype(tall_ref.dtype)

    return pl.pallas_call(
        body,
        name="gdn_delta_fwd",
        grid=(N,),
        in_specs=[
            pl.BlockSpec((C, W), lambda n: (n, 0)),
            pl.BlockSpec((C, LANES), lambda n: (n, 0)),
            pl.BlockSpec((None, 8, C), lambda n: (n, 0, 0)),
            pl.BlockSpec((C, LANES), lambda n: (n, 0)),
        ],
        out_specs=[pl.BlockSpec((C, D), lambda n: (n, 0)), pl.BlockSpec((None, H, HEAD_DIM, HEAD_DIM), lambda n: (n, 0, 0, 0)),
                   pl.BlockSpec((None, H, C, C), lambda n: (n, 0, 0, 0))],
        out_shape=[jax.ShapeDtypeStruct((S, D), F32), jax.ShapeDtypeStruct((N, H, HEAD_DIM, HEAD_DIM), F32), jax.ShapeDtypeStruct((N, H, C, C), BF16)],
        scratch_shapes=[pltpu.VMEM((H, HEAD_DIM, HEAD_DIM), F32)],
        compiler_params=_params(1),
    )(qkvn, g_pad, g_rows, beta_pad)


def _delta_bwd(qkvn, g_pad, g_rows, beta_pad, s_all, t_all, do):
    S, W = qkvn.shape
    D = W // 3
    H = D // HEAD_DIM
    C = min(CHUNK, S)
    N = S // C

    def body(x_ref, gp_ref, gr_ref, bp_ref, sall_ref, tall_ref, do_ref, dx_ref, dgp_ref, dgr_ref, dbp_ref, dst):
        n = pl.program_id(0)

        @pl.when(n == 0)
        def _():
            dst[...] = jnp.zeros_like(dst)

        low = _tri(C)
        gc_cols = _tri_dot(low, gp_ref[...], "nn", 0)
        gc_rows = _tri_dot(gr_ref[...], low, "nt", 1)
        bcols = bp_ref[...]
        lane = lax.broadcasted_iota(jnp.int32, (1, LANES), 1)
        row8 = lax.broadcasted_iota(jnp.int32, (8, 1), 0)
        dgc_cols = jnp.zeros((C, LANES), F32)
        dgc_rows = jnp.zeros((8, C), F32)
        dbcols = jnp.zeros((C, LANES), F32)
        hs = range(H)
        t_saved = tuple(tall_ref[h].astype(F32) for h in hs)
        _, vjp = jax.vjp(
            lambda *args: _chunk_heads(*args, t_saved=t_saved),
            tuple(x_ref[:, h * HEAD_DIM : (h + 1) * HEAD_DIM] for h in hs),
            tuple(x_ref[:, D + h * HEAD_DIM : D + (h + 1) * HEAD_DIM] for h in hs),
            tuple(x_ref[:, 2 * D + h * HEAD_DIM : 2 * D + (h + 1) * HEAD_DIM] for h in hs),
            tuple(_pick_lane(gc_cols, h) for h in hs), tuple(_pick_row(gc_rows, h) for h in hs),
            tuple(_pick_lane(bcols, h) for h in hs), tuple(sall_ref[h] for h in hs),
        )
        dq, dk, dv, dgc, dgr, dbc, ds0 = vjp((tuple(do_ref[:, h * HEAD_DIM : (h + 1) * HEAD_DIM] for h in hs), tuple(dst[h] for h in hs)))
        for h in hs:
            dst[h] = ds0[h]
            dx_ref[:, h * HEAD_DIM : (h + 1) * HEAD_DIM] = dq[h]
            dx_ref[:, D + h * HEAD_DIM : D + (h + 1) * HEAD_DIM] = dk[h]
            dx_ref[:, 2 * D + h * HEAD_DIM : 2 * D + (h + 1) * HEAD_DIM] = dv[h]
            dgc_cols = dgc_cols + dgc[h] * (lane == h).astype(F32)
            dgc_rows = dgc_rows + dgr[h] * (row8 == h).astype(F32)
            dbcols = dbcols + dbc[h] * (lane == h).astype(F32)
        dgp_ref[...] = _tri_dot(low, dgc_cols, "tn", 0)
        dgr_ref[...] = _tri_dot(dgc_rows, low, "nn", 1)
        dbp_ref[...] = dbcols

    rev = lambda n: N - 1 - n
    return pl.pallas_call(
        body,
        name="gdn_delta_bwd",
        grid=(N,),
        in_specs=[
            pl.BlockSpec((C, W), lambda n: (rev(n), 0)),
            pl.BlockSpec((C, LANES), lambda n: (rev(n), 0)),
            pl.BlockSpec((None, 8, C), lambda n: (rev(n), 0, 0)),
            pl.BlockSpec((C, LANES), lambda n: (rev(n), 0)),
            pl.BlockSpec((None, H, HEAD_DIM, HEAD_DIM), lambda n: (rev(n), 0, 0, 0)),
            pl.BlockSpec((None, H, C, C), lambda n: (rev(n), 0, 0, 0)),
            pl.BlockSpec((C, D), lambda n: (rev(n), 0)),
        ],
        out_specs=[
            pl.BlockSpec((C, W), lambda n: (rev(n), 0)),
            pl.BlockSpec((C, LANES), lambda n: (rev(n), 0)),
            pl.BlockSpec((None, 8, C), lambda n: (rev(n), 0, 0)),
            pl.BlockSpec((C, LANES), lambda n: (rev(n), 0)),
        ],
        out_shape=[
            jax.ShapeDtypeStruct((S, W), F32),
            jax.ShapeDtypeStruct((S, LANES), F32),
            jax.ShapeDtypeStruct((N, 8, C), F32),
            jax.ShapeDtypeStruct((S, LANES), F32),
        ],
        scratch_shapes=[pltpu.VMEM((H, HEAD_DIM, HEAD_DIM), F32)],
        compiler_params=_params(1),
    )(qkvn, g_pad, g_rows, beta_pad, s_all, t_all, do)


def _gate_norm_head(o, z, nw):
    return o * lax.rsqrt(jnp.mean(o * o, axis=-1, keepdims=True) + RMS_EPS) * nw * _silu(z)


def _gate_norm_fwd(o, z, nw):
    S, D = o.shape
    H = D // HEAD_DIM
    ts = min(ROW_TILE, S)

    def fn(i, n, rv, hv, cv):
        ov, zv = rv
        parts = [_gate_norm_head(ov[:, h * HEAD_DIM : (h + 1) * HEAD_DIM], zv[:, h * HEAD_DIM : (h + 1) * HEAD_DIM], cv[0]) for h in range(H)]
        return (jnp.concatenate(parts, axis=1),), ()

    return _rowwise("gdn_gate_norm_fwd", fn, S, ts, rows=[o, z], consts=[nw], outs=[(D, BF16)])[0]


def _gate_norm_bwd_epi(dog, o, z, nw):
    D = dog.shape[1]
    dos, dzs, dnw = [], [], None
    for h in range(D // HEAD_DIM):
        sl = slice(h * HEAD_DIM, (h + 1) * HEAD_DIM)
        _, vjp = jax.vjp(_gate_norm_head, o[:, sl], z[:, sl], nw)
        a, b_, c_ = vjp(dog[:, sl])
        dos.append(a)
        dzs.append(b_)
        dnw = c_ if dnw is None else dnw + c_
    wide = jnp.concatenate([dnw, jnp.zeros((1, D - HEAD_DIM), F32)], axis=1) if D > HEAD_DIM else dnw
    return jnp.concatenate(dos, axis=1), jnp.concatenate(dzs, axis=1), wide


def _square_bf16(r):
    return r * r


def _mlp_ple_dw(li, dhb, dgate, dpp, xa, p, r, w2):
    dpre = _mm(f"l{li}_mlp_down_bwd", dhb, w2, "nt", [BF16], epi=lambda acc, rr: (acc * (2.0 * rr.astype(F32)),), extras=[(r, "tile")], tm=1024, tn=1024, b_outer=True)
    dw2 = _mm(f"l{li}_mlp_dw2", r, dhb, "tn", [BF16], a_fn=_square_bf16, **DW_TILES)
    dw1 = _mm(f"l{li}_mlp_dw1", xa, dpre, "tn", [BF16], out_blocks=N_DEV, **DW_TILES)
    dwg = _mm(f"l{li}_ple_dwg", xa, dgate, "tn", [BF16], **DW_TILES)
    dwp = _mm(f"l{li}_ple_dwp", p, dpp, "tn", [BF16], out_blocks=N_DEV, **DW_TILES)
    rows = lambda a: a.reshape((N_DEV, a.shape[0] // N_DEV) + a.shape[1:])
    return dpre, dw1, rows(dw2), rows(dwg), dwp


def _mlp_ple_dx(li, dh, dpre, dgate, w1, wg, after, xhat, rstd, g):
    t = _mm(f"l{li}_ple_gate_bwd", dgate, wg, "nt", [F32], epi=lambda acc, d: (acc + ALPHA * d,), extras=[(dh, "tile")], tm=1024, after=after)
    return _mm(f"l{li}_mlp_up_bwd", dpre, w1, "nt", [F32, BF16], epi=_ln_bwd_epi, extras=[(t, "tile"), (xhat, "tile"), (rstd, "rows"), (g, "row")],
               tm=256, tk=4096, accs=2)


def _local_step(x, p, tgt, W, fetch, emit):
    S, D = x.shape
    H = D // HEAD_DIM
    C = min(CHUNK, S)
    N = S // C
    lg = lambda i, j: W["ln_gain"][2 * i + j][None, :]
    lb = lambda i, j: W["ln_bias"][2 * i + j][None, :]
    G = {}

    pooled, xh0a, rs0a, x0a = _pool_fwd(x, W["pool_w"], W["pool_b"], W["pool_scale"], lg(0, 0), lb(0, 0))
    w0a = fetch("l0a", x0a)
    r0 = _mm("l0_mlp_up", x0a, w0a["mlp_w1"], "nn", [BF16], epi=lambda acc: (jnp.maximum(acc, 0.0),), tm=1024, tn=1024, b_outer=True, after=w0a.get("_after", ()))
    w0b = fetch("l0b", r0)
    gate0 = _mm("l0_ple_gate", x0a, w0b["ple_gate_w"], "nn", [F32], epi=lambda acc, bias: (acc + bias,), extras=[(W["ple_gate_b"][0:1], "row")], tm=1024)
    pp0 = _mm("l0_ple_proj", p[0], w0b["ple_proj"], "nn", [F32])
    w0c = fetch("l0c", pp0)
    ln_rows = lambda i, j, i2, j2: [(lg(i, j), "row"), (lb(i, j), "row"), (lg(i2, j2), "row"), (lb(i2, j2), "row")]
    xh0b, rs0b, x0b = _mm("l0_mlp_down", r0, w0c["mlp_w2"], "nn", [F32, (F32, LANES), BF16], a_fn=_square_bf16, tm=256, tn=D, tk=4096, epi=_res_ln_ffpe_epi,
                          extras=[(xh0a, "tile"), (gate0, "tile"), (pp0, "tile")] + ln_rows(0, 0, 0, 1), after=w0c.get("_after", ()))

    wg_ = fetch("gdn", x0b)
    qkv_pre = _mm("gdn_in_qkv", x0b, wg_["gdn_wqkv"], "nn", [F32], tm=1024, tn=1024, b_outer=True, after=wg_.get("_after", ()))
    z = _mm("gdn_in_z", x0b, wg_["gdn_wz"], "nn", [F32], tm=1024)
    ba = _mm("gdn_in_ba", x0b, wg_["gdn_wba"], "nn", [F32])
    qkvn = _conv_fwd(qkv_pre, W["gdn_conv"])
    beta_pad, g_pad = _gates_fwd(ba, W["gdn_a_log"], W["gdn_dt_bias"])
    g_rows = g_pad[:, :8].reshape(N, C, 8).transpose(0, 2, 1)
    o, s_all, t_all = _delta_fwd(qkvn, g_pad, g_rows, beta_pad)
    og = _gate_norm_fwd(o, z, W["gdn_norm_w"])
    wg_.update(fetch("gdo", og))
    xh1a, rs1a, x1a = _mm("gdn_out", og, wg_["gdn_w_out"], "nn", [F32, (F32, LANES), BF16], tm=512, tn=D, epi=_res_ln_epi,
                          extras=[(xh0b, "tile")] + ln_rows(0, 1, 1, 0), after=wg_.get("_after", ()))
    w1_ = fetch("l1", x1a)
    r1 = _mm("l1_mlp_up", x1a, w1_["mlp_w1"], "nn", [BF16], epi=lambda acc: (jnp.maximum(acc, 0.0),), tm=1024, tn=1024, b_outer=True)
    gate1 = _mm("l1_ple_gate", x1a, w1_["ple_gate_w"], "nn", [F32], epi=lambda acc, bias: (acc + bias,), extras=[(W["ple_gate_b"][1:2], "row")], tm=1024)
    pp1 = _mm("l1_ple_proj", p[1], w1_["ple_proj"], "nn", [F32])
    dh1b, dh1b_b, dgate1, dpp1, loss_cols, dg11, db11, dbg_1 = _mm(
        "l1_mlp_down", r1, w1_["mlp_w2"], "nn", [F32, BF16, BF16, BF16], a_fn=_square_bf16, tm=256, tn=D, tk=4096, epi=_final_ln_loss_epi,
        extras=[(xh1a, "tile"), (gate1, "tile"), (pp1, "tile"), (tgt, "tile")] + ln_rows(1, 0, 1, 1), accs=4)

    dpre1, dw1_1, dw2_1, dwg_1, dwp_1 = _mlp_ple_dw(1, dh1b_b, dgate1, dpp1, x1a, p[1], r1, w1_["mlp_w2"])
    tok = emit("l1", {"mlp_w1": dw1_1, "mlp_w2": dw2_1, "ple_gate_w": dwg_1, "ple_proj": dwp_1})
    dh1a, dh1a_b, dg10, db10 = _mlp_ple_dx(1, dh1b, dpre1, dgate1, w1_["mlp_w1"], w1_["ple_gate_w"], [tok], xh1a, rs1a, lg(1, 0))
    do, dz, dnw = _mm("gdn_out_bwd", dh1a_b, wg_["gdn_w_out"], "nt", [F32, BF16], tm=256, tn=D, tk=D, epi=_gate_norm_bwd_epi,
                      extras=[(o, "tile"), (z, "tile"), (W["gdn_norm_w"], "whole")], accs=1)
    dnw = dnw[:, :HEAD_DIM]
    dw_out = _mm("gdn_dw_out", og, dh1a_b, "tn", [BF16], **DW_TILES)
    dw_out = dw_out.reshape((N_DEV, dw_out.shape[0] // N_DEV) + dw_out.shape[1:])
    dqkvn, dg_col, dg_row, dbeta = _delta_bwd(qkvn, g_pad, g_rows, beta_pad, s_all, t_all, do)
    dg_all = dg_col + jnp.pad(dg_row.transpose(0, 2, 1).reshape(S, 8), ((0, 0), (0, LANES - 8)))
    dba, dalog, ddt = _gates_bwd(ba, W["gdn_a_log"], W["gdn_dt_bias"], dbeta, dg_all, H)
    dqkv, dconv = _conv_bwd(qkv_pre, W["gdn_conv"], dqkvn)
    dwqkv = _mm("gdn_dwqkv", x0b, dqkv, "tn", [F32], **DW_TILES)
    dwz = _mm("gdn_dwz", x0b, dz, "tn", [F32], **DW_TILES)
    dwba = _mm("gdn_dwba", x0b, dba, "tn", [F32], **DW_TILES)
    dw_in = jnp.concatenate([dwqkv, dwz, dwba[:, :H], dwba[:, LANES : LANES + H]], axis=1)
    tok = emit("gdn", {"gdn_w_in": _split_blocks("gdn_w_in", dw_in).astype(BF16), "gdn_w_out": dw_out})
    t = _mm("gdn_in_ba_bwd", dba, wg_["gdn_wba"], "nt", [F32], epi=lambda acc, d: (acc + ALPHA * d,), extras=[(dh1a, "tile")], after=[tok])
    t = _mm("gdn_in_z_bwd", dz, wg_["gdn_wz"], "nt", [F32], epi=lambda acc, d: (acc + d,), extras=[(t, "tile")], tm=1024)
    dh0b, dh0b_b, dgate0, dpp0, dg01, db01, dbg_0 = _mm(
        "gdn_in_qkv_bwd", dqkv, wg_["gdn_wqkv"], "nt", [F32, BF16, BF16, BF16], epi=_ln_bwd_ple_epi,
        extras=[(t, "tile"), (xh0b, "tile"), (rs0b, "rows"), (gate0, "tile"), (pp0, "tile"), (lg(0, 1), "row")], tm=256, tk=3072, accs=3)

    dpre0, dw1_0, dw2_0, dwg_0, dwp_0 = _mlp_ple_dw(0, dh0b_b, dgate0, dpp0, x0a, p[0], r0, w0c["mlp_w2"])
    tok = emit("l0", {"mlp_w1": dw1_0, "mlp_w2": dw2_0, "ple_gate_w": dwg_0, "ple_proj": dwp_0})
    dh0a, _, dg00, db00 = _mlp_ple_dx(0, dh0b, dpre0, dgate0, w0a["mlp_w1"], w0b["ple_gate_w"], [tok], xh0a, rs0a, lg(0, 0))
    grad_x, dyp, dscale, dpb = _pool_bwd(dh0a, pooled, W["pool_w"], W["pool_b"], W["pool_scale"])
    G["pool_w"] = _pool_dw(pooled, dyp)

    G["ln_gain"] = jnp.concatenate([dg00, dg01, dg10, dg11], axis=0)
    G["ln_bias"] = jnp.concatenate([db00, db01, db10, db11], axis=0)
    G["pool_b"] = dpb
    G["pool_scale"] = dscale
    G["gdn_conv"] = dconv
    G["gdn_a_log"] = dalog[:, :H]
    G["gdn_dt_bias"] = ddt[:, :H]
    G["gdn_norm_w"] = dnw
    G["ple_gate_b"] = jnp.concatenate([dbg_0, dbg_1], axis=0)
    return loss_cols, grad_x, G


_HBM = pl.BlockSpec(memory_space=pltpu.HBM)


def _all_gather(name, shards):
    T = len(shards)

    def body(*refs):
        ins, outs = refs[:T], refs[T : 2 * T]
        send_sems, recv_sems, local_sems = refs[2 * T :]
        x, y, c = lax.axis_index("x"), lax.axis_index("y"), lax.axis_index("c")
        me, sibling = (x, y, c), (x, y, 1 - c)
        chips = [(1 - x, y), (x, 1 - y), (1 - x, 1 - y)]

        def blk(t, px, py, pc):
            return outs[t].at[4 * px + 2 * py + pc]

        def copy(t, k, block, to, src=None):
            return pltpu.make_async_remote_copy(
                src_ref=blk(t, *block) if src is None else src, dst_ref=blk(t, *block),
                send_sem=send_sems.at[t, k], recv_sem=recv_sems.at[t, k], device_id=to, device_id_type=MESH,
            )

        mine = [pltpu.make_async_copy(ins[t], blk(t, *me), local_sems.at[t]) for t in range(T)]
        for cp in mine:
            cp.start()
        first = []
        for t in range(T):
            first.append(copy(t, 0, me, sibling, src=ins[t]))
            first += [copy(t, 1 + j, me, (*chip, c), src=ins[t]) for j, chip in enumerate(chips)]
        for cp in first:
            cp.start()
        passed = []
        for j, chip in enumerate(chips):
            for t in range(T):
                copy(t, 1 + j, (*chip, c), me).wait_recv()
                fw = copy(t, 4 + j, (*chip, c), sibling)
                fw.start()
                passed.append(fw)
        for t in range(T):
            copy(t, 0, sibling, me).wait_recv()
            for j, chip in enumerate(chips):
                copy(t, 4 + j, (*chip, 1 - c), me).wait_recv()
        for cp in first + passed:
            cp.wait_send()
        for cp in mine:
            cp.wait()

    return pl.pallas_call(
        body,
        name=name,
        in_specs=[_HBM] * T,
        out_specs=[_HBM] * T,
        out_shape=[jax.ShapeDtypeStruct((N_DEV,) + s.shape, s.dtype) for s in shards],
        scratch_shapes=[pltpu.SemaphoreType.DMA((T, 7)), pltpu.SemaphoreType.DMA((T, 7)), pltpu.SemaphoreType.DMA((T,))],
    )(*shards)


def _exchange(name, blocks):
    def body(g_ref, o_ref, send_sems, recv_sems, local_sem):
        x, y, c = lax.axis_index("x"), lax.axis_index("y"), lax.axis_index("c")
        own = pltpu.make_async_copy(g_ref.at[4 * x + 2 * y + c], o_ref.at[N_DEV - 1], local_sem)
        own.start()
        copies = []
        for rel in range(1, N_DEV):
            px = 1 - x if rel & 4 else x
            py = 1 - y if rel & 2 else y
            pc = 1 - c if rel & 1 else c
            copies.append(
                pltpu.make_async_remote_copy(
                    src_ref=g_ref.at[4 * px + 2 * py + pc], dst_ref=o_ref.at[rel - 1],
                    send_sem=send_sems.at[rel - 1], recv_sem=recv_sems.at[rel - 1], device_id=(px, py, pc), device_id_type=MESH,
                )
            )
        for cp in copies:
            cp.start()
        for cp in copies:
            cp.wait_recv()
        for cp in copies:
            cp.wait_send()
        own.wait()

    return pl.pallas_call(
        body,
        name=name,
        in_specs=[_HBM],
        out_specs=_HBM,
        out_shape=jax.ShapeDtypeStruct(blocks.shape, blocks.dtype),
        scratch_shapes=[pltpu.SemaphoreType.DMA((N_DEV - 1,)), pltpu.SemaphoreType.DMA((N_DEV - 1,)), pltpu.SemaphoreType.DMA],
    )(blocks)


_SEM = pl.BlockSpec(memory_space=pltpu.SEMAPHORE)
_ANY = pl.BlockSpec(memory_space=pl.ANY)
_DATAFLOW = pltpu.SideEffectType.DATAFLOW_SIDE_EFFECTING
N_PEERS = N_DEV - 1


def _peer(rel, x, y, c):
    return (1 - x if rel & 4 else x, 1 - y if rel & 2 else y, 1 - c if rel & 1 else c)


def _send_start(name, srcs, lands, gather, after):
    T = len(srcs)

    def body(*refs):
        src_refs, land_refs = refs[:T], refs[T : 2 * T]
        send_sems, recv_sems = refs[2 * T + 1], refs[2 * T + 2]
        token = refs[-1]
        x, y, c = lax.axis_index("x"), lax.axis_index("y"), lax.axis_index("c")
        for t in range(T):
            for rel in range(1, N_DEV):
                px, py, pc = _peer(rel, x, y, c)
                pltpu.make_async_remote_copy(
                    src_ref=src_refs[t] if gather else src_refs[t].at[4 * px + 2 * py + pc],
                    dst_ref=land_refs[t].at[4 * x + 2 * y + c] if gather else land_refs[t].at[rel - 1],
                    send_sem=send_sems.at[t * N_PEERS + rel - 1], recv_sem=recv_sems.at[t * N_PEERS + rel - 1], device_id=(px, py, pc), device_id_type=MESH,
                ).start()
        token[...] = jnp.zeros_like(token)

    hbm = lambda a: pltpu.HBM(a.shape, a.dtype)
    return pl.pallas_call(
        body,
        name=name,
        out_shape=(pltpu.SemaphoreType.DMA((T * N_PEERS,)), pltpu.SemaphoreType.DMA((T * N_PEERS,)), *[hbm(a) for a in srcs],
                   *[hbm(a) for a in lands], jax.ShapeDtypeStruct((8, LANES), F32)),
        in_specs=(_HBM,) * (2 * T) + (_ANY,),
        out_specs=(_SEM, _SEM) + (_HBM,) * (2 * T) + (pl.BlockSpec(memory_space=pltpu.VMEM),),
        input_output_aliases={t: 2 + t for t in range(2 * T)},
        compiler_params=pltpu.CompilerParams(has_side_effects=_DATAFLOW),
    )(*[pltpu.with_memory_space_constraint(a, pltpu.HBM) for a in list(srcs) + list(lands)], after)


def _send_wait(name, started, after, gather):
    T = (len(started) - 3) // 2
    send_sems, recv_sems, token = started[0], started[1], started[-1]
    thru = started[2:-1]

    def body(*refs):
        src_refs, land_refs = refs[:T], refs[T : 2 * T]
        send_sems, recv_sems = refs[2 * T], refs[2 * T + 1]
        x, y, c = lax.axis_index("x"), lax.axis_index("y"), lax.axis_index("c")
        for t in range(T):
            for rel in range(1, N_DEV):
                cp = pltpu.make_async_remote_copy(
                    src_ref=src_refs[t] if gather else src_refs[t].at[0], dst_ref=land_refs[t].at[0],
                    send_sem=send_sems.at[t * N_PEERS + rel - 1], recv_sem=recv_sems.at[t * N_PEERS + rel - 1], device_id=_peer(rel, x, y, c), device_id_type=MESH,
                )
                cp.wait_send()
                cp.wait_recv()

    outs = pl.pallas_call(
        body,
        name=name,
        out_shape=tuple(pltpu.HBM(a.shape, a.dtype) for a in thru),
        in_specs=(_HBM,) * (2 * T) + (_SEM, _SEM, _ANY),
        out_specs=(_HBM,) * (2 * T),
        input_output_aliases={t: t for t in range(2 * T)},
        compiler_params=pltpu.CompilerParams(has_side_effects=_DATAFLOW),
    )(*thru, send_sems, recv_sems, after)
    return list(outs[:T]), list(outs[T:])


def _sum_blocks(name, parts, tr):
    _, R, Cw = parts[0].shape
    tr = tr if R % tr == 0 else R

    def body(*refs):
        acc = None
        for p_ref in refs[:-1]:
            for d in range(p_ref.shape[0]):
                v = p_ref[d].astype(F32)
                acc = v if acc is None else acc + v
        refs[-1][...] = acc

    return pl.pallas_call(
        body,
        name=name,
        grid=(R // tr,),
        in_specs=[pl.BlockSpec((a.shape[0], tr, Cw), lambda i: (0, i, 0)) for a in parts],
        out_specs=pl.BlockSpec((tr, Cw), lambda i: (i, 0)),
        out_shape=jax.ShapeDtypeStruct((R, Cw), F32),
        compiler_params=_params(1),
    )(*parts)


def _adamw(name, w, g, m, v):
    shape = w.shape
    cols = shape[-1]
    rows = w.size // cols
    tr = rows if rows <= 512 else 512
    assert rows % tr == 0
    w2, g2, m2, v2 = (a.reshape(rows, cols) for a in (w, g, m, v))

    def body(w_ref, g_ref, m_ref, v_ref, d_ref, mo_ref, vo_ref):
        gv = g_ref[...]
        mn = ADAM_B1 * m_ref[...] + (1.0 - ADAM_B1) * gv
        vn = ADAM_B2 * v_ref[...] + (1.0 - ADAM_B2) * jnp.square(gv)
        m_hat = mn / (1.0 - ADAM_B1**ADAM_STEP)
        v_hat = vn / (1.0 - ADAM_B2**ADAM_STEP)
        d_ref[...] = -ADAM_LR * (m_hat / (jnp.sqrt(v_hat) + ADAM_EPS) + ADAM_WD * w_ref[...])
        mo_ref[...] = mn
        vo_ref[...] = vn

    spec = pl.BlockSpec((tr, cols), lambda i: (i, 0))
    d, mn, vn = pl.pallas_call(
        body,
        name=name,
        grid=(rows // tr,),
        in_specs=[spec] * 4,
        out_specs=[spec] * 3,
        out_shape=[jax.ShapeDtypeStruct((rows, cols), F32)] * 3,
        compiler_params=_params(1),
    )(w2, g2, m2, v2)
    return d.reshape(shape), mn.reshape(shape), vn.reshape(shape)


SMALL_SHARDED = ("ln_gain", "ln_bias", "pool_b", "gdn_conv")
SMALL_REPLICATED = ("pool_scale", "gdn_a_log", "gdn_dt_bias", "gdn_norm_w", "ple_gate_b")
WEIGHTS = ("ln_gain", "ln_bias", "pool_w", "pool_b", "pool_scale", "gdn_w_in", "gdn_conv", "gdn_a_log", "gdn_dt_bias",
           "gdn_norm_w", "gdn_w_out", "mlp_w1", "mlp_w2", "ple_gate_w", "ple_gate_b", "ple_proj")
BIG_AXIS = {"gdn_w_in": 1, "gdn_w_out": 0, "mlp_w1": 1, "mlp_w2": 0, "ple_gate_w": 0, "ple_proj": 1, "pool_w": 1}
GATHER_GROUPS = {
    "l0a": (("mlp_w1", 0),),
    "l0b": (("ple_gate_w", 0), ("ple_proj", 0)),
    "l0c": (("mlp_w2", 0),),
    "gdn": (("gdn_w_in", 0),),
    "gdo": (("gdn_w_out", 0),),
    "l1": (("mlp_w1", 1), ("mlp_w2", 1), ("ple_gate_w", 1), ("ple_proj", 1)),
}
GATHER_AFTER = {"l0b": "l0a", "l0c": "l0a", "gdn": "l0c", "gdo": "gdn", "l1": "gdn"}
GRAD_GROUPS = {
    "l1": (("mlp_w1", 1), ("mlp_w2", 1), ("ple_gate_w", 1), ("ple_proj", 1)),
    "gdn": (("gdn_w_in", 0), ("gdn_w_out", 0)),
    "l0": (("mlp_w1", 0), ("mlp_w2", 0), ("ple_gate_w", 0), ("ple_proj", 0)),
}
PACK_PART_ALIGN = 16
SUM_TILE = 128


def _part_rows(a, width):
    rows = a.size // width
    return rows + (-rows) % PACK_PART_ALIGN


def _pack_rows(parts, width, dtype, align):
    padded = []
    for a in parts:
        a2 = a.reshape(-1, width).astype(dtype)
        padded.append(jnp.pad(a2, ((0, _part_rows(a, width) - a2.shape[0]), (0, 0))))
    flat = jnp.concatenate(padded, axis=0)
    return jnp.pad(flat, ((0, (-flat.shape[0]) % align), (0, 0)))


def _pack_blocks(parts, width, dtype, align):
    padded = []
    for a in parts:
        a2 = a.reshape(a.shape[0], -1, width).astype(dtype)
        padded.append(jnp.pad(a2, ((0, 0), (0, _part_rows(a[0], width) - a2.shape[1]), (0, 0))))
    flat = jnp.concatenate(padded, axis=1)
    return jnp.pad(flat, ((0, 0), (0, (-flat.shape[1]) % align), (0, 0)))


def _unpack_rows(packed, shapes, width):
    out, off = [], 0
    for shp in shapes:
        size = 1
        for d in shp:
            size *= d
        out.append(packed[..., off : off + size // width, :].reshape(packed.shape[:-2] + tuple(shp)))
        off += size // width + (-(size // width)) % PACK_PART_ALIGN
    return out


def _split_blocks(name, full):
    ax = BIG_AXIS[name]
    shp = full.shape
    a = full.reshape(shp[:ax] + (N_DEV, shp[ax] // N_DEV) + shp[ax + 1 :])
    return jnp.moveaxis(a, ax, 0)


def _join_blocks(name, blocks):
    ax = BIG_AXIS[name]
    a = jnp.moveaxis(blocks, 0, ax)
    shp = a.shape
    return a.reshape(shp[:ax] + (shp[ax] * shp[ax + 1],) + shp[ax + 2 :])


def _pack_small(parts):
    flat = jnp.concatenate([jnp.pad(a.reshape(-1), (0, (-a.size) % LANES)) for a in parts])
    rows = flat.size // LANES
    return jnp.pad(flat.reshape(rows, LANES), ((0, (-rows) % 8), (0, 0)))


def _unpack_small(packed, shapes):
    flat = packed.reshape(packed.shape[:-2] + (-1,))
    out, off = [], 0
    for shp in shapes:
        size = 1
        for s in shp:
            size *= s
        out.append(flat[..., off : off + size].reshape(flat.shape[:-1] + tuple(shp)))
        off += size + (-size) % LANES
    return out


def _split_w_in(w_in, D, H):
    pad = lambda a: jnp.pad(a, ((0, 0), (0, LANES - H)))
    return w_in[:, : 3 * D], w_in[:, 3 * D : 4 * D], jnp.concatenate([pad(w_in[:, 4 * D : 4 * D + H]), pad(w_in[:, 4 * D + H :])], axis=1)


def kernel(x, p, ln_gain, ln_bias, pool_w, pool_b, pool_scale, gdn_w_in, gdn_conv, gdn_a_log, gdn_dt_bias, gdn_norm_w, gdn_w_out, mlp_w1, mlp_w2, ple_gate_w, ple_gate_b, ple_proj, loss_target, m_ln_gain, m_ln_bias, m_pool_w, m_pool_b, m_pool_scale, m_gdn_w_in, m_gdn_conv, m_gdn_a_log, m_gdn_dt_bias, m_gdn_norm_w, m_gdn_w_out, m_mlp_w1, m_mlp_w2, m_ple_gate_w, m_ple_gate_b, m_ple_proj, v_ln_gain, v_ln_bias, v_pool_w, v_pool_b, v_pool_scale, v_gdn_w_in, v_gdn_conv, v_gdn_a_log, v_gdn_dt_bias, v_gdn_norm_w, v_gdn_w_out, v_mlp_w1, v_mlp_w2, v_ple_gate_w, v_ple_gate_b, v_ple_proj):
    w_sh = dict(ln_gain=ln_gain, ln_bias=ln_bias, pool_w=pool_w, pool_b=pool_b, pool_scale=pool_scale, gdn_w_in=gdn_w_in,
                gdn_conv=gdn_conv, gdn_a_log=gdn_a_log, gdn_dt_bias=gdn_dt_bias, gdn_norm_w=gdn_norm_w, gdn_w_out=gdn_w_out,
                mlp_w1=mlp_w1, mlp_w2=mlp_w2, ple_gate_w=ple_gate_w, ple_gate_b=ple_gate_b, ple_proj=ple_proj)
    m_sh = dict(ln_gain=m_ln_gain, ln_bias=m_ln_bias, pool_w=m_pool_w, pool_b=m_pool_b, pool_scale=m_pool_scale, gdn_w_in=m_gdn_w_in,
                gdn_conv=m_gdn_conv, gdn_a_log=m_gdn_a_log, gdn_dt_bias=m_gdn_dt_bias, gdn_norm_w=m_gdn_norm_w, gdn_w_out=m_gdn_w_out,
                mlp_w1=m_mlp_w1, mlp_w2=m_mlp_w2, ple_gate_w=m_ple_gate_w, ple_gate_b=m_ple_gate_b, ple_proj=m_ple_proj)
    v_sh = dict(ln_gain=v_ln_gain, ln_bias=v_ln_bias, pool_w=v_pool_w, pool_b=v_pool_b, pool_scale=v_pool_scale, gdn_w_in=v_gdn_w_in,
                gdn_conv=v_gdn_conv, gdn_a_log=v_gdn_a_log, gdn_dt_bias=v_gdn_dt_bias, gdn_norm_w=v_gdn_norm_w, gdn_w_out=v_gdn_w_out,
                mlp_w1=v_mlp_w1, mlp_w2=v_mlp_w2, ple_gate_w=v_ple_gate_w, ple_gate_b=v_ple_gate_b, ple_proj=v_ple_proj)
    xs, tg = x[0], loss_target[0]
    ps = p[:, 0]
    S, D = xs.shape
    H = D // HEAD_DIM
    me = 4 * lax.axis_index("x") + 2 * lax.axis_index("y") + lax.axis_index("c")
    layer = lambda n, l: (w_sh[n][0] if n in ("gdn_w_in", "gdn_w_out") else w_sh[n][l])

    pool_packed = _pack_rows([w_sh["pool_w"][0]], D, BF16, PACK_PART_ALIGN)
    small_packed = _pack_small([w_sh[n] for n in SMALL_SHARDED])
    pool_gathered, small_gathered = _all_gather("gather_first", [pool_packed, small_packed])
    W = {"pool_w": _join_blocks("pool_w", _unpack_rows(pool_gathered, [w_sh["pool_w"][0].shape], D)[0])}

    started = {}

    def start(g, after):
        srcs = [layer(n, l).astype(BF16) for n, l in GATHER_GROUPS[g]]
        started[g] = tuple(_send_start(f"gather_{g}_start", srcs, [lax.empty((N_DEV,) + a.shape, BF16) for a in srcs], True, after))
        return started[g][-1]

    first_token = start("l0a", small_gathered)
    smalls = _unpack_small(small_gathered, [w_sh[n].shape for n in SMALL_SHARDED])
    for n, a in zip(SMALL_SHARDED, smalls):
        W[n] = jnp.moveaxis(a, 0, -2).reshape(a.shape[1:-1] + (N_DEV * a.shape[-1],))
    W["ln_gain"] = W["ln_gain"].reshape(2 * DEPTH, D)
    W["ln_bias"] = W["ln_bias"].reshape(2 * DEPTH, D)
    W["pool_b"] = W["pool_b"].reshape(1, D) + first_token[0:1, 0:1]
    W["gdn_conv"] = W["gdn_conv"][0]
    W["pool_scale"] = pool_scale
    W["ple_gate_b"] = ple_gate_b
    W["gdn_norm_w"] = gdn_norm_w
    W["gdn_a_log"] = jnp.pad(gdn_a_log, ((0, 0), (0, LANES - H)))
    W["gdn_dt_bias"] = jnp.pad(gdn_dt_bias, ((0, 0), (0, LANES - H)))

    def fetch(g, after):
        members = GATHER_GROUPS[g]
        srcs, lands = _send_wait(f"gather_{g}_wait", started[g], after, True)
        tokens = [start(nxt, lands[0]) for nxt, prev in GATHER_AFTER.items() if prev == g]
        out = {n: _join_blocks(n, lax.dynamic_update_index_in_dim(land, src, me, 0)) for (n, _), src, land in zip(members, srcs, lands)}
        if "gdn_w_in" in out:
            out["gdn_wqkv"], out["gdn_wz"], out["gdn_wba"] = _split_w_in(out.pop("gdn_w_in"), D, H)
        out["_after"] = tokens
        return out

    sent = {}

    def emit(g, grads):
        srcs = [grads[n] for n, _ in GRAD_GROUPS[g]]
        lands = [lax.empty((N_PEERS,) + a.shape[1:], BF16) for a in srcs]
        sent[g] = tuple(_send_start(f"grads_{g}_start", srcs, lands, False, srcs[0]))
        return sent[g][-1]

    loss_cols, grad_x, G = _local_step(xs, ps, tg, W, fetch, emit)
    loss = lax.psum(0.5 * jnp.sum(loss_cols) / D, MESH_AXES)

    pool_src = _pack_blocks([_split_blocks("pool_w", G["pool_w"])], D, BF16, PACK_PART_ALIGN)
    pool_sum = _sum_blocks("sum_pool_grads", [_exchange("exchange_pool_grads", pool_src)], SUM_TILE)
    grads = {"pool_w": _unpack_rows(pool_sum, [w_sh["pool_w"][0].shape], D)[0].reshape(w_sh["pool_w"].shape)}
    small_names = SMALL_SHARDED + SMALL_REPLICATED
    gs_packed = _pack_small([G[n] for n in small_names])
    (gs_all,) = _all_gather("gather_small_grads", [gs_packed])
    gs_sum = _sum_blocks("sum_small_grads", [gs_all], SUM_TILE)
    for n, a in zip(small_names, _unpack_small(gs_sum, [G[n].shape for n in small_names])):
        if n in SMALL_SHARDED:
            width = w_sh[n].shape[-1]
            a = a.reshape(w_sh[n].shape[:-1] + (N_DEV * width,))
            a = lax.dynamic_slice_in_dim(a, me * width, width, axis=a.ndim - 1)
        grads[n] = a.reshape(w_sh[n].shape)

    per_layer = {}
    for g, members in GRAD_GROUPS.items():
        srcs, lands = _send_wait(f"grads_{g}_wait", sent[g], grad_x, False)
        for (n, l), src, land in zip(members, srcs, lands):
            own = lax.dynamic_index_in_dim(src, me, 0, keepdims=True)
            as3d = lambda a: a.reshape(a.shape[0], -1, a.shape[-1])
            per_layer[(n, l)] = _sum_blocks(f"sum_grads_{n}_{l}", [as3d(land), as3d(own)], SUM_TILE).reshape(layer(n, l).shape)
    for n in ("gdn_w_in", "gdn_w_out"):
        grads[n] = per_layer[(n, 0)][None]
    for n in ("mlp_w1", "mlp_w2", "ple_gate_w", "ple_proj"):
        grads[n] = jnp.stack([per_layer[(n, 0)], per_layer[(n, 1)]])

    deltas, new_m, new_v = {}, {}, {}
    for n in WEIGHTS:
        if n not in small_names:
            deltas[n], new_m[n], new_v[n] = _adamw(f"adamw_{n}", w_sh[n], grads[n], m_sh[n], v_sh[n])
    small_out = _adamw("adamw_small", *[_pack_small([d[n] for n in small_names]) for d in (w_sh, grads, m_sh, v_sh)])
    for res, into in zip(small_out, (deltas, new_m, new_v)):
        for n, a in zip(small_names, _unpack_small(res, [w_sh[n].shape for n in small_names])):
            into[n] = a
    return (loss, grad_x[None], *[grads[n] for n in WEIGHTS], *[deltas[n] for n in WEIGHTS],
            *[new_m[n] for n in WEIGHTS], *[new_v[n] for n in WEIGHTS])
```
